```python
import jax, jax.numpy as jnp
from jax import lax
import numpy as np

D_MODEL = 2048
BATCH = 8
SEQ = 2048
DEPTH = 4

HEAD_DIM = 64
N_FOX_HEADS = D_MODEL // 2 // HEAD_DIM
N_SWA_HEADS = D_MODEL // 2 // HEAD_DIM
N_SWA_KV_HEADS = max(1, N_SWA_HEADS // 8)
FOX_WIDTH = N_FOX_HEADS * HEAD_DIM
SWA_WIDTH = N_SWA_HEADS * HEAD_DIM
SWA_KV_WIDTH = N_SWA_KV_HEADS * HEAD_DIM
MIX_WIDTH = FOX_WIDTH + SWA_WIDTH
IN_SPLIT_SIZES = (FOX_WIDTH, FOX_WIDTH, FOX_WIDTH, N_FOX_HEADS, SWA_WIDTH, SWA_KV_WIDTH, SWA_KV_WIDTH)
IN_PROJ_WIDTH = sum(IN_SPLIT_SIZES)
D_FF = ((8 * D_MODEL // 3 + 255) // 256) * 256
N_META = 16
BLOCK = 128
WINDOW = 128
PAD = BLOCK - N_META
EPS = 1e-6
NEG_INF = -1e30

kernel_name = "hymba_fox_swa_sink_alibi_macaron"


def rms_norm(x, g):
    xf = x.astype(jnp.float32)
    y = xf * lax.rsqrt(jnp.mean(xf * xf, axis=-1, keepdims=True) + EPS)
    return (y * g.astype(jnp.float32)).astype(x.dtype)


def swiglu(x, w_gate, w_up, w_down):
    return (jax.nn.silu(x @ w_gate) * (x @ w_up)) @ w_down


def alibi_slopes(n_heads):
    return jnp.asarray(2.0 ** (-8.0 * np.arange(1, n_heads + 1) / n_heads), dtype=jnp.float32)


def forgetting_attention(q, k, v, log_f):
    L = q.shape[1]
    scale = HEAD_DIM ** -0.5
    c = jnp.cumsum(log_f, axis=1).transpose(0, 2, 1)
    pos = jnp.arange(L)
    outs = []
    for i in range(L // BLOCK):
        q0, q1 = i * BLOCK, (i + 1) * BLOCK
        s = jnp.einsum('bqhd,bkhd->bhqk', q[:, q0:q1], k[:, :q1]).astype(jnp.float32) * scale
        s = s + c[:, :, q0:q1, None] - c[:, :, None, :q1]
        qp = pos[q0:q1][:, None]
        kp = pos[:q1][None, :]
        allowed = (kp <= qp) & (kp >= PAD)
        p = jax.nn.softmax(jnp.where(allowed, s, NEG_INF), axis=-1)
        outs.append(jnp.einsum('bhqk,bkhd->bqhd', p.astype(v.dtype), v[:, :q1]))
    return jnp.concatenate(outs, axis=1)


def sliding_window_sink_attention(q, k, v, sinks):
    B, L, Hq, Dh = q.shape
    Hkv = k.shape[2]
    G = Hq // Hkv
    NB = L // BLOCK
    scale = HEAD_DIM ** -0.5
    qb = q.reshape(B, NB, BLOCK, Hkv, G, Dh)
    kb = k.reshape(B, NB, BLOCK, Hkv, Dh)
    vb = v.reshape(B, NB, BLOCK, Hkv, Dh)
    shift = ((0, 0), (1, 0), (0, 0), (0, 0), (0, 0))
    k_band = jnp.concatenate([jnp.pad(kb, shift)[:, :-1], kb], axis=2)
    v_band = jnp.concatenate([jnp.pad(vb, shift)[:, :-1], vb], axis=2)
    s = jnp.einsum('bnqhgd,bnkhd->bnhgqk', qb, k_band).astype(jnp.float32) * scale
    blk = jnp.arange(NB)[:, None] * BLOCK
    qpos = blk + jnp.arange(BLOCK)[None, :]
    kpos = blk - BLOCK + jnp.arange(2 * BLOCK)[None, :]
    dist = qpos[:, :, None] - kpos[:, None, :]
    allowed = (dist >= 0) & (dist < WINDOW) & (kpos[:, None, :] >= PAD)
    slopes = alibi_slopes(Hq).reshape(Hkv, G)[None, None, :, :, None, None]
    s = s - slopes * dist.astype(jnp.float32)[None, :, None, None]
    s = jnp.where(allowed[None, :, None, None], s, NEG_INF)
    sink = sinks.astype(jnp.float32).reshape(Hkv, G)[None, None, :, :, None, None]
    m = jnp.maximum(jnp.max(s, axis=-1, keepdims=True), sink)
    p = jnp.exp(s - m)
    p = p / (jnp.sum(p, axis=-1, keepdims=True) + jnp.exp(sink - m))
    o = jnp.einsum('bnhgqk,bnkhd->bnqhgd', p.astype(v.dtype), v_band)
    return o.reshape(B, L, Hq, Dh)


def _fwd_setup_inputs(seed: int = 0) -> dict:
    key = jax.random.key(seed)
    ks = jax.random.split(key, 24)
    f32 = jnp.float32

    def nrm(k, shape, scale):
        return jax.random.normal(k, shape, f32) * scale

    def gain(k, shape):
        return 1.0 + 0.02 * jax.random.normal(k, shape, f32)

    return {
        "x": nrm(ks[0], (BATCH, SEQ, D_MODEL), 1.0),
        "meta_tokens": nrm(ks[1], (N_META, D_MODEL), 1.0),
        "ffn1_norm": gain(ks[2], (DEPTH, D_MODEL)),
        "ffn1_w_gate": nrm(ks[3], (DEPTH, D_MODEL, D_FF), D_MODEL ** -0.5),
        "ffn1_w_up": nrm(ks[4], (DEPTH, D_MODEL, D_FF), D_MODEL ** -0.5),
        "ffn1_w_down": nrm(ks[5], (DEPTH, D_FF, D_MODEL), D_FF ** -0.5),
        "mix_norm": gain(ks[6], (DEPTH, D_MODEL)),
        "w_in": nrm(ks[7], (DEPTH, D_MODEL, IN_PROJ_WIDTH), D_MODEL ** -0.5),
        "b_forget": 2.0 + 0.3 * jax.random.normal(ks[8], (DEPTH, N_FOX_HEADS), f32),
        "fox_q_norm": gain(ks[9], (DEPTH, HEAD_DIM)),
        "fox_k_norm": gain(ks[10], (DEPTH, HEAD_DIM)),
        "swa_q_norm": gain(ks[11], (DEPTH, HEAD_DIM)),
        "swa_k_norm": gain(ks[12], (DEPTH, HEAD_DIM)),
        "swa_sinks": nrm(ks[13], (DEPTH, N_SWA_HEADS), 1.0),
        "fox_out_norm": gain(ks[14], (DEPTH, FOX_WIDTH)),
        "swa_out_norm": gain(ks[15], (DEPTH, SWA_WIDTH)),
        "w_out": nrm(ks[16], (DEPTH, MIX_WIDTH, D_MODEL), MIX_WIDTH ** -0.5),
        "ffn2_norm": gain(ks[17], (DEPTH, D_MODEL)),
        "ffn2_w_gate": nrm(ks[18], (DEPTH, D_MODEL, D_FF), D_MODEL ** -0.5),
        "ffn2_w_up": nrm(ks[19], (DEPTH, D_MODEL, D_FF), D_MODEL ** -0.5),
        "ffn2_w_down": nrm(ks[20], (DEPTH, D_FF, D_MODEL), D_FF ** -0.5),
    }


def _fwd_reference(x, meta_tokens, ffn1_norm, ffn1_w_gate, ffn1_w_up, ffn1_w_down, mix_norm, w_in,
              b_forget, fox_q_norm, fox_k_norm, swa_q_norm, swa_k_norm, swa_sinks,
              fox_out_norm, swa_out_norm, w_out, ffn2_norm, ffn2_w_gate, ffn2_w_up, ffn2_w_down):
    B, S, D = x.shape
    h = jnp.concatenate([
        jnp.zeros((B, PAD, D), x.dtype),
        jnp.broadcast_to(meta_tokens.astype(x.dtype)[None], (B, N_META, D)),
        x,
    ], axis=1)
    L = h.shape[1]
    split_idx = [int(v) for v in np.cumsum(IN_SPLIT_SIZES)[:-1]]

    for l in range(DEPTH):
        h = h + 0.5 * swiglu(rms_norm(h, ffn1_norm[l]), ffn1_w_gate[l], ffn1_w_up[l], ffn1_w_down[l])

        u = rms_norm(h, mix_norm[l]) @ w_in[l]
        fq, fk, fv, fz, sq, sk, sv = jnp.split(u, split_idx, axis=-1)
        fq = rms_norm(fq.reshape(B, L, N_FOX_HEADS, HEAD_DIM), fox_q_norm[l])
        fk = rms_norm(fk.reshape(B, L, N_FOX_HEADS, HEAD_DIM), fox_k_norm[l])
        fv = fv.reshape(B, L, N_FOX_HEADS, HEAD_DIM)
        log_f = jax.nn.log_sigmoid(fz.astype(jnp.float32) + b_forget[l].astype(jnp.float32))
        sq = rms_norm(sq.reshape(B, L, N_SWA_HEADS, HEAD_DIM), swa_q_norm[l])
        sk = rms_norm(sk.reshape(B, L, N_SWA_KV_HEADS, HEAD_DIM), swa_k_norm[l])
        sv = sv.reshape(B, L, N_SWA_KV_HEADS, HEAD_DIM)

        o_fox = forgetting_attention(fq, fk, fv, log_f).reshape(B, L, FOX_WIDTH)
        o_swa = sliding_window_sink_attention(sq, sk, sv, swa_sinks[l]).reshape(B, L, SWA_WIDTH)
        o = jnp.concatenate([rms_norm(o_fox, fox_out_norm[l]), rms_norm(o_swa, swa_out_norm[l])], axis=-1)
        h = h + o @ w_out[l]

        h = h + 0.5 * swiglu(rms_norm(h, ffn2_norm[l]), ffn2_w_gate[l], ffn2_w_up[l], ffn2_w_down[l])

    return h[:, BLOCK:]


import jax as _jax
import jax.numpy as _jnp

TWIN_FORMAT = 'train_step'
FWD_PARAMS = ['x', 'meta_tokens', 'ffn1_norm', 'ffn1_w_gate', 'ffn1_w_up', 'ffn1_w_down', 'mix_norm', 'w_in', 'b_forget', 'fox_q_norm', 'fox_k_norm', 'swa_q_norm', 'swa_k_norm', 'swa_sinks', 'fox_out_norm', 'swa_out_norm', 'w_out', 'ffn2_norm', 'ffn2_w_gate', 'ffn2_w_up', 'ffn2_w_down']
TWIN_WEIGHTS = ['meta_tokens', 'ffn1_norm', 'ffn1_w_gate', 'ffn1_w_up', 'ffn1_w_down', 'mix_norm', 'w_in', 'b_forget', 'fox_q_norm', 'fox_k_norm', 'swa_q_norm', 'swa_k_norm', 'swa_sinks', 'fox_out_norm', 'swa_out_norm', 'w_out', 'ffn2_norm', 'ffn2_w_gate', 'ffn2_w_up', 'ffn2_w_down']
TWIN_DIFF_INPUT = 'x'
TWIN_INPUTS = ['x', 'meta_tokens', 'ffn1_norm', 'ffn1_w_gate', 'ffn1_w_up', 'ffn1_w_down', 'mix_norm', 'w_in', 'b_forget', 'fox_q_norm', 'fox_k_norm', 'swa_q_norm', 'swa_k_norm', 'swa_sinks', 'fox_out_norm', 'swa_out_norm', 'w_out', 'ffn2_norm', 'ffn2_w_gate', 'ffn2_w_up', 'ffn2_w_down', 'loss_target', 'm_meta_tokens', 'm_ffn1_norm', 'm_ffn1_w_gate', 'm_ffn1_w_up', 'm_ffn1_w_down', 'm_mix_norm', 'm_w_in', 'm_b_forget', 'm_fox_q_norm', 'm_fox_k_norm', 'm_swa_q_norm', 'm_swa_k_norm', 'm_swa_sinks', 'm_fox_out_norm', 'm_swa_out_norm', 'm_w_out', 'm_ffn2_norm', 'm_ffn2_w_gate', 'm_ffn2_w_up', 'm_ffn2_w_down', 'v_meta_tokens', 'v_ffn1_norm', 'v_ffn1_w_gate', 'v_ffn1_w_up', 'v_ffn1_w_down', 'v_mix_norm', 'v_w_in', 'v_b_forget', 'v_fox_q_norm', 'v_fox_k_norm', 'v_swa_q_norm', 'v_swa_k_norm', 'v_swa_sinks', 'v_fox_out_norm', 'v_swa_out_norm', 'v_w_out', 'v_ffn2_norm', 'v_ffn2_w_gate', 'v_ffn2_w_up', 'v_ffn2_w_down']
TWIN_OUTPUTS = ['loss', 'grad_x', 'grad_meta_tokens', 'grad_ffn1_norm', 'grad_ffn1_w_gate', 'grad_ffn1_w_up', 'grad_ffn1_w_down', 'grad_mix_norm', 'grad_w_in', 'grad_b_forget', 'grad_fox_q_norm', 'grad_fox_k_norm', 'grad_swa_q_norm', 'grad_swa_k_norm', 'grad_swa_sinks', 'grad_fox_out_norm', 'grad_swa_out_norm', 'grad_w_out', 'grad_ffn2_norm', 'grad_ffn2_w_gate', 'grad_ffn2_w_up', 'grad_ffn2_w_down', 'delta_meta_tokens', 'delta_ffn1_norm', 'delta_ffn1_w_gate', 'delta_ffn1_w_up', 'delta_ffn1_w_down', 'delta_mix_norm', 'delta_w_in', 'delta_b_forget', 'delta_fox_q_norm', 'delta_fox_k_norm', 'delta_swa_q_norm', 'delta_swa_k_norm', 'delta_swa_sinks', 'delta_fox_out_norm', 'delta_swa_out_norm', 'delta_w_out', 'delta_ffn2_norm', 'delta_ffn2_w_gate', 'delta_ffn2_w_up', 'delta_ffn2_w_down', 'new_m_meta_tokens', 'new_m_ffn1_norm', 'new_m_ffn1_w_gate', 'new_m_ffn1_w_up', 'new_m_ffn1_w_down', 'new_m_mix_norm', 'new_m_w_in', 'new_m_b_forget', 'new_m_fox_q_norm', 'new_m_fox_k_norm', 'new_m_swa_q_norm', 'new_m_swa_k_norm', 'new_m_swa_sinks', 'new_m_fox_out_norm', 'new_m_swa_out_norm', 'new_m_w_out', 'new_m_ffn2_norm', 'new_m_ffn2_w_gate', 'new_m_ffn2_w_up', 'new_m_ffn2_w_down', 'new_v_meta_tokens', 'new_v_ffn1_norm', 'new_v_ffn1_w_gate', 'new_v_ffn1_w_up', 'new_v_ffn1_w_down', 'new_v_mix_norm', 'new_v_w_in', 'new_v_b_forget', 'new_v_fox_q_norm', 'new_v_fox_k_norm', 'new_v_swa_q_norm', 'new_v_swa_k_norm', 'new_v_swa_sinks', 'new_v_fox_out_norm', 'new_v_swa_out_norm', 'new_v_w_out', 'new_v_ffn2_norm', 'new_v_ffn2_w_gate', 'new_v_ffn2_w_up', 'new_v_ffn2_w_down']
TWIN_LEAF_KINDS = {'loss': 'loss', 'grad_x': 'grad_x', 'grad_meta_tokens': 'grad_w', 'grad_ffn1_norm': 'grad_w', 'grad_ffn1_w_gate': 'grad_w', 'grad_ffn1_w_up': 'grad_w', 'grad_ffn1_w_down': 'grad_w', 'grad_mix_norm': 'grad_w', 'grad_w_in': 'grad_w', 'grad_b_forget': 'grad_w', 'grad_fox_q_norm': 'grad_w', 'grad_fox_k_norm': 'grad_w', 'grad_swa_q_norm': 'grad_w', 'grad_swa_k_norm': 'grad_w', 'grad_swa_sinks': 'grad_w', 'grad_fox_out_norm': 'grad_w', 'grad_swa_out_norm': 'grad_w', 'grad_w_out': 'grad_w', 'grad_ffn2_norm': 'grad_w', 'grad_ffn2_w_gate': 'grad_w', 'grad_ffn2_w_up': 'grad_w', 'grad_ffn2_w_down': 'grad_w', 'delta_meta_tokens': 'delta_w', 'delta_ffn1_norm': 'delta_w', 'delta_ffn1_w_gate': 'delta_w', 'delta_ffn1_w_up': 'delta_w', 'delta_ffn1_w_down': 'delta_w', 'delta_mix_norm': 'delta_w', 'delta_w_in': 'delta_w', 'delta_b_forget': 'delta_w', 'delta_fox_q_norm': 'delta_w', 'delta_fox_k_norm': 'delta_w', 'delta_swa_q_norm': 'delta_w', 'delta_swa_k_norm': 'delta_w', 'delta_swa_sinks': 'delta_w', 'delta_fox_out_norm': 'delta_w', 'delta_swa_out_norm': 'delta_w', 'delta_w_out': 'delta_w', 'delta_ffn2_norm': 'delta_w', 'delta_ffn2_w_gate': 'delta_w', 'delta_ffn2_w_up': 'delta_w', 'delta_ffn2_w_down': 'delta_w', 'new_m_meta_tokens': 'new_m', 'new_m_ffn1_norm': 'new_m', 'new_m_ffn1_w_gate': 'new_m', 'new_m_ffn1_w_up': 'new_m', 'new_m_ffn1_w_down': 'new_m', 'new_m_mix_norm': 'new_m', 'new_m_w_in': 'new_m', 'new_m_b_forget': 'new_m', 'new_m_fox_q_norm': 'new_m', 'new_m_fox_k_norm': 'new_m', 'new_m_swa_q_norm': 'new_m', 'new_m_swa_k_norm': 'new_m', 'new_m_swa_sinks': 'new_m', 'new_m_fox_out_norm': 'new_m', 'new_m_swa_out_norm': 'new_m', 'new_m_w_out': 'new_m', 'new_m_ffn2_norm': 'new_m', 'new_m_ffn2_w_gate': 'new_m', 'new_m_ffn2_w_up': 'new_m', 'new_m_ffn2_w_down': 'new_m', 'new_v_meta_tokens': 'new_v', 'new_v_ffn1_norm': 'new_v', 'new_v_ffn1_w_gate': 'new_v', 'new_v_ffn1_w_up': 'new_v', 'new_v_ffn1_w_down': 'new_v', 'new_v_mix_norm': 'new_v', 'new_v_w_in': 'new_v', 'new_v_b_forget': 'new_v', 'new_v_fox_q_norm': 'new_v', 'new_v_fox_k_norm': 'new_v', 'new_v_swa_q_norm': 'new_v', 'new_v_swa_k_norm': 'new_v', 'new_v_swa_sinks': 'new_v', 'new_v_fox_out_norm': 'new_v', 'new_v_swa_out_norm': 'new_v', 'new_v_w_out': 'new_v', 'new_v_ffn2_norm': 'new_v', 'new_v_ffn2_w_gate': 'new_v', 'new_v_ffn2_w_up': 'new_v', 'new_v_ffn2_w_down': 'new_v'}


def _forward(args):
    return _fwd_reference(*[args[k] for k in FWD_PARAMS])


def _output_shape():
    out = _jax.eval_shape(lambda: _forward(_fwd_setup_inputs(0)))
    return out.shape, out.dtype

N_MICROBATCH = 1
ADAM_LR = 0.001
ADAM_B1 = 0.9
ADAM_B2 = 0.999
ADAM_EPS = 1e-08
ADAM_WD = 0.01
ADAM_STEP = 10
PER_EXAMPLE_BATCH_AXIS = {'x': 0, 'loss_target': 0}
SHARED_INPUTS = []
_WEIGHT_DTYPES = {'meta_tokens': _jnp.float32, 'ffn1_norm': _jnp.float32, 'ffn1_w_gate': _jnp.float32, 'ffn1_w_up': _jnp.float32, 'ffn1_w_down': _jnp.float32, 'mix_norm': _jnp.float32, 'w_in': _jnp.float32, 'b_forget': _jnp.float32, 'fox_q_norm': _jnp.float32, 'fox_k_norm': _jnp.float32, 'swa_q_norm': _jnp.float32, 'swa_k_norm': _jnp.float32, 'swa_sinks': _jnp.float32, 'fox_out_norm': _jnp.float32, 'swa_out_norm': _jnp.float32, 'w_out': _jnp.float32, 'ffn2_norm': _jnp.float32, 'ffn2_w_gate': _jnp.float32, 'ffn2_w_up': _jnp.float32, 'ffn2_w_down': _jnp.float32}
MOMENT_SCALE = {'meta_tokens': 1.439741e-01, 'ffn1_norm': 1.386769e+00, 'ffn1_w_gate': 1.216983e-01, 'ffn1_w_up': 1.206853e-01, 'ffn1_w_down': 1.996366e-01, 'mix_norm': 8.241384e-01, 'w_in': 5.682822e-01, 'b_forget': 2.131487e+00, 'fox_q_norm': 8.985922e-01, 'fox_k_norm': 8.969168e-01, 'swa_q_norm': 9.256431e-01, 'swa_k_norm': 9.207436e-01, 'swa_sinks': 1.058921e+00, 'fox_out_norm': 7.844163e+00, 'swa_out_norm': 9.448701e+00, 'w_out': 7.981200e-01, 'ffn2_norm': 1.472746e+00, 'ffn2_w_gate': 6.440365e-02, 'ffn2_w_up': 6.904773e-02, 'ffn2_w_down': 1.138927e-01}


def _to_microbatches(a, axis):
    t = _jnp.moveaxis(a, axis, 0)
    t = t.reshape((N_MICROBATCH, t.shape[0] // N_MICROBATCH) + t.shape[1:])
    return _jnp.moveaxis(t, 1, axis + 1)


def setup_inputs(seed: int = 0) -> dict:
    inp = _fwd_setup_inputs(seed)
    key = _jax.random.fold_in(_jax.random.key(seed), 7919)
    shape, _ = _output_shape()
    out = dict(inp)
    out["loss_target"] = _jax.random.normal(_jax.random.fold_in(key, 0), shape, _jnp.float32)
    for i, name in enumerate(TWIN_WEIGHTS):
        w = inp[name].astype(_jnp.float32)
        if MOMENT_SCALE is None:
            s = _jnp.sqrt(_jnp.mean(_jnp.square(w)) + 1e-30)
        else:
            s = MOMENT_SCALE[name]
        km, kv = _jax.random.split(_jax.random.fold_in(key, i + 1))
        out[name] = w
        out["m_" + name] = s * _jax.random.normal(km, w.shape, _jnp.float32)
        out["v_" + name] = (s * s) * _jax.random.uniform(kv, w.shape, _jnp.float32, 0.5, 1.5)
    if N_MICROBATCH > 1:
        for name, axis in PER_EXAMPLE_BATCH_AXIS.items():
            out[name] = _to_microbatches(out[name], axis)
    return {'x': out['x'], 'meta_tokens': out['meta_tokens'], 'ffn1_norm': out['ffn1_norm'], 'ffn1_w_gate': out['ffn1_w_gate'], 'ffn1_w_up': out['ffn1_w_up'], 'ffn1_w_down': out['ffn1_w_down'], 'mix_norm': out['mix_norm'], 'w_in': out['w_in'], 'b_forget': out['b_forget'], 'fox_q_norm': out['fox_q_norm'], 'fox_k_norm': out['fox_k_norm'], 'swa_q_norm': out['swa_q_norm'], 'swa_k_norm': out['swa_k_norm'], 'swa_sinks': out['swa_sinks'], 'fox_out_norm': out['fox_out_norm'], 'swa_out_norm': out['swa_out_norm'], 'w_out': out['w_out'], 'ffn2_norm': out['ffn2_norm'], 'ffn2_w_gate': out['ffn2_w_gate'], 'ffn2_w_up': out['ffn2_w_up'], 'ffn2_w_down': out['ffn2_w_down'], 'loss_target': out['loss_target'], 'm_meta_tokens': out['m_meta_tokens'], 'm_ffn1_norm': out['m_ffn1_norm'], 'm_ffn1_w_gate': out['m_ffn1_w_gate'], 'm_ffn1_w_up': out['m_ffn1_w_up'], 'm_ffn1_w_down': out['m_ffn1_w_down'], 'm_mix_norm': out['m_mix_norm'], 'm_w_in': out['m_w_in'], 'm_b_forget': out['m_b_forget'], 'm_fox_q_norm': out['m_fox_q_norm'], 'm_fox_k_norm': out['m_fox_k_norm'], 'm_swa_q_norm': out['m_swa_q_norm'], 'm_swa_k_norm': out['m_swa_k_norm'], 'm_swa_sinks': out['m_swa_sinks'], 'm_fox_out_norm': out['m_fox_out_norm'], 'm_swa_out_norm': out['m_swa_out_norm'], 'm_w_out': out['m_w_out'], 'm_ffn2_norm': out['m_ffn2_norm'], 'm_ffn2_w_gate': out['m_ffn2_w_gate'], 'm_ffn2_w_up': out['m_ffn2_w_up'], 'm_ffn2_w_down': out['m_ffn2_w_down'], 'v_meta_tokens': out['v_meta_tokens'], 'v_ffn1_norm': out['v_ffn1_norm'], 'v_ffn1_w_gate': out['v_ffn1_w_gate'], 'v_ffn1_w_up': out['v_ffn1_w_up'], 'v_ffn1_w_down': out['v_ffn1_w_down'], 'v_mix_norm': out['v_mix_norm'], 'v_w_in': out['v_w_in'], 'v_b_forget': out['v_b_forget'], 'v_fox_q_norm': out['v_fox_q_norm'], 'v_fox_k_norm': out['v_fox_k_norm'], 'v_swa_q_norm': out['v_swa_q_norm'], 'v_swa_k_norm': out['v_swa_k_norm'], 'v_swa_sinks': out['v_swa_sinks'], 'v_fox_out_norm': out['v_fox_out_norm'], 'v_swa_out_norm': out['v_swa_out_norm'], 'v_w_out': out['v_w_out'], 'v_ffn2_norm': out['v_ffn2_norm'], 'v_ffn2_w_gate': out['v_ffn2_w_gate'], 'v_ffn2_w_up': out['v_ffn2_w_up'], 'v_ffn2_w_down': out['v_ffn2_w_down']}


def _loss(weights, diff, rest, loss_target):
    with _jax.named_scope("forward"):
        args = {**rest, TWIN_DIFF_INPUT: diff, **{k: w.astype(_WEIGHT_DTYPES[k]) for k, w in weights.items()}}
        y = _forward(args)
    with _jax.named_scope("loss_head"):
        err = _jnp.square(y.astype(_jnp.float32) - loss_target)
        return 0.5 * _jnp.sum(_jnp.mean(err, axis=-1)) if err.ndim else 0.5 * err


def _adamw(w, g, m, v):
    m = ADAM_B1 * m + (1.0 - ADAM_B1) * g
    v = ADAM_B2 * v + (1.0 - ADAM_B2) * _jnp.square(g)
    m_hat = m / (1.0 - ADAM_B1 ** ADAM_STEP)
    v_hat = v / (1.0 - ADAM_B2 ** ADAM_STEP)
    delta = -ADAM_LR * (m_hat / (_jnp.sqrt(v_hat) + ADAM_EPS) + ADAM_WD * w)
    return delta, m, v


def reference(x, meta_tokens, ffn1_norm, ffn1_w_gate, ffn1_w_up, ffn1_w_down, mix_norm, w_in, b_forget, fox_q_norm, fox_k_norm, swa_q_norm, swa_k_norm, swa_sinks, fox_out_norm, swa_out_norm, w_out, ffn2_norm, ffn2_w_gate, ffn2_w_up, ffn2_w_down, loss_target, m_meta_tokens, m_ffn1_norm, m_ffn1_w_gate, m_ffn1_w_up, m_ffn1_w_down, m_mix_norm, m_w_in, m_b_forget, m_fox_q_norm, m_fox_k_norm, m_swa_q_norm, m_swa_k_norm, m_swa_sinks, m_fox_out_norm, m_swa_out_norm, m_w_out, m_ffn2_norm, m_ffn2_w_gate, m_ffn2_w_up, m_ffn2_w_down, v_meta_tokens, v_ffn1_norm, v_ffn1_w_gate, v_ffn1_w_up, v_ffn1_w_down, v_mix_norm, v_w_in, v_b_forget, v_fox_q_norm, v_fox_k_norm, v_swa_q_norm, v_swa_k_norm, v_swa_sinks, v_fox_out_norm, v_swa_out_norm, v_w_out, v_ffn2_norm, v_ffn2_w_gate, v_ffn2_w_up, v_ffn2_w_down):
    given = dict(x=x, meta_tokens=meta_tokens, ffn1_norm=ffn1_norm, ffn1_w_gate=ffn1_w_gate, ffn1_w_up=ffn1_w_up, ffn1_w_down=ffn1_w_down, mix_norm=mix_norm, w_in=w_in, b_forget=b_forget, fox_q_norm=fox_q_norm, fox_k_norm=fox_k_norm, swa_q_norm=swa_q_norm, swa_k_norm=swa_k_norm, swa_sinks=swa_sinks, fox_out_norm=fox_out_norm, swa_out_norm=swa_out_norm, w_out=w_out, ffn2_norm=ffn2_norm, ffn2_w_gate=ffn2_w_gate, ffn2_w_up=ffn2_w_up, ffn2_w_down=ffn2_w_down, loss_target=loss_target, m_meta_tokens=m_meta_tokens, m_ffn1_norm=m_ffn1_norm, m_ffn1_w_gate=m_ffn1_w_gate, m_ffn1_w_up=m_ffn1_w_up, m_ffn1_w_down=m_ffn1_w_down, m_mix_norm=m_mix_norm, m_w_in=m_w_in, m_b_forget=m_b_forget, m_fox_q_norm=m_fox_q_norm, m_fox_k_norm=m_fox_k_norm, m_swa_q_norm=m_swa_q_norm, m_swa_k_norm=m_swa_k_norm, m_swa_sinks=m_swa_sinks, m_fox_out_norm=m_fox_out_norm, m_swa_out_norm=m_swa_out_norm, m_w_out=m_w_out, m_ffn2_norm=m_ffn2_norm, m_ffn2_w_gate=m_ffn2_w_gate, m_ffn2_w_up=m_ffn2_w_up, m_ffn2_w_down=m_ffn2_w_down, v_meta_tokens=v_meta_tokens, v_ffn1_norm=v_ffn1_norm, v_ffn1_w_gate=v_ffn1_w_gate, v_ffn1_w_up=v_ffn1_w_up, v_ffn1_w_down=v_ffn1_w_down, v_mix_norm=v_mix_norm, v_w_in=v_w_in, v_b_forget=v_b_forget, v_fox_q_norm=v_fox_q_norm, v_fox_k_norm=v_fox_k_norm, v_swa_q_norm=v_swa_q_norm, v_swa_k_norm=v_swa_k_norm, v_swa_sinks=v_swa_sinks, v_fox_out_norm=v_fox_out_norm, v_swa_out_norm=v_swa_out_norm, v_w_out=v_w_out, v_ffn2_norm=v_ffn2_norm, v_ffn2_w_gate=v_ffn2_w_gate, v_ffn2_w_up=v_ffn2_w_up, v_ffn2_w_down=v_ffn2_w_down)
    weights = {n: given[n] for n in TWIN_WEIGHTS}
    shared = {n: given[n] for n in SHARED_INPUTS}
    per_example = {n: given[n] for n in ['x']}
    grad_fn = _jax.value_and_grad(_loss, argnums=(0, 1))

    def one_microbatch(ex, loss_target):
        ex = dict(ex)
        diff = ex.pop(TWIN_DIFF_INPUT)
        return grad_fn(weights, diff, {**shared, **ex}, loss_target)

    if N_MICROBATCH == 1:
        loss, (grad_w, grad_x) = one_microbatch(per_example, given["loss_target"])
    else:
        def body(carry, xs):
            loss_sum, grad_sum = carry
            l_k, (gw_k, gx_k) = one_microbatch(xs[0], xs[1])
            with _jax.named_scope("update"):
                return (loss_sum + l_k, _jax.tree.map(_jnp.add, grad_sum, gw_k)), gx_k

        init = (_jnp.zeros((), _jnp.float32), _jax.tree.map(_jnp.zeros_like, weights))
        (loss, grad_w), grad_x = _jax.lax.scan(body, init, (per_example, given["loss_target"]))
    with _jax.named_scope("update"):
        delta_w, new_m, new_v = {}, {}, {}
        for n in TWIN_WEIGHTS:
            delta_w[n], new_m[n], new_v[n] = _adamw(weights[n], grad_w[n], given["m_" + n], given["v_" + n])
    return (loss, grad_x, *[grad_w[n] for n in TWIN_WEIGHTS], *[delta_w[n] for n in TWIN_WEIGHTS],
            *[new_m[n] for n in TWIN_WEIGHTS], *[new_v[n] for n in TWIN_WEIGHTS])
```

```python
import functools

import numpy as np
import jax
import jax.numpy as jnp
from jax import lax
from jax.experimental import pallas as pl
from jax.experimental.pallas import tpu as pltpu

F32 = jnp.float32
BF16 = jnp.bfloat16
MESH = pl.DeviceIdType.MESH

HEAD_DIM = 64
BLOCK = 128
LANES = 128
N_DEV = 8
EPS = 1e-6
NEG_INF = -1e30
SCALE = HEAD_DIM ** -0.5

ADAM_LR = 0.001
ADAM_B1 = 0.9
ADAM_B2 = 0.999
ADAM_EPS = 1e-08
ADAM_WD = 0.01
ADAM_STEP = 10

VMEM_BYTES_V7X = 64 * 1024 * 1024
VMEM_LIMIT = VMEM_BYTES_V7X * 3 // 4

NT = (((1,), (1,)), ((), ()))
TN = (((0,), (0,)), ((), ()))
HI = lax.Precision.HIGHEST


def _cparams(sem=None, vmem=VMEM_LIMIT):
    return pltpu.CompilerParams(dimension_semantics=sem, vmem_limit_bytes=vmem)


def _tile(n, pref, mult):
    best = None
    for t in range(mult, min(n, pref) + 1, mult):
        if n % t == 0:
            best = t
    return best if best is not None else n


def _dot(a, b):
    return jnp.dot(a, b, preferred_element_type=F32)


def _dot_nt(a, b):
    return lax.dot_general(a, b, NT, preferred_element_type=F32)


def _dot_tn(a, b):
    return lax.dot_general(a, b, TN, preferred_element_type=F32)


def _lane(shape):
    return lax.broadcasted_iota(jnp.int32, shape, len(shape) - 1)


def _half_sum(x, lo):
    s0 = jnp.sum(jnp.where(lo, x, 0.0), axis=1, keepdims=True)
    s1 = jnp.sum(jnp.where(lo, 0.0, x), axis=1, keepdims=True)
    return jnp.where(lo, s0, s1)


def _sigmoid(x):
    return 1.0 / (1.0 + jnp.exp(-x))


def _matmul(name, a, b, *, trans_a=False, trans_b=False, out_dtype=F32, scale=None, residual=None,
            tm=512, tn=512, tk=512):
    if trans_a:
        K, M = a.shape
    else:
        M, K = a.shape
    if trans_b:
        N, Kb = b.shape
    else:
        Kb, N = b.shape
    assert K == Kb, (name, a.shape, b.shape)
    tm = _tile(M, tm, LANES if trans_a else 16)
    tn = _tile(N, tn, LANES)
    tk = _tile(K, tk, 16 if (trans_a and not trans_b) else LANES)
    nk = K // tk
    dims = (((0 if trans_a else 1,), (1 if trans_b else 0,)), ((), ()))

    def body(*refs):
        if residual is None:
            a_ref, b_ref, o_ref, acc_ref = refs
            r_ref = None
        else:
            a_ref, b_ref, r_ref, o_ref, acc_ref = refs
        k = pl.program_id(2)

        @pl.when(k == 0)
        def _():
            acc_ref[...] = jnp.zeros_like(acc_ref)

        acc_ref[...] += lax.dot_general(a_ref[...].astype(BF16), b_ref[...].astype(BF16), dims,
                                        preferred_element_type=F32)

        @pl.when(k == nk - 1)
        def _():
            r = acc_ref[...]
            if scale is not None:
                r = r * scale
            if r_ref is not None:
                r = r + r_ref[...].astype(F32)
            o_ref[...] = r.astype(o_ref.dtype)

    a_spec = pl.BlockSpec((tk, tm), lambda i, j, k: (k, i)) if trans_a else pl.BlockSpec((tm, tk), lambda i, j, k: (i, k))
    b_spec = pl.BlockSpec((tn, tk), lambda i, j, k: (j, k)) if trans_b else pl.BlockSpec((tk, tn), lambda i, j, k: (k, j))
    in_specs = [a_spec, b_spec]
    args = [a, b]
    if residual is not None:
        in_specs.append(pl.BlockSpec((tm, tn), lambda i, j, k: (i, j)))
        args.append(residual)
    return pl.pallas_call(
        body, name=name,
        grid=(M // tm, N // tn, nk),
        in_specs=in_specs,
        out_specs=pl.BlockSpec((tm, tn), lambda i, j, k: (i, j)),
        out_shape=jax.ShapeDtypeStruct((M, N), out_dtype),
        scratch_shapes=[pltpu.VMEM((tm, tn), F32)],
        compiler_params=_cparams(("parallel", "parallel", "arbitrary")),
    )(*args)


def _rmsnorm_fwd(name, h, g):
    L, D = h.shape
    tm = _tile(L, 256, 16)

    def body(h_ref, g_ref, o_ref):
        x = h_ref[...]
        r = lax.rsqrt(jnp.mean(x * x, axis=1, keepdims=True) + EPS)
        o_ref[...] = (x * r * g_ref[...]).astype(o_ref.dtype)

    return pl.pallas_call(
        body, name=name, grid=(L // tm,),
        in_specs=[pl.BlockSpec((tm, D), lambda i: (i, 0)), pl.BlockSpec((1, D), lambda i: (0, 0))],
        out_specs=pl.BlockSpec((tm, D), lambda i: (i, 0)),
        out_shape=jax.ShapeDtypeStruct((L, D), BF16),
        compiler_params=_cparams(("parallel",)),
    )(h, g)


def _rmsnorm_bwd(name, dy, h, g, dres):
    L, D = h.shape
    tm = _tile(L, 256, 16)

    def body(dy_ref, h_ref, g_ref, dres_ref, dh_ref, dg_ref):
        i = pl.program_id(0)
        x = h_ref[...]
        dyv = dy_ref[...].astype(F32)
        r = lax.rsqrt(jnp.mean(x * x, axis=1, keepdims=True) + EPS)
        w = dyv * g_ref[...]
        proj = jnp.sum(w * x, axis=1, keepdims=True) * (1.0 / D)
        dh_ref[...] = dres_ref[...] + r * w - x * (r * r * r * proj)

        @pl.when(i == 0)
        def _():
            dg_ref[...] = jnp.zeros_like(dg_ref)

        dg_ref[...] += jnp.sum(dyv * x * r, axis=0, keepdims=True)

    return pl.pallas_call(
        body, name=name, grid=(L // tm,),
        in_specs=[pl.BlockSpec((tm, D), lambda i: (i, 0)), pl.BlockSpec((tm, D), lambda i: (i, 0)),
                  pl.BlockSpec((1, D), lambda i: (0, 0)), pl.BlockSpec((tm, D), lambda i: (i, 0))],
        out_specs=[pl.BlockSpec((tm, D), lambda i: (i, 0)), pl.BlockSpec((1, D), lambda i: (0, 0))],
        out_shape=[jax.ShapeDtypeStruct((L, D), F32), jax.ShapeDtypeStruct((1, D), F32)],
        compiler_params=_cparams(("arbitrary",)),
    )(dy, h, g, dres)


def _ffn_up(name, xn, wgT, wuT):
    L, D = xn.shape
    F = wgT.shape[0]
    tm = _tile(L, 544, 16)
    tn = _tile(F, 512, LANES)

    def body(x_ref, wg_ref, wu_ref, g_ref, u_ref, a_ref):
        x = x_ref[...]
        g = _dot_nt(x, wg_ref[...])
        u = _dot_nt(x, wu_ref[...])
        g_ref[...] = g.astype(BF16)
        u_ref[...] = u.astype(BF16)
        a_ref[...] = (g * _sigmoid(g) * u).astype(BF16)

    o_spec = pl.BlockSpec((tm, tn), lambda i, j: (i, j))
    o_shape = jax.ShapeDtypeStruct((L, F), BF16)
    return pl.pallas_call(
        body, name=name, grid=(L // tm, F // tn),
        in_specs=[pl.BlockSpec((tm, D), lambda i, j: (i, 0)), pl.BlockSpec((tn, D), lambda i, j: (j, 0)),
                  pl.BlockSpec((tn, D), lambda i, j: (j, 0))],
        out_specs=[o_spec, o_spec, o_spec], out_shape=[o_shape, o_shape, o_shape],
        compiler_params=_cparams(("parallel", "parallel")),
    )(xn, wgT, wuT)


def _ffn_bwd_act(name, dh, wd, gate, up):
    L, D = dh.shape
    F = wd.shape[0]
    tm = _tile(L, 544, 16)
    tn = _tile(F, 512, LANES)

    def body(dh_ref, wd_ref, g_ref, u_ref, dg_ref, du_ref, a_ref):
        da = 0.5 * _dot_nt(dh_ref[...].astype(BF16), wd_ref[...])
        g = g_ref[...].astype(F32)
        u = u_ref[...].astype(F32)
        sg = _sigmoid(g)
        silu = g * sg
        dg_ref[...] = (da * u * (sg * (1.0 + g * (1.0 - sg)))).astype(BF16)
        du_ref[...] = (da * silu).astype(BF16)
        a_ref[...] = (silu * u).astype(BF16)

    o_spec = pl.BlockSpec((tm, tn), lambda i, j: (i, j))
    o_shape = jax.ShapeDtypeStruct((L, F), BF16)
    return pl.pallas_call(
        body, name=name, grid=(L // tm, F // tn),
        in_specs=[pl.BlockSpec((tm, D), lambda i, j: (i, 0)), pl.BlockSpec((tn, D), lambda i, j: (j, 0)),
                  o_spec, o_spec],
        out_specs=[o_spec, o_spec, o_spec], out_shape=[o_shape, o_shape, o_shape],
        compiler_params=_cparams(("parallel", "parallel")),
    )(dh, wd, gate, up)


class _Cfg:
    def __init__(self, S, D, hf, hs, hkv):
        self.S, self.D, self.L = S, D, S + BLOCK
        self.hf, self.hs, self.hkv = hf, hs, hkv
        self.wf, self.ws = hf * HEAD_DIM, hs * HEAD_DIM
        self.group = hs // hkv
        self.cs = 2 * LANES // hf
        self.n_meta = 16
        self.pad = BLOCK - self.n_meta
        self.o_fk = self.wf
        self.o_fv = 2 * self.wf
        self.o_sq = 3 * self.wf
        self.o_sk = self.o_sq + self.ws
        self.o_sv = self.o_sk + LANES
        self.o_fz = self.o_sv + LANES
        self.up = self.o_fz + LANES
        assert hkv == 2 and hf % 2 == 0 and self.group % 2 == 0 and self.cs % 8 == 0
        assert self.o_sq % self.ws == 0 and (2 * self.wf) % LANES == 0


def _head_norm(x, gain, lo, mult):
    r = lax.rsqrt(_half_sum(x * x, lo) * (1.0 / HEAD_DIM) + EPS)
    return x * r * (gain * mult)


def _head_norm_bwd(dy, x, gain, lo, mult):
    r = lax.rsqrt(_half_sum(x * x, lo) * (1.0 / HEAD_DIM) + EPS)
    w = dy * (gain * mult)
    proj = _half_sum(w * x, lo) * (1.0 / HEAD_DIM)
    dx = r * w - x * (r * r * r * proj)
    dgain = jnp.sum(dy * mult * x * r, axis=0, keepdims=True)
    return dx, dgain


def _dup(x, lo):
    xr = pltpu.roll(x, 64, 1)
    return jnp.where(lo, x, xr), jnp.where(lo, xr, x)


def _mix_prep(name, cfg, u, gq, gk, gsq, gsk, bias):
    L, wf, ws = cfg.L, cfg.wf, cfg.ws
    T = BLOCK
    npf, nps = wf // LANES, ws // LANES

    def body(fqk_ref, fv_ref, sq_ref, sk_ref, sv_ref, fz_ref, gq_ref, gk_ref, gsq_ref, gsk_ref, b_ref,
             qn_ref, kn_ref, fvo_ref, sqn_ref, skd_ref, svd_ref, c_ref, ct_ref, carry_ref):
        i = pl.program_id(0)
        lo = _lane((T, LANES)) < HEAD_DIM
        for p in range(npf):
            sl = slice(p * LANES, (p + 1) * LANES)
            qn_ref[:, sl] = _head_norm(fqk_ref[:, sl], gq_ref[...], lo, SCALE).astype(BF16)
            kn_ref[:, sl] = _head_norm(fqk_ref[:, wf + p * LANES: wf + (p + 1) * LANES], gk_ref[...], lo, 1.0).astype(BF16)
        fvo_ref[...] = fv_ref[...].astype(BF16)
        for p in range(nps):
            sl = slice(p * LANES, (p + 1) * LANES)
            sqn_ref[:, sl] = _head_norm(sq_ref[:, sl], gsq_ref[...], lo, SCALE).astype(BF16)
        k0, k1 = _dup(_head_norm(sk_ref[...], gsk_ref[...], lo, 1.0), lo)
        skd_ref[:, :LANES] = k0.astype(BF16)
        skd_ref[:, LANES:] = k1.astype(BF16)
        v0, v1 = _dup(sv_ref[...], lo)
        svd_ref[:, :LANES] = v0.astype(BF16)
        svd_ref[:, LANES:] = v1.astype(BF16)

        @pl.when(i == 0)
        def _():
            carry_ref[...] = jnp.zeros_like(carry_ref)

        z = fz_ref[...] + b_ref[...]
        lf = jnp.minimum(z, 0.0) - jnp.log(1.0 + jnp.exp(-jnp.abs(z)))
        row = lax.broadcasted_iota(jnp.int32, (T, T), 0)
        col = lax.broadcasted_iota(jnp.int32, (T, T), 1)
        tri = jnp.where(col <= row, 1.0, 0.0).astype(F32)
        c = jnp.dot(tri, lf, precision=HI, preferred_element_type=F32) + carry_ref[0:1, :]
        c_ref[...] = c
        ct_ref[...] = c.T
        carry_ref[0:1, :] = c_ref[T - 1:T, :]

    def rows(w, cb):
        return pl.BlockSpec((T, w), lambda i, cb=cb: (i, cb))

    vec = pl.BlockSpec((1, LANES), lambda i: (0, 0))
    return pl.pallas_call(
        body, name=name, grid=(L // T,),
        in_specs=[rows(2 * wf, 0), rows(wf, 2), rows(ws, cfg.o_sq // ws), rows(LANES, cfg.o_sk // LANES),
                  rows(LANES, cfg.o_sv // LANES), rows(LANES, cfg.o_fz // LANES), vec, vec, vec, vec, vec],
        out_specs=[rows(wf, 0), rows(wf, 0), rows(wf, 0), rows(ws, 0), rows(2 * LANES, 0), rows(2 * LANES, 0),
                   rows(LANES, 0), pl.BlockSpec((LANES, T), lambda i: (0, i))],
        out_shape=[jax.ShapeDtypeStruct((L, wf), BF16)] * 3 + [jax.ShapeDtypeStruct((L, ws), BF16)]
        + [jax.ShapeDtypeStruct((L, 2 * LANES), BF16)] * 2
        + [jax.ShapeDtypeStruct((L, LANES), F32), jax.ShapeDtypeStruct((LANES, L), F32)],
        scratch_shapes=[pltpu.VMEM((8, LANES), F32)],
        compiler_params=_cparams(("arbitrary",)),
    )(u, u, u, u, u, u, gq, gk, gsq, gsk, bias)


def _mix_prep_bwd(name, cfg, u, gq, gk, gsq, gsk, bias, dqn, dkn, dfv, dsqn, dskd, dsvd, dct, dcq):
    L, wf, ws = cfg.L, cfg.wf, cfg.ws
    T = BLOCK
    nb = L // T
    npf, nps = wf // LANES, ws // LANES

    def body(fqk_ref, sq_ref, sk_ref, fz_ref, gq_ref, gk_ref, gsq_ref, gsk_ref, b_ref,
             dqn_ref, dkn_ref, dfv_ref, dsqn_ref, dskd_ref, dsvd_ref, dct_ref, dcq_ref,
             du_ref, dgq_ref, dgk_ref, dgsq_ref, dgsk_ref, db_ref, carry_ref):
        i = pl.program_id(0)
        lo = _lane((T, LANES)) < HEAD_DIM

        @pl.when(i == 0)
        def _():
            carry_ref[...] = jnp.zeros_like(carry_ref)
            for r in (dgq_ref, dgk_ref, dgsq_ref, dgsk_ref, db_ref):
                r[...] = jnp.zeros_like(r)

        accq = jnp.zeros((1, LANES), F32)
        acck = jnp.zeros((1, LANES), F32)
        for p in range(npf):
            sl = slice(p * LANES, (p + 1) * LANES)
            dx, dg = _head_norm_bwd(dqn_ref[:, sl], fqk_ref[:, sl], gq_ref[...], lo, SCALE)
            du_ref[:, sl] = dx.astype(BF16)
            accq = accq + dg
            slk = slice(wf + p * LANES, wf + (p + 1) * LANES)
            dx, dg = _head_norm_bwd(dkn_ref[:, sl], fqk_ref[:, slk], gk_ref[...], lo, 1.0)
            du_ref[:, slk] = dx.astype(BF16)
            acck = acck + dg
        dgq_ref[...] += accq
        dgk_ref[...] += acck
        du_ref[:, cfg.o_fv:cfg.o_fv + wf] = dfv_ref[...].astype(BF16)
        accs = jnp.zeros((1, LANES), F32)
        for p in range(nps):
            sl = slice(p * LANES, (p + 1) * LANES)
            dx, dg = _head_norm_bwd(dsqn_ref[:, sl], sq_ref[:, sl], gsq_ref[...], lo, SCALE)
            du_ref[:, cfg.o_sq + p * LANES: cfg.o_sq + (p + 1) * LANES] = dx.astype(BF16)
            accs = accs + dg
        dgsq_ref[...] += accs

        def fold(ref):
            a0, a1 = ref[:, :LANES], ref[:, LANES:]
            return jnp.where(lo, a0 + pltpu.roll(a0, 64, 1), a1 + pltpu.roll(a1, 64, 1))

        dx, dg = _head_norm_bwd(fold(dskd_ref), sk_ref[...], gsk_ref[...], lo, 1.0)
        du_ref[:, cfg.o_sk:cfg.o_sk + LANES] = dx.astype(BF16)
        dgsk_ref[...] += dg
        du_ref[:, cfg.o_sv:cfg.o_sv + LANES] = fold(dsvd_ref).astype(BF16)

        dc = dct_ref[...].T + dcq_ref[...]
        row = lax.broadcasted_iota(jnp.int32, (T, T), 0)
        col = lax.broadcasted_iota(jnp.int32, (T, T), 1)
        triu = jnp.where(col >= row, 1.0, 0.0).astype(F32)
        dlf = jnp.dot(triu, dc, precision=HI, preferred_element_type=F32) + carry_ref[0:1, :]
        carry_ref[0:1, :] = dlf[0:1, :]
        z = fz_ref[...] + b_ref[...]
        dz = dlf * _sigmoid(-z)
        du_ref[:, cfg.o_fz:cfg.o_fz + LANES] = dz.astype(BF16)
        db_ref[...] += jnp.sum(dz, axis=0, keepdims=True)

    def rows(w, cb):
        return pl.BlockSpec((T, w), lambda i, cb=cb: (nb - 1 - i, cb))

    vec = pl.BlockSpec((1, LANES), lambda i: (0, 0))
    vshape = jax.ShapeDtypeStruct((1, LANES), F32)
    return pl.pallas_call(
        body, name=name, grid=(nb,),
        in_specs=[rows(2 * wf, 0), rows(ws, cfg.o_sq // ws), rows(LANES, cfg.o_sk // LANES),
                  rows(LANES, cfg.o_fz // LANES), vec, vec, vec, vec, vec,
                  rows(wf, 0), rows(wf, 0), rows(wf, 0), rows(ws, 0), rows(2 * LANES, 0), rows(2 * LANES, 0),
                  pl.BlockSpec((LANES, T), lambda i: (0, nb - 1 - i)), rows(LANES, 0)],
        out_specs=[rows(cfg.up, 0), vec, vec, vec, vec, vec],
        out_shape=[jax.ShapeDtypeStruct((L, cfg.up), BF16)] + [vshape] * 5,
        scratch_shapes=[pltpu.VMEM((8, LANES), F32)],
        compiler_params=_cparams(("arbitrary",)),
    )(u, u, u, u, gq, gk, gsq, gsk, bias, dqn, dkn, dfv, dsqn, dskd, dsvd, dct, dcq)


def _fox_fwd(name, cfg, qn, kn, fv, c, ct):
    L, wf, cs = cfg.L, cfg.wf, cfg.cs
    T = BLOCK
    npairs = wf // LANES
    pad = cfg.pad

    def body(q_ref, k_ref, v_ref, c_ref, ct_ref, o_ref, lse_ref):
        p = pl.program_id(0)
        i = pl.program_id(1)
        lane = _lane((T, LANES))
        lo = lane < HEAD_DIM
        q = q_ref[...]
        qh = (jnp.where(lo, q, jnp.zeros_like(q)), jnp.where(lo, jnp.zeros_like(q), q))
        cblk = c_ref[...]
        cq = tuple(jnp.sum(jnp.where(lane == p * cs + hh, cblk, 0.0), axis=1, keepdims=True) for hh in range(2))
        qpos = i * T + lax.broadcasted_iota(jnp.int32, (T, T), 0)

        def step(j, carry):
            m0, m1, l0, l1, acc = carry
            ms, ls = [m0, m1], [l0, l1]
            off = pl.multiple_of(j * T, T)
            k = k_ref[pl.ds(off, T), :]
            v = v_ref[pl.ds(off, T), :]
            kpos = j * T + lax.broadcasted_iota(jnp.int32, (T, T), 1)
            allowed = (kpos <= qpos) & (kpos >= pad)
            for hh in range(2):
                ck = ct_ref[hh:hh + 1, pl.ds(off, T)]
                s = _dot_nt(qh[hh], k) + cq[hh] - ck
                s = jnp.where(allowed, s, NEG_INF)
                m_new = jnp.maximum(ms[hh], jnp.max(s, axis=1, keepdims=True))
                alpha = jnp.exp(ms[hh] - m_new)
                pr = jnp.exp(s - m_new)
                ls[hh] = alpha * ls[hh] + jnp.sum(pr, axis=1, keepdims=True)
                ms[hh] = m_new
                sel = lo if hh == 0 else jnp.logical_not(lo)
                vh = jnp.where(sel, v, jnp.zeros_like(v))
                acc = acc * jnp.where(sel, alpha, 1.0) + _dot(pr.astype(BF16), vh)
            return ms[0], ms[1], ls[0], ls[1], acc

        init = (jnp.full((T, 1), NEG_INF, F32), jnp.full((T, 1), NEG_INF, F32),
                jnp.zeros((T, 1), F32), jnp.zeros((T, 1), F32), jnp.zeros((T, LANES), F32))
        m0, m1, l0, l1, acc = lax.fori_loop(0, i + 1, step, init)
        o_ref[...] = acc / jnp.where(lo, l0, l1)
        lse_ref[...] = jnp.where(lo, m0 + jnp.log(l0), m1 + jnp.log(l1))

    blk = pl.BlockSpec((T, LANES), lambda p, i: (i, p))
    full = pl.BlockSpec((L, LANES), lambda p, i: (0, p))
    return pl.pallas_call(
        body, name=name, grid=(npairs, L // T),
        in_specs=[blk, full, full, pl.BlockSpec((T, LANES), lambda p, i: (i, 0)),
                  pl.BlockSpec((cs, L), lambda p, i: (p, 0))],
        out_specs=[blk, blk],
        out_shape=[jax.ShapeDtypeStruct((L, wf), F32)] * 2,
        compiler_params=_cparams(("parallel", "parallel")),
    )(qn, kn, fv, c, ct)


def _fox_bwd(name, cfg, qn, kn, fv, c, ct, o, lse, do):
    L, wf, cs = cfg.L, cfg.wf, cfg.cs
    T = BLOCK
    nb = L // T
    npairs = wf // LANES
    pad = cfg.pad

    def body(q_ref, k_ref, v_ref, c_ref, ct_ref, o_ref, lse_ref, do_ref, dq_ref, dk_ref, dv_ref, dct_ref, dcq_ref):
        p = pl.program_id(0)
        j = pl.program_id(1)
        lane = _lane((T, LANES))
        lo = lane < HEAD_DIM
        sels = (lo, jnp.logical_not(lo))

        @pl.when(j == 0)
        def _():
            dq_ref[...] = jnp.zeros_like(dq_ref)

        @pl.when((j == 0) & (p == 0))
        def _():
            dcq_ref[...] = jnp.zeros_like(dcq_ref)

        k = k_ref[...]
        v = v_ref[...]
        kh = tuple(jnp.where(s_, k, jnp.zeros_like(k)) for s_ in sels)
        ck = tuple(ct_ref[hh:hh + 1, :] for hh in range(2))
        kpos = j * T + lax.broadcasted_iota(jnp.int32, (T, T), 1)

        def step(i, carry):
            dk, dv, dc0, dc1 = carry
            dcs = [dc0, dc1]
            off = pl.multiple_of(i * T, T)
            q = q_ref[pl.ds(off, T), :]
            dov = do_ref[pl.ds(off, T), :]
            dd = dov * o_ref[pl.ds(off, T), :]
            lse_b = lse_ref[pl.ds(off, T), :]
            cblk = c_ref[pl.ds(off, T), :]
            qpos = i * T + lax.broadcasted_iota(jnp.int32, (T, T), 0)
            allowed = (kpos <= qpos) & (kpos >= pad)
            dq = jnp.zeros((T, LANES), F32)
            dcq = jnp.zeros((T, LANES), F32)
            for hh in range(2):
                sel = sels[hh]
                qhh = jnp.where(sel, q, jnp.zeros_like(q))
                doh = jnp.where(sel, dov, 0.0).astype(BF16)
                dsum = jnp.sum(jnp.where(sel, dd, 0.0), axis=1, keepdims=True)
                lse_h = jnp.sum(jnp.where(lane == hh * HEAD_DIM, lse_b, 0.0), axis=1, keepdims=True)
                cq = jnp.sum(jnp.where(lane == p * cs + hh, cblk, 0.0), axis=1, keepdims=True)
                s = _dot_nt(qhh, k) + cq - ck[hh]
                pr = jnp.where(allowed, jnp.exp(jnp.where(allowed, s, NEG_INF) - lse_h), 0.0)
                dp = _dot_nt(doh, v)
                ds = pr * (dp - dsum)
                dsb = ds.astype(BF16)
                dv = dv + _dot_tn(pr.astype(BF16), doh)
                dk = dk + _dot_tn(dsb, qhh)
                dq = dq + _dot(dsb, kh[hh])
                dcs[hh] = dcs[hh] - jnp.sum(ds, axis=0, keepdims=True)
                dcq = dcq + jnp.where(lane == p * cs + hh, jnp.sum(ds, axis=1, keepdims=True), 0.0)
            dq_ref[pl.ds(off, T), :] += dq
            dcq_ref[pl.ds(off, T), :] += dcq
            return dk, dv, dcs[0], dcs[1]

        init = (jnp.zeros((T, LANES), F32), jnp.zeros((T, LANES), F32),
                jnp.zeros((1, T), F32), jnp.zeros((1, T), F32))
        dk, dv, dc0, dc1 = lax.fori_loop(j, nb, step, init)
        dk_ref[...] = dk
        dv_ref[...] = dv
        dct_ref[...] = jnp.zeros_like(dct_ref)
        dct_ref[0:1, :] = dc0
        dct_ref[1:2, :] = dc1

    blk = pl.BlockSpec((T, LANES), lambda p, j: (j, p))
    full = pl.BlockSpec((L, LANES), lambda p, j: (0, p))
    return pl.pallas_call(
        body, name=name, grid=(npairs, nb),
        in_specs=[full, blk, blk, pl.BlockSpec((L, LANES), lambda p, j: (0, 0)),
                  pl.BlockSpec((cs, T), lambda p, j: (p, j)), full, full, full],
        out_specs=[full, blk, blk, pl.BlockSpec((cs, T), lambda p, j: (p, j)),
                   pl.BlockSpec((L, LANES), lambda p, j: (0, 0))],
        out_shape=[jax.ShapeDtypeStruct((L, wf), F32)] * 3
        + [jax.ShapeDtypeStruct((LANES, L), F32), jax.ShapeDtypeStruct((L, LANES), F32)],
        compiler_params=_cparams(("arbitrary", "arbitrary")),
    )(qn, kn, fv, c, ct, o, lse, do)


def _swa_scores(qh, kp, kc, slope, sink, i, pad):
    T = BLOCK
    t = lax.broadcasted_iota(jnp.int32, (T, T), 0)
    s_ = lax.broadcasted_iota(jnp.int32, (T, T), 1)
    dist_c = t - s_
    dist_p = dist_c + T
    ok_c = (dist_c >= 0) & (i * T + s_ >= pad)
    ok_p = (dist_p < T) & ((i - 1) * T + s_ >= pad)
    sp = jnp.where(ok_p, _dot_nt(qh, kp) - slope * dist_p.astype(F32), NEG_INF)
    sc = jnp.where(ok_c, _dot_nt(qh, kc) - slope * dist_c.astype(F32), NEG_INF)
    m = jnp.maximum(jnp.maximum(jnp.max(sp, axis=1, keepdims=True), jnp.max(sc, axis=1, keepdims=True)), sink)
    ep = jnp.exp(sp - m)
    ec = jnp.exp(sc - m)
    es = jnp.exp(sink - m)
    den = jnp.sum(ep, axis=1, keepdims=True) + jnp.sum(ec, axis=1, keepdims=True) + es
    return ep / den, ec / den, es / den


def _swa_fwd(name, cfg, sqn, skd, svd, sinks, slopes):
    L, ws, group = cfg.L, cfg.ws, cfg.group
    T = BLOCK
    npairs = ws // LANES
    ppk = group // 2
    pad = cfg.pad

    def body(sink_ref, slope_ref, q_ref, kp_ref, kc_ref, vp_ref, vc_ref, o_ref):
        p = pl.program_id(0)
        i = pl.program_id(1)
        lo = _lane((T, LANES)) < HEAD_DIM
        q = q_ref[...]
        acc = jnp.zeros((T, LANES), F32)
        for hh in range(2):
            sel = lo if hh == 0 else jnp.logical_not(lo)
            qh = jnp.where(sel, q, jnp.zeros_like(q))
            pp, pc, _ = _swa_scores(qh, kp_ref[...], kc_ref[...], slope_ref[2 * p + hh], sink_ref[2 * p + hh], i, pad)
            vp = jnp.where(sel, vp_ref[...], jnp.zeros_like(q))
            vc = jnp.where(sel, vc_ref[...], jnp.zeros_like(q))
            acc = acc + _dot(pp.astype(BF16), vp) + _dot(pc.astype(BF16), vc)
        o_ref[...] = acc

    smem = pl.BlockSpec(memory_space=pltpu.SMEM)
    prev = pl.BlockSpec((T, LANES), lambda p, i: (jnp.maximum(i - 1, 0), p // ppk))
    cur = pl.BlockSpec((T, LANES), lambda p, i: (i, p // ppk))
    blk = pl.BlockSpec((T, LANES), lambda p, i: (i, p))
    return pl.pallas_call(
        body, name=name, grid=(npairs, L // T),
        in_specs=[smem, smem, blk, prev, cur, prev, cur],
        out_specs=blk, out_shape=jax.ShapeDtypeStruct((L, ws), F32),
        compiler_params=_cparams(("parallel", "parallel")),
    )(sinks, slopes, sqn, skd, skd, svd, svd)


def _swa_bwd(name, cfg, sqn, skd, svd, sinks, slopes, o, do):
    L, ws, group, hkv = cfg.L, cfg.ws, cfg.group, cfg.hkv
    T = BLOCK
    gw = group * HEAD_DIM
    ppk = group // 2
    pad = cfg.pad

    def body(sink_ref, slope_ref, q_ref, kp_ref, kc_ref, vp_ref, vc_ref, o_ref, do_ref,
             dq_ref, dk_ref, dv_ref, dsink_ref):
        kv = pl.program_id(0)
        i = pl.program_id(1)
        lane = _lane((T, LANES))
        lo = lane < HEAD_DIM

        @pl.when(i == 0)
        def _():
            dk_ref[...] = jnp.zeros_like(dk_ref)
            dv_ref[...] = jnp.zeros_like(dv_ref)
            dsink_ref[...] = jnp.zeros_like(dsink_ref)

        kp, kc, vp, vc = kp_ref[...], kc_ref[...], vp_ref[...], vc_ref[...]
        dkp = jnp.zeros((T, LANES), F32)
        dkc = jnp.zeros((T, LANES), F32)
        dvp = jnp.zeros((T, LANES), F32)
        dvc = jnp.zeros((T, LANES), F32)
        for pp_ in range(ppk):
            sl = slice(pp_ * LANES, (pp_ + 1) * LANES)
            q = q_ref[:, sl]
            dov = do_ref[:, sl]
            dd = dov * o_ref[:, sl]
            dq = jnp.zeros((T, LANES), F32)
            dsk = jnp.zeros((1, LANES), F32)
            for hh in range(2):
                sel = lo if hh == 0 else jnp.logical_not(lo)
                h = kv * group + 2 * pp_ + hh
                qh = jnp.where(sel, q, jnp.zeros_like(q))
                pp, pc, ps = _swa_scores(qh, kp, kc, slope_ref[h], sink_ref[h], i, pad)
                doh = jnp.where(sel, dov, 0.0).astype(BF16)
                dsum = jnp.sum(jnp.where(sel, dd, 0.0), axis=1, keepdims=True)
                dsp = (pp * (_dot_nt(doh, vp) - dsum)).astype(BF16)
                dsc = (pc * (_dot_nt(doh, vc) - dsum)).astype(BF16)
                khp = jnp.where(sel, kp, jnp.zeros_like(kp))
                khc = jnp.where(sel, kc, jnp.zeros_like(kc))
                dq = dq + _dot(dsp, khp) + _dot(dsc, khc)
                dkp = dkp + _dot_tn(dsp, qh)
                dkc = dkc + _dot_tn(dsc, qh)
                dvp = dvp + _dot_tn(pp.astype(BF16), doh)
                dvc = dvc + _dot_tn(pc.astype(BF16), doh)
                dsk = dsk + jnp.where(lane[0:1, :] == hh * HEAD_DIM, -jnp.sum(ps * dsum), 0.0)
            dq_ref[:, sl] = dq
            dsink_ref[8 * pp_:8 * pp_ + 1, :] += dsk

        cur = pl.multiple_of(i * T, T)
        dk_ref[pl.ds(cur, T), :] += dkc
        dv_ref[pl.ds(cur, T), :] += dvc

        @pl.when(i > 0)
        def _():
            prv = pl.multiple_of((i - 1) * T, T)
            dk_ref[pl.ds(prv, T), :] += dkp
            dv_ref[pl.ds(prv, T), :] += dvp

    smem = pl.BlockSpec(memory_space=pltpu.SMEM)
    prev = pl.BlockSpec((T, LANES), lambda kv, i: (jnp.maximum(i - 1, 0), kv))
    cur = pl.BlockSpec((T, LANES), lambda kv, i: (i, kv))
    qblk = pl.BlockSpec((T, gw), lambda kv, i: (i, kv))
    full = pl.BlockSpec((L, LANES), lambda kv, i: (0, kv))
    return pl.pallas_call(
        body, name=name, grid=(hkv, L // T),
        in_specs=[smem, smem, qblk, prev, cur, prev, cur, qblk, qblk],
        out_specs=[qblk, full, full, pl.BlockSpec((8 * ppk, LANES), lambda kv, i: (kv, 0))],
        out_shape=[jax.ShapeDtypeStruct((L, ws), F32), jax.ShapeDtypeStruct((L, 2 * LANES), F32),
                   jax.ShapeDtypeStruct((L, 2 * LANES), F32), jax.ShapeDtypeStruct((8 * ppk * hkv, LANES), F32)],
        compiler_params=_cparams(("parallel", "arbitrary")),
    )(sinks, slopes, sqn, skd, skd, svd, svd, o, do)


def _out_norm(name, cfg, o_fox, o_swa, g_fox, g_swa):
    L, wf, ws = cfg.L, cfg.wf, cfg.ws
    tm = _tile(L, 256, 16)

    def body(of_ref, os_ref, gf_ref, gs_ref, o_ref):
        for src, g_ref, lo_, w in ((of_ref, gf_ref, 0, wf), (os_ref, gs_ref, wf, ws)):
            x = src[...]
            r = lax.rsqrt(jnp.mean(x * x, axis=1, keepdims=True) + EPS)
            o_ref[:, lo_:lo_ + w] = (x * r * g_ref[...]).astype(BF16)

    return pl.pallas_call(
        body, name=name, grid=(L // tm,),
        in_specs=[pl.BlockSpec((tm, wf), lambda i: (i, 0)), pl.BlockSpec((tm, ws), lambda i: (i, 0)),
                  pl.BlockSpec((1, wf), lambda i: (0, 0)), pl.BlockSpec((1, ws), lambda i: (0, 0))],
        out_specs=pl.BlockSpec((tm, wf + ws), lambda i: (i, 0)),
        out_shape=jax.ShapeDtypeStruct((L, wf + ws), BF16),
        compiler_params=_cparams(("parallel",)),
    )(o_fox, o_swa, g_fox, g_swa)


def _out_norm_bwd(name, cfg, dcat, o_fox, o_swa, g_fox, g_swa):
    L, wf, ws = cfg.L, cfg.wf, cfg.ws
    tm = _tile(L, 256, 16)

    def body(d_ref, of_ref, os_ref, gf_ref, gs_ref, dof_ref, dos_ref, dgf_ref, dgs_ref):
        i = pl.program_id(0)

        @pl.when(i == 0)
        def _():
            dgf_ref[...] = jnp.zeros_like(dgf_ref)
            dgs_ref[...] = jnp.zeros_like(dgs_ref)

        for src, g_ref, dst, dg_ref, lo_, w in ((of_ref, gf_ref, dof_ref, dgf_ref, 0, wf),
                                                (os_ref, gs_ref, dos_ref, dgs_ref, wf, ws)):
            x = src[...]
            dy = d_ref[:, lo_:lo_ + w]
            r = lax.rsqrt(jnp.mean(x * x, axis=1, keepdims=True) + EPS)
            wv = dy * g_ref[...]
            proj = jnp.sum(wv * x, axis=1, keepdims=True) * (1.0 / w)
            dst[...] = r * wv - x * (r * r * r * proj)
            dg_ref[...] += jnp.sum(dy * x * r, axis=0, keepdims=True)

    return pl.pallas_call(
        body, name=name, grid=(L // tm,),
        in_specs=[pl.BlockSpec((tm, wf + ws), lambda i: (i, 0)), pl.BlockSpec((tm, wf), lambda i: (i, 0)),
                  pl.BlockSpec((tm, ws), lambda i: (i, 0)),
                  pl.BlockSpec((1, wf), lambda i: (0, 0)), pl.BlockSpec((1, ws), lambda i: (0, 0))],
        out_specs=[pl.BlockSpec((tm, wf), lambda i: (i, 0)), pl.BlockSpec((tm, ws), lambda i: (i, 0)),
                   pl.BlockSpec((1, wf), lambda i: (0, 0)), pl.BlockSpec((1, ws), lambda i: (0, 0))],
        out_shape=[jax.ShapeDtypeStruct((L, wf), F32), jax.ShapeDtypeStruct((L, ws), F32),
                   jax.ShapeDtypeStruct((1, wf), F32), jax.ShapeDtypeStruct((1, ws), F32)],
        compiler_params=_cparams(("arbitrary",)),
    )(dcat, o_fox, o_swa, g_fox, g_swa)


def _loss_head(name, h, target):
    L, D = h.shape
    T = BLOCK

    def body(h_ref, t_ref, loss_ref, dh_ref):
        i = pl.program_id(0)

        @pl.when(i == 0)
        def _():
            loss_ref[...] = jnp.zeros_like(loss_ref)
            dh_ref[...] = jnp.zeros_like(dh_ref)

        @pl.when(i > 0)
        def _():
            err = h_ref[...] - t_ref[...]
            dh_ref[...] = err * (1.0 / D)
            loss_ref[...] += jnp.sum(err * err) * (0.5 / D)

    return pl.pallas_call(
        body, name=name, grid=(L // T,),
        in_specs=[pl.BlockSpec((T, D), lambda i: (i, 0)), pl.BlockSpec((T, D), lambda i: (jnp.maximum(i - 1, 0), 0))],
        out_specs=[pl.BlockSpec((8, LANES), lambda i: (0, 0)), pl.BlockSpec((T, D), lambda i: (i, 0))],
        out_shape=[jax.ShapeDtypeStruct((8, LANES), F32), jax.ShapeDtypeStruct((L, D), F32)],
        compiler_params=_cparams(("arbitrary",)),
    )(h, target)


def _place():
    return lax.axis_index("x"), lax.axis_index("y"), lax.axis_index("c")


def _all_gather(name, shards):
    n = len(shards)

    def body(*refs):
        ins, outs = refs[:n], refs[n:2 * n]
        send_sems, recv_sems, local_sems = refs[2 * n:]
        x, y, c = _place()
        me, sibling = (x, y, c), (x, y, 1 - c)
        chips = [(1 - x, y), (x, 1 - y), (1 - x, 1 - y)]

        def rows(a, dev):
            px, py, pc = dev
            return outs[a].at[:, 4 * px + 2 * py + pc]

        def copy(a, k, block, to, src=None):
            return pltpu.make_async_remote_copy(
                src_ref=rows(a, block) if src is None else src, dst_ref=rows(a, block),
                send_sem=send_sems.at[a, k], recv_sem=recv_sems.at[a, k], device_id=to, device_id_type=MESH)

        mine = [pltpu.make_async_copy(ins[a], rows(a, me), local_sems.at[a]) for a in range(n)]
        for cp in mine:
            cp.start()
        first = []
        for a in range(n):
            first.append(copy(a, 0, me, sibling, src=ins[a]))
            first += [copy(a, 1 + j, me, (*chip, c), src=ins[a]) for j, chip in enumerate(chips)]
        for cp in first:
            cp.start()
        passed = []
        for j, chip in enumerate(chips):
            for a in range(n):
                copy(a, 1 + j, (*chip, c), me).wait_recv()
                fwd = copy(a, 4 + j, (*chip, c), sibling)
                fwd.start()
                passed.append(fwd)
        for a in range(n):
            copy(a, 0, sibling, me).wait_recv()
            for j, chip in enumerate(chips):
                copy(a, 4 + j, (*chip, 1 - c), me).wait_recv()
        for cp in first + passed:
            cp.wait_send()
        for cp in mine:
            cp.wait()

    any_spec = pl.BlockSpec(memory_space=pl.ANY)
    return pl.pallas_call(
        body, name=name,
        in_specs=[any_spec] * n, out_specs=[any_spec] * n,
        out_shape=[jax.ShapeDtypeStruct((s.shape[0], N_DEV) + s.shape[1:], s.dtype) for s in shards],
        scratch_shapes=[pltpu.SemaphoreType.DMA((n, 7)), pltpu.SemaphoreType.DMA((n, 7)), pltpu.SemaphoreType.DMA((n,))],
    )(*shards)


def _send_to_sibling(name, grads):
    n = len(grads)

    def body(*refs):
        ins, outs = refs[:n], refs[n:2 * n]
        send_sems, recv_sems = refs[2 * n:]
        x, y, c = _place()
        copies = [pltpu.make_async_remote_copy(
            src_ref=ins[a].at[:, :, 1 - c], dst_ref=outs[a], send_sem=send_sems.at[a], recv_sem=recv_sems.at[a],
            device_id=(x, y, 1 - c), device_id_type=MESH) for a in range(n)]
        for cp in copies:
            cp.start()
        for cp in copies:
            cp.wait()

    any_spec = pl.BlockSpec(memory_space=pl.ANY)
    return pl.pallas_call(
        body, name=name,
        in_specs=[any_spec] * n, out_specs=[any_spec] * n,
        out_shape=[jax.ShapeDtypeStruct(g.shape[:2] + g.shape[3:], g.dtype) for g in grads],
        scratch_shapes=[pltpu.SemaphoreType.DMA((n,)), pltpu.SemaphoreType.DMA((n,))],
    )(*grads)


def _send_to_chips(name, parts):
    n = len(parts)

    def body(*refs):
        ins, outs = refs[:n], refs[n:2 * n]
        send_sems, recv_sems = refs[2 * n:]
        x, y, c = _place()
        chips = [(1 - x, y), (x, 1 - y), (1 - x, 1 - y)]
        copies = []
        for a in range(n):
            for k, (px, py) in enumerate(chips):
                copies.append(pltpu.make_async_remote_copy(
                    src_ref=ins[a].at[:, 2 * px + py], dst_ref=outs[a].at[k],
                    send_sem=send_sems.at[a, k], recv_sem=recv_sems.at[a, k],
                    device_id=(px, py, c), device_id_type=MESH))
        for cp in copies:
            cp.start()
        for cp in copies:
            cp.wait()

    any_spec = pl.BlockSpec(memory_space=pl.ANY)
    return pl.pallas_call(
        body, name=name,
        in_specs=[any_spec] * n, out_specs=[any_spec] * n,
        out_shape=[jax.ShapeDtypeStruct((3, p.shape[0]) + p.shape[2:], p.dtype) for p in parts],
        scratch_shapes=[pltpu.SemaphoreType.DMA((n, 3)), pltpu.SemaphoreType.DMA((n, 3))],
    )(*parts)


def _pair_add(name, grad, landed, core):
    A, _, _, r, C = grad.shape

    def body(s_ref, g_ref, l_ref, o_ref):
        o_ref[...] = (g_ref[...].astype(F32) + l_ref[...].astype(F32)).astype(o_ref.dtype)

    blk = pl.BlockSpec((None, None, r, C), lambda a, k, s: (a, k, 0, 0))
    return pl.pallas_call(
        body, name=name,
        grid_spec=pltpu.PrefetchScalarGridSpec(
            num_scalar_prefetch=1, grid=(A, 4),
            in_specs=[pl.BlockSpec((None, None, None, r, C), lambda a, k, s: (a, k, s[0], 0, 0)), blk],
            out_specs=blk),
        out_shape=jax.ShapeDtypeStruct(landed.shape, landed.dtype),
        compiler_params=_cparams(("parallel", "parallel")),
    )(core, grad, landed)


def _chip_sum(name, part, landed, chip):
    A, _, r, C = part.shape

    def body(s_ref, p_ref, l0_ref, l1_ref, l2_ref, o_ref):
        o_ref[...] = ((p_ref[...].astype(F32) + l0_ref[...].astype(F32))
                      + (l1_ref[...].astype(F32) + l2_ref[...].astype(F32)))

    def land(k):
        return pl.BlockSpec((None, None, r, C), lambda a, s, k=k: (k, a, 0, 0))

    return pl.pallas_call(
        body, name=name,
        grid_spec=pltpu.PrefetchScalarGridSpec(
            num_scalar_prefetch=1, grid=(A,),
            in_specs=[pl.BlockSpec((None, None, r, C), lambda a, s: (a, s[0], 0, 0)), land(0), land(1), land(2)],
            out_specs=pl.BlockSpec((None, r, C), lambda a, s: (a, 0, 0))),
        out_shape=jax.ShapeDtypeStruct((A, r, C), F32),
        compiler_params=_cparams(("parallel",)),
    )(chip, part, landed, landed, landed)


def _gather_sum(name, v):
    R = v.shape[0]

    def body(x_ref, o_ref, buf_ref, send_sems, recv_sems):
        x, y, c = _place()
        me, sibling = (x, y, c), (x, y, 1 - c)
        chips = [(1 - x, y), (x, 1 - y), (1 - x, 1 - y)]

        def rows(dev):
            px, py, pc = dev
            return buf_ref.at[4 * px + 2 * py + pc]

        def copy(k, block, to, src=None):
            return pltpu.make_async_remote_copy(
                src_ref=rows(block) if src is None else src, dst_ref=rows(block),
                send_sem=send_sems.at[k], recv_sem=recv_sems.at[k], device_id=to, device_id_type=MESH)

        first = [copy(0, me, sibling, src=x_ref)]
        first += [copy(1 + j, me, (*chip, c), src=x_ref) for j, chip in enumerate(chips)]
        for cp in first:
            cp.start()
        rows(me)[...] = x_ref[...]
        passed = [copy(4 + j, (*chip, c), sibling) for j, chip in enumerate(chips)]
        for j, chip in enumerate(chips):
            copy(1 + j, (*chip, c), me).wait_recv()
            passed[j].start()
        copy(0, sibling, me).wait_recv()
        for j, chip in enumerate(chips):
            copy(4 + j, (*chip, 1 - c), me).wait_recv()
        for cp in first + passed:
            cp.wait_send()
        acc = buf_ref[0]
        for d in range(1, N_DEV):
            acc = acc + buf_ref[d]
        o_ref[...] = acc

    vm = pl.BlockSpec(memory_space=pltpu.VMEM)
    return pl.pallas_call(
        body, name=name, in_specs=[vm], out_specs=vm,
        out_shape=jax.ShapeDtypeStruct((R, LANES), F32),
        scratch_shapes=[pltpu.VMEM((N_DEV, R, LANES), F32), pltpu.SemaphoreType.DMA((7,)), pltpu.SemaphoreType.DMA((7,))],
    )(v)


def _adamw(name, g, w, m, v):
    R, C = g.shape
    tr = _tile(R, max(8, (1 << 18) // max(C, 1) // 8 * 8), 8)

    def body(g_ref, w_ref, m_ref, v_ref, d_ref, nm_ref, nv_ref):
        gv = g_ref[...]
        nm = ADAM_B1 * m_ref[...] + (1.0 - ADAM_B1) * gv
        nv = ADAM_B2 * v_ref[...] + (1.0 - ADAM_B2) * (gv * gv)
        m_hat = nm / (1.0 - ADAM_B1 ** ADAM_STEP)
        v_hat = nv / (1.0 - ADAM_B2 ** ADAM_STEP)
        d_ref[...] = -ADAM_LR * (m_hat / (jnp.sqrt(v_hat) + ADAM_EPS) + ADAM_WD * w_ref[...])
        nm_ref[...] = nm
        nv_ref[...] = nv

    blk = pl.BlockSpec((tr, C), lambda i: (i, 0))
    shp = jax.ShapeDtypeStruct((R, C), F32)
    return pl.pallas_call(
        body, name=name, grid=(R // tr,),
        in_specs=[blk] * 4, out_specs=[blk] * 3, out_shape=[shp] * 3,
        compiler_params=_cparams(("parallel",)),
    )(g, w, m, v)


def _adamw_nd(name, g, w, m, v):
    shape = w.shape
    flat = [a.reshape(-1, shape[-1]) for a in (g, w, m, v)]
    return tuple(o.reshape(shape) for o in _adamw(name, *flat))


def _scatter_heads(cfg, vals):
    idx = np.array([(h // 2) * cfg.cs + (h % 2) for h in range(cfg.hf)])
    return jnp.zeros((LANES,), F32).at[idx].set(vals)[None]


def _permute_w_in(cfg, w_in_t):
    wf, ws, hf = cfg.wf, cfg.ws, cfg.hf
    o = 3 * wf
    fz = w_in_t[o:o + hf]
    rest = w_in_t[o + hf:]
    idx = np.array([(h // 2) * cfg.cs + (h % 2) for h in range(hf)])
    fz_blk = jnp.zeros((LANES, w_in_t.shape[1]), w_in_t.dtype).at[idx].set(fz)
    return jnp.concatenate([w_in_t[:o], rest, fz_blk], axis=0)


def _unpermute_dw_in(cfg, dwp):
    wf, hf = cfg.wf, cfg.hf
    o = 3 * wf
    idx = np.array([(h // 2) * cfg.cs + (h % 2) for h in range(hf)])
    fz = dwp[cfg.o_fz:][idx]
    return jnp.concatenate([dwp[:o], fz, dwp[o:cfg.o_fz]], axis=0)


def _pair_gain(g):
    return jnp.tile(g, 2)[None]


def _fold_pair(dg):
    return dg[0, :HEAD_DIM] + dg[0, HEAD_DIM:]


def kernel(x, meta_tokens, ffn1_norm, ffn1_w_gate, ffn1_w_up, ffn1_w_down, mix_norm, w_in, b_forget, fox_q_norm, fox_k_norm, swa_q_norm, swa_k_norm, swa_sinks, fox_out_norm, swa_out_norm, w_out, ffn2_norm, ffn2_w_gate, ffn2_w_up, ffn2_w_down, loss_target, m_meta_tokens, m_ffn1_norm, m_ffn1_w_gate, m_ffn1_w_up, m_ffn1_w_down, m_mix_norm, m_w_in, m_b_forget, m_fox_q_norm, m_fox_k_norm, m_swa_q_norm, m_swa_k_norm, m_swa_sinks, m_fox_out_norm, m_swa_out_norm, m_w_out, m_ffn2_norm, m_ffn2_w_gate, m_ffn2_w_up, m_ffn2_w_down, v_meta_tokens, v_ffn1_norm, v_ffn1_w_gate, v_ffn1_w_up, v_ffn1_w_down, v_mix_norm, v_w_in, v_b_forget, v_fox_q_norm, v_fox_k_norm, v_swa_q_norm, v_swa_k_norm, v_swa_sinks, v_fox_out_norm, v_swa_out_norm, v_w_out, v_ffn2_norm, v_ffn2_w_gate, v_ffn2_w_up, v_ffn2_w_down):
    weights = dict(meta_tokens=meta_tokens, ffn1_norm=ffn1_norm, ffn1_w_gate=ffn1_w_gate, ffn1_w_up=ffn1_w_up,
                   ffn1_w_down=ffn1_w_down, mix_norm=mix_norm, w_in=w_in, b_forget=b_forget, fox_q_norm=fox_q_norm,
                   fox_k_norm=fox_k_norm, swa_q_norm=swa_q_norm, swa_k_norm=swa_k_norm, swa_sinks=swa_sinks,
                   fox_out_norm=fox_out_norm, swa_out_norm=swa_out_norm, w_out=w_out, ffn2_norm=ffn2_norm,
                   ffn2_w_gate=ffn2_w_gate, ffn2_w_up=ffn2_w_up, ffn2_w_down=ffn2_w_down)
    mom_m = dict(meta_tokens=m_meta_tokens, ffn1_norm=m_ffn1_norm, ffn1_w_gate=m_ffn1_w_gate, ffn1_w_up=m_ffn1_w_up,
                 ffn1_w_down=m_ffn1_w_down, mix_norm=m_mix_norm, w_in=m_w_in, b_forget=m_b_forget,
                 fox_q_norm=m_fox_q_norm, fox_k_norm=m_fox_k_norm, swa_q_norm=m_swa_q_norm, swa_k_norm=m_swa_k_norm,
                 swa_sinks=m_swa_sinks, fox_out_norm=m_fox_out_norm, swa_out_norm=m_swa_out_norm, w_out=m_w_out,
                 ffn2_norm=m_ffn2_norm, ffn2_w_gate=m_ffn2_w_gate, ffn2_w_up=m_ffn2_w_up, ffn2_w_down=m_ffn2_w_down)
    mom_v = dict(meta_tokens=v_meta_tokens, ffn1_norm=v_ffn1_norm, ffn1_w_gate=v_ffn1_w_gate, ffn1_w_up=v_ffn1_w_up,
                 ffn1_w_down=v_ffn1_w_down, mix_norm=v_mix_norm, w_in=v_w_in, b_forget=v_b_forget,
                 fox_q_norm=v_fox_q_norm, fox_k_norm=v_fox_k_norm, swa_q_norm=v_swa_q_norm, swa_k_norm=v_swa_k_norm,
                 swa_sinks=v_swa_sinks, fox_out_norm=v_fox_out_norm, swa_out_norm=v_swa_out_norm, w_out=v_w_out,
                 ffn2_norm=v_ffn2_norm, ffn2_w_gate=v_ffn2_w_gate, ffn2_w_up=v_ffn2_w_up, ffn2_w_down=v_ffn2_w_down)
    names = list(weights)

    _, S, D = x.shape
    depth = ffn1_norm.shape[0]
    hf, hs = b_forget.shape[1], swa_sinks.shape[1]
    U = w_in.shape[2] * N_DEV
    hkv = (U - 3 * HEAD_DIM * hf - hf - HEAD_DIM * hs) // (2 * HEAD_DIM)
    cfg = _Cfg(S, D, hf, hs, hkv)
    L = cfg.L
    n_meta = meta_tokens.shape[0]
    assert n_meta == cfg.n_meta
    x_idx, y_idx, c_idx = _place()
    chip_idx = 2 * x_idx + y_idx
    dev_idx = 2 * chip_idx + c_idx

    col_sharded = ("ffn1_w_gate", "ffn1_w_up", "w_in", "ffn2_w_gate", "ffn2_w_up")
    row_sharded = ("ffn1_w_down", "w_out", "ffn2_w_down")
    big = col_sharded + row_sharded
    shards = [jnp.swapaxes(weights[k], 1, 2).astype(BF16) for k in col_sharded]
    shards += [weights[k].astype(BF16) for k in row_sharded]
    shards.append(meta_tokens[None])
    gathered = _all_gather("all_gather_weights", shards)
    full = {k: g.reshape(depth, -1, D) for k, g in zip(big, gathered[:-1])}
    meta_full = jnp.swapaxes(gathered[-1][0], 0, 1).reshape(n_meta, D)
    w_in_p = [_permute_w_in(cfg, full["w_in"][l]) for l in range(depth)]

    slopes = jnp.asarray(2.0 ** (-8.0 * np.arange(1, hs + 1) / hs), dtype=F32)

    h = jnp.concatenate([jnp.zeros((cfg.pad, D), F32), meta_full, x[0]], axis=0)
    saved = []

    def ffn_fwd(tag, l, h_in, norm, wg, wu, wd):
        xn = _rmsnorm_fwd(f"{tag}_norm", h_in, norm[l][None])
        gate, up, act = _ffn_up(f"{tag}_up", xn, full[wg][l], full[wu][l])
        h_out = _matmul(f"{tag}_down", act, full[wd][l], scale=0.5, residual=h_in, tm=544, tn=1024, tk=512)
        return h_out, (xn, gate, up)

    for l in range(depth):
        st = {"h0": h}
        h, st["ffn1"] = ffn_fwd("ffn1", l, h, ffn1_norm, "ffn1_w_gate", "ffn1_w_up", "ffn1_w_down")
        st["h1"] = h
        xn = _rmsnorm_fwd("mix_norm", h, mix_norm[l][None])
        u = _matmul("mix_in", xn, w_in_p[l], trans_b=True, tm=544, tn=640, tk=2048)
        gq, gk = _pair_gain(fox_q_norm[l]), _pair_gain(fox_k_norm[l])
        gsq, gsk = _pair_gain(swa_q_norm[l]), _pair_gain(swa_k_norm[l])
        bias = _scatter_heads(cfg, b_forget[l])
        qn, kn, fv, sqn, skd, svd, c, ct = _mix_prep("mix_prep", cfg, u, gq, gk, gsq, gsk, bias)
        o_fox, lse = _fox_fwd("fox_fwd", cfg, qn, kn, fv, c, ct)
        o_swa = _swa_fwd("swa_fwd", cfg, sqn, skd, svd, swa_sinks[l], slopes)
        o_cat = _out_norm("out_norm", cfg, o_fox, o_swa, fox_out_norm[l][None], swa_out_norm[l][None])
        st["mix"] = (xn, u, gq, gk, gsq, gsk, bias, qn, kn, fv, sqn, skd, svd, c, ct, o_fox, lse, o_swa, o_cat)
        h = _matmul("mix_out", o_cat, full["w_out"][l], residual=h, tm=544, tn=1024, tk=512)
        st["h2"] = h
        h, st["ffn2"] = ffn_fwd("ffn2", l, h, ffn2_norm, "ffn2_w_gate", "ffn2_w_up", "ffn2_w_down")
        saved.append(st)

    loss_blk, dh = _loss_head("loss_head", h, loss_target[0])

    big_grads = {k: [None] * depth for k in big}
    small = {k: [None] * depth for k in names if k not in big and k != "meta_tokens"}

    def ffn_bwd(tag, l, dh_out, h_in, st_, norm, wg, wu, wd):
        xn, gate, up = st_
        dgate, dup, act = _ffn_bwd_act(f"{tag}_dact", dh_out, full[wd][l], gate, up)
        big_grads[wd][l] = _matmul(f"{tag}_dwd", act, dh_out, trans_a=True, scale=0.5, out_dtype=BF16,
                                   tm=512, tn=1024, tk=544)
        big_grads[wg][l] = _matmul(f"{tag}_dwg", dgate, xn, trans_a=True, out_dtype=BF16, tm=512, tn=1024, tk=544)
        big_grads[wu][l] = _matmul(f"{tag}_dwu", dup, xn, trans_a=True, out_dtype=BF16, tm=512, tn=1024, tk=544)
        dxn = _matmul(f"{tag}_dxg", dgate, full[wg][l], tm=544, tn=1024, tk=512)
        dxn = _matmul(f"{tag}_dxu", dup, full[wu][l], residual=dxn, tm=544, tn=1024, tk=512)
        dh_in, dg = _rmsnorm_bwd(f"{tag}_dnorm", dxn, h_in, norm[l][None], dh_out)
        return dh_in, dg[0]

    for l in reversed(range(depth)):
        st = saved[l]
        dh, small["ffn2_norm"][l] = ffn_bwd("ffn2", l, dh, st["h2"], st["ffn2"], ffn2_norm,
                                             "ffn2_w_gate", "ffn2_w_up", "ffn2_w_down")
        xn, u, gq, gk, gsq, gsk, bias, qn, kn, fv, sqn, skd, svd, c, ct, o_fox, lse, o_swa, o_cat = st["mix"]
        dcat = _matmul("mix_dcat", dh, full["w_out"][l], trans_b=True, tm=544, tn=1024, tk=2048)
        big_grads["w_out"][l] = _matmul("mix_dwout", o_cat, dh, trans_a=True, out_dtype=BF16, tm=512, tn=1024, tk=544)
        do_fox, do_swa, dgf, dgs = _out_norm_bwd("out_norm_bwd", cfg, dcat, o_fox, o_swa,
                                                 fox_out_norm[l][None], swa_out_norm[l][None])
        small["fox_out_norm"][l], small["swa_out_norm"][l] = dgf[0], dgs[0]
        dqn, dkn, dfv, dct, dcq = _fox_bwd("fox_bwd", cfg, qn, kn, fv, c, ct, o_fox, lse, do_fox)
        dsqn, dskd, dsvd, dsink = _swa_bwd("swa_bwd", cfg, sqn, skd, svd, swa_sinks[l], slopes, o_swa, do_swa)
        small["swa_sinks"][l] = dsink.reshape(hs // 2, 8, LANES)[:, 0, ::HEAD_DIM].reshape(hs)
        du, dgq, dgk, dgsq, dgsk, db = _mix_prep_bwd("mix_prep_bwd", cfg, u, gq, gk, gsq, gsk, bias,
                                                     dqn, dkn, dfv, dsqn, dskd, dsvd, dct, dcq)
        small["fox_q_norm"][l], small["fox_k_norm"][l] = _fold_pair(dgq), _fold_pair(dgk)
        small["swa_q_norm"][l], small["swa_k_norm"][l] = _fold_pair(dgsq), _fold_pair(dgsk)
        idx = np.array([(hh // 2) * cfg.cs + (hh % 2) for hh in range(hf)])
        small["b_forget"][l] = db[0][idx]
        dwp = _matmul("mix_dwin", du, xn, trans_a=True, out_dtype=BF16, tm=640, tn=1024, tk=544)
        big_grads["w_in"][l] = _unpermute_dw_in(cfg, dwp)
        dxn = _matmul("mix_dxn", du, w_in_p[l], tm=544, tn=1024, tk=640)
        dh, dg = _rmsnorm_bwd("mix_dnorm", dxn, st["h1"], mix_norm[l][None], dh)
        small["mix_norm"][l] = dg[0]
        dh, small["ffn1_norm"][l] = ffn_bwd("ffn1", l, dh, st["h0"], st["ffn1"], ffn1_norm,
                                             "ffn1_w_gate", "ffn1_w_up", "ffn1_w_down")

    grad_x = dh[BLOCK:][None]
    dmeta = dh[cfg.pad:BLOCK]

    stacked = []
    for k in big:
        g = jnp.stack(big_grads[k])
        stacked.append(g.reshape(depth, 4, 2, -1, D))
    core_s = jnp.reshape(c_idx, (1,)).astype(jnp.int32)
    chip_s = jnp.reshape(chip_idx, (1,)).astype(jnp.int32)
    landed = _send_to_sibling("grads_to_sibling", stacked)
    parts = [_pair_add("grads_pair_add", g, la, core_s) for g, la in zip(stacked, landed)]
    landed2 = _send_to_chips("grads_to_chips", parts)
    grads = {}
    for k, p, lb in zip(big, parts, landed2):
        g = _chip_sum("grads_chip_sum", p, lb, chip_s)
        grads[k] = jnp.swapaxes(g, 1, 2) if k in col_sharded else g

    small_names = list(small)
    pieces = [loss_blk[0, :1], dmeta.reshape(-1)] + [jnp.stack(small[k]).reshape(-1) for k in small_names]
    sizes = [int(p.shape[0]) for p in pieces]
    total = sum(sizes)
    padded = -(-total // (8 * LANES)) * (8 * LANES)
    vec = jnp.concatenate(pieces + [jnp.zeros((padded - total,), F32)]).reshape(-1, LANES)
    summed = _gather_sum("small_gather_sum", vec).reshape(-1)
    offs = np.cumsum([0] + sizes)
    loss = summed[0]
    dmeta_full = summed[offs[1]:offs[2]].reshape(n_meta, D)
    mcols = meta_tokens.shape[1]
    grads["meta_tokens"] = lax.dynamic_slice_in_dim(dmeta_full, dev_idx * mcols, mcols, axis=1)
    for n_, k in enumerate(small_names):
        grads[k] = summed[offs[2 + n_]:offs[3 + n_]].reshape(weights[k].shape)

    delta, new_m, new_v = {}, {}, {}
    for k in names:
        delta[k], new_m[k], new_v[k] = _adamw_nd("adamw", grads[k], weights[k], mom_m[k], mom_v[k])

    return (loss, grad_x, *[grads[k] for k in names], *[delta[k] for k in names],
            *[new_m[k] for k in names], *[new_v[k] for k in names])
```

```python
import functools

import numpy as np
import jax
import jax.numpy as jnp
from jax import lax
from jax.experimental import pallas as pl
from jax.experimental.pallas import tpu as pltpu

F32 = jnp.float32
BF16 = jnp.bfloat16
MESH = pl.DeviceIdType.MESH

HEAD_DIM = 64
BLOCK = 128
LANES = 128
N_DEV = 8
EPS = 1e-6
NEG_INF = -1e30
SCALE = HEAD_DIM ** -0.5

ADAM_LR = 0.001
ADAM_B1 = 0.9
ADAM_B2 = 0.999
ADAM_EPS = 1e-08
ADAM_WD = 0.01
ADAM_STEP = 10

VMEM_BYTES_V7X = 64 * 1024 * 1024
VMEM_LIMIT = VMEM_BYTES_V7X * 3 // 4

NT = (((1,), (1,)), ((), ()))
TN = (((0,), (0,)), ((), ()))
HI = lax.Precision.HIGHEST


def _cparams(sem=None, vmem=VMEM_LIMIT):
    return pltpu.CompilerParams(dimension_semantics=sem, vmem_limit_bytes=vmem)


def _tile(n, pref, mult):
    best = None
    for t in range(mult, min(n, pref) + 1, mult):
        if n % t == 0:
            best = t
    return best if best is not None else n


def _dot(a, b):
    return jnp.dot(a, b, preferred_element_type=F32)


def _dot_nt(a, b):
    return lax.dot_general(a, b, NT, preferred_element_type=F32)


def _dot_tn(a, b):
    return lax.dot_general(a, b, TN, preferred_element_type=F32)


def _lane(shape):
    return lax.broadcasted_iota(jnp.int32, shape, len(shape) - 1)


def _half_sum(x, lo):
    s0 = jnp.sum(jnp.where(lo, x, 0.0), axis=1, keepdims=True)
    s1 = jnp.sum(jnp.where(lo, 0.0, x), axis=1, keepdims=True)
    return jnp.where(lo, s0, s1)


def _sigmoid(x):
    return 1.0 / (1.0 + jnp.exp(-x))


def _place():
    return lax.axis_index("x"), lax.axis_index("y"), lax.axis_index("c")


def _peers(x, y, c):
    return [(x, y, 1 - c), (1 - x, y, c), (x, 1 - y, c), (1 - x, 1 - y, c)]


def _dev_index(dev):
    px, py, pc = dev
    return 4 * px + 2 * py + pc


class _Carry:
    def __init__(self):
        self.ins, self.outs, self.alias, self.items = [], [], {}, []
        self.nsem = self.nloc = 0

    def add(self, ins, outs, alias, nsem, nloc, build):
        i0, o0 = len(self.ins), len(self.outs)
        for src, dst in alias.items():
            self.alias[i0 + src] = o0 + dst
        self.items.append((i0, len(ins), o0, len(outs), self.nsem, self.nloc, build))
        self.ins += ins
        self.outs += outs
        self.nsem += nsem
        self.nloc += nloc
        return list(range(o0, o0 + len(outs)))

    def build(self, in_refs, out_refs, ssem, rsem, lsem):
        ops = []
        for i0, ni, o0, no, s0, l0, fn in self.items:
            ops.append(fn(in_refs[i0:i0 + ni], out_refs[o0:o0 + no],
                          lambda k, s0=s0: (ssem.at[s0 + k], rsem.at[s0 + k]), lambda k, l0=l0: lsem.at[l0 + k]))
        return ops


def _remote(src, dst, sems, dev):
    return pltpu.make_async_remote_copy(src_ref=src, dst_ref=dst, send_sem=sems[0], recv_sem=sems[1],
                                        device_id=dev, device_id_type=MESH)


def _gather_first(carry, shard):
    def build(ins, outs, sems, locs):
        src, buf = ins[0], outs[0]
        x, y, c = _place()
        me = _dev_index((x, y, c))
        peers = _peers(x, y, c)
        local = pltpu.make_async_copy(src, buf.at[me], locs(0))
        sends = [_remote(src, buf.at[me], sems(k), dev) for k, dev in enumerate(peers)]
        recvs = [_remote(src, buf.at[_dev_index(dev)], sems(k), dev) for k, dev in enumerate(peers)]

        def start():
            local.start()
            for cp in sends:
                cp.start()

        def wait():
            for cp in sends:
                cp.wait_send()
            for cp in recvs:
                cp.wait_recv()
            local.wait()

        return start, wait

    return carry.add([shard], [jax.ShapeDtypeStruct((N_DEV,) + shard.shape, shard.dtype)], {}, 4, 1, build)[0]


def _gather_second(carry, buf):
    def build(ins, outs, sems, locs):
        b = outs[0]
        x, y, c = _place()
        chips = _peers(x, y, c)[1:]
        sends = [_remote(b.at[_dev_index(dev)], b.at[_dev_index(dev)], sems(k), (x, y, 1 - c))
                 for k, dev in enumerate(chips)]
        recvs = [_remote(b.at[_dev_index(dev)], b.at[_dev_index((dev[0], dev[1], 1 - c))], sems(k), (x, y, 1 - c))
                 for k, dev in enumerate(chips)]

        def start():
            for cp in sends:
                cp.start()

        def wait():
            for cp in sends:
                cp.wait_send()
            for cp in recvs:
                cp.wait_recv()

        return start, wait

    return carry.add([buf], [jax.ShapeDtypeStruct(buf.shape, buf.dtype)], {0: 0}, 3, 0, build)[0]


def _scatter_sibling(carry, grad):
    def build(ins, outs, sems, locs):
        x, y, c = _place()
        cp = _remote(ins[0].at[:, 1 - c], outs[0], sems(0), (x, y, 1 - c))
        return cp.start, cp.wait

    shape = (grad.shape[0],) + grad.shape[2:]
    return carry.add([grad], [jax.ShapeDtypeStruct(shape, grad.dtype)], {}, 1, 0, build)[0]


def _scatter_chips(carry, parts, landing, layer):
    def build(ins, outs, sems, locs):
        x, y, c = _place()
        cps = [_remote(ins[0].at[layer, 2 * dev[0] + dev[1]], outs[0].at[k, layer], sems(k), dev)
               for k, dev in enumerate(_peers(x, y, c)[1:])]

        def start():
            for cp in cps:
                cp.start()

        def wait():
            for cp in cps:
                cp.wait()

        return start, wait

    return carry.add([parts, landing], [jax.ShapeDtypeStruct(landing.shape, landing.dtype)], {1: 0}, 3, 0, build)[0]


def _call(name, body, grid, in_specs, out_specs, out_shape, args, scratch=(), carry=None):
    ni, no, ns = len(args), len(out_shape), len(scratch)
    if carry is None or not carry.items:
        res = pl.pallas_call(
            body, name=name, grid=grid, in_specs=list(in_specs), out_specs=list(out_specs), out_shape=list(out_shape),
            scratch_shapes=list(scratch), compiler_params=_cparams(("arbitrary",) * len(grid)))(*args)
        return list(res), []
    nci, nco = len(carry.ins), len(carry.outs)

    def full_body(*refs):
        c_in = refs[ni:ni + nci]
        c_out = refs[ni + nci + no:ni + nci + no + nco]
        sc = refs[ni + nci + no + nco:]
        ops = carry.build(c_in, c_out, sc[ns], sc[ns + 1], sc[ns + 2])
        first = last = None
        for d, n in enumerate(grid):
            pid = pl.program_id(d)
            first = (pid == 0) if first is None else first & (pid == 0)
            last = (pid == n - 1) if last is None else last & (pid == n - 1)

        @pl.when(first)
        def _():
            for start, _w in ops:
                start()

        body(*refs[:ni], *refs[ni + nci:ni + nci + no], *sc[:ns])

        @pl.when(last)
        def _():
            for _s, wait in ops:
                wait()

    hbm = pl.BlockSpec(memory_space=pl.ANY)
    res = pl.pallas_call(
        full_body, name=name, grid=grid,
        in_specs=list(in_specs) + [hbm] * nci, out_specs=list(out_specs) + [hbm] * nco,
        out_shape=list(out_shape) + list(carry.outs),
        input_output_aliases={ni + s: no + d for s, d in carry.alias.items()},
        scratch_shapes=list(scratch) + [pltpu.SemaphoreType.DMA((carry.nsem,)), pltpu.SemaphoreType.DMA((carry.nsem,)),
                                        pltpu.SemaphoreType.DMA((max(carry.nloc, 1),))],
        compiler_params=_cparams(("arbitrary",) * len(grid)))(*args, *carry.ins)
    return list(res[:no]), list(res[no:])


def _comm_only(name, carry):
    return _call(name, lambda *refs: None, (1,), [], [], [], [], carry=carry)[1]


def _matmul(name, a, b, *, pair2=None, trans_a=False, trans_b=False, out_dtype=F32, scale=None, residual=None,
            tm=512, tn=512, tk=512, carry=None):
    if trans_a:
        K, M = a.shape
    else:
        M, K = a.shape
    if trans_b:
        N, Kb = b.shape
    else:
        Kb, N = b.shape
    assert K == Kb, (name, a.shape, b.shape)
    tm = _tile(M, tm, LANES if trans_a else 16)
    tn = _tile(N, tn, LANES)
    tk = _tile(K, tk, 16 if (trans_a and not trans_b) else LANES)
    nk = K // tk
    dims = (((0 if trans_a else 1,), (1 if trans_b else 0,)), ((), ()))
    pairs = [(a, b)] + ([pair2] if pair2 is not None else [])
    npair = len(pairs)

    def body(*refs):
        ab = refs[:2 * npair]
        pos = 2 * npair
        r_ref = None
        if residual is not None:
            r_ref = refs[pos]
            pos += 1
        o_ref = refs[pos]

        def partial():
            t = None
            for q in range(npair):
                d = lax.dot_general(ab[2 * q][...].astype(BF16), ab[2 * q + 1][...].astype(BF16), dims,
                                    preferred_element_type=F32)
                t = d if t is None else t + d
            return t

        def finish(r):
            if scale is not None:
                r = r * scale
            if r_ref is not None:
                r = r + r_ref[...].astype(F32)
            o_ref[...] = r.astype(o_ref.dtype)

        if nk == 1:
            finish(partial())
            return
        acc_ref = refs[pos + 1]
        k = pl.program_id(2)

        @pl.when(k == 0)
        def _():
            acc_ref[...] = partial()

        @pl.when(k > 0)
        def _():
            acc_ref[...] += partial()

        @pl.when(k == nk - 1)
        def _():
            finish(acc_ref[...])

    a_spec = pl.BlockSpec((tk, tm), lambda i, j, k: (k, i)) if trans_a else pl.BlockSpec((tm, tk), lambda i, j, k: (i, k))
    b_spec = pl.BlockSpec((tn, tk), lambda i, j, k: (j, k)) if trans_b else pl.BlockSpec((tk, tn), lambda i, j, k: (k, j))
    in_specs, args = [], []
    for pa, pb in pairs:
        in_specs += [a_spec, b_spec]
        args += [pa, pb]
    if residual is not None:
        in_specs.append(pl.BlockSpec((tm, tn), lambda i, j, k: (i, j)))
        args.append(residual)
    res, extra = _call(
        name, body, (M // tm, N // tn, nk), in_specs, [pl.BlockSpec((tm, tn), lambda i, j, k: (i, j))],
        [jax.ShapeDtypeStruct((M, N), out_dtype)], args,
        scratch=[pltpu.VMEM((tm, tn), F32)] if nk > 1 else [], carry=carry)
    return res[0], extra


def _rmsnorm_fwd(name, h, g):
    L, D = h.shape
    tm = _tile(L, 256, 16)

    def body(h_ref, g_ref, o_ref):
        x = h_ref[...]
        r = lax.rsqrt(jnp.mean(x * x, axis=1, keepdims=True) + EPS)
        o_ref[...] = (x * r * g_ref[...]).astype(o_ref.dtype)

    return pl.pallas_call(
        body, name=name, grid=(L // tm,),
        in_specs=[pl.BlockSpec((tm, D), lambda i: (i, 0)), pl.BlockSpec((1, D), lambda i: (0, 0))],
        out_specs=pl.BlockSpec((tm, D), lambda i: (i, 0)),
        out_shape=jax.ShapeDtypeStruct((L, D), BF16),
        compiler_params=_cparams(("parallel",)),
    )(h, g)


def _rmsnorm_bwd(name, dy, h, g, dres):
    L, D = h.shape
    tm = _tile(L, 256, 16)

    def body(dy_ref, h_ref, g_ref, dres_ref, dh_ref, dh16_ref, dg_ref):
        i = pl.program_id(0)
        x = h_ref[...]
        dyv = dy_ref[...].astype(F32)
        r = lax.rsqrt(jnp.mean(x * x, axis=1, keepdims=True) + EPS)
        w = dyv * g_ref[...]
        proj = jnp.sum(w * x, axis=1, keepdims=True) * (1.0 / D)
        dh = dres_ref[...] + r * w - x * (r * r * r * proj)
        dh_ref[...] = dh
        dh16_ref[...] = dh.astype(BF16)

        @pl.when(i == 0)
        def _():
            dg_ref[...] = jnp.zeros_like(dg_ref)

        dg_ref[...] += jnp.sum(dyv * x * r, axis=0, keepdims=True)

    return pl.pallas_call(
        body, name=name, grid=(L // tm,),
        in_specs=[pl.BlockSpec((tm, D), lambda i: (i, 0)), pl.BlockSpec((tm, D), lambda i: (i, 0)),
                  pl.BlockSpec((1, D), lambda i: (0, 0)), pl.BlockSpec((tm, D), lambda i: (i, 0))],
        out_specs=[pl.BlockSpec((tm, D), lambda i: (i, 0)), pl.BlockSpec((tm, D), lambda i: (i, 0)),
                   pl.BlockSpec((1, D), lambda i: (0, 0))],
        out_shape=[jax.ShapeDtypeStruct((L, D), F32), jax.ShapeDtypeStruct((L, D), BF16),
                   jax.ShapeDtypeStruct((1, D), F32)],
        compiler_params=_cparams(("arbitrary",)),
    )(dy, h, g, dres)


def _ffn_up(name, xn, wgT, wuT, carry=None):
    L, D = xn.shape
    F = wgT.shape[0]
    tm = _tile(L, 544, 16)
    tn = _tile(F, 512, LANES)

    def body(x_ref, wg_ref, wu_ref, g_ref, u_ref, a_ref):
        x = x_ref[...]
        g = _dot_nt(x, wg_ref[...])
        u = _dot_nt(x, wu_ref[...])
        g_ref[...] = g.astype(BF16)
        u_ref[...] = u.astype(BF16)
        a_ref[...] = (g * _sigmoid(g) * u).astype(BF16)

    o_spec = pl.BlockSpec((tm, tn), lambda i, j: (i, j))
    o_shape = jax.ShapeDtypeStruct((L, F), BF16)
    return _call(
        name, body, (L // tm, F // tn),
        [pl.BlockSpec((tm, D), lambda i, j: (i, 0)), pl.BlockSpec((tn, D), lambda i, j: (j, 0)),
         pl.BlockSpec((tn, D), lambda i, j: (j, 0))],
        [o_spec, o_spec, o_spec], [o_shape, o_shape, o_shape], [xn, wgT, wuT], carry=carry)


def _ffn_bwd_act(name, dh, wd, gate, up, carry=None):
    L, D = dh.shape
    F = wd.shape[0]
    tm = _tile(L, 544, 16)
    tn = _tile(F, 512, LANES)

    def body(dh_ref, wd_ref, g_ref, u_ref, dg_ref, du_ref, a_ref):
        da = 0.5 * _dot_nt(dh_ref[...].astype(BF16), wd_ref[...])
        g = g_ref[...].astype(F32)
        u = u_ref[...].astype(F32)
        sg = _sigmoid(g)
        silu = g * sg
        dg_ref[...] = (da * u * (sg * (1.0 + g * (1.0 - sg)))).astype(BF16)
        du_ref[...] = (da * silu).astype(BF16)
        a_ref[...] = (silu * u).astype(BF16)

    o_spec = pl.BlockSpec((tm, tn), lambda i, j: (i, j))
    o_shape = jax.ShapeDtypeStruct((L, F), BF16)
    return _call(
        name, body, (L // tm, F // tn),
        [pl.BlockSpec((tm, D), lambda i, j: (i, 0)), pl.BlockSpec((tn, D), lambda i, j: (j, 0)), o_spec, o_spec],
        [o_spec, o_spec, o_spec], [o_shape, o_shape, o_shape], [dh, wd, gate, up], carry=carry)


class _Cfg:
    def __init__(self, S, D, hf, hs, hkv):
        self.S, self.D, self.L = S, D, S + BLOCK
        self.hf, self.hs, self.hkv = hf, hs, hkv
        self.wf, self.ws = hf * HEAD_DIM, hs * HEAD_DIM
        self.group = hs // hkv
        self.cs = 2 * LANES // hf
        self.n_meta = 16
        self.pad = BLOCK - self.n_meta
        self.o_fk = self.wf
        self.o_fv = 2 * self.wf
        self.o_sq = 3 * self.wf
        self.o_sk = self.o_sq + self.ws
        self.o_sv = self.o_sk + LANES
        self.o_fz = self.o_sv + LANES
        self.up = self.o_fz + LANES
        assert hkv == 2 and hf % 2 == 0 and self.group % 2 == 0 and self.cs % 8 == 0
        assert self.o_sq % self.ws == 0 and (2 * self.wf) % LANES == 0


def _head_norm(x, gain, lo, mult):
    r = lax.rsqrt(_half_sum(x * x, lo) * (1.0 / HEAD_DIM) + EPS)
    return x * r * (gain * mult)


def _head_norm_bwd(dy, x, gain, lo, mult):
    r = lax.rsqrt(_half_sum(x * x, lo) * (1.0 / HEAD_DIM) + EPS)
    w = dy * (gain * mult)
    proj = _half_sum(w * x, lo) * (1.0 / HEAD_DIM)
    dx = r * w - x * (r * r * r * proj)
    dgain = jnp.sum(dy * mult * x * r, axis=0, keepdims=True)
    return dx, dgain


def _dup(x, lo):
    xr = pltpu.roll(x, 64, 1)
    return jnp.where(lo, x, xr), jnp.where(lo, xr, x)


def _mix_prep(name, cfg, u, gq, gk, gsq, gsk, bias):
    L, wf, ws = cfg.L, cfg.wf, cfg.ws
    T = BLOCK
    npf, nps = wf // LANES, ws // LANES

    def body(fqk_ref, fv_ref, sq_ref, sk_ref, sv_ref, fz_ref, gq_ref, gk_ref, gsq_ref, gsk_ref, b_ref,
             qn_ref, kn_ref, fvo_ref, sqn_ref, skd_ref, svd_ref, c_ref, ct_ref, carry_ref):
        i = pl.program_id(0)
        lo = _lane((T, LANES)) < HEAD_DIM
        for p in range(npf):
            sl = slice(p * LANES, (p + 1) * LANES)
            qn_ref[:, sl] = _head_norm(fqk_ref[:, sl], gq_ref[...], lo, SCALE).astype(BF16)
            kn_ref[:, sl] = _head_norm(fqk_ref[:, wf + p * LANES: wf + (p + 1) * LANES], gk_ref[...], lo, 1.0).astype(BF16)
        fvo_ref[...] = fv_ref[...].astype(BF16)
        for p in range(nps):
            sl = slice(p * LANES, (p + 1) * LANES)
            sqn_ref[:, sl] = _head_norm(sq_ref[:, sl], gsq_ref[...], lo, SCALE).astype(BF16)
        k0, k1 = _dup(_head_norm(sk_ref[...], gsk_ref[...], lo, 1.0), lo)
        skd_ref[:, :LANES] = k0.astype(BF16)
        skd_ref[:, LANES:] = k1.astype(BF16)
        v0, v1 = _dup(sv_ref[...], lo)
        svd_ref[:, :LANES] = v0.astype(BF16)
        svd_ref[:, LANES:] = v1.astype(BF16)

        @pl.when(i == 0)
        def _():
            carry_ref[...] = jnp.zeros_like(carry_ref)

        z = fz_ref[...] + b_ref[...]
        lf = jnp.minimum(z, 0.0) - jnp.log(1.0 + jnp.exp(-jnp.abs(z)))
        row = lax.broadcasted_iota(jnp.int32, (T, T), 0)
        col = lax.broadcasted_iota(jnp.int32, (T, T), 1)
        tri = jnp.where(col <= row, 1.0, 0.0).astype(F32)
        c = jnp.dot(tri, lf, precision=HI, preferred_element_type=F32) + carry_ref[0:1, :]
        c_ref[...] = c
        ct_ref[...] = c.T
        carry_ref[0:1, :] = c_ref[T - 1:T, :]

    def rows(w, cb):
        return pl.BlockSpec((T, w), lambda i, cb=cb: (i, cb))

    vec = pl.BlockSpec((1, LANES), lambda i: (0, 0))
    return pl.pallas_call(
        body, name=name, grid=(L // T,),
        in_specs=[rows(2 * wf, 0), rows(wf, 2), rows(ws, cfg.o_sq // ws), rows(LANES, cfg.o_sk // LANES),
                  rows(LANES, cfg.o_sv // LANES), rows(LANES, cfg.o_fz // LANES), vec, vec, vec, vec, vec],
        out_specs=[rows(wf, 0), rows(wf, 0), rows(wf, 0), rows(ws, 0), rows(2 * LANES, 0), rows(2 * LANES, 0),
                   rows(LANES, 0), pl.BlockSpec((LANES, T), lambda i: (0, i))],
        out_shape=[jax.ShapeDtypeStruct((L, wf), BF16)] * 3 + [jax.ShapeDtypeStruct((L, ws), BF16)]
        + [jax.ShapeDtypeStruct((L, 2 * LANES), BF16)] * 2
        + [jax.ShapeDtypeStruct((L, LANES), F32), jax.ShapeDtypeStruct((LANES, L), F32)],
        scratch_shapes=[pltpu.VMEM((8, LANES), F32)],
        compiler_params=_cparams(("arbitrary",)),
    )(u, u, u, u, u, u, gq, gk, gsq, gsk, bias)


def _mix_prep_bwd(name, cfg, u, gq, gk, gsq, gsk, bias, dqn, dkn, dfv, dsqn, dskd, dsvd, dct, dcq):
    L, wf, ws = cfg.L, cfg.wf, cfg.ws
    T = BLOCK
    nb = L // T
    npf, nps = wf // LANES, ws // LANES

    def body(fqk_ref, sq_ref, sk_ref, fz_ref, gq_ref, gk_ref, gsq_ref, gsk_ref, b_ref,
             dqn_ref, dkn_ref, dfv_ref, dsqn_ref, dskd_ref, dsvd_ref, dct_ref, dcq_ref,
             du_ref, dgq_ref, dgk_ref, dgsq_ref, dgsk_ref, db_ref, carry_ref):
        i = pl.program_id(0)
        lo = _lane((T, LANES)) < HEAD_DIM

        @pl.when(i == 0)
        def _():
            carry_ref[...] = jnp.zeros_like(carry_ref)
            for r in (dgq_ref, dgk_ref, dgsq_ref, dgsk_ref, db_ref):
                r[...] = jnp.zeros_like(r)

        accq = jnp.zeros((1, LANES), F32)
        acck = jnp.zeros((1, LANES), F32)
        for p in range(npf):
            sl = slice(p * LANES, (p + 1) * LANES)
            dx, dg = _head_norm_bwd(dqn_ref[:, sl], fqk_ref[:, sl], gq_ref[...], lo, SCALE)
            du_ref[:, sl] = dx.astype(BF16)
            accq = accq + dg
            slk = slice(wf + p * LANES, wf + (p + 1) * LANES)
            dx, dg = _head_norm_bwd(dkn_ref[:, sl], fqk_ref[:, slk], gk_ref[...], lo, 1.0)
            du_ref[:, slk] = dx.astype(BF16)
            acck = acck + dg
        dgq_ref[...] += accq
        dgk_ref[...] += acck
        du_ref[:, cfg.o_fv:cfg.o_fv + wf] = dfv_ref[...].astype(BF16)
        accs = jnp.zeros((1, LANES), F32)
        for p in range(nps):
            sl = slice(p * LANES, (p + 1) * LANES)
            dx, dg = _head_norm_bwd(dsqn_ref[:, sl], sq_ref[:, sl], gsq_ref[...], lo, SCALE)
            du_ref[:, cfg.o_sq + p * LANES: cfg.o_sq + (p + 1) * LANES] = dx.astype(BF16)
            accs = accs + dg
        dgsq_ref[...] += accs

        def fold(ref):
            a0, a1 = ref[:, :LANES], ref[:, LANES:]
            return jnp.where(lo, a0 + pltpu.roll(a0, 64, 1), a1 + pltpu.roll(a1, 64, 1))

        dx, dg = _head_norm_bwd(fold(dskd_ref), sk_ref[...], gsk_ref[...], lo, 1.0)
        du_ref[:, cfg.o_sk:cfg.o_sk + LANES] = dx.astype(BF16)
        dgsk_ref[...] += dg
        du_ref[:, cfg.o_sv:cfg.o_sv + LANES] = fold(dsvd_ref).astype(BF16)

        dc = dct_ref[...].T + dcq_ref[...]
        row = lax.broadcasted_iota(jnp.int32, (T, T), 0)
        col = lax.broadcasted_iota(jnp.int32, (T, T), 1)
        triu = jnp.where(col >= row, 1.0, 0.0).astype(F32)
        dlf = jnp.dot(triu, dc, precision=HI, preferred_element_type=F32) + carry_ref[0:1, :]
        carry_ref[0:1, :] = dlf[0:1, :]
        z = fz_ref[...] + b_ref[...]
        dz = dlf * _sigmoid(-z)
        du_ref[:, cfg.o_fz:cfg.o_fz + LANES] = dz.astype(BF16)
        db_ref[...] += jnp.sum(dz, axis=0, keepdims=True)

    def rows(w, cb):
        return pl.BlockSpec((T, w), lambda i, cb=cb: (nb - 1 - i, cb))

    vec = pl.BlockSpec((1, LANES), lambda i: (0, 0))
    vshape = jax.ShapeDtypeStruct((1, LANES), F32)
    return pl.pallas_call(
        body, name=name, grid=(nb,),
        in_specs=[rows(2 * wf, 0), rows(ws, cfg.o_sq // ws), rows(LANES, cfg.o_sk // LANES),
                  rows(LANES, cfg.o_fz // LANES), vec, vec, vec, vec, vec,
                  rows(wf, 0), rows(wf, 0), rows(wf, 0), rows(ws, 0), rows(2 * LANES, 0), rows(2 * LANES, 0),
                  pl.BlockSpec((LANES, T), lambda i: (0, nb - 1 - i)), rows(LANES, 0)],
        out_specs=[rows(cfg.up, 0), vec, vec, vec, vec, vec],
        out_shape=[jax.ShapeDtypeStruct((L, cfg.up), BF16)] + [vshape] * 5,
        scratch_shapes=[pltpu.VMEM((8, LANES), F32)],
        compiler_params=_cparams(("arbitrary",)),
    )(u, u, u, u, gq, gk, gsq, gsk, bias, dqn, dkn, dfv, dsqn, dskd, dsvd, dct, dcq)


def _fox_fwd(name, cfg, qn, kn, fv, c, ct, carry=None):
    L, wf, cs = cfg.L, cfg.wf, cfg.cs
    T = BLOCK
    TQ = _tile(L, 544, 8)
    npairs = wf // LANES
    pad = cfg.pad

    def body(q_ref, k_ref, v_ref, c_ref, ct_ref, o_ref, lse_ref):
        p = pl.program_id(0)
        i = pl.program_id(1)
        lane = _lane((TQ, LANES))
        lo = lane < HEAD_DIM
        q = q_ref[...]
        qh = (jnp.where(lo, q, jnp.zeros_like(q)), jnp.where(lo, jnp.zeros_like(q), q))
        cblk = c_ref[...]
        cq = tuple(jnp.sum(jnp.where(lane == p * cs + hh, cblk, 0.0), axis=1, keepdims=True) for hh in range(2))
        qpos = i * TQ + lax.broadcasted_iota(jnp.int32, (TQ, T), 0)
        lo_k = _lane((T, LANES)) < HEAD_DIM

        def step(j, carry_):
            m0, m1, l0, l1, acc = carry_
            ms, ls = [m0, m1], [l0, l1]
            off = pl.multiple_of(j * T, T)
            k = k_ref[pl.ds(off, T), :]
            v = v_ref[pl.ds(off, T), :]
            kpos = j * T + lax.broadcasted_iota(jnp.int32, (TQ, T), 1)
            allowed = (kpos <= qpos) & (kpos >= pad)
            for hh in range(2):
                ck = ct_ref[hh:hh + 1, pl.ds(off, T)]
                s = _dot_nt(qh[hh], k) + cq[hh] - ck
                s = jnp.where(allowed, s, NEG_INF)
                m_new = jnp.maximum(ms[hh], jnp.max(s, axis=1, keepdims=True))
                alpha = jnp.exp(ms[hh] - m_new)
                pr = jnp.exp(s - m_new)
                ls[hh] = alpha * ls[hh] + jnp.sum(pr, axis=1, keepdims=True)
                ms[hh] = m_new
                sel = lo if hh == 0 else jnp.logical_not(lo)
                sel_k = lo_k if hh == 0 else jnp.logical_not(lo_k)
                vh = jnp.where(sel_k, v, jnp.zeros_like(v))
                acc = acc * jnp.where(sel, alpha, 1.0) + _dot(pr.astype(BF16), vh)
            return ms[0], ms[1], ls[0], ls[1], acc

        init = (jnp.full((TQ, 1), NEG_INF, F32), jnp.full((TQ, 1), NEG_INF, F32),
                jnp.zeros((TQ, 1), F32), jnp.zeros((TQ, 1), F32), jnp.zeros((TQ, LANES), F32))
        m0, m1, l0, l1, acc = lax.fori_loop(0, ((i + 1) * TQ + T - 1) // T, step, init)
        o_ref[...] = acc / jnp.where(lo, l0, l1)
        lse_ref[...] = jnp.where(lo, m0 + jnp.log(l0), m1 + jnp.log(l1))

    blk = pl.BlockSpec((TQ, LANES), lambda p, i: (i, p))
    full = pl.BlockSpec((L, LANES), lambda p, i: (0, p))
    return _call(
        name, body, (npairs, L // TQ),
        [blk, full, full, pl.BlockSpec((TQ, LANES), lambda p, i: (i, 0)), pl.BlockSpec((cs, L), lambda p, i: (p, 0))],
        [blk, blk], [jax.ShapeDtypeStruct((L, wf), F32)] * 2, [qn, kn, fv, c, ct], carry=carry)


def _fox_bwd(name, cfg, qn, kn, fv, c, ct, o, lse, do, carry=None):
    L, wf, cs = cfg.L, cfg.wf, cfg.cs
    T = BLOCK
    TQ = _tile(L, 544, 8)
    nb = L // T
    nq = L // TQ
    npairs = wf // LANES
    pad = cfg.pad

    def body(q_ref, k_ref, v_ref, c_ref, ct_ref, o_ref, lse_ref, do_ref, dq_ref, dk_ref, dv_ref, dct_ref, dcq_ref):
        p = pl.program_id(0)
        j = pl.program_id(1)
        lane = _lane((TQ, LANES))
        lo = lane < HEAD_DIM
        sels = (lo, jnp.logical_not(lo))
        lo_k = _lane((T, LANES)) < HEAD_DIM
        sels_k = (lo_k, jnp.logical_not(lo_k))

        @pl.when(j == 0)
        def _():
            dq_ref[...] = jnp.zeros_like(dq_ref)

        @pl.when((j == 0) & (p == 0))
        def _():
            dcq_ref[...] = jnp.zeros_like(dcq_ref)

        k = k_ref[...]
        v = v_ref[...]
        kh = tuple(jnp.where(s_, k, jnp.zeros_like(k)) for s_ in sels_k)
        ck = tuple(ct_ref[hh:hh + 1, :] for hh in range(2))
        kpos = j * T + lax.broadcasted_iota(jnp.int32, (TQ, T), 1)

        def step(i, carry_):
            dk, dv, dc0, dc1 = carry_
            dcs = [dc0, dc1]
            off = pl.multiple_of(i * TQ, 8)
            q = q_ref[pl.ds(off, TQ), :]
            dov = do_ref[pl.ds(off, TQ), :]
            dd = dov * o_ref[pl.ds(off, TQ), :]
            lse_b = lse_ref[pl.ds(off, TQ), :]
            cblk = c_ref[pl.ds(off, TQ), :]
            qpos = i * TQ + lax.broadcasted_iota(jnp.int32, (TQ, T), 0)
            allowed = (kpos <= qpos) & (kpos >= pad)
            dq = jnp.zeros((TQ, LANES), F32)
            dcq = jnp.zeros((TQ, LANES), F32)
            for hh in range(2):
                sel = sels[hh]
                qhh = jnp.where(sel, q, jnp.zeros_like(q))
                doh = jnp.where(sel, dov, 0.0).astype(BF16)
                dsum = jnp.sum(jnp.where(sel, dd, 0.0), axis=1, keepdims=True)
                lse_h = jnp.sum(jnp.where(lane == hh * HEAD_DIM, lse_b, 0.0), axis=1, keepdims=True)
                cq = jnp.sum(jnp.where(lane == p * cs + hh, cblk, 0.0), axis=1, keepdims=True)
                s = _dot_nt(qhh, k) + cq - ck[hh]
                pr = jnp.where(allowed, jnp.exp(jnp.where(allowed, s, NEG_INF) - lse_h), 0.0)
                dp = _dot_nt(doh, v)
                ds = pr * (dp - dsum)
                dsb = ds.astype(BF16)
                dv = dv + _dot_tn(pr.astype(BF16), doh)
                dk = dk + _dot_tn(dsb, qhh)
                dq = dq + _dot(dsb, kh[hh])
                dcs[hh] = dcs[hh] - jnp.sum(ds, axis=0, keepdims=True)
                dcq = dcq + jnp.where(lane == p * cs + hh, jnp.sum(ds, axis=1, keepdims=True), 0.0)
            dq_ref[pl.ds(off, TQ), :] += dq
            dcq_ref[pl.ds(off, TQ), :] += dcq
            return dk, dv, dcs[0], dcs[1]

        init = (jnp.zeros((T, LANES), F32), jnp.zeros((T, LANES), F32),
                jnp.zeros((1, T), F32), jnp.zeros((1, T), F32))
        dk, dv, dc0, dc1 = lax.fori_loop((j * T) // TQ, nq, step, init)
        dk_ref[...] = dk
        dv_ref[...] = dv
        dct_ref[...] = jnp.zeros_like(dct_ref)
        dct_ref[0:1, :] = dc0
        dct_ref[1:2, :] = dc1

    blk = pl.BlockSpec((T, LANES), lambda p, j: (j, p))
    full = pl.BlockSpec((L, LANES), lambda p, j: (0, p))
    return _call(
        name, body, (npairs, nb),
        [full, blk, blk, pl.BlockSpec((L, LANES), lambda p, j: (0, 0)), pl.BlockSpec((cs, T), lambda p, j: (p, j)),
         full, full, full],
        [full, blk, blk, pl.BlockSpec((cs, T), lambda p, j: (p, j)), pl.BlockSpec((L, LANES), lambda p, j: (0, 0))],
        [jax.ShapeDtypeStruct((L, wf), F32)] * 3
        + [jax.ShapeDtypeStruct((LANES, L), F32), jax.ShapeDtypeStruct((L, LANES), F32)],
        [qn, kn, fv, c, ct, o, lse, do], carry=carry)


def _swa_scores(qh, kp, kc, slope, sink, i, pad):
    T = BLOCK
    t = lax.broadcasted_iota(jnp.int32, (T, T), 0)
    s_ = lax.broadcasted_iota(jnp.int32, (T, T), 1)
    dist_c = t - s_
    dist_p = dist_c + T
    ok_c = (dist_c >= 0) & (i * T + s_ >= pad)
    ok_p = (dist_p < T) & ((i - 1) * T + s_ >= pad)
    sp = jnp.where(ok_p, _dot_nt(qh, kp) - slope * dist_p.astype(F32), NEG_INF)
    sc = jnp.where(ok_c, _dot_nt(qh, kc) - slope * dist_c.astype(F32), NEG_INF)
    m = jnp.maximum(jnp.maximum(jnp.max(sp, axis=1, keepdims=True), jnp.max(sc, axis=1, keepdims=True)), sink)
    ep = jnp.exp(sp - m)
    ec = jnp.exp(sc - m)
    es = jnp.exp(sink - m)
    den = jnp.sum(ep, axis=1, keepdims=True) + jnp.sum(ec, axis=1, keepdims=True) + es
    return ep / den, ec / den, es / den


def _swa_fwd(name, cfg, sqn, skd, svd, sinks, slopes):
    L, ws, group = cfg.L, cfg.ws, cfg.group
    T = BLOCK
    npairs = ws // LANES
    ppk = group // 2
    pad = cfg.pad

    def body(sink_ref, slope_ref, q_ref, kp_ref, kc_ref, vp_ref, vc_ref, o_ref):
        p = pl.program_id(0)
        i = pl.program_id(1)
        lo = _lane((T, LANES)) < HEAD_DIM
        q = q_ref[...]
        acc = jnp.zeros((T, LANES), F32)
        for hh in range(2):
            sel = lo if hh == 0 else jnp.logical_not(lo)
            qh = jnp.where(sel, q, jnp.zeros_like(q))
            pp, pc, _ = _swa_scores(qh, kp_ref[...], kc_ref[...], slope_ref[2 * p + hh], sink_ref[2 * p + hh], i, pad)
            vp = jnp.where(sel, vp_ref[...], jnp.zeros_like(q))
            vc = jnp.where(sel, vc_ref[...], jnp.zeros_like(q))
            acc = acc + _dot(pp.astype(BF16), vp) + _dot(pc.astype(BF16), vc)
        o_ref[...] = acc

    smem = pl.BlockSpec(memory_space=pltpu.SMEM)
    prev = pl.BlockSpec((T, LANES), lambda p, i: (jnp.maximum(i - 1, 0), p // ppk))
    cur = pl.BlockSpec((T, LANES), lambda p, i: (i, p // ppk))
    blk = pl.BlockSpec((T, LANES), lambda p, i: (i, p))
    return pl.pallas_call(
        body, name=name, grid=(npairs, L // T),
        in_specs=[smem, smem, blk, prev, cur, prev, cur],
        out_specs=blk, out_shape=jax.ShapeDtypeStruct((L, ws), F32),
        compiler_params=_cparams(("parallel", "parallel")),
    )(sinks, slopes, sqn, skd, skd, svd, svd)


def _swa_bwd(name, cfg, sqn, skd, svd, sinks, slopes, o, do):
    L, ws, group, hkv = cfg.L, cfg.ws, cfg.group, cfg.hkv
    T = BLOCK
    gw = group * HEAD_DIM
    ppk = group // 2
    pad = cfg.pad

    def body(sink_ref, slope_ref, q_ref, kp_ref, kc_ref, vp_ref, vc_ref, o_ref, do_ref,
             dq_ref, dk_ref, dv_ref, dsink_ref):
        kv = pl.program_id(0)
        i = pl.program_id(1)
        lane = _lane((T, LANES))
        lo = lane < HEAD_DIM

        @pl.when(i == 0)
        def _():
            dk_ref[...] = jnp.zeros_like(dk_ref)
            dv_ref[...] = jnp.zeros_like(dv_ref)
            dsink_ref[...] = jnp.zeros_like(dsink_ref)

        kp, kc, vp, vc = kp_ref[...], kc_ref[...], vp_ref[...], vc_ref[...]
        dkp = jnp.zeros((T, LANES), F32)
        dkc = jnp.zeros((T, LANES), F32)
        dvp = jnp.zeros((T, LANES), F32)
        dvc = jnp.zeros((T, LANES), F32)
        for pp_ in range(ppk):
            sl = slice(pp_ * LANES, (pp_ + 1) * LANES)
            q = q_ref[:, sl]
            dov = do_ref[:, sl]
            dd = dov * o_ref[:, sl]
            dq = jnp.zeros((T, LANES), F32)
            dsk = jnp.zeros((1, LANES), F32)
            for hh in range(2):
                sel = lo if hh == 0 else jnp.logical_not(lo)
                h = kv * group + 2 * pp_ + hh
                qh = jnp.where(sel, q, jnp.zeros_like(q))
                pp, pc, ps = _swa_scores(qh, kp, kc, slope_ref[h], sink_ref[h], i, pad)
                doh = jnp.where(sel, dov, 0.0).astype(BF16)
                dsum = jnp.sum(jnp.where(sel, dd, 0.0), axis=1, keepdims=True)
                dsp = (pp * (_dot_nt(doh, vp) - dsum)).astype(BF16)
                dsc = (pc * (_dot_nt(doh, vc) - dsum)).astype(BF16)
                khp = jnp.where(sel, kp, jnp.zeros_like(kp))
                khc = jnp.where(sel, kc, jnp.zeros_like(kc))
                dq = dq + _dot(dsp, khp) + _dot(dsc, khc)
                dkp = dkp + _dot_tn(dsp, qh)
                dkc = dkc + _dot_tn(dsc, qh)
                dvp = dvp + _dot_tn(pp.astype(BF16), doh)
                dvc = dvc + _dot_tn(pc.astype(BF16), doh)
                dsk = dsk + jnp.where(lane[0:1, :] == hh * HEAD_DIM, -jnp.sum(ps * dsum), 0.0)
            dq_ref[:, sl] = dq
            dsink_ref[8 * pp_:8 * pp_ + 1, :] += dsk

        cur = pl.multiple_of(i * T, T)
        dk_ref[pl.ds(cur, T), :] += dkc
        dv_ref[pl.ds(cur, T), :] += dvc

        @pl.when(i > 0)
        def _():
            prv = pl.multiple_of((i - 1) * T, T)
            dk_ref[pl.ds(prv, T), :] += dkp
            dv_ref[pl.ds(prv, T), :] += dvp

    smem = pl.BlockSpec(memory_space=pltpu.SMEM)
    prev = pl.BlockSpec((T, LANES), lambda kv, i: (jnp.maximum(i - 1, 0), kv))
    cur = pl.BlockSpec((T, LANES), lambda kv, i: (i, kv))
    qblk = pl.BlockSpec((T, gw), lambda kv, i: (i, kv))
    full = pl.BlockSpec((L, LANES), lambda kv, i: (0, kv))
    return pl.pallas_call(
        body, name=name, grid=(hkv, L // T),
        in_specs=[smem, smem, qblk, prev, cur, prev, cur, qblk, qblk],
        out_specs=[qblk, full, full, pl.BlockSpec((8 * ppk, LANES), lambda kv, i: (kv, 0))],
        out_shape=[jax.ShapeDtypeStruct((L, ws), F32), jax.ShapeDtypeStruct((L, 2 * LANES), F32),
                   jax.ShapeDtypeStruct((L, 2 * LANES), F32), jax.ShapeDtypeStruct((8 * ppk * hkv, LANES), F32)],
        compiler_params=_cparams(("parallel", "arbitrary")),
    )(sinks, slopes, sqn, skd, skd, svd, svd, o, do)


def _out_norm(name, cfg, o_fox, o_swa, g_fox, g_swa):
    L, wf, ws = cfg.L, cfg.wf, cfg.ws
    tm = _tile(L, 256, 16)

    def body(of_ref, os_ref, gf_ref, gs_ref, o_ref):
        for src, g_ref, lo_, w in ((of_ref, gf_ref, 0, wf), (os_ref, gs_ref, wf, ws)):
            x = src[...]
            r = lax.rsqrt(jnp.mean(x * x, axis=1, keepdims=True) + EPS)
            o_ref[:, lo_:lo_ + w] = (x * r * g_ref[...]).astype(BF16)

    return pl.pallas_call(
        body, name=name, grid=(L // tm,),
        in_specs=[pl.BlockSpec((tm, wf), lambda i: (i, 0)), pl.BlockSpec((tm, ws), lambda i: (i, 0)),
                  pl.BlockSpec((1, wf), lambda i: (0, 0)), pl.BlockSpec((1, ws), lambda i: (0, 0))],
        out_specs=pl.BlockSpec((tm, wf + ws), lambda i: (i, 0)),
        out_shape=jax.ShapeDtypeStruct((L, wf + ws), BF16),
        compiler_params=_cparams(("parallel",)),
    )(o_fox, o_swa, g_fox, g_swa)


def _out_norm_bwd(name, cfg, dcat, o_fox, o_swa, g_fox, g_swa):
    L, wf, ws = cfg.L, cfg.wf, cfg.ws
    tm = _tile(L, 256, 16)

    def body(d_ref, of_ref, os_ref, gf_ref, gs_ref, dof_ref, dos_ref, dgf_ref, dgs_ref):
        i = pl.program_id(0)

        @pl.when(i == 0)
        def _():
            dgf_ref[...] = jnp.zeros_like(dgf_ref)
            dgs_ref[...] = jnp.zeros_like(dgs_ref)

        for src, g_ref, dst, dg_ref, lo_, w in ((of_ref, gf_ref, dof_ref, dgf_ref, 0, wf),
                                                (os_ref, gs_ref, dos_ref, dgs_ref, wf, ws)):
            x = src[...]
            dy = d_ref[:, lo_:lo_ + w]
            r = lax.rsqrt(jnp.mean(x * x, axis=1, keepdims=True) + EPS)
            wv = dy * g_ref[...]
            proj = jnp.sum(wv * x, axis=1, keepdims=True) * (1.0 / w)
            dst[...] = r * wv - x * (r * r * r * proj)
            dg_ref[...] += jnp.sum(dy * x * r, axis=0, keepdims=True)

    return pl.pallas_call(
        body, name=name, grid=(L // tm,),
        in_specs=[pl.BlockSpec((tm, wf + ws), lambda i: (i, 0)), pl.BlockSpec((tm, wf), lambda i: (i, 0)),
                  pl.BlockSpec((tm, ws), lambda i: (i, 0)),
                  pl.BlockSpec((1, wf), lambda i: (0, 0)), pl.BlockSpec((1, ws), lambda i: (0, 0))],
        out_specs=[pl.BlockSpec((tm, wf), lambda i: (i, 0)), pl.BlockSpec((tm, ws), lambda i: (i, 0)),
                   pl.BlockSpec((1, wf), lambda i: (0, 0)), pl.BlockSpec((1, ws), lambda i: (0, 0))],
        out_shape=[jax.ShapeDtypeStruct((L, wf), F32), jax.ShapeDtypeStruct((L, ws), F32),
                   jax.ShapeDtypeStruct((1, wf), F32), jax.ShapeDtypeStruct((1, ws), F32)],
        compiler_params=_cparams(("arbitrary",)),
    )(dcat, o_fox, o_swa, g_fox, g_swa)


def _loss_head(name, h, target):
    L, D = h.shape
    T = BLOCK

    def body(h_ref, t_ref, loss_ref, dh_ref, dh16_ref):
        i = pl.program_id(0)

        @pl.when(i == 0)
        def _():
            loss_ref[...] = jnp.zeros_like(loss_ref)
            dh_ref[...] = jnp.zeros_like(dh_ref)
            dh16_ref[...] = jnp.zeros_like(dh16_ref)

        @pl.when(i > 0)
        def _():
            err = h_ref[...] - t_ref[...]
            dh = err * (1.0 / D)
            dh_ref[...] = dh
            dh16_ref[...] = dh.astype(BF16)
            loss_ref[...] += jnp.sum(err * err) * (0.5 / D)

    return pl.pallas_call(
        body, name=name, grid=(L // T,),
        in_specs=[pl.BlockSpec((T, D), lambda i: (i, 0)), pl.BlockSpec((T, D), lambda i: (jnp.maximum(i - 1, 0), 0))],
        out_specs=[pl.BlockSpec((8, LANES), lambda i: (0, 0)), pl.BlockSpec((T, D), lambda i: (i, 0)),
                   pl.BlockSpec((T, D), lambda i: (i, 0))],
        out_shape=[jax.ShapeDtypeStruct((8, LANES), F32), jax.ShapeDtypeStruct((L, D), F32),
                   jax.ShapeDtypeStruct((L, D), BF16)],
        compiler_params=_cparams(("arbitrary",)),
    )(h, target)


def _pair_add(name, grad, landed, parts, layer, core):
    _, _, r, C = grad.shape

    def body(s_ref, g_ref, l_ref, p_ref, o_ref):
        o_ref[...] = (g_ref[...].astype(F32) + l_ref[...].astype(F32)).astype(o_ref.dtype)

    return pl.pallas_call(
        body, name=name,
        grid_spec=pltpu.PrefetchScalarGridSpec(
            num_scalar_prefetch=1, grid=(4,),
            in_specs=[pl.BlockSpec((None, None, r, C), lambda k, s: (k, s[0], 0, 0)),
                      pl.BlockSpec((None, r, C), lambda k, s: (k, 0, 0)),
                      pl.BlockSpec(memory_space=pl.ANY)],
            out_specs=pl.BlockSpec((None, None, r, C), lambda k, s: (layer, k, 0, 0))),
        out_shape=jax.ShapeDtypeStruct(parts.shape, parts.dtype),
        input_output_aliases={3: 0},
        compiler_params=_cparams(("arbitrary",)),
    )(core, grad, landed, parts)


def _chip_sum(name, part, landed, chip):
    A, _, r, C = part.shape

    def body(s_ref, p_ref, l0_ref, l1_ref, l2_ref, o_ref):
        o_ref[...] = ((p_ref[...].astype(F32) + l0_ref[...].astype(F32))
                      + (l1_ref[...].astype(F32) + l2_ref[...].astype(F32)))

    def land(k):
        return pl.BlockSpec((None, None, r, C), lambda a, s, k=k: (k, a, 0, 0))

    return pl.pallas_call(
        body, name=name,
        grid_spec=pltpu.PrefetchScalarGridSpec(
            num_scalar_prefetch=1, grid=(A,),
            in_specs=[pl.BlockSpec((None, None, r, C), lambda a, s: (a, s[0], 0, 0)), land(0), land(1), land(2)],
            out_specs=pl.BlockSpec((None, r, C), lambda a, s: (a, 0, 0))),
        out_shape=jax.ShapeDtypeStruct((A, r, C), F32),
        compiler_params=_cparams(("parallel",)),
    )(chip, part, landed, landed, landed)


def _gather_sum(name, v):
    R = v.shape[0]

    def body(x_ref, o_ref, buf_ref, send_sems, recv_sems):
        x, y, c = _place()
        me, sibling = (x, y, c), (x, y, 1 - c)
        chips = [(1 - x, y), (x, 1 - y), (1 - x, 1 - y)]

        def rows(dev):
            px, py, pc = dev
            return buf_ref.at[4 * px + 2 * py + pc]

        def copy(k, block, to, src=None):
            return pltpu.make_async_remote_copy(
                src_ref=rows(block) if src is None else src, dst_ref=rows(block),
                send_sem=send_sems.at[k], recv_sem=recv_sems.at[k], device_id=to, device_id_type=MESH)

        first = [copy(0, me, sibling, src=x_ref)]
        first += [copy(1 + j, me, (*chip, c), src=x_ref) for j, chip in enumerate(chips)]
        for cp in first:
            cp.start()
        rows(me)[...] = x_ref[...]
        passed = [copy(4 + j, (*chip, c), sibling) for j, chip in enumerate(chips)]
        for j, chip in enumerate(chips):
            copy(1 + j, (*chip, c), me).wait_recv()
            passed[j].start()
        copy(0, sibling, me).wait_recv()
        for j, chip in enumerate(chips):
            copy(4 + j, (*chip, 1 - c), me).wait_recv()
        for cp in first + passed:
            cp.wait_send()
        acc = buf_ref[0]
        for d in range(1, N_DEV):
            acc = acc + buf_ref[d]
        o_ref[...] = acc

    vm = pl.BlockSpec(memory_space=pltpu.VMEM)
    return pl.pallas_call(
        body, name=name, in_specs=[vm], out_specs=vm,
        out_shape=jax.ShapeDtypeStruct((R, LANES), F32),
        scratch_shapes=[pltpu.VMEM((N_DEV, R, LANES), F32), pltpu.SemaphoreType.DMA((7,)), pltpu.SemaphoreType.DMA((7,))],
    )(v)


def _adamw(name, g, w, m, v):
    R, C = g.shape
    tr = _tile(R, max(8, (1 << 18) // max(C, 1) // 8 * 8), 8)

    def body(g_ref, w_ref, m_ref, v_ref, d_ref, nm_ref, nv_ref):
        gv = g_ref[...]
        nm = ADAM_B1 * m_ref[...] + (1.0 - ADAM_B1) * gv
        nv = ADAM_B2 * v_ref[...] + (1.0 - ADAM_B2) * (gv * gv)
        m_hat = nm / (1.0 - ADAM_B1 ** ADAM_STEP)
        v_hat = nv / (1.0 - ADAM_B2 ** ADAM_STEP)
        d_ref[...] = -ADAM_LR * (m_hat / (jnp.sqrt(v_hat) + ADAM_EPS) + ADAM_WD * w_ref[...])
        nm_ref[...] = nm
        nv_ref[...] = nv

    blk = pl.BlockSpec((tr, C), lambda i: (i, 0))
    shp = jax.ShapeDtypeStruct((R, C), F32)
    return pl.pallas_call(
        body, name=name, grid=(R // tr,),
        in_specs=[blk] * 4, out_specs=[blk] * 3, out_shape=[shp] * 3,
        compiler_params=_cparams(("parallel",)),
    )(g, w, m, v)


def _adamw_nd(name, g, w, m, v):
    shape = w.shape
    flat = [a.reshape(-1, shape[-1]) for a in (g, w, m, v)]
    return tuple(o.reshape(shape) for o in _adamw(name, *flat))


def _scatter_heads(cfg, vals):
    v = jnp.pad(vals.reshape(cfg.hf // 2, 2), ((0, 0), (0, cfg.cs - 2)))
    return v.reshape(1, LANES)


def _gather_heads(cfg, row):
    return row.reshape(cfg.hf // 2, cfg.cs)[:, :2].reshape(cfg.hf)


def _permute_w_in(cfg, w_in_t):
    wf, hf = cfg.wf, cfg.hf
    o = 3 * wf
    cols = w_in_t.shape[1]
    fz = w_in_t[o:o + hf].reshape(hf // 2, 2, cols)
    fz_blk = jnp.pad(fz, ((0, 0), (0, cfg.cs - 2), (0, 0))).reshape(LANES, cols)
    return jnp.concatenate([w_in_t[:o], w_in_t[o + hf:], fz_blk], axis=0)


def _unpermute_dw_in(cfg, dwp):
    wf, hf = cfg.wf, cfg.hf
    o = 3 * wf
    cols = dwp.shape[1]
    fz = dwp[cfg.o_fz:].reshape(hf // 2, cfg.cs, cols)[:, :2].reshape(hf, cols)
    return jnp.concatenate([dwp[:o], fz, dwp[o:cfg.o_fz]], axis=0)


def _pair_gain(g):
    return jnp.tile(g, 2)[None]


def _fold_pair(dg):
    return dg[0, :HEAD_DIM] + dg[0, HEAD_DIM:]


def kernel(x, meta_tokens, ffn1_norm, ffn1_w_gate, ffn1_w_up, ffn1_w_down, mix_norm, w_in, b_forget, fox_q_norm, fox_k_norm, swa_q_norm, swa_k_norm, swa_sinks, fox_out_norm, swa_out_norm, w_out, ffn2_norm, ffn2_w_gate, ffn2_w_up, ffn2_w_down, loss_target, m_meta_tokens, m_ffn1_norm, m_ffn1_w_gate, m_ffn1_w_up, m_ffn1_w_down, m_mix_norm, m_w_in, m_b_forget, m_fox_q_norm, m_fox_k_norm, m_swa_q_norm, m_swa_k_norm, m_swa_sinks, m_fox_out_norm, m_swa_out_norm, m_w_out, m_ffn2_norm, m_ffn2_w_gate, m_ffn2_w_up, m_ffn2_w_down, v_meta_tokens, v_ffn1_norm, v_ffn1_w_gate, v_ffn1_w_up, v_ffn1_w_down, v_mix_norm, v_w_in, v_b_forget, v_fox_q_norm, v_fox_k_norm, v_swa_q_norm, v_swa_k_norm, v_swa_sinks, v_fox_out_norm, v_swa_out_norm, v_w_out, v_ffn2_norm, v_ffn2_w_gate, v_ffn2_w_up, v_ffn2_w_down):
    weights = dict(meta_tokens=meta_tokens, ffn1_norm=ffn1_norm, ffn1_w_gate=ffn1_w_gate, ffn1_w_up=ffn1_w_up,
                   ffn1_w_down=ffn1_w_down, mix_norm=mix_norm, w_in=w_in, b_forget=b_forget, fox_q_norm=fox_q_norm,
                   fox_k_norm=fox_k_norm, swa_q_norm=swa_q_norm, swa_k_norm=swa_k_norm, swa_sinks=swa_sinks,
                   fox_out_norm=fox_out_norm, swa_out_norm=swa_out_norm, w_out=w_out, ffn2_norm=ffn2_norm,
                   ffn2_w_gate=ffn2_w_gate, ffn2_w_up=ffn2_w_up, ffn2_w_down=ffn2_w_down)
    mom_m = dict(meta_tokens=m_meta_tokens, ffn1_norm=m_ffn1_norm, ffn1_w_gate=m_ffn1_w_gate, ffn1_w_up=m_ffn1_w_up,
                 ffn1_w_down=m_ffn1_w_down, mix_norm=m_mix_norm, w_in=m_w_in, b_forget=m_b_forget,
                 fox_q_norm=m_fox_q_norm, fox_k_norm=m_fox_k_norm, swa_q_norm=m_swa_q_norm, swa_k_norm=m_swa_k_norm,
                 swa_sinks=m_swa_sinks, fox_out_norm=m_fox_out_norm, swa_out_norm=m_swa_out_norm, w_out=m_w_out,
                 ffn2_norm=m_ffn2_norm, ffn2_w_gate=m_ffn2_w_gate, ffn2_w_up=m_ffn2_w_up, ffn2_w_down=m_ffn2_w_down)
    mom_v = dict(meta_tokens=v_meta_tokens, ffn1_norm=v_ffn1_norm, ffn1_w_gate=v_ffn1_w_gate, ffn1_w_up=v_ffn1_w_up,
                 ffn1_w_down=v_ffn1_w_down, mix_norm=v_mix_norm, w_in=v_w_in, b_forget=v_b_forget,
                 fox_q_norm=v_fox_q_norm, fox_k_norm=v_fox_k_norm, swa_q_norm=v_swa_q_norm, swa_k_norm=v_swa_k_norm,
                 swa_sinks=v_swa_sinks, fox_out_norm=v_fox_out_norm, swa_out_norm=v_swa_out_norm, w_out=v_w_out,
                 ffn2_norm=v_ffn2_norm, ffn2_w_gate=v_ffn2_w_gate, ffn2_w_up=v_ffn2_w_up, ffn2_w_down=v_ffn2_w_down)
    names = list(weights)

    _, S, D = x.shape
    depth = ffn1_norm.shape[0]
    hf, hs = b_forget.shape[1], swa_sinks.shape[1]
    U = w_in.shape[2] * N_DEV
    hkv = (U - 3 * HEAD_DIM * hf - hf - HEAD_DIM * hs) // (2 * HEAD_DIM)
    cfg = _Cfg(S, D, hf, hs, hkv)
    n_meta = meta_tokens.shape[0]
    assert n_meta == cfg.n_meta
    x_idx, y_idx, c_idx = _place()
    chip_idx = 2 * x_idx + y_idx
    dev_idx = 2 * chip_idx + c_idx
    core_s = jnp.reshape(c_idx, (1,)).astype(jnp.int32)
    chip_s = jnp.reshape(chip_idx, (1,)).astype(jnp.int32)

    col_sharded = ("ffn1_w_gate", "ffn1_w_up", "w_in", "ffn2_w_gate", "ffn2_w_up")
    use_order = ("ffn1_w_gate", "ffn1_w_up", "ffn1_w_down", "w_in", "w_out", "ffn2_w_gate", "ffn2_w_up", "ffn2_w_down")

    def shard(key):
        k, l = key
        if k == "meta_tokens":
            return meta_tokens
        w = weights[k][l]
        return (w.T if k in col_sharded else w).astype(BF16)

    waiting = [("meta_tokens", 0)] + [(k, l) for l in range(depth) for k in use_order]
    halfway = []
    gathered = {}

    def fwd_carry(n_first):
        cy = _Carry()
        second = [(key, _gather_second(cy, buf)) for key, buf in halfway]
        first = [(key, _gather_first(cy, shard(key))) for key in waiting[:n_first]]
        del waiting[:n_first]
        halfway.clear()
        return cy, (first, second)

    def fwd_absorb(extra, plan):
        first, second = plan
        for key, idx in second:
            gathered[key] = extra[idx]
        for key, idx in first:
            halfway.append((key, extra[idx]))

    def weight(k, l):
        key = (k, l)
        while key not in gathered:
            n = 0 if any(key == hk for hk, _ in halfway) else waiting.index(key) + 1
            cy, plan = fwd_carry(n)
            fwd_absorb(_comm_only("weights_gather", cy), plan)
        g = gathered[key]
        return g.reshape(-1, g.shape[-1])

    weight("ffn1_w_down", 0)
    meta_full = jnp.swapaxes(weight("meta_tokens", 0).reshape(N_DEV, n_meta, -1), 0, 1).reshape(n_meta, D)
    slopes = jnp.asarray(2.0 ** (-8.0 * np.arange(1, hs + 1) / hs), dtype=F32)

    h = jnp.concatenate([jnp.zeros((cfg.pad, D), F32), meta_full, x[0]], axis=0)
    saved = []
    w_in_p = [None] * depth

    def mm_f(name, a, b, n_first=1, **kw):
        cy, plan = fwd_carry(n_first)
        out, extra = _matmul(name, a, b, carry=cy, **kw)
        fwd_absorb(extra, plan)
        return out

    def ffn_fwd(tag, l, h_in, norm, wg, wu, wd):
        xn = _rmsnorm_fwd(f"{tag}_norm", h_in, norm[l][None])
        wg_t, wu_t = weight(wg, l), weight(wu, l)
        cy, plan = fwd_carry(1)
        (gate, up, act), extra = _ffn_up(f"{tag}_up", xn, wg_t, wu_t, carry=cy)
        fwd_absorb(extra, plan)
        h_out = mm_f(f"{tag}_down", act, weight(wd, l), scale=0.5, residual=h_in, tm=544, tn=1024, tk=2816)
        return h_out, (xn, gate, up)

    for l in range(depth):
        st = {"h0": h}
        h, st["ffn1"] = ffn_fwd("ffn1", l, h, ffn1_norm, "ffn1_w_gate", "ffn1_w_up", "ffn1_w_down")
        st["h1"] = h
        xn = _rmsnorm_fwd("mix_norm", h, mix_norm[l][None])
        w_in_p[l] = _permute_w_in(cfg, weight("w_in", l))
        u = mm_f("mix_in", xn, w_in_p[l], trans_b=True, tm=544, tn=640, tk=2048)
        gq, gk = _pair_gain(fox_q_norm[l]), _pair_gain(fox_k_norm[l])
        gsq, gsk = _pair_gain(swa_q_norm[l]), _pair_gain(swa_k_norm[l])
        bias = _scatter_heads(cfg, b_forget[l])
        qn, kn, fv, sqn, skd, svd, c, ct = _mix_prep("mix_prep", cfg, u, gq, gk, gsq, gsk, bias)
        cy, plan = fwd_carry(2)
        (o_fox, lse), extra = _fox_fwd("fox_fwd", cfg, qn, kn, fv, c, ct, carry=cy)
        fwd_absorb(extra, plan)
        o_swa = _swa_fwd("swa_fwd", cfg, sqn, skd, svd, swa_sinks[l], slopes)
        o_cat = _out_norm("out_norm", cfg, o_fox, o_swa, fox_out_norm[l][None], swa_out_norm[l][None])
        st["mix"] = (xn, u, gq, gk, gsq, gsk, bias, qn, kn, fv, sqn, skd, svd, c, ct, o_fox, lse, o_swa, o_cat)
        h = mm_f("mix_out", o_cat, weight("w_out", l), residual=h, tm=544, tn=1024, tk=2048)
        st["h2"] = h
        h, st["ffn2"] = ffn_fwd("ffn2", l, h, ffn2_norm, "ffn2_w_gate", "ffn2_w_up", "ffn2_w_down")
        saved.append(st)

    loss_blk, dh, dh16 = _loss_head("loss_head", h, loss_target[0])

    small = {k: [None] * depth for k in names if k not in use_order and k != "meta_tokens"}
    parts, landing = {}, {}
    to_sibling, to_chips = [], []

    def bwd_carry():
        cy = _Carry()
        t3 = t1 = None
        if to_chips:
            k, l = to_chips.pop(0)
            t3 = (k, _scatter_chips(cy, parts[k], landing[k], l))
        if to_sibling:
            k, l, g = to_sibling.pop(0)
            t1 = (k, l, g, _scatter_sibling(cy, g))
        return cy, (t1, t3)

    def bwd_absorb(extra, plan):
        t1, t3 = plan
        if t3 is not None:
            landing[t3[0]] = extra[t3[1]]
        if t1 is not None:
            k, l, g, idx = t1
            parts[k] = _pair_add("grads_pair_add", g, extra[idx], parts[k], l, core_s)
            to_chips.append((k, l))

    def emit_grad(k, l, dw):
        r, C = dw.shape[0] // N_DEV, dw.shape[1]
        if k not in parts:
            parts[k] = lax.empty((depth, 4, r, C), BF16)
            landing[k] = lax.empty((3, depth, r, C), BF16)
        to_sibling.append((k, l, dw.reshape(4, 2, r, C)))

    def mm_b(name, a, b, **kw):
        cy, plan = bwd_carry()
        out, extra = _matmul(name, a, b, carry=cy, **kw)
        bwd_absorb(extra, plan)
        return out

    def ffn_bwd(tag, l, dh_out, dh_out16, h_in, st_, norm, wg, wu, wd):
        xn, gate, up = st_
        cy, plan = bwd_carry()
        (dgate, dup, act), extra = _ffn_bwd_act(f"{tag}_dact", dh_out16, weight(wd, l), gate, up, carry=cy)
        bwd_absorb(extra, plan)
        emit_grad(wd, l, mm_b(f"{tag}_dwd", act, dh_out16, trans_a=True, scale=0.5, out_dtype=BF16,
                              tm=512, tn=1024, tk=2176))
        emit_grad(wg, l, mm_b(f"{tag}_dwg", dgate, xn, trans_a=True, out_dtype=BF16, tm=512, tn=1024, tk=2176))
        emit_grad(wu, l, mm_b(f"{tag}_dwu", dup, xn, trans_a=True, out_dtype=BF16, tm=512, tn=1024, tk=2176))
        dxn = mm_b(f"{tag}_dxn", dgate, weight(wg, l), pair2=(dup, weight(wu, l)), tm=544, tn=1024, tk=1408)
        dh_in, dh_in16, dg = _rmsnorm_bwd(f"{tag}_dnorm", dxn, h_in, norm[l][None], dh_out)
        return dh_in, dh_in16, dg[0]

    for l in reversed(range(depth)):
        st = saved[l]
        dh, dh16, small["ffn2_norm"][l] = ffn_bwd("ffn2", l, dh, dh16, st["h2"], st["ffn2"], ffn2_norm,
                                                   "ffn2_w_gate", "ffn2_w_up", "ffn2_w_down")
        xn, u, gq, gk, gsq, gsk, bias, qn, kn, fv, sqn, skd, svd, c, ct, o_fox, lse, o_swa, o_cat = st["mix"]
        dcat = mm_b("mix_dcat", dh16, weight("w_out", l), trans_b=True, tm=544, tn=1024, tk=2048)
        emit_grad("w_out", l, mm_b("mix_dwout", o_cat, dh16, trans_a=True, out_dtype=BF16, tm=512, tn=1024, tk=2176))
        do_fox, do_swa, dgf, dgs = _out_norm_bwd("out_norm_bwd", cfg, dcat, o_fox, o_swa,
                                                 fox_out_norm[l][None], swa_out_norm[l][None])
        small["fox_out_norm"][l], small["swa_out_norm"][l] = dgf[0], dgs[0]
        cy, plan = bwd_carry()
        (dqn, dkn, dfv, dct, dcq), extra = _fox_bwd("fox_bwd", cfg, qn, kn, fv, c, ct, o_fox, lse, do_fox, carry=cy)
        bwd_absorb(extra, plan)
        dsqn, dskd, dsvd, dsink = _swa_bwd("swa_bwd", cfg, sqn, skd, svd, swa_sinks[l], slopes, o_swa, do_swa)
        small["swa_sinks"][l] = dsink.reshape(hs // 2, 8, LANES)[:, 0, ::HEAD_DIM].reshape(hs)
        du, dgq, dgk, dgsq, dgsk, db = _mix_prep_bwd("mix_prep_bwd", cfg, u, gq, gk, gsq, gsk, bias,
                                                     dqn, dkn, dfv, dsqn, dskd, dsvd, dct, dcq)
        small["fox_q_norm"][l], small["fox_k_norm"][l] = _fold_pair(dgq), _fold_pair(dgk)
        small["swa_q_norm"][l], small["swa_k_norm"][l] = _fold_pair(dgsq), _fold_pair(dgsk)
        small["b_forget"][l] = _gather_heads(cfg, db[0])
        dwp = mm_b("mix_dwin", du, xn, trans_a=True, out_dtype=BF16, tm=640, tn=1024, tk=2176)
        emit_grad("w_in", l, _unpermute_dw_in(cfg, dwp))
        dxn = mm_b("mix_dxn", du, w_in_p[l], tm=544, tn=1024, tk=4480)
        dh, dh16, dg = _rmsnorm_bwd("mix_dnorm", dxn, st["h1"], mix_norm[l][None], dh)
        small["mix_norm"][l] = dg[0]
        dh, dh16, small["ffn1_norm"][l] = ffn_bwd("ffn1", l, dh, dh16, st["h0"], st["ffn1"], ffn1_norm,
                                                   "ffn1_w_gate", "ffn1_w_up", "ffn1_w_down")

    grad_x = dh[BLOCK:][None]
    dmeta = dh[cfg.pad:BLOCK]

    while to_sibling or to_chips:
        cy, plan = bwd_carry()
        bwd_absorb(_comm_only("grads_scatter", cy), plan)

    grads = {}
    for k in use_order:
        g = _chip_sum("grads_chip_sum", parts[k], landing[k], chip_s)
        grads[k] = jnp.swapaxes(g, 1, 2) if k in col_sharded else g

    small_names = list(small)
    pieces = [loss_blk[0, :1], dmeta.reshape(-1)] + [jnp.stack(small[k]).reshape(-1) for k in small_names]
    sizes = [int(p.shape[0]) for p in pieces]
    total = sum(sizes)
    padded = -(-total // (8 * LANES)) * (8 * LANES)
    vec = jnp.concatenate(pieces + [jnp.zeros((padded - total,), F32)]).reshape(-1, LANES)
    summed = _gather_sum("small_gather_sum", vec).reshape(-1)
    offs = np.cumsum([0] + sizes)
    loss = summed[0]
    dmeta_full = summed[offs[1]:offs[2]].reshape(n_meta, D)
    mcols = meta_tokens.shape[1]
    grads["meta_tokens"] = lax.dynamic_slice_in_dim(dmeta_full, dev_idx * mcols, mcols, axis=1)
    for n_, k in enumerate(small_names):
        grads[k] = summed[offs[2 + n_]:offs[3 + n_]].reshape(weights[k].shape)

    delta, new_m, new_v = {}, {}, {}
    for k in names:
        delta[k], new_m[k], new_v[k] = _adamw_nd("adamw", grads[k], weights[k], mom_m[k], mom_v[k])

    return (loss, grad_x, *[grads[k] for k in names], *[delta[k] for k in names],
            *[new_m[k] for k in names], *[new_v[k] for k in names])
```

```python
import functools

import numpy as np
import jax
import jax.numpy as jnp
from jax import lax
from jax.experimental import pallas as pl
from jax.experimental.pallas import tpu as pltpu

F32 = jnp.float32
BF16 = jnp.bfloat16
MESH = pl.DeviceIdType.MESH

HEAD_DIM = 64
BLOCK = 128
LANES = 128
N_DEV = 8
EPS = 1e-6
NEG_INF = -1e30
SCALE = HEAD_DIM ** -0.5

ADAM_LR = 0.001
ADAM_B1 = 0.9
ADAM_B2 = 0.999
ADAM_EPS = 1e-08
ADAM_WD = 0.01
ADAM_STEP = 10

VMEM_BYTES_V7X = 64 * 1024 * 1024
VMEM_LIMIT = VMEM_BYTES_V7X * 3 // 4

NT = (((1,), (1,)), ((), ()))
TN = (((0,), (0,)), ((), ()))
HI = lax.Precision.HIGHEST


def _cparams(sem=None, vmem=VMEM_LIMIT):
    return pltpu.CompilerParams(dimension_semantics=sem, vmem_limit_bytes=vmem)


def _tile(n, pref, mult):
    best = None
    for t in range(mult, min(n, pref) + 1, mult):
        if n % t == 0:
            best = t
    return best if best is not None else n


def _dot(a, b):
    return jnp.dot(a, b, preferred_element_type=F32)


def _dot_nt(a, b):
    return lax.dot_general(a, b, NT, preferred_element_type=F32)


def _dot_tn(a, b):
    return lax.dot_general(a, b, TN, preferred_element_type=F32)


def _lane(shape):
    return lax.broadcasted_iota(jnp.int32, shape, len(shape) - 1)


def _half_sum(x, lo):
    s0 = jnp.sum(jnp.where(lo, x, 0.0), axis=1, keepdims=True)
    s1 = jnp.sum(jnp.where(lo, 0.0, x), axis=1, keepdims=True)
    return jnp.where(lo, s0, s1)


def _sigmoid(x):
    return 1.0 / (1.0 + jnp.exp(-x))


def _place():
    return lax.axis_index("x"), lax.axis_index("y"), lax.axis_index("c")


def _peers(x, y, c):
    return [(x, y, 1 - c), (1 - x, y, c), (x, 1 - y, c), (1 - x, 1 - y, c)]


def _dev_index(dev):
    px, py, pc = dev
    return 4 * px + 2 * py + pc


class _Carry:
    def __init__(self):
        self.ins, self.outs, self.alias, self.items = [], [], {}, []
        self.nsem = self.nloc = 0

    def add(self, ins, outs, alias, nsem, nloc, build):
        i0, o0 = len(self.ins), len(self.outs)
        for src, dst in alias.items():
            self.alias[i0 + src] = o0 + dst
        self.items.append((i0, len(ins), o0, len(outs), self.nsem, self.nloc, build))
        self.ins += ins
        self.outs += outs
        self.nsem += nsem
        self.nloc += nloc
        return list(range(o0, o0 + len(outs)))

    def build(self, in_refs, out_refs, ssem, rsem, lsem):
        ops = []
        for i0, ni, o0, no, s0, l0, fn in self.items:
            ops.append(fn(in_refs[i0:i0 + ni], out_refs[o0:o0 + no],
                          lambda k, s0=s0: (ssem.at[s0 + k], rsem.at[s0 + k]), lambda k, l0=l0: lsem.at[l0 + k]))
        return ops


def _remote(src, dst, sems, dev):
    return pltpu.make_async_remote_copy(src_ref=src, dst_ref=dst, send_sem=sems[0], recv_sem=sems[1],
                                        device_id=dev, device_id_type=MESH)


def _gather_first(carry, shard):
    def build(ins, outs, sems, locs):
        src, buf = ins[0], outs[0]
        x, y, c = _place()
        me = _dev_index((x, y, c))
        peers = _peers(x, y, c)
        local = pltpu.make_async_copy(src, buf.at[me], locs(0))
        sends = [_remote(src, buf.at[me], sems(k), dev) for k, dev in enumerate(peers)]
        recvs = [_remote(src, buf.at[_dev_index(dev)], sems(k), dev) for k, dev in enumerate(peers)]

        def start():
            local.start()
            for cp in sends:
                cp.start()

        def wait():
            for cp in sends:
                cp.wait_send()
            for cp in recvs:
                cp.wait_recv()
            local.wait()

        return start, wait

    return carry.add([shard], [jax.ShapeDtypeStruct((N_DEV,) + shard.shape, shard.dtype)], {}, 4, 1, build)[0]


def _gather_second(carry, buf):
    def build(ins, outs, sems, locs):
        b = outs[0]
        x, y, c = _place()
        chips = _peers(x, y, c)[1:]
        sends = [_remote(b.at[_dev_index(dev)], b.at[_dev_index(dev)], sems(k), (x, y, 1 - c))
                 for k, dev in enumerate(chips)]
        recvs = [_remote(b.at[_dev_index(dev)], b.at[_dev_index((dev[0], dev[1], 1 - c))], sems(k), (x, y, 1 - c))
                 for k, dev in enumerate(chips)]

        def start():
            for cp in sends:
                cp.start()

        def wait():
            for cp in sends:
                cp.wait_send()
            for cp in recvs:
                cp.wait_recv()

        return start, wait

    return carry.add([buf], [jax.ShapeDtypeStruct(buf.shape, buf.dtype)], {0: 0}, 3, 0, build)[0]


def _scatter_sibling(carry, grad):
    def build(ins, outs, sems, locs):
        x, y, c = _place()
        cp = _remote(ins[0].at[:, 1 - c], outs[0], sems(0), (x, y, 1 - c))
        return cp.start, cp.wait

    shape = (grad.shape[0],) + grad.shape[2:]
    return carry.add([grad], [jax.ShapeDtypeStruct(shape, grad.dtype)], {}, 1, 0, build)[0]


def _scatter_chips(carry, parts, landing, layer):
    def build(ins, outs, sems, locs):
        x, y, c = _place()
        cps = [_remote(ins[0].at[layer, 2 * dev[0] + dev[1]], outs[0].at[k, layer], sems(k), dev)
               for k, dev in enumerate(_peers(x, y, c)[1:])]

        def start():
            for cp in cps:
                cp.start()

        def wait():
            for cp in cps:
                cp.wait()

        return start, wait

    return carry.add([parts, landing], [jax.ShapeDtypeStruct(landing.shape, landing.dtype)], {1: 0}, 3, 0, build)[0]


def _call(name, body, grid, in_specs, out_specs, out_shape, args, scratch=(), carry=None):
    ni, no, ns = len(args), len(out_shape), len(scratch)
    if carry is None or not carry.items:
        res = pl.pallas_call(
            body, name=name, grid=grid, in_specs=list(in_specs), out_specs=list(out_specs), out_shape=list(out_shape),
            scratch_shapes=list(scratch), compiler_params=_cparams(("arbitrary",) * len(grid)))(*args)
        return list(res), []
    nci, nco = len(carry.ins), len(carry.outs)

    def full_body(*refs):
        c_in = refs[ni:ni + nci]
        c_out = refs[ni + nci + no:ni + nci + no + nco]
        sc = refs[ni + nci + no + nco:]
        ops = carry.build(c_in, c_out, sc[ns], sc[ns + 1], sc[ns + 2])
        first = last = None
        for d, n in enumerate(grid):
            pid = pl.program_id(d)
            first = (pid == 0) if first is None else first & (pid == 0)
            last = (pid == n - 1) if last is None else last & (pid == n - 1)

        @pl.when(first)
        def _():
            for start, _w in ops:
                start()

        body(*refs[:ni], *refs[ni + nci:ni + nci + no], *sc[:ns])

        @pl.when(last)
        def _():
            for _s, wait in ops:
                wait()

    hbm = pl.BlockSpec(memory_space=pl.ANY)
    res = pl.pallas_call(
        full_body, name=name, grid=grid,
        in_specs=list(in_specs) + [hbm] * nci, out_specs=list(out_specs) + [hbm] * nco,
        out_shape=list(out_shape) + list(carry.outs),
        input_output_aliases={ni + s: no + d for s, d in carry.alias.items()},
        scratch_shapes=list(scratch) + [pltpu.SemaphoreType.DMA((carry.nsem,)), pltpu.SemaphoreType.DMA((carry.nsem,)),
                                        pltpu.SemaphoreType.DMA((max(carry.nloc, 1),))],
        compiler_params=_cparams(("arbitrary",) * len(grid)))(*args, *carry.ins)
    return list(res[:no]), list(res[no:])


def _comm_only(name, carry):
    return _call(name, lambda *refs: None, (1,), [], [], [], [], carry=carry)[1]


def _matmul(name, a, b, *, pair2=None, trans_a=False, trans_b=False, out_dtype=F32, scale=None, residual=None,
            tm=512, tn=512, tk=512, carry=None):
    if trans_a:
        K, M = a.shape
    else:
        M, K = a.shape
    if trans_b:
        N, Kb = b.shape
    else:
        Kb, N = b.shape
    assert K == Kb, (name, a.shape, b.shape)
    tm = _tile(M, tm, LANES if trans_a else 16)
    tn = _tile(N, tn, LANES)
    tk = _tile(K, tk, 16 if (trans_a and not trans_b) else LANES)
    nk = K // tk
    dims = (((0 if trans_a else 1,), (1 if trans_b else 0,)), ((), ()))
    pairs = [(a, b)] + ([pair2] if pair2 is not None else [])
    npair = len(pairs)

    def body(*refs):
        ab = refs[:2 * npair]
        pos = 2 * npair
        r_ref = None
        if residual is not None:
            r_ref = refs[pos]
            pos += 1
        o_ref = refs[pos]

        def partial():
            t = None
            for q in range(npair):
                d = lax.dot_general(ab[2 * q][...].astype(BF16), ab[2 * q + 1][...].astype(BF16), dims,
                                    preferred_element_type=F32)
                t = d if t is None else t + d
            return t

        def finish(r):
            if scale is not None:
                r = r * scale
            if r_ref is not None:
                r = r + r_ref[...].astype(F32)
            o_ref[...] = r.astype(o_ref.dtype)

        if nk == 1:
            finish(partial())
            return
        acc_ref = refs[pos + 1]
        k = pl.program_id(2)

        @pl.when(k == 0)
        def _():
            acc_ref[...] = partial()

        @pl.when(k > 0)
        def _():
            acc_ref[...] += partial()

        @pl.when(k == nk - 1)
        def _():
            finish(acc_ref[...])

    a_spec = pl.BlockSpec((tk, tm), lambda i, j, k: (k, i)) if trans_a else pl.BlockSpec((tm, tk), lambda i, j, k: (i, k))
    b_spec = pl.BlockSpec((tn, tk), lambda i, j, k: (j, k)) if trans_b else pl.BlockSpec((tk, tn), lambda i, j, k: (k, j))
    in_specs, args = [], []
    for pa, pb in pairs:
        in_specs += [a_spec, b_spec]
        args += [pa, pb]
    if residual is not None:
        in_specs.append(pl.BlockSpec((tm, tn), lambda i, j, k: (i, j)))
        args.append(residual)
    res, extra = _call(
        name, body, (M // tm, N // tn, nk), in_specs, [pl.BlockSpec((tm, tn), lambda i, j, k: (i, j))],
        [jax.ShapeDtypeStruct((M, N), out_dtype)], args,
        scratch=[pltpu.VMEM((tm, tn), F32)] if nk > 1 else [], carry=carry)
    return res[0], extra


def _rmsnorm_fwd(name, h, g):
    L, D = h.shape
    tm = _tile(L, 256, 16)

    def body(h_ref, g_ref, o_ref):
        x = h_ref[...]
        r = lax.rsqrt(jnp.mean(x * x, axis=1, keepdims=True) + EPS)
        o_ref[...] = (x * r * g_ref[...]).astype(o_ref.dtype)

    return pl.pallas_call(
        body, name=name, grid=(L // tm,),
        in_specs=[pl.BlockSpec((tm, D), lambda i: (i, 0)), pl.BlockSpec((1, D), lambda i: (0, 0))],
        out_specs=pl.BlockSpec((tm, D), lambda i: (i, 0)),
        out_shape=jax.ShapeDtypeStruct((L, D), BF16),
        compiler_params=_cparams(("parallel",)),
    )(h, g)


def _rmsnorm_bwd(name, dy, h, g, dres):
    L, D = h.shape
    tm = _tile(L, 256, 16)

    def body(dy_ref, h_ref, g_ref, dres_ref, dh_ref, dh16_ref, dg_ref):
        i = pl.program_id(0)
        x = h_ref[...]
        dyv = dy_ref[...].astype(F32)
        r = lax.rsqrt(jnp.mean(x * x, axis=1, keepdims=True) + EPS)
        w = dyv * g_ref[...]
        proj = jnp.sum(w * x, axis=1, keepdims=True) * (1.0 / D)
        dh = dres_ref[...] + r * w - x * (r * r * r * proj)
        dh_ref[...] = dh
        dh16_ref[...] = dh.astype(BF16)

        @pl.when(i == 0)
        def _():
            dg_ref[...] = jnp.zeros_like(dg_ref)

        dg_ref[...] += jnp.sum(dyv * x * r, axis=0, keepdims=True)

    return pl.pallas_call(
        body, name=name, grid=(L // tm,),
        in_specs=[pl.BlockSpec((tm, D), lambda i: (i, 0)), pl.BlockSpec((tm, D), lambda i: (i, 0)),
                  pl.BlockSpec((1, D), lambda i: (0, 0)), pl.BlockSpec((tm, D), lambda i: (i, 0))],
        out_specs=[pl.BlockSpec((tm, D), lambda i: (i, 0)), pl.BlockSpec((tm, D), lambda i: (i, 0)),
                   pl.BlockSpec((1, D), lambda i: (0, 0))],
        out_shape=[jax.ShapeDtypeStruct((L, D), F32), jax.ShapeDtypeStruct((L, D), BF16),
                   jax.ShapeDtypeStruct((1, D), F32)],
        compiler_params=_cparams(("arbitrary",)),
    )(dy, h, g, dres)


def _ffn_up(name, xn, wgT, wuT, carry=None):
    L, D = xn.shape
    F = wgT.shape[0]
    tm = _tile(L, 544, 16)
    tn = _tile(F, 512, LANES)

    def body(x_ref, wg_ref, wu_ref, g_ref, u_ref, a_ref):
        x = x_ref[...]
        g = _dot_nt(x, wg_ref[...])
        u = _dot_nt(x, wu_ref[...])
        g_ref[...] = g.astype(BF16)
        u_ref[...] = u.astype(BF16)
        a_ref[...] = (g * _sigmoid(g) * u).astype(BF16)

    o_spec = pl.BlockSpec((tm, tn), lambda i, j: (i, j))
    o_shape = jax.ShapeDtypeStruct((L, F), BF16)
    return _call(
        name, body, (L // tm, F // tn),
        [pl.BlockSpec((tm, D), lambda i, j: (i, 0)), pl.BlockSpec((tn, D), lambda i, j: (j, 0)),
         pl.BlockSpec((tn, D), lambda i, j: (j, 0))],
        [o_spec, o_spec, o_spec], [o_shape, o_shape, o_shape], [xn, wgT, wuT], carry=carry)


def _ffn_bwd_act(name, dh, wd, gate, up, carry=None):
    L, D = dh.shape
    F = wd.shape[0]
    tm = _tile(L, 544, 16)
    tn = _tile(F, 512, LANES)

    def body(dh_ref, wd_ref, g_ref, u_ref, dg_ref, du_ref, a_ref):
        da = 0.5 * _dot_nt(dh_ref[...].astype(BF16), wd_ref[...])
        g = g_ref[...].astype(F32)
        u = u_ref[...].astype(F32)
        sg = _sigmoid(g)
        silu = g * sg
        dg_ref[...] = (da * u * (sg * (1.0 + g * (1.0 - sg)))).astype(BF16)
        du_ref[...] = (da * silu).astype(BF16)
        a_ref[...] = (silu * u).astype(BF16)

    o_spec = pl.BlockSpec((tm, tn), lambda i, j: (i, j))
    o_shape = jax.ShapeDtypeStruct((L, F), BF16)
    return _call(
        name, body, (L // tm, F // tn),
        [pl.BlockSpec((tm, D), lambda i, j: (i, 0)), pl.BlockSpec((tn, D), lambda i, j: (j, 0)), o_spec, o_spec],
        [o_spec, o_spec, o_spec], [o_shape, o_shape, o_shape], [dh, wd, gate, up], carry=carry)


class _Cfg:
    def __init__(self, S, D, hf, hs, hkv):
        self.S, self.D, self.L = S, D, S + BLOCK
        self.hf, self.hs, self.hkv = hf, hs, hkv
        self.wf, self.ws = hf * HEAD_DIM, hs * HEAD_DIM
        self.group = hs // hkv
        self.cs = 2 * LANES // hf
        self.n_meta = 16
        self.pad = BLOCK - self.n_meta
        self.o_fk = self.wf
        self.o_fv = 2 * self.wf
        self.o_sq = 3 * self.wf
        self.o_sk = self.o_sq + self.ws
        self.o_sv = self.o_sk + LANES
        self.o_fz = self.o_sv + LANES
        self.up = self.o_fz + LANES
        assert hkv == 2 and hf % 2 == 0 and self.group % 2 == 0 and self.cs % 8 == 0
        assert self.o_sq % self.ws == 0 and (2 * self.wf) % LANES == 0


def _head_norm(x, gain, lo, mult):
    r = lax.rsqrt(_half_sum(x * x, lo) * (1.0 / HEAD_DIM) + EPS)
    return x * r * (gain * mult)


def _head_norm_bwd(dy, x, gain, lo, mult):
    r = lax.rsqrt(_half_sum(x * x, lo) * (1.0 / HEAD_DIM) + EPS)
    w = dy * (gain * mult)
    proj = _half_sum(w * x, lo) * (1.0 / HEAD_DIM)
    dx = r * w - x * (r * r * r * proj)
    dgain = jnp.sum(dy * mult * x * r, axis=0, keepdims=True)
    return dx, dgain


def _dup(x, lo):
    xr = pltpu.roll(x, 64, 1)
    return jnp.where(lo, x, xr), jnp.where(lo, xr, x)


def _mix_prep(name, cfg, u, gq, gk, gsq, gsk, bias):
    L, wf, ws = cfg.L, cfg.wf, cfg.ws
    T = BLOCK
    npf, nps = wf // LANES, ws // LANES

    def body(fqk_ref, fv_ref, sq_ref, sk_ref, sv_ref, fz_ref, gq_ref, gk_ref, gsq_ref, gsk_ref, b_ref,
             qn_ref, kn_ref, fvo_ref, sqn_ref, skd_ref, svd_ref, c_ref, ct_ref, carry_ref):
        i = pl.program_id(0)
        lo = _lane((T, LANES)) < HEAD_DIM
        for p in range(npf):
            sl = slice(p * LANES, (p + 1) * LANES)
            qn_ref[:, sl] = _head_norm(fqk_ref[:, sl], gq_ref[...], lo, SCALE).astype(BF16)
            kn_ref[:, sl] = _head_norm(fqk_ref[:, wf + p * LANES: wf + (p + 1) * LANES], gk_ref[...], lo, 1.0).astype(BF16)
        fvo_ref[...] = fv_ref[...].astype(BF16)
        for p in range(nps):
            sl = slice(p * LANES, (p + 1) * LANES)
            sqn_ref[:, sl] = _head_norm(sq_ref[:, sl], gsq_ref[...], lo, SCALE).astype(BF16)
        k0, k1 = _dup(_head_norm(sk_ref[...], gsk_ref[...], lo, 1.0), lo)
        skd_ref[:, :LANES] = k0.astype(BF16)
        skd_ref[:, LANES:] = k1.astype(BF16)
        v0, v1 = _dup(sv_ref[...], lo)
        svd_ref[:, :LANES] = v0.astype(BF16)
        svd_ref[:, LANES:] = v1.astype(BF16)

        @pl.when(i == 0)
        def _():
            carry_ref[...] = jnp.zeros_like(carry_ref)

        z = fz_ref[...] + b_ref[...]
        lf = jnp.minimum(z, 0.0) - jnp.log(1.0 + jnp.exp(-jnp.abs(z)))
        row = lax.broadcasted_iota(jnp.int32, (T, T), 0)
        col = lax.broadcasted_iota(jnp.int32, (T, T), 1)
        tri = jnp.where(col <= row, 1.0, 0.0).astype(F32)
        c = jnp.dot(tri, lf, precision=HI, preferred_element_type=F32) + carry_ref[0:1, :]
        c_ref[...] = c
        ct_ref[...] = c.T
        carry_ref[0:1, :] = c_ref[T - 1:T, :]

    def rows(w, cb):
        return pl.BlockSpec((T, w), lambda i, cb=cb: (i, cb))

    vec = pl.BlockSpec((1, LANES), lambda i: (0, 0))
    return pl.pallas_call(
        body, name=name, grid=(L // T,),
        in_specs=[rows(2 * wf, 0), rows(wf, 2), rows(ws, cfg.o_sq // ws), rows(LANES, cfg.o_sk // LANES),
                  rows(LANES, cfg.o_sv // LANES), rows(LANES, cfg.o_fz // LANES), vec, vec, vec, vec, vec],
        out_specs=[rows(wf, 0), rows(wf, 0), rows(wf, 0), rows(ws, 0), rows(2 * LANES, 0), rows(2 * LANES, 0),
                   rows(LANES, 0), pl.BlockSpec((LANES, T), lambda i: (0, i))],
        out_shape=[jax.ShapeDtypeStruct((L, wf), BF16)] * 3 + [jax.ShapeDtypeStruct((L, ws), BF16)]
        + [jax.ShapeDtypeStruct((L, 2 * LANES), BF16)] * 2
        + [jax.ShapeDtypeStruct((L, LANES), F32), jax.ShapeDtypeStruct((LANES, L), F32)],
        scratch_shapes=[pltpu.VMEM((8, LANES), F32)],
        compiler_params=_cparams(("arbitrary",)),
    )(u, u, u, u, u, u, gq, gk, gsq, gsk, bias)


def _mix_prep_bwd(name, cfg, u, gq, gk, gsq, gsk, bias, dqn, dkn, dfv, dsqn, dskd, dsvd, dct, dcq):
    L, wf, ws = cfg.L, cfg.wf, cfg.ws
    T = BLOCK
    nb = L // T
    npf, nps = wf // LANES, ws // LANES

    def body(fqk_ref, sq_ref, sk_ref, fz_ref, gq_ref, gk_ref, gsq_ref, gsk_ref, b_ref,
             dqn_ref, dkn_ref, dfv_ref, dsqn_ref, dskd_ref, dsvd_ref, dct_ref, dcq_ref,
             du_ref, dgq_ref, dgk_ref, dgsq_ref, dgsk_ref, db_ref, carry_ref):
        i = pl.program_id(0)
        lo = _lane((T, LANES)) < HEAD_DIM

        @pl.when(i == 0)
        def _():
            carry_ref[...] = jnp.zeros_like(carry_ref)
            for r in (dgq_ref, dgk_ref, dgsq_ref, dgsk_ref, db_ref):
                r[...] = jnp.zeros_like(r)

        accq = jnp.zeros((1, LANES), F32)
        acck = jnp.zeros((1, LANES), F32)
        for p in range(npf):
            sl = slice(p * LANES, (p + 1) * LANES)
            dx, dg = _head_norm_bwd(dqn_ref[:, sl], fqk_ref[:, sl], gq_ref[...], lo, SCALE)
            du_ref[:, sl] = dx.astype(BF16)
            accq = accq + dg
            slk = slice(wf + p * LANES, wf + (p + 1) * LANES)
            dx, dg = _head_norm_bwd(dkn_ref[:, sl], fqk_ref[:, slk], gk_ref[...], lo, 1.0)
            du_ref[:, slk] = dx.astype(BF16)
            acck = acck + dg
        dgq_ref[...] += accq
        dgk_ref[...] += acck
        du_ref[:, cfg.o_fv:cfg.o_fv + wf] = dfv_ref[...].astype(BF16)
        accs = jnp.zeros((1, LANES), F32)
        for p in range(nps):
            sl = slice(p * LANES, (p + 1) * LANES)
            dx, dg = _head_norm_bwd(dsqn_ref[:, sl], sq_ref[:, sl], gsq_ref[...], lo, SCALE)
            du_ref[:, cfg.o_sq + p * LANES: cfg.o_sq + (p + 1) * LANES] = dx.astype(BF16)
            accs = accs + dg
        dgsq_ref[...] += accs

        def fold(ref):
            a0, a1 = ref[:, :LANES], ref[:, LANES:]
            return jnp.where(lo, a0 + pltpu.roll(a0, 64, 1), a1 + pltpu.roll(a1, 64, 1))

        dx, dg = _head_norm_bwd(fold(dskd_ref), sk_ref[...], gsk_ref[...], lo, 1.0)
        du_ref[:, cfg.o_sk:cfg.o_sk + LANES] = dx.astype(BF16)
        dgsk_ref[...] += dg
        du_ref[:, cfg.o_sv:cfg.o_sv + LANES] = fold(dsvd_ref).astype(BF16)

        dc = dct_ref[...].T + dcq_ref[...]
        row = lax.broadcasted_iota(jnp.int32, (T, T), 0)
        col = lax.broadcasted_iota(jnp.int32, (T, T), 1)
        triu = jnp.where(col >= row, 1.0, 0.0).astype(F32)
        dlf = jnp.dot(triu, dc, precision=HI, preferred_element_type=F32) + carry_ref[0:1, :]
        carry_ref[0:1, :] = dlf[0:1, :]
        z = fz_ref[...] + b_ref[...]
        dz = dlf * _sigmoid(-z)
        du_ref[:, cfg.o_fz:cfg.o_fz + LANES] = dz.astype(BF16)
        db_ref[...] += jnp.sum(dz, axis=0, keepdims=True)

    def rows(w, cb):
        return pl.BlockSpec((T, w), lambda i, cb=cb: (nb - 1 - i, cb))

    vec = pl.BlockSpec((1, LANES), lambda i: (0, 0))
    vshape = jax.ShapeDtypeStruct((1, LANES), F32)
    return pl.pallas_call(
        body, name=name, grid=(nb,),
        in_specs=[rows(2 * wf, 0), rows(ws, cfg.o_sq // ws), rows(LANES, cfg.o_sk // LANES),
                  rows(LANES, cfg.o_fz // LANES), vec, vec, vec, vec, vec,
                  rows(wf, 0), rows(wf, 0), rows(wf, 0), rows(ws, 0), rows(2 * LANES, 0), rows(2 * LANES, 0),
                  pl.BlockSpec((LANES, T), lambda i: (0, nb - 1 - i)), rows(LANES, 0)],
        out_specs=[rows(cfg.up, 0), vec, vec, vec, vec, vec],
        out_shape=[jax.ShapeDtypeStruct((L, cfg.up), BF16)] + [vshape] * 5,
        scratch_shapes=[pltpu.VMEM((8, LANES), F32)],
        compiler_params=_cparams(("arbitrary",)),
    )(u, u, u, u, gq, gk, gsq, gsk, bias, dqn, dkn, dfv, dsqn, dskd, dsvd, dct, dcq)


def _fox_fwd(name, cfg, qn, kn, fv, c, ct, carry=None):
    L, wf, cs = cfg.L, cfg.wf, cfg.cs
    T = BLOCK
    TQ = _tile(L, 544, 8)
    npairs = wf // LANES
    pad = cfg.pad

    def body(q_ref, k_ref, v_ref, c_ref, ct_ref, o_ref, lse_ref):
        p = pl.program_id(0)
        i = pl.program_id(1)
        lane = _lane((TQ, LANES))
        lo = lane < HEAD_DIM
        q = q_ref[...]
        qh = (jnp.where(lo, q, jnp.zeros_like(q)), jnp.where(lo, jnp.zeros_like(q), q))
        cblk = c_ref[...]
        cq = tuple(jnp.sum(jnp.where(lane == p * cs + hh, cblk, 0.0), axis=1, keepdims=True) for hh in range(2))
        qpos = i * TQ + lax.broadcasted_iota(jnp.int32, (TQ, T), 0)
        lo_k = _lane((T, LANES)) < HEAD_DIM

        def step(j, carry_):
            m0, m1, l0, l1, acc = carry_
            ms, ls = [m0, m1], [l0, l1]
            off = pl.multiple_of(j * T, T)
            k = k_ref[pl.ds(off, T), :]
            v = v_ref[pl.ds(off, T), :]
            kpos = j * T + lax.broadcasted_iota(jnp.int32, (TQ, T), 1)
            allowed = (kpos <= qpos) & (kpos >= pad)
            for hh in range(2):
                ck = ct_ref[hh:hh + 1, pl.ds(off, T)]
                s = _dot_nt(qh[hh], k) + cq[hh] - ck
                s = jnp.where(allowed, s, NEG_INF)
                m_new = jnp.maximum(ms[hh], jnp.max(s, axis=1, keepdims=True))
                alpha = jnp.exp(ms[hh] - m_new)
                pr = jnp.exp(s - m_new)
                ls[hh] = alpha * ls[hh] + jnp.sum(pr, axis=1, keepdims=True)
                ms[hh] = m_new
                sel = lo if hh == 0 else jnp.logical_not(lo)
                sel_k = lo_k if hh == 0 else jnp.logical_not(lo_k)
                vh = jnp.where(sel_k, v, jnp.zeros_like(v))
                acc = acc * jnp.where(sel, alpha, 1.0) + _dot(pr.astype(BF16), vh)
            return ms[0], ms[1], ls[0], ls[1], acc

        init = (jnp.full((TQ, 1), NEG_INF, F32), jnp.full((TQ, 1), NEG_INF, F32),
                jnp.zeros((TQ, 1), F32), jnp.zeros((TQ, 1), F32), jnp.zeros((TQ, LANES), F32))
        m0, m1, l0, l1, acc = lax.fori_loop(0, ((i + 1) * TQ + T - 1) // T, step, init)
        o_ref[...] = acc / jnp.where(lo, l0, l1)
        lse_ref[...] = jnp.where(lo, m0 + jnp.log(l0), m1 + jnp.log(l1))

    blk = pl.BlockSpec((TQ, LANES), lambda p, i: (i, p))
    full = pl.BlockSpec((L, LANES), lambda p, i: (0, p))
    return _call(
        name, body, (npairs, L // TQ),
        [blk, full, full, pl.BlockSpec((TQ, LANES), lambda p, i: (i, 0)), pl.BlockSpec((cs, L), lambda p, i: (p, 0))],
        [blk, blk], [jax.ShapeDtypeStruct((L, wf), F32)] * 2, [qn, kn, fv, c, ct], carry=carry)


def _fox_bwd(name, cfg, qn, kn, fv, c, ct, o, lse, do, carry=None):
    L, wf, cs = cfg.L, cfg.wf, cfg.cs
    T = BLOCK
    TQ = _tile(L, 544, 8)
    nb = L // T
    nq = L // TQ
    npairs = wf // LANES
    pad = cfg.pad

    def body(q_ref, k_ref, v_ref, c_ref, ct_ref, o_ref, lse_ref, do_ref, dq_ref, dk_ref, dv_ref, dct_ref, dcq_ref):
        p = pl.program_id(0)
        j = pl.program_id(1)
        lane = _lane((TQ, LANES))
        lo = lane < HEAD_DIM
        sels = (lo, jnp.logical_not(lo))
        lo_k = _lane((T, LANES)) < HEAD_DIM
        sels_k = (lo_k, jnp.logical_not(lo_k))

        @pl.when(j == 0)
        def _():
            dq_ref[...] = jnp.zeros_like(dq_ref)

        @pl.when((j == 0) & (p == 0))
        def _():
            dcq_ref[...] = jnp.zeros_like(dcq_ref)

        k = k_ref[...]
        v = v_ref[...]
        kh = tuple(jnp.where(s_, k, jnp.zeros_like(k)) for s_ in sels_k)
        ck = tuple(ct_ref[hh:hh + 1, :] for hh in range(2))
        kpos = j * T + lax.broadcasted_iota(jnp.int32, (TQ, T), 1)

        def step(i, carry_):
            dk, dv, dc0, dc1 = carry_
            dcs = [dc0, dc1]
            off = pl.multiple_of(i * TQ, 8)
            q = q_ref[pl.ds(off, TQ), :]
            dov = do_ref[pl.ds(off, TQ), :]
            dd = dov * o_ref[pl.ds(off, TQ), :]
            lse_b = lse_ref[pl.ds(off, TQ), :]
            cblk = c_ref[pl.ds(off, TQ), :]
            qpos = i * TQ + lax.broadcasted_iota(jnp.int32, (TQ, T), 0)
            allowed = (kpos <= qpos) & (kpos >= pad)
            dq = jnp.zeros((TQ, LANES), F32)
            dcq = jnp.zeros((TQ, LANES), F32)
            for hh in range(2):
                sel = sels[hh]
                qhh = jnp.where(sel, q, jnp.zeros_like(q))
                doh = jnp.where(sel, dov, 0.0).astype(BF16)
                dsum = jnp.sum(jnp.where(sel, dd, 0.0), axis=1, keepdims=True)
                lse_h = jnp.sum(jnp.where(lane == hh * HEAD_DIM, lse_b, 0.0), axis=1, keepdims=True)
                cq = jnp.sum(jnp.where(lane == p * cs + hh, cblk, 0.0), axis=1, keepdims=True)
                s = _dot_nt(qhh, k) + cq - ck[hh]
                pr = jnp.where(allowed, jnp.exp(jnp.where(allowed, s, NEG_INF) - lse_h), 0.0)
                dp = _dot_nt(doh, v)
                ds = pr * (dp - dsum)
                dsb = ds.astype(BF16)
                dv = dv + _dot_tn(pr.astype(BF16), doh)
                dk = dk + _dot_tn(dsb, qhh)
                dq = dq + _dot(dsb, kh[hh])
                dcs[hh] = dcs[hh] - jnp.sum(ds, axis=0, keepdims=True)
                dcq = dcq + jnp.where(lane == p * cs + hh, jnp.sum(ds, axis=1, keepdims=True), 0.0)
            dq_ref[pl.ds(off, TQ), :] += dq
            dcq_ref[pl.ds(off, TQ), :] += dcq
            return dk, dv, dcs[0], dcs[1]

        init = (jnp.zeros((T, LANES), F32), jnp.zeros((T, LANES), F32),
                jnp.zeros((1, T), F32), jnp.zeros((1, T), F32))
        dk, dv, dc0, dc1 = lax.fori_loop((j * T) // TQ, nq, step, init)
        dk_ref[...] = dk
        dv_ref[...] = dv
        dct_ref[...] = jnp.zeros_like(dct_ref)
        dct_ref[0:1, :] = dc0
        dct_ref[1:2, :] = dc1

    blk = pl.BlockSpec((T, LANES), lambda p, j: (j, p))
    full = pl.BlockSpec((L, LANES), lambda p, j: (0, p))
    return _call(
        name, body, (npairs, nb),
        [full, blk, blk, pl.BlockSpec((L, LANES), lambda p, j: (0, 0)), pl.BlockSpec((cs, T), lambda p, j: (p, j)),
         full, full, full],
        [full, blk, blk, pl.BlockSpec((cs, T), lambda p, j: (p, j)), pl.BlockSpec((L, LANES), lambda p, j: (0, 0))],
        [jax.ShapeDtypeStruct((L, wf), F32)] * 3
        + [jax.ShapeDtypeStruct((LANES, L), F32), jax.ShapeDtypeStruct((L, LANES), F32)],
        [qn, kn, fv, c, ct, o, lse, do], carry=carry)


def _swa_scores(qh, kp, kc, slope, sink, i, pad):
    T = BLOCK
    t = lax.broadcasted_iota(jnp.int32, (T, T), 0)
    s_ = lax.broadcasted_iota(jnp.int32, (T, T), 1)
    dist_c = t - s_
    dist_p = dist_c + T
    ok_c = (dist_c >= 0) & (i * T + s_ >= pad)
    ok_p = (dist_p < T) & ((i - 1) * T + s_ >= pad)
    sp = jnp.where(ok_p, _dot_nt(qh, kp) - slope * dist_p.astype(F32), NEG_INF)
    sc = jnp.where(ok_c, _dot_nt(qh, kc) - slope * dist_c.astype(F32), NEG_INF)
    m = jnp.maximum(jnp.maximum(jnp.max(sp, axis=1, keepdims=True), jnp.max(sc, axis=1, keepdims=True)), sink)
    ep = jnp.exp(sp - m)
    ec = jnp.exp(sc - m)
    es = jnp.exp(sink - m)
    den = jnp.sum(ep, axis=1, keepdims=True) + jnp.sum(ec, axis=1, keepdims=True) + es
    return ep / den, ec / den, es / den


def _swa_fwd(name, cfg, sqn, skd, svd, sinks, slopes, carry=None):
    L, ws, group = cfg.L, cfg.ws, cfg.group
    T = BLOCK
    npairs = ws // LANES
    ppk = group // 2
    pad = cfg.pad

    def body(sink_ref, slope_ref, q_ref, kp_ref, kc_ref, vp_ref, vc_ref, o_ref):
        p = pl.program_id(0)
        i = pl.program_id(1)
        lo = _lane((T, LANES)) < HEAD_DIM
        q = q_ref[...]
        acc = jnp.zeros((T, LANES), F32)
        for hh in range(2):
            sel = lo if hh == 0 else jnp.logical_not(lo)
            qh = jnp.where(sel, q, jnp.zeros_like(q))
            pp, pc, _ = _swa_scores(qh, kp_ref[...], kc_ref[...], slope_ref[2 * p + hh], sink_ref[2 * p + hh], i, pad)
            vp = jnp.where(sel, vp_ref[...], jnp.zeros_like(q))
            vc = jnp.where(sel, vc_ref[...], jnp.zeros_like(q))
            acc = acc + _dot(pp.astype(BF16), vp) + _dot(pc.astype(BF16), vc)
        o_ref[...] = acc

    smem = pl.BlockSpec(memory_space=pltpu.SMEM)
    prev = pl.BlockSpec((T, LANES), lambda p, i: (jnp.maximum(i - 1, 0), p // ppk))
    cur = pl.BlockSpec((T, LANES), lambda p, i: (i, p // ppk))
    blk = pl.BlockSpec((T, LANES), lambda p, i: (i, p))
    return _call(name, body, (npairs, L // T), [smem, smem, blk, prev, cur, prev, cur], [blk],
                 [jax.ShapeDtypeStruct((L, ws), F32)], [sinks, slopes, sqn, skd, skd, svd, svd], carry=carry)


def _swa_bwd(name, cfg, sqn, skd, svd, sinks, slopes, o, do, carry=None):
    L, ws, group, hkv = cfg.L, cfg.ws, cfg.group, cfg.hkv
    T = BLOCK
    gw = group * HEAD_DIM
    ppk = group // 2
    pad = cfg.pad

    def body(sink_ref, slope_ref, q_ref, kp_ref, kc_ref, vp_ref, vc_ref, o_ref, do_ref,
             dq_ref, dk_ref, dv_ref, dsink_ref):
        kv = pl.program_id(0)
        i = pl.program_id(1)
        lane = _lane((T, LANES))
        lo = lane < HEAD_DIM

        @pl.when(i == 0)
        def _():
            dk_ref[...] = jnp.zeros_like(dk_ref)
            dv_ref[...] = jnp.zeros_like(dv_ref)
            dsink_ref[...] = jnp.zeros_like(dsink_ref)

        kp, kc, vp, vc = kp_ref[...], kc_ref[...], vp_ref[...], vc_ref[...]
        dkp = jnp.zeros((T, LANES), F32)
        dkc = jnp.zeros((T, LANES), F32)
        dvp = jnp.zeros((T, LANES), F32)
        dvc = jnp.zeros((T, LANES), F32)
        for pp_ in range(ppk):
            sl = slice(pp_ * LANES, (pp_ + 1) * LANES)
            q = q_ref[:, sl]
            dov = do_ref[:, sl]
            dd = dov * o_ref[:, sl]
            dq = jnp.zeros((T, LANES), F32)
            dsk = jnp.zeros((1, LANES), F32)
            for hh in range(2):
                sel = lo if hh == 0 else jnp.logical_not(lo)
                h = kv * group + 2 * pp_ + hh
                qh = jnp.where(sel, q, jnp.zeros_like(q))
                pp, pc, ps = _swa_scores(qh, kp, kc, slope_ref[h], sink_ref[h], i, pad)
                doh = jnp.where(sel, dov, 0.0).astype(BF16)
                dsum = jnp.sum(jnp.where(sel, dd, 0.0), axis=1, keepdims=True)
                dsp = (pp * (_dot_nt(doh, vp) - dsum)).astype(BF16)
                dsc = (pc * (_dot_nt(doh, vc) - dsum)).astype(BF16)
                khp = jnp.where(sel, kp, jnp.zeros_like(kp))
                khc = jnp.where(sel, kc, jnp.zeros_like(kc))
                dq = dq + _dot(dsp, khp) + _dot(dsc, khc)
                dkp = dkp + _dot_tn(dsp, qh)
                dkc = dkc + _dot_tn(dsc, qh)
                dvp = dvp + _dot_tn(pp.astype(BF16), doh)
                dvc = dvc + _dot_tn(pc.astype(BF16), doh)
                dsk = dsk + jnp.where(lane[0:1, :] == hh * HEAD_DIM, -jnp.sum(ps * dsum), 0.0)
            dq_ref[:, sl] = dq
            dsink_ref[8 * pp_:8 * pp_ + 1, :] += dsk

        cur = pl.multiple_of(i * T, T)
        dk_ref[pl.ds(cur, T), :] += dkc
        dv_ref[pl.ds(cur, T), :] += dvc

        @pl.when(i > 0)
        def _():
            prv = pl.multiple_of((i - 1) * T, T)
            dk_ref[pl.ds(prv, T), :] += dkp
            dv_ref[pl.ds(prv, T), :] += dvp

    smem = pl.BlockSpec(memory_space=pltpu.SMEM)
    prev = pl.BlockSpec((T, LANES), lambda kv, i: (jnp.maximum(i - 1, 0), kv))
    cur = pl.BlockSpec((T, LANES), lambda kv, i: (i, kv))
    qblk = pl.BlockSpec((T, gw), lambda kv, i: (i, kv))
    full = pl.BlockSpec((L, LANES), lambda kv, i: (0, kv))
    return _call(
        name, body, (hkv, L // T), [smem, smem, qblk, prev, cur, prev, cur, qblk, qblk],
        [qblk, full, full, pl.BlockSpec((8 * ppk, LANES), lambda kv, i: (kv, 0))],
        [jax.ShapeDtypeStruct((L, ws), F32), jax.ShapeDtypeStruct((L, 2 * LANES), F32),
         jax.ShapeDtypeStruct((L, 2 * LANES), F32), jax.ShapeDtypeStruct((8 * ppk * hkv, LANES), F32)],
        [sinks, slopes, sqn, skd, skd, svd, svd, o, do], carry=carry)


def _out_norm(name, cfg, o_fox, o_swa, g_fox, g_swa):
    L, wf, ws = cfg.L, cfg.wf, cfg.ws
    tm = _tile(L, 256, 16)

    def body(of_ref, os_ref, gf_ref, gs_ref, o_ref):
        for src, g_ref, lo_, w in ((of_ref, gf_ref, 0, wf), (os_ref, gs_ref, wf, ws)):
            x = src[...]
            r = lax.rsqrt(jnp.mean(x * x, axis=1, keepdims=True) + EPS)
            o_ref[:, lo_:lo_ + w] = (x * r * g_ref[...]).astype(BF16)

    return pl.pallas_call(
        body, name=name, grid=(L // tm,),
        in_specs=[pl.BlockSpec((tm, wf), lambda i: (i, 0)), pl.BlockSpec((tm, ws), lambda i: (i, 0)),
                  pl.BlockSpec((1, wf), lambda i: (0, 0)), pl.BlockSpec((1, ws), lambda i: (0, 0))],
        out_specs=pl.BlockSpec((tm, wf + ws), lambda i: (i, 0)),
        out_shape=jax.ShapeDtypeStruct((L, wf + ws), BF16),
        compiler_params=_cparams(("parallel",)),
    )(o_fox, o_swa, g_fox, g_swa)


def _out_norm_bwd(name, cfg, dcat, o_fox, o_swa, g_fox, g_swa):
    L, wf, ws = cfg.L, cfg.wf, cfg.ws
    tm = _tile(L, 256, 16)

    def body(d_ref, of_ref, os_ref, gf_ref, gs_ref, dof_ref, dos_ref, dgf_ref, dgs_ref):
        i = pl.program_id(0)

        @pl.when(i == 0)
        def _():
            dgf_ref[...] = jnp.zeros_like(dgf_ref)
            dgs_ref[...] = jnp.zeros_like(dgs_ref)

        for src, g_ref, dst, dg_ref, lo_, w in ((of_ref, gf_ref, dof_ref, dgf_ref, 0, wf),
                                                (os_ref, gs_ref, dos_ref, dgs_ref, wf, ws)):
            x = src[...]
            dy = d_ref[:, lo_:lo_ + w]
            r = lax.rsqrt(jnp.mean(x * x, axis=1, keepdims=True) + EPS)
            wv = dy * g_ref[...]
            proj = jnp.sum(wv * x, axis=1, keepdims=True) * (1.0 / w)
            dst[...] = r * wv - x * (r * r * r * proj)
            dg_ref[...] += jnp.sum(dy * x * r, axis=0, keepdims=True)

    return pl.pallas_call(
        body, name=name, grid=(L // tm,),
        in_specs=[pl.BlockSpec((tm, wf + ws), lambda i: (i, 0)), pl.BlockSpec((tm, wf), lambda i: (i, 0)),
                  pl.BlockSpec((tm, ws), lambda i: (i, 0)),
                  pl.BlockSpec((1, wf), lambda i: (0, 0)), pl.BlockSpec((1, ws), lambda i: (0, 0))],
        out_specs=[pl.BlockSpec((tm, wf), lambda i: (i, 0)), pl.BlockSpec((tm, ws), lambda i: (i, 0)),
                   pl.BlockSpec((1, wf), lambda i: (0, 0)), pl.BlockSpec((1, ws), lambda i: (0, 0))],
        out_shape=[jax.ShapeDtypeStruct((L, wf), F32), jax.ShapeDtypeStruct((L, ws), F32),
                   jax.ShapeDtypeStruct((1, wf), F32), jax.ShapeDtypeStruct((1, ws), F32)],
        compiler_params=_cparams(("arbitrary",)),
    )(dcat, o_fox, o_swa, g_fox, g_swa)


def _loss_head(name, h, target):
    L, D = h.shape
    T = BLOCK

    def body(h_ref, t_ref, loss_ref, dh_ref, dh16_ref):
        i = pl.program_id(0)

        @pl.when(i == 0)
        def _():
            loss_ref[...] = jnp.zeros_like(loss_ref)
            dh_ref[...] = jnp.zeros_like(dh_ref)
            dh16_ref[...] = jnp.zeros_like(dh16_ref)

        @pl.when(i > 0)
        def _():
            err = h_ref[...] - t_ref[...]
            dh = err * (1.0 / D)
            dh_ref[...] = dh
            dh16_ref[...] = dh.astype(BF16)
            loss_ref[...] += jnp.sum(err * err) * (0.5 / D)

    return pl.pallas_call(
        body, name=name, grid=(L // T,),
        in_specs=[pl.BlockSpec((T, D), lambda i: (i, 0)), pl.BlockSpec((T, D), lambda i: (jnp.maximum(i - 1, 0), 0))],
        out_specs=[pl.BlockSpec((8, LANES), lambda i: (0, 0)), pl.BlockSpec((T, D), lambda i: (i, 0)),
                   pl.BlockSpec((T, D), lambda i: (i, 0))],
        out_shape=[jax.ShapeDtypeStruct((8, LANES), F32), jax.ShapeDtypeStruct((L, D), F32),
                   jax.ShapeDtypeStruct((L, D), BF16)],
        compiler_params=_cparams(("arbitrary",)),
    )(h, target)


def _pair_add(name, grad, landed, parts, layer, core):
    _, _, r, C = grad.shape

    def body(s_ref, g_ref, l_ref, p_ref, o_ref):
        o_ref[...] = (g_ref[...].astype(F32) + l_ref[...].astype(F32)).astype(o_ref.dtype)

    return pl.pallas_call(
        body, name=name,
        grid_spec=pltpu.PrefetchScalarGridSpec(
            num_scalar_prefetch=1, grid=(4,),
            in_specs=[pl.BlockSpec((None, None, r, C), lambda k, s: (k, s[0], 0, 0)),
                      pl.BlockSpec((None, r, C), lambda k, s: (k, 0, 0)),
                      pl.BlockSpec(memory_space=pl.ANY)],
            out_specs=pl.BlockSpec((None, None, r, C), lambda k, s: (layer, k, 0, 0))),
        out_shape=jax.ShapeDtypeStruct(parts.shape, parts.dtype),
        input_output_aliases={3: 0},
        compiler_params=_cparams(("arbitrary",)),
    )(core, grad, landed, parts)


def _chip_sum(name, part, landed, chip):
    A, _, r, C = part.shape

    def body(s_ref, p_ref, l0_ref, l1_ref, l2_ref, o_ref):
        o_ref[...] = ((p_ref[...].astype(F32) + l0_ref[...].astype(F32))
                      + (l1_ref[...].astype(F32) + l2_ref[...].astype(F32)))

    def land(k):
        return pl.BlockSpec((None, None, r, C), lambda a, s, k=k: (k, a, 0, 0))

    return pl.pallas_call(
        body, name=name,
        grid_spec=pltpu.PrefetchScalarGridSpec(
            num_scalar_prefetch=1, grid=(A,),
            in_specs=[pl.BlockSpec((None, None, r, C), lambda a, s: (a, s[0], 0, 0)), land(0), land(1), land(2)],
            out_specs=pl.BlockSpec((None, r, C), lambda a, s: (a, 0, 0))),
        out_shape=jax.ShapeDtypeStruct((A, r, C), F32),
        compiler_params=_cparams(("parallel",)),
    )(chip, part, landed, landed, landed)


def _gather_sum(name, v):
    R = v.shape[0]

    def body(x_ref, o_ref, buf_ref, send_sems, recv_sems):
        x, y, c = _place()
        me, sibling = (x, y, c), (x, y, 1 - c)
        chips = [(1 - x, y), (x, 1 - y), (1 - x, 1 - y)]

        def rows(dev):
            px, py, pc = dev
            return buf_ref.at[4 * px + 2 * py + pc]

        def copy(k, block, to, src=None):
            return pltpu.make_async_remote_copy(
                src_ref=rows(block) if src is None else src, dst_ref=rows(block),
                send_sem=send_sems.at[k], recv_sem=recv_sems.at[k], device_id=to, device_id_type=MESH)

        first = [copy(0, me, sibling, src=x_ref)]
        first += [copy(1 + j, me, (*chip, c), src=x_ref) for j, chip in enumerate(chips)]
        for cp in first:
            cp.start()
        rows(me)[...] = x_ref[...]
        passed = [copy(4 + j, (*chip, c), sibling) for j, chip in enumerate(chips)]
        for j, chip in enumerate(chips):
            copy(1 + j, (*chip, c), me).wait_recv()
            passed[j].start()
        copy(0, sibling, me).wait_recv()
        for j, chip in enumerate(chips):
            copy(4 + j, (*chip, 1 - c), me).wait_recv()
        for cp in first + passed:
            cp.wait_send()
        acc = buf_ref[0]
        for d in range(1, N_DEV):
            acc = acc + buf_ref[d]
        o_ref[...] = acc

    vm = pl.BlockSpec(memory_space=pltpu.VMEM)
    return pl.pallas_call(
        body, name=name, in_specs=[vm], out_specs=vm,
        out_shape=jax.ShapeDtypeStruct((R, LANES), F32),
        scratch_shapes=[pltpu.VMEM((N_DEV, R, LANES), F32), pltpu.SemaphoreType.DMA((7,)), pltpu.SemaphoreType.DMA((7,))],
    )(v)


def _adamw(name, g, w, m, v):
    R, C = g.shape
    tr = _tile(R, max(8, (1 << 18) // max(C, 1) // 8 * 8), 8)

    def body(g_ref, w_ref, m_ref, v_ref, d_ref, nm_ref, nv_ref):
        gv = g_ref[...]
        nm = ADAM_B1 * m_ref[...] + (1.0 - ADAM_B1) * gv
        nv = ADAM_B2 * v_ref[...] + (1.0 - ADAM_B2) * (gv * gv)
        m_hat = nm / (1.0 - ADAM_B1 ** ADAM_STEP)
        v_hat = nv / (1.0 - ADAM_B2 ** ADAM_STEP)
        d_ref[...] = -ADAM_LR * (m_hat / (jnp.sqrt(v_hat) + ADAM_EPS) + ADAM_WD * w_ref[...])
        nm_ref[...] = nm
        nv_ref[...] = nv

    blk = pl.BlockSpec((tr, C), lambda i: (i, 0))
    shp = jax.ShapeDtypeStruct((R, C), F32)
    return pl.pallas_call(
        body, name=name, grid=(R // tr,),
        in_specs=[blk] * 4, out_specs=[blk] * 3, out_shape=[shp] * 3,
        compiler_params=_cparams(("parallel",)),
    )(g, w, m, v)


def _adamw_nd(name, g, w, m, v):
    shape = w.shape
    flat = [a.reshape(-1, shape[-1]) for a in (g, w, m, v)]
    return tuple(o.reshape(shape) for o in _adamw(name, *flat))


def _scatter_heads(cfg, vals):
    v = jnp.pad(vals.reshape(cfg.hf // 2, 2), ((0, 0), (0, cfg.cs - 2)))
    return v.reshape(1, LANES)


def _gather_heads(cfg, row):
    return row.reshape(cfg.hf // 2, cfg.cs)[:, :2].reshape(cfg.hf)


def _permute_w_in(cfg, w_in_t):
    wf, hf = cfg.wf, cfg.hf
    o = 3 * wf
    cols = w_in_t.shape[1]
    fz = w_in_t[o:o + hf].reshape(hf // 2, 2, cols)
    fz_blk = jnp.pad(fz, ((0, 0), (0, cfg.cs - 2), (0, 0))).reshape(LANES, cols)
    return jnp.concatenate([w_in_t[:o], w_in_t[o + hf:], fz_blk], axis=0)


def _unpermute_dw_in(cfg, dwp):
    wf, hf = cfg.wf, cfg.hf
    o = 3 * wf
    cols = dwp.shape[1]
    fz = dwp[cfg.o_fz:].reshape(hf // 2, cfg.cs, cols)[:, :2].reshape(hf, cols)
    return jnp.concatenate([dwp[:o], fz, dwp[o:cfg.o_fz]], axis=0)


def _pair_gain(g):
    return jnp.tile(g, 2)[None]


def _fold_pair(dg):
    return dg[0, :HEAD_DIM] + dg[0, HEAD_DIM:]


def kernel(x, meta_tokens, ffn1_norm, ffn1_w_gate, ffn1_w_up, ffn1_w_down, mix_norm, w_in, b_forget, fox_q_norm, fox_k_norm, swa_q_norm, swa_k_norm, swa_sinks, fox_out_norm, swa_out_norm, w_out, ffn2_norm, ffn2_w_gate, ffn2_w_up, ffn2_w_down, loss_target, m_meta_tokens, m_ffn1_norm, m_ffn1_w_gate, m_ffn1_w_up, m_ffn1_w_down, m_mix_norm, m_w_in, m_b_forget, m_fox_q_norm, m_fox_k_norm, m_swa_q_norm, m_swa_k_norm, m_swa_sinks, m_fox_out_norm, m_swa_out_norm, m_w_out, m_ffn2_norm, m_ffn2_w_gate, m_ffn2_w_up, m_ffn2_w_down, v_meta_tokens, v_ffn1_norm, v_ffn1_w_gate, v_ffn1_w_up, v_ffn1_w_down, v_mix_norm, v_w_in, v_b_forget, v_fox_q_norm, v_fox_k_norm, v_swa_q_norm, v_swa_k_norm, v_swa_sinks, v_fox_out_norm, v_swa_out_norm, v_w_out, v_ffn2_norm, v_ffn2_w_gate, v_ffn2_w_up, v_ffn2_w_down):
    weights = dict(meta_tokens=meta_tokens, ffn1_norm=ffn1_norm, ffn1_w_gate=ffn1_w_gate, ffn1_w_up=ffn1_w_up,
                   ffn1_w_down=ffn1_w_down, mix_norm=mix_norm, w_in=w_in, b_forget=b_forget, fox_q_norm=fox_q_norm,
                   fox_k_norm=fox_k_norm, swa_q_norm=swa_q_norm, swa_k_norm=swa_k_norm, swa_sinks=swa_sinks,
                   fox_out_norm=fox_out_norm, swa_out_norm=swa_out_norm, w_out=w_out, ffn2_norm=ffn2_norm,
                   ffn2_w_gate=ffn2_w_gate, ffn2_w_up=ffn2_w_up, ffn2_w_down=ffn2_w_down)
    mom_m = dict(meta_tokens=m_meta_tokens, ffn1_norm=m_ffn1_norm, ffn1_w_gate=m_ffn1_w_gate, ffn1_w_up=m_ffn1_w_up,
                 ffn1_w_down=m_ffn1_w_down, mix_norm=m_mix_norm, w_in=m_w_in, b_forget=m_b_forget,
                 fox_q_norm=m_fox_q_norm, fox_k_norm=m_fox_k_norm, swa_q_norm=m_swa_q_norm, swa_k_norm=m_swa_k_norm,
                 swa_sinks=m_swa_sinks, fox_out_norm=m_fox_out_norm, swa_out_norm=m_swa_out_norm, w_out=m_w_out,
                 ffn2_norm=m_ffn2_norm, ffn2_w_gate=m_ffn2_w_gate, ffn2_w_up=m_ffn2_w_up, ffn2_w_down=m_ffn2_w_down)
    mom_v = dict(meta_tokens=v_meta_tokens, ffn1_norm=v_ffn1_norm, ffn1_w_gate=v_ffn1_w_gate, ffn1_w_up=v_ffn1_w_up,
                 ffn1_w_down=v_ffn1_w_down, mix_norm=v_mix_norm, w_in=v_w_in, b_forget=v_b_forget,
                 fox_q_norm=v_fox_q_norm, fox_k_norm=v_fox_k_norm, swa_q_norm=v_swa_q_norm, swa_k_norm=v_swa_k_norm,
                 swa_sinks=v_swa_sinks, fox_out_norm=v_fox_out_norm, swa_out_norm=v_swa_out_norm, w_out=v_w_out,
                 ffn2_norm=v_ffn2_norm, ffn2_w_gate=v_ffn2_w_gate, ffn2_w_up=v_ffn2_w_up, ffn2_w_down=v_ffn2_w_down)
    names = list(weights)

    _, S, D = x.shape
    depth = ffn1_norm.shape[0]
    hf, hs = b_forget.shape[1], swa_sinks.shape[1]
    U = w_in.shape[2] * N_DEV
    hkv = (U - 3 * HEAD_DIM * hf - hf - HEAD_DIM * hs) // (2 * HEAD_DIM)
    cfg = _Cfg(S, D, hf, hs, hkv)
    n_meta = meta_tokens.shape[0]
    assert n_meta == cfg.n_meta
    x_idx, y_idx, c_idx = _place()
    chip_idx = 2 * x_idx + y_idx
    dev_idx = 2 * chip_idx + c_idx
    core_s = jnp.reshape(c_idx, (1,)).astype(jnp.int32)
    chip_s = jnp.reshape(chip_idx, (1,)).astype(jnp.int32)

    col_sharded = ("ffn1_w_gate", "ffn1_w_up", "w_in", "ffn2_w_gate", "ffn2_w_up")
    use_order = ("ffn1_w_gate", "ffn1_w_up", "ffn1_w_down", "w_in", "w_out", "ffn2_w_gate", "ffn2_w_up", "ffn2_w_down")

    def shard(key):
        k, l = key
        if k == "meta_tokens":
            return meta_tokens
        w = weights[k][l]
        return (w.T if k in col_sharded else w).astype(BF16)

    waiting = [("meta_tokens", 0)] + [(k, l) for l in range(depth) for k in use_order]
    halfway = []
    gathered = {}

    def fwd_carry(n_first):
        cy = _Carry()
        second = [(key, _gather_second(cy, buf)) for key, buf in halfway]
        first = [(key, _gather_first(cy, shard(key))) for key in waiting[:n_first]]
        del waiting[:n_first]
        halfway.clear()
        return cy, (first, second)

    def fwd_absorb(extra, plan):
        first, second = plan
        for key, idx in second:
            gathered[key] = extra[idx]
        for key, idx in first:
            halfway.append((key, extra[idx]))

    def weight(k, l):
        key = (k, l)
        while key not in gathered:
            n = 0 if any(key == hk for hk, _ in halfway) else waiting.index(key) + 1
            cy, plan = fwd_carry(n)
            fwd_absorb(_comm_only("weights_gather", cy), plan)
        g = gathered[key]
        return g.reshape(-1, g.shape[-1])

    weight("ffn1_w_down", 0)
    meta_full = jnp.swapaxes(weight("meta_tokens", 0).reshape(N_DEV, n_meta, -1), 0, 1).reshape(n_meta, D)
    slopes = jnp.asarray(2.0 ** (-8.0 * np.arange(1, hs + 1) / hs), dtype=F32)

    h = jnp.concatenate([jnp.zeros((cfg.pad, D), F32), meta_full, x[0]], axis=0)
    saved = []
    w_in_p = [None] * depth

    def mm_f(name, a, b, n_first=1, **kw):
        cy, plan = fwd_carry(n_first)
        out, extra = _matmul(name, a, b, carry=cy, **kw)
        fwd_absorb(extra, plan)
        return out

    def ffn_fwd(tag, l, h_in, norm, wg, wu, wd):
        xn = _rmsnorm_fwd(f"{tag}_norm", h_in, norm[l][None])
        wg_t, wu_t = weight(wg, l), weight(wu, l)
        cy, plan = fwd_carry(1)
        (gate, up, act), extra = _ffn_up(f"{tag}_up", xn, wg_t, wu_t, carry=cy)
        fwd_absorb(extra, plan)
        h_out = mm_f(f"{tag}_down", act, weight(wd, l), scale=0.5, residual=h_in, tm=544, tn=1024, tk=2816)
        return h_out, (xn, gate, up)

    for l in range(depth):
        st = {"h0": h}
        h, st["ffn1"] = ffn_fwd("ffn1", l, h, ffn1_norm, "ffn1_w_gate", "ffn1_w_up", "ffn1_w_down")
        st["h1"] = h
        xn = _rmsnorm_fwd("mix_norm", h, mix_norm[l][None])
        w_in_p[l] = _permute_w_in(cfg, weight("w_in", l))
        u = mm_f("mix_in", xn, w_in_p[l], trans_b=True, tm=544, tn=640, tk=2048)
        gq, gk = _pair_gain(fox_q_norm[l]), _pair_gain(fox_k_norm[l])
        gsq, gsk = _pair_gain(swa_q_norm[l]), _pair_gain(swa_k_norm[l])
        bias = _scatter_heads(cfg, b_forget[l])
        qn, kn, fv, sqn, skd, svd, c, ct = _mix_prep("mix_prep", cfg, u, gq, gk, gsq, gsk, bias)
        cy, plan = fwd_carry(2)
        (o_fox, lse), extra = _fox_fwd("fox_fwd", cfg, qn, kn, fv, c, ct, carry=cy)
        fwd_absorb(extra, plan)
        cy, plan = fwd_carry(1)
        (o_swa,), extra = _swa_fwd("swa_fwd", cfg, sqn, skd, svd, swa_sinks[l], slopes, carry=cy)
        fwd_absorb(extra, plan)
        o_cat = _out_norm("out_norm", cfg, o_fox, o_swa, fox_out_norm[l][None], swa_out_norm[l][None])
        st["mix"] = (xn, u, gq, gk, gsq, gsk, bias, qn, kn, fv, sqn, skd, svd, c, ct, o_fox, lse, o_swa, o_cat)
        h = mm_f("mix_out", o_cat, weight("w_out", l), n_first=0, residual=h, tm=544, tn=1024, tk=2048)
        st["h2"] = h
        h, st["ffn2"] = ffn_fwd("ffn2", l, h, ffn2_norm, "ffn2_w_gate", "ffn2_w_up", "ffn2_w_down")
        saved.append(st)

    loss_blk, dh, dh16 = _loss_head("loss_head", h, loss_target[0])

    small = {k: [None] * depth for k in names if k not in use_order and k != "meta_tokens"}
    parts, landing = {}, {}
    to_sibling, to_chips = [], []

    def bwd_carry(n_chips):
        cy = _Carry()
        t1, t3 = [], []
        for k, l in list(to_chips):
            if len(t3) < n_chips and all(k != k3 for k3, _ in t3):
                to_chips.remove((k, l))
                t3.append((k, _scatter_chips(cy, parts[k], landing[k], l)))
        while to_sibling:
            k, l, g = to_sibling.pop(0)
            t1.append((k, l, g, _scatter_sibling(cy, g)))
        return cy, (t1, t3)

    def bwd_absorb(extra, plan):
        t1, t3 = plan
        for k, idx in t3:
            landing[k] = extra[idx]
        for k, l, g, idx in t1:
            parts[k] = _pair_add("grads_pair_add", g, extra[idx], parts[k], l, core_s)
            to_chips.append((k, l))

    def emit_grad(k, l, dw):
        r, C = dw.shape[0] // N_DEV, dw.shape[1]
        if k not in parts:
            parts[k] = lax.empty((depth, 4, r, C), BF16)
            landing[k] = lax.empty((3, depth, r, C), BF16)
        to_sibling.append((k, l, dw.reshape(4, 2, r, C)))

    def mm_b(name, a, b, n_chips=0, **kw):
        cy, plan = bwd_carry(n_chips)
        out, extra = _matmul(name, a, b, carry=cy, **kw)
        bwd_absorb(extra, plan)
        return out

    def ffn_bwd(tag, l, dh_out, dh_out16, h_in, st_, norm, wg, wu, wd):
        xn, gate, up = st_
        short = 1 if l == 0 else 0
        cy, plan = bwd_carry(1)
        (dgate, dup, act), extra = _ffn_bwd_act(f"{tag}_dact", dh_out16, weight(wd, l), gate, up, carry=cy)
        bwd_absorb(extra, plan)
        emit_grad(wd, l, mm_b(f"{tag}_dwd", act, dh_out16, short, trans_a=True, scale=0.5, out_dtype=BF16,
                              tm=512, tn=1024, tk=2176))
        emit_grad(wg, l, mm_b(f"{tag}_dwg", dgate, xn, short, trans_a=True, out_dtype=BF16, tm=512, tn=1024, tk=2176))
        emit_grad(wu, l, mm_b(f"{tag}_dwu", dup, xn, short, trans_a=True, out_dtype=BF16, tm=512, tn=1024, tk=2176))
        dxn = mm_b(f"{tag}_dxn", dgate, weight(wg, l), 1, pair2=(dup, weight(wu, l)), tm=544, tn=1024, tk=1408)
        dh_in, dh_in16, dg = _rmsnorm_bwd(f"{tag}_dnorm", dxn, h_in, norm[l][None], dh_out)
        return dh_in, dh_in16, dg[0]

    for l in reversed(range(depth)):
        st = saved[l]
        dh, dh16, small["ffn2_norm"][l] = ffn_bwd("ffn2", l, dh, dh16, st["h2"], st["ffn2"], ffn2_norm,
                                                   "ffn2_w_gate", "ffn2_w_up", "ffn2_w_down")
        xn, u, gq, gk, gsq, gsk, bias, qn, kn, fv, sqn, skd, svd, c, ct, o_fox, lse, o_swa, o_cat = st["mix"]
        dcat = mm_b("mix_dcat", dh16, weight("w_out", l), int(l == 0), trans_b=True, tm=544, tn=1024, tk=2048)
        emit_grad("w_out", l, mm_b("mix_dwout", o_cat, dh16, int(l == 0), trans_a=True, out_dtype=BF16, tm=512, tn=1024, tk=2176))
        do_fox, do_swa, dgf, dgs = _out_norm_bwd("out_norm_bwd", cfg, dcat, o_fox, o_swa,
                                                 fox_out_norm[l][None], swa_out_norm[l][None])
        small["fox_out_norm"][l], small["swa_out_norm"][l] = dgf[0], dgs[0]
        cy, plan = bwd_carry(3)
        (dqn, dkn, dfv, dct, dcq), extra = _fox_bwd("fox_bwd", cfg, qn, kn, fv, c, ct, o_fox, lse, do_fox, carry=cy)
        bwd_absorb(extra, plan)
        cy, plan = bwd_carry(1)
        (dsqn, dskd, dsvd, dsink), extra = _swa_bwd("swa_bwd", cfg, sqn, skd, svd, swa_sinks[l], slopes, o_swa, do_swa,
                                                    carry=cy)
        bwd_absorb(extra, plan)
        small["swa_sinks"][l] = dsink.reshape(hs // 2, 8, LANES)[:, 0, ::HEAD_DIM].reshape(hs)
        du, dgq, dgk, dgsq, dgsk, db = _mix_prep_bwd("mix_prep_bwd", cfg, u, gq, gk, gsq, gsk, bias,
                                                     dqn, dkn, dfv, dsqn, dskd, dsvd, dct, dcq)
        small["fox_q_norm"][l], small["fox_k_norm"][l] = _fold_pair(dgq), _fold_pair(dgk)
        small["swa_q_norm"][l], small["swa_k_norm"][l] = _fold_pair(dgsq), _fold_pair(dgsk)
        small["b_forget"][l] = _gather_heads(cfg, db[0])
        dwp = mm_b("mix_dwin", du, xn, int(l == 0), trans_a=True, out_dtype=BF16, tm=640, tn=1024, tk=2176)
        emit_grad("w_in", l, _unpermute_dw_in(cfg, dwp))
        dxn = mm_b("mix_dxn", du, w_in_p[l], int(l == 0), tm=544, tn=1024, tk=4480)
        dh, dh16, dg = _rmsnorm_bwd("mix_dnorm", dxn, st["h1"], mix_norm[l][None], dh)
        small["mix_norm"][l] = dg[0]
        dh, dh16, small["ffn1_norm"][l] = ffn_bwd("ffn1", l, dh, dh16, st["h0"], st["ffn1"], ffn1_norm,
                                                   "ffn1_w_gate", "ffn1_w_up", "ffn1_w_down")

    grad_x = dh[BLOCK:][None]
    dmeta = dh[cfg.pad:BLOCK]

    while to_sibling or to_chips:
        cy, plan = bwd_carry(len(use_order))
        bwd_absorb(_comm_only("grads_scatter", cy), plan)

    grads = {}
    for k in use_order:
        g = _chip_sum("grads_chip_sum", parts[k], landing[k], chip_s)
        grads[k] = jnp.swapaxes(g, 1, 2) if k in col_sharded else g

    small_names = list(small)
    pieces = [loss_blk[0, :1], dmeta.reshape(-1)] + [jnp.stack(small[k]).reshape(-1) for k in small_names]
    sizes = [int(p.shape[0]) for p in pieces]
    total = sum(sizes)
    padded = -(-total // (8 * LANES)) * (8 * LANES)
    vec = jnp.concatenate(pieces + [jnp.zeros((padded - total,), F32)]).reshape(-1, LANES)
    summed = _gather_sum("small_gather_sum", vec).reshape(-1)
    offs = np.cumsum([0] + sizes)
    loss = summed[0]
    dmeta_full = summed[offs[1]:offs[2]].reshape(n_meta, D)
    mcols = meta_tokens.shape[1]
    grads["meta_tokens"] = lax.dynamic_slice_in_dim(dmeta_full, dev_idx * mcols, mcols, axis=1)
    for n_, k in enumerate(small_names):
        grads[k] = summed[offs[2 + n_]:offs[3 + n_]].reshape(weights[k].shape)

    delta, new_m, new_v = {}, {}, {}
    for k in names:
        delta[k], new_m[k], new_v[k] = _adamw_nd("adamw", grads[k], weights[k], mom_m[k], mom_v[k])

    return (loss, grad_x, *[grads[k] for k in names], *[delta[k] for k in names],
            *[new_m[k] for k in names], *[new_v[k] for k in names])
```

```python
import functools

import numpy as np
import jax
import jax.numpy as jnp
from jax import lax
from jax.experimental import pallas as pl
from jax.experimental.pallas import tpu as pltpu

F32 = jnp.float32
BF16 = jnp.bfloat16
MESH = pl.DeviceIdType.MESH

HEAD_DIM = 64
BLOCK = 128
LANES = 128
N_DEV = 8
EPS = 1e-6
NEG_INF = -1e30
SCALE = HEAD_DIM ** -0.5

ADAM_LR = 0.001
ADAM_B1 = 0.9
ADAM_B2 = 0.999
ADAM_EPS = 1e-08
ADAM_WD = 0.01
ADAM_STEP = 10

VMEM_BYTES_V7X = 64 * 1024 * 1024
VMEM_LIMIT = VMEM_BYTES_V7X * 3 // 4

NT = (((1,), (1,)), ((), ()))
TN = (((0,), (0,)), ((), ()))
HI = lax.Precision.HIGHEST


def _cparams(sem=None, vmem=VMEM_LIMIT):
    return pltpu.CompilerParams(dimension_semantics=sem, vmem_limit_bytes=vmem)


def _tile(n, pref, mult):
    best = None
    for t in range(mult, min(n, pref) + 1, mult):
        if n % t == 0:
            best = t
    return best if best is not None else n


def _dot(a, b):
    return jnp.dot(a, b, preferred_element_type=F32)


def _dot_nt(a, b):
    return lax.dot_general(a, b, NT, preferred_element_type=F32)


def _dot_tn(a, b):
    return lax.dot_general(a, b, TN, preferred_element_type=F32)


def _lane(shape):
    return lax.broadcasted_iota(jnp.int32, shape, len(shape) - 1)


def _half_sum(x, lo):
    s0 = jnp.sum(jnp.where(lo, x, 0.0), axis=1, keepdims=True)
    s1 = jnp.sum(jnp.where(lo, 0.0, x), axis=1, keepdims=True)
    return jnp.where(lo, s0, s1)


def _sigmoid(x):
    return 1.0 / (1.0 + jnp.exp(-x))


def _place():
    return lax.axis_index("x"), lax.axis_index("y"), lax.axis_index("c")


def _peers(x, y, c):
    return [(x, y, 1 - c), (1 - x, y, c), (x, 1 - y, c), (1 - x, 1 - y, c)]


def _dev_index(dev):
    px, py, pc = dev
    return 4 * px + 2 * py + pc


class _Carry:
    def __init__(self):
        self.ins, self.outs, self.alias, self.items = [], [], {}, []
        self.nsem = self.nloc = 0

    def add(self, ins, outs, alias, nsem, nloc, build):
        i0, o0 = len(self.ins), len(self.outs)
        for src, dst in alias.items():
            self.alias[i0 + src] = o0 + dst
        self.items.append((i0, len(ins), o0, len(outs), self.nsem, self.nloc, build))
        self.ins += ins
        self.outs += outs
        self.nsem += nsem
        self.nloc += nloc
        return list(range(o0, o0 + len(outs)))

    def build(self, in_refs, out_refs, ssem, rsem, lsem):
        ops = []
        for i0, ni, o0, no, s0, l0, fn in self.items:
            ops.append(fn(in_refs[i0:i0 + ni], out_refs[o0:o0 + no],
                          lambda k, s0=s0: (ssem.at[s0 + k], rsem.at[s0 + k]), lambda k, l0=l0: lsem.at[l0 + k]))
        return ops


def _remote(src, dst, sems, dev):
    return pltpu.make_async_remote_copy(src_ref=src, dst_ref=dst, send_sem=sems[0], recv_sem=sems[1],
                                        device_id=dev, device_id_type=MESH)


def _gather_first(carry, shard):
    def build(ins, outs, sems, locs):
        src, buf = ins[0], outs[0]
        x, y, c = _place()
        me = _dev_index((x, y, c))
        peers = _peers(x, y, c)
        local = pltpu.make_async_copy(src, buf.at[me], locs(0))
        sends = [_remote(src, buf.at[me], sems(k), dev) for k, dev in enumerate(peers)]
        recvs = [_remote(src, buf.at[_dev_index(dev)], sems(k), dev) for k, dev in enumerate(peers)]

        def start():
            local.start()
            for cp in sends:
                cp.start()

        def wait():
            for cp in sends:
                cp.wait_send()
            for cp in recvs:
                cp.wait_recv()
            local.wait()

        return start, wait

    return carry.add([shard], [jax.ShapeDtypeStruct((N_DEV,) + shard.shape, shard.dtype)], {}, 4, 1, build)[0]


def _gather_second(carry, buf):
    def build(ins, outs, sems, locs):
        b = outs[0]
        x, y, c = _place()
        chips = _peers(x, y, c)[1:]
        sends = [_remote(b.at[_dev_index(dev)], b.at[_dev_index(dev)], sems(k), (x, y, 1 - c))
                 for k, dev in enumerate(chips)]
        recvs = [_remote(b.at[_dev_index(dev)], b.at[_dev_index((dev[0], dev[1], 1 - c))], sems(k), (x, y, 1 - c))
                 for k, dev in enumerate(chips)]

        def start():
            for cp in sends:
                cp.start()

        def wait():
            for cp in sends:
                cp.wait_send()
            for cp in recvs:
                cp.wait_recv()

        return start, wait

    return carry.add([buf], [jax.ShapeDtypeStruct(buf.shape, buf.dtype)], {0: 0}, 3, 0, build)[0]


def _scatter_sibling(carry, grad):
    def build(ins, outs, sems, locs):
        x, y, c = _place()
        cp = _remote(ins[0].at[:, 1 - c], outs[0], sems(0), (x, y, 1 - c))
        return cp.start, cp.wait

    shape = (grad.shape[0],) + grad.shape[2:]
    return carry.add([grad], [jax.ShapeDtypeStruct(shape, grad.dtype)], {}, 1, 0, build)[0]


def _scatter_chips(carry, parts, landing, layer):
    def build(ins, outs, sems, locs):
        x, y, c = _place()
        cps = [_remote(ins[0].at[layer, 2 * dev[0] + dev[1]], outs[0].at[k, layer], sems(k), dev)
               for k, dev in enumerate(_peers(x, y, c)[1:])]

        def start():
            for cp in cps:
                cp.start()

        def wait():
            for cp in cps:
                cp.wait()

        return start, wait

    return carry.add([parts, landing], [jax.ShapeDtypeStruct(landing.shape, landing.dtype)], {1: 0}, 3, 0, build)[0]


def _call(name, body, grid, in_specs, out_specs, out_shape, args, scratch=(), carry=None):
    ni, no, ns = len(args), len(out_shape), len(scratch)
    if carry is None or not carry.items:
        res = pl.pallas_call(
            body, name=name, grid=grid, in_specs=list(in_specs), out_specs=list(out_specs), out_shape=list(out_shape),
            scratch_shapes=list(scratch), compiler_params=_cparams(("arbitrary",) * len(grid)))(*args)
        return list(res), []
    nci, nco = len(carry.ins), len(carry.outs)

    def full_body(*refs):
        c_in = refs[ni:ni + nci]
        c_out = refs[ni + nci + no:ni + nci + no + nco]
        sc = refs[ni + nci + no + nco:]
        ops = carry.build(c_in, c_out, sc[ns], sc[ns + 1], sc[ns + 2])
        first = last = None
        for d, n in enumerate(grid):
            pid = pl.program_id(d)
            first = (pid == 0) if first is None else first & (pid == 0)
            last = (pid == n - 1) if last is None else last & (pid == n - 1)

        @pl.when(first)
        def _():
            for start, _w in ops:
                start()

        body(*refs[:ni], *refs[ni + nci:ni + nci + no], *sc[:ns])

        @pl.when(last)
        def _():
            for _s, wait in ops:
                wait()

    hbm = pl.BlockSpec(memory_space=pl.ANY)
    res = pl.pallas_call(
        full_body, name=name, grid=grid,
        in_specs=list(in_specs) + [hbm] * nci, out_specs=list(out_specs) + [hbm] * nco,
        out_shape=list(out_shape) + list(carry.outs),
        input_output_aliases={ni + s: no + d for s, d in carry.alias.items()},
        scratch_shapes=list(scratch) + [pltpu.SemaphoreType.DMA((carry.nsem,)), pltpu.SemaphoreType.DMA((carry.nsem,)),
                                        pltpu.SemaphoreType.DMA((max(carry.nloc, 1),))],
        compiler_params=_cparams(("arbitrary",) * len(grid)))(*args, *carry.ins)
    return list(res[:no]), list(res[no:])


def _comm_only(name, carry):
    return _call(name, lambda *refs: None, (1,), [], [], [], [], carry=carry)[1]


def _matmul(name, a, b, *, pair2=None, trans_a=False, trans_b=False, out_dtype=F32, scale=None, residual=None,
            tm=512, tn=512, tk=512, carry=None):
    if trans_a:
        K, M = a.shape
    else:
        M, K = a.shape
    if trans_b:
        N, Kb = b.shape
    else:
        Kb, N = b.shape
    assert K == Kb, (name, a.shape, b.shape)
    tm = _tile(M, tm, LANES if trans_a else 16)
    tn = _tile(N, tn, LANES)
    tk = _tile(K, tk, 16 if (trans_a and not trans_b) else LANES)
    nk = K // tk
    dims = (((0 if trans_a else 1,), (1 if trans_b else 0,)), ((), ()))
    pairs = [(a, b)] + ([pair2] if pair2 is not None else [])
    npair = len(pairs)

    def body(*refs):
        ab = refs[:2 * npair]
        pos = 2 * npair
        r_ref = None
        if residual is not None:
            r_ref = refs[pos]
            pos += 1
        o_ref = refs[pos]

        def partial():
            t = None
            for q in range(npair):
                d = lax.dot_general(ab[2 * q][...].astype(BF16), ab[2 * q + 1][...].astype(BF16), dims,
                                    preferred_element_type=F32)
                t = d if t is None else t + d
            return t

        def finish(r):
            if scale is not None:
                r = r * scale
            if r_ref is not None:
                r = r + r_ref[...].astype(F32)
            o_ref[...] = r.astype(o_ref.dtype)

        if nk == 1:
            finish(partial())
            return
        acc_ref = refs[pos + 1]
        k = pl.program_id(2)

        @pl.when(k == 0)
        def _():
            acc_ref[...] = partial()

        @pl.when(k > 0)
        def _():
            acc_ref[...] += partial()

        @pl.when(k == nk - 1)
        def _():
            finish(acc_ref[...])

    a_spec = pl.BlockSpec((tk, tm), lambda i, j, k: (k, i)) if trans_a else pl.BlockSpec((tm, tk), lambda i, j, k: (i, k))
    b_spec = pl.BlockSpec((tn, tk), lambda i, j, k: (j, k)) if trans_b else pl.BlockSpec((tk, tn), lambda i, j, k: (k, j))
    in_specs, args = [], []
    for pa, pb in pairs:
        in_specs += [a_spec, b_spec]
        args += [pa, pb]
    if residual is not None:
        in_specs.append(pl.BlockSpec((tm, tn), lambda i, j, k: (i, j)))
        args.append(residual)
    res, extra = _call(
        name, body, (M // tm, N // tn, nk), in_specs, [pl.BlockSpec((tm, tn), lambda i, j, k: (i, j))],
        [jax.ShapeDtypeStruct((M, N), out_dtype)], args,
        scratch=[pltpu.VMEM((tm, tn), F32)] if nk > 1 else [], carry=carry)
    return res[0], extra


def _rmsnorm_fwd(name, h, g):
    L, D = h.shape
    tm = _tile(L, 256, 16)

    def body(h_ref, g_ref, o_ref):
        x = h_ref[...]
        r = lax.rsqrt(jnp.mean(x * x, axis=1, keepdims=True) + EPS)
        o_ref[...] = (x * r * g_ref[...]).astype(o_ref.dtype)

    return pl.pallas_call(
        body, name=name, grid=(L // tm,),
        in_specs=[pl.BlockSpec((tm, D), lambda i: (i, 0)), pl.BlockSpec((1, D), lambda i: (0, 0))],
        out_specs=pl.BlockSpec((tm, D), lambda i: (i, 0)),
        out_shape=jax.ShapeDtypeStruct((L, D), BF16),
        compiler_params=_cparams(("parallel",)),
    )(h, g)


def _rmsnorm_bwd(name, dy, h, g, dres):
    L, D = h.shape
    tm = _tile(L, 256, 16)

    def body(dy_ref, h_ref, g_ref, dres_ref, dh_ref, dh16_ref, dg_ref):
        i = pl.program_id(0)
        x = h_ref[...]
        dyv = dy_ref[...].astype(F32)
        r = lax.rsqrt(jnp.mean(x * x, axis=1, keepdims=True) + EPS)
        w = dyv * g_ref[...]
        proj = jnp.sum(w * x, axis=1, keepdims=True) * (1.0 / D)
        dh = dres_ref[...] + r * w - x * (r * r * r * proj)
        dh_ref[...] = dh
        dh16_ref[...] = dh.astype(BF16)

        @pl.when(i == 0)
        def _():
            dg_ref[...] = jnp.zeros_like(dg_ref)

        dg_ref[...] += jnp.sum(dyv * x * r, axis=0, keepdims=True)

    return pl.pallas_call(
        body, name=name, grid=(L // tm,),
        in_specs=[pl.BlockSpec((tm, D), lambda i: (i, 0)), pl.BlockSpec((tm, D), lambda i: (i, 0)),
                  pl.BlockSpec((1, D), lambda i: (0, 0)), pl.BlockSpec((tm, D), lambda i: (i, 0))],
        out_specs=[pl.BlockSpec((tm, D), lambda i: (i, 0)), pl.BlockSpec((tm, D), lambda i: (i, 0)),
                   pl.BlockSpec((1, D), lambda i: (0, 0))],
        out_shape=[jax.ShapeDtypeStruct((L, D), F32), jax.ShapeDtypeStruct((L, D), BF16),
                   jax.ShapeDtypeStruct((1, D), F32)],
        compiler_params=_cparams(("arbitrary",)),
    )(dy, h, g, dres)


def _ffn_up(name, xn, wgT, wuT, carry=None):
    L, D = xn.shape
    F = wgT.shape[0]
    tm = _tile(L, 544, 16)
    tn = _tile(F, 512, LANES)

    def body(x_ref, wg_ref, wu_ref, g_ref, u_ref, a_ref):
        x = x_ref[...]
        g = _dot_nt(x, wg_ref[...])
        u = _dot_nt(x, wu_ref[...])
        g_ref[...] = g.astype(BF16)
        u_ref[...] = u.astype(BF16)
        a_ref[...] = (g * _sigmoid(g) * u).astype(BF16)

    o_spec = pl.BlockSpec((tm, tn), lambda i, j: (i, j))
    o_shape = jax.ShapeDtypeStruct((L, F), BF16)
    return _call(
        name, body, (L // tm, F // tn),
        [pl.BlockSpec((tm, D), lambda i, j: (i, 0)), pl.BlockSpec((tn, D), lambda i, j: (j, 0)),
         pl.BlockSpec((tn, D), lambda i, j: (j, 0))],
        [o_spec, o_spec, o_spec], [o_shape, o_shape, o_shape], [xn, wgT, wuT], carry=carry)


def _ffn_bwd_act(name, dh, wd, gate, up, carry=None):
    L, D = dh.shape
    F = wd.shape[0]
    tm = _tile(L, 544, 16)
    tn = _tile(F, 512, LANES)

    def body(dh_ref, wd_ref, g_ref, u_ref, dg_ref, du_ref, a_ref):
        da = 0.5 * _dot_nt(dh_ref[...].astype(BF16), wd_ref[...])
        g = g_ref[...].astype(F32)
        u = u_ref[...].astype(F32)
        sg = _sigmoid(g)
        silu = g * sg
        dg_ref[...] = (da * u * (sg * (1.0 + g * (1.0 - sg)))).astype(BF16)
        du_ref[...] = (da * silu).astype(BF16)
        a_ref[...] = (silu * u).astype(BF16)

    o_spec = pl.BlockSpec((tm, tn), lambda i, j: (i, j))
    o_shape = jax.ShapeDtypeStruct((L, F), BF16)
    return _call(
        name, body, (L // tm, F // tn),
        [pl.BlockSpec((tm, D), lambda i, j: (i, 0)), pl.BlockSpec((tn, D), lambda i, j: (j, 0)), o_spec, o_spec],
        [o_spec, o_spec, o_spec], [o_shape, o_shape, o_shape], [dh, wd, gate, up], carry=carry)


class _Cfg:
    def __init__(self, S, D, hf, hs, hkv):
        self.S, self.D, self.L = S, D, S + BLOCK
        self.hf, self.hs, self.hkv = hf, hs, hkv
        self.wf, self.ws = hf * HEAD_DIM, hs * HEAD_DIM
        self.group = hs // hkv
        self.cs = 2 * LANES // hf
        self.n_meta = 16
        self.pad = BLOCK - self.n_meta
        self.o_fk = self.wf
        self.o_fv = 2 * self.wf
        self.o_sq = 3 * self.wf
        self.o_sk = self.o_sq + self.ws
        self.o_sv = self.o_sk + LANES
        self.o_fz = self.o_sv + LANES
        self.up = self.o_fz + LANES
        assert hkv == 2 and hf % 2 == 0 and self.group % 2 == 0 and self.cs % 8 == 0
        assert self.o_sq % self.ws == 0 and (2 * self.wf) % LANES == 0


def _head_norm(x, gain, lo, mult):
    r = lax.rsqrt(_half_sum(x * x, lo) * (1.0 / HEAD_DIM) + EPS)
    return x * r * (gain * mult)


def _head_norm_bwd(dy, x, gain, lo, mult):
    r = lax.rsqrt(_half_sum(x * x, lo) * (1.0 / HEAD_DIM) + EPS)
    w = dy * (gain * mult)
    proj = _half_sum(w * x, lo) * (1.0 / HEAD_DIM)
    dx = r * w - x * (r * r * r * proj)
    dgain = jnp.sum(dy * mult * x * r, axis=0, keepdims=True)
    return dx, dgain


def _dup(x, lo):
    xr = pltpu.roll(x, 64, 1)
    return jnp.where(lo, x, xr), jnp.where(lo, xr, x)


def _mix_prep(name, cfg, u, gq, gk, gsq, gsk, bias):
    L, wf, ws = cfg.L, cfg.wf, cfg.ws
    T = BLOCK
    npf, nps = wf // LANES, ws // LANES

    def body(fqk_ref, fv_ref, sq_ref, sk_ref, sv_ref, fz_ref, gq_ref, gk_ref, gsq_ref, gsk_ref, b_ref,
             qn_ref, kn_ref, fvo_ref, sqn_ref, skd_ref, svd_ref, c_ref, ct_ref, carry_ref):
        i = pl.program_id(0)
        lo = _lane((T, LANES)) < HEAD_DIM
        for p in range(npf):
            sl = slice(p * LANES, (p + 1) * LANES)
            qn_ref[:, sl] = _head_norm(fqk_ref[:, sl], gq_ref[...], lo, SCALE).astype(BF16)
            kn_ref[:, sl] = _head_norm(fqk_ref[:, wf + p * LANES: wf + (p + 1) * LANES], gk_ref[...], lo, 1.0).astype(BF16)
        fvo_ref[...] = fv_ref[...].astype(BF16)
        for p in range(nps):
            sl = slice(p * LANES, (p + 1) * LANES)
            sqn_ref[:, sl] = _head_norm(sq_ref[:, sl], gsq_ref[...], lo, SCALE).astype(BF16)
        k0, k1 = _dup(_head_norm(sk_ref[...], gsk_ref[...], lo, 1.0), lo)
        skd_ref[:, :LANES] = k0.astype(BF16)
        skd_ref[:, LANES:] = k1.astype(BF16)
        v0, v1 = _dup(sv_ref[...], lo)
        svd_ref[:, :LANES] = v0.astype(BF16)
        svd_ref[:, LANES:] = v1.astype(BF16)

        @pl.when(i == 0)
        def _():
            carry_ref[...] = jnp.zeros_like(carry_ref)

        z = fz_ref[...] + b_ref[...]
        lf = jnp.minimum(z, 0.0) - jnp.log(1.0 + jnp.exp(-jnp.abs(z)))
        row = lax.broadcasted_iota(jnp.int32, (T, T), 0)
        col = lax.broadcasted_iota(jnp.int32, (T, T), 1)
        tri = jnp.where(col <= row, 1.0, 0.0).astype(F32)
        c = jnp.dot(tri, lf, precision=HI, preferred_element_type=F32) + carry_ref[0:1, :]
        c_ref[...] = c
        ct_ref[...] = c.T
        carry_ref[0:1, :] = c_ref[T - 1:T, :]

    def rows(w, cb):
        return pl.BlockSpec((T, w), lambda i, cb=cb: (i, cb))

    vec = pl.BlockSpec((1, LANES), lambda i: (0, 0))
    return pl.pallas_call(
        body, name=name, grid=(L // T,),
        in_specs=[rows(2 * wf, 0), rows(wf, 2), rows(ws, cfg.o_sq // ws), rows(LANES, cfg.o_sk // LANES),
                  rows(LANES, cfg.o_sv // LANES), rows(LANES, cfg.o_fz // LANES), vec, vec, vec, vec, vec],
        out_specs=[rows(wf, 0), rows(wf, 0), rows(wf, 0), rows(ws, 0), rows(2 * LANES, 0), rows(2 * LANES, 0),
                   rows(LANES, 0), pl.BlockSpec((LANES, T), lambda i: (0, i))],
        out_shape=[jax.ShapeDtypeStruct((L, wf), BF16)] * 3 + [jax.ShapeDtypeStruct((L, ws), BF16)]
        + [jax.ShapeDtypeStruct((L, 2 * LANES), BF16)] * 2
        + [jax.ShapeDtypeStruct((L, LANES), F32), jax.ShapeDtypeStruct((LANES, L), F32)],
        scratch_shapes=[pltpu.VMEM((8, LANES), F32)],
        compiler_params=_cparams(("arbitrary",)),
    )(u, u, u, u, u, u, gq, gk, gsq, gsk, bias)


def _mix_prep_bwd(name, cfg, u, gq, gk, gsq, gsk, bias, dqn, dkn, dfv, dsqn, dskd, dsvd, dct, dcq):
    L, wf, ws = cfg.L, cfg.wf, cfg.ws
    T = BLOCK
    nb = L // T
    npf, nps = wf // LANES, ws // LANES

    def body(fqk_ref, sq_ref, sk_ref, fz_ref, gq_ref, gk_ref, gsq_ref, gsk_ref, b_ref,
             dqn_ref, dkn_ref, dfv_ref, dsqn_ref, dskd_ref, dsvd_ref, dct_ref, dcq_ref,
             du_ref, dgq_ref, dgk_ref, dgsq_ref, dgsk_ref, db_ref, carry_ref):
        i = pl.program_id(0)
        lo = _lane((T, LANES)) < HEAD_DIM

        @pl.when(i == 0)
        def _():
            carry_ref[...] = jnp.zeros_like(carry_ref)
            for r in (dgq_ref, dgk_ref, dgsq_ref, dgsk_ref, db_ref):
                r[...] = jnp.zeros_like(r)

        accq = jnp.zeros((1, LANES), F32)
        acck = jnp.zeros((1, LANES), F32)
        for p in range(npf):
            sl = slice(p * LANES, (p + 1) * LANES)
            dx, dg = _head_norm_bwd(dqn_ref[:, sl], fqk_ref[:, sl], gq_ref[...], lo, SCALE)
            du_ref[:, sl] = dx.astype(BF16)
            accq = accq + dg
            slk = slice(wf + p * LANES, wf + (p + 1) * LANES)
            dx, dg = _head_norm_bwd(dkn_ref[:, sl], fqk_ref[:, slk], gk_ref[...], lo, 1.0)
            du_ref[:, slk] = dx.astype(BF16)
            acck = acck + dg
        dgq_ref[...] += accq
        dgk_ref[...] += acck
        du_ref[:, cfg.o_fv:cfg.o_fv + wf] = dfv_ref[...].astype(BF16)
        accs = jnp.zeros((1, LANES), F32)
        for p in range(nps):
            sl = slice(p * LANES, (p + 1) * LANES)
            dx, dg = _head_norm_bwd(dsqn_ref[:, sl], sq_ref[:, sl], gsq_ref[...], lo, SCALE)
            du_ref[:, cfg.o_sq + p * LANES: cfg.o_sq + (p + 1) * LANES] = dx.astype(BF16)
            accs = accs + dg
        dgsq_ref[...] += accs

        def fold(ref):
            a0, a1 = ref[:, :LANES], ref[:, LANES:]
            return jnp.where(lo, a0 + pltpu.roll(a0, 64, 1), a1 + pltpu.roll(a1, 64, 1))

        dx, dg = _head_norm_bwd(fold(dskd_ref), sk_ref[...], gsk_ref[...], lo, 1.0)
        du_ref[:, cfg.o_sk:cfg.o_sk + LANES] = dx.astype(BF16)
        dgsk_ref[...] += dg
        du_ref[:, cfg.o_sv:cfg.o_sv + LANES] = fold(dsvd_ref).astype(BF16)

        dc = dct_ref[...].T + dcq_ref[...]
        row = lax.broadcasted_iota(jnp.int32, (T, T), 0)
        col = lax.broadcasted_iota(jnp.int32, (T, T), 1)
        triu = jnp.where(col >= row, 1.0, 0.0).astype(F32)
        dlf = jnp.dot(triu, dc, precision=HI, preferred_element_type=F32) + carry_ref[0:1, :]
        carry_ref[0:1, :] = dlf[0:1, :]
        z = fz_ref[...] + b_ref[...]
        dz = dlf * _sigmoid(-z)
        du_ref[:, cfg.o_fz:cfg.o_fz + LANES] = dz.astype(BF16)
        db_ref[...] += jnp.sum(dz, axis=0, keepdims=True)

    def rows(w, cb):
        return pl.BlockSpec((T, w), lambda i, cb=cb: (nb - 1 - i, cb))

    vec = pl.BlockSpec((1, LANES), lambda i: (0, 0))
    vshape = jax.ShapeDtypeStruct((1, LANES), F32)
    return pl.pallas_call(
        body, name=name, grid=(nb,),
        in_specs=[rows(2 * wf, 0), rows(ws, cfg.o_sq // ws), rows(LANES, cfg.o_sk // LANES),
                  rows(LANES, cfg.o_fz // LANES), vec, vec, vec, vec, vec,
                  rows(wf, 0), rows(wf, 0), rows(wf, 0), rows(ws, 0), rows(2 * LANES, 0), rows(2 * LANES, 0),
                  pl.BlockSpec((LANES, T), lambda i: (0, nb - 1 - i)), rows(LANES, 0)],
        out_specs=[rows(cfg.up, 0), vec, vec, vec, vec, vec],
        out_shape=[jax.ShapeDtypeStruct((L, cfg.up), BF16)] + [vshape] * 5,
        scratch_shapes=[pltpu.VMEM((8, LANES), F32)],
        compiler_params=_cparams(("arbitrary",)),
    )(u, u, u, u, gq, gk, gsq, gsk, bias, dqn, dkn, dfv, dsqn, dskd, dsvd, dct, dcq)


def _fox_fwd(name, cfg, qn, kn, fv, c, ct, carry=None):
    L, wf, cs = cfg.L, cfg.wf, cfg.cs
    TQ = BLOCK
    TK = _tile(L, 544, 8)
    npairs = wf // LANES
    pad = cfg.pad

    def body(q_ref, k_ref, v_ref, c_ref, ct_ref, o_ref, lse_ref):
        p = pl.program_id(0)
        i = pl.program_id(1)
        lo_q = _lane((TQ, LANES)) < HEAD_DIM
        lane_k = _lane((TK, LANES))
        lo_k = lane_k < HEAD_DIM
        lo_d = lax.broadcasted_iota(jnp.int32, (LANES, TQ), 0) < HEAD_DIM
        first = _lane((TK, 2 * TQ)) < TQ
        q = q_ref[...]
        qs = jnp.concatenate([jnp.where(lo_q, q, jnp.zeros_like(q)), jnp.where(lo_q, jnp.zeros_like(q), q)], axis=0)
        cq = jnp.concatenate([ct_ref[0:1, :], ct_ref[1:2, :]], axis=1)
        qrow = lax.broadcasted_iota(jnp.int32, (TK, 2 * TQ), 1)
        qpos = i * TQ + jnp.where(first, qrow, qrow - TQ)

        def step(j, carry_):
            m, l, acc = carry_
            off = pl.multiple_of(j * TK, 8)
            k = k_ref[pl.ds(off, TK), :]
            v = v_ref[pl.ds(off, TK), :]
            cblk = c_ref[pl.ds(off, TK), :]
            kpos = j * TK + lax.broadcasted_iota(jnp.int32, (TK, 2 * TQ), 0)
            allowed = (kpos <= qpos) & (kpos >= pad)
            ck0 = jnp.sum(jnp.where(lane_k == p * cs, cblk, 0.0), axis=1, keepdims=True)
            ck1 = jnp.sum(jnp.where(lane_k == p * cs + 1, cblk, 0.0), axis=1, keepdims=True)
            s = _dot_nt(k, qs) + cq - jnp.where(first, ck0, ck1)
            s = jnp.where(allowed, s, NEG_INF)
            m_new = jnp.maximum(m, jnp.max(s, axis=0, keepdims=True))
            alpha = jnp.exp(m - m_new)
            pr = jnp.exp(s - m_new)
            l = alpha * l + jnp.sum(pr, axis=0, keepdims=True)
            prb = pr.astype(BF16)
            prs = jnp.concatenate([prb[:, :TQ], prb[:, TQ:]], axis=0)
            vs = jnp.concatenate([jnp.where(lo_k, v, jnp.zeros_like(v)), jnp.where(lo_k, jnp.zeros_like(v), v)], axis=0)
            acc = acc * jnp.where(lo_d, alpha[:, :TQ], alpha[:, TQ:]) + _dot_tn(vs, prs)
            return m_new, l, acc

        init = (jnp.full((1, 2 * TQ), NEG_INF, F32), jnp.zeros((1, 2 * TQ), F32), jnp.zeros((LANES, TQ), F32))
        m, l, acc = lax.fori_loop(0, ((i + 1) * TQ + TK - 1) // TK, step, init)
        o_ref[...] = (acc / jnp.where(lo_d, l[:, :TQ], l[:, TQ:])).T
        lse = m + jnp.log(l)
        lse_ref[...] = jnp.where(lo_d, lse[:, :TQ], lse[:, TQ:]).T

    blk = pl.BlockSpec((TQ, LANES), lambda p, i: (i, p))
    full = pl.BlockSpec((L, LANES), lambda p, i: (0, p))
    return _call(
        name, body, (npairs, L // TQ),
        [blk, full, full, pl.BlockSpec((L, LANES), lambda p, i: (0, 0)), pl.BlockSpec((cs, TQ), lambda p, i: (p, i))],
        [blk, blk], [jax.ShapeDtypeStruct((L, wf), F32)] * 2, [qn, kn, fv, c, ct], carry=carry)


def _fox_bwd(name, cfg, qn, kn, fv, c, ct, o, lse, do, carry=None):
    L, wf, cs = cfg.L, cfg.wf, cfg.cs
    T = BLOCK
    TQ = _tile(L, 544, 8)
    nb = L // T
    nq = L // TQ
    npairs = wf // LANES
    pad = cfg.pad

    def body(q_ref, k_ref, v_ref, c_ref, ct_ref, o_ref, lse_ref, do_ref, dq_ref, dk_ref, dv_ref, dct_ref, dcq_ref):
        p = pl.program_id(0)
        j = pl.program_id(1)
        lane = _lane((TQ, LANES))
        lo = lane < HEAD_DIM
        sels = (lo, jnp.logical_not(lo))
        lo_k = _lane((T, LANES)) < HEAD_DIM
        sels_k = (lo_k, jnp.logical_not(lo_k))

        @pl.when(j == 0)
        def _():
            dq_ref[...] = jnp.zeros_like(dq_ref)

        @pl.when((j == 0) & (p == 0))
        def _():
            dcq_ref[...] = jnp.zeros_like(dcq_ref)

        k = k_ref[...]
        v = v_ref[...]
        kh = tuple(jnp.where(s_, k, jnp.zeros_like(k)) for s_ in sels_k)
        ck = tuple(ct_ref[hh:hh + 1, :] for hh in range(2))
        kpos = j * T + lax.broadcasted_iota(jnp.int32, (TQ, T), 1)

        def step(i, carry_):
            dk, dv, dc0, dc1 = carry_
            dcs = [dc0, dc1]
            off = pl.multiple_of(i * TQ, 8)
            q = q_ref[pl.ds(off, TQ), :]
            dov = do_ref[pl.ds(off, TQ), :]
            dd = dov * o_ref[pl.ds(off, TQ), :]
            lse_b = lse_ref[pl.ds(off, TQ), :]
            cblk = c_ref[pl.ds(off, TQ), :]
            qpos = i * TQ + lax.broadcasted_iota(jnp.int32, (TQ, T), 0)
            allowed = (kpos <= qpos) & (kpos >= pad)
            dq = jnp.zeros((TQ, LANES), F32)
            dcq = jnp.zeros((TQ, LANES), F32)
            for hh in range(2):
                sel = sels[hh]
                qhh = jnp.where(sel, q, jnp.zeros_like(q))
                doh = jnp.where(sel, dov, 0.0).astype(BF16)
                dsum = jnp.sum(jnp.where(sel, dd, 0.0), axis=1, keepdims=True)
                lse_h = jnp.sum(jnp.where(lane == hh * HEAD_DIM, lse_b, 0.0), axis=1, keepdims=True)
                cq = jnp.sum(jnp.where(lane == p * cs + hh, cblk, 0.0), axis=1, keepdims=True)
                s = _dot_nt(qhh, k) + cq - ck[hh]
                pr = jnp.where(allowed, jnp.exp(jnp.where(allowed, s, NEG_INF) - lse_h), 0.0)
                dp = _dot_nt(doh, v)
                ds = pr * (dp - dsum)
                dsb = ds.astype(BF16)
                dv = dv + _dot_tn(pr.astype(BF16), doh)
                dk = dk + _dot_tn(dsb, qhh)
                dq = dq + _dot(dsb, kh[hh])
                dcs[hh] = dcs[hh] - jnp.sum(ds, axis=0, keepdims=True)
                dcq = dcq + jnp.where(lane == p * cs + hh, jnp.sum(ds, axis=1, keepdims=True), 0.0)
            dq_ref[pl.ds(off, TQ), :] += dq
            dcq_ref[pl.ds(off, TQ), :] += dcq
            return dk, dv, dcs[0], dcs[1]

        init = (jnp.zeros((T, LANES), F32), jnp.zeros((T, LANES), F32),
                jnp.zeros((1, T), F32), jnp.zeros((1, T), F32))
        dk, dv, dc0, dc1 = lax.fori_loop((j * T) // TQ, nq, step, init)
        dk_ref[...] = dk
        dv_ref[...] = dv
        dct_ref[...] = jnp.zeros_like(dct_ref)
        dct_ref[0:1, :] = dc0
        dct_ref[1:2, :] = dc1

    blk = pl.BlockSpec((T, LANES), lambda p, j: (j, p))
    full = pl.BlockSpec((L, LANES), lambda p, j: (0, p))
    return _call(
        name, body, (npairs, nb),
        [full, blk, blk, pl.BlockSpec((L, LANES), lambda p, j: (0, 0)), pl.BlockSpec((cs, T), lambda p, j: (p, j)),
         full, full, full],
        [full, blk, blk, pl.BlockSpec((cs, T), lambda p, j: (p, j)), pl.BlockSpec((L, LANES), lambda p, j: (0, 0))],
        [jax.ShapeDtypeStruct((L, wf), F32)] * 3
        + [jax.ShapeDtypeStruct((LANES, L), F32), jax.ShapeDtypeStruct((L, LANES), F32)],
        [qn, kn, fv, c, ct, o, lse, do], carry=carry)


def _swa_scores(qh, kp, kc, slope, sink, i, pad):
    T = BLOCK
    t = lax.broadcasted_iota(jnp.int32, (T, T), 0)
    s_ = lax.broadcasted_iota(jnp.int32, (T, T), 1)
    dist_c = t - s_
    dist_p = dist_c + T
    ok_c = (dist_c >= 0) & (i * T + s_ >= pad)
    ok_p = (dist_p < T) & ((i - 1) * T + s_ >= pad)
    sp = jnp.where(ok_p, _dot_nt(qh, kp) - slope * dist_p.astype(F32), NEG_INF)
    sc = jnp.where(ok_c, _dot_nt(qh, kc) - slope * dist_c.astype(F32), NEG_INF)
    m = jnp.maximum(jnp.maximum(jnp.max(sp, axis=1, keepdims=True), jnp.max(sc, axis=1, keepdims=True)), sink)
    ep = jnp.exp(sp - m)
    ec = jnp.exp(sc - m)
    es = jnp.exp(sink - m)
    den = jnp.sum(ep, axis=1, keepdims=True) + jnp.sum(ec, axis=1, keepdims=True) + es
    return ep / den, ec / den, es / den


def _swa_fwd(name, cfg, sqn, skd, svd, sinks, slopes, carry=None):
    L, ws, group = cfg.L, cfg.ws, cfg.group
    T = BLOCK
    npairs = ws // LANES
    ppk = group // 2
    pad = cfg.pad

    def body(sink_ref, slope_ref, q_ref, kp_ref, kc_ref, vp_ref, vc_ref, o_ref):
        p = pl.program_id(0)
        i = pl.program_id(1)
        lo = _lane((T, LANES)) < HEAD_DIM
        q = q_ref[...]
        acc = jnp.zeros((T, LANES), F32)
        for hh in range(2):
            sel = lo if hh == 0 else jnp.logical_not(lo)
            qh = jnp.where(sel, q, jnp.zeros_like(q))
            pp, pc, _ = _swa_scores(qh, kp_ref[...], kc_ref[...], slope_ref[2 * p + hh], sink_ref[2 * p + hh], i, pad)
            vp = jnp.where(sel, vp_ref[...], jnp.zeros_like(q))
            vc = jnp.where(sel, vc_ref[...], jnp.zeros_like(q))
            acc = acc + _dot(pp.astype(BF16), vp) + _dot(pc.astype(BF16), vc)
        o_ref[...] = acc

    smem = pl.BlockSpec(memory_space=pltpu.SMEM)
    prev = pl.BlockSpec((T, LANES), lambda p, i: (jnp.maximum(i - 1, 0), p // ppk))
    cur = pl.BlockSpec((T, LANES), lambda p, i: (i, p // ppk))
    blk = pl.BlockSpec((T, LANES), lambda p, i: (i, p))
    return _call(name, body, (npairs, L // T), [smem, smem, blk, prev, cur, prev, cur], [blk],
                 [jax.ShapeDtypeStruct((L, ws), F32)], [sinks, slopes, sqn, skd, skd, svd, svd], carry=carry)


def _swa_bwd(name, cfg, sqn, skd, svd, sinks, slopes, o, do, carry=None):
    L, ws, group, hkv = cfg.L, cfg.ws, cfg.group, cfg.hkv
    T = BLOCK
    gw = group * HEAD_DIM
    ppk = group // 2
    pad = cfg.pad

    def body(sink_ref, slope_ref, q_ref, kp_ref, kc_ref, vp_ref, vc_ref, o_ref, do_ref,
             dq_ref, dk_ref, dv_ref, dsink_ref):
        kv = pl.program_id(0)
        i = pl.program_id(1)
        lane = _lane((T, LANES))
        lo = lane < HEAD_DIM

        @pl.when(i == 0)
        def _():
            dk_ref[...] = jnp.zeros_like(dk_ref)
            dv_ref[...] = jnp.zeros_like(dv_ref)
            dsink_ref[...] = jnp.zeros_like(dsink_ref)

        kp, kc, vp, vc = kp_ref[...], kc_ref[...], vp_ref[...], vc_ref[...]
        dkp = jnp.zeros((T, LANES), F32)
        dkc = jnp.zeros((T, LANES), F32)
        dvp = jnp.zeros((T, LANES), F32)
        dvc = jnp.zeros((T, LANES), F32)
        for pp_ in range(ppk):
            sl = slice(pp_ * LANES, (pp_ + 1) * LANES)
            q = q_ref[:, sl]
            dov = do_ref[:, sl]
            dd = dov * o_ref[:, sl]
            dq = jnp.zeros((T, LANES), F32)
            dsk = jnp.zeros((1, LANES), F32)
            for hh in range(2):
                sel = lo if hh == 0 else jnp.logical_not(lo)
                h = kv * group + 2 * pp_ + hh
                qh = jnp.where(sel, q, jnp.zeros_like(q))
                pp, pc, ps = _swa_scores(qh, kp, kc, slope_ref[h], sink_ref[h], i, pad)
                doh = jnp.where(sel, dov, 0.0).astype(BF16)
                dsum = jnp.sum(jnp.where(sel, dd, 0.0), axis=1, keepdims=True)
                dsp = (pp * (_dot_nt(doh, vp) - dsum)).astype(BF16)
                dsc = (pc * (_dot_nt(doh, vc) - dsum)).astype(BF16)
                khp = jnp.where(sel, kp, jnp.zeros_like(kp))
                khc = jnp.where(sel, kc, jnp.zeros_like(kc))
                dq = dq + _dot(dsp, khp) + _dot(dsc, khc)
                dkp = dkp + _dot_tn(dsp, qh)
                dkc = dkc + _dot_tn(dsc, qh)
                dvp = dvp + _dot_tn(pp.astype(BF16), doh)
                dvc = dvc + _dot_tn(pc.astype(BF16), doh)
                dsk = dsk + jnp.where(lane[0:1, :] == hh * HEAD_DIM, -jnp.sum(ps * dsum), 0.0)
            dq_ref[:, sl] = dq
            dsink_ref[8 * pp_:8 * pp_ + 1, :] += dsk

        cur = pl.multiple_of(i * T, T)
        dk_ref[pl.ds(cur, T), :] += dkc
        dv_ref[pl.ds(cur, T), :] += dvc

        @pl.when(i > 0)
        def _():
            prv = pl.multiple_of((i - 1) * T, T)
            dk_ref[pl.ds(prv, T), :] += dkp
            dv_ref[pl.ds(prv, T), :] += dvp

    smem = pl.BlockSpec(memory_space=pltpu.SMEM)
    prev = pl.BlockSpec((T, LANES), lambda kv, i: (jnp.maximum(i - 1, 0), kv))
    cur = pl.BlockSpec((T, LANES), lambda kv, i: (i, kv))
    qblk = pl.BlockSpec((T, gw), lambda kv, i: (i, kv))
    full = pl.BlockSpec((L, LANES), lambda kv, i: (0, kv))
    return _call(
        name, body, (hkv, L // T), [smem, smem, qblk, prev, cur, prev, cur, qblk, qblk],
        [qblk, full, full, pl.BlockSpec((8 * ppk, LANES), lambda kv, i: (kv, 0))],
        [jax.ShapeDtypeStruct((L, ws), F32), jax.ShapeDtypeStruct((L, 2 * LANES), F32),
         jax.ShapeDtypeStruct((L, 2 * LANES), F32), jax.ShapeDtypeStruct((8 * ppk * hkv, LANES), F32)],
        [sinks, slopes, sqn, skd, skd, svd, svd, o, do], carry=carry)


def _out_norm(name, cfg, o_fox, o_swa, g_fox, g_swa):
    L, wf, ws = cfg.L, cfg.wf, cfg.ws
    tm = _tile(L, 256, 16)

    def body(of_ref, os_ref, gf_ref, gs_ref, o_ref):
        for src, g_ref, lo_, w in ((of_ref, gf_ref, 0, wf), (os_ref, gs_ref, wf, ws)):
            x = src[...]
            r = lax.rsqrt(jnp.mean(x * x, axis=1, keepdims=True) + EPS)
            o_ref[:, lo_:lo_ + w] = (x * r * g_ref[...]).astype(BF16)

    return pl.pallas_call(
        body, name=name, grid=(L // tm,),
        in_specs=[pl.BlockSpec((tm, wf), lambda i: (i, 0)), pl.BlockSpec((tm, ws), lambda i: (i, 0)),
                  pl.BlockSpec((1, wf), lambda i: (0, 0)), pl.BlockSpec((1, ws), lambda i: (0, 0))],
        out_specs=pl.BlockSpec((tm, wf + ws), lambda i: (i, 0)),
        out_shape=jax.ShapeDtypeStruct((L, wf + ws), BF16),
        compiler_params=_cparams(("parallel",)),
    )(o_fox, o_swa, g_fox, g_swa)


def _out_norm_bwd(name, cfg, dcat, o_fox, o_swa, g_fox, g_swa):
    L, wf, ws = cfg.L, cfg.wf, cfg.ws
    tm = _tile(L, 256, 16)

    def body(d_ref, of_ref, os_ref, gf_ref, gs_ref, dof_ref, dos_ref, dgf_ref, dgs_ref):
        i = pl.program_id(0)

        @pl.when(i == 0)
        def _():
            dgf_ref[...] = jnp.zeros_like(dgf_ref)
            dgs_ref[...] = jnp.zeros_like(dgs_ref)

        for src, g_ref, dst, dg_ref, lo_, w in ((of_ref, gf_ref, dof_ref, dgf_ref, 0, wf),
                                                (os_ref, gs_ref, dos_ref, dgs_ref, wf, ws)):
            x = src[...]
            dy = d_ref[:, lo_:lo_ + w]
            r = lax.rsqrt(jnp.mean(x * x, axis=1, keepdims=True) + EPS)
            wv = dy * g_ref[...]
            proj = jnp.sum(wv * x, axis=1, keepdims=True) * (1.0 / w)
            dst[...] = r * wv - x * (r * r * r * proj)
            dg_ref[...] += jnp.sum(dy * x * r, axis=0, keepdims=True)

    return pl.pallas_call(
        body, name=name, grid=(L // tm,),
        in_specs=[pl.BlockSpec((tm, wf + ws), lambda i: (i, 0)), pl.BlockSpec((tm, wf), lambda i: (i, 0)),
                  pl.BlockSpec((tm, ws), lambda i: (i, 0)),
                  pl.BlockSpec((1, wf), lambda i: (0, 0)), pl.BlockSpec((1, ws), lambda i: (0, 0))],
        out_specs=[pl.BlockSpec((tm, wf), lambda i: (i, 0)), pl.BlockSpec((tm, ws), lambda i: (i, 0)),
                   pl.BlockSpec((1, wf), lambda i: (0, 0)), pl.BlockSpec((1, ws), lambda i: (0, 0))],
        out_shape=[jax.ShapeDtypeStruct((L, wf), F32), jax.ShapeDtypeStruct((L, ws), F32),
                   jax.ShapeDtypeStruct((1, wf), F32), jax.ShapeDtypeStruct((1, ws), F32)],
        compiler_params=_cparams(("arbitrary",)),
    )(dcat, o_fox, o_swa, g_fox, g_swa)


def _loss_head(name, h, target):
    L, D = h.shape
    T = BLOCK

    def body(h_ref, t_ref, loss_ref, dh_ref, dh16_ref):
        i = pl.program_id(0)

        @pl.when(i == 0)
        def _():
            loss_ref[...] = jnp.zeros_like(loss_ref)
            dh_ref[...] = jnp.zeros_like(dh_ref)
            dh16_ref[...] = jnp.zeros_like(dh16_ref)

        @pl.when(i > 0)
        def _():
            err = h_ref[...] - t_ref[...]
            dh = err * (1.0 / D)
            dh_ref[...] = dh
            dh16_ref[...] = dh.astype(BF16)
            loss_ref[...] += jnp.sum(err * err) * (0.5 / D)

    return pl.pallas_call(
        body, name=name, grid=(L // T,),
        in_specs=[pl.BlockSpec((T, D), lambda i: (i, 0)), pl.BlockSpec((T, D), lambda i: (jnp.maximum(i - 1, 0), 0))],
        out_specs=[pl.BlockSpec((8, LANES), lambda i: (0, 0)), pl.BlockSpec((T, D), lambda i: (i, 0)),
                   pl.BlockSpec((T, D), lambda i: (i, 0))],
        out_shape=[jax.ShapeDtypeStruct((8, LANES), F32), jax.ShapeDtypeStruct((L, D), F32),
                   jax.ShapeDtypeStruct((L, D), BF16)],
        compiler_params=_cparams(("arbitrary",)),
    )(h, target)


def _pair_add(name, grad, landed, parts, layer, core):
    _, _, r, C = grad.shape

    def body(s_ref, g_ref, l_ref, p_ref, o_ref):
        o_ref[...] = (g_ref[...].astype(F32) + l_ref[...].astype(F32)).astype(o_ref.dtype)

    return pl.pallas_call(
        body, name=name,
        grid_spec=pltpu.PrefetchScalarGridSpec(
            num_scalar_prefetch=1, grid=(4,),
            in_specs=[pl.BlockSpec((None, None, r, C), lambda k, s: (k, s[0], 0, 0)),
                      pl.BlockSpec((None, r, C), lambda k, s: (k, 0, 0)),
                      pl.BlockSpec(memory_space=pl.ANY)],
            out_specs=pl.BlockSpec((None, None, r, C), lambda k, s: (layer, k, 0, 0))),
        out_shape=jax.ShapeDtypeStruct(parts.shape, parts.dtype),
        input_output_aliases={3: 0},
        compiler_params=_cparams(("arbitrary",)),
    )(core, grad, landed, parts)


def _chip_sum(name, part, landed, chip, transpose):
    A, _, r, C = part.shape

    def body(s_ref, p_ref, l0_ref, l1_ref, l2_ref, o_ref):
        total = ((p_ref[...].astype(F32) + l0_ref[...].astype(F32))
                 + (l1_ref[...].astype(F32) + l2_ref[...].astype(F32)))
        o_ref[...] = total.T if transpose else total

    def land(k):
        return pl.BlockSpec((None, None, r, C), lambda a, s, k=k: (k, a, 0, 0))

    out = (C, r) if transpose else (r, C)
    return pl.pallas_call(
        body, name=name,
        grid_spec=pltpu.PrefetchScalarGridSpec(
            num_scalar_prefetch=1, grid=(A,),
            in_specs=[pl.BlockSpec((None, None, r, C), lambda a, s: (a, s[0], 0, 0)), land(0), land(1), land(2)],
            out_specs=pl.BlockSpec((None,) + out, lambda a, s: (a, 0, 0))),
        out_shape=jax.ShapeDtypeStruct((A,) + out, F32),
        compiler_params=_cparams(("parallel",)),
    )(chip, part, landed, landed, landed)


def _gather_sum(name, v):
    R = v.shape[0]

    def body(x_ref, o_ref, buf_ref, send_sems, recv_sems):
        x, y, c = _place()
        me, sibling = (x, y, c), (x, y, 1 - c)
        chips = [(1 - x, y), (x, 1 - y), (1 - x, 1 - y)]

        def rows(dev):
            px, py, pc = dev
            return buf_ref.at[4 * px + 2 * py + pc]

        def copy(k, block, to, src=None):
            return pltpu.make_async_remote_copy(
                src_ref=rows(block) if src is None else src, dst_ref=rows(block),
                send_sem=send_sems.at[k], recv_sem=recv_sems.at[k], device_id=to, device_id_type=MESH)

        first = [copy(0, me, sibling, src=x_ref)]
        first += [copy(1 + j, me, (*chip, c), src=x_ref) for j, chip in enumerate(chips)]
        for cp in first:
            cp.start()
        rows(me)[...] = x_ref[...]
        passed = [copy(4 + j, (*chip, c), sibling) for j, chip in enumerate(chips)]
        for j, chip in enumerate(chips):
            copy(1 + j, (*chip, c), me).wait_recv()
            passed[j].start()
        copy(0, sibling, me).wait_recv()
        for j, chip in enumerate(chips):
            copy(4 + j, (*chip, 1 - c), me).wait_recv()
        for cp in first + passed:
            cp.wait_send()
        acc = buf_ref[0]
        for d in range(1, N_DEV):
            acc = acc + buf_ref[d]
        o_ref[...] = acc

    vm = pl.BlockSpec(memory_space=pltpu.VMEM)
    return pl.pallas_call(
        body, name=name, in_specs=[vm], out_specs=vm,
        out_shape=jax.ShapeDtypeStruct((R, LANES), F32),
        scratch_shapes=[pltpu.VMEM((N_DEV, R, LANES), F32), pltpu.SemaphoreType.DMA((7,)), pltpu.SemaphoreType.DMA((7,))],
    )(v)


def _adamw(name, g, w, m, v):
    R, C = g.shape
    tr = _tile(R, max(8, (1 << 18) // max(C, 1) // 8 * 8), 8)

    def body(g_ref, w_ref, m_ref, v_ref, d_ref, nm_ref, nv_ref):
        gv = g_ref[...]
        nm = ADAM_B1 * m_ref[...] + (1.0 - ADAM_B1) * gv
        nv = ADAM_B2 * v_ref[...] + (1.0 - ADAM_B2) * (gv * gv)
        m_hat = nm / (1.0 - ADAM_B1 ** ADAM_STEP)
        v_hat = nv / (1.0 - ADAM_B2 ** ADAM_STEP)
        d_ref[...] = -ADAM_LR * (m_hat / (jnp.sqrt(v_hat) + ADAM_EPS) + ADAM_WD * w_ref[...])
        nm_ref[...] = nm
        nv_ref[...] = nv

    blk = pl.BlockSpec((tr, C), lambda i: (i, 0))
    shp = jax.ShapeDtypeStruct((R, C), F32)
    return pl.pallas_call(
        body, name=name, grid=(R // tr,),
        in_specs=[blk] * 4, out_specs=[blk] * 3, out_shape=[shp] * 3,
        compiler_params=_cparams(("parallel",)),
    )(g, w, m, v)


def _adamw_nd(name, g, w, m, v):
    shape = w.shape
    flat = [a.reshape(-1, shape[-1]) for a in (g, w, m, v)]
    return tuple(o.reshape(shape) for o in _adamw(name, *flat))


def _scatter_heads(cfg, vals):
    v = jnp.pad(vals.reshape(cfg.hf // 2, 2), ((0, 0), (0, cfg.cs - 2)))
    return v.reshape(1, LANES)


def _gather_heads(cfg, row):
    return row.reshape(cfg.hf // 2, cfg.cs)[:, :2].reshape(cfg.hf)


def _permute_w_in(cfg, w_in_t):
    wf, hf = cfg.wf, cfg.hf
    o = 3 * wf
    cols = w_in_t.shape[1]
    fz = w_in_t[o:o + hf].reshape(hf // 2, 2, cols)
    fz_blk = jnp.pad(fz, ((0, 0), (0, cfg.cs - 2), (0, 0))).reshape(LANES, cols)
    return jnp.concatenate([w_in_t[:o], w_in_t[o + hf:], fz_blk], axis=0)


def _unpermute_dw_in(cfg, dwp):
    wf, hf = cfg.wf, cfg.hf
    o = 3 * wf
    cols = dwp.shape[1]
    fz = dwp[cfg.o_fz:].reshape(hf // 2, cfg.cs, cols)[:, :2].reshape(hf, cols)
    return jnp.concatenate([dwp[:o], fz, dwp[o:cfg.o_fz]], axis=0)


def _pair_gain(g):
    return jnp.tile(g, 2)[None]


def _fold_pair(dg):
    return dg[0, :HEAD_DIM] + dg[0, HEAD_DIM:]


def kernel(x, meta_tokens, ffn1_norm, ffn1_w_gate, ffn1_w_up, ffn1_w_down, mix_norm, w_in, b_forget, fox_q_norm, fox_k_norm, swa_q_norm, swa_k_norm, swa_sinks, fox_out_norm, swa_out_norm, w_out, ffn2_norm, ffn2_w_gate, ffn2_w_up, ffn2_w_down, loss_target, m_meta_tokens, m_ffn1_norm, m_ffn1_w_gate, m_ffn1_w_up, m_ffn1_w_down, m_mix_norm, m_w_in, m_b_forget, m_fox_q_norm, m_fox_k_norm, m_swa_q_norm, m_swa_k_norm, m_swa_sinks, m_fox_out_norm, m_swa_out_norm, m_w_out, m_ffn2_norm, m_ffn2_w_gate, m_ffn2_w_up, m_ffn2_w_down, v_meta_tokens, v_ffn1_norm, v_ffn1_w_gate, v_ffn1_w_up, v_ffn1_w_down, v_mix_norm, v_w_in, v_b_forget, v_fox_q_norm, v_fox_k_norm, v_swa_q_norm, v_swa_k_norm, v_swa_sinks, v_fox_out_norm, v_swa_out_norm, v_w_out, v_ffn2_norm, v_ffn2_w_gate, v_ffn2_w_up, v_ffn2_w_down):
    weights = dict(meta_tokens=meta_tokens, ffn1_norm=ffn1_norm, ffn1_w_gate=ffn1_w_gate, ffn1_w_up=ffn1_w_up,
                   ffn1_w_down=ffn1_w_down, mix_norm=mix_norm, w_in=w_in, b_forget=b_forget, fox_q_norm=fox_q_norm,
                   fox_k_norm=fox_k_norm, swa_q_norm=swa_q_norm, swa_k_norm=swa_k_norm, swa_sinks=swa_sinks,
                   fox_out_norm=fox_out_norm, swa_out_norm=swa_out_norm, w_out=w_out, ffn2_norm=ffn2_norm,
                   ffn2_w_gate=ffn2_w_gate, ffn2_w_up=ffn2_w_up, ffn2_w_down=ffn2_w_down)
    mom_m = dict(meta_tokens=m_meta_tokens, ffn1_norm=m_ffn1_norm, ffn1_w_gate=m_ffn1_w_gate, ffn1_w_up=m_ffn1_w_up,
                 ffn1_w_down=m_ffn1_w_down, mix_norm=m_mix_norm, w_in=m_w_in, b_forget=m_b_forget,
                 fox_q_norm=m_fox_q_norm, fox_k_norm=m_fox_k_norm, swa_q_norm=m_swa_q_norm, swa_k_norm=m_swa_k_norm,
                 swa_sinks=m_swa_sinks, fox_out_norm=m_fox_out_norm, swa_out_norm=m_swa_out_norm, w_out=m_w_out,
                 ffn2_norm=m_ffn2_norm, ffn2_w_gate=m_ffn2_w_gate, ffn2_w_up=m_ffn2_w_up, ffn2_w_down=m_ffn2_w_down)
    mom_v = dict(meta_tokens=v_meta_tokens, ffn1_norm=v_ffn1_norm, ffn1_w_gate=v_ffn1_w_gate, ffn1_w_up=v_ffn1_w_up,
                 ffn1_w_down=v_ffn1_w_down, mix_norm=v_mix_norm, w_in=v_w_in, b_forget=v_b_forget,
                 fox_q_norm=v_fox_q_norm, fox_k_norm=v_fox_k_norm, swa_q_norm=v_swa_q_norm, swa_k_norm=v_swa_k_norm,
                 swa_sinks=v_swa_sinks, fox_out_norm=v_fox_out_norm, swa_out_norm=v_swa_out_norm, w_out=v_w_out,
                 ffn2_norm=v_ffn2_norm, ffn2_w_gate=v_ffn2_w_gate, ffn2_w_up=v_ffn2_w_up, ffn2_w_down=v_ffn2_w_down)
    names = list(weights)

    _, S, D = x.shape
    depth = ffn1_norm.shape[0]
    hf, hs = b_forget.shape[1], swa_sinks.shape[1]
    U = w_in.shape[2] * N_DEV
    hkv = (U - 3 * HEAD_DIM * hf - hf - HEAD_DIM * hs) // (2 * HEAD_DIM)
    cfg = _Cfg(S, D, hf, hs, hkv)
    n_meta = meta_tokens.shape[0]
    assert n_meta == cfg.n_meta
    x_idx, y_idx, c_idx = _place()
    chip_idx = 2 * x_idx + y_idx
    dev_idx = 2 * chip_idx + c_idx
    core_s = jnp.reshape(c_idx, (1,)).astype(jnp.int32)
    chip_s = jnp.reshape(chip_idx, (1,)).astype(jnp.int32)

    col_sharded = ("ffn1_w_gate", "ffn1_w_up", "w_in", "ffn2_w_gate", "ffn2_w_up")
    use_order = ("ffn1_w_gate", "ffn1_w_up", "ffn1_w_down", "w_in", "w_out", "ffn2_w_gate", "ffn2_w_up", "ffn2_w_down")

    def shard(key):
        k, l = key
        if k == "meta_tokens":
            return meta_tokens
        w = weights[k][l]
        return (w.T if k in col_sharded else w).astype(BF16)

    waiting = [("meta_tokens", 0)] + [(k, l) for l in range(depth) for k in use_order]
    halfway = []
    gathered = {}

    def fwd_carry(n_first):
        cy = _Carry()
        second = [(key, _gather_second(cy, buf)) for key, buf in halfway]
        first = [(key, _gather_first(cy, shard(key))) for key in waiting[:n_first]]
        del waiting[:n_first]
        halfway.clear()
        return cy, (first, second)

    def fwd_absorb(extra, plan):
        first, second = plan
        for key, idx in second:
            gathered[key] = extra[idx]
        for key, idx in first:
            halfway.append((key, extra[idx]))

    def weight(k, l):
        key = (k, l)
        while key not in gathered:
            n = 0 if any(key == hk for hk, _ in halfway) else waiting.index(key) + 1
            cy, plan = fwd_carry(n)
            fwd_absorb(_comm_only("weights_gather", cy), plan)
        g = gathered[key]
        return g.reshape(-1, g.shape[-1])

    weight("ffn1_w_down", 0)
    meta_full = jnp.swapaxes(weight("meta_tokens", 0).reshape(N_DEV, n_meta, -1), 0, 1).reshape(n_meta, D)
    slopes = jnp.asarray(2.0 ** (-8.0 * np.arange(1, hs + 1) / hs), dtype=F32)

    h = jnp.concatenate([jnp.zeros((cfg.pad, D), F32), meta_full, x[0]], axis=0)
    saved = []
    w_in_p = [None] * depth

    def mm_f(name, a, b, n_first=1, **kw):
        cy, plan = fwd_carry(n_first)
        out, extra = _matmul(name, a, b, carry=cy, **kw)
        fwd_absorb(extra, plan)
        return out

    def ffn_fwd(tag, l, h_in, norm, wg, wu, wd):
        xn = _rmsnorm_fwd(f"{tag}_norm", h_in, norm[l][None])
        wg_t, wu_t = weight(wg, l), weight(wu, l)
        cy, plan = fwd_carry(1)
        (gate, up, act), extra = _ffn_up(f"{tag}_up", xn, wg_t, wu_t, carry=cy)
        fwd_absorb(extra, plan)
        h_out = mm_f(f"{tag}_down", act, weight(wd, l), scale=0.5, residual=h_in, tm=544, tn=1024, tk=2816)
        return h_out, (xn, gate, up)

    for l in range(depth):
        st = {"h0": h}
        h, st["ffn1"] = ffn_fwd("ffn1", l, h, ffn1_norm, "ffn1_w_gate", "ffn1_w_up", "ffn1_w_down")
        st["h1"] = h
        xn = _rmsnorm_fwd("mix_norm", h, mix_norm[l][None])
        w_in_p[l] = _permute_w_in(cfg, weight("w_in", l))
        u = mm_f("mix_in", xn, w_in_p[l], trans_b=True, tm=544, tn=640, tk=2048)
        gq, gk = _pair_gain(fox_q_norm[l]), _pair_gain(fox_k_norm[l])
        gsq, gsk = _pair_gain(swa_q_norm[l]), _pair_gain(swa_k_norm[l])
        bias = _scatter_heads(cfg, b_forget[l])
        qn, kn, fv, sqn, skd, svd, c, ct = _mix_prep("mix_prep", cfg, u, gq, gk, gsq, gsk, bias)
        cy, plan = fwd_carry(2)
        (o_fox, lse), extra = _fox_fwd("fox_fwd", cfg, qn, kn, fv, c, ct, carry=cy)
        fwd_absorb(extra, plan)
        cy, plan = fwd_carry(1)
        (o_swa,), extra = _swa_fwd("swa_fwd", cfg, sqn, skd, svd, swa_sinks[l], slopes, carry=cy)
        fwd_absorb(extra, plan)
        o_cat = _out_norm("out_norm", cfg, o_fox, o_swa, fox_out_norm[l][None], swa_out_norm[l][None])
        st["mix"] = (xn, u, gq, gk, gsq, gsk, bias, qn, kn, fv, sqn, skd, svd, c, ct, o_fox, lse, o_swa, o_cat)
        h = mm_f("mix_out", o_cat, weight("w_out", l), n_first=0, residual=h, tm=544, tn=1024, tk=2048)
        st["h2"] = h
        h, st["ffn2"] = ffn_fwd("ffn2", l, h, ffn2_norm, "ffn2_w_gate", "ffn2_w_up", "ffn2_w_down")
        saved.append(st)

    loss_blk, dh, dh16 = _loss_head("loss_head", h, loss_target[0])

    small = {k: [None] * depth for k in names if k not in use_order and k != "meta_tokens"}
    parts, landing = {}, {}
    to_sibling, to_chips = [], []

    def bwd_carry(n_chips):
        cy = _Carry()
        t1, t3 = [], []
        for k, l in list(to_chips):
            if len(t3) < n_chips and all(k != k3 for k3, _ in t3):
                to_chips.remove((k, l))
                t3.append((k, _scatter_chips(cy, parts[k], landing[k], l)))
        while to_sibling:
            k, l, g = to_sibling.pop(0)
            t1.append((k, l, g, _scatter_sibling(cy, g)))
        return cy, (t1, t3)

    def bwd_absorb(extra, plan):
        t1, t3 = plan
        for k, idx in t3:
            landing[k] = extra[idx]
        for k, l, g, idx in t1:
            parts[k] = _pair_add("grads_pair_add", g, extra[idx], parts[k], l, core_s)
            to_chips.append((k, l))

    def emit_grad(k, l, dw):
        r, C = dw.shape[0] // N_DEV, dw.shape[1]
        if k not in parts:
            parts[k] = lax.empty((depth, 4, r, C), BF16)
            landing[k] = lax.empty((3, depth, r, C), BF16)
        to_sibling.append((k, l, dw.reshape(4, 2, r, C)))

    def mm_b(name, a, b, n_chips=0, **kw):
        cy, plan = bwd_carry(n_chips)
        out, extra = _matmul(name, a, b, carry=cy, **kw)
        bwd_absorb(extra, plan)
        return out

    def ffn_bwd(tag, l, dh_out, dh_out16, h_in, st_, norm, wg, wu, wd):
        xn, gate, up = st_
        short = 1 if l == 0 else 0
        cy, plan = bwd_carry(1)
        (dgate, dup, act), extra = _ffn_bwd_act(f"{tag}_dact", dh_out16, weight(wd, l), gate, up, carry=cy)
        bwd_absorb(extra, plan)
        emit_grad(wd, l, mm_b(f"{tag}_dwd", act, dh_out16, short, trans_a=True, scale=0.5, out_dtype=BF16,
                              tm=512, tn=1024, tk=2176))
        emit_grad(wg, l, mm_b(f"{tag}_dwg", dgate, xn, short, trans_a=True, out_dtype=BF16, tm=512, tn=1024, tk=2176))
        emit_grad(wu, l, mm_b(f"{tag}_dwu", dup, xn, short, trans_a=True, out_dtype=BF16, tm=512, tn=1024, tk=2176))
        dxn = mm_b(f"{tag}_dxn", dgate, weight(wg, l), 1, pair2=(dup, weight(wu, l)), tm=544, tn=1024, tk=1408)
        dh_in, dh_in16, dg = _rmsnorm_bwd(f"{tag}_dnorm", dxn, h_in, norm[l][None], dh_out)
        return dh_in, dh_in16, dg[0]

    for l in reversed(range(depth)):
        st = saved[l]
        dh, dh16, small["ffn2_norm"][l] = ffn_bwd("ffn2", l, dh, dh16, st["h2"], st["ffn2"], ffn2_norm,
                                                   "ffn2_w_gate", "ffn2_w_up", "ffn2_w_down")
        xn, u, gq, gk, gsq, gsk, bias, qn, kn, fv, sqn, skd, svd, c, ct, o_fox, lse, o_swa, o_cat = st["mix"]
        dcat = mm_b("mix_dcat", dh16, weight("w_out", l), int(l == 0), trans_b=True, tm=544, tn=1024, tk=2048)
        emit_grad("w_out", l, mm_b("mix_dwout", o_cat, dh16, int(l == 0), trans_a=True, out_dtype=BF16, tm=512, tn=1024, tk=2176))
        do_fox, do_swa, dgf, dgs = _out_norm_bwd("out_norm_bwd", cfg, dcat, o_fox, o_swa,
                                                 fox_out_norm[l][None], swa_out_norm[l][None])
        small["fox_out_norm"][l], small["swa_out_norm"][l] = dgf[0], dgs[0]
        cy, plan = bwd_carry(3)
        (dqn, dkn, dfv, dct, dcq), extra = _fox_bwd("fox_bwd", cfg, qn, kn, fv, c, ct, o_fox, lse, do_fox, carry=cy)
        bwd_absorb(extra, plan)
        cy, plan = bwd_carry(1)
        (dsqn, dskd, dsvd, dsink), extra = _swa_bwd("swa_bwd", cfg, sqn, skd, svd, swa_sinks[l], slopes, o_swa, do_swa,
                                                    carry=cy)
        bwd_absorb(extra, plan)
        small["swa_sinks"][l] = dsink.reshape(hs // 2, 8, LANES)[:, 0, ::HEAD_DIM].reshape(hs)
        du, dgq, dgk, dgsq, dgsk, db = _mix_prep_bwd("mix_prep_bwd", cfg, u, gq, gk, gsq, gsk, bias,
                                                     dqn, dkn, dfv, dsqn, dskd, dsvd, dct, dcq)
        small["fox_q_norm"][l], small["fox_k_norm"][l] = _fold_pair(dgq), _fold_pair(dgk)
        small["swa_q_norm"][l], small["swa_k_norm"][l] = _fold_pair(dgsq), _fold_pair(dgsk)
        small["b_forget"][l] = _gather_heads(cfg, db[0])
        dwp = mm_b("mix_dwin", du, xn, int(l == 0), trans_a=True, out_dtype=BF16, tm=640, tn=1024, tk=2176)
        emit_grad("w_in", l, _unpermute_dw_in(cfg, dwp))
        dxn = mm_b("mix_dxn", du, w_in_p[l], int(l == 0), tm=544, tn=1024, tk=4480)
        dh, dh16, dg = _rmsnorm_bwd("mix_dnorm", dxn, st["h1"], mix_norm[l][None], dh)
        small["mix_norm"][l] = dg[0]
        dh, dh16, small["ffn1_norm"][l] = ffn_bwd("ffn1", l, dh, dh16, st["h0"], st["ffn1"], ffn1_norm,
                                                   "ffn1_w_gate", "ffn1_w_up", "ffn1_w_down")

    grad_x = dh[BLOCK:][None]
    dmeta = dh[cfg.pad:BLOCK]

    while to_sibling or to_chips:
        cy, plan = bwd_carry(len(use_order))
        bwd_absorb(_comm_only("grads_scatter", cy), plan)

    grads = {}
    for k in use_order:
        grads[k] = _chip_sum("grads_chip_sum", parts[k], landing[k], chip_s, k in col_sharded)

    small_names = list(small)
    pieces = [loss_blk[0, :1], dmeta.reshape(-1)] + [jnp.stack(small[k]).reshape(-1) for k in small_names]
    sizes = [int(p.shape[0]) for p in pieces]
    total = sum(sizes)
    padded = -(-total // (8 * LANES)) * (8 * LANES)
    vec = jnp.concatenate(pieces + [jnp.zeros((padded - total,), F32)]).reshape(-1, LANES)
    summed = _gather_sum("small_gather_sum", vec).reshape(-1)
    offs = np.cumsum([0] + sizes)
    loss = summed[0]
    dmeta_full = summed[offs[1]:offs[2]].reshape(n_meta, D)
    mcols = meta_tokens.shape[1]
    grads["meta_tokens"] = lax.dynamic_slice_in_dim(dmeta_full, dev_idx * mcols, mcols, axis=1)
    for n_, k in enumerate(small_names):
        grads[k] = summed[offs[2 + n_]:offs[3 + n_]].reshape(weights[k].shape)

    delta, new_m, new_v = {}, {}, {}
    for k in names:
        delta[k], new_m[k], new_v[k] = _adamw_nd("adamw", grads[k], weights[k], mom_m[k], mom_v[k])

    return (loss, grad_x, *[grads[k] for k in names], *[delta[k] for k in names],
            *[new_m[k] for k in names], *[new_v[k] for k in names])
```

```python
import functools

import numpy as np
import jax
import jax.numpy as jnp
from jax import lax
from jax.experimental import pallas as pl
from jax.experimental.pallas import tpu as pltpu

F32 = jnp.float32
BF16 = jnp.bfloat16
MESH = pl.DeviceIdType.MESH

HEAD_DIM = 64
BLOCK = 128
LANES = 128
N_DEV = 8
EPS = 1e-6
NEG_INF = -1e30
SCALE = HEAD_DIM ** -0.5

ADAM_LR = 0.001
ADAM_B1 = 0.9
ADAM_B2 = 0.999
ADAM_EPS = 1e-08
ADAM_WD = 0.01
ADAM_STEP = 10

VMEM_BYTES_V7X = 64 * 1024 * 1024
VMEM_LIMIT = VMEM_BYTES_V7X * 3 // 4

NT = (((1,), (1,)), ((), ()))
TN = (((0,), (0,)), ((), ()))
HI = lax.Precision.HIGHEST


def _cparams(sem=None, vmem=VMEM_LIMIT):
    return pltpu.CompilerParams(dimension_semantics=sem, vmem_limit_bytes=vmem)


def _tile(n, pref, mult):
    best = None
    for t in range(mult, min(n, pref) + 1, mult):
        if n % t == 0:
            best = t
    return best if best is not None else n


def _dot(a, b):
    return jnp.dot(a, b, preferred_element_type=F32)


def _dot_nt(a, b):
    return lax.dot_general(a, b, NT, preferred_element_type=F32)


def _dot_tn(a, b):
    return lax.dot_general(a, b, TN, preferred_element_type=F32)


def _lane(shape):
    return lax.broadcasted_iota(jnp.int32, shape, len(shape) - 1)


def _half_sum(x, lo):
    s0 = jnp.sum(jnp.where(lo, x, 0.0), axis=1, keepdims=True)
    s1 = jnp.sum(jnp.where(lo, 0.0, x), axis=1, keepdims=True)
    return jnp.where(lo, s0, s1)


def _sigmoid(x):
    return 1.0 / (1.0 + jnp.exp(-x))


def _place():
    return lax.axis_index("x"), lax.axis_index("y"), lax.axis_index("c")


def _peers(x, y, c):
    return [(x, y, 1 - c), (1 - x, y, c), (x, 1 - y, c), (1 - x, 1 - y, c)]


def _dev_index(dev):
    px, py, pc = dev
    return 4 * px + 2 * py + pc


class _Carry:
    def __init__(self):
        self.ins, self.outs, self.alias, self.items = [], [], {}, []
        self.nsem = self.nloc = 0

    def add(self, ins, outs, alias, nsem, nloc, build):
        i0, o0 = len(self.ins), len(self.outs)
        for src, dst in alias.items():
            self.alias[i0 + src] = o0 + dst
        self.items.append((i0, len(ins), o0, len(outs), self.nsem, self.nloc, build))
        self.ins += ins
        self.outs += outs
        self.nsem += nsem
        self.nloc += nloc
        return list(range(o0, o0 + len(outs)))

    def build(self, in_refs, out_refs, ssem, rsem, lsem):
        ops = []
        for i0, ni, o0, no, s0, l0, fn in self.items:
            ops.append(fn(in_refs[i0:i0 + ni], out_refs[o0:o0 + no],
                          lambda k, s0=s0: (ssem.at[s0 + k], rsem.at[s0 + k]), lambda k, l0=l0: lsem.at[l0 + k]))
        return ops


def _remote(src, dst, sems, dev):
    return pltpu.make_async_remote_copy(src_ref=src, dst_ref=dst, send_sem=sems[0], recv_sem=sems[1],
                                        device_id=dev, device_id_type=MESH)


def _gather_first(carry, shard):
    def build(ins, outs, sems, locs):
        src, buf = ins[0], outs[0]
        x, y, c = _place()
        me = _dev_index((x, y, c))
        peers = _peers(x, y, c)
        local = pltpu.make_async_copy(src, buf.at[me], locs(0))
        sends = [_remote(src, buf.at[me], sems(k), dev) for k, dev in enumerate(peers)]
        recvs = [_remote(src, buf.at[_dev_index(dev)], sems(k), dev) for k, dev in enumerate(peers)]

        def start():
            local.start()
            for cp in sends:
                cp.start()

        def wait():
            for cp in sends:
                cp.wait_send()
            for cp in recvs:
                cp.wait_recv()
            local.wait()

        return start, wait

    return carry.add([shard], [jax.ShapeDtypeStruct((N_DEV,) + shard.shape, shard.dtype)], {}, 4, 1, build)[0]


def _gather_second(carry, buf):
    def build(ins, outs, sems, locs):
        b = outs[0]
        x, y, c = _place()
        chips = _peers(x, y, c)[1:]
        sends = [_remote(b.at[_dev_index(dev)], b.at[_dev_index(dev)], sems(k), (x, y, 1 - c))
                 for k, dev in enumerate(chips)]
        recvs = [_remote(b.at[_dev_index(dev)], b.at[_dev_index((dev[0], dev[1], 1 - c))], sems(k), (x, y, 1 - c))
                 for k, dev in enumerate(chips)]

        def start():
            for cp in sends:
                cp.start()

        def wait():
            for cp in sends:
                cp.wait_send()
            for cp in recvs:
                cp.wait_recv()

        return start, wait

    return carry.add([buf], [jax.ShapeDtypeStruct(buf.shape, buf.dtype)], {0: 0}, 3, 0, build)[0]


def _scatter_sibling(carry, grad):
    def build(ins, outs, sems, locs):
        x, y, c = _place()
        cp = _remote(ins[0].at[:, 1 - c], outs[0], sems(0), (x, y, 1 - c))
        return cp.start, cp.wait

    shape = (grad.shape[0],) + grad.shape[2:]
    return carry.add([grad], [jax.ShapeDtypeStruct(shape, grad.dtype)], {}, 1, 0, build)[0]


def _scatter_chips(carry, parts, landing, layer):
    def build(ins, outs, sems, locs):
        x, y, c = _place()
        cps = [_remote(ins[0].at[layer, 2 * dev[0] + dev[1]], outs[0].at[k, layer], sems(k), dev)
               for k, dev in enumerate(_peers(x, y, c)[1:])]

        def start():
            for cp in cps:
                cp.start()

        def wait():
            for cp in cps:
                cp.wait()

        return start, wait

    return carry.add([parts, landing], [jax.ShapeDtypeStruct(landing.shape, landing.dtype)], {1: 0}, 3, 0, build)[0]


def _call(name, body, grid, in_specs, out_specs, out_shape, args, scratch=(), carry=None):
    ni, no, ns = len(args), len(out_shape), len(scratch)
    if carry is None or not carry.items:
        res = pl.pallas_call(
            body, name=name, grid=grid, in_specs=list(in_specs), out_specs=list(out_specs), out_shape=list(out_shape),
            scratch_shapes=list(scratch), compiler_params=_cparams(("arbitrary",) * len(grid)))(*args)
        return list(res), []
    nci, nco = len(carry.ins), len(carry.outs)

    def full_body(*refs):
        c_in = refs[ni:ni + nci]
        c_out = refs[ni + nci + no:ni + nci + no + nco]
        sc = refs[ni + nci + no + nco:]
        ops = carry.build(c_in, c_out, sc[ns], sc[ns + 1], sc[ns + 2])
        first = last = None
        for d, n in enumerate(grid):
            pid = pl.program_id(d)
            first = (pid == 0) if first is None else first & (pid == 0)
            last = (pid == n - 1) if last is None else last & (pid == n - 1)

        @pl.when(first)
        def _():
            for start, _w in ops:
                start()

        body(*refs[:ni], *refs[ni + nci:ni + nci + no], *sc[:ns])

        @pl.when(last)
        def _():
            for _s, wait in ops:
                wait()

    hbm = pl.BlockSpec(memory_space=pl.ANY)
    res = pl.pallas_call(
        full_body, name=name, grid=grid,
        in_specs=list(in_specs) + [hbm] * nci, out_specs=list(out_specs) + [hbm] * nco,
        out_shape=list(out_shape) + list(carry.outs),
        input_output_aliases={ni + s: no + d for s, d in carry.alias.items()},
        scratch_shapes=list(scratch) + [pltpu.SemaphoreType.DMA((carry.nsem,)), pltpu.SemaphoreType.DMA((carry.nsem,)),
                                        pltpu.SemaphoreType.DMA((max(carry.nloc, 1),))],
        compiler_params=_cparams(("arbitrary",) * len(grid)))(*args, *carry.ins)
    return list(res[:no]), list(res[no:])


def _comm_only(name, carry):
    return _call(name, lambda *refs: None, (1,), [], [], [], [], carry=carry)[1]


def _matmul(name, a, b, *, pair2=None, trans_a=False, trans_b=False, out_dtype=F32, scale=None, residual=None,
            tm=512, tn=512, tk=512, carry=None):
    if trans_a:
        K, M = a.shape
    else:
        M, K = a.shape
    if trans_b:
        N, Kb = b.shape
    else:
        Kb, N = b.shape
    assert K == Kb, (name, a.shape, b.shape)
    tm = _tile(M, tm, LANES if trans_a else 16)
    tn = _tile(N, tn, LANES)
    tk = _tile(K, tk, 16 if (trans_a and not trans_b) else LANES)
    nk = K // tk
    dims = (((0 if trans_a else 1,), (1 if trans_b else 0,)), ((), ()))
    pairs = [(a, b)] + ([pair2] if pair2 is not None else [])
    npair = len(pairs)

    def body(*refs):
        ab = refs[:2 * npair]
        pos = 2 * npair
        r_ref = None
        if residual is not None:
            r_ref = refs[pos]
            pos += 1
        o_ref = refs[pos]

        def partial():
            t = None
            for q in range(npair):
                d = lax.dot_general(ab[2 * q][...].astype(BF16), ab[2 * q + 1][...].astype(BF16), dims,
                                    preferred_element_type=F32)
                t = d if t is None else t + d
            return t

        def finish(r):
            if scale is not None:
                r = r * scale
            if r_ref is not None:
                r = r + r_ref[...].astype(F32)
            o_ref[...] = r.astype(o_ref.dtype)

        if nk == 1:
            finish(partial())
            return
        acc_ref = refs[pos + 1]
        k = pl.program_id(2)

        @pl.when(k == 0)
        def _():
            acc_ref[...] = partial()

        @pl.when(k > 0)
        def _():
            acc_ref[...] += partial()

        @pl.when(k == nk - 1)
        def _():
            finish(acc_ref[...])

    a_spec = pl.BlockSpec((tk, tm), lambda i, j, k: (k, i)) if trans_a else pl.BlockSpec((tm, tk), lambda i, j, k: (i, k))
    b_spec = pl.BlockSpec((tn, tk), lambda i, j, k: (j, k)) if trans_b else pl.BlockSpec((tk, tn), lambda i, j, k: (k, j))
    in_specs, args = [], []
    for pa, pb in pairs:
        in_specs += [a_spec, b_spec]
        args += [pa, pb]
    if residual is not None:
        in_specs.append(pl.BlockSpec((tm, tn), lambda i, j, k: (i, j)))
        args.append(residual)
    res, extra = _call(
        name, body, (M // tm, N // tn, nk), in_specs, [pl.BlockSpec((tm, tn), lambda i, j, k: (i, j))],
        [jax.ShapeDtypeStruct((M, N), out_dtype)], args,
        scratch=[pltpu.VMEM((tm, tn), F32)] if nk > 1 else [], carry=carry)
    return res[0], extra


def _shard_bf16(name, w, layer, transpose):
    _, R, C = w.shape
    tr = _tile(R, 512, LANES if transpose else 16)

    def body(x_ref, o_ref):
        x = x_ref[...]
        o_ref[...] = (x.T if transpose else x).astype(BF16)

    return pl.pallas_call(
        body, name=name, grid=(R // tr,),
        in_specs=[pl.BlockSpec((None, tr, C), lambda i: (layer, i, 0))],
        out_specs=pl.BlockSpec((C, tr), lambda i: (0, i)) if transpose else pl.BlockSpec((tr, C), lambda i: (i, 0)),
        out_shape=jax.ShapeDtypeStruct((C, R) if transpose else (R, C), BF16),
        compiler_params=_cparams(("parallel",)),
    )(w)


def _rmsnorm_fwd(name, h, g):
    L, D = h.shape
    tm = _tile(L, 256, 16)

    def body(h_ref, g_ref, o_ref):
        x = h_ref[...]
        r = lax.rsqrt(jnp.mean(x * x, axis=1, keepdims=True) + EPS)
        o_ref[...] = (x * r * g_ref[...]).astype(o_ref.dtype)

    return pl.pallas_call(
        body, name=name, grid=(L // tm,),
        in_specs=[pl.BlockSpec((tm, D), lambda i: (i, 0)), pl.BlockSpec((1, D), lambda i: (0, 0))],
        out_specs=pl.BlockSpec((tm, D), lambda i: (i, 0)),
        out_shape=jax.ShapeDtypeStruct((L, D), BF16),
        compiler_params=_cparams(("parallel",)),
    )(h, g)


def _rmsnorm_bwd(name, dy, h, g, dres):
    L, D = h.shape
    tm = _tile(L, 256, 16)

    def body(dy_ref, h_ref, g_ref, dres_ref, dh_ref, dh16_ref, dg_ref):
        i = pl.program_id(0)
        x = h_ref[...]
        dyv = dy_ref[...].astype(F32)
        r = lax.rsqrt(jnp.mean(x * x, axis=1, keepdims=True) + EPS)
        w = dyv * g_ref[...]
        proj = jnp.sum(w * x, axis=1, keepdims=True) * (1.0 / D)
        dh = dres_ref[...] + r * w - x * (r * r * r * proj)
        dh_ref[...] = dh
        dh16_ref[...] = dh.astype(BF16)

        @pl.when(i == 0)
        def _():
            dg_ref[...] = jnp.zeros_like(dg_ref)

        dg_ref[...] += jnp.sum(dyv * x * r, axis=0, keepdims=True)

    return pl.pallas_call(
        body, name=name, grid=(L // tm,),
        in_specs=[pl.BlockSpec((tm, D), lambda i: (i, 0)), pl.BlockSpec((tm, D), lambda i: (i, 0)),
                  pl.BlockSpec((1, D), lambda i: (0, 0)), pl.BlockSpec((tm, D), lambda i: (i, 0))],
        out_specs=[pl.BlockSpec((tm, D), lambda i: (i, 0)), pl.BlockSpec((tm, D), lambda i: (i, 0)),
                   pl.BlockSpec((1, D), lambda i: (0, 0))],
        out_shape=[jax.ShapeDtypeStruct((L, D), F32), jax.ShapeDtypeStruct((L, D), BF16),
                   jax.ShapeDtypeStruct((1, D), F32)],
        compiler_params=_cparams(("arbitrary",)),
    )(dy, h, g, dres)


def _ffn_up(name, xn, wgT, wuT, carry=None):
    L, D = xn.shape
    F = wgT.shape[0]
    tm = _tile(L, 544, 16)
    tn = _tile(F, 512, LANES)

    def body(x_ref, wg_ref, wu_ref, g_ref, u_ref, a_ref):
        x = x_ref[...]
        g = _dot_nt(x, wg_ref[...])
        u = _dot_nt(x, wu_ref[...])
        g_ref[...] = g.astype(BF16)
        u_ref[...] = u.astype(BF16)
        a_ref[...] = (g * _sigmoid(g) * u).astype(BF16)

    o_spec = pl.BlockSpec((tm, tn), lambda i, j: (i, j))
    o_shape = jax.ShapeDtypeStruct((L, F), BF16)
    return _call(
        name, body, (L // tm, F // tn),
        [pl.BlockSpec((tm, D), lambda i, j: (i, 0)), pl.BlockSpec((tn, D), lambda i, j: (j, 0)),
         pl.BlockSpec((tn, D), lambda i, j: (j, 0))],
        [o_spec, o_spec, o_spec], [o_shape, o_shape, o_shape], [xn, wgT, wuT], carry=carry)


def _ffn_bwd_act(name, dh, wd, gate, up, carry=None):
    L, D = dh.shape
    F = wd.shape[0]
    tm = _tile(L, 544, 16)
    tn = _tile(F, 512, LANES)

    def body(dh_ref, wd_ref, g_ref, u_ref, dg_ref, du_ref, a_ref):
        da = 0.5 * _dot_nt(dh_ref[...].astype(BF16), wd_ref[...])
        g = g_ref[...].astype(F32)
        u = u_ref[...].astype(F32)
        sg = _sigmoid(g)
        silu = g * sg
        dg_ref[...] = (da * u * (sg * (1.0 + g * (1.0 - sg)))).astype(BF16)
        du_ref[...] = (da * silu).astype(BF16)
        a_ref[...] = (silu * u).astype(BF16)

    o_spec = pl.BlockSpec((tm, tn), lambda i, j: (i, j))
    o_shape = jax.ShapeDtypeStruct((L, F), BF16)
    return _call(
        name, body, (L // tm, F // tn),
        [pl.BlockSpec((tm, D), lambda i, j: (i, 0)), pl.BlockSpec((tn, D), lambda i, j: (j, 0)), o_spec, o_spec],
        [o_spec, o_spec, o_spec], [o_shape, o_shape, o_shape], [dh, wd, gate, up], carry=carry)


class _Cfg:
    def __init__(self, S, D, hf, hs, hkv):
        self.S, self.D, self.L = S, D, S + BLOCK
        self.hf, self.hs, self.hkv = hf, hs, hkv
        self.wf, self.ws = hf * HEAD_DIM, hs * HEAD_DIM
        self.group = hs // hkv
        self.cs = 2 * LANES // hf
        self.n_meta = 16
        self.pad = BLOCK - self.n_meta
        self.o_fk = self.wf
        self.o_fv = 2 * self.wf
        self.o_sq = 3 * self.wf
        self.o_sk = self.o_sq + self.ws
        self.o_sv = self.o_sk + LANES
        self.o_fz = self.o_sv + LANES
        self.up = self.o_fz + LANES
        assert hkv == 2 and hf % 2 == 0 and self.group % 2 == 0 and self.cs % 8 == 0
        assert self.o_sq % self.ws == 0 and (2 * self.wf) % LANES == 0


def _head_norm(x, gain, lo, mult):
    r = lax.rsqrt(_half_sum(x * x, lo) * (1.0 / HEAD_DIM) + EPS)
    return x * r * (gain * mult)


def _head_norm_bwd(dy, x, gain, lo, mult):
    r = lax.rsqrt(_half_sum(x * x, lo) * (1.0 / HEAD_DIM) + EPS)
    w = dy * (gain * mult)
    proj = _half_sum(w * x, lo) * (1.0 / HEAD_DIM)
    dx = r * w - x * (r * r * r * proj)
    dgain = jnp.sum(dy * mult * x * r, axis=0, keepdims=True)
    return dx, dgain


def _dup(x, lo):
    xr = pltpu.roll(x, 64, 1)
    return jnp.where(lo, x, xr), jnp.where(lo, xr, x)


def _mix_prep(name, cfg, u, gq, gk, gsq, gsk, bias):
    L, wf, ws = cfg.L, cfg.wf, cfg.ws
    T = BLOCK
    npf, nps = wf // LANES, ws // LANES

    def body(fqk_ref, fv_ref, sq_ref, sk_ref, sv_ref, fz_ref, gq_ref, gk_ref, gsq_ref, gsk_ref, b_ref,
             qn_ref, kn_ref, fvo_ref, sqn_ref, skd_ref, svd_ref, c_ref, ct_ref, carry_ref):
        i = pl.program_id(0)
        lo = _lane((T, LANES)) < HEAD_DIM
        for p in range(npf):
            sl = slice(p * LANES, (p + 1) * LANES)
            qn_ref[:, sl] = _head_norm(fqk_ref[:, sl], gq_ref[...], lo, SCALE).astype(BF16)
            kn_ref[:, sl] = _head_norm(fqk_ref[:, wf + p * LANES: wf + (p + 1) * LANES], gk_ref[...], lo, 1.0).astype(BF16)
        fvo_ref[...] = fv_ref[...].astype(BF16)
        for p in range(nps):
            sl = slice(p * LANES, (p + 1) * LANES)
            sqn_ref[:, sl] = _head_norm(sq_ref[:, sl], gsq_ref[...], lo, SCALE).astype(BF16)
        k0, k1 = _dup(_head_norm(sk_ref[...], gsk_ref[...], lo, 1.0), lo)
        skd_ref[:, :LANES] = k0.astype(BF16)
        skd_ref[:, LANES:] = k1.astype(BF16)
        v0, v1 = _dup(sv_ref[...], lo)
        svd_ref[:, :LANES] = v0.astype(BF16)
        svd_ref[:, LANES:] = v1.astype(BF16)

        @pl.when(i == 0)
        def _():
            carry_ref[...] = jnp.zeros_like(carry_ref)

        z = fz_ref[...] + b_ref[...]
        lf = jnp.minimum(z, 0.0) - jnp.log(1.0 + jnp.exp(-jnp.abs(z)))
        row = lax.broadcasted_iota(jnp.int32, (T, T), 0)
        col = lax.broadcasted_iota(jnp.int32, (T, T), 1)
        tri = jnp.where(col <= row, 1.0, 0.0).astype(F32)
        c = jnp.dot(tri, lf, precision=HI, preferred_element_type=F32) + carry_ref[0:1, :]
        c_ref[...] = c
        ct_ref[...] = c.T
        carry_ref[0:1, :] = c_ref[T - 1:T, :]

    def rows(w, cb):
        return pl.BlockSpec((T, w), lambda i, cb=cb: (i, cb))

    vec = pl.BlockSpec((1, LANES), lambda i: (0, 0))
    return pl.pallas_call(
        body, name=name, grid=(L // T,),
        in_specs=[rows(2 * wf, 0), rows(wf, 2), rows(ws, cfg.o_sq // ws), rows(LANES, cfg.o_sk // LANES),
                  rows(LANES, cfg.o_sv // LANES), rows(LANES, cfg.o_fz // LANES), vec, vec, vec, vec, vec],
        out_specs=[rows(wf, 0), rows(wf, 0), rows(wf, 0), rows(ws, 0), rows(2 * LANES, 0), rows(2 * LANES, 0),
                   rows(LANES, 0), pl.BlockSpec((LANES, T), lambda i: (0, i))],
        out_shape=[jax.ShapeDtypeStruct((L, wf), BF16)] * 3 + [jax.ShapeDtypeStruct((L, ws), BF16)]
        + [jax.ShapeDtypeStruct((L, 2 * LANES), BF16)] * 2
        + [jax.ShapeDtypeStruct((L, LANES), F32), jax.ShapeDtypeStruct((LANES, L), F32)],
        scratch_shapes=[pltpu.VMEM((8, LANES), F32)],
        compiler_params=_cparams(("arbitrary",)),
    )(u, u, u, u, u, u, gq, gk, gsq, gsk, bias)


def _mix_prep_bwd(name, cfg, u, gq, gk, gsq, gsk, bias, dqn, dkn, dfv, dsqn, dskd, dsvd, dct, dcq):
    L, wf, ws = cfg.L, cfg.wf, cfg.ws
    T = BLOCK
    nb = L // T
    npf, nps = wf // LANES, ws // LANES

    def body(fqk_ref, sq_ref, sk_ref, fz_ref, gq_ref, gk_ref, gsq_ref, gsk_ref, b_ref,
             dqn_ref, dkn_ref, dfv_ref, dsqn_ref, dskd_ref, dsvd_ref, dct_ref, dcq_ref,
             du_ref, dgq_ref, dgk_ref, dgsq_ref, dgsk_ref, db_ref, carry_ref):
        i = pl.program_id(0)
        lo = _lane((T, LANES)) < HEAD_DIM

        @pl.when(i == 0)
        def _():
            carry_ref[...] = jnp.zeros_like(carry_ref)
            for r in (dgq_ref, dgk_ref, dgsq_ref, dgsk_ref, db_ref):
                r[...] = jnp.zeros_like(r)

        accq = jnp.zeros((1, LANES), F32)
        acck = jnp.zeros((1, LANES), F32)
        for p in range(npf):
            sl = slice(p * LANES, (p + 1) * LANES)
            dx, dg = _head_norm_bwd(dqn_ref[:, sl], fqk_ref[:, sl], gq_ref[...], lo, SCALE)
            du_ref[:, sl] = dx.astype(BF16)
            accq = accq + dg
            slk = slice(wf + p * LANES, wf + (p + 1) * LANES)
            dx, dg = _head_norm_bwd(dkn_ref[:, sl], fqk_ref[:, slk], gk_ref[...], lo, 1.0)
            du_ref[:, slk] = dx.astype(BF16)
            acck = acck + dg
        dgq_ref[...] += accq
        dgk_ref[...] += acck
        du_ref[:, cfg.o_fv:cfg.o_fv + wf] = dfv_ref[...].astype(BF16)
        accs = jnp.zeros((1, LANES), F32)
        for p in range(nps):
            sl = slice(p * LANES, (p + 1) * LANES)
            dx, dg = _head_norm_bwd(dsqn_ref[:, sl], sq_ref[:, sl], gsq_ref[...], lo, SCALE)
            du_ref[:, cfg.o_sq + p * LANES: cfg.o_sq + (p + 1) * LANES] = dx.astype(BF16)
            accs = accs + dg
        dgsq_ref[...] += accs

        def fold(ref):
            a0, a1 = ref[:, :LANES], ref[:, LANES:]
            return jnp.where(lo, a0 + pltpu.roll(a0, 64, 1), a1 + pltpu.roll(a1, 64, 1))

        dx, dg = _head_norm_bwd(fold(dskd_ref), sk_ref[...], gsk_ref[...], lo, 1.0)
        du_ref[:, cfg.o_sk:cfg.o_sk + LANES] = dx.astype(BF16)
        dgsk_ref[...] += dg
        du_ref[:, cfg.o_sv:cfg.o_sv + LANES] = fold(dsvd_ref).astype(BF16)

        dc = dct_ref[...].T + dcq_ref[...]
        row = lax.broadcasted_iota(jnp.int32, (T, T), 0)
        col = lax.broadcasted_iota(jnp.int32, (T, T), 1)
        triu = jnp.where(col >= row, 1.0, 0.0).astype(F32)
        dlf = jnp.dot(triu, dc, precision=HI, preferred_element_type=F32) + carry_ref[0:1, :]
        carry_ref[0:1, :] = dlf[0:1, :]
        z = fz_ref[...] + b_ref[...]
        dz = dlf * _sigmoid(-z)
        du_ref[:, cfg.o_fz:cfg.o_fz + LANES] = dz.astype(BF16)
        db_ref[...] += jnp.sum(dz, axis=0, keepdims=True)

    def rows(w, cb):
        return pl.BlockSpec((T, w), lambda i, cb=cb: (nb - 1 - i, cb))

    vec = pl.BlockSpec((1, LANES), lambda i: (0, 0))
    vshape = jax.ShapeDtypeStruct((1, LANES), F32)
    return pl.pallas_call(
        body, name=name, grid=(nb,),
        in_specs=[rows(2 * wf, 0), rows(ws, cfg.o_sq // ws), rows(LANES, cfg.o_sk // LANES),
                  rows(LANES, cfg.o_fz // LANES), vec, vec, vec, vec, vec,
                  rows(wf, 0), rows(wf, 0), rows(wf, 0), rows(ws, 0), rows(2 * LANES, 0), rows(2 * LANES, 0),
                  pl.BlockSpec((LANES, T), lambda i: (0, nb - 1 - i)), rows(LANES, 0)],
        out_specs=[rows(cfg.up, 0), vec, vec, vec, vec, vec],
        out_shape=[jax.ShapeDtypeStruct((L, cfg.up), BF16)] + [vshape] * 5,
        scratch_shapes=[pltpu.VMEM((8, LANES), F32)],
        compiler_params=_cparams(("arbitrary",)),
    )(u, u, u, u, gq, gk, gsq, gsk, bias, dqn, dkn, dfv, dsqn, dskd, dsvd, dct, dcq)


def _fox_fwd(name, cfg, qn, kn, fv, c, ct, carry=None):
    L, wf, cs = cfg.L, cfg.wf, cfg.cs
    TQ = BLOCK
    TK = _tile(L, 544, 8)
    npairs = wf // LANES
    pad = cfg.pad

    def body(q_ref, k_ref, v_ref, c_ref, ct_ref, o_ref, lse_ref):
        p = pl.program_id(0)
        i = pl.program_id(1)
        lo_q = _lane((TQ, LANES)) < HEAD_DIM
        lane_k = _lane((TK, LANES))
        lo_k = lane_k < HEAD_DIM
        lo_d = lax.broadcasted_iota(jnp.int32, (LANES, TQ), 0) < HEAD_DIM
        first = _lane((TK, 2 * TQ)) < TQ
        q = q_ref[...]
        qs = jnp.concatenate([jnp.where(lo_q, q, jnp.zeros_like(q)), jnp.where(lo_q, jnp.zeros_like(q), q)], axis=0)
        cq = jnp.concatenate([ct_ref[0:1, :], ct_ref[1:2, :]], axis=1)
        qrow = lax.broadcasted_iota(jnp.int32, (TK, 2 * TQ), 1)
        qpos = i * TQ + jnp.where(first, qrow, qrow - TQ)

        def step(j, carry_):
            m, l, acc = carry_
            off = pl.multiple_of(j * TK, 8)
            k = k_ref[pl.ds(off, TK), :]
            v = v_ref[pl.ds(off, TK), :]
            cblk = c_ref[pl.ds(off, TK), :]
            kpos = j * TK + lax.broadcasted_iota(jnp.int32, (TK, 2 * TQ), 0)
            allowed = (kpos <= qpos) & (kpos >= pad)
            ck0 = jnp.sum(jnp.where(lane_k == p * cs, cblk, 0.0), axis=1, keepdims=True)
            ck1 = jnp.sum(jnp.where(lane_k == p * cs + 1, cblk, 0.0), axis=1, keepdims=True)
            s = _dot_nt(k, qs) + cq - jnp.where(first, ck0, ck1)
            s = jnp.where(allowed, s, NEG_INF)
            m_new = jnp.maximum(m, jnp.max(s, axis=0, keepdims=True))
            alpha = jnp.exp(m - m_new)
            pr = jnp.exp(s - m_new)
            l = alpha * l + jnp.sum(pr, axis=0, keepdims=True)
            prb = pr.astype(BF16)
            prs = jnp.concatenate([prb[:, :TQ], prb[:, TQ:]], axis=0)
            vs = jnp.concatenate([jnp.where(lo_k, v, jnp.zeros_like(v)), jnp.where(lo_k, jnp.zeros_like(v), v)], axis=0)
            acc = acc * jnp.where(lo_d, alpha[:, :TQ], alpha[:, TQ:]) + _dot_tn(vs, prs)
            return m_new, l, acc

        init = (jnp.full((1, 2 * TQ), NEG_INF, F32), jnp.zeros((1, 2 * TQ), F32), jnp.zeros((LANES, TQ), F32))
        m, l, acc = lax.fori_loop(0, ((i + 1) * TQ + TK - 1) // TK, step, init)
        o_ref[...] = (acc / jnp.where(lo_d, l[:, :TQ], l[:, TQ:])).T
        lse = m + jnp.log(l)
        lse_ref[...] = jnp.where(lo_d, lse[:, :TQ], lse[:, TQ:]).T

    blk = pl.BlockSpec((TQ, LANES), lambda p, i: (i, p))
    full = pl.BlockSpec((L, LANES), lambda p, i: (0, p))
    return _call(
        name, body, (npairs, L // TQ),
        [blk, full, full, pl.BlockSpec((L, LANES), lambda p, i: (0, 0)), pl.BlockSpec((cs, TQ), lambda p, i: (p, i))],
        [blk, blk], [jax.ShapeDtypeStruct((L, wf), F32)] * 2, [qn, kn, fv, c, ct], carry=carry)


def _fox_bwd(name, cfg, qn, kn, fv, c, ct, o, lse, do, carry=None):
    L, wf, cs = cfg.L, cfg.wf, cfg.cs
    T = BLOCK
    TQ = _tile(L, 544, 8)
    nb = L // T
    nq = L // TQ
    npairs = wf // LANES
    pad = cfg.pad

    def body(q_ref, k_ref, v_ref, c_ref, ct_ref, o_ref, lse_ref, do_ref, dq_ref, dk_ref, dv_ref, dct_ref, dcq_ref,
             cq_ref, lser_ref, dsum_ref, dsacc_ref):
        p = pl.program_id(0)
        j = pl.program_id(1)
        lane = _lane((TQ, LANES))
        lo = lane < HEAD_DIM
        sels = (lo, jnp.logical_not(lo))
        lo_k = _lane((T, LANES)) < HEAD_DIM
        sels_k = (lo_k, jnp.logical_not(lo_k))

        @pl.when((j == 0) & (p == 0))
        def _():
            dcq_ref[...] = jnp.zeros_like(dcq_ref)

        @pl.when(j == 0)
        def _():
            dq_ref[...] = jnp.zeros_like(dq_ref)
            dsacc_ref[...] = jnp.zeros_like(dsacc_ref)
            for t in range(nq):
                rows = slice(t * TQ, (t + 1) * TQ)
                dd = do_ref[rows, :] * o_ref[rows, :]
                lse_b = lse_ref[rows, :]
                cblk = c_ref[rows, :]
                for hh in range(2):
                    dsum_ref[hh, rows, :] = jnp.broadcast_to(
                        jnp.sum(jnp.where(sels[hh], dd, 0.0), axis=1, keepdims=True), (TQ, LANES))
                    lser_ref[hh, rows, :] = jnp.broadcast_to(
                        jnp.sum(jnp.where(lane == hh * HEAD_DIM, lse_b, 0.0), axis=1, keepdims=True), (TQ, LANES))
                    cq_ref[hh, rows, :] = jnp.broadcast_to(
                        jnp.sum(jnp.where(lane == p * cs + hh, cblk, 0.0), axis=1, keepdims=True), (TQ, LANES))

        k = k_ref[...]
        v = v_ref[...]
        kh = tuple(jnp.where(s_, k, jnp.zeros_like(k)) for s_ in sels_k)
        ck = tuple(ct_ref[hh:hh + 1, :] for hh in range(2))
        kpos = j * T + lax.broadcasted_iota(jnp.int32, (TQ, T), 1)

        def step(i, carry_):
            dk, dv, dc0, dc1 = carry_
            dcs = [dc0, dc1]
            off = pl.multiple_of(i * TQ, 8)
            rows = pl.ds(off, TQ)
            q = q_ref[rows, :]
            dov = do_ref[rows, :]
            qpos = i * TQ + lax.broadcasted_iota(jnp.int32, (TQ, T), 0)
            allowed = (kpos <= qpos) & (kpos >= pad)
            dq = jnp.zeros((TQ, LANES), F32)
            for hh in range(2):
                sel = sels[hh]
                qhh = jnp.where(sel, q, jnp.zeros_like(q))
                doh = jnp.where(sel, dov, 0.0).astype(BF16)
                s = _dot_nt(qhh, k) + cq_ref[hh, rows, :] - ck[hh]
                pr = jnp.where(allowed, jnp.exp(jnp.where(allowed, s, NEG_INF) - lser_ref[hh, rows, :]), 0.0)
                ds = pr * (_dot_nt(doh, v) - dsum_ref[hh, rows, :])
                dsb = ds.astype(BF16)
                dv = dv + _dot_tn(pr.astype(BF16), doh)
                dk = dk + _dot_tn(dsb, qhh)
                dq = dq + _dot(dsb, kh[hh])
                dcs[hh] = dcs[hh] - jnp.sum(ds, axis=0, keepdims=True)
                dsacc_ref[hh, rows, :] += ds
            dq_ref[rows, :] += dq
            return dk, dv, dcs[0], dcs[1]

        init = (jnp.zeros((T, LANES), F32), jnp.zeros((T, LANES), F32),
                jnp.zeros((1, T), F32), jnp.zeros((1, T), F32))
        dk, dv, dc0, dc1 = lax.fori_loop((j * T) // TQ, nq, step, init)
        dk_ref[...] = dk
        dv_ref[...] = dv
        dct_ref[...] = jnp.zeros_like(dct_ref)
        dct_ref[0:1, :] = dc0
        dct_ref[1:2, :] = dc1

        @pl.when(j == nb - 1)
        def _():
            for t in range(nq):
                rows = slice(t * TQ, (t + 1) * TQ)
                upd = jnp.zeros((TQ, LANES), F32)
                for hh in range(2):
                    upd = upd + jnp.where(lane == p * cs + hh,
                                          jnp.sum(dsacc_ref[hh, rows, :], axis=1, keepdims=True), 0.0)
                dcq_ref[rows, :] += upd

    blk = pl.BlockSpec((T, LANES), lambda p, j: (j, p))
    full = pl.BlockSpec((L, LANES), lambda p, j: (0, p))
    return _call(
        name, body, (npairs, nb),
        [full, blk, blk, pl.BlockSpec((L, LANES), lambda p, j: (0, 0)), pl.BlockSpec((cs, T), lambda p, j: (p, j)),
         full, full, full],
        [full, blk, blk, pl.BlockSpec((cs, T), lambda p, j: (p, j)), pl.BlockSpec((L, LANES), lambda p, j: (0, 0))],
        [jax.ShapeDtypeStruct((L, wf), F32)] * 3
        + [jax.ShapeDtypeStruct((LANES, L), F32), jax.ShapeDtypeStruct((L, LANES), F32)],
        [qn, kn, fv, c, ct, o, lse, do], scratch=[pltpu.VMEM((2, L, LANES), F32)] * 4, carry=carry)


def _swa_scores(qh, kp, kc, slope, sink, i, pad):
    T = BLOCK
    t = lax.broadcasted_iota(jnp.int32, (T, T), 0)
    s_ = lax.broadcasted_iota(jnp.int32, (T, T), 1)
    dist_c = t - s_
    dist_p = dist_c + T
    ok_c = (dist_c >= 0) & (i * T + s_ >= pad)
    ok_p = (dist_p < T) & ((i - 1) * T + s_ >= pad)
    sp = jnp.where(ok_p, _dot_nt(qh, kp) - slope * dist_p.astype(F32), NEG_INF)
    sc = jnp.where(ok_c, _dot_nt(qh, kc) - slope * dist_c.astype(F32), NEG_INF)
    m = jnp.maximum(jnp.maximum(jnp.max(sp, axis=1, keepdims=True), jnp.max(sc, axis=1, keepdims=True)), sink)
    ep = jnp.exp(sp - m)
    ec = jnp.exp(sc - m)
    es = jnp.exp(sink - m)
    den = jnp.sum(ep, axis=1, keepdims=True) + jnp.sum(ec, axis=1, keepdims=True) + es
    return ep / den, ec / den, es / den


def _swa_fwd(name, cfg, sqn, skd, svd, sinks, slopes, carry=None):
    L, ws, group = cfg.L, cfg.ws, cfg.group
    T = BLOCK
    npairs = ws // LANES
    ppk = group // 2
    pad = cfg.pad

    def body(sink_ref, slope_ref, q_ref, kp_ref, kc_ref, vp_ref, vc_ref, o_ref):
        p = pl.program_id(0)
        i = pl.program_id(1)
        lo = _lane((T, LANES)) < HEAD_DIM
        q = q_ref[...]
        acc = jnp.zeros((T, LANES), F32)
        for hh in range(2):
            sel = lo if hh == 0 else jnp.logical_not(lo)
            qh = jnp.where(sel, q, jnp.zeros_like(q))
            pp, pc, _ = _swa_scores(qh, kp_ref[...], kc_ref[...], slope_ref[2 * p + hh], sink_ref[2 * p + hh], i, pad)
            vp = jnp.where(sel, vp_ref[...], jnp.zeros_like(q))
            vc = jnp.where(sel, vc_ref[...], jnp.zeros_like(q))
            acc = acc + _dot(pp.astype(BF16), vp) + _dot(pc.astype(BF16), vc)
        o_ref[...] = acc

    smem = pl.BlockSpec(memory_space=pltpu.SMEM)
    prev = pl.BlockSpec((T, LANES), lambda p, i: (jnp.maximum(i - 1, 0), p // ppk))
    cur = pl.BlockSpec((T, LANES), lambda p, i: (i, p // ppk))
    blk = pl.BlockSpec((T, LANES), lambda p, i: (i, p))
    return _call(name, body, (npairs, L // T), [smem, smem, blk, prev, cur, prev, cur], [blk],
                 [jax.ShapeDtypeStruct((L, ws), F32)], [sinks, slopes, sqn, skd, skd, svd, svd], carry=carry)


def _swa_bwd(name, cfg, sqn, skd, svd, sinks, slopes, o, do, carry=None):
    L, ws, group, hkv = cfg.L, cfg.ws, cfg.group, cfg.hkv
    T = BLOCK
    gw = group * HEAD_DIM
    ppk = group // 2
    pad = cfg.pad

    def body(sink_ref, slope_ref, q_ref, kp_ref, kc_ref, vp_ref, vc_ref, o_ref, do_ref,
             dq_ref, dk_ref, dv_ref, dsink_ref):
        kv = pl.program_id(0)
        i = pl.program_id(1)
        lane = _lane((T, LANES))
        lo = lane < HEAD_DIM

        @pl.when(i == 0)
        def _():
            dk_ref[...] = jnp.zeros_like(dk_ref)
            dv_ref[...] = jnp.zeros_like(dv_ref)
            dsink_ref[...] = jnp.zeros_like(dsink_ref)

        kp, kc, vp, vc = kp_ref[...], kc_ref[...], vp_ref[...], vc_ref[...]
        dkp = jnp.zeros((T, LANES), F32)
        dkc = jnp.zeros((T, LANES), F32)
        dvp = jnp.zeros((T, LANES), F32)
        dvc = jnp.zeros((T, LANES), F32)
        for pp_ in range(ppk):
            sl = slice(pp_ * LANES, (pp_ + 1) * LANES)
            q = q_ref[:, sl]
            dov = do_ref[:, sl]
            dd = dov * o_ref[:, sl]
            dq = jnp.zeros((T, LANES), F32)
            dsk = jnp.zeros((1, LANES), F32)
            for hh in range(2):
                sel = lo if hh == 0 else jnp.logical_not(lo)
                h = kv * group + 2 * pp_ + hh
                qh = jnp.where(sel, q, jnp.zeros_like(q))
                pp, pc, ps = _swa_scores(qh, kp, kc, slope_ref[h], sink_ref[h], i, pad)
                doh = jnp.where(sel, dov, 0.0).astype(BF16)
                dsum = jnp.sum(jnp.where(sel, dd, 0.0), axis=1, keepdims=True)
                dsp = (pp * (_dot_nt(doh, vp) - dsum)).astype(BF16)
                dsc = (pc * (_dot_nt(doh, vc) - dsum)).astype(BF16)
                khp = jnp.where(sel, kp, jnp.zeros_like(kp))
                khc = jnp.where(sel, kc, jnp.zeros_like(kc))
                dq = dq + _dot(dsp, khp) + _dot(dsc, khc)
                dkp = dkp + _dot_tn(dsp, qh)
                dkc = dkc + _dot_tn(dsc, qh)
                dvp = dvp + _dot_tn(pp.astype(BF16), doh)
                dvc = dvc + _dot_tn(pc.astype(BF16), doh)
                dsk = dsk + jnp.where(lane[0:1, :] == hh * HEAD_DIM, -jnp.sum(ps * dsum), 0.0)
            dq_ref[:, sl] = dq
            dsink_ref[8 * pp_:8 * pp_ + 1, :] += dsk

        cur = pl.multiple_of(i * T, T)
        dk_ref[pl.ds(cur, T), :] += dkc
        dv_ref[pl.ds(cur, T), :] += dvc

        @pl.when(i > 0)
        def _():
            prv = pl.multiple_of((i - 1) * T, T)
            dk_ref[pl.ds(prv, T), :] += dkp
            dv_ref[pl.ds(prv, T), :] += dvp

    smem = pl.BlockSpec(memory_space=pltpu.SMEM)
    prev = pl.BlockSpec((T, LANES), lambda kv, i: (jnp.maximum(i - 1, 0), kv))
    cur = pl.BlockSpec((T, LANES), lambda kv, i: (i, kv))
    qblk = pl.BlockSpec((T, gw), lambda kv, i: (i, kv))
    full = pl.BlockSpec((L, LANES), lambda kv, i: (0, kv))
    return _call(
        name, body, (hkv, L // T), [smem, smem, qblk, prev, cur, prev, cur, qblk, qblk],
        [qblk, full, full, pl.BlockSpec((8 * ppk, LANES), lambda kv, i: (kv, 0))],
        [jax.ShapeDtypeStruct((L, ws), F32), jax.ShapeDtypeStruct((L, 2 * LANES), F32),
         jax.ShapeDtypeStruct((L, 2 * LANES), F32), jax.ShapeDtypeStruct((8 * ppk * hkv, LANES), F32)],
        [sinks, slopes, sqn, skd, skd, svd, svd, o, do], carry=carry)


def _out_norm(name, cfg, o_fox, o_swa, g_fox, g_swa):
    L, wf, ws = cfg.L, cfg.wf, cfg.ws
    tm = _tile(L, 256, 16)

    def body(of_ref, os_ref, gf_ref, gs_ref, o_ref):
        for src, g_ref, lo_, w in ((of_ref, gf_ref, 0, wf), (os_ref, gs_ref, wf, ws)):
            x = src[...]
            r = lax.rsqrt(jnp.mean(x * x, axis=1, keepdims=True) + EPS)
            o_ref[:, lo_:lo_ + w] = (x * r * g_ref[...]).astype(BF16)

    return pl.pallas_call(
        body, name=name, grid=(L // tm,),
        in_specs=[pl.BlockSpec((tm, wf), lambda i: (i, 0)), pl.BlockSpec((tm, ws), lambda i: (i, 0)),
                  pl.BlockSpec((1, wf), lambda i: (0, 0)), pl.BlockSpec((1, ws), lambda i: (0, 0))],
        out_specs=pl.BlockSpec((tm, wf + ws), lambda i: (i, 0)),
        out_shape=jax.ShapeDtypeStruct((L, wf + ws), BF16),
        compiler_params=_cparams(("parallel",)),
    )(o_fox, o_swa, g_fox, g_swa)


def _out_norm_bwd(name, cfg, dcat, o_fox, o_swa, g_fox, g_swa):
    L, wf, ws = cfg.L, cfg.wf, cfg.ws
    tm = _tile(L, 256, 16)

    def body(d_ref, of_ref, os_ref, gf_ref, gs_ref, dof_ref, dos_ref, dgf_ref, dgs_ref):
        i = pl.program_id(0)

        @pl.when(i == 0)
        def _():
            dgf_ref[...] = jnp.zeros_like(dgf_ref)
            dgs_ref[...] = jnp.zeros_like(dgs_ref)

        for src, g_ref, dst, dg_ref, lo_, w in ((of_ref, gf_ref, dof_ref, dgf_ref, 0, wf),
                                                (os_ref, gs_ref, dos_ref, dgs_ref, wf, ws)):
            x = src[...]
            dy = d_ref[:, lo_:lo_ + w]
            r = lax.rsqrt(jnp.mean(x * x, axis=1, keepdims=True) + EPS)
            wv = dy * g_ref[...]
            proj = jnp.sum(wv * x, axis=1, keepdims=True) * (1.0 / w)
            dst[...] = r * wv - x * (r * r * r * proj)
            dg_ref[...] += jnp.sum(dy * x * r, axis=0, keepdims=True)

    return pl.pallas_call(
        body, name=name, grid=(L // tm,),
        in_specs=[pl.BlockSpec((tm, wf + ws), lambda i: (i, 0)), pl.BlockSpec((tm, wf), lambda i: (i, 0)),
                  pl.BlockSpec((tm, ws), lambda i: (i, 0)),
                  pl.BlockSpec((1, wf), lambda i: (0, 0)), pl.BlockSpec((1, ws), lambda i: (0, 0))],
        out_specs=[pl.BlockSpec((tm, wf), lambda i: (i, 0)), pl.BlockSpec((tm, ws), lambda i: (i, 0)),
                   pl.BlockSpec((1, wf), lambda i: (0, 0)), pl.BlockSpec((1, ws), lambda i: (0, 0))],
        out_shape=[jax.ShapeDtypeStruct((L, wf), F32), jax.ShapeDtypeStruct((L, ws), F32),
                   jax.ShapeDtypeStruct((1, wf), F32), jax.ShapeDtypeStruct((1, ws), F32)],
        compiler_params=_cparams(("arbitrary",)),
    )(dcat, o_fox, o_swa, g_fox, g_swa)


def _loss_head(name, h, target):
    L, D = h.shape
    T = BLOCK

    def body(h_ref, t_ref, loss_ref, dh_ref, dh16_ref):
        i = pl.program_id(0)

        @pl.when(i == 0)
        def _():
            loss_ref[...] = jnp.zeros_like(loss_ref)
            dh_ref[...] = jnp.zeros_like(dh_ref)
            dh16_ref[...] = jnp.zeros_like(dh16_ref)

        @pl.when(i > 0)
        def _():
            err = h_ref[...] - t_ref[...]
            dh = err * (1.0 / D)
            dh_ref[...] = dh
            dh16_ref[...] = dh.astype(BF16)
            loss_ref[...] += jnp.sum(err * err) * (0.5 / D)

    return pl.pallas_call(
        body, name=name, grid=(L // T,),
        in_specs=[pl.BlockSpec((T, D), lambda i: (i, 0)), pl.BlockSpec((T, D), lambda i: (jnp.maximum(i - 1, 0), 0))],
        out_specs=[pl.BlockSpec((8, LANES), lambda i: (0, 0)), pl.BlockSpec((T, D), lambda i: (i, 0)),
                   pl.BlockSpec((T, D), lambda i: (i, 0))],
        out_shape=[jax.ShapeDtypeStruct((8, LANES), F32), jax.ShapeDtypeStruct((L, D), F32),
                   jax.ShapeDtypeStruct((L, D), BF16)],
        compiler_params=_cparams(("arbitrary",)),
    )(h, target)


def _pair_add(name, grad, landed, parts, layer, core):
    _, _, r, C = grad.shape

    def body(s_ref, g_ref, l_ref, p_ref, o_ref):
        o_ref[...] = (g_ref[...].astype(F32) + l_ref[...].astype(F32)).astype(o_ref.dtype)

    return pl.pallas_call(
        body, name=name,
        grid_spec=pltpu.PrefetchScalarGridSpec(
            num_scalar_prefetch=1, grid=(4,),
            in_specs=[pl.BlockSpec((None, None, r, C), lambda k, s: (k, s[0], 0, 0)),
                      pl.BlockSpec((None, r, C), lambda k, s: (k, 0, 0)),
                      pl.BlockSpec(memory_space=pl.ANY)],
            out_specs=pl.BlockSpec((None, None, r, C), lambda k, s: (layer, k, 0, 0))),
        out_shape=jax.ShapeDtypeStruct(parts.shape, parts.dtype),
        input_output_aliases={3: 0},
        compiler_params=_cparams(("arbitrary",)),
    )(core, grad, landed, parts)


def _chip_sum(name, part, landed, chip, transpose):
    A, _, r, C = part.shape

    def body(s_ref, p_ref, l0_ref, l1_ref, l2_ref, o_ref):
        total = ((p_ref[...].astype(F32) + l0_ref[...].astype(F32))
                 + (l1_ref[...].astype(F32) + l2_ref[...].astype(F32)))
        o_ref[...] = total.T if transpose else total

    def land(k):
        return pl.BlockSpec((None, None, r, C), lambda a, s, k=k: (k, a, 0, 0))

    out = (C, r) if transpose else (r, C)
    return pl.pallas_call(
        body, name=name,
        grid_spec=pltpu.PrefetchScalarGridSpec(
            num_scalar_prefetch=1, grid=(A,),
            in_specs=[pl.BlockSpec((None, None, r, C), lambda a, s: (a, s[0], 0, 0)), land(0), land(1), land(2)],
            out_specs=pl.BlockSpec((None,) + out, lambda a, s: (a, 0, 0))),
        out_shape=jax.ShapeDtypeStruct((A,) + out, F32),
        compiler_params=_cparams(("parallel",)),
    )(chip, part, landed, landed, landed)


def _gather_sum(name, v):
    R = v.shape[0]

    def body(x_ref, o_ref, buf_ref, send_sems, recv_sems):
        x, y, c = _place()
        me, sibling = (x, y, c), (x, y, 1 - c)
        chips = [(1 - x, y), (x, 1 - y), (1 - x, 1 - y)]

        def rows(dev):
            px, py, pc = dev
            return buf_ref.at[4 * px + 2 * py + pc]

        def copy(k, block, to, src=None):
            return pltpu.make_async_remote_copy(
                src_ref=rows(block) if src is None else src, dst_ref=rows(block),
                send_sem=send_sems.at[k], recv_sem=recv_sems.at[k], device_id=to, device_id_type=MESH)

        first = [copy(0, me, sibling, src=x_ref)]
        first += [copy(1 + j, me, (*chip, c), src=x_ref) for j, chip in enumerate(chips)]
        for cp in first:
            cp.start()
        rows(me)[...] = x_ref[...]
        passed = [copy(4 + j, (*chip, c), sibling) for j, chip in enumerate(chips)]
        for j, chip in enumerate(chips):
            copy(1 + j, (*chip, c), me).wait_recv()
            passed[j].start()
        copy(0, sibling, me).wait_recv()
        for j, chip in enumerate(chips):
            copy(4 + j, (*chip, 1 - c), me).wait_recv()
        for cp in first + passed:
            cp.wait_send()
        acc = buf_ref[0]
        for d in range(1, N_DEV):
            acc = acc + buf_ref[d]
        o_ref[...] = acc

    vm = pl.BlockSpec(memory_space=pltpu.VMEM)
    return pl.pallas_call(
        body, name=name, in_specs=[vm], out_specs=vm,
        out_shape=jax.ShapeDtypeStruct((R, LANES), F32),
        scratch_shapes=[pltpu.VMEM((N_DEV, R, LANES), F32), pltpu.SemaphoreType.DMA((7,)), pltpu.SemaphoreType.DMA((7,))],
    )(v)


def _adamw(name, g, w, m, v):
    R, C = g.shape
    tr = _tile(R, max(8, (1 << 18) // max(C, 1) // 8 * 8), 8)

    def body(g_ref, w_ref, m_ref, v_ref, d_ref, nm_ref, nv_ref):
        gv = g_ref[...]
        nm = ADAM_B1 * m_ref[...] + (1.0 - ADAM_B1) * gv
        nv = ADAM_B2 * v_ref[...] + (1.0 - ADAM_B2) * (gv * gv)
        m_hat = nm / (1.0 - ADAM_B1 ** ADAM_STEP)
        v_hat = nv / (1.0 - ADAM_B2 ** ADAM_STEP)
        d_ref[...] = -ADAM_LR * (m_hat / (jnp.sqrt(v_hat) + ADAM_EPS) + ADAM_WD * w_ref[...])
        nm_ref[...] = nm
        nv_ref[...] = nv

    blk = pl.BlockSpec((tr, C), lambda i: (i, 0))
    shp = jax.ShapeDtypeStruct((R, C), F32)
    return pl.pallas_call(
        body, name=name, grid=(R // tr,),
        in_specs=[blk] * 4, out_specs=[blk] * 3, out_shape=[shp] * 3,
        compiler_params=_cparams(("parallel",)),
    )(g, w, m, v)


def _adamw_nd(name, g, w, m, v):
    shape = w.shape
    flat = [a.reshape(-1, shape[-1]) for a in (g, w, m, v)]
    return tuple(o.reshape(shape) for o in _adamw(name, *flat))


def _scatter_heads(cfg, vals):
    v = jnp.pad(vals.reshape(cfg.hf // 2, 2), ((0, 0), (0, cfg.cs - 2)))
    return v.reshape(1, LANES)


def _gather_heads(cfg, row):
    return row.reshape(cfg.hf // 2, cfg.cs)[:, :2].reshape(cfg.hf)


def _permute_w_in(cfg, w_in_t):
    wf, hf = cfg.wf, cfg.hf
    o = 3 * wf
    cols = w_in_t.shape[1]
    fz = w_in_t[o:o + hf].reshape(hf // 2, 2, cols)
    fz_blk = jnp.pad(fz, ((0, 0), (0, cfg.cs - 2), (0, 0))).reshape(LANES, cols)
    return jnp.concatenate([w_in_t[:o], w_in_t[o + hf:], fz_blk], axis=0)


def _unpermute_dw_in(cfg, dwp):
    wf, hf = cfg.wf, cfg.hf
    o = 3 * wf
    cols = dwp.shape[1]
    fz = dwp[cfg.o_fz:].reshape(hf // 2, cfg.cs, cols)[:, :2].reshape(hf, cols)
    return jnp.concatenate([dwp[:o], fz, dwp[o:cfg.o_fz]], axis=0)


def _pair_gain(g):
    return jnp.tile(g, 2)[None]


def _fold_pair(dg):
    return dg[0, :HEAD_DIM] + dg[0, HEAD_DIM:]


def kernel(x, meta_tokens, ffn1_norm, ffn1_w_gate, ffn1_w_up, ffn1_w_down, mix_norm, w_in, b_forget, fox_q_norm, fox_k_norm, swa_q_norm, swa_k_norm, swa_sinks, fox_out_norm, swa_out_norm, w_out, ffn2_norm, ffn2_w_gate, ffn2_w_up, ffn2_w_down, loss_target, m_meta_tokens, m_ffn1_norm, m_ffn1_w_gate, m_ffn1_w_up, m_ffn1_w_down, m_mix_norm, m_w_in, m_b_forget, m_fox_q_norm, m_fox_k_norm, m_swa_q_norm, m_swa_k_norm, m_swa_sinks, m_fox_out_norm, m_swa_out_norm, m_w_out, m_ffn2_norm, m_ffn2_w_gate, m_ffn2_w_up, m_ffn2_w_down, v_meta_tokens, v_ffn1_norm, v_ffn1_w_gate, v_ffn1_w_up, v_ffn1_w_down, v_mix_norm, v_w_in, v_b_forget, v_fox_q_norm, v_fox_k_norm, v_swa_q_norm, v_swa_k_norm, v_swa_sinks, v_fox_out_norm, v_swa_out_norm, v_w_out, v_ffn2_norm, v_ffn2_w_gate, v_ffn2_w_up, v_ffn2_w_down):
    weights = dict(meta_tokens=meta_tokens, ffn1_norm=ffn1_norm, ffn1_w_gate=ffn1_w_gate, ffn1_w_up=ffn1_w_up,
                   ffn1_w_down=ffn1_w_down, mix_norm=mix_norm, w_in=w_in, b_forget=b_forget, fox_q_norm=fox_q_norm,
                   fox_k_norm=fox_k_norm, swa_q_norm=swa_q_norm, swa_k_norm=swa_k_norm, swa_sinks=swa_sinks,
                   fox_out_norm=fox_out_norm, swa_out_norm=swa_out_norm, w_out=w_out, ffn2_norm=ffn2_norm,
                   ffn2_w_gate=ffn2_w_gate, ffn2_w_up=ffn2_w_up, ffn2_w_down=ffn2_w_down)
    mom_m = dict(meta_tokens=m_meta_tokens, ffn1_norm=m_ffn1_norm, ffn1_w_gate=m_ffn1_w_gate, ffn1_w_up=m_ffn1_w_up,
                 ffn1_w_down=m_ffn1_w_down, mix_norm=m_mix_norm, w_in=m_w_in, b_forget=m_b_forget,
                 fox_q_norm=m_fox_q_norm, fox_k_norm=m_fox_k_norm, swa_q_norm=m_swa_q_norm, swa_k_norm=m_swa_k_norm,
                 swa_sinks=m_swa_sinks, fox_out_norm=m_fox_out_norm, swa_out_norm=m_swa_out_norm, w_out=m_w_out,
                 ffn2_norm=m_ffn2_norm, ffn2_w_gate=m_ffn2_w_gate, ffn2_w_up=m_ffn2_w_up, ffn2_w_down=m_ffn2_w_down)
    mom_v = dict(meta_tokens=v_meta_tokens, ffn1_norm=v_ffn1_norm, ffn1_w_gate=v_ffn1_w_gate, ffn1_w_up=v_ffn1_w_up,
                 ffn1_w_down=v_ffn1_w_down, mix_norm=v_mix_norm, w_in=v_w_in, b_forget=v_b_forget,
                 fox_q_norm=v_fox_q_norm, fox_k_norm=v_fox_k_norm, swa_q_norm=v_swa_q_norm, swa_k_norm=v_swa_k_norm,
                 swa_sinks=v_swa_sinks, fox_out_norm=v_fox_out_norm, swa_out_norm=v_swa_out_norm, w_out=v_w_out,
                 ffn2_norm=v_ffn2_norm, ffn2_w_gate=v_ffn2_w_gate, ffn2_w_up=v_ffn2_w_up, ffn2_w_down=v_ffn2_w_down)
    names = list(weights)

    _, S, D = x.shape
    depth = ffn1_norm.shape[0]
    hf, hs = b_forget.shape[1], swa_sinks.shape[1]
    U = w_in.shape[2] * N_DEV
    hkv = (U - 3 * HEAD_DIM * hf - hf - HEAD_DIM * hs) // (2 * HEAD_DIM)
    cfg = _Cfg(S, D, hf, hs, hkv)
    n_meta = meta_tokens.shape[0]
    assert n_meta == cfg.n_meta
    x_idx, y_idx, c_idx = _place()
    chip_idx = 2 * x_idx + y_idx
    dev_idx = 2 * chip_idx + c_idx
    core_s = jnp.reshape(c_idx, (1,)).astype(jnp.int32)
    chip_s = jnp.reshape(chip_idx, (1,)).astype(jnp.int32)

    col_sharded = ("ffn1_w_gate", "ffn1_w_up", "w_in", "ffn2_w_gate", "ffn2_w_up")
    use_order = ("ffn1_w_gate", "ffn1_w_up", "ffn1_w_down", "w_in", "w_out", "ffn2_w_gate", "ffn2_w_up", "ffn2_w_down")

    def shard(key):
        k, l = key
        if k == "meta_tokens":
            return meta_tokens
        return _shard_bf16("weight_shard", weights[k], l, k in col_sharded)

    waiting = [("meta_tokens", 0)] + [(k, l) for l in range(depth) for k in use_order]
    halfway = []
    gathered = {}

    def fwd_carry(n_first):
        cy = _Carry()
        second = [(key, _gather_second(cy, buf)) for key, buf in halfway]
        first = [(key, _gather_first(cy, shard(key))) for key in waiting[:n_first]]
        del waiting[:n_first]
        halfway.clear()
        return cy, (first, second)

    def fwd_absorb(extra, plan):
        first, second = plan
        for key, idx in second:
            gathered[key] = extra[idx]
        for key, idx in first:
            halfway.append((key, extra[idx]))

    def weight(k, l):
        key = (k, l)
        while key not in gathered:
            n = 0 if any(key == hk for hk, _ in halfway) else waiting.index(key) + 1
            cy, plan = fwd_carry(n)
            fwd_absorb(_comm_only("weights_gather", cy), plan)
        g = gathered[key]
        return g.reshape(-1, g.shape[-1])

    weight("ffn1_w_down", 0)
    meta_full = jnp.swapaxes(weight("meta_tokens", 0).reshape(N_DEV, n_meta, -1), 0, 1).reshape(n_meta, D)
    slopes = jnp.asarray(2.0 ** (-8.0 * np.arange(1, hs + 1) / hs), dtype=F32)

    h = jnp.concatenate([jnp.zeros((cfg.pad, D), F32), meta_full, x[0]], axis=0)
    saved = []
    w_in_p = [None] * depth

    def mm_f(name, a, b, n_first=1, **kw):
        cy, plan = fwd_carry(n_first)
        out, extra = _matmul(name, a, b, carry=cy, **kw)
        fwd_absorb(extra, plan)
        return out

    def ffn_fwd(tag, l, h_in, norm, wg, wu, wd):
        xn = _rmsnorm_fwd(f"{tag}_norm", h_in, norm[l][None])
        wg_t, wu_t = weight(wg, l), weight(wu, l)
        cy, plan = fwd_carry(1)
        (gate, up, act), extra = _ffn_up(f"{tag}_up", xn, wg_t, wu_t, carry=cy)
        fwd_absorb(extra, plan)
        h_out = mm_f(f"{tag}_down", act, weight(wd, l), scale=0.5, residual=h_in, tm=544, tn=1024, tk=2816)
        return h_out, (xn, gate, up)

    for l in range(depth):
        st = {"h0": h}
        h, st["ffn1"] = ffn_fwd("ffn1", l, h, ffn1_norm, "ffn1_w_gate", "ffn1_w_up", "ffn1_w_down")
        st["h1"] = h
        xn = _rmsnorm_fwd("mix_norm", h, mix_norm[l][None])
        w_in_p[l] = _permute_w_in(cfg, weight("w_in", l))
        u = mm_f("mix_in", xn, w_in_p[l], trans_b=True, tm=544, tn=640, tk=2048)
        gq, gk = _pair_gain(fox_q_norm[l]), _pair_gain(fox_k_norm[l])
        gsq, gsk = _pair_gain(swa_q_norm[l]), _pair_gain(swa_k_norm[l])
        bias = _scatter_heads(cfg, b_forget[l])
        qn, kn, fv, sqn, skd, svd, c, ct = _mix_prep("mix_prep", cfg, u, gq, gk, gsq, gsk, bias)
        cy, plan = fwd_carry(2)
        (o_fox, lse), extra = _fox_fwd("fox_fwd", cfg, qn, kn, fv, c, ct, carry=cy)
        fwd_absorb(extra, plan)
        cy, plan = fwd_carry(1)
        (o_swa,), extra = _swa_fwd("swa_fwd", cfg, sqn, skd, svd, swa_sinks[l], slopes, carry=cy)
        fwd_absorb(extra, plan)
        o_cat = _out_norm("out_norm", cfg, o_fox, o_swa, fox_out_norm[l][None], swa_out_norm[l][None])
        st["mix"] = (xn, u, gq, gk, gsq, gsk, bias, qn, kn, fv, sqn, skd, svd, c, ct, o_fox, lse, o_swa, o_cat)
        h = mm_f("mix_out", o_cat, weight("w_out", l), n_first=0, residual=h, tm=544, tn=1024, tk=2048)
        st["h2"] = h
        h, st["ffn2"] = ffn_fwd("ffn2", l, h, ffn2_norm, "ffn2_w_gate", "ffn2_w_up", "ffn2_w_down")
        saved.append(st)

    loss_blk, dh, dh16 = _loss_head("loss_head", h, loss_target[0])

    small = {k: [None] * depth for k in names if k not in use_order and k != "meta_tokens"}
    parts, landing = {}, {}
    to_sibling, to_chips = [], []

    def bwd_carry(n_chips):
        cy = _Carry()
        t1, t3 = [], []
        for k, l in list(to_chips):
            if len(t3) < n_chips and all(k != k3 for k3, _ in t3):
                to_chips.remove((k, l))
                t3.append((k, _scatter_chips(cy, parts[k], landing[k], l)))
        while to_sibling:
            k, l, g = to_sibling.pop(0)
            t1.append((k, l, g, _scatter_sibling(cy, g)))
        return cy, (t1, t3)

    def bwd_absorb(extra, plan):
        t1, t3 = plan
        for k, idx in t3:
            landing[k] = extra[idx]
        for k, l, g, idx in t1:
            parts[k] = _pair_add("grads_pair_add", g, extra[idx], parts[k], l, core_s)
            to_chips.append((k, l))

    def emit_grad(k, l, dw):
        r, C = dw.shape[0] // N_DEV, dw.shape[1]
        if k not in parts:
            parts[k] = lax.empty((depth, 4, r, C), BF16)
            landing[k] = lax.empty((3, depth, r, C), BF16)
        to_sibling.append((k, l, dw.reshape(4, 2, r, C)))

    def mm_b(name, a, b, n_chips=0, **kw):
        cy, plan = bwd_carry(n_chips)
        out, extra = _matmul(name, a, b, carry=cy, **kw)
        bwd_absorb(extra, plan)
        return out

    def ffn_bwd(tag, l, dh_out, dh_out16, h_in, st_, norm, wg, wu, wd):
        xn, gate, up = st_
        short = 1 if l == 0 else 0
        cy, plan = bwd_carry(1)
        (dgate, dup, act), extra = _ffn_bwd_act(f"{tag}_dact", dh_out16, weight(wd, l), gate, up, carry=cy)
        bwd_absorb(extra, plan)
        emit_grad(wd, l, mm_b(f"{tag}_dwd", act, dh_out16, short, trans_a=True, scale=0.5, out_dtype=BF16,
                              tm=512, tn=1024, tk=2176))
        emit_grad(wg, l, mm_b(f"{tag}_dwg", dgate, xn, short, trans_a=True, out_dtype=BF16, tm=512, tn=1024, tk=2176))
        emit_grad(wu, l, mm_b(f"{tag}_dwu", dup, xn, short, trans_a=True, out_dtype=BF16, tm=512, tn=1024, tk=2176))
        dxn = mm_b(f"{tag}_dxn", dgate, weight(wg, l), 1, pair2=(dup, weight(wu, l)), tm=544, tn=1024, tk=1408)
        dh_in, dh_in16, dg = _rmsnorm_bwd(f"{tag}_dnorm", dxn, h_in, norm[l][None], dh_out)
        return dh_in, dh_in16, dg[0]

    for l in reversed(range(depth)):
        st = saved[l]
        dh, dh16, small["ffn2_norm"][l] = ffn_bwd("ffn2", l, dh, dh16, st["h2"], st["ffn2"], ffn2_norm,
                                                   "ffn2_w_gate", "ffn2_w_up", "ffn2_w_down")
        xn, u, gq, gk, gsq, gsk, bias, qn, kn, fv, sqn, skd, svd, c, ct, o_fox, lse, o_swa, o_cat = st["mix"]
        dcat = mm_b("mix_dcat", dh16, weight("w_out", l), int(l == 0), trans_b=True, tm=544, tn=1024, tk=2048)
        emit_grad("w_out", l, mm_b("mix_dwout", o_cat, dh16, int(l == 0), trans_a=True, out_dtype=BF16, tm=512, tn=1024, tk=2176))
        do_fox, do_swa, dgf, dgs = _out_norm_bwd("out_norm_bwd", cfg, dcat, o_fox, o_swa,
                                                 fox_out_norm[l][None], swa_out_norm[l][None])
        small["fox_out_norm"][l], small["swa_out_norm"][l] = dgf[0], dgs[0]
        cy, plan = bwd_carry(3)
        (dqn, dkn, dfv, dct, dcq), extra = _fox_bwd("fox_bwd", cfg, qn, kn, fv, c, ct, o_fox, lse, do_fox, carry=cy)
        bwd_absorb(extra, plan)
        cy, plan = bwd_carry(1)
        (dsqn, dskd, dsvd, dsink), extra = _swa_bwd("swa_bwd", cfg, sqn, skd, svd, swa_sinks[l], slopes, o_swa, do_swa,
                                                    carry=cy)
        bwd_absorb(extra, plan)
        small["swa_sinks"][l] = dsink.reshape(hs // 2, 8, LANES)[:, 0, ::HEAD_DIM].reshape(hs)
        du, dgq, dgk, dgsq, dgsk, db = _mix_prep_bwd("mix_prep_bwd", cfg, u, gq, gk, gsq, gsk, bias,
                                                     dqn, dkn, dfv, dsqn, dskd, dsvd, dct, dcq)
        small["fox_q_norm"][l], small["fox_k_norm"][l] = _fold_pair(dgq), _fold_pair(dgk)
        small["swa_q_norm"][l], small["swa_k_norm"][l] = _fold_pair(dgsq), _fold_pair(dgsk)
        small["b_forget"][l] = _gather_heads(cfg, db[0])
        dwp = mm_b("mix_dwin", du, xn, int(l == 0), trans_a=True, out_dtype=BF16, tm=640, tn=1024, tk=2176)
        emit_grad("w_in", l, _unpermute_dw_in(cfg, dwp))
        dxn = mm_b("mix_dxn", du, w_in_p[l], int(l == 0), tm=544, tn=1024, tk=4480)
        dh, dh16, dg = _rmsnorm_bwd("mix_dnorm", dxn, st["h1"], mix_norm[l][None], dh)
        small["mix_norm"][l] = dg[0]
        dh, dh16, small["ffn1_norm"][l] = ffn_bwd("ffn1", l, dh, dh16, st["h0"], st["ffn1"], ffn1_norm,
                                                   "ffn1_w_gate", "ffn1_w_up", "ffn1_w_down")

    grad_x = dh[BLOCK:][None]
    dmeta = dh[cfg.pad:BLOCK]

    while to_sibling or to_chips:
        cy, plan = bwd_carry(len(use_order))
        bwd_absorb(_comm_only("grads_scatter", cy), plan)

    grads = {}
    for k in use_order:
        grads[k] = _chip_sum("grads_chip_sum", parts[k], landing[k], chip_s, k in col_sharded)

    small_names = list(small)
    pieces = [loss_blk[0, :1], dmeta.reshape(-1)] + [jnp.stack(small[k]).reshape(-1) for k in small_names]
    sizes = [int(p.shape[0]) for p in pieces]
    total = sum(sizes)
    padded = -(-total // (8 * LANES)) * (8 * LANES)
    vec = jnp.concatenate(pieces + [jnp.zeros((padded - total,), F32)]).reshape(-1, LANES)
    summed = _gather_sum("small_gather_sum", vec).reshape(-1)
    offs = np.cumsum([0] + sizes)
    loss = summed[0]
    dmeta_full = summed[offs[1]:offs[2]].reshape(n_meta, D)
    mcols = meta_tokens.shape[1]
    grads["meta_tokens"] = lax.dynamic_slice_in_dim(dmeta_full, dev_idx * mcols, mcols, axis=1)
    for n_, k in enumerate(small_names):
        grads[k] = summed[offs[2 + n_]:offs[3 + n_]].reshape(weights[k].shape)

    delta, new_m, new_v = {}, {}, {}
    for k in names:
        delta[k], new_m[k], new_v[k] = _adamw_nd("adamw", grads[k], weights[k], mom_m[k], mom_v[k])

    return (loss, grad_x, *[grads[k] for k in names], *[delta[k] for k in names],
            *[new_m[k] for k in names], *[new_v[k] for k in names])
```

```python
import functools

import numpy as np
import jax
import jax.numpy as jnp
from jax import lax
from jax.experimental import pallas as pl
from jax.experimental.pallas import tpu as pltpu

F32 = jnp.float32
BF16 = jnp.bfloat16
MESH = pl.DeviceIdType.MESH

HEAD_DIM = 64
BLOCK = 128
LANES = 128
N_DEV = 8
EPS = 1e-6
NEG_INF = -1e30
SCALE = HEAD_DIM ** -0.5

ADAM_LR = 0.001
ADAM_B1 = 0.9
ADAM_B2 = 0.999
ADAM_EPS = 1e-08
ADAM_WD = 0.01
ADAM_STEP = 10

VMEM_BYTES_V7X = 64 * 1024 * 1024
VMEM_LIMIT = VMEM_BYTES_V7X * 3 // 4

NT = (((1,), (1,)), ((), ()))
TN = (((0,), (0,)), ((), ()))
HI = lax.Precision.HIGHEST


def _cparams(sem=None, vmem=VMEM_LIMIT):
    return pltpu.CompilerParams(dimension_semantics=sem, vmem_limit_bytes=vmem)


def _tile(n, pref, mult):
    best = None
    for t in range(mult, min(n, pref) + 1, mult):
        if n % t == 0:
            best = t
    return best if best is not None else n


def _dot(a, b):
    return jnp.dot(a, b, preferred_element_type=F32)


def _dot_nt(a, b):
    return lax.dot_general(a, b, NT, preferred_element_type=F32)


def _dot_tn(a, b):
    return lax.dot_general(a, b, TN, preferred_element_type=F32)


def _lane(shape):
    return lax.broadcasted_iota(jnp.int32, shape, len(shape) - 1)


def _half_sum(x, lo):
    s0 = jnp.sum(jnp.where(lo, x, 0.0), axis=1, keepdims=True)
    s1 = jnp.sum(jnp.where(lo, 0.0, x), axis=1, keepdims=True)
    return jnp.where(lo, s0, s1)


def _sigmoid(x):
    return 1.0 / (1.0 + jnp.exp(-x))


def _place():
    return lax.axis_index("x"), lax.axis_index("y"), lax.axis_index("c")


def _peers(x, y, c):
    return [(x, y, 1 - c), (1 - x, y, c), (x, 1 - y, c), (1 - x, 1 - y, c)]


def _dev_index(dev):
    px, py, pc = dev
    return 4 * px + 2 * py + pc


class _Carry:
    def __init__(self):
        self.ins, self.outs, self.alias, self.items = [], [], {}, []
        self.nsem = self.nloc = 0

    def add(self, ins, outs, alias, nsem, nloc, build):
        i0, o0 = len(self.ins), len(self.outs)
        for src, dst in alias.items():
            self.alias[i0 + src] = o0 + dst
        self.items.append((i0, len(ins), o0, len(outs), self.nsem, self.nloc, build))
        self.ins += ins
        self.outs += outs
        self.nsem += nsem
        self.nloc += nloc
        return list(range(o0, o0 + len(outs)))

    def build(self, in_refs, out_refs, ssem, rsem, lsem):
        ops = []
        for i0, ni, o0, no, s0, l0, fn in self.items:
            ops.append(fn(in_refs[i0:i0 + ni], out_refs[o0:o0 + no],
                          lambda k, s0=s0: (ssem.at[s0 + k], rsem.at[s0 + k]), lambda k, l0=l0: lsem.at[l0 + k]))
        return ops


def _remote(src, dst, sems, dev):
    return pltpu.make_async_remote_copy(src_ref=src, dst_ref=dst, send_sem=sems[0], recv_sem=sems[1],
                                        device_id=dev, device_id_type=MESH)


def _gather_first(carry, shard):
    def build(ins, outs, sems, locs):
        src, buf = ins[0], outs[0]
        x, y, c = _place()
        me = _dev_index((x, y, c))
        peers = _peers(x, y, c)
        local = pltpu.make_async_copy(src, buf.at[me], locs(0))
        sends = [_remote(src, buf.at[me], sems(k), dev) for k, dev in enumerate(peers)]
        recvs = [_remote(src, buf.at[_dev_index(dev)], sems(k), dev) for k, dev in enumerate(peers)]

        def start():
            local.start()
            for cp in sends:
                cp.start()

        def wait():
            for cp in sends:
                cp.wait_send()
            for cp in recvs:
                cp.wait_recv()
            local.wait()

        return start, wait

    return carry.add([shard], [jax.ShapeDtypeStruct((N_DEV,) + shard.shape, shard.dtype)], {}, 4, 1, build)[0]


def _gather_second(carry, buf):
    def build(ins, outs, sems, locs):
        b = outs[0]
        x, y, c = _place()
        chips = _peers(x, y, c)[1:]
        sends = [_remote(b.at[_dev_index(dev)], b.at[_dev_index(dev)], sems(k), (x, y, 1 - c))
                 for k, dev in enumerate(chips)]
        recvs = [_remote(b.at[_dev_index(dev)], b.at[_dev_index((dev[0], dev[1], 1 - c))], sems(k), (x, y, 1 - c))
                 for k, dev in enumerate(chips)]

        def start():
            for cp in sends:
                cp.start()

        def wait():
            for cp in sends:
                cp.wait_send()
            for cp in recvs:
                cp.wait_recv()

        return start, wait

    return carry.add([buf], [jax.ShapeDtypeStruct(buf.shape, buf.dtype)], {0: 0}, 3, 0, build)[0]


def _scatter_sibling(carry, grad):
    def build(ins, outs, sems, locs):
        x, y, c = _place()
        cp = _remote(ins[0].at[:, 1 - c], outs[0], sems(0), (x, y, 1 - c))
        return cp.start, cp.wait

    shape = (grad.shape[0],) + grad.shape[2:]
    return carry.add([grad], [jax.ShapeDtypeStruct(shape, grad.dtype)], {}, 1, 0, build)[0]


def _scatter_chips(carry, parts, landing, layer):
    def build(ins, outs, sems, locs):
        x, y, c = _place()
        cps = [_remote(ins[0].at[layer, 2 * dev[0] + dev[1]], outs[0].at[k, layer], sems(k), dev)
               for k, dev in enumerate(_peers(x, y, c)[1:])]

        def start():
            for cp in cps:
                cp.start()

        def wait():
            for cp in cps:
                cp.wait()

        return start, wait

    return carry.add([parts, landing], [jax.ShapeDtypeStruct(landing.shape, landing.dtype)], {1: 0}, 3, 0, build)[0]


def _call(name, body, grid, in_specs, out_specs, out_shape, args, scratch=(), carry=None):
    ni, no, ns = len(args), len(out_shape), len(scratch)
    if carry is None or not carry.items:
        res = pl.pallas_call(
            body, name=name, grid=grid, in_specs=list(in_specs), out_specs=list(out_specs), out_shape=list(out_shape),
            scratch_shapes=list(scratch), compiler_params=_cparams(("arbitrary",) * len(grid)))(*args)
        return list(res), []
    nci, nco = len(carry.ins), len(carry.outs)

    def full_body(*refs):
        c_in = refs[ni:ni + nci]
        c_out = refs[ni + nci + no:ni + nci + no + nco]
        sc = refs[ni + nci + no + nco:]
        ops = carry.build(c_in, c_out, sc[ns], sc[ns + 1], sc[ns + 2])
        first = last = None
        for d, n in enumerate(grid):
            pid = pl.program_id(d)
            first = (pid == 0) if first is None else first & (pid == 0)
            last = (pid == n - 1) if last is None else last & (pid == n - 1)

        @pl.when(first)
        def _():
            for start, _w in ops:
                start()

        body(*refs[:ni], *refs[ni + nci:ni + nci + no], *sc[:ns])

        @pl.when(last)
        def _():
            for _s, wait in ops:
                wait()

    hbm = pl.BlockSpec(memory_space=pl.ANY)
    res = pl.pallas_call(
        full_body, name=name, grid=grid,
        in_specs=list(in_specs) + [hbm] * nci, out_specs=list(out_specs) + [hbm] * nco,
        out_shape=list(out_shape) + list(carry.outs),
        input_output_aliases={ni + s: no + d for s, d in carry.alias.items()},
        scratch_shapes=list(scratch) + [pltpu.SemaphoreType.DMA((carry.nsem,)), pltpu.SemaphoreType.DMA((carry.nsem,)),
                                        pltpu.SemaphoreType.DMA((max(carry.nloc, 1),))],
        compiler_params=_cparams(("arbitrary",) * len(grid)))(*args, *carry.ins)
    return list(res[:no]), list(res[no:])


def _comm_only(name, carry):
    return _call(name, lambda *refs: None, (1,), [], [], [], [], carry=carry)[1]


def _matmul(name, a, b, *, pair2=None, trans_a=False, trans_b=False, out_dtype=F32, scale=None, residual=None,
            tm=512, tn=512, tk=512, carry=None):
    if trans_a:
        K, M = a.shape
    else:
        M, K = a.shape
    if trans_b:
        N, Kb = b.shape
    else:
        Kb, N = b.shape
    assert K == Kb, (name, a.shape, b.shape)
    tm = _tile(M, tm, LANES if trans_a else 16)
    tn = _tile(N, tn, LANES)
    tk = _tile(K, tk, 16 if (trans_a and not trans_b) else LANES)
    nk = K // tk
    dims = (((0 if trans_a else 1,), (1 if trans_b else 0,)), ((), ()))
    pairs = [(a, b)] + ([pair2] if pair2 is not None else [])
    npair = len(pairs)

    def body(*refs):
        ab = refs[:2 * npair]
        pos = 2 * npair
        r_ref = None
        if residual is not None:
            r_ref = refs[pos]
            pos += 1
        o_ref = refs[pos]

        def partial():
            t = None
            for q in range(npair):
                d = lax.dot_general(ab[2 * q][...].astype(BF16), ab[2 * q + 1][...].astype(BF16), dims,
                                    preferred_element_type=F32)
                t = d if t is None else t + d
            return t

        def finish(r):
            if scale is not None:
                r = r * scale
            if r_ref is not None:
                r = r + r_ref[...].astype(F32)
            o_ref[...] = r.astype(o_ref.dtype)

        if nk == 1:
            finish(partial())
            return
        acc_ref = refs[pos + 1]
        k = pl.program_id(2)

        @pl.when(k == 0)
        def _():
            acc_ref[...] = partial()

        @pl.when(k > 0)
        def _():
            acc_ref[...] += partial()

        @pl.when(k == nk - 1)
        def _():
            finish(acc_ref[...])

    a_spec = pl.BlockSpec((tk, tm), lambda i, j, k: (k, i)) if trans_a else pl.BlockSpec((tm, tk), lambda i, j, k: (i, k))
    b_spec = pl.BlockSpec((tn, tk), lambda i, j, k: (j, k)) if trans_b else pl.BlockSpec((tk, tn), lambda i, j, k: (k, j))
    in_specs, args = [], []
    for pa, pb in pairs:
        in_specs += [a_spec, b_spec]
        args += [pa, pb]
    if residual is not None:
        in_specs.append(pl.BlockSpec((tm, tn), lambda i, j, k: (i, j)))
        args.append(residual)
    res, extra = _call(
        name, body, (M // tm, N // tn, nk), in_specs, [pl.BlockSpec((tm, tn), lambda i, j, k: (i, j))],
        [jax.ShapeDtypeStruct((M, N), out_dtype)], args,
        scratch=[pltpu.VMEM((tm, tn), F32)] if nk > 1 else [], carry=carry)
    return res[0], extra


def _shard_bf16(name, w, layer, layer_axis):
    R, C = w.shape[1 - layer_axis], w.shape[2]
    tr = _tile(R, 512, 16)

    def body(x_ref, o_ref):
        o_ref[...] = x_ref[...].astype(BF16)

    if layer_axis == 0:
        in_spec = pl.BlockSpec((None, tr, C), lambda i: (layer, i, 0))
    else:
        in_spec = pl.BlockSpec((tr, C), lambda i: (i, layer))
        w = w.reshape(R, -1)
    return pl.pallas_call(
        body, name=name, grid=(R // tr,),
        in_specs=[in_spec], out_specs=pl.BlockSpec((tr, C), lambda i: (i, 0)),
        out_shape=jax.ShapeDtypeStruct((R, C), BF16),
        compiler_params=_cparams(("parallel",)),
    )(w)


def _rmsnorm_fwd(name, h, g):
    L, D = h.shape
    tm = _tile(L, 256, 16)

    def body(h_ref, g_ref, o_ref):
        x = h_ref[...]
        r = lax.rsqrt(jnp.mean(x * x, axis=1, keepdims=True) + EPS)
        o_ref[...] = (x * r * g_ref[...]).astype(o_ref.dtype)

    return pl.pallas_call(
        body, name=name, grid=(L // tm,),
        in_specs=[pl.BlockSpec((tm, D), lambda i: (i, 0)), pl.BlockSpec((1, D), lambda i: (0, 0))],
        out_specs=pl.BlockSpec((tm, D), lambda i: (i, 0)),
        out_shape=jax.ShapeDtypeStruct((L, D), BF16),
        compiler_params=_cparams(("parallel",)),
    )(h, g)


def _rmsnorm_bwd(name, dy, h, g, dres):
    L, D = h.shape
    tm = _tile(L, 256, 16)

    def body(dy_ref, h_ref, g_ref, dres_ref, dh_ref, dh16_ref, dg_ref):
        i = pl.program_id(0)
        x = h_ref[...]
        dyv = dy_ref[...].astype(F32)
        r = lax.rsqrt(jnp.mean(x * x, axis=1, keepdims=True) + EPS)
        w = dyv * g_ref[...]
        proj = jnp.sum(w * x, axis=1, keepdims=True) * (1.0 / D)
        dh = dres_ref[...] + r * w - x * (r * r * r * proj)
        dh_ref[...] = dh
        dh16_ref[...] = dh.astype(BF16)

        @pl.when(i == 0)
        def _():
            dg_ref[...] = jnp.zeros_like(dg_ref)

        dg_ref[...] += jnp.sum(dyv * x * r, axis=0, keepdims=True)

    return pl.pallas_call(
        body, name=name, grid=(L // tm,),
        in_specs=[pl.BlockSpec((tm, D), lambda i: (i, 0)), pl.BlockSpec((tm, D), lambda i: (i, 0)),
                  pl.BlockSpec((1, D), lambda i: (0, 0)), pl.BlockSpec((tm, D), lambda i: (i, 0))],
        out_specs=[pl.BlockSpec((tm, D), lambda i: (i, 0)), pl.BlockSpec((tm, D), lambda i: (i, 0)),
                   pl.BlockSpec((1, D), lambda i: (0, 0))],
        out_shape=[jax.ShapeDtypeStruct((L, D), F32), jax.ShapeDtypeStruct((L, D), BF16),
                   jax.ShapeDtypeStruct((1, D), F32)],
        compiler_params=_cparams(("arbitrary",)),
    )(dy, h, g, dres)


def _ffn_up(name, xn, wgT, wuT, carry=None):
    L, D = xn.shape
    F = wgT.shape[0]
    tm = _tile(L, 544, 16)
    tn = _tile(F, 512, LANES)

    def body(x_ref, wg_ref, wu_ref, g_ref, u_ref, a_ref):
        x = x_ref[...]
        g = _dot_nt(x, wg_ref[...])
        u = _dot_nt(x, wu_ref[...])
        g_ref[...] = g.astype(BF16)
        u_ref[...] = u.astype(BF16)
        a_ref[...] = (g * _sigmoid(g) * u).astype(BF16)

    o_spec = pl.BlockSpec((tm, tn), lambda i, j: (i, j))
    o_shape = jax.ShapeDtypeStruct((L, F), BF16)
    return _call(
        name, body, (L // tm, F // tn),
        [pl.BlockSpec((tm, D), lambda i, j: (i, 0)), pl.BlockSpec((tn, D), lambda i, j: (j, 0)),
         pl.BlockSpec((tn, D), lambda i, j: (j, 0))],
        [o_spec, o_spec, o_spec], [o_shape, o_shape, o_shape], [xn, wgT, wuT], carry=carry)


def _ffn_bwd_act(name, dh, wd, gate, up, carry=None):
    L, D = dh.shape
    F = wd.shape[0]
    tm = _tile(L, 544, 16)
    tn = _tile(F, 512, LANES)

    def body(dh_ref, wd_ref, g_ref, u_ref, dg_ref, du_ref, a_ref):
        da = 0.5 * _dot_nt(dh_ref[...].astype(BF16), wd_ref[...])
        g = g_ref[...].astype(F32)
        u = u_ref[...].astype(F32)
        sg = _sigmoid(g)
        silu = g * sg
        dg_ref[...] = (da * u * (sg * (1.0 + g * (1.0 - sg)))).astype(BF16)
        du_ref[...] = (da * silu).astype(BF16)
        a_ref[...] = (silu * u).astype(BF16)

    o_spec = pl.BlockSpec((tm, tn), lambda i, j: (i, j))
    o_shape = jax.ShapeDtypeStruct((L, F), BF16)
    return _call(
        name, body, (L // tm, F // tn),
        [pl.BlockSpec((tm, D), lambda i, j: (i, 0)), pl.BlockSpec((tn, D), lambda i, j: (j, 0)), o_spec, o_spec],
        [o_spec, o_spec, o_spec], [o_shape, o_shape, o_shape], [dh, wd, gate, up], carry=carry)


class _Cfg:
    def __init__(self, S, D, hf, hs, hkv):
        self.S, self.D, self.L = S, D, S + BLOCK
        self.hf, self.hs, self.hkv = hf, hs, hkv
        self.wf, self.ws = hf * HEAD_DIM, hs * HEAD_DIM
        self.group = hs // hkv
        self.cs = 2 * LANES // hf
        self.n_meta = 16
        self.pad = BLOCK - self.n_meta
        self.o_fk = self.wf
        self.o_fv = 2 * self.wf
        self.o_sq = 3 * self.wf
        self.o_sk = self.o_sq + self.ws
        self.o_sv = self.o_sk + LANES
        self.o_fz = self.o_sv + LANES
        self.up = self.o_fz + LANES
        assert hkv == 2 and hf % 2 == 0 and self.group % 2 == 0 and self.cs % 8 == 0
        assert self.o_sq % self.ws == 0 and (2 * self.wf) % LANES == 0


def _head_norm(x, gain, lo, mult):
    r = lax.rsqrt(_half_sum(x * x, lo) * (1.0 / HEAD_DIM) + EPS)
    return x * r * (gain * mult)


def _head_norm_bwd(dy, x, gain, lo, mult):
    r = lax.rsqrt(_half_sum(x * x, lo) * (1.0 / HEAD_DIM) + EPS)
    w = dy * (gain * mult)
    proj = _half_sum(w * x, lo) * (1.0 / HEAD_DIM)
    dx = r * w - x * (r * r * r * proj)
    dgain = jnp.sum(dy * mult * x * r, axis=0, keepdims=True)
    return dx, dgain


def _dup(x, lo):
    xr = pltpu.roll(x, 64, 1)
    return jnp.where(lo, x, xr), jnp.where(lo, xr, x)


def _mix_prep(name, cfg, u, gq, gk, gsq, gsk, bias):
    L, wf, ws = cfg.L, cfg.wf, cfg.ws
    T = BLOCK
    npf, nps = wf // LANES, ws // LANES

    def body(fqk_ref, fv_ref, sq_ref, sk_ref, sv_ref, fz_ref, gq_ref, gk_ref, gsq_ref, gsk_ref, b_ref,
             qn_ref, kn_ref, fvo_ref, sqn_ref, skd_ref, svd_ref, c_ref, ct_ref, carry_ref):
        i = pl.program_id(0)
        lo = _lane((T, LANES)) < HEAD_DIM
        for p in range(npf):
            sl = slice(p * LANES, (p + 1) * LANES)
            qn_ref[:, sl] = _head_norm(fqk_ref[:, sl], gq_ref[...], lo, SCALE).astype(BF16)
            kn_ref[:, sl] = _head_norm(fqk_ref[:, wf + p * LANES: wf + (p + 1) * LANES], gk_ref[...], lo, 1.0).astype(BF16)
        fvo_ref[...] = fv_ref[...].astype(BF16)
        for p in range(nps):
            sl = slice(p * LANES, (p + 1) * LANES)
            sqn_ref[:, sl] = _head_norm(sq_ref[:, sl], gsq_ref[...], lo, SCALE).astype(BF16)
        k0, k1 = _dup(_head_norm(sk_ref[...], gsk_ref[...], lo, 1.0), lo)
        skd_ref[:, :LANES] = k0.astype(BF16)
        skd_ref[:, LANES:] = k1.astype(BF16)
        v0, v1 = _dup(sv_ref[...], lo)
        svd_ref[:, :LANES] = v0.astype(BF16)
        svd_ref[:, LANES:] = v1.astype(BF16)

        @pl.when(i == 0)
        def _():
            carry_ref[...] = jnp.zeros_like(carry_ref)

        z = fz_ref[...] + b_ref[...]
        lf = jnp.minimum(z, 0.0) - jnp.log(1.0 + jnp.exp(-jnp.abs(z)))
        row = lax.broadcasted_iota(jnp.int32, (T, T), 0)
        col = lax.broadcasted_iota(jnp.int32, (T, T), 1)
        tri = jnp.where(col <= row, 1.0, 0.0).astype(F32)
        c = jnp.dot(tri, lf, precision=HI, preferred_element_type=F32) + carry_ref[0:1, :]
        c_ref[...] = c
        ct_ref[...] = c.T
        carry_ref[0:1, :] = c_ref[T - 1:T, :]

    def rows(w, cb):
        return pl.BlockSpec((T, w), lambda i, cb=cb: (i, cb))

    vec = pl.BlockSpec((1, LANES), lambda i: (0, 0))
    return pl.pallas_call(
        body, name=name, grid=(L // T,),
        in_specs=[rows(2 * wf, 0), rows(wf, 2), rows(ws, cfg.o_sq // ws), rows(LANES, cfg.o_sk // LANES),
                  rows(LANES, cfg.o_sv // LANES), rows(LANES, cfg.o_fz // LANES), vec, vec, vec, vec, vec],
        out_specs=[rows(wf, 0), rows(wf, 0), rows(wf, 0), rows(ws, 0), rows(2 * LANES, 0), rows(2 * LANES, 0),
                   rows(LANES, 0), pl.BlockSpec((LANES, T), lambda i: (0, i))],
        out_shape=[jax.ShapeDtypeStruct((L, wf), BF16)] * 3 + [jax.ShapeDtypeStruct((L, ws), BF16)]
        + [jax.ShapeDtypeStruct((L, 2 * LANES), BF16)] * 2
        + [jax.ShapeDtypeStruct((L, LANES), F32), jax.ShapeDtypeStruct((LANES, L), F32)],
        scratch_shapes=[pltpu.VMEM((8, LANES), F32)],
        compiler_params=_cparams(("arbitrary",)),
    )(u, u, u, u, u, u, gq, gk, gsq, gsk, bias)


def _mix_prep_bwd(name, cfg, u, gq, gk, gsq, gsk, bias, dqn, dkn, dfv, dsqn, dskd, dsvd, dct, dcq):
    L, wf, ws = cfg.L, cfg.wf, cfg.ws
    T = BLOCK
    nb = L // T
    npf, nps = wf // LANES, ws // LANES

    def body(fqk_ref, sq_ref, sk_ref, fz_ref, gq_ref, gk_ref, gsq_ref, gsk_ref, b_ref,
             dqn_ref, dkn_ref, dfv_ref, dsqn_ref, dskd_ref, dsvd_ref, dct_ref, dcq_ref,
             du_ref, dgq_ref, dgk_ref, dgsq_ref, dgsk_ref, db_ref, carry_ref):
        i = pl.program_id(0)
        lo = _lane((T, LANES)) < HEAD_DIM

        @pl.when(i == 0)
        def _():
            carry_ref[...] = jnp.zeros_like(carry_ref)
            for r in (dgq_ref, dgk_ref, dgsq_ref, dgsk_ref, db_ref):
                r[...] = jnp.zeros_like(r)

        accq = jnp.zeros((1, LANES), F32)
        acck = jnp.zeros((1, LANES), F32)
        for p in range(npf):
            sl = slice(p * LANES, (p + 1) * LANES)
            dx, dg = _head_norm_bwd(dqn_ref[:, sl], fqk_ref[:, sl], gq_ref[...], lo, SCALE)
            du_ref[:, sl] = dx.astype(BF16)
            accq = accq + dg
            slk = slice(wf + p * LANES, wf + (p + 1) * LANES)
            dx, dg = _head_norm_bwd(dkn_ref[:, sl], fqk_ref[:, slk], gk_ref[...], lo, 1.0)
            du_ref[:, slk] = dx.astype(BF16)
            acck = acck + dg
        dgq_ref[...] += accq
        dgk_ref[...] += acck
        du_ref[:, cfg.o_fv:cfg.o_fv + wf] = dfv_ref[...].astype(BF16)
        accs = jnp.zeros((1, LANES), F32)
        for p in range(nps):
            sl = slice(p * LANES, (p + 1) * LANES)
            dx, dg = _head_norm_bwd(dsqn_ref[:, sl], sq_ref[:, sl], gsq_ref[...], lo, SCALE)
            du_ref[:, cfg.o_sq + p * LANES: cfg.o_sq + (p + 1) * LANES] = dx.astype(BF16)
            accs = accs + dg
        dgsq_ref[...] += accs

        def fold(ref):
            a0, a1 = ref[:, :LANES], ref[:, LANES:]
            return jnp.where(lo, a0 + pltpu.roll(a0, 64, 1), a1 + pltpu.roll(a1, 64, 1))

        dx, dg = _head_norm_bwd(fold(dskd_ref), sk_ref[...], gsk_ref[...], lo, 1.0)
        du_ref[:, cfg.o_sk:cfg.o_sk + LANES] = dx.astype(BF16)
        dgsk_ref[...] += dg
        du_ref[:, cfg.o_sv:cfg.o_sv + LANES] = fold(dsvd_ref).astype(BF16)

        dc = dct_ref[...].T + dcq_ref[...]
        row = lax.broadcasted_iota(jnp.int32, (T, T), 0)
        col = lax.broadcasted_iota(jnp.int32, (T, T), 1)
        triu = jnp.where(col >= row, 1.0, 0.0).astype(F32)
        dlf = jnp.dot(triu, dc, precision=HI, preferred_element_type=F32) + carry_ref[0:1, :]
        carry_ref[0:1, :] = dlf[0:1, :]
        z = fz_ref[...] + b_ref[...]
        dz = dlf * _sigmoid(-z)
        du_ref[:, cfg.o_fz:cfg.o_fz + LANES] = dz.astype(BF16)
        db_ref[...] += jnp.sum(dz, axis=0, keepdims=True)

    def rows(w, cb):
        return pl.BlockSpec((T, w), lambda i, cb=cb: (nb - 1 - i, cb))

    vec = pl.BlockSpec((1, LANES), lambda i: (0, 0))
    vshape = jax.ShapeDtypeStruct((1, LANES), F32)
    return pl.pallas_call(
        body, name=name, grid=(nb,),
        in_specs=[rows(2 * wf, 0), rows(ws, cfg.o_sq // ws), rows(LANES, cfg.o_sk // LANES),
                  rows(LANES, cfg.o_fz // LANES), vec, vec, vec, vec, vec,
                  rows(wf, 0), rows(wf, 0), rows(wf, 0), rows(ws, 0), rows(2 * LANES, 0), rows(2 * LANES, 0),
                  pl.BlockSpec((LANES, T), lambda i: (0, nb - 1 - i)), rows(LANES, 0)],
        out_specs=[rows(cfg.up, 0), vec, vec, vec, vec, vec],
        out_shape=[jax.ShapeDtypeStruct((L, cfg.up), BF16)] + [vshape] * 5,
        scratch_shapes=[pltpu.VMEM((8, LANES), F32)],
        compiler_params=_cparams(("arbitrary",)),
    )(u, u, u, u, gq, gk, gsq, gsk, bias, dqn, dkn, dfv, dsqn, dskd, dsvd, dct, dcq)


def _fox_fwd(name, cfg, qn, kn, fv, c, ct, carry=None):
    L, wf, cs = cfg.L, cfg.wf, cfg.cs
    TQ = BLOCK
    TK = _tile(L, 544, 8)
    npairs = wf // LANES
    pad = cfg.pad

    def body(q_ref, k_ref, v_ref, c_ref, ct_ref, o_ref, lse_ref):
        p = pl.program_id(0)
        i = pl.program_id(1)
        lo_q = _lane((TQ, LANES)) < HEAD_DIM
        lane_k = _lane((TK, LANES))
        lo_k = lane_k < HEAD_DIM
        lo_d = lax.broadcasted_iota(jnp.int32, (LANES, TQ), 0) < HEAD_DIM
        first = _lane((TK, 2 * TQ)) < TQ
        q = q_ref[...]
        qs = jnp.concatenate([jnp.where(lo_q, q, jnp.zeros_like(q)), jnp.where(lo_q, jnp.zeros_like(q), q)], axis=0)
        cq = jnp.concatenate([ct_ref[0:1, :], ct_ref[1:2, :]], axis=1)
        qrow = lax.broadcasted_iota(jnp.int32, (TK, 2 * TQ), 1)
        qpos = i * TQ + jnp.where(first, qrow, qrow - TQ)

        def step(j, carry_):
            m, l, acc = carry_
            off = pl.multiple_of(j * TK, 8)
            k = k_ref[pl.ds(off, TK), :]
            v = v_ref[pl.ds(off, TK), :]
            cblk = c_ref[pl.ds(off, TK), :]
            kpos = j * TK + lax.broadcasted_iota(jnp.int32, (TK, 2 * TQ), 0)
            allowed = (kpos <= qpos) & (kpos >= pad)
            ck0 = jnp.sum(jnp.where(lane_k == p * cs, cblk, 0.0), axis=1, keepdims=True)
            ck1 = jnp.sum(jnp.where(lane_k == p * cs + 1, cblk, 0.0), axis=1, keepdims=True)
            s = _dot_nt(k, qs) + cq - jnp.where(first, ck0, ck1)
            s = jnp.where(allowed, s, NEG_INF)
            m_new = jnp.maximum(m, jnp.max(s, axis=0, keepdims=True))
            alpha = jnp.exp(m - m_new)
            pr = jnp.exp(s - m_new)
            l = alpha * l + jnp.sum(pr, axis=0, keepdims=True)
            prb = pr.astype(BF16)
            prs = jnp.concatenate([prb[:, :TQ], prb[:, TQ:]], axis=0)
            vs = jnp.concatenate([jnp.where(lo_k, v, jnp.zeros_like(v)), jnp.where(lo_k, jnp.zeros_like(v), v)], axis=0)
            acc = acc * jnp.where(lo_d, alpha[:, :TQ], alpha[:, TQ:]) + _dot_tn(vs, prs)
            return m_new, l, acc

        init = (jnp.full((1, 2 * TQ), NEG_INF, F32), jnp.zeros((1, 2 * TQ), F32), jnp.zeros((LANES, TQ), F32))
        m, l, acc = lax.fori_loop(0, ((i + 1) * TQ + TK - 1) // TK, step, init)
        o_ref[...] = (acc / jnp.where(lo_d, l[:, :TQ], l[:, TQ:])).T
        lse = m + jnp.log(l)
        lse_ref[...] = jnp.where(lo_d, lse[:, :TQ], lse[:, TQ:]).T

    blk = pl.BlockSpec((TQ, LANES), lambda p, i: (i, p))
    full = pl.BlockSpec((L, LANES), lambda p, i: (0, p))
    return _call(
        name, body, (npairs, L // TQ),
        [blk, full, full, pl.BlockSpec((L, LANES), lambda p, i: (0, 0)), pl.BlockSpec((cs, TQ), lambda p, i: (p, i))],
        [blk, blk], [jax.ShapeDtypeStruct((L, wf), F32)] * 2, [qn, kn, fv, c, ct], carry=carry)


def _fox_bwd(name, cfg, qn, kn, fv, c, ct, o, lse, do, carry=None):
    L, wf, cs = cfg.L, cfg.wf, cfg.cs
    T = BLOCK
    TQ = _tile(L, 544, 8)
    nb = L // T
    nq = L // TQ
    npairs = wf // LANES
    pad = cfg.pad

    def body(q_ref, k_ref, v_ref, c_ref, ct_ref, o_ref, lse_ref, do_ref, dq_ref, dk_ref, dv_ref, dct_ref, dcq_ref,
             cq_ref, lser_ref, dsum_ref, dsacc_ref):
        p = pl.program_id(0)
        j = pl.program_id(1)
        lane = _lane((TQ, LANES))
        lo = lane < HEAD_DIM
        sels = (lo, jnp.logical_not(lo))
        lo_k = _lane((T, LANES)) < HEAD_DIM
        sels_k = (lo_k, jnp.logical_not(lo_k))

        @pl.when((j == 0) & (p == 0))
        def _():
            dcq_ref[...] = jnp.zeros_like(dcq_ref)

        @pl.when(j == 0)
        def _():
            dq_ref[...] = jnp.zeros_like(dq_ref)
            dsacc_ref[...] = jnp.zeros_like(dsacc_ref)
            for t in range(nq):
                rows = slice(t * TQ, (t + 1) * TQ)
                dd = do_ref[rows, :] * o_ref[rows, :]
                lse_b = lse_ref[rows, :]
                cblk = c_ref[rows, :]
                for hh in range(2):
                    dsum_ref[hh, rows, :] = jnp.broadcast_to(
                        jnp.sum(jnp.where(sels[hh], dd, 0.0), axis=1, keepdims=True), (TQ, LANES))
                    lser_ref[hh, rows, :] = jnp.broadcast_to(
                        jnp.sum(jnp.where(lane == hh * HEAD_DIM, lse_b, 0.0), axis=1, keepdims=True), (TQ, LANES))
                    cq_ref[hh, rows, :] = jnp.broadcast_to(
                        jnp.sum(jnp.where(lane == p * cs + hh, cblk, 0.0), axis=1, keepdims=True), (TQ, LANES))

        k = k_ref[...]
        v = v_ref[...]
        kh = tuple(jnp.where(s_, k, jnp.zeros_like(k)) for s_ in sels_k)
        ck = tuple(ct_ref[hh:hh + 1, :] for hh in range(2))
        kpos = j * T + lax.broadcasted_iota(jnp.int32, (TQ, T), 1)

        def step(i, carry_):
            dk, dv, dc0, dc1 = carry_
            dcs = [dc0, dc1]
            off = pl.multiple_of(i * TQ, 8)
            rows = pl.ds(off, TQ)
            q = q_ref[rows, :]
            dov = do_ref[rows, :]
            qpos = i * TQ + lax.broadcasted_iota(jnp.int32, (TQ, T), 0)
            allowed = (kpos <= qpos) & (kpos >= pad)
            dq = jnp.zeros((TQ, LANES), F32)
            for hh in range(2):
                sel = sels[hh]
                qhh = jnp.where(sel, q, jnp.zeros_like(q))
                doh = jnp.where(sel, dov, 0.0).astype(BF16)
                s = _dot_nt(qhh, k) + cq_ref[hh, rows, :] - ck[hh]
                pr = jnp.where(allowed, jnp.exp(jnp.where(allowed, s, NEG_INF) - lser_ref[hh, rows, :]), 0.0)
                ds = pr * (_dot_nt(doh, v) - dsum_ref[hh, rows, :])
                dsb = ds.astype(BF16)
                dv = dv + _dot_tn(pr.astype(BF16), doh)
                dk = dk + _dot_tn(dsb, qhh)
                dq = dq + _dot(dsb, kh[hh])
                dcs[hh] = dcs[hh] - jnp.sum(ds, axis=0, keepdims=True)
                dsacc_ref[hh, rows, :] += ds
            dq_ref[rows, :] += dq
            return dk, dv, dcs[0], dcs[1]

        init = (jnp.zeros((T, LANES), F32), jnp.zeros((T, LANES), F32),
                jnp.zeros((1, T), F32), jnp.zeros((1, T), F32))
        dk, dv, dc0, dc1 = lax.fori_loop((j * T) // TQ, nq, step, init)
        dk_ref[...] = dk
        dv_ref[...] = dv
        dct_ref[...] = jnp.zeros_like(dct_ref)
        dct_ref[0:1, :] = dc0
        dct_ref[1:2, :] = dc1

        @pl.when(j == nb - 1)
        def _():
            for t in range(nq):
                rows = slice(t * TQ, (t + 1) * TQ)
                upd = jnp.zeros((TQ, LANES), F32)
                for hh in range(2):
                    upd = upd + jnp.where(lane == p * cs + hh,
                                          jnp.sum(dsacc_ref[hh, rows, :], axis=1, keepdims=True), 0.0)
                dcq_ref[rows, :] += upd

    blk = pl.BlockSpec((T, LANES), lambda p, j: (j, p))
    full = pl.BlockSpec((L, LANES), lambda p, j: (0, p))
    return _call(
        name, body, (npairs, nb),
        [full, blk, blk, pl.BlockSpec((L, LANES), lambda p, j: (0, 0)), pl.BlockSpec((cs, T), lambda p, j: (p, j)),
         full, full, full],
        [full, blk, blk, pl.BlockSpec((cs, T), lambda p, j: (p, j)), pl.BlockSpec((L, LANES), lambda p, j: (0, 0))],
        [jax.ShapeDtypeStruct((L, wf), F32)] * 3
        + [jax.ShapeDtypeStruct((LANES, L), F32), jax.ShapeDtypeStruct((L, LANES), F32)],
        [qn, kn, fv, c, ct, o, lse, do], scratch=[pltpu.VMEM((2, L, LANES), F32)] * 4, carry=carry)


def _swa_scores(qh, kp, kc, slope, sink, i, pad):
    T = BLOCK
    t = lax.broadcasted_iota(jnp.int32, (T, T), 0)
    s_ = lax.broadcasted_iota(jnp.int32, (T, T), 1)
    dist_c = t - s_
    dist_p = dist_c + T
    ok_c = (dist_c >= 0) & (i * T + s_ >= pad)
    ok_p = (dist_p < T) & ((i - 1) * T + s_ >= pad)
    sp = jnp.where(ok_p, _dot_nt(qh, kp) - slope * dist_p.astype(F32), NEG_INF)
    sc = jnp.where(ok_c, _dot_nt(qh, kc) - slope * dist_c.astype(F32), NEG_INF)
    m = jnp.maximum(jnp.maximum(jnp.max(sp, axis=1, keepdims=True), jnp.max(sc, axis=1, keepdims=True)), sink)
    ep = jnp.exp(sp - m)
    ec = jnp.exp(sc - m)
    es = jnp.exp(sink - m)
    den = jnp.sum(ep, axis=1, keepdims=True) + jnp.sum(ec, axis=1, keepdims=True) + es
    return ep / den, ec / den, es / den


def _swa_fwd(name, cfg, sqn, skd, svd, sinks, slopes, carry=None):
    L, ws, group = cfg.L, cfg.ws, cfg.group
    T = BLOCK
    npairs = ws // LANES
    ppk = group // 2
    pad = cfg.pad

    def body(sink_ref, slope_ref, q_ref, kp_ref, kc_ref, vp_ref, vc_ref, o_ref):
        p = pl.program_id(0)
        i = pl.program_id(1)
        lo = _lane((T, LANES)) < HEAD_DIM
        q = q_ref[...]
        acc = jnp.zeros((T, LANES), F32)
        for hh in range(2):
            sel = lo if hh == 0 else jnp.logical_not(lo)
            qh = jnp.where(sel, q, jnp.zeros_like(q))
            pp, pc, _ = _swa_scores(qh, kp_ref[...], kc_ref[...], slope_ref[2 * p + hh], sink_ref[2 * p + hh], i, pad)
            vp = jnp.where(sel, vp_ref[...], jnp.zeros_like(q))
            vc = jnp.where(sel, vc_ref[...], jnp.zeros_like(q))
            acc = acc + _dot(pp.astype(BF16), vp) + _dot(pc.astype(BF16), vc)
        o_ref[...] = acc

    smem = pl.BlockSpec(memory_space=pltpu.SMEM)
    prev = pl.BlockSpec((T, LANES), lambda p, i: (jnp.maximum(i - 1, 0), p // ppk))
    cur = pl.BlockSpec((T, LANES), lambda p, i: (i, p // ppk))
    blk = pl.BlockSpec((T, LANES), lambda p, i: (i, p))
    return _call(name, body, (npairs, L // T), [smem, smem, blk, prev, cur, prev, cur], [blk],
                 [jax.ShapeDtypeStruct((L, ws), F32)], [sinks, slopes, sqn, skd, skd, svd, svd], carry=carry)


def _swa_bwd(name, cfg, sqn, skd, svd, sinks, slopes, o, do, carry=None):
    L, ws, group, hkv = cfg.L, cfg.ws, cfg.group, cfg.hkv
    T = BLOCK
    gw = group * HEAD_DIM
    ppk = group // 2
    pad = cfg.pad

    def body(sink_ref, slope_ref, q_ref, kp_ref, kc_ref, vp_ref, vc_ref, o_ref, do_ref,
             dq_ref, dk_ref, dv_ref, dsink_ref):
        kv = pl.program_id(0)
        i = pl.program_id(1)
        lane = _lane((T, LANES))
        lo = lane < HEAD_DIM

        @pl.when(i == 0)
        def _():
            dk_ref[...] = jnp.zeros_like(dk_ref)
            dv_ref[...] = jnp.zeros_like(dv_ref)
            dsink_ref[...] = jnp.zeros_like(dsink_ref)

        kp, kc, vp, vc = kp_ref[...], kc_ref[...], vp_ref[...], vc_ref[...]
        dkp = jnp.zeros((T, LANES), F32)
        dkc = jnp.zeros((T, LANES), F32)
        dvp = jnp.zeros((T, LANES), F32)
        dvc = jnp.zeros((T, LANES), F32)
        for pp_ in range(ppk):
            sl = slice(pp_ * LANES, (pp_ + 1) * LANES)
            q = q_ref[:, sl]
            dov = do_ref[:, sl]
            dd = dov * o_ref[:, sl]
            dq = jnp.zeros((T, LANES), F32)
            dsk = jnp.zeros((1, LANES), F32)
            for hh in range(2):
                sel = lo if hh == 0 else jnp.logical_not(lo)
                h = kv * group + 2 * pp_ + hh
                qh = jnp.where(sel, q, jnp.zeros_like(q))
                pp, pc, ps = _swa_scores(qh, kp, kc, slope_ref[h], sink_ref[h], i, pad)
                doh = jnp.where(sel, dov, 0.0).astype(BF16)
                dsum = jnp.sum(jnp.where(sel, dd, 0.0), axis=1, keepdims=True)
                dsp = (pp * (_dot_nt(doh, vp) - dsum)).astype(BF16)
                dsc = (pc * (_dot_nt(doh, vc) - dsum)).astype(BF16)
                khp = jnp.where(sel, kp, jnp.zeros_like(kp))
                khc = jnp.where(sel, kc, jnp.zeros_like(kc))
                dq = dq + _dot(dsp, khp) + _dot(dsc, khc)
                dkp = dkp + _dot_tn(dsp, qh)
                dkc = dkc + _dot_tn(dsc, qh)
                dvp = dvp + _dot_tn(pp.astype(BF16), doh)
                dvc = dvc + _dot_tn(pc.astype(BF16), doh)
                dsk = dsk + jnp.where(lane[0:1, :] == hh * HEAD_DIM, -jnp.sum(ps * dsum), 0.0)
            dq_ref[:, sl] = dq
            dsink_ref[8 * pp_:8 * pp_ + 1, :] += dsk

        cur = pl.multiple_of(i * T, T)
        dk_ref[pl.ds(cur, T), :] += dkc
        dv_ref[pl.ds(cur, T), :] += dvc

        @pl.when(i > 0)
        def _():
            prv = pl.multiple_of((i - 1) * T, T)
            dk_ref[pl.ds(prv, T), :] += dkp
            dv_ref[pl.ds(prv, T), :] += dvp

    smem = pl.BlockSpec(memory_space=pltpu.SMEM)
    prev = pl.BlockSpec((T, LANES), lambda kv, i: (jnp.maximum(i - 1, 0), kv))
    cur = pl.BlockSpec((T, LANES), lambda kv, i: (i, kv))
    qblk = pl.BlockSpec((T, gw), lambda kv, i: (i, kv))
    full = pl.BlockSpec((L, LANES), lambda kv, i: (0, kv))
    return _call(
        name, body, (hkv, L // T), [smem, smem, qblk, prev, cur, prev, cur, qblk, qblk],
        [qblk, full, full, pl.BlockSpec((8 * ppk, LANES), lambda kv, i: (kv, 0))],
        [jax.ShapeDtypeStruct((L, ws), F32), jax.ShapeDtypeStruct((L, 2 * LANES), F32),
         jax.ShapeDtypeStruct((L, 2 * LANES), F32), jax.ShapeDtypeStruct((8 * ppk * hkv, LANES), F32)],
        [sinks, slopes, sqn, skd, skd, svd, svd, o, do], carry=carry)


def _out_norm(name, cfg, o_fox, o_swa, g_fox, g_swa):
    L, wf, ws = cfg.L, cfg.wf, cfg.ws
    tm = _tile(L, 256, 16)

    def body(of_ref, os_ref, gf_ref, gs_ref, o_ref):
        for src, g_ref, lo_, w in ((of_ref, gf_ref, 0, wf), (os_ref, gs_ref, wf, ws)):
            x = src[...]
            r = lax.rsqrt(jnp.mean(x * x, axis=1, keepdims=True) + EPS)
            o_ref[:, lo_:lo_ + w] = (x * r * g_ref[...]).astype(BF16)

    return pl.pallas_call(
        body, name=name, grid=(L // tm,),
        in_specs=[pl.BlockSpec((tm, wf), lambda i: (i, 0)), pl.BlockSpec((tm, ws), lambda i: (i, 0)),
                  pl.BlockSpec((1, wf), lambda i: (0, 0)), pl.BlockSpec((1, ws), lambda i: (0, 0))],
        out_specs=pl.BlockSpec((tm, wf + ws), lambda i: (i, 0)),
        out_shape=jax.ShapeDtypeStruct((L, wf + ws), BF16),
        compiler_params=_cparams(("parallel",)),
    )(o_fox, o_swa, g_fox, g_swa)


def _out_norm_bwd(name, cfg, dcat, o_fox, o_swa, g_fox, g_swa):
    L, wf, ws = cfg.L, cfg.wf, cfg.ws
    tm = _tile(L, 256, 16)

    def body(d_ref, of_ref, os_ref, gf_ref, gs_ref, dof_ref, dos_ref, dgf_ref, dgs_ref):
        i = pl.program_id(0)

        @pl.when(i == 0)
        def _():
            dgf_ref[...] = jnp.zeros_like(dgf_ref)
            dgs_ref[...] = jnp.zeros_like(dgs_ref)

        for src, g_ref, dst, dg_ref, lo_, w in ((of_ref, gf_ref, dof_ref, dgf_ref, 0, wf),
                                                (os_ref, gs_ref, dos_ref, dgs_ref, wf, ws)):
            x = src[...]
            dy = d_ref[:, lo_:lo_ + w]
            r = lax.rsqrt(jnp.mean(x * x, axis=1, keepdims=True) + EPS)
            wv = dy * g_ref[...]
            proj = jnp.sum(wv * x, axis=1, keepdims=True) * (1.0 / w)
            dst[...] = r * wv - x * (r * r * r * proj)
            dg_ref[...] += jnp.sum(dy * x * r, axis=0, keepdims=True)

    return pl.pallas_call(
        body, name=name, grid=(L // tm,),
        in_specs=[pl.BlockSpec((tm, wf + ws), lambda i: (i, 0)), pl.BlockSpec((tm, wf), lambda i: (i, 0)),
                  pl.BlockSpec((tm, ws), lambda i: (i, 0)),
                  pl.BlockSpec((1, wf), lambda i: (0, 0)), pl.BlockSpec((1, ws), lambda i: (0, 0))],
        out_specs=[pl.BlockSpec((tm, wf), lambda i: (i, 0)), pl.BlockSpec((tm, ws), lambda i: (i, 0)),
                   pl.BlockSpec((1, wf), lambda i: (0, 0)), pl.BlockSpec((1, ws), lambda i: (0, 0))],
        out_shape=[jax.ShapeDtypeStruct((L, wf), F32), jax.ShapeDtypeStruct((L, ws), F32),
                   jax.ShapeDtypeStruct((1, wf), F32), jax.ShapeDtypeStruct((1, ws), F32)],
        compiler_params=_cparams(("arbitrary",)),
    )(dcat, o_fox, o_swa, g_fox, g_swa)


def _loss_head(name, h, target):
    L, D = h.shape
    T = BLOCK

    def body(h_ref, t_ref, loss_ref, dh_ref, dh16_ref):
        i = pl.program_id(0)

        @pl.when(i == 0)
        def _():
            loss_ref[...] = jnp.zeros_like(loss_ref)
            dh_ref[...] = jnp.zeros_like(dh_ref)
            dh16_ref[...] = jnp.zeros_like(dh16_ref)

        @pl.when(i > 0)
        def _():
            err = h_ref[...] - t_ref[...]
            dh = err * (1.0 / D)
            dh_ref[...] = dh
            dh16_ref[...] = dh.astype(BF16)
            loss_ref[...] += jnp.sum(err * err) * (0.5 / D)

    return pl.pallas_call(
        body, name=name, grid=(L // T,),
        in_specs=[pl.BlockSpec((T, D), lambda i: (i, 0)), pl.BlockSpec((T, D), lambda i: (jnp.maximum(i - 1, 0), 0))],
        out_specs=[pl.BlockSpec((8, LANES), lambda i: (0, 0)), pl.BlockSpec((T, D), lambda i: (i, 0)),
                   pl.BlockSpec((T, D), lambda i: (i, 0))],
        out_shape=[jax.ShapeDtypeStruct((8, LANES), F32), jax.ShapeDtypeStruct((L, D), F32),
                   jax.ShapeDtypeStruct((L, D), BF16)],
        compiler_params=_cparams(("arbitrary",)),
    )(h, target)


def _pair_add(name, grad, landed, parts, layer, core):
    _, _, r, C = grad.shape

    def body(s_ref, g_ref, l_ref, p_ref, o_ref):
        o_ref[...] = (g_ref[...].astype(F32) + l_ref[...].astype(F32)).astype(o_ref.dtype)

    return pl.pallas_call(
        body, name=name,
        grid_spec=pltpu.PrefetchScalarGridSpec(
            num_scalar_prefetch=1, grid=(4,),
            in_specs=[pl.BlockSpec((None, None, r, C), lambda k, s: (k, s[0], 0, 0)),
                      pl.BlockSpec((None, r, C), lambda k, s: (k, 0, 0)),
                      pl.BlockSpec(memory_space=pl.ANY)],
            out_specs=pl.BlockSpec((None, None, r, C), lambda k, s: (layer, k, 0, 0))),
        out_shape=jax.ShapeDtypeStruct(parts.shape, parts.dtype),
        input_output_aliases={3: 0},
        compiler_params=_cparams(("arbitrary",)),
    )(core, grad, landed, parts)


def _chip_sum(name, part, landed, chip, layer_axis):
    A, _, r, C = part.shape

    def body(s_ref, p_ref, l0_ref, l1_ref, l2_ref, o_ref):
        o_ref[...] = ((p_ref[...].astype(F32) + l0_ref[...].astype(F32))
                      + (l1_ref[...].astype(F32) + l2_ref[...].astype(F32)))

    def land(k):
        return pl.BlockSpec((None, None, r, C), lambda a, s, k=k: (k, a, 0, 0))

    if layer_axis == 0:
        out_spec, out_shape = pl.BlockSpec((None, r, C), lambda a, s: (a, 0, 0)), (A, r, C)
    else:
        out_spec, out_shape = pl.BlockSpec((r, C), lambda a, s: (0, a)), (r, A * C)
    out = pl.pallas_call(
        body, name=name,
        grid_spec=pltpu.PrefetchScalarGridSpec(
            num_scalar_prefetch=1, grid=(A,),
            in_specs=[pl.BlockSpec((None, None, r, C), lambda a, s: (a, s[0], 0, 0)), land(0), land(1), land(2)],
            out_specs=out_spec),
        out_shape=jax.ShapeDtypeStruct(out_shape, F32),
        compiler_params=_cparams(("parallel",)),
    )(chip, part, landed, landed, landed)
    return out if layer_axis == 0 else out.reshape(r, A, C)


def _gather_sum(name, v):
    R = v.shape[0]

    def body(x_ref, o_ref, buf_ref, send_sems, recv_sems):
        x, y, c = _place()
        me, sibling = (x, y, c), (x, y, 1 - c)
        chips = [(1 - x, y), (x, 1 - y), (1 - x, 1 - y)]

        def rows(dev):
            px, py, pc = dev
            return buf_ref.at[4 * px + 2 * py + pc]

        def copy(k, block, to, src=None):
            return pltpu.make_async_remote_copy(
                src_ref=rows(block) if src is None else src, dst_ref=rows(block),
                send_sem=send_sems.at[k], recv_sem=recv_sems.at[k], device_id=to, device_id_type=MESH)

        first = [copy(0, me, sibling, src=x_ref)]
        first += [copy(1 + j, me, (*chip, c), src=x_ref) for j, chip in enumerate(chips)]
        for cp in first:
            cp.start()
        rows(me)[...] = x_ref[...]
        passed = [copy(4 + j, (*chip, c), sibling) for j, chip in enumerate(chips)]
        for j, chip in enumerate(chips):
            copy(1 + j, (*chip, c), me).wait_recv()
            passed[j].start()
        copy(0, sibling, me).wait_recv()
        for j, chip in enumerate(chips):
            copy(4 + j, (*chip, 1 - c), me).wait_recv()
        for cp in first + passed:
            cp.wait_send()
        acc = buf_ref[0]
        for d in range(1, N_DEV):
            acc = acc + buf_ref[d]
        o_ref[...] = acc

    vm = pl.BlockSpec(memory_space=pltpu.VMEM)
    return pl.pallas_call(
        body, name=name, in_specs=[vm], out_specs=vm,
        out_shape=jax.ShapeDtypeStruct((R, LANES), F32),
        scratch_shapes=[pltpu.VMEM((N_DEV, R, LANES), F32), pltpu.SemaphoreType.DMA((7,)), pltpu.SemaphoreType.DMA((7,))],
    )(v)


def _adamw(name, g, w, m, v):
    R, C = g.shape
    tr = _tile(R, max(8, (1 << 18) // max(C, 1) // 8 * 8), 8)

    def body(g_ref, w_ref, m_ref, v_ref, d_ref, nm_ref, nv_ref):
        gv = g_ref[...]
        nm = ADAM_B1 * m_ref[...] + (1.0 - ADAM_B1) * gv
        nv = ADAM_B2 * v_ref[...] + (1.0 - ADAM_B2) * (gv * gv)
        m_hat = nm / (1.0 - ADAM_B1 ** ADAM_STEP)
        v_hat = nv / (1.0 - ADAM_B2 ** ADAM_STEP)
        d_ref[...] = -ADAM_LR * (m_hat / (jnp.sqrt(v_hat) + ADAM_EPS) + ADAM_WD * w_ref[...])
        nm_ref[...] = nm
        nv_ref[...] = nv

    blk = pl.BlockSpec((tr, C), lambda i: (i, 0))
    shp = jax.ShapeDtypeStruct((R, C), F32)
    return pl.pallas_call(
        body, name=name, grid=(R // tr,),
        in_specs=[blk] * 4, out_specs=[blk] * 3, out_shape=[shp] * 3,
        compiler_params=_cparams(("parallel",)),
    )(g, w, m, v)


def _adamw_nd(name, g, w, m, v):
    shape = w.shape
    flat = [a.reshape(-1, shape[-1]) for a in (g, w, m, v)]
    return tuple(o.reshape(shape) for o in _adamw(name, *flat))


def _scatter_heads(cfg, vals):
    v = jnp.pad(vals.reshape(cfg.hf // 2, 2), ((0, 0), (0, cfg.cs - 2)))
    return v.reshape(1, LANES)


def _gather_heads(cfg, row):
    return row.reshape(cfg.hf // 2, cfg.cs)[:, :2].reshape(cfg.hf)


def _permute_w_in(cfg, w_in_t):
    wf, hf = cfg.wf, cfg.hf
    o = 3 * wf
    cols = w_in_t.shape[1]
    fz = w_in_t[o:o + hf].reshape(hf // 2, 2, cols)
    fz_blk = jnp.pad(fz, ((0, 0), (0, cfg.cs - 2), (0, 0))).reshape(LANES, cols)
    return jnp.concatenate([w_in_t[:o], w_in_t[o + hf:], fz_blk], axis=0)


def _unpermute_dw_in(cfg, dwp):
    wf, hf = cfg.wf, cfg.hf
    o = 3 * wf
    cols = dwp.shape[1]
    fz = dwp[cfg.o_fz:].reshape(hf // 2, cfg.cs, cols)[:, :2].reshape(hf, cols)
    return jnp.concatenate([dwp[:o], fz, dwp[o:cfg.o_fz]], axis=0)


def _pair_gain(g):
    return jnp.tile(g, 2)[None]


def _fold_pair(dg):
    return dg[0, :HEAD_DIM] + dg[0, HEAD_DIM:]


def kernel(x, meta_tokens, ffn1_norm, ffn1_w_gate, ffn1_w_up, ffn1_w_down, mix_norm, w_in, b_forget, fox_q_norm, fox_k_norm, swa_q_norm, swa_k_norm, swa_sinks, fox_out_norm, swa_out_norm, w_out, ffn2_norm, ffn2_w_gate, ffn2_w_up, ffn2_w_down, loss_target, m_meta_tokens, m_ffn1_norm, m_ffn1_w_gate, m_ffn1_w_up, m_ffn1_w_down, m_mix_norm, m_w_in, m_b_forget, m_fox_q_norm, m_fox_k_norm, m_swa_q_norm, m_swa_k_norm, m_swa_sinks, m_fox_out_norm, m_swa_out_norm, m_w_out, m_ffn2_norm, m_ffn2_w_gate, m_ffn2_w_up, m_ffn2_w_down, v_meta_tokens, v_ffn1_norm, v_ffn1_w_gate, v_ffn1_w_up, v_ffn1_w_down, v_mix_norm, v_w_in, v_b_forget, v_fox_q_norm, v_fox_k_norm, v_swa_q_norm, v_swa_k_norm, v_swa_sinks, v_fox_out_norm, v_swa_out_norm, v_w_out, v_ffn2_norm, v_ffn2_w_gate, v_ffn2_w_up, v_ffn2_w_down):
    weights = dict(meta_tokens=meta_tokens, ffn1_norm=ffn1_norm, ffn1_w_gate=ffn1_w_gate, ffn1_w_up=ffn1_w_up,
                   ffn1_w_down=ffn1_w_down, mix_norm=mix_norm, w_in=w_in, b_forget=b_forget, fox_q_norm=fox_q_norm,
                   fox_k_norm=fox_k_norm, swa_q_norm=swa_q_norm, swa_k_norm=swa_k_norm, swa_sinks=swa_sinks,
                   fox_out_norm=fox_out_norm, swa_out_norm=swa_out_norm, w_out=w_out, ffn2_norm=ffn2_norm,
                   ffn2_w_gate=ffn2_w_gate, ffn2_w_up=ffn2_w_up, ffn2_w_down=ffn2_w_down)
    mom_m = dict(meta_tokens=m_meta_tokens, ffn1_norm=m_ffn1_norm, ffn1_w_gate=m_ffn1_w_gate, ffn1_w_up=m_ffn1_w_up,
                 ffn1_w_down=m_ffn1_w_down, mix_norm=m_mix_norm, w_in=m_w_in, b_forget=m_b_forget,
                 fox_q_norm=m_fox_q_norm, fox_k_norm=m_fox_k_norm, swa_q_norm=m_swa_q_norm, swa_k_norm=m_swa_k_norm,
                 swa_sinks=m_swa_sinks, fox_out_norm=m_fox_out_norm, swa_out_norm=m_swa_out_norm, w_out=m_w_out,
                 ffn2_norm=m_ffn2_norm, ffn2_w_gate=m_ffn2_w_gate, ffn2_w_up=m_ffn2_w_up, ffn2_w_down=m_ffn2_w_down)
    mom_v = dict(meta_tokens=v_meta_tokens, ffn1_norm=v_ffn1_norm, ffn1_w_gate=v_ffn1_w_gate, ffn1_w_up=v_ffn1_w_up,
                 ffn1_w_down=v_ffn1_w_down, mix_norm=v_mix_norm, w_in=v_w_in, b_forget=v_b_forget,
                 fox_q_norm=v_fox_q_norm, fox_k_norm=v_fox_k_norm, swa_q_norm=v_swa_q_norm, swa_k_norm=v_swa_k_norm,
                 swa_sinks=v_swa_sinks, fox_out_norm=v_fox_out_norm, swa_out_norm=v_swa_out_norm, w_out=v_w_out,
                 ffn2_norm=v_ffn2_norm, ffn2_w_gate=v_ffn2_w_gate, ffn2_w_up=v_ffn2_w_up, ffn2_w_down=v_ffn2_w_down)
    names = list(weights)

    _, S, D = x.shape
    depth = ffn1_norm.shape[0]
    hf, hs = b_forget.shape[1], swa_sinks.shape[1]
    U = w_in.shape[2] * N_DEV
    hkv = (U - 3 * HEAD_DIM * hf - hf - HEAD_DIM * hs) // (2 * HEAD_DIM)
    cfg = _Cfg(S, D, hf, hs, hkv)
    n_meta = meta_tokens.shape[0]
    assert n_meta == cfg.n_meta
    x_idx, y_idx, c_idx = _place()
    chip_idx = 2 * x_idx + y_idx
    dev_idx = 2 * chip_idx + c_idx
    core_s = jnp.reshape(c_idx, (1,)).astype(jnp.int32)
    chip_s = jnp.reshape(chip_idx, (1,)).astype(jnp.int32)

    col_sharded = ("ffn1_w_gate", "ffn1_w_up", "w_in", "ffn2_w_gate", "ffn2_w_up")
    use_order = ("ffn1_w_gate", "ffn1_w_up", "ffn1_w_down", "w_in", "w_out", "ffn2_w_gate", "ffn2_w_up", "ffn2_w_down")

    layer_axis = {k: (1 if k == "w_in" else 0) for k in use_order}

    def rows_view(k, a):
        if k not in col_sharded:
            return a
        return jnp.transpose(a, (2, 0, 1)) if k == "w_in" else jnp.swapaxes(a, 1, 2)

    def from_rows_view(k, a):
        if k not in col_sharded:
            return a
        return jnp.transpose(a, (1, 2, 0)) if k == "w_in" else jnp.swapaxes(a, 1, 2)

    w_rows = {k: rows_view(k, weights[k]) for k in use_order}

    def shard(key):
        k, l = key
        if k == "meta_tokens":
            return meta_tokens
        return _shard_bf16("weight_shard", w_rows[k], l, layer_axis[k] if k in col_sharded else 0)

    waiting = [("meta_tokens", 0)] + [(k, l) for l in range(depth) for k in use_order]
    halfway = []
    gathered = {}

    def fwd_carry(n_first):
        cy = _Carry()
        second = [(key, _gather_second(cy, buf)) for key, buf in halfway]
        first = [(key, _gather_first(cy, shard(key))) for key in waiting[:n_first]]
        del waiting[:n_first]
        halfway.clear()
        return cy, (first, second)

    def fwd_absorb(extra, plan):
        first, second = plan
        for key, idx in second:
            gathered[key] = extra[idx]
        for key, idx in first:
            halfway.append((key, extra[idx]))

    def weight(k, l):
        key = (k, l)
        while key not in gathered:
            n = 0 if any(key == hk for hk, _ in halfway) else waiting.index(key) + 1
            cy, plan = fwd_carry(n)
            fwd_absorb(_comm_only("weights_gather", cy), plan)
        g = gathered[key]
        return g.reshape(-1, g.shape[-1])

    weight("ffn1_w_down", 0)
    meta_full = jnp.swapaxes(weight("meta_tokens", 0).reshape(N_DEV, n_meta, -1), 0, 1).reshape(n_meta, D)
    slopes = jnp.asarray(2.0 ** (-8.0 * np.arange(1, hs + 1) / hs), dtype=F32)

    h = jnp.concatenate([jnp.zeros((cfg.pad, D), F32), meta_full, x[0]], axis=0)
    saved = []
    w_in_p = [None] * depth

    def mm_f(name, a, b, n_first=1, **kw):
        cy, plan = fwd_carry(n_first)
        out, extra = _matmul(name, a, b, carry=cy, **kw)
        fwd_absorb(extra, plan)
        return out

    def ffn_fwd(tag, l, h_in, norm, wg, wu, wd):
        xn = _rmsnorm_fwd(f"{tag}_norm", h_in, norm[l][None])
        wg_t, wu_t = weight(wg, l), weight(wu, l)
        cy, plan = fwd_carry(1)
        (gate, up, act), extra = _ffn_up(f"{tag}_up", xn, wg_t, wu_t, carry=cy)
        fwd_absorb(extra, plan)
        h_out = mm_f(f"{tag}_down", act, weight(wd, l), scale=0.5, residual=h_in, tm=544, tn=1024, tk=2816)
        return h_out, (xn, gate, up)

    for l in range(depth):
        st = {"h0": h}
        h, st["ffn1"] = ffn_fwd("ffn1", l, h, ffn1_norm, "ffn1_w_gate", "ffn1_w_up", "ffn1_w_down")
        st["h1"] = h
        xn = _rmsnorm_fwd("mix_norm", h, mix_norm[l][None])
        w_in_p[l] = _permute_w_in(cfg, weight("w_in", l))
        u = mm_f("mix_in", xn, w_in_p[l], trans_b=True, tm=544, tn=640, tk=2048)
        gq, gk = _pair_gain(fox_q_norm[l]), _pair_gain(fox_k_norm[l])
        gsq, gsk = _pair_gain(swa_q_norm[l]), _pair_gain(swa_k_norm[l])
        bias = _scatter_heads(cfg, b_forget[l])
        qn, kn, fv, sqn, skd, svd, c, ct = _mix_prep("mix_prep", cfg, u, gq, gk, gsq, gsk, bias)
        cy, plan = fwd_carry(2)
        (o_fox, lse), extra = _fox_fwd("fox_fwd", cfg, qn, kn, fv, c, ct, carry=cy)
        fwd_absorb(extra, plan)
        cy, plan = fwd_carry(1)
        (o_swa,), extra = _swa_fwd("swa_fwd", cfg, sqn, skd, svd, swa_sinks[l], slopes, carry=cy)
        fwd_absorb(extra, plan)
        o_cat = _out_norm("out_norm", cfg, o_fox, o_swa, fox_out_norm[l][None], swa_out_norm[l][None])
        st["mix"] = (xn, u, gq, gk, gsq, gsk, bias, qn, kn, fv, sqn, skd, svd, c, ct, o_fox, lse, o_swa, o_cat)
        h = mm_f("mix_out", o_cat, weight("w_out", l), n_first=0, residual=h, tm=544, tn=1024, tk=2048)
        st["h2"] = h
        h, st["ffn2"] = ffn_fwd("ffn2", l, h, ffn2_norm, "ffn2_w_gate", "ffn2_w_up", "ffn2_w_down")
        saved.append(st)

    loss_blk, dh, dh16 = _loss_head("loss_head", h, loss_target[0])

    small = {k: [None] * depth for k in names if k not in use_order and k != "meta_tokens"}
    parts, landing = {}, {}
    to_sibling, to_chips = [], []

    def bwd_carry(n_chips):
        cy = _Carry()
        t1, t3 = [], []
        for k, l in list(to_chips):
            if len(t3) < n_chips and all(k != k3 for k3, _ in t3):
                to_chips.remove((k, l))
                t3.append((k, _scatter_chips(cy, parts[k], landing[k], l)))
        while to_sibling:
            k, l, g = to_sibling.pop(0)
            t1.append((k, l, g, _scatter_sibling(cy, g)))
        return cy, (t1, t3)

    def bwd_absorb(extra, plan):
        t1, t3 = plan
        for k, idx in t3:
            landing[k] = extra[idx]
        for k, l, g, idx in t1:
            parts[k] = _pair_add("grads_pair_add", g, extra[idx], parts[k], l, core_s)
            to_chips.append((k, l))

    def emit_grad(k, l, dw):
        r, C = dw.shape[0] // N_DEV, dw.shape[1]
        if k not in parts:
            parts[k] = lax.empty((depth, 4, r, C), BF16)
            landing[k] = lax.empty((3, depth, r, C), BF16)
        to_sibling.append((k, l, dw.reshape(4, 2, r, C)))

    def mm_b(name, a, b, n_chips=0, **kw):
        cy, plan = bwd_carry(n_chips)
        out, extra = _matmul(name, a, b, carry=cy, **kw)
        bwd_absorb(extra, plan)
        return out

    def ffn_bwd(tag, l, dh_out, dh_out16, h_in, st_, norm, wg, wu, wd):
        xn, gate, up = st_
        short = 1 if l == 0 else 0
        cy, plan = bwd_carry(1)
        (dgate, dup, act), extra = _ffn_bwd_act(f"{tag}_dact", dh_out16, weight(wd, l), gate, up, carry=cy)
        bwd_absorb(extra, plan)
        emit_grad(wd, l, mm_b(f"{tag}_dwd", act, dh_out16, short, trans_a=True, scale=0.5, out_dtype=BF16,
                              tm=512, tn=1024, tk=2176))
        emit_grad(wg, l, mm_b(f"{tag}_dwg", dgate, xn, short, trans_a=True, out_dtype=BF16, tm=512, tn=1024, tk=2176))
        emit_grad(wu, l, mm_b(f"{tag}_dwu", dup, xn, short, trans_a=True, out_dtype=BF16, tm=512, tn=1024, tk=2176))
        dxn = mm_b(f"{tag}_dxn", dgate, weight(wg, l), 1, pair2=(dup, weight(wu, l)), tm=544, tn=1024, tk=1408)
        dh_in, dh_in16, dg = _rmsnorm_bwd(f"{tag}_dnorm", dxn, h_in, norm[l][None], dh_out)
        return dh_in, dh_in16, dg[0]

    for l in reversed(range(depth)):
        st = saved[l]
        dh, dh16, small["ffn2_norm"][l] = ffn_bwd("ffn2", l, dh, dh16, st["h2"], st["ffn2"], ffn2_norm,
                                                   "ffn2_w_gate", "ffn2_w_up", "ffn2_w_down")
        xn, u, gq, gk, gsq, gsk, bias, qn, kn, fv, sqn, skd, svd, c, ct, o_fox, lse, o_swa, o_cat = st["mix"]
        dcat = mm_b("mix_dcat", dh16, weight("w_out", l), int(l == 0), trans_b=True, tm=544, tn=1024, tk=2048)
        emit_grad("w_out", l, mm_b("mix_dwout", o_cat, dh16, int(l == 0), trans_a=True, out_dtype=BF16, tm=512, tn=1024, tk=2176))
        do_fox, do_swa, dgf, dgs = _out_norm_bwd("out_norm_bwd", cfg, dcat, o_fox, o_swa,
                                                 fox_out_norm[l][None], swa_out_norm[l][None])
        small["fox_out_norm"][l], small["swa_out_norm"][l] = dgf[0], dgs[0]
        cy, plan = bwd_carry(3)
        (dqn, dkn, dfv, dct, dcq), extra = _fox_bwd("fox_bwd", cfg, qn, kn, fv, c, ct, o_fox, lse, do_fox, carry=cy)
        bwd_absorb(extra, plan)
        cy, plan = bwd_carry(1)
        (dsqn, dskd, dsvd, dsink), extra = _swa_bwd("swa_bwd", cfg, sqn, skd, svd, swa_sinks[l], slopes, o_swa, do_swa,
                                                    carry=cy)
        bwd_absorb(extra, plan)
        small["swa_sinks"][l] = dsink.reshape(hs // 2, 8, LANES)[:, 0, ::HEAD_DIM].reshape(hs)
        du, dgq, dgk, dgsq, dgsk, db = _mix_prep_bwd("mix_prep_bwd", cfg, u, gq, gk, gsq, gsk, bias,
                                                     dqn, dkn, dfv, dsqn, dskd, dsvd, dct, dcq)
        small["fox_q_norm"][l], small["fox_k_norm"][l] = _fold_pair(dgq), _fold_pair(dgk)
        small["swa_q_norm"][l], small["swa_k_norm"][l] = _fold_pair(dgsq), _fold_pair(dgsk)
        small["b_forget"][l] = _gather_heads(cfg, db[0])
        dwp = mm_b("mix_dwin", du, xn, int(l == 0), trans_a=True, out_dtype=BF16, tm=640, tn=1024, tk=2176)
        emit_grad("w_in", l, _unpermute_dw_in(cfg, dwp))
        dxn = mm_b("mix_dxn", du, w_in_p[l], int(l == 0), tm=544, tn=1024, tk=4480)
        dh, dh16, dg = _rmsnorm_bwd("mix_dnorm", dxn, st["h1"], mix_norm[l][None], dh)
        small["mix_norm"][l] = dg[0]
        dh, dh16, small["ffn1_norm"][l] = ffn_bwd("ffn1", l, dh, dh16, st["h0"], st["ffn1"], ffn1_norm,
                                                   "ffn1_w_gate", "ffn1_w_up", "ffn1_w_down")

    grad_x = dh[BLOCK:][None]
    dmeta = dh[cfg.pad:BLOCK]

    while to_sibling or to_chips:
        cy, plan = bwd_carry(len(use_order))
        bwd_absorb(_comm_only("grads_scatter", cy), plan)

    grads = {}
    for k in use_order:
        grads[k] = _chip_sum("grads_chip_sum", parts[k], landing[k], chip_s, layer_axis[k] if k in col_sharded else 0)

    small_names = list(small)
    pieces = [loss_blk[0, :1], dmeta.reshape(-1)] + [jnp.stack(small[k]).reshape(-1) for k in small_names]
    sizes = [int(p.shape[0]) for p in pieces]
    total = sum(sizes)
    padded = -(-total // (8 * LANES)) * (8 * LANES)
    vec = jnp.concatenate(pieces + [jnp.zeros((padded - total,), F32)]).reshape(-1, LANES)
    summed = _gather_sum("small_gather_sum", vec).reshape(-1)
    offs = np.cumsum([0] + sizes)
    loss = summed[0]
    dmeta_full = summed[offs[1]:offs[2]].reshape(n_meta, D)
    mcols = meta_tokens.shape[1]
    grads["meta_tokens"] = lax.dynamic_slice_in_dim(dmeta_full, dev_idx * mcols, mcols, axis=1)
    for n_, k in enumerate(small_names):
        grads[k] = summed[offs[2 + n_]:offs[3 + n_]].reshape(weights[k].shape)

    delta, new_m, new_v = {}, {}, {}
    for k in names:
        if k in col_sharded:
            outs = _adamw_nd("adamw", grads[k], w_rows[k], rows_view(k, mom_m[k]), rows_view(k, mom_v[k]))
            delta[k], new_m[k], new_v[k] = (from_rows_view(k, o) for o in outs)
            grads[k] = from_rows_view(k, grads[k])
        else:
            delta[k], new_m[k], new_v[k] = _adamw_nd("adamw", grads[k], weights[k], mom_m[k], mom_v[k])

    return (loss, grad_x, *[grads[k] for k in names], *[delta[k] for k in names],
            *[new_m[k] for k in names], *[new_v[k] for k in names])
```

```python
import functools

import numpy as np
import jax
import jax.numpy as jnp
from jax import lax
from jax.experimental import pallas as pl
from jax.experimental.pallas import tpu as pltpu

F32 = jnp.float32
BF16 = jnp.bfloat16
MESH = pl.DeviceIdType.MESH

HEAD_DIM = 64
BLOCK = 128
LANES = 128
N_DEV = 8
EPS = 1e-6
NEG_INF = -1e30
SCALE = HEAD_DIM ** -0.5

ADAM_LR = 0.001
ADAM_B1 = 0.9
ADAM_B2 = 0.999
ADAM_EPS = 1e-08
ADAM_WD = 0.01
ADAM_STEP = 10

VMEM_BYTES_V7X = 64 * 1024 * 1024
VMEM_LIMIT = VMEM_BYTES_V7X * 3 // 4

NT = (((1,), (1,)), ((), ()))
TN = (((0,), (0,)), ((), ()))
HI = lax.Precision.HIGHEST


def _cparams(sem=None, vmem=VMEM_LIMIT):
    return pltpu.CompilerParams(dimension_semantics=sem, vmem_limit_bytes=vmem)


def _tile(n, pref, mult):
    best = None
    for t in range(mult, min(n, pref) + 1, mult):
        if n % t == 0:
            best = t
    return best if best is not None else n


def _dot(a, b):
    return jnp.dot(a, b, preferred_element_type=F32)


def _dot_nt(a, b):
    return lax.dot_general(a, b, NT, preferred_element_type=F32)


def _dot_tn(a, b):
    return lax.dot_general(a, b, TN, preferred_element_type=F32)


def _lane(shape):
    return lax.broadcasted_iota(jnp.int32, shape, len(shape) - 1)


def _half_sum(x, lo):
    s0 = jnp.sum(jnp.where(lo, x, 0.0), axis=1, keepdims=True)
    s1 = jnp.sum(jnp.where(lo, 0.0, x), axis=1, keepdims=True)
    return jnp.where(lo, s0, s1)


def _sigmoid(x):
    return 1.0 / (1.0 + jnp.exp(-x))


def _place():
    return lax.axis_index("x"), lax.axis_index("y"), lax.axis_index("c")


def _peers(x, y, c):
    return [(x, y, 1 - c), (1 - x, y, c), (x, 1 - y, c), (1 - x, 1 - y, c)]


def _dev_index(dev):
    px, py, pc = dev
    return 4 * px + 2 * py + pc


class _Carry:
    def __init__(self):
        self.ins, self.outs, self.alias, self.items = [], [], {}, []
        self.nsem = self.nloc = 0

    def add(self, ins, outs, alias, nsem, nloc, build):
        i0, o0 = len(self.ins), len(self.outs)
        for src, dst in alias.items():
            self.alias[i0 + src] = o0 + dst
        self.items.append((i0, len(ins), o0, len(outs), self.nsem, self.nloc, build))
        self.ins += ins
        self.outs += outs
        self.nsem += nsem
        self.nloc += nloc
        return list(range(o0, o0 + len(outs)))

    def build(self, in_refs, out_refs, ssem, rsem, lsem):
        ops = []
        for i0, ni, o0, no, s0, l0, fn in self.items:
            ops.append(fn(in_refs[i0:i0 + ni], out_refs[o0:o0 + no],
                          lambda k, s0=s0: (ssem.at[s0 + k], rsem.at[s0 + k]), lambda k, l0=l0: lsem.at[l0 + k]))
        return ops


def _remote(src, dst, sems, dev):
    return pltpu.make_async_remote_copy(src_ref=src, dst_ref=dst, send_sem=sems[0], recv_sem=sems[1],
                                        device_id=dev, device_id_type=MESH)


def _gather_first(carry, shard):
    def build(ins, outs, sems, locs):
        src, buf = ins[0], outs[0]
        x, y, c = _place()
        me = _dev_index((x, y, c))
        peers = _peers(x, y, c)
        local = pltpu.make_async_copy(src, buf.at[me], locs(0))
        sends = [_remote(src, buf.at[me], sems(k), dev) for k, dev in enumerate(peers)]
        recvs = [_remote(src, buf.at[_dev_index(dev)], sems(k), dev) for k, dev in enumerate(peers)]

        def start():
            local.start()
            for cp in sends:
                cp.start()

        def wait():
            for cp in sends:
                cp.wait_send()
            for cp in recvs:
                cp.wait_recv()
            local.wait()

        return start, wait

    return carry.add([shard], [jax.ShapeDtypeStruct((N_DEV,) + shard.shape, shard.dtype)], {}, 4, 1, build)[0]


def _gather_second(carry, buf):
    def build(ins, outs, sems, locs):
        b = outs[0]
        x, y, c = _place()
        chips = _peers(x, y, c)[1:]
        sends = [_remote(b.at[_dev_index(dev)], b.at[_dev_index(dev)], sems(k), (x, y, 1 - c))
                 for k, dev in enumerate(chips)]
        recvs = [_remote(b.at[_dev_index(dev)], b.at[_dev_index((dev[0], dev[1], 1 - c))], sems(k), (x, y, 1 - c))
                 for k, dev in enumerate(chips)]

        def start():
            for cp in sends:
                cp.start()

        def wait():
            for cp in sends:
                cp.wait_send()
            for cp in recvs:
                cp.wait_recv()

        return start, wait

    return carry.add([buf], [jax.ShapeDtypeStruct(buf.shape, buf.dtype)], {0: 0}, 3, 0, build)[0]


def _scatter_sibling(carry, grad):
    def build(ins, outs, sems, locs):
        x, y, c = _place()
        cp = _remote(ins[0].at[:, 1 - c], outs[0], sems(0), (x, y, 1 - c))
        return cp.start, cp.wait

    shape = (grad.shape[0],) + grad.shape[2:]
    return carry.add([grad], [jax.ShapeDtypeStruct(shape, grad.dtype)], {}, 1, 0, build)[0]


def _scatter_chips(carry, parts, landing, layer):
    def build(ins, outs, sems, locs):
        x, y, c = _place()
        cps = [_remote(ins[0].at[layer, 2 * dev[0] + dev[1]], outs[0].at[k, layer], sems(k), dev)
               for k, dev in enumerate(_peers(x, y, c)[1:])]

        def start():
            for cp in cps:
                cp.start()

        def wait():
            for cp in cps:
                cp.wait()

        return start, wait

    return carry.add([parts, landing], [jax.ShapeDtypeStruct(landing.shape, landing.dtype)], {1: 0}, 3, 0, build)[0]


def _call(name, body, grid, in_specs, out_specs, out_shape, args, scratch=(), carry=None):
    ni, no, ns = len(args), len(out_shape), len(scratch)
    if carry is None or not carry.items:
        res = pl.pallas_call(
            body, name=name, grid=grid, in_specs=list(in_specs), out_specs=list(out_specs), out_shape=list(out_shape),
            scratch_shapes=list(scratch), compiler_params=_cparams(("arbitrary",) * len(grid)))(*args)
        return list(res), []
    nci, nco = len(carry.ins), len(carry.outs)

    def full_body(*refs):
        c_in = refs[ni:ni + nci]
        c_out = refs[ni + nci + no:ni + nci + no + nco]
        sc = refs[ni + nci + no + nco:]
        ops = carry.build(c_in, c_out, sc[ns], sc[ns + 1], sc[ns + 2])
        first = last = None
        for d, n in enumerate(grid):
            pid = pl.program_id(d)
            first = (pid == 0) if first is None else first & (pid == 0)
            last = (pid == n - 1) if last is None else last & (pid == n - 1)

        @pl.when(first)
        def _():
            for start, _w in ops:
                start()

        body(*refs[:ni], *refs[ni + nci:ni + nci + no], *sc[:ns])

        @pl.when(last)
        def _():
            for _s, wait in ops:
                wait()

    hbm = pl.BlockSpec(memory_space=pl.ANY)
    res = pl.pallas_call(
        full_body, name=name, grid=grid,
        in_specs=list(in_specs) + [hbm] * nci, out_specs=list(out_specs) + [hbm] * nco,
        out_shape=list(out_shape) + list(carry.outs),
        input_output_aliases={ni + s: no + d for s, d in carry.alias.items()},
        scratch_shapes=list(scratch) + [pltpu.SemaphoreType.DMA((carry.nsem,)), pltpu.SemaphoreType.DMA((carry.nsem,)),
                                        pltpu.SemaphoreType.DMA((max(carry.nloc, 1),))],
        compiler_params=_cparams(("arbitrary",) * len(grid)))(*args, *carry.ins)
    return list(res[:no]), list(res[no:])


def _comm_only(name, carry):
    return _call(name, lambda *refs: None, (1,), [], [], [], [], carry=carry)[1]


def _matmul(name, a, b, *, pair2=None, trans_a=False, trans_b=False, out_dtype=F32, scale=None, residual=None,
            tm=512, tn=512, tk=512, carry=None):
    if trans_a:
        K, M = a.shape
    else:
        M, K = a.shape
    if trans_b:
        N, Kb = b.shape
    else:
        Kb, N = b.shape
    assert K == Kb, (name, a.shape, b.shape)
    tm = _tile(M, tm, LANES if trans_a else 16)
    tn = _tile(N, tn, LANES)
    tk = _tile(K, tk, 16 if (trans_a and not trans_b) else LANES)
    nk = K // tk
    dims = (((0 if trans_a else 1,), (1 if trans_b else 0,)), ((), ()))
    pairs = [(a, b)] + ([pair2] if pair2 is not None else [])
    npair = len(pairs)

    def body(*refs):
        ab = refs[:2 * npair]
        pos = 2 * npair
        r_ref = None
        if residual is not None:
            r_ref = refs[pos]
            pos += 1
        o_ref = refs[pos]

        def partial():
            t = None
            for q in range(npair):
                d = lax.dot_general(ab[2 * q][...].astype(BF16), ab[2 * q + 1][...].astype(BF16), dims,
                                    preferred_element_type=F32)
                t = d if t is None else t + d
            return t

        def finish(r):
            if scale is not None:
                r = r * scale
            if r_ref is not None:
                r = r + r_ref[...].astype(F32)
            o_ref[...] = r.astype(o_ref.dtype)

        if nk == 1:
            finish(partial())
            return
        acc_ref = refs[pos + 1]
        k = pl.program_id(2)

        @pl.when(k == 0)
        def _():
            acc_ref[...] = partial()

        @pl.when(k > 0)
        def _():
            acc_ref[...] += partial()

        @pl.when(k == nk - 1)
        def _():
            finish(acc_ref[...])

    a_spec = pl.BlockSpec((tk, tm), lambda i, j, k: (k, i)) if trans_a else pl.BlockSpec((tm, tk), lambda i, j, k: (i, k))
    b_spec = pl.BlockSpec((tn, tk), lambda i, j, k: (j, k)) if trans_b else pl.BlockSpec((tk, tn), lambda i, j, k: (k, j))
    in_specs, args = [], []
    for pa, pb in pairs:
        in_specs += [a_spec, b_spec]
        args += [pa, pb]
    if residual is not None:
        in_specs.append(pl.BlockSpec((tm, tn), lambda i, j, k: (i, j)))
        args.append(residual)
    res, extra = _call(
        name, body, (M // tm, N // tn, nk), in_specs, [pl.BlockSpec((tm, tn), lambda i, j, k: (i, j))],
        [jax.ShapeDtypeStruct((M, N), out_dtype)], args,
        scratch=[pltpu.VMEM((tm, tn), F32)] if nk > 1 else [], carry=carry)
    return res[0], extra


def _shard_bf16(name, w, layer, layer_axis):
    R, C = w.shape[1 - layer_axis], w.shape[2]
    tr = _tile(R, 512, 16)

    def body(x_ref, o_ref):
        o_ref[...] = x_ref[...].astype(BF16)

    if layer_axis == 0:
        in_spec = pl.BlockSpec((None, tr, C), lambda i: (layer, i, 0))
    else:
        in_spec = pl.BlockSpec((tr, C), lambda i: (i, layer))
        w = w.reshape(R, -1)
    return pl.pallas_call(
        body, name=name, grid=(R // tr,),
        in_specs=[in_spec], out_specs=pl.BlockSpec((tr, C), lambda i: (i, 0)),
        out_shape=jax.ShapeDtypeStruct((R, C), BF16),
        compiler_params=_cparams(("parallel",)),
    )(w)


def _rmsnorm_fwd(name, h, g):
    L, D = h.shape
    tm = _tile(L, 256, 16)

    def body(h_ref, g_ref, o_ref):
        x = h_ref[...]
        r = lax.rsqrt(jnp.mean(x * x, axis=1, keepdims=True) + EPS)
        o_ref[...] = (x * r * g_ref[...]).astype(o_ref.dtype)

    return pl.pallas_call(
        body, name=name, grid=(L // tm,),
        in_specs=[pl.BlockSpec((tm, D), lambda i: (i, 0)), pl.BlockSpec((1, D), lambda i: (0, 0))],
        out_specs=pl.BlockSpec((tm, D), lambda i: (i, 0)),
        out_shape=jax.ShapeDtypeStruct((L, D), BF16),
        compiler_params=_cparams(("parallel",)),
    )(h, g)


def _rmsnorm_bwd(name, dy, h, g, dres):
    L, D = h.shape
    tm = _tile(L, 256, 16)

    def body(dy_ref, h_ref, g_ref, dres_ref, dh_ref, dh16_ref, dg_ref):
        i = pl.program_id(0)
        x = h_ref[...]
        dyv = dy_ref[...].astype(F32)
        r = lax.rsqrt(jnp.mean(x * x, axis=1, keepdims=True) + EPS)
        w = dyv * g_ref[...]
        proj = jnp.sum(w * x, axis=1, keepdims=True) * (1.0 / D)
        dh = dres_ref[...] + r * w - x * (r * r * r * proj)
        dh_ref[...] = dh
        dh16_ref[...] = dh.astype(BF16)

        @pl.when(i == 0)
        def _():
            dg_ref[...] = jnp.zeros_like(dg_ref)

        dg_ref[...] += jnp.sum(dyv * x * r, axis=0, keepdims=True)

    return pl.pallas_call(
        body, name=name, grid=(L // tm,),
        in_specs=[pl.BlockSpec((tm, D), lambda i: (i, 0)), pl.BlockSpec((tm, D), lambda i: (i, 0)),
                  pl.BlockSpec((1, D), lambda i: (0, 0)), pl.BlockSpec((tm, D), lambda i: (i, 0))],
        out_specs=[pl.BlockSpec((tm, D), lambda i: (i, 0)), pl.BlockSpec((tm, D), lambda i: (i, 0)),
                   pl.BlockSpec((1, D), lambda i: (0, 0))],
        out_shape=[jax.ShapeDtypeStruct((L, D), F32), jax.ShapeDtypeStruct((L, D), BF16),
                   jax.ShapeDtypeStruct((1, D), F32)],
        compiler_params=_cparams(("arbitrary",)),
    )(dy, h, g, dres)


def _ffn_up(name, xn, wgT, wuT, carry=None):
    L, D = xn.shape
    F = wgT.shape[0]
    tm = _tile(L, 544, 16)
    tn = _tile(F, 512, LANES)

    def body(x_ref, wg_ref, wu_ref, g_ref, u_ref, a_ref):
        x = x_ref[...]
        g = _dot_nt(x, wg_ref[...])
        u = _dot_nt(x, wu_ref[...])
        g_ref[...] = g.astype(BF16)
        u_ref[...] = u.astype(BF16)
        a_ref[...] = (g * _sigmoid(g) * u).astype(BF16)

    o_spec = pl.BlockSpec((tm, tn), lambda i, j: (i, j))
    o_shape = jax.ShapeDtypeStruct((L, F), BF16)
    return _call(
        name, body, (L // tm, F // tn),
        [pl.BlockSpec((tm, D), lambda i, j: (i, 0)), pl.BlockSpec((tn, D), lambda i, j: (j, 0)),
         pl.BlockSpec((tn, D), lambda i, j: (j, 0))],
        [o_spec, o_spec, o_spec], [o_shape, o_shape, o_shape], [xn, wgT, wuT], carry=carry)


def _ffn_bwd_act(name, dh, wd, gate, up, carry=None):
    L, D = dh.shape
    F = wd.shape[0]
    tm = _tile(L, 544, 16)
    tn = _tile(F, 512, LANES)

    def body(dh_ref, wd_ref, g_ref, u_ref, dg_ref, du_ref, a_ref):
        da = 0.5 * _dot_nt(dh_ref[...].astype(BF16), wd_ref[...])
        g = g_ref[...].astype(F32)
        u = u_ref[...].astype(F32)
        sg = _sigmoid(g)
        silu = g * sg
        dg_ref[...] = (da * u * (sg * (1.0 + g * (1.0 - sg)))).astype(BF16)
        du_ref[...] = (da * silu).astype(BF16)
        a_ref[...] = (silu * u).astype(BF16)

    o_spec = pl.BlockSpec((tm, tn), lambda i, j: (i, j))
    o_shape = jax.ShapeDtypeStruct((L, F), BF16)
    return _call(
        name, body, (L // tm, F // tn),
        [pl.BlockSpec((tm, D), lambda i, j: (i, 0)), pl.BlockSpec((tn, D), lambda i, j: (j, 0)), o_spec, o_spec],
        [o_spec, o_spec, o_spec], [o_shape, o_shape, o_shape], [dh, wd, gate, up], carry=carry)


class _Cfg:
    def __init__(self, S, D, hf, hs, hkv):
        self.S, self.D, self.L = S, D, S + BLOCK
        self.hf, self.hs, self.hkv = hf, hs, hkv
        self.wf, self.ws = hf * HEAD_DIM, hs * HEAD_DIM
        self.group = hs // hkv
        self.cs = 2 * LANES // hf
        self.n_meta = 16
        self.pad = BLOCK - self.n_meta
        self.o_fk = self.wf
        self.o_fv = 2 * self.wf
        self.o_sq = 3 * self.wf
        self.o_sk = self.o_sq + self.ws
        self.o_sv = self.o_sk + LANES
        self.o_fz = self.o_sv + LANES
        self.up = self.o_fz + LANES
        assert hkv == 2 and hf % 2 == 0 and self.group % 2 == 0 and self.cs % 8 == 0
        assert self.o_sq % self.ws == 0 and (2 * self.wf) % LANES == 0


def _head_norm(x, gain, lo, mult):
    r = lax.rsqrt(_half_sum(x * x, lo) * (1.0 / HEAD_DIM) + EPS)
    return x * r * (gain * mult)


def _head_norm_bwd(dy, x, gain, lo, mult):
    r = lax.rsqrt(_half_sum(x * x, lo) * (1.0 / HEAD_DIM) + EPS)
    w = dy * (gain * mult)
    proj = _half_sum(w * x, lo) * (1.0 / HEAD_DIM)
    dx = r * w - x * (r * r * r * proj)
    dgain = jnp.sum(dy * mult * x * r, axis=0, keepdims=True)
    return dx, dgain


def _dup(x, lo):
    xr = pltpu.roll(x, 64, 1)
    return jnp.where(lo, x, xr), jnp.where(lo, xr, x)


def _mix_prep(name, cfg, u, gq, gk, gsq, gsk, bias):
    L, wf, ws = cfg.L, cfg.wf, cfg.ws
    T = BLOCK
    npf, nps = wf // LANES, ws // LANES

    def body(fqk_ref, fv_ref, sq_ref, sk_ref, sv_ref, fz_ref, gq_ref, gk_ref, gsq_ref, gsk_ref, b_ref,
             qn_ref, kn_ref, fvo_ref, sqn_ref, skd_ref, svd_ref, c_ref, ct_ref, carry_ref):
        i = pl.program_id(0)
        lo = _lane((T, LANES)) < HEAD_DIM
        for p in range(npf):
            sl = slice(p * LANES, (p + 1) * LANES)
            qn_ref[:, sl] = _head_norm(fqk_ref[:, sl], gq_ref[...], lo, SCALE).astype(BF16)
            kn_ref[:, sl] = _head_norm(fqk_ref[:, wf + p * LANES: wf + (p + 1) * LANES], gk_ref[...], lo, 1.0).astype(BF16)
        fvo_ref[...] = fv_ref[...].astype(BF16)
        for p in range(nps):
            sl = slice(p * LANES, (p + 1) * LANES)
            sqn_ref[:, sl] = _head_norm(sq_ref[:, sl], gsq_ref[...], lo, SCALE).astype(BF16)
        k0, k1 = _dup(_head_norm(sk_ref[...], gsk_ref[...], lo, 1.0), lo)
        skd_ref[:, :LANES] = k0.astype(BF16)
        skd_ref[:, LANES:] = k1.astype(BF16)
        v0, v1 = _dup(sv_ref[...], lo)
        svd_ref[:, :LANES] = v0.astype(BF16)
        svd_ref[:, LANES:] = v1.astype(BF16)

        @pl.when(i == 0)
        def _():
            carry_ref[...] = jnp.zeros_like(carry_ref)

        z = fz_ref[...] + b_ref[...]
        lf = jnp.minimum(z, 0.0) - jnp.log(1.0 + jnp.exp(-jnp.abs(z)))
        row = lax.broadcasted_iota(jnp.int32, (T, T), 0)
        col = lax.broadcasted_iota(jnp.int32, (T, T), 1)
        tri = jnp.where(col <= row, 1.0, 0.0).astype(F32)
        c = jnp.dot(tri, lf, precision=HI, preferred_element_type=F32) + carry_ref[0:1, :]
        c_ref[...] = c
        ct_ref[...] = c.T
        carry_ref[0:1, :] = c_ref[T - 1:T, :]

    def rows(w, cb):
        return pl.BlockSpec((T, w), lambda i, cb=cb: (i, cb))

    vec = pl.BlockSpec((1, LANES), lambda i: (0, 0))
    return pl.pallas_call(
        body, name=name, grid=(L // T,),
        in_specs=[rows(2 * wf, 0), rows(wf, 2), rows(ws, cfg.o_sq // ws), rows(LANES, cfg.o_sk // LANES),
                  rows(LANES, cfg.o_sv // LANES), rows(LANES, cfg.o_fz // LANES), vec, vec, vec, vec, vec],
        out_specs=[rows(wf, 0), rows(wf, 0), rows(wf, 0), rows(ws, 0), rows(2 * LANES, 0), rows(2 * LANES, 0),
                   rows(LANES, 0), pl.BlockSpec((LANES, T), lambda i: (0, i))],
        out_shape=[jax.ShapeDtypeStruct((L, wf), BF16)] * 3 + [jax.ShapeDtypeStruct((L, ws), BF16)]
        + [jax.ShapeDtypeStruct((L, 2 * LANES), BF16)] * 2
        + [jax.ShapeDtypeStruct((L, LANES), F32), jax.ShapeDtypeStruct((LANES, L), F32)],
        scratch_shapes=[pltpu.VMEM((8, LANES), F32)],
        compiler_params=_cparams(("arbitrary",)),
    )(u, u, u, u, u, u, gq, gk, gsq, gsk, bias)


def _mix_prep_bwd(name, cfg, u, gq, gk, gsq, gsk, bias, dqn, dkn, dfv, dsqn, dskd, dsvd, dct, dcq):
    L, wf, ws = cfg.L, cfg.wf, cfg.ws
    T = BLOCK
    nb = L // T
    npf, nps = wf // LANES, ws // LANES

    def body(fqk_ref, sq_ref, sk_ref, fz_ref, gq_ref, gk_ref, gsq_ref, gsk_ref, b_ref,
             dqn_ref, dkn_ref, dfv_ref, dsqn_ref, dskd_ref, dsvd_ref, dct_ref, dcq_ref,
             du_ref, dgq_ref, dgk_ref, dgsq_ref, dgsk_ref, db_ref, carry_ref):
        i = pl.program_id(0)
        lo = _lane((T, LANES)) < HEAD_DIM

        @pl.when(i == 0)
        def _():
            carry_ref[...] = jnp.zeros_like(carry_ref)
            for r in (dgq_ref, dgk_ref, dgsq_ref, dgsk_ref, db_ref):
                r[...] = jnp.zeros_like(r)

        accq = jnp.zeros((1, LANES), F32)
        acck = jnp.zeros((1, LANES), F32)
        for p in range(npf):
            sl = slice(p * LANES, (p + 1) * LANES)
            dx, dg = _head_norm_bwd(dqn_ref[:, sl], fqk_ref[:, sl], gq_ref[...], lo, SCALE)
            du_ref[:, sl] = dx.astype(BF16)
            accq = accq + dg
            slk = slice(wf + p * LANES, wf + (p + 1) * LANES)
            dx, dg = _head_norm_bwd(dkn_ref[:, sl], fqk_ref[:, slk], gk_ref[...], lo, 1.0)
            du_ref[:, slk] = dx.astype(BF16)
            acck = acck + dg
        dgq_ref[...] += accq
        dgk_ref[...] += acck
        du_ref[:, cfg.o_fv:cfg.o_fv + wf] = dfv_ref[...].astype(BF16)
        accs = jnp.zeros((1, LANES), F32)
        for p in range(nps):
            sl = slice(p * LANES, (p + 1) * LANES)
            dx, dg = _head_norm_bwd(dsqn_ref[:, sl], sq_ref[:, sl], gsq_ref[...], lo, SCALE)
            du_ref[:, cfg.o_sq + p * LANES: cfg.o_sq + (p + 1) * LANES] = dx.astype(BF16)
            accs = accs + dg
        dgsq_ref[...] += accs

        def fold(ref):
            a0, a1 = ref[:, :LANES], ref[:, LANES:]
            return jnp.where(lo, a0 + pltpu.roll(a0, 64, 1), a1 + pltpu.roll(a1, 64, 1))

        dx, dg = _head_norm_bwd(fold(dskd_ref), sk_ref[...], gsk_ref[...], lo, 1.0)
        du_ref[:, cfg.o_sk:cfg.o_sk + LANES] = dx.astype(BF16)
        dgsk_ref[...] += dg
        du_ref[:, cfg.o_sv:cfg.o_sv + LANES] = fold(dsvd_ref).astype(BF16)

        dc = dct_ref[...].T + dcq_ref[...]
        row = lax.broadcasted_iota(jnp.int32, (T, T), 0)
        col = lax.broadcasted_iota(jnp.int32, (T, T), 1)
        triu = jnp.where(col >= row, 1.0, 0.0).astype(F32)
        dlf = jnp.dot(triu, dc, precision=HI, preferred_element_type=F32) + carry_ref[0:1, :]
        carry_ref[0:1, :] = dlf[0:1, :]
        z = fz_ref[...] + b_ref[...]
        dz = dlf * _sigmoid(-z)
        du_ref[:, cfg.o_fz:cfg.o_fz + LANES] = dz.astype(BF16)
        db_ref[...] += jnp.sum(dz, axis=0, keepdims=True)

    def rows(w, cb):
        return pl.BlockSpec((T, w), lambda i, cb=cb: (nb - 1 - i, cb))

    vec = pl.BlockSpec((1, LANES), lambda i: (0, 0))
    vshape = jax.ShapeDtypeStruct((1, LANES), F32)
    return pl.pallas_call(
        body, name=name, grid=(nb,),
        in_specs=[rows(2 * wf, 0), rows(ws, cfg.o_sq // ws), rows(LANES, cfg.o_sk // LANES),
                  rows(LANES, cfg.o_fz // LANES), vec, vec, vec, vec, vec,
                  rows(wf, 0), rows(wf, 0), rows(wf, 0), rows(ws, 0), rows(2 * LANES, 0), rows(2 * LANES, 0),
                  pl.BlockSpec((LANES, T), lambda i: (0, nb - 1 - i)), rows(LANES, 0)],
        out_specs=[rows(cfg.up, 0), vec, vec, vec, vec, vec],
        out_shape=[jax.ShapeDtypeStruct((L, cfg.up), BF16)] + [vshape] * 5,
        scratch_shapes=[pltpu.VMEM((8, LANES), F32)],
        compiler_params=_cparams(("arbitrary",)),
    )(u, u, u, u, gq, gk, gsq, gsk, bias, dqn, dkn, dfv, dsqn, dskd, dsvd, dct, dcq)


def _fox_fwd(name, cfg, qn, kn, fv, c, ct, carry=None):
    L, wf, cs = cfg.L, cfg.wf, cfg.cs
    TQ = BLOCK
    TK = _tile(L, 544, 8)
    npairs = wf // LANES
    pad = cfg.pad

    def body(q_ref, k_ref, v_ref, c_ref, ct_ref, o_ref, lse_ref):
        p = pl.program_id(0)
        i = pl.program_id(1)
        lo_q = _lane((TQ, LANES)) < HEAD_DIM
        lane_k = _lane((TK, LANES))
        lo_k = lane_k < HEAD_DIM
        lo_d = lax.broadcasted_iota(jnp.int32, (LANES, TQ), 0) < HEAD_DIM
        first = _lane((TK, 2 * TQ)) < TQ
        q = q_ref[...]
        qs = jnp.concatenate([jnp.where(lo_q, q, jnp.zeros_like(q)), jnp.where(lo_q, jnp.zeros_like(q), q)], axis=0)
        cq = jnp.concatenate([ct_ref[0:1, :], ct_ref[1:2, :]], axis=1)
        qrow = lax.broadcasted_iota(jnp.int32, (TK, 2 * TQ), 1)
        qpos = i * TQ + jnp.where(first, qrow, qrow - TQ)

        def step(j, carry_):
            m, l, acc = carry_
            off = pl.multiple_of(j * TK, 8)
            k = k_ref[pl.ds(off, TK), :]
            v = v_ref[pl.ds(off, TK), :]
            cblk = c_ref[pl.ds(off, TK), :]
            kpos = j * TK + lax.broadcasted_iota(jnp.int32, (TK, 2 * TQ), 0)
            allowed = (kpos <= qpos) & (kpos >= pad)
            ck0 = jnp.sum(jnp.where(lane_k == p * cs, cblk, 0.0), axis=1, keepdims=True)
            ck1 = jnp.sum(jnp.where(lane_k == p * cs + 1, cblk, 0.0), axis=1, keepdims=True)
            s = _dot_nt(k, qs) + cq - jnp.where(first, ck0, ck1)
            s = jnp.where(allowed, s, NEG_INF)
            m_new = jnp.maximum(m, jnp.max(s, axis=0, keepdims=True))
            alpha = jnp.exp(m - m_new)
            pr = jnp.exp(s - m_new)
            l = alpha * l + jnp.sum(pr, axis=0, keepdims=True)
            prb = pr.astype(BF16)
            prs = jnp.concatenate([prb[:, :TQ], prb[:, TQ:]], axis=0)
            vs = jnp.concatenate([jnp.where(lo_k, v, jnp.zeros_like(v)), jnp.where(lo_k, jnp.zeros_like(v), v)], axis=0)
            acc = acc * jnp.where(lo_d, alpha[:, :TQ], alpha[:, TQ:]) + _dot_tn(vs, prs)
            return m_new, l, acc

        init = (jnp.full((1, 2 * TQ), NEG_INF, F32), jnp.zeros((1, 2 * TQ), F32), jnp.zeros((LANES, TQ), F32))
        m, l, acc = lax.fori_loop(0, ((i + 1) * TQ + TK - 1) // TK, step, init)
        o_ref[...] = (acc / jnp.where(lo_d, l[:, :TQ], l[:, TQ:])).T
        lse = m + jnp.log(l)
        lse_ref[...] = jnp.where(lo_d, lse[:, :TQ], lse[:, TQ:]).T

    blk = pl.BlockSpec((TQ, LANES), lambda p, i: (i, p))
    full = pl.BlockSpec((L, LANES), lambda p, i: (0, p))
    return _call(
        name, body, (npairs, L // TQ),
        [blk, full, full, pl.BlockSpec((L, LANES), lambda p, i: (0, 0)), pl.BlockSpec((cs, TQ), lambda p, i: (p, i))],
        [blk, blk], [jax.ShapeDtypeStruct((L, wf), F32)] * 2, [qn, kn, fv, c, ct], carry=carry)


def _fox_bwd(name, cfg, qn, kn, fv, c, ct, o, lse, do, carry=None):
    L, wf, cs = cfg.L, cfg.wf, cfg.cs
    T = BLOCK
    TQ = _tile(L, 544, 8)
    nb = L // T
    nq = L // TQ
    npairs = wf // LANES
    pad = cfg.pad

    def body(q_ref, k_ref, v_ref, c_ref, ct_ref, o_ref, lse_ref, do_ref, dq_ref, dk_ref, dv_ref, dct_ref, dcq_ref,
             cq_ref, lser_ref, dsum_ref, dsacc_ref):
        p = pl.program_id(0)
        j = pl.program_id(1)
        lane = _lane((TQ, LANES))
        lo = lane < HEAD_DIM
        sels = (lo, jnp.logical_not(lo))
        lo_k = _lane((T, LANES)) < HEAD_DIM
        sels_k = (lo_k, jnp.logical_not(lo_k))

        @pl.when((j == 0) & (p == 0))
        def _():
            dcq_ref[...] = jnp.zeros_like(dcq_ref)

        @pl.when(j == 0)
        def _():
            dq_ref[...] = jnp.zeros_like(dq_ref)
            dsacc_ref[...] = jnp.zeros_like(dsacc_ref)
            for t in range(nq):
                rows = slice(t * TQ, (t + 1) * TQ)
                dd = do_ref[rows, :] * o_ref[rows, :]
                lse_b = lse_ref[rows, :]
                cblk = c_ref[rows, :]
                for hh in range(2):
                    dsum_ref[hh, rows, :] = jnp.broadcast_to(
                        jnp.sum(jnp.where(sels[hh], dd, 0.0), axis=1, keepdims=True), (TQ, LANES))
                    lser_ref[hh, rows, :] = jnp.broadcast_to(
                        jnp.sum(jnp.where(lane == hh * HEAD_DIM, lse_b, 0.0), axis=1, keepdims=True), (TQ, LANES))
                    cq_ref[hh, rows, :] = jnp.broadcast_to(
                        jnp.sum(jnp.where(lane == p * cs + hh, cblk, 0.0), axis=1, keepdims=True), (TQ, LANES))

        k = k_ref[...]
        v = v_ref[...]
        kh = tuple(jnp.where(s_, k, jnp.zeros_like(k)) for s_ in sels_k)
        ck = tuple(ct_ref[hh:hh + 1, :] for hh in range(2))
        kpos = j * T + lax.broadcasted_iota(jnp.int32, (TQ, T), 1)

        def step(i, carry_):
            dk, dv, dc0, dc1 = carry_
            dcs = [dc0, dc1]
            off = pl.multiple_of(i * TQ, 8)
            rows = pl.ds(off, TQ)
            q = q_ref[rows, :]
            dov = do_ref[rows, :]
            qpos = i * TQ + lax.broadcasted_iota(jnp.int32, (TQ, T), 0)
            allowed = (kpos <= qpos) & (kpos >= pad)
            dq = jnp.zeros((TQ, LANES), F32)
            for hh in range(2):
                sel = sels[hh]
                qhh = jnp.where(sel, q, jnp.zeros_like(q))
                doh = jnp.where(sel, dov, 0.0).astype(BF16)
                s = _dot_nt(qhh, k) + cq_ref[hh, rows, :] - ck[hh]
                pr = jnp.where(allowed, jnp.exp(jnp.where(allowed, s, NEG_INF) - lser_ref[hh, rows, :]), 0.0)
                ds = pr * (_dot_nt(doh, v) - dsum_ref[hh, rows, :])
                dsb = ds.astype(BF16)
                dv = dv + _dot_tn(pr.astype(BF16), doh)
                dk = dk + _dot_tn(dsb, qhh)
                dq = dq + _dot(dsb, kh[hh])
                dcs[hh] = dcs[hh] - jnp.sum(ds, axis=0, keepdims=True)
                dsacc_ref[hh, rows, :] += ds
            dq_ref[rows, :] += dq
            return dk, dv, dcs[0], dcs[1]

        init = (jnp.zeros((T, LANES), F32), jnp.zeros((T, LANES), F32),
                jnp.zeros((1, T), F32), jnp.zeros((1, T), F32))
        dk, dv, dc0, dc1 = lax.fori_loop((j * T) // TQ, nq, step, init)
        dk_ref[...] = dk
        dv_ref[...] = dv
        dct_ref[...] = jnp.zeros_like(dct_ref)
        dct_ref[0:1, :] = dc0
        dct_ref[1:2, :] = dc1

        @pl.when(j == nb - 1)
        def _():
            for t in range(nq):
                rows = slice(t * TQ, (t + 1) * TQ)
                upd = jnp.zeros((TQ, LANES), F32)
                for hh in range(2):
                    upd = upd + jnp.where(lane == p * cs + hh,
                                          jnp.sum(dsacc_ref[hh, rows, :], axis=1, keepdims=True), 0.0)
                dcq_ref[rows, :] += upd

    blk = pl.BlockSpec((T, LANES), lambda p, j: (j, p))
    full = pl.BlockSpec((L, LANES), lambda p, j: (0, p))
    return _call(
        name, body, (npairs, nb),
        [full, blk, blk, pl.BlockSpec((L, LANES), lambda p, j: (0, 0)), pl.BlockSpec((cs, T), lambda p, j: (p, j)),
         full, full, full],
        [full, blk, blk, pl.BlockSpec((cs, T), lambda p, j: (p, j)), pl.BlockSpec((L, LANES), lambda p, j: (0, 0))],
        [jax.ShapeDtypeStruct((L, wf), F32)] * 3
        + [jax.ShapeDtypeStruct((LANES, L), F32), jax.ShapeDtypeStruct((L, LANES), F32)],
        [qn, kn, fv, c, ct, o, lse, do], scratch=[pltpu.VMEM((2, L, LANES), F32)] * 4, carry=carry)


def _swa_band(i, pad):
    T = BLOCK
    t = lax.broadcasted_iota(jnp.int32, (T, 2 * T), 0)
    u = lax.broadcasted_iota(jnp.int32, (T, 2 * T), 1)
    dist = t - u + T
    allowed = (dist >= 0) & (dist < T) & ((i - 1) * T + u >= pad)
    return dist.astype(F32), allowed


def _swa_probs(s, dist, allowed, slope, sink):
    s = jnp.where(allowed, s - slope * dist, NEG_INF)
    m = jnp.maximum(jnp.max(s, axis=1, keepdims=True), sink)
    e = jnp.exp(s - m)
    es = jnp.exp(sink - m)
    den = jnp.sum(e, axis=1, keepdims=True) + es
    return e / den, es / den


def _swa_stack(x, group, lo):
    tiles = []
    for h in range(group):
        pair = x[:, (h // 2) * LANES:(h // 2 + 1) * LANES]
        tiles.append(jnp.where(lo if h % 2 == 0 else jnp.logical_not(lo), pair, jnp.zeros_like(pair)))
    return jnp.concatenate(tiles, axis=0)


def _swa_fwd(name, cfg, sqn, skd, svd, sinks, slopes, carry=None):
    L, ws, group, hkv = cfg.L, cfg.ws, cfg.group, cfg.hkv
    T = BLOCK
    gw = group * HEAD_DIM
    pad = cfg.pad

    def body(sink_ref, slope_ref, q_ref, kp_ref, kc_ref, vp_ref, vc_ref, o_ref):
        kv = pl.program_id(0)
        i = pl.program_id(1)
        lo = _lane((T, LANES)) < HEAD_DIM
        dist, allowed = _swa_band(i, pad)
        kb = jnp.concatenate([kp_ref[...], kc_ref[...]], axis=0)
        vb = jnp.concatenate([vp_ref[...], vc_ref[...]], axis=0)
        s = _dot_nt(_swa_stack(q_ref[...], group, lo), kb)
        probs = []
        for h in range(group):
            p, _ = _swa_probs(s[h * T:(h + 1) * T], dist, allowed, slope_ref[kv * group + h], sink_ref[kv * group + h])
            probs.append(p.astype(BF16))
        o = _dot(jnp.concatenate(probs, axis=0), vb)
        for pp in range(group // 2):
            o_ref[:, pp * LANES:(pp + 1) * LANES] = jnp.where(lo, o[2 * pp * T:(2 * pp + 1) * T],
                                                              o[(2 * pp + 1) * T:(2 * pp + 2) * T])

    smem = pl.BlockSpec(memory_space=pltpu.SMEM)
    prev = pl.BlockSpec((T, LANES), lambda kv, i: (jnp.maximum(i - 1, 0), kv))
    cur = pl.BlockSpec((T, LANES), lambda kv, i: (i, kv))
    qblk = pl.BlockSpec((T, gw), lambda kv, i: (i, kv))
    return _call(name, body, (hkv, L // T), [smem, smem, qblk, prev, cur, prev, cur], [qblk],
                 [jax.ShapeDtypeStruct((L, ws), F32)], [sinks, slopes, sqn, skd, skd, svd, svd], carry=carry)


def _swa_bwd(name, cfg, sqn, skd, svd, sinks, slopes, o, do, carry=None):
    L, ws, group, hkv = cfg.L, cfg.ws, cfg.group, cfg.hkv
    T = BLOCK
    gw = group * HEAD_DIM
    ppk = group // 2
    pad = cfg.pad

    def body(sink_ref, slope_ref, q_ref, kp_ref, kc_ref, vp_ref, vc_ref, o_ref, do_ref,
             dq_ref, dk_ref, dv_ref, dsink_ref):
        kv = pl.program_id(0)
        i = pl.program_id(1)
        lane = _lane((T, LANES))
        lo = lane < HEAD_DIM

        @pl.when(i == 0)
        def _():
            dk_ref[...] = jnp.zeros_like(dk_ref)
            dv_ref[...] = jnp.zeros_like(dv_ref)
            dsink_ref[...] = jnp.zeros_like(dsink_ref)

        dist, allowed = _swa_band(i, pad)
        kb = jnp.concatenate([kp_ref[...], kc_ref[...]], axis=0)
        vb = jnp.concatenate([vp_ref[...], vc_ref[...]], axis=0)
        dov = do_ref[...]
        qs = _swa_stack(q_ref[...], group, lo)
        dos = _swa_stack(dov, group, lo).astype(BF16)
        dd = dov * o_ref[...]
        s = _dot_nt(qs, kb)
        dp = _dot_nt(dos, vb)
        probs, dss = [], []
        for h in range(group):
            pair = dd[:, (h // 2) * LANES:(h // 2 + 1) * LANES]
            dsum = jnp.sum(jnp.where(lo if h % 2 == 0 else jnp.logical_not(lo), pair, 0.0), axis=1, keepdims=True)
            p, ps = _swa_probs(s[h * T:(h + 1) * T], dist, allowed, slope_ref[kv * group + h], sink_ref[kv * group + h])
            probs.append(p.astype(BF16))
            dss.append((p * (dp[h * T:(h + 1) * T] - dsum)).astype(BF16))
            dsink_ref[8 * (h // 2):8 * (h // 2) + 1, :] += jnp.where(lane[0:1, :] == (h % 2) * HEAD_DIM,
                                                                    -jnp.sum(ps * dsum), 0.0)
        ds = jnp.concatenate(dss, axis=0)
        dq = _dot(ds, kb)
        for pp in range(ppk):
            dq_ref[:, pp * LANES:(pp + 1) * LANES] = jnp.where(lo, dq[2 * pp * T:(2 * pp + 1) * T],
                                                               dq[(2 * pp + 1) * T:(2 * pp + 2) * T])
        dk = _dot_tn(ds, qs)
        dv = _dot_tn(jnp.concatenate(probs, axis=0), dos)
        cur = pl.multiple_of(i * T, T)
        dk_ref[pl.ds(cur, T), :] += dk[T:]
        dv_ref[pl.ds(cur, T), :] += dv[T:]

        @pl.when(i > 0)
        def _():
            prv = pl.multiple_of((i - 1) * T, T)
            dk_ref[pl.ds(prv, T), :] += dk[:T]
            dv_ref[pl.ds(prv, T), :] += dv[:T]

    smem = pl.BlockSpec(memory_space=pltpu.SMEM)
    prev = pl.BlockSpec((T, LANES), lambda kv, i: (jnp.maximum(i - 1, 0), kv))
    cur = pl.BlockSpec((T, LANES), lambda kv, i: (i, kv))
    qblk = pl.BlockSpec((T, gw), lambda kv, i: (i, kv))
    full = pl.BlockSpec((L, LANES), lambda kv, i: (0, kv))
    return _call(
        name, body, (hkv, L // T), [smem, smem, qblk, prev, cur, prev, cur, qblk, qblk],
        [qblk, full, full, pl.BlockSpec((8 * ppk, LANES), lambda kv, i: (kv, 0))],
        [jax.ShapeDtypeStruct((L, ws), F32), jax.ShapeDtypeStruct((L, 2 * LANES), F32),
         jax.ShapeDtypeStruct((L, 2 * LANES), F32), jax.ShapeDtypeStruct((8 * ppk * hkv, LANES), F32)],
        [sinks, slopes, sqn, skd, skd, svd, svd, o, do], carry=carry)


def _out_norm(name, cfg, o_fox, o_swa, g_fox, g_swa):
    L, wf, ws = cfg.L, cfg.wf, cfg.ws
    tm = _tile(L, 256, 16)

    def body(of_ref, os_ref, gf_ref, gs_ref, o_ref):
        for src, g_ref, lo_, w in ((of_ref, gf_ref, 0, wf), (os_ref, gs_ref, wf, ws)):
            x = src[...]
            r = lax.rsqrt(jnp.mean(x * x, axis=1, keepdims=True) + EPS)
            o_ref[:, lo_:lo_ + w] = (x * r * g_ref[...]).astype(BF16)

    return pl.pallas_call(
        body, name=name, grid=(L // tm,),
        in_specs=[pl.BlockSpec((tm, wf), lambda i: (i, 0)), pl.BlockSpec((tm, ws), lambda i: (i, 0)),
                  pl.BlockSpec((1, wf), lambda i: (0, 0)), pl.BlockSpec((1, ws), lambda i: (0, 0))],
        out_specs=pl.BlockSpec((tm, wf + ws), lambda i: (i, 0)),
        out_shape=jax.ShapeDtypeStruct((L, wf + ws), BF16),
        compiler_params=_cparams(("parallel",)),
    )(o_fox, o_swa, g_fox, g_swa)


def _out_norm_bwd(name, cfg, dcat, o_fox, o_swa, g_fox, g_swa):
    L, wf, ws = cfg.L, cfg.wf, cfg.ws
    tm = _tile(L, 256, 16)

    def body(d_ref, of_ref, os_ref, gf_ref, gs_ref, dof_ref, dos_ref, dgf_ref, dgs_ref):
        i = pl.program_id(0)

        @pl.when(i == 0)
        def _():
            dgf_ref[...] = jnp.zeros_like(dgf_ref)
            dgs_ref[...] = jnp.zeros_like(dgs_ref)

        for src, g_ref, dst, dg_ref, lo_, w in ((of_ref, gf_ref, dof_ref, dgf_ref, 0, wf),
                                                (os_ref, gs_ref, dos_ref, dgs_ref, wf, ws)):
            x = src[...]
            dy = d_ref[:, lo_:lo_ + w]
            r = lax.rsqrt(jnp.mean(x * x, axis=1, keepdims=True) + EPS)
            wv = dy * g_ref[...]
            proj = jnp.sum(wv * x, axis=1, keepdims=True) * (1.0 / w)
            dst[...] = r * wv - x * (r * r * r * proj)
            dg_ref[...] += jnp.sum(dy * x * r, axis=0, keepdims=True)

    return pl.pallas_call(
        body, name=name, grid=(L // tm,),
        in_specs=[pl.BlockSpec((tm, wf + ws), lambda i: (i, 0)), pl.BlockSpec((tm, wf), lambda i: (i, 0)),
                  pl.BlockSpec((tm, ws), lambda i: (i, 0)),
                  pl.BlockSpec((1, wf), lambda i: (0, 0)), pl.BlockSpec((1, ws), lambda i: (0, 0))],
        out_specs=[pl.BlockSpec((tm, wf), lambda i: (i, 0)), pl.BlockSpec((tm, ws), lambda i: (i, 0)),
                   pl.BlockSpec((1, wf), lambda i: (0, 0)), pl.BlockSpec((1, ws), lambda i: (0, 0))],
        out_shape=[jax.ShapeDtypeStruct((L, wf), F32), jax.ShapeDtypeStruct((L, ws), F32),
                   jax.ShapeDtypeStruct((1, wf), F32), jax.ShapeDtypeStruct((1, ws), F32)],
        compiler_params=_cparams(("arbitrary",)),
    )(dcat, o_fox, o_swa, g_fox, g_swa)


def _loss_head(name, h, target):
    L, D = h.shape
    T = BLOCK

    def body(h_ref, t_ref, loss_ref, dh_ref, dh16_ref):
        i = pl.program_id(0)

        @pl.when(i == 0)
        def _():
            loss_ref[...] = jnp.zeros_like(loss_ref)
            dh_ref[...] = jnp.zeros_like(dh_ref)
            dh16_ref[...] = jnp.zeros_like(dh16_ref)

        @pl.when(i > 0)
        def _():
            err = h_ref[...] - t_ref[...]
            dh = err * (1.0 / D)
            dh_ref[...] = dh
            dh16_ref[...] = dh.astype(BF16)
            loss_ref[...] += jnp.sum(err * err) * (0.5 / D)

    return pl.pallas_call(
        body, name=name, grid=(L // T,),
        in_specs=[pl.BlockSpec((T, D), lambda i: (i, 0)), pl.BlockSpec((T, D), lambda i: (jnp.maximum(i - 1, 0), 0))],
        out_specs=[pl.BlockSpec((8, LANES), lambda i: (0, 0)), pl.BlockSpec((T, D), lambda i: (i, 0)),
                   pl.BlockSpec((T, D), lambda i: (i, 0))],
        out_shape=[jax.ShapeDtypeStruct((8, LANES), F32), jax.ShapeDtypeStruct((L, D), F32),
                   jax.ShapeDtypeStruct((L, D), BF16)],
        compiler_params=_cparams(("arbitrary",)),
    )(h, target)


def _pair_add(name, grad, landed, parts, layer, core):
    _, _, r, C = grad.shape

    def body(s_ref, g_ref, l_ref, p_ref, o_ref):
        o_ref[...] = (g_ref[...].astype(F32) + l_ref[...].astype(F32)).astype(o_ref.dtype)

    return pl.pallas_call(
        body, name=name,
        grid_spec=pltpu.PrefetchScalarGridSpec(
            num_scalar_prefetch=1, grid=(4,),
            in_specs=[pl.BlockSpec((None, None, r, C), lambda k, s: (k, s[0], 0, 0)),
                      pl.BlockSpec((None, r, C), lambda k, s: (k, 0, 0)),
                      pl.BlockSpec(memory_space=pl.ANY)],
            out_specs=pl.BlockSpec((None, None, r, C), lambda k, s: (layer, k, 0, 0))),
        out_shape=jax.ShapeDtypeStruct(parts.shape, parts.dtype),
        input_output_aliases={3: 0},
        compiler_params=_cparams(("arbitrary",)),
    )(core, grad, landed, parts)


def _chip_sum(name, part, landed, chip, layer_axis):
    A, _, r, C = part.shape

    def body(s_ref, p_ref, l0_ref, l1_ref, l2_ref, o_ref):
        o_ref[...] = ((p_ref[...].astype(F32) + l0_ref[...].astype(F32))
                      + (l1_ref[...].astype(F32) + l2_ref[...].astype(F32)))

    def land(k):
        return pl.BlockSpec((None, None, r, C), lambda a, s, k=k: (k, a, 0, 0))

    if layer_axis == 0:
        out_spec, out_shape = pl.BlockSpec((None, r, C), lambda a, s: (a, 0, 0)), (A, r, C)
    else:
        out_spec, out_shape = pl.BlockSpec((r, C), lambda a, s: (0, a)), (r, A * C)
    out = pl.pallas_call(
        body, name=name,
        grid_spec=pltpu.PrefetchScalarGridSpec(
            num_scalar_prefetch=1, grid=(A,),
            in_specs=[pl.BlockSpec((None, None, r, C), lambda a, s: (a, s[0], 0, 0)), land(0), land(1), land(2)],
            out_specs=out_spec),
        out_shape=jax.ShapeDtypeStruct(out_shape, F32),
        compiler_params=_cparams(("parallel",)),
    )(chip, part, landed, landed, landed)
    return out if layer_axis == 0 else out.reshape(r, A, C)


def _gather_sum(name, v):
    R = v.shape[0]

    def body(x_ref, o_ref, buf_ref, send_sems, recv_sems):
        x, y, c = _place()
        me, sibling = (x, y, c), (x, y, 1 - c)
        chips = [(1 - x, y), (x, 1 - y), (1 - x, 1 - y)]

        def rows(dev):
            px, py, pc = dev
            return buf_ref.at[4 * px + 2 * py + pc]

        def copy(k, block, to, src=None):
            return pltpu.make_async_remote_copy(
                src_ref=rows(block) if src is None else src, dst_ref=rows(block),
                send_sem=send_sems.at[k], recv_sem=recv_sems.at[k], device_id=to, device_id_type=MESH)

        first = [copy(0, me, sibling, src=x_ref)]
        first += [copy(1 + j, me, (*chip, c), src=x_ref) for j, chip in enumerate(chips)]
        for cp in first:
            cp.start()
        rows(me)[...] = x_ref[...]
        passed = [copy(4 + j, (*chip, c), sibling) for j, chip in enumerate(chips)]
        for j, chip in enumerate(chips):
            copy(1 + j, (*chip, c), me).wait_recv()
            passed[j].start()
        copy(0, sibling, me).wait_recv()
        for j, chip in enumerate(chips):
            copy(4 + j, (*chip, 1 - c), me).wait_recv()
        for cp in first + passed:
            cp.wait_send()
        acc = buf_ref[0]
        for d in range(1, N_DEV):
            acc = acc + buf_ref[d]
        o_ref[...] = acc

    vm = pl.BlockSpec(memory_space=pltpu.VMEM)
    return pl.pallas_call(
        body, name=name, in_specs=[vm], out_specs=vm,
        out_shape=jax.ShapeDtypeStruct((R, LANES), F32),
        scratch_shapes=[pltpu.VMEM((N_DEV, R, LANES), F32), pltpu.SemaphoreType.DMA((7,)), pltpu.SemaphoreType.DMA((7,))],
    )(v)


def _adamw(name, g, w, m, v):
    A, R, C = g.shape
    budget = 1 << 18
    tr = _tile(R, max(8, budget // C // 8 * 8), 8)
    ta = _tile(A, max(1, budget // (tr * C)), 1)

    def body(g_ref, w_ref, m_ref, v_ref, d_ref, nm_ref, nv_ref):
        gv = g_ref[...]
        nm = ADAM_B1 * m_ref[...] + (1.0 - ADAM_B1) * gv
        nv = ADAM_B2 * v_ref[...] + (1.0 - ADAM_B2) * (gv * gv)
        m_hat = nm / (1.0 - ADAM_B1 ** ADAM_STEP)
        v_hat = nv / (1.0 - ADAM_B2 ** ADAM_STEP)
        d_ref[...] = -ADAM_LR * (m_hat / (jnp.sqrt(v_hat) + ADAM_EPS) + ADAM_WD * w_ref[...])
        nm_ref[...] = nm
        nv_ref[...] = nv

    blk = pl.BlockSpec((ta, tr, C), lambda i, j: (i, j, 0))
    shp = jax.ShapeDtypeStruct((A, R, C), F32)
    return pl.pallas_call(
        body, name=name, grid=(A // ta, R // tr),
        in_specs=[blk] * 4, out_specs=[blk] * 3, out_shape=[shp] * 3,
        compiler_params=_cparams(("parallel", "parallel")),
    )(g, w, m, v)


def _adamw_nd(name, g, w, m, v):
    shape = w.shape
    three = (1,) + shape if len(shape) == 2 else shape
    return tuple(o.reshape(shape) for o in _adamw(name, *[a.reshape(three) for a in (g, w, m, v)]))


def _scatter_heads(cfg, vals):
    v = jnp.pad(vals.reshape(cfg.hf // 2, 2), ((0, 0), (0, cfg.cs - 2)))
    return v.reshape(1, LANES)


def _gather_heads(cfg, row):
    return row.reshape(cfg.hf // 2, cfg.cs)[:, :2].reshape(cfg.hf)


def _permute_w_in(cfg, w_in_t):
    wf, hf = cfg.wf, cfg.hf
    o = 3 * wf
    cols = w_in_t.shape[1]
    fz = w_in_t[o:o + hf].reshape(hf // 2, 2, cols)
    fz_blk = jnp.pad(fz, ((0, 0), (0, cfg.cs - 2), (0, 0))).reshape(LANES, cols)
    return jnp.concatenate([w_in_t[:o], w_in_t[o + hf:], fz_blk], axis=0)


def _unpermute_dw_in(cfg, dwp):
    wf, hf = cfg.wf, cfg.hf
    o = 3 * wf
    cols = dwp.shape[1]
    fz = dwp[cfg.o_fz:].reshape(hf // 2, cfg.cs, cols)[:, :2].reshape(hf, cols)
    return jnp.concatenate([dwp[:o], fz, dwp[o:cfg.o_fz]], axis=0)


def _pair_gain(g):
    return jnp.tile(g, 2)[None]


def _fold_pair(dg):
    return dg[0, :HEAD_DIM] + dg[0, HEAD_DIM:]


def kernel(x, meta_tokens, ffn1_norm, ffn1_w_gate, ffn1_w_up, ffn1_w_down, mix_norm, w_in, b_forget, fox_q_norm, fox_k_norm, swa_q_norm, swa_k_norm, swa_sinks, fox_out_norm, swa_out_norm, w_out, ffn2_norm, ffn2_w_gate, ffn2_w_up, ffn2_w_down, loss_target, m_meta_tokens, m_ffn1_norm, m_ffn1_w_gate, m_ffn1_w_up, m_ffn1_w_down, m_mix_norm, m_w_in, m_b_forget, m_fox_q_norm, m_fox_k_norm, m_swa_q_norm, m_swa_k_norm, m_swa_sinks, m_fox_out_norm, m_swa_out_norm, m_w_out, m_ffn2_norm, m_ffn2_w_gate, m_ffn2_w_up, m_ffn2_w_down, v_meta_tokens, v_ffn1_norm, v_ffn1_w_gate, v_ffn1_w_up, v_ffn1_w_down, v_mix_norm, v_w_in, v_b_forget, v_fox_q_norm, v_fox_k_norm, v_swa_q_norm, v_swa_k_norm, v_swa_sinks, v_fox_out_norm, v_swa_out_norm, v_w_out, v_ffn2_norm, v_ffn2_w_gate, v_ffn2_w_up, v_ffn2_w_down):
    weights = dict(meta_tokens=meta_tokens, ffn1_norm=ffn1_norm, ffn1_w_gate=ffn1_w_gate, ffn1_w_up=ffn1_w_up,
                   ffn1_w_down=ffn1_w_down, mix_norm=mix_norm, w_in=w_in, b_forget=b_forget, fox_q_norm=fox_q_norm,
                   fox_k_norm=fox_k_norm, swa_q_norm=swa_q_norm, swa_k_norm=swa_k_norm, swa_sinks=swa_sinks,
                   fox_out_norm=fox_out_norm, swa_out_norm=swa_out_norm, w_out=w_out, ffn2_norm=ffn2_norm,
                   ffn2_w_gate=ffn2_w_gate, ffn2_w_up=ffn2_w_up, ffn2_w_down=ffn2_w_down)
    mom_m = dict(meta_tokens=m_meta_tokens, ffn1_norm=m_ffn1_norm, ffn1_w_gate=m_ffn1_w_gate, ffn1_w_up=m_ffn1_w_up,
                 ffn1_w_down=m_ffn1_w_down, mix_norm=m_mix_norm, w_in=m_w_in, b_forget=m_b_forget,
                 fox_q_norm=m_fox_q_norm, fox_k_norm=m_fox_k_norm, swa_q_norm=m_swa_q_norm, swa_k_norm=m_swa_k_norm,
                 swa_sinks=m_swa_sinks, fox_out_norm=m_fox_out_norm, swa_out_norm=m_swa_out_norm, w_out=m_w_out,
                 ffn2_norm=m_ffn2_norm, ffn2_w_gate=m_ffn2_w_gate, ffn2_w_up=m_ffn2_w_up, ffn2_w_down=m_ffn2_w_down)
    mom_v = dict(meta_tokens=v_meta_tokens, ffn1_norm=v_ffn1_norm, ffn1_w_gate=v_ffn1_w_gate, ffn1_w_up=v_ffn1_w_up,
                 ffn1_w_down=v_ffn1_w_down, mix_norm=v_mix_norm, w_in=v_w_in, b_forget=v_b_forget,
                 fox_q_norm=v_fox_q_norm, fox_k_norm=v_fox_k_norm, swa_q_norm=v_swa_q_norm, swa_k_norm=v_swa_k_norm,
                 swa_sinks=v_swa_sinks, fox_out_norm=v_fox_out_norm, swa_out_norm=v_swa_out_norm, w_out=v_w_out,
                 ffn2_norm=v_ffn2_norm, ffn2_w_gate=v_ffn2_w_gate, ffn2_w_up=v_ffn2_w_up, ffn2_w_down=v_ffn2_w_down)
    names = list(weights)

    _, S, D = x.shape
    depth = ffn1_norm.shape[0]
    hf, hs = b_forget.shape[1], swa_sinks.shape[1]
    U = w_in.shape[2] * N_DEV
    hkv = (U - 3 * HEAD_DIM * hf - hf - HEAD_DIM * hs) // (2 * HEAD_DIM)
    cfg = _Cfg(S, D, hf, hs, hkv)
    n_meta = meta_tokens.shape[0]
    assert n_meta == cfg.n_meta
    x_idx, y_idx, c_idx = _place()
    chip_idx = 2 * x_idx + y_idx
    dev_idx = 2 * chip_idx + c_idx
    core_s = jnp.reshape(c_idx, (1,)).astype(jnp.int32)
    chip_s = jnp.reshape(chip_idx, (1,)).astype(jnp.int32)

    col_sharded = ("ffn1_w_gate", "ffn1_w_up", "w_in", "ffn2_w_gate", "ffn2_w_up")
    use_order = ("ffn1_w_gate", "ffn1_w_up", "ffn1_w_down", "w_in", "w_out", "ffn2_w_gate", "ffn2_w_up", "ffn2_w_down")

    layer_axis = {k: (1 if k == "w_in" else 0) for k in use_order}

    def rows_view(k, a):
        if k not in col_sharded:
            return a
        return jnp.transpose(a, (2, 0, 1)) if k == "w_in" else jnp.swapaxes(a, 1, 2)

    def from_rows_view(k, a):
        if k not in col_sharded:
            return a
        return jnp.transpose(a, (1, 2, 0)) if k == "w_in" else jnp.swapaxes(a, 1, 2)

    w_rows = {k: rows_view(k, weights[k]) for k in use_order}

    def shard(key):
        k, l = key
        if k == "meta_tokens":
            return meta_tokens
        return _shard_bf16("weight_shard", w_rows[k], l, layer_axis[k] if k in col_sharded else 0)

    waiting = [("meta_tokens", 0)] + [(k, l) for l in range(depth) for k in use_order]
    halfway = []
    gathered = {}

    def fwd_carry(n_first):
        cy = _Carry()
        second = [(key, _gather_second(cy, buf)) for key, buf in halfway]
        first = [(key, _gather_first(cy, shard(key))) for key in waiting[:n_first]]
        del waiting[:n_first]
        halfway.clear()
        return cy, (first, second)

    def fwd_absorb(extra, plan):
        first, second = plan
        for key, idx in second:
            gathered[key] = extra[idx]
        for key, idx in first:
            halfway.append((key, extra[idx]))

    def weight(k, l):
        key = (k, l)
        while key not in gathered:
            n = 0 if any(key == hk for hk, _ in halfway) else waiting.index(key) + 1
            cy, plan = fwd_carry(n)
            fwd_absorb(_comm_only("weights_gather", cy), plan)
        g = gathered[key]
        return g.reshape(-1, g.shape[-1])

    weight("ffn1_w_down", 0)
    meta_full = jnp.swapaxes(weight("meta_tokens", 0).reshape(N_DEV, n_meta, -1), 0, 1).reshape(n_meta, D)
    slopes = jnp.asarray(2.0 ** (-8.0 * np.arange(1, hs + 1) / hs), dtype=F32)

    h = jnp.concatenate([jnp.zeros((cfg.pad, D), F32), meta_full, x[0]], axis=0)
    saved = []
    w_in_p = [None] * depth

    def mm_f(name, a, b, n_first=1, **kw):
        cy, plan = fwd_carry(n_first)
        out, extra = _matmul(name, a, b, carry=cy, **kw)
        fwd_absorb(extra, plan)
        return out

    def ffn_fwd(tag, l, h_in, norm, wg, wu, wd):
        xn = _rmsnorm_fwd(f"{tag}_norm", h_in, norm[l][None])
        wg_t, wu_t = weight(wg, l), weight(wu, l)
        cy, plan = fwd_carry(1)
        (gate, up, act), extra = _ffn_up(f"{tag}_up", xn, wg_t, wu_t, carry=cy)
        fwd_absorb(extra, plan)
        h_out = mm_f(f"{tag}_down", act, weight(wd, l), scale=0.5, residual=h_in, tm=544, tn=1024, tk=2816)
        return h_out, (xn, gate, up)

    for l in range(depth):
        st = {"h0": h}
        h, st["ffn1"] = ffn_fwd("ffn1", l, h, ffn1_norm, "ffn1_w_gate", "ffn1_w_up", "ffn1_w_down")
        st["h1"] = h
        xn = _rmsnorm_fwd("mix_norm", h, mix_norm[l][None])
        w_in_p[l] = _permute_w_in(cfg, weight("w_in", l))
        u = mm_f("mix_in", xn, w_in_p[l], trans_b=True, tm=544, tn=640, tk=2048)
        gq, gk = _pair_gain(fox_q_norm[l]), _pair_gain(fox_k_norm[l])
        gsq, gsk = _pair_gain(swa_q_norm[l]), _pair_gain(swa_k_norm[l])
        bias = _scatter_heads(cfg, b_forget[l])
        qn, kn, fv, sqn, skd, svd, c, ct = _mix_prep("mix_prep", cfg, u, gq, gk, gsq, gsk, bias)
        cy, plan = fwd_carry(2)
        (o_fox, lse), extra = _fox_fwd("fox_fwd", cfg, qn, kn, fv, c, ct, carry=cy)
        fwd_absorb(extra, plan)
        cy, plan = fwd_carry(1)
        (o_swa,), extra = _swa_fwd("swa_fwd", cfg, sqn, skd, svd, swa_sinks[l], slopes, carry=cy)
        fwd_absorb(extra, plan)
        o_cat = _out_norm("out_norm", cfg, o_fox, o_swa, fox_out_norm[l][None], swa_out_norm[l][None])
        st["mix"] = (xn, u, gq, gk, gsq, gsk, bias, qn, kn, fv, sqn, skd, svd, c, ct, o_fox, lse, o_swa, o_cat)
        h = mm_f("mix_out", o_cat, weight("w_out", l), n_first=0, residual=h, tm=544, tn=1024, tk=2048)
        st["h2"] = h
        h, st["ffn2"] = ffn_fwd("ffn2", l, h, ffn2_norm, "ffn2_w_gate", "ffn2_w_up", "ffn2_w_down")
        saved.append(st)

    loss_blk, dh, dh16 = _loss_head("loss_head", h, loss_target[0])

    small = {k: [None] * depth for k in names if k not in use_order and k != "meta_tokens"}
    parts, landing = {}, {}
    to_sibling, to_chips = [], []

    def bwd_carry(n_chips):
        cy = _Carry()
        t1, t3 = [], []
        for k, l in list(to_chips):
            if len(t3) < n_chips and all(k != k3 for k3, _ in t3):
                to_chips.remove((k, l))
                t3.append((k, _scatter_chips(cy, parts[k], landing[k], l)))
        while to_sibling:
            k, l, g = to_sibling.pop(0)
            t1.append((k, l, g, _scatter_sibling(cy, g)))
        return cy, (t1, t3)

    def bwd_absorb(extra, plan):
        t1, t3 = plan
        for k, idx in t3:
            landing[k] = extra[idx]
        for k, l, g, idx in t1:
            parts[k] = _pair_add("grads_pair_add", g, extra[idx], parts[k], l, core_s)
            to_chips.append((k, l))

    def emit_grad(k, l, dw):
        r, C = dw.shape[0] // N_DEV, dw.shape[1]
        if k not in parts:
            parts[k] = lax.empty((depth, 4, r, C), BF16)
            landing[k] = lax.empty((3, depth, r, C), BF16)
        to_sibling.append((k, l, dw.reshape(4, 2, r, C)))

    def mm_b(name, a, b, n_chips=0, **kw):
        cy, plan = bwd_carry(n_chips)
        out, extra = _matmul(name, a, b, carry=cy, **kw)
        bwd_absorb(extra, plan)
        return out

    def ffn_bwd(tag, l, dh_out, dh_out16, h_in, st_, norm, wg, wu, wd):
        xn, gate, up = st_
        short = 1 if l == 0 else 0
        cy, plan = bwd_carry(1)
        (dgate, dup, act), extra = _ffn_bwd_act(f"{tag}_dact", dh_out16, weight(wd, l), gate, up, carry=cy)
        bwd_absorb(extra, plan)
        emit_grad(wd, l, mm_b(f"{tag}_dwd", act, dh_out16, short, trans_a=True, scale=0.5, out_dtype=BF16,
                              tm=512, tn=1024, tk=2176))
        emit_grad(wg, l, mm_b(f"{tag}_dwg", dgate, xn, short, trans_a=True, out_dtype=BF16, tm=512, tn=1024, tk=2176))
        emit_grad(wu, l, mm_b(f"{tag}_dwu", dup, xn, short, trans_a=True, out_dtype=BF16, tm=512, tn=1024, tk=2176))
        dxn = mm_b(f"{tag}_dxn", dgate, weight(wg, l), 1, pair2=(dup, weight(wu, l)), tm=544, tn=1024, tk=1408)
        dh_in, dh_in16, dg = _rmsnorm_bwd(f"{tag}_dnorm", dxn, h_in, norm[l][None], dh_out)
        return dh_in, dh_in16, dg[0]

    for l in reversed(range(depth)):
        st = saved[l]
        dh, dh16, small["ffn2_norm"][l] = ffn_bwd("ffn2", l, dh, dh16, st["h2"], st["ffn2"], ffn2_norm,
                                                   "ffn2_w_gate", "ffn2_w_up", "ffn2_w_down")
        xn, u, gq, gk, gsq, gsk, bias, qn, kn, fv, sqn, skd, svd, c, ct, o_fox, lse, o_swa, o_cat = st["mix"]
        dcat = mm_b("mix_dcat", dh16, weight("w_out", l), int(l == 0), trans_b=True, tm=544, tn=1024, tk=2048)
        emit_grad("w_out", l, mm_b("mix_dwout", o_cat, dh16, int(l == 0), trans_a=True, out_dtype=BF16, tm=512, tn=1024, tk=2176))
        do_fox, do_swa, dgf, dgs = _out_norm_bwd("out_norm_bwd", cfg, dcat, o_fox, o_swa,
                                                 fox_out_norm[l][None], swa_out_norm[l][None])
        small["fox_out_norm"][l], small["swa_out_norm"][l] = dgf[0], dgs[0]
        cy, plan = bwd_carry(3)
        (dqn, dkn, dfv, dct, dcq), extra = _fox_bwd("fox_bwd", cfg, qn, kn, fv, c, ct, o_fox, lse, do_fox, carry=cy)
        bwd_absorb(extra, plan)
        cy, plan = bwd_carry(1)
        (dsqn, dskd, dsvd, dsink), extra = _swa_bwd("swa_bwd", cfg, sqn, skd, svd, swa_sinks[l], slopes, o_swa, do_swa,
                                                    carry=cy)
        bwd_absorb(extra, plan)
        small["swa_sinks"][l] = dsink.reshape(hs // 2, 8, LANES)[:, 0, ::HEAD_DIM].reshape(hs)
        du, dgq, dgk, dgsq, dgsk, db = _mix_prep_bwd("mix_prep_bwd", cfg, u, gq, gk, gsq, gsk, bias,
                                                     dqn, dkn, dfv, dsqn, dskd, dsvd, dct, dcq)
        small["fox_q_norm"][l], small["fox_k_norm"][l] = _fold_pair(dgq), _fold_pair(dgk)
        small["swa_q_norm"][l], small["swa_k_norm"][l] = _fold_pair(dgsq), _fold_pair(dgsk)
        small["b_forget"][l] = _gather_heads(cfg, db[0])
        dwp = mm_b("mix_dwin", du, xn, int(l == 0), trans_a=True, out_dtype=BF16, tm=640, tn=1024, tk=2176)
        emit_grad("w_in", l, _unpermute_dw_in(cfg, dwp))
        dxn = mm_b("mix_dxn", du, w_in_p[l], int(l == 0), tm=544, tn=1024, tk=4480)
        dh, dh16, dg = _rmsnorm_bwd("mix_dnorm", dxn, st["h1"], mix_norm[l][None], dh)
        small["mix_norm"][l] = dg[0]
        dh, dh16, small["ffn1_norm"][l] = ffn_bwd("ffn1", l, dh, dh16, st["h0"], st["ffn1"], ffn1_norm,
                                                   "ffn1_w_gate", "ffn1_w_up", "ffn1_w_down")

    grad_x = dh[BLOCK:][None]
    dmeta = dh[cfg.pad:BLOCK]

    while to_sibling or to_chips:
        cy, plan = bwd_carry(len(use_order))
        bwd_absorb(_comm_only("grads_scatter", cy), plan)

    grads = {}
    for k in use_order:
        grads[k] = _chip_sum("grads_chip_sum", parts[k], landing[k], chip_s, layer_axis[k] if k in col_sharded else 0)

    small_names = list(small)
    pieces = [loss_blk[0, :1], dmeta.reshape(-1)] + [jnp.stack(small[k]).reshape(-1) for k in small_names]
    sizes = [int(p.shape[0]) for p in pieces]
    total = sum(sizes)
    padded = -(-total // (8 * LANES)) * (8 * LANES)
    vec = jnp.concatenate(pieces + [jnp.zeros((padded - total,), F32)]).reshape(-1, LANES)
    summed = _gather_sum("small_gather_sum", vec).reshape(-1)
    offs = np.cumsum([0] + sizes)
    loss = summed[0]
    dmeta_full = summed[offs[1]:offs[2]].reshape(n_meta, D)
    mcols = meta_tokens.shape[1]
    grads["meta_tokens"] = lax.dynamic_slice_in_dim(dmeta_full, dev_idx * mcols, mcols, axis=1)
    for n_, k in enumerate(small_names):
        grads[k] = summed[offs[2 + n_]:offs[3 + n_]].reshape(weights[k].shape)

    delta, new_m, new_v = {}, {}, {}
    for k in names:
        if k in col_sharded:
            outs = _adamw_nd("adamw", grads[k], w_rows[k], rows_view(k, mom_m[k]), rows_view(k, mom_v[k]))
            delta[k], new_m[k], new_v[k] = (from_rows_view(k, o) for o in outs)
            grads[k] = from_rows_view(k, grads[k])
        else:
            delta[k], new_m[k], new_v[k] = _adamw_nd("adamw", grads[k], weights[k], mom_m[k], mom_v[k])

    return (loss, grad_x, *[grads[k] for k in names], *[delta[k] for k in names],
            *[new_m[k] for k in names], *[new_v[k] for k in names])
```

```python
import functools

import numpy as np
import jax
import jax.numpy as jnp
from jax import lax
from jax.experimental import pallas as pl
from jax.experimental.pallas import tpu as pltpu

F32 = jnp.float32
BF16 = jnp.bfloat16
MESH = pl.DeviceIdType.MESH

HEAD_DIM = 64
BLOCK = 128
LANES = 128
N_DEV = 8
EPS = 1e-6
NEG_INF = -1e30
SCALE = HEAD_DIM ** -0.5

ADAM_LR = 0.001
ADAM_B1 = 0.9
ADAM_B2 = 0.999
ADAM_EPS = 1e-08
ADAM_WD = 0.01
ADAM_STEP = 10

VMEM_BYTES_V7X = 64 * 1024 * 1024
VMEM_LIMIT = VMEM_BYTES_V7X * 3 // 4

NT = (((1,), (1,)), ((), ()))
TN = (((0,), (0,)), ((), ()))
HI = lax.Precision.HIGHEST


def _cparams(sem=None, vmem=VMEM_LIMIT):
    return pltpu.CompilerParams(dimension_semantics=sem, vmem_limit_bytes=vmem)


def _tile(n, pref, mult):
    best = None
    for t in range(mult, min(n, pref) + 1, mult):
        if n % t == 0:
            best = t
    return best if best is not None else n


def _dot(a, b):
    return jnp.dot(a, b, preferred_element_type=F32)


def _dot_nt(a, b):
    return lax.dot_general(a, b, NT, preferred_element_type=F32)


def _dot_tn(a, b):
    return lax.dot_general(a, b, TN, preferred_element_type=F32)


def _lane(shape):
    return lax.broadcasted_iota(jnp.int32, shape, len(shape) - 1)


def _half_sum(x, lo):
    s0 = jnp.sum(jnp.where(lo, x, 0.0), axis=1, keepdims=True)
    s1 = jnp.sum(jnp.where(lo, 0.0, x), axis=1, keepdims=True)
    return jnp.where(lo, s0, s1)


def _sigmoid(x):
    return 1.0 / (1.0 + jnp.exp(-x))


def _place():
    return lax.axis_index("x"), lax.axis_index("y"), lax.axis_index("c")


def _peers(x, y, c):
    return [(x, y, 1 - c), (1 - x, y, c), (x, 1 - y, c), (1 - x, 1 - y, c)]


def _dev_index(dev):
    px, py, pc = dev
    return 4 * px + 2 * py + pc


class _Carry:
    def __init__(self):
        self.ins, self.outs, self.alias, self.items = [], [], {}, []
        self.nsem = self.nloc = 0

    def add(self, ins, outs, alias, nsem, nloc, build):
        i0, o0 = len(self.ins), len(self.outs)
        for src, dst in alias.items():
            self.alias[i0 + src] = o0 + dst
        self.items.append((i0, len(ins), o0, len(outs), self.nsem, self.nloc, build))
        self.ins += ins
        self.outs += outs
        self.nsem += nsem
        self.nloc += nloc
        return list(range(o0, o0 + len(outs)))

    def build(self, in_refs, out_refs, ssem, rsem, lsem):
        ops = []
        for i0, ni, o0, no, s0, l0, fn in self.items:
            ops.append(fn(in_refs[i0:i0 + ni], out_refs[o0:o0 + no],
                          lambda k, s0=s0: (ssem.at[s0 + k], rsem.at[s0 + k]), lambda k, l0=l0: lsem.at[l0 + k]))
        return ops


def _remote(src, dst, sems, dev):
    return pltpu.make_async_remote_copy(src_ref=src, dst_ref=dst, send_sem=sems[0], recv_sem=sems[1],
                                        device_id=dev, device_id_type=MESH)


def _gather_first(carry, shard):
    def build(ins, outs, sems, locs):
        src, buf = ins[0], outs[0]
        x, y, c = _place()
        me = _dev_index((x, y, c))
        peers = _peers(x, y, c)
        local = pltpu.make_async_copy(src, buf.at[me], locs(0))
        sends = [_remote(src, buf.at[me], sems(k), dev) for k, dev in enumerate(peers)]
        recvs = [_remote(src, buf.at[_dev_index(dev)], sems(k), dev) for k, dev in enumerate(peers)]

        def start():
            local.start()
            for cp in sends:
                cp.start()

        def wait():
            for cp in sends:
                cp.wait_send()
            for cp in recvs:
                cp.wait_recv()
            local.wait()

        return start, wait

    return carry.add([shard], [jax.ShapeDtypeStruct((N_DEV,) + shard.shape, shard.dtype)], {}, 4, 1, build)[0]


def _gather_second(carry, buf):
    def build(ins, outs, sems, locs):
        b = outs[0]
        x, y, c = _place()
        chips = _peers(x, y, c)[1:]
        sends = [_remote(b.at[_dev_index(dev)], b.at[_dev_index(dev)], sems(k), (x, y, 1 - c))
                 for k, dev in enumerate(chips)]
        recvs = [_remote(b.at[_dev_index(dev)], b.at[_dev_index((dev[0], dev[1], 1 - c))], sems(k), (x, y, 1 - c))
                 for k, dev in enumerate(chips)]

        def start():
            for cp in sends:
                cp.start()

        def wait():
            for cp in sends:
                cp.wait_send()
            for cp in recvs:
                cp.wait_recv()

        return start, wait

    return carry.add([buf], [jax.ShapeDtypeStruct(buf.shape, buf.dtype)], {0: 0}, 3, 0, build)[0]


def _scatter_sibling(carry, grad):
    def build(ins, outs, sems, locs):
        x, y, c = _place()
        cp = _remote(ins[0].at[:, 1 - c], outs[0], sems(0), (x, y, 1 - c))
        return cp.start, cp.wait

    shape = (grad.shape[0],) + grad.shape[2:]
    return carry.add([grad], [jax.ShapeDtypeStruct(shape, grad.dtype)], {}, 1, 0, build)[0]


def _scatter_chips(carry, parts, landing, layer):
    def build(ins, outs, sems, locs):
        x, y, c = _place()
        cps = [_remote(ins[0].at[layer, 2 * dev[0] + dev[1]], outs[0].at[k, layer], sems(k), dev)
               for k, dev in enumerate(_peers(x, y, c)[1:])]

        def start():
            for cp in cps:
                cp.start()

        def wait():
            for cp in cps:
                cp.wait()

        return start, wait

    return carry.add([parts, landing], [jax.ShapeDtypeStruct(landing.shape, landing.dtype)], {1: 0}, 3, 0, build)[0]


def _call(name, body, grid, in_specs, out_specs, out_shape, args, scratch=(), carry=None):
    ni, no, ns = len(args), len(out_shape), len(scratch)
    if carry is None or not carry.items:
        res = pl.pallas_call(
            body, name=name, grid=grid, in_specs=list(in_specs), out_specs=list(out_specs), out_shape=list(out_shape),
            scratch_shapes=list(scratch), compiler_params=_cparams(("arbitrary",) * len(grid)))(*args)
        return list(res), []
    nci, nco = len(carry.ins), len(carry.outs)

    def full_body(*refs):
        c_in = refs[ni:ni + nci]
        c_out = refs[ni + nci + no:ni + nci + no + nco]
        sc = refs[ni + nci + no + nco:]
        ops = carry.build(c_in, c_out, sc[ns], sc[ns + 1], sc[ns + 2])
        first = last = None
        for d, n in enumerate(grid):
            pid = pl.program_id(d)
            first = (pid == 0) if first is None else first & (pid == 0)
            last = (pid == n - 1) if last is None else last & (pid == n - 1)

        @pl.when(first)
        def _():
            for start, _w in ops:
                start()

        body(*refs[:ni], *refs[ni + nci:ni + nci + no], *sc[:ns])

        @pl.when(last)
        def _():
            for _s, wait in ops:
                wait()

    hbm = pl.BlockSpec(memory_space=pl.ANY)
    res = pl.pallas_call(
        full_body, name=name, grid=grid,
        in_specs=list(in_specs) + [hbm] * nci, out_specs=list(out_specs) + [hbm] * nco,
        out_shape=list(out_shape) + list(carry.outs),
        input_output_aliases={ni + s: no + d for s, d in carry.alias.items()},
        scratch_shapes=list(scratch) + [pltpu.SemaphoreType.DMA((carry.nsem,)), pltpu.SemaphoreType.DMA((carry.nsem,)),
                                        pltpu.SemaphoreType.DMA((max(carry.nloc, 1),))],
        compiler_params=_cparams(("arbitrary",) * len(grid)))(*args, *carry.ins)
    return list(res[:no]), list(res[no:])


def _comm_only(name, carry):
    return _call(name, lambda *refs: None, (1,), [], [], [], [], carry=carry)[1]


def _matmul(name, a, b, *, pair2=None, trans_a=False, trans_b=False, out_dtype=F32, scale=None, residual=None,
            tm=512, tn=512, tk=512, carry=None):
    if trans_a:
        K, M = a.shape
    else:
        M, K = a.shape
    if trans_b:
        N, Kb = b.shape
    else:
        Kb, N = b.shape
    assert K == Kb, (name, a.shape, b.shape)
    tm = _tile(M, tm, LANES if trans_a else 16)
    tn = _tile(N, tn, LANES)
    tk = _tile(K, tk, 16 if (trans_a and not trans_b) else LANES)
    nk = K // tk
    dims = (((0 if trans_a else 1,), (1 if trans_b else 0,)), ((), ()))
    pairs = [(a, b)] + ([pair2] if pair2 is not None else [])
    npair = len(pairs)

    def body(*refs):
        ab = refs[:2 * npair]
        pos = 2 * npair
        r_ref = None
        if residual is not None:
            r_ref = refs[pos]
            pos += 1
        o_ref = refs[pos]

        def partial():
            t = None
            for q in range(npair):
                d = lax.dot_general(ab[2 * q][...].astype(BF16), ab[2 * q + 1][...].astype(BF16), dims,
                                    preferred_element_type=F32)
                t = d if t is None else t + d
            return t

        def finish(r):
            if scale is not None:
                r = r * scale
            if r_ref is not None:
                r = r + r_ref[...].astype(F32)
            o_ref[...] = r.astype(o_ref.dtype)

        if nk == 1:
            finish(partial())
            return
        acc_ref = refs[pos + 1]
        k = pl.program_id(2)

        @pl.when(k == 0)
        def _():
            acc_ref[...] = partial()

        @pl.when(k > 0)
        def _():
            acc_ref[...] += partial()

        @pl.when(k == nk - 1)
        def _():
            finish(acc_ref[...])

    a_spec = pl.BlockSpec((tk, tm), lambda i, j, k: (k, i)) if trans_a else pl.BlockSpec((tm, tk), lambda i, j, k: (i, k))
    b_spec = pl.BlockSpec((tn, tk), lambda i, j, k: (j, k)) if trans_b else pl.BlockSpec((tk, tn), lambda i, j, k: (k, j))
    in_specs, args = [], []
    for pa, pb in pairs:
        in_specs += [a_spec, b_spec]
        args += [pa, pb]
    if residual is not None:
        in_specs.append(pl.BlockSpec((tm, tn), lambda i, j, k: (i, j)))
        args.append(residual)
    res, extra = _call(
        name, body, (M // tm, N // tn, nk), in_specs, [pl.BlockSpec((tm, tn), lambda i, j, k: (i, j))],
        [jax.ShapeDtypeStruct((M, N), out_dtype)], args,
        scratch=[pltpu.VMEM((tm, tn), F32)] if nk > 1 else [], carry=carry)
    return res[0], extra


def _shard_bf16(name, w, layer, layer_axis):
    R, C = w.shape[1 - layer_axis], w.shape[2]
    tr = _tile(R, 512, 16)

    def body(x_ref, o_ref):
        o_ref[...] = x_ref[...].astype(BF16)

    if layer_axis == 0:
        in_spec = pl.BlockSpec((None, tr, C), lambda i: (layer, i, 0))
    else:
        in_spec = pl.BlockSpec((tr, C), lambda i: (i, layer))
        w = w.reshape(R, -1)
    return pl.pallas_call(
        body, name=name, grid=(R // tr,),
        in_specs=[in_spec], out_specs=pl.BlockSpec((tr, C), lambda i: (i, 0)),
        out_shape=jax.ShapeDtypeStruct((R, C), BF16),
        compiler_params=_cparams(("parallel",)),
    )(w)


def _rmsnorm_fwd(name, h, g):
    L, D = h.shape
    tm = _tile(L, 256, 16)

    def body(h_ref, g_ref, o_ref):
        x = h_ref[...]
        r = lax.rsqrt(jnp.mean(x * x, axis=1, keepdims=True) + EPS)
        o_ref[...] = (x * r * g_ref[...]).astype(o_ref.dtype)

    return pl.pallas_call(
        body, name=name, grid=(L // tm,),
        in_specs=[pl.BlockSpec((tm, D), lambda i: (i, 0)), pl.BlockSpec((1, D), lambda i: (0, 0))],
        out_specs=pl.BlockSpec((tm, D), lambda i: (i, 0)),
        out_shape=jax.ShapeDtypeStruct((L, D), BF16),
        compiler_params=_cparams(("parallel",)),
    )(h, g)


def _rmsnorm_bwd(name, dy, h, g, dres):
    L, D = h.shape
    tm = _tile(L, 256, 16)

    def body(dy_ref, h_ref, g_ref, dres_ref, dh_ref, dh16_ref, dg_ref):
        i = pl.program_id(0)
        x = h_ref[...]
        dyv = dy_ref[...].astype(F32)
        r = lax.rsqrt(jnp.mean(x * x, axis=1, keepdims=True) + EPS)
        w = dyv * g_ref[...]
        proj = jnp.sum(w * x, axis=1, keepdims=True) * (1.0 / D)
        dh = dres_ref[...] + r * w - x * (r * r * r * proj)
        dh_ref[...] = dh
        dh16_ref[...] = dh.astype(BF16)

        @pl.when(i == 0)
        def _():
            dg_ref[...] = jnp.zeros_like(dg_ref)

        dg_ref[...] += jnp.sum(dyv * x * r, axis=0, keepdims=True)

    return pl.pallas_call(
        body, name=name, grid=(L // tm,),
        in_specs=[pl.BlockSpec((tm, D), lambda i: (i, 0)), pl.BlockSpec((tm, D), lambda i: (i, 0)),
                  pl.BlockSpec((1, D), lambda i: (0, 0)), pl.BlockSpec((tm, D), lambda i: (i, 0))],
        out_specs=[pl.BlockSpec((tm, D), lambda i: (i, 0)), pl.BlockSpec((tm, D), lambda i: (i, 0)),
                   pl.BlockSpec((1, D), lambda i: (0, 0))],
        out_shape=[jax.ShapeDtypeStruct((L, D), F32), jax.ShapeDtypeStruct((L, D), BF16),
                   jax.ShapeDtypeStruct((1, D), F32)],
        compiler_params=_cparams(("arbitrary",)),
    )(dy, h, g, dres)


def _ffn_up(name, xn, wgT, wuT, carry=None):
    L, D = xn.shape
    F = wgT.shape[0]
    tm = _tile(L, 544, 16)
    tn = _tile(F, 512, LANES)

    def body(x_ref, wg_ref, wu_ref, g_ref, u_ref, a_ref):
        x = x_ref[...]
        g = _dot_nt(x, wg_ref[...])
        u = _dot_nt(x, wu_ref[...])
        g_ref[...] = g.astype(BF16)
        u_ref[...] = u.astype(BF16)
        a_ref[...] = (g * _sigmoid(g) * u).astype(BF16)

    o_spec = pl.BlockSpec((tm, tn), lambda i, j: (i, j))
    o_shape = jax.ShapeDtypeStruct((L, F), BF16)
    return _call(
        name, body, (L // tm, F // tn),
        [pl.BlockSpec((tm, D), lambda i, j: (i, 0)), pl.BlockSpec((tn, D), lambda i, j: (j, 0)),
         pl.BlockSpec((tn, D), lambda i, j: (j, 0))],
        [o_spec, o_spec, o_spec], [o_shape, o_shape, o_shape], [xn, wgT, wuT], carry=carry)


def _ffn_bwd_act(name, dh, wd, gate, up, carry=None):
    L, D = dh.shape
    F = wd.shape[0]
    tm = _tile(L, 544, 16)
    tn = _tile(F, 512, LANES)

    def body(dh_ref, wd_ref, g_ref, u_ref, dg_ref, du_ref, a_ref):
        da = 0.5 * _dot_nt(dh_ref[...].astype(BF16), wd_ref[...])
        g = g_ref[...].astype(F32)
        u = u_ref[...].astype(F32)
        sg = _sigmoid(g)
        silu = g * sg
        dg_ref[...] = (da * u * (sg * (1.0 + g * (1.0 - sg)))).astype(BF16)
        du_ref[...] = (da * silu).astype(BF16)
        a_ref[...] = (silu * u).astype(BF16)

    o_spec = pl.BlockSpec((tm, tn), lambda i, j: (i, j))
    o_shape = jax.ShapeDtypeStruct((L, F), BF16)
    return _call(
        name, body, (L // tm, F // tn),
        [pl.BlockSpec((tm, D), lambda i, j: (i, 0)), pl.BlockSpec((tn, D), lambda i, j: (j, 0)), o_spec, o_spec],
        [o_spec, o_spec, o_spec], [o_shape, o_shape, o_shape], [dh, wd, gate, up], carry=carry)


class _Cfg:
    def __init__(self, S, D, hf, hs, hkv):
        self.S, self.D, self.L = S, D, S + BLOCK
        self.hf, self.hs, self.hkv = hf, hs, hkv
        self.wf, self.ws = hf * HEAD_DIM, hs * HEAD_DIM
        self.group = hs // hkv
        self.cs = 2 * LANES // hf
        self.n_meta = 16
        self.pad = BLOCK - self.n_meta
        self.o_fk = self.wf
        self.o_fv = 2 * self.wf
        self.o_sq = 3 * self.wf
        self.o_sk = self.o_sq + self.ws
        self.o_sv = self.o_sk + LANES
        self.o_fz = self.o_sv + LANES
        self.up = self.o_fz + LANES
        assert hkv == 2 and hf % 2 == 0 and self.group % 2 == 0 and self.cs % 8 == 0
        assert self.o_sq % self.ws == 0 and (2 * self.wf) % LANES == 0


def _head_norm(x, gain, lo, mult):
    r = lax.rsqrt(_half_sum(x * x, lo) * (1.0 / HEAD_DIM) + EPS)
    return x * r * (gain * mult)


def _head_norm_bwd(dy, x, gain, lo, mult):
    r = lax.rsqrt(_half_sum(x * x, lo) * (1.0 / HEAD_DIM) + EPS)
    w = dy * (gain * mult)
    proj = _half_sum(w * x, lo) * (1.0 / HEAD_DIM)
    dx = r * w - x * (r * r * r * proj)
    dgain = jnp.sum(dy * mult * x * r, axis=0, keepdims=True)
    return dx, dgain


def _dup(x, lo):
    xr = pltpu.roll(x, 64, 1)
    return jnp.where(lo, x, xr), jnp.where(lo, xr, x)


def _mix_prep(name, cfg, u, gq, gk, gsq, gsk, bias):
    L, wf, ws = cfg.L, cfg.wf, cfg.ws
    T = BLOCK
    npf, nps = wf // LANES, ws // LANES

    def body(fqk_ref, fv_ref, sq_ref, sk_ref, sv_ref, fz_ref, gq_ref, gk_ref, gsq_ref, gsk_ref, b_ref,
             qn_ref, kn_ref, fvo_ref, sqn_ref, skd_ref, svd_ref, c_ref, ct_ref, carry_ref):
        i = pl.program_id(0)
        lo = _lane((T, LANES)) < HEAD_DIM
        for p in range(npf):
            sl = slice(p * LANES, (p + 1) * LANES)
            qn_ref[:, sl] = _head_norm(fqk_ref[:, sl], gq_ref[...], lo, SCALE).astype(BF16)
            kn_ref[:, sl] = _head_norm(fqk_ref[:, wf + p * LANES: wf + (p + 1) * LANES], gk_ref[...], lo, 1.0).astype(BF16)
        fvo_ref[...] = fv_ref[...].astype(BF16)
        for p in range(nps):
            sl = slice(p * LANES, (p + 1) * LANES)
            sqn_ref[:, sl] = _head_norm(sq_ref[:, sl], gsq_ref[...], lo, SCALE).astype(BF16)
        k0, k1 = _dup(_head_norm(sk_ref[...], gsk_ref[...], lo, 1.0), lo)
        skd_ref[:, :LANES] = k0.astype(BF16)
        skd_ref[:, LANES:] = k1.astype(BF16)
        v0, v1 = _dup(sv_ref[...], lo)
        svd_ref[:, :LANES] = v0.astype(BF16)
        svd_ref[:, LANES:] = v1.astype(BF16)

        @pl.when(i == 0)
        def _():
            carry_ref[...] = jnp.zeros_like(carry_ref)

        z = fz_ref[...] + b_ref[...]
        lf = jnp.minimum(z, 0.0) - jnp.log(1.0 + jnp.exp(-jnp.abs(z)))
        row = lax.broadcasted_iota(jnp.int32, (T, T), 0)
        col = lax.broadcasted_iota(jnp.int32, (T, T), 1)
        tri = jnp.where(col <= row, 1.0, 0.0).astype(F32)
        c = jnp.dot(tri, lf, precision=HI, preferred_element_type=F32) + carry_ref[0:1, :]
        c_ref[...] = c
        ct_ref[...] = c.T
        carry_ref[0:1, :] = c_ref[T - 1:T, :]

    def rows(w, cb):
        return pl.BlockSpec((T, w), lambda i, cb=cb: (i, cb))

    vec = pl.BlockSpec((1, LANES), lambda i: (0, 0))
    return pl.pallas_call(
        body, name=name, grid=(L // T,),
        in_specs=[rows(2 * wf, 0), rows(wf, 2), rows(ws, cfg.o_sq // ws), rows(LANES, cfg.o_sk // LANES),
                  rows(LANES, cfg.o_sv // LANES), rows(LANES, cfg.o_fz // LANES), vec, vec, vec, vec, vec],
        out_specs=[rows(wf, 0), rows(wf, 0), rows(wf, 0), rows(ws, 0), rows(2 * LANES, 0), rows(2 * LANES, 0),
                   rows(LANES, 0), pl.BlockSpec((LANES, T), lambda i: (0, i))],
        out_shape=[jax.ShapeDtypeStruct((L, wf), BF16)] * 3 + [jax.ShapeDtypeStruct((L, ws), BF16)]
        + [jax.ShapeDtypeStruct((L, 2 * LANES), BF16)] * 2
        + [jax.ShapeDtypeStruct((L, LANES), F32), jax.ShapeDtypeStruct((LANES, L), F32)],
        scratch_shapes=[pltpu.VMEM((8, LANES), F32)],
        compiler_params=_cparams(("arbitrary",)),
    )(u, u, u, u, u, u, gq, gk, gsq, gsk, bias)


def _mix_prep_bwd(name, cfg, u, gq, gk, gsq, gsk, bias, dqn, dkn, dfv, dsqn, dskd, dsvd, dct, dcq):
    L, wf, ws = cfg.L, cfg.wf, cfg.ws
    T = BLOCK
    nb = L // T
    npf, nps = wf // LANES, ws // LANES

    def body(fqk_ref, sq_ref, sk_ref, fz_ref, gq_ref, gk_ref, gsq_ref, gsk_ref, b_ref,
             dqn_ref, dkn_ref, dfv_ref, dsqn_ref, dskd_ref, dsvd_ref, dct_ref, dcq_ref,
             du_ref, dgq_ref, dgk_ref, dgsq_ref, dgsk_ref, db_ref, carry_ref):
        i = pl.program_id(0)
        lo = _lane((T, LANES)) < HEAD_DIM

        @pl.when(i == 0)
        def _():
            carry_ref[...] = jnp.zeros_like(carry_ref)
            for r in (dgq_ref, dgk_ref, dgsq_ref, dgsk_ref, db_ref):
                r[...] = jnp.zeros_like(r)

        accq = jnp.zeros((1, LANES), F32)
        acck = jnp.zeros((1, LANES), F32)
        for p in range(npf):
            sl = slice(p * LANES, (p + 1) * LANES)
            dx, dg = _head_norm_bwd(dqn_ref[:, sl], fqk_ref[:, sl], gq_ref[...], lo, SCALE)
            du_ref[:, sl] = dx.astype(BF16)
            accq = accq + dg
            slk = slice(wf + p * LANES, wf + (p + 1) * LANES)
            dx, dg = _head_norm_bwd(dkn_ref[:, sl], fqk_ref[:, slk], gk_ref[...], lo, 1.0)
            du_ref[:, slk] = dx.astype(BF16)
            acck = acck + dg
        dgq_ref[...] += accq
        dgk_ref[...] += acck
        du_ref[:, cfg.o_fv:cfg.o_fv + wf] = dfv_ref[...].astype(BF16)
        accs = jnp.zeros((1, LANES), F32)
        for p in range(nps):
            sl = slice(p * LANES, (p + 1) * LANES)
            dx, dg = _head_norm_bwd(dsqn_ref[:, sl], sq_ref[:, sl], gsq_ref[...], lo, SCALE)
            du_ref[:, cfg.o_sq + p * LANES: cfg.o_sq + (p + 1) * LANES] = dx.astype(BF16)
            accs = accs + dg
        dgsq_ref[...] += accs

        def fold(ref):
            a0, a1 = ref[:, :LANES], ref[:, LANES:]
            return jnp.where(lo, a0 + pltpu.roll(a0, 64, 1), a1 + pltpu.roll(a1, 64, 1))

        dx, dg = _head_norm_bwd(fold(dskd_ref), sk_ref[...], gsk_ref[...], lo, 1.0)
        du_ref[:, cfg.o_sk:cfg.o_sk + LANES] = dx.astype(BF16)
        dgsk_ref[...] += dg
        du_ref[:, cfg.o_sv:cfg.o_sv + LANES] = fold(dsvd_ref).astype(BF16)

        dc = dct_ref[...].T + dcq_ref[...]
        row = lax.broadcasted_iota(jnp.int32, (T, T), 0)
        col = lax.broadcasted_iota(jnp.int32, (T, T), 1)
        triu = jnp.where(col >= row, 1.0, 0.0).astype(F32)
        dlf = jnp.dot(triu, dc, precision=HI, preferred_element_type=F32) + carry_ref[0:1, :]
        carry_ref[0:1, :] = dlf[0:1, :]
        z = fz_ref[...] + b_ref[...]
        dz = dlf * _sigmoid(-z)
        du_ref[:, cfg.o_fz:cfg.o_fz + LANES] = dz.astype(BF16)
        db_ref[...] += jnp.sum(dz, axis=0, keepdims=True)

    def rows(w, cb):
        return pl.BlockSpec((T, w), lambda i, cb=cb: (nb - 1 - i, cb))

    vec = pl.BlockSpec((1, LANES), lambda i: (0, 0))
    vshape = jax.ShapeDtypeStruct((1, LANES), F32)
    return pl.pallas_call(
        body, name=name, grid=(nb,),
        in_specs=[rows(2 * wf, 0), rows(ws, cfg.o_sq // ws), rows(LANES, cfg.o_sk // LANES),
                  rows(LANES, cfg.o_fz // LANES), vec, vec, vec, vec, vec,
                  rows(wf, 0), rows(wf, 0), rows(wf, 0), rows(ws, 0), rows(2 * LANES, 0), rows(2 * LANES, 0),
                  pl.BlockSpec((LANES, T), lambda i: (0, nb - 1 - i)), rows(LANES, 0)],
        out_specs=[rows(cfg.up, 0), vec, vec, vec, vec, vec],
        out_shape=[jax.ShapeDtypeStruct((L, cfg.up), BF16)] + [vshape] * 5,
        scratch_shapes=[pltpu.VMEM((8, LANES), F32)],
        compiler_params=_cparams(("arbitrary",)),
    )(u, u, u, u, gq, gk, gsq, gsk, bias, dqn, dkn, dfv, dsqn, dskd, dsvd, dct, dcq)


def _fox_fwd(name, cfg, qn, kn, fv, c, ct, carry=None):
    L, wf, cs = cfg.L, cfg.wf, cfg.cs
    TQ = BLOCK
    TK = _tile(L, 544, 8)
    npairs = wf // LANES
    pad = cfg.pad

    def body(q_ref, k_ref, v_ref, c_ref, ct_ref, o_ref, lse_ref):
        p = pl.program_id(0)
        i = pl.program_id(1)
        lo_q = _lane((TQ, LANES)) < HEAD_DIM
        lane_k = _lane((TK, LANES))
        lo_k = lane_k < HEAD_DIM
        lo_d = lax.broadcasted_iota(jnp.int32, (LANES, TQ), 0) < HEAD_DIM
        first = _lane((TK, 2 * TQ)) < TQ
        q = q_ref[...]
        qs = jnp.concatenate([jnp.where(lo_q, q, jnp.zeros_like(q)), jnp.where(lo_q, jnp.zeros_like(q), q)], axis=0)
        cq = jnp.concatenate([ct_ref[0:1, :], ct_ref[1:2, :]], axis=1)
        qrow = lax.broadcasted_iota(jnp.int32, (TK, 2 * TQ), 1)
        qpos = i * TQ + jnp.where(first, qrow, qrow - TQ)

        def step(j, carry_):
            m, l, acc = carry_
            off = pl.multiple_of(j * TK, 8)
            k = k_ref[pl.ds(off, TK), :]
            v = v_ref[pl.ds(off, TK), :]
            cblk = c_ref[pl.ds(off, TK), :]
            kpos = j * TK + lax.broadcasted_iota(jnp.int32, (TK, 2 * TQ), 0)
            allowed = (kpos <= qpos) & (kpos >= pad)
            ck0 = jnp.sum(jnp.where(lane_k == p * cs, cblk, 0.0), axis=1, keepdims=True)
            ck1 = jnp.sum(jnp.where(lane_k == p * cs + 1, cblk, 0.0), axis=1, keepdims=True)
            s = _dot_nt(k, qs) + cq - jnp.where(first, ck0, ck1)
            s = jnp.where(allowed, s, NEG_INF)
            m_new = jnp.maximum(m, jnp.max(s, axis=0, keepdims=True))
            alpha = jnp.exp(m - m_new)
            pr = jnp.exp(s - m_new)
            l = alpha * l + jnp.sum(pr, axis=0, keepdims=True)
            prb = pr.astype(BF16)
            prs = jnp.concatenate([prb[:, :TQ], prb[:, TQ:]], axis=0)
            vs = jnp.concatenate([jnp.where(lo_k, v, jnp.zeros_like(v)), jnp.where(lo_k, jnp.zeros_like(v), v)], axis=0)
            acc = acc * jnp.where(lo_d, alpha[:, :TQ], alpha[:, TQ:]) + _dot_tn(vs, prs)
            return m_new, l, acc

        init = (jnp.full((1, 2 * TQ), NEG_INF, F32), jnp.zeros((1, 2 * TQ), F32), jnp.zeros((LANES, TQ), F32))
        m, l, acc = lax.fori_loop(0, ((i + 1) * TQ + TK - 1) // TK, step, init)
        o_ref[...] = (acc / jnp.where(lo_d, l[:, :TQ], l[:, TQ:])).T
        lse = m + jnp.log(l)
        lse_ref[...] = jnp.where(lo_d, lse[:, :TQ], lse[:, TQ:]).T

    blk = pl.BlockSpec((TQ, LANES), lambda p, i: (i, p))
    full = pl.BlockSpec((L, LANES), lambda p, i: (0, p))
    return _call(
        name, body, (npairs, L // TQ),
        [blk, full, full, pl.BlockSpec((L, LANES), lambda p, i: (0, 0)), pl.BlockSpec((cs, TQ), lambda p, i: (p, i))],
        [blk, blk], [jax.ShapeDtypeStruct((L, wf), F32)] * 2, [qn, kn, fv, c, ct], carry=carry)


def _fox_bwd(name, cfg, qn, kn, fv, c, ct, o, lse, do, carry=None):
    L, wf, cs = cfg.L, cfg.wf, cfg.cs
    T = BLOCK
    TQ = _tile(L, 544, 8)
    nb = L // T
    nq = L // TQ
    npairs = wf // LANES
    pad = cfg.pad

    def body(q_ref, k_ref, v_ref, c_ref, ct_ref, o_ref, lse_ref, do_ref, dq_ref, dk_ref, dv_ref, dct_ref, dcq_ref,
             cq_ref, lser_ref, dsum_ref, dsacc_ref):
        p = pl.program_id(0)
        j = pl.program_id(1)
        lane = _lane((TQ, LANES))
        lo = lane < HEAD_DIM
        sels = (lo, jnp.logical_not(lo))

        def stack(x):
            return jnp.concatenate([jnp.where(lo, x, jnp.zeros_like(x)), jnp.where(lo, jnp.zeros_like(x), x)], axis=0)

        @pl.when((j == 0) & (p == 0))
        def _():
            dcq_ref[...] = jnp.zeros_like(dcq_ref)

        @pl.when(j == 0)
        def _():
            dq_ref[...] = jnp.zeros_like(dq_ref)
            dsacc_ref[...] = jnp.zeros_like(dsacc_ref)
            for t in range(nq):
                rows = slice(t * TQ, (t + 1) * TQ)
                dd = do_ref[rows, :] * o_ref[rows, :]
                lse_b = lse_ref[rows, :]
                cblk = c_ref[rows, :]
                for hh in range(2):
                    half = slice(hh * TQ, (hh + 1) * TQ)
                    dsum_ref[t, half, :] = jnp.broadcast_to(
                        jnp.sum(jnp.where(sels[hh], dd, 0.0), axis=1, keepdims=True), (TQ, LANES))
                    lser_ref[t, half, :] = jnp.broadcast_to(
                        jnp.sum(jnp.where(lane == hh * HEAD_DIM, lse_b, 0.0), axis=1, keepdims=True), (TQ, LANES))
                    cq_ref[t, half, :] = jnp.broadcast_to(
                        jnp.sum(jnp.where(lane == p * cs + hh, cblk, 0.0), axis=1, keepdims=True), (TQ, LANES))

        k = k_ref[...]
        v = v_ref[...]
        ck = jnp.concatenate([jnp.broadcast_to(ct_ref[0:1, :], (TQ, T)), jnp.broadcast_to(ct_ref[1:2, :], (TQ, T))], axis=0)
        row = lax.broadcasted_iota(jnp.int32, (2 * TQ, T), 0)
        qrow = jnp.where(row < TQ, row, row - TQ)
        kpos = j * T + lax.broadcasted_iota(jnp.int32, (2 * TQ, T), 1)

        def step(i, carry_):
            dk, dv, dc0, dc1 = carry_
            off = pl.multiple_of(i * TQ, 8)
            rows = pl.ds(off, TQ)
            qs = stack(q_ref[rows, :])
            dos = stack(do_ref[rows, :]).astype(BF16)
            allowed = (kpos <= i * TQ + qrow) & (kpos >= pad)
            s = _dot_nt(qs, k) + cq_ref[i] - ck
            pr = jnp.where(allowed, jnp.exp(jnp.where(allowed, s, NEG_INF) - lser_ref[i]), 0.0)
            ds = pr * (_dot_nt(dos, v) - dsum_ref[i])
            dsb = ds.astype(BF16)
            dv = dv + _dot_tn(pr.astype(BF16), dos)
            dk = dk + _dot_tn(dsb, qs)
            dqs = _dot(dsb, k)
            dq_ref[rows, :] += jnp.where(lo, dqs[:TQ], dqs[TQ:])
            dc0 = dc0 - jnp.sum(ds[:TQ], axis=0, keepdims=True)
            dc1 = dc1 - jnp.sum(ds[TQ:], axis=0, keepdims=True)
            dsacc_ref[i] += ds
            return dk, dv, dc0, dc1

        init = (jnp.zeros((T, LANES), F32), jnp.zeros((T, LANES), F32),
                jnp.zeros((1, T), F32), jnp.zeros((1, T), F32))
        dk, dv, dc0, dc1 = lax.fori_loop((j * T) // TQ, nq, step, init)
        dk_ref[...] = dk
        dv_ref[...] = dv
        dct_ref[...] = jnp.zeros_like(dct_ref)
        dct_ref[0:1, :] = dc0
        dct_ref[1:2, :] = dc1

        @pl.when(j == nb - 1)
        def _():
            for t in range(nq):
                upd = jnp.zeros((TQ, LANES), F32)
                for hh in range(2):
                    upd = upd + jnp.where(lane == p * cs + hh,
                                          jnp.sum(dsacc_ref[t, hh * TQ:(hh + 1) * TQ, :], axis=1, keepdims=True), 0.0)
                dcq_ref[t * TQ:(t + 1) * TQ, :] += upd

    blk = pl.BlockSpec((T, LANES), lambda p, j: (j, p))
    full = pl.BlockSpec((L, LANES), lambda p, j: (0, p))
    return _call(
        name, body, (npairs, nb),
        [full, blk, blk, pl.BlockSpec((L, LANES), lambda p, j: (0, 0)), pl.BlockSpec((cs, T), lambda p, j: (p, j)),
         full, full, full],
        [full, blk, blk, pl.BlockSpec((cs, T), lambda p, j: (p, j)), pl.BlockSpec((L, LANES), lambda p, j: (0, 0))],
        [jax.ShapeDtypeStruct((L, wf), F32)] * 3
        + [jax.ShapeDtypeStruct((LANES, L), F32), jax.ShapeDtypeStruct((L, LANES), F32)],
        [qn, kn, fv, c, ct, o, lse, do], scratch=[pltpu.VMEM((nq, 2 * TQ, LANES), F32)] * 4, carry=carry)


def _swa_band(i, pad):
    T = BLOCK
    t = lax.broadcasted_iota(jnp.int32, (T, 2 * T), 0)
    u = lax.broadcasted_iota(jnp.int32, (T, 2 * T), 1)
    dist = t - u + T
    allowed = (dist >= 0) & (dist < T) & ((i - 1) * T + u >= pad)
    return dist.astype(F32), allowed


def _swa_probs(s, dist, allowed, slope, sink):
    s = jnp.where(allowed, s - slope * dist, NEG_INF)
    m = jnp.maximum(jnp.max(s, axis=1, keepdims=True), sink)
    e = jnp.exp(s - m)
    es = jnp.exp(sink - m)
    den = jnp.sum(e, axis=1, keepdims=True) + es
    return e / den, es / den


def _swa_stack(x, group, lo):
    tiles = []
    for h in range(group):
        pair = x[:, (h // 2) * LANES:(h // 2 + 1) * LANES]
        tiles.append(jnp.where(lo if h % 2 == 0 else jnp.logical_not(lo), pair, jnp.zeros_like(pair)))
    return jnp.concatenate(tiles, axis=0)


def _swa_fwd(name, cfg, sqn, skd, svd, sinks, slopes, carry=None):
    L, ws, group, hkv = cfg.L, cfg.ws, cfg.group, cfg.hkv
    T = BLOCK
    gw = group * HEAD_DIM
    pad = cfg.pad

    def body(sink_ref, slope_ref, q_ref, kp_ref, kc_ref, vp_ref, vc_ref, o_ref):
        kv = pl.program_id(0)
        i = pl.program_id(1)
        lo = _lane((T, LANES)) < HEAD_DIM
        dist, allowed = _swa_band(i, pad)
        kb = jnp.concatenate([kp_ref[...], kc_ref[...]], axis=0)
        vb = jnp.concatenate([vp_ref[...], vc_ref[...]], axis=0)
        s = _dot_nt(_swa_stack(q_ref[...], group, lo), kb)
        probs = []
        for h in range(group):
            p, _ = _swa_probs(s[h * T:(h + 1) * T], dist, allowed, slope_ref[kv * group + h], sink_ref[kv * group + h])
            probs.append(p.astype(BF16))
        o = _dot(jnp.concatenate(probs, axis=0), vb)
        for pp in range(group // 2):
            o_ref[:, pp * LANES:(pp + 1) * LANES] = jnp.where(lo, o[2 * pp * T:(2 * pp + 1) * T],
                                                              o[(2 * pp + 1) * T:(2 * pp + 2) * T])

    smem = pl.BlockSpec(memory_space=pltpu.SMEM)
    prev = pl.BlockSpec((T, LANES), lambda kv, i: (jnp.maximum(i - 1, 0), kv))
    cur = pl.BlockSpec((T, LANES), lambda kv, i: (i, kv))
    qblk = pl.BlockSpec((T, gw), lambda kv, i: (i, kv))
    return _call(name, body, (hkv, L // T), [smem, smem, qblk, prev, cur, prev, cur], [qblk],
                 [jax.ShapeDtypeStruct((L, ws), F32)], [sinks, slopes, sqn, skd, skd, svd, svd], carry=carry)


def _swa_bwd(name, cfg, sqn, skd, svd, sinks, slopes, o, do, carry=None):
    L, ws, group, hkv = cfg.L, cfg.ws, cfg.group, cfg.hkv
    T = BLOCK
    gw = group * HEAD_DIM
    ppk = group // 2
    pad = cfg.pad

    def body(sink_ref, slope_ref, q_ref, kp_ref, kc_ref, vp_ref, vc_ref, o_ref, do_ref,
             dq_ref, dk_ref, dv_ref, dsink_ref):
        kv = pl.program_id(0)
        i = pl.program_id(1)
        lane = _lane((T, LANES))
        lo = lane < HEAD_DIM

        @pl.when(i == 0)
        def _():
            dk_ref[...] = jnp.zeros_like(dk_ref)
            dv_ref[...] = jnp.zeros_like(dv_ref)
            dsink_ref[...] = jnp.zeros_like(dsink_ref)

        dist, allowed = _swa_band(i, pad)
        kb = jnp.concatenate([kp_ref[...], kc_ref[...]], axis=0)
        vb = jnp.concatenate([vp_ref[...], vc_ref[...]], axis=0)
        dov = do_ref[...]
        qs = _swa_stack(q_ref[...], group, lo)
        dos = _swa_stack(dov, group, lo).astype(BF16)
        dd = dov * o_ref[...]
        s = _dot_nt(qs, kb)
        dp = _dot_nt(dos, vb)
        probs, dss = [], []
        for h in range(group):
            pair = dd[:, (h // 2) * LANES:(h // 2 + 1) * LANES]
            dsum = jnp.sum(jnp.where(lo if h % 2 == 0 else jnp.logical_not(lo), pair, 0.0), axis=1, keepdims=True)
            p, ps = _swa_probs(s[h * T:(h + 1) * T], dist, allowed, slope_ref[kv * group + h], sink_ref[kv * group + h])
            probs.append(p.astype(BF16))
            dss.append((p * (dp[h * T:(h + 1) * T] - dsum)).astype(BF16))
            dsink_ref[8 * (h // 2):8 * (h // 2) + 1, :] += jnp.where(lane[0:1, :] == (h % 2) * HEAD_DIM,
                                                                    -jnp.sum(ps * dsum), 0.0)
        ds = jnp.concatenate(dss, axis=0)
        dq = _dot(ds, kb)
        for pp in range(ppk):
            dq_ref[:, pp * LANES:(pp + 1) * LANES] = jnp.where(lo, dq[2 * pp * T:(2 * pp + 1) * T],
                                                               dq[(2 * pp + 1) * T:(2 * pp + 2) * T])
        dk = _dot_tn(ds, qs)
        dv = _dot_tn(jnp.concatenate(probs, axis=0), dos)
        cur = pl.multiple_of(i * T, T)
        dk_ref[pl.ds(cur, T), :] += dk[T:]
        dv_ref[pl.ds(cur, T), :] += dv[T:]

        @pl.when(i > 0)
        def _():
            prv = pl.multiple_of((i - 1) * T, T)
            dk_ref[pl.ds(prv, T), :] += dk[:T]
            dv_ref[pl.ds(prv, T), :] += dv[:T]

    smem = pl.BlockSpec(memory_space=pltpu.SMEM)
    prev = pl.BlockSpec((T, LANES), lambda kv, i: (jnp.maximum(i - 1, 0), kv))
    cur = pl.BlockSpec((T, LANES), lambda kv, i: (i, kv))
    qblk = pl.BlockSpec((T, gw), lambda kv, i: (i, kv))
    full = pl.BlockSpec((L, LANES), lambda kv, i: (0, kv))
    return _call(
        name, body, (hkv, L // T), [smem, smem, qblk, prev, cur, prev, cur, qblk, qblk],
        [qblk, full, full, pl.BlockSpec((8 * ppk, LANES), lambda kv, i: (kv, 0))],
        [jax.ShapeDtypeStruct((L, ws), F32), jax.ShapeDtypeStruct((L, 2 * LANES), F32),
         jax.ShapeDtypeStruct((L, 2 * LANES), F32), jax.ShapeDtypeStruct((8 * ppk * hkv, LANES), F32)],
        [sinks, slopes, sqn, skd, skd, svd, svd, o, do], carry=carry)


def _out_norm(name, cfg, o_fox, o_swa, g_fox, g_swa):
    L, wf, ws = cfg.L, cfg.wf, cfg.ws
    tm = _tile(L, 256, 16)

    def body(of_ref, os_ref, gf_ref, gs_ref, o_ref):
        for src, g_ref, lo_, w in ((of_ref, gf_ref, 0, wf), (os_ref, gs_ref, wf, ws)):
            x = src[...]
            r = lax.rsqrt(jnp.mean(x * x, axis=1, keepdims=True) + EPS)
            o_ref[:, lo_:lo_ + w] = (x * r * g_ref[...]).astype(BF16)

    return pl.pallas_call(
        body, name=name, grid=(L // tm,),
        in_specs=[pl.BlockSpec((tm, wf), lambda i: (i, 0)), pl.BlockSpec((tm, ws), lambda i: (i, 0)),
                  pl.BlockSpec((1, wf), lambda i: (0, 0)), pl.BlockSpec((1, ws), lambda i: (0, 0))],
        out_specs=pl.BlockSpec((tm, wf + ws), lambda i: (i, 0)),
        out_shape=jax.ShapeDtypeStruct((L, wf + ws), BF16),
        compiler_params=_cparams(("parallel",)),
    )(o_fox, o_swa, g_fox, g_swa)


def _out_norm_bwd(name, cfg, dcat, o_fox, o_swa, g_fox, g_swa):
    L, wf, ws = cfg.L, cfg.wf, cfg.ws
    tm = _tile(L, 256, 16)

    def body(d_ref, of_ref, os_ref, gf_ref, gs_ref, dof_ref, dos_ref, dgf_ref, dgs_ref):
        i = pl.program_id(0)

        @pl.when(i == 0)
        def _():
            dgf_ref[...] = jnp.zeros_like(dgf_ref)
            dgs_ref[...] = jnp.zeros_like(dgs_ref)

        for src, g_ref, dst, dg_ref, lo_, w in ((of_ref, gf_ref, dof_ref, dgf_ref, 0, wf),
                                                (os_ref, gs_ref, dos_ref, dgs_ref, wf, ws)):
            x = src[...]
            dy = d_ref[:, lo_:lo_ + w]
            r = lax.rsqrt(jnp.mean(x * x, axis=1, keepdims=True) + EPS)
            wv = dy * g_ref[...]
            proj = jnp.sum(wv * x, axis=1, keepdims=True) * (1.0 / w)
            dst[...] = r * wv - x * (r * r * r * proj)
            dg_ref[...] += jnp.sum(dy * x * r, axis=0, keepdims=True)

    return pl.pallas_call(
        body, name=name, grid=(L // tm,),
        in_specs=[pl.BlockSpec((tm, wf + ws), lambda i: (i, 0)), pl.BlockSpec((tm, wf), lambda i: (i, 0)),
                  pl.BlockSpec((tm, ws), lambda i: (i, 0)),
                  pl.BlockSpec((1, wf), lambda i: (0, 0)), pl.BlockSpec((1, ws), lambda i: (0, 0))],
        out_specs=[pl.BlockSpec((tm, wf), lambda i: (i, 0)), pl.BlockSpec((tm, ws), lambda i: (i, 0)),
                   pl.BlockSpec((1, wf), lambda i: (0, 0)), pl.BlockSpec((1, ws), lambda i: (0, 0))],
        out_shape=[jax.ShapeDtypeStruct((L, wf), F32), jax.ShapeDtypeStruct((L, ws), F32),
                   jax.ShapeDtypeStruct((1, wf), F32), jax.ShapeDtypeStruct((1, ws), F32)],
        compiler_params=_cparams(("arbitrary",)),
    )(dcat, o_fox, o_swa, g_fox, g_swa)


def _loss_head(name, h, target):
    L, D = h.shape
    T = BLOCK

    def body(h_ref, t_ref, loss_ref, dh_ref, dh16_ref):
        i = pl.program_id(0)

        @pl.when(i == 0)
        def _():
            loss_ref[...] = jnp.zeros_like(loss_ref)
            dh_ref[...] = jnp.zeros_like(dh_ref)
            dh16_ref[...] = jnp.zeros_like(dh16_ref)

        @pl.when(i > 0)
        def _():
            err = h_ref[...] - t_ref[...]
            dh = err * (1.0 / D)
            dh_ref[...] = dh
            dh16_ref[...] = dh.astype(BF16)
            loss_ref[...] += jnp.sum(err * err) * (0.5 / D)

    return pl.pallas_call(
        body, name=name, grid=(L // T,),
        in_specs=[pl.BlockSpec((T, D), lambda i: (i, 0)), pl.BlockSpec((T, D), lambda i: (jnp.maximum(i - 1, 0), 0))],
        out_specs=[pl.BlockSpec((8, LANES), lambda i: (0, 0)), pl.BlockSpec((T, D), lambda i: (i, 0)),
                   pl.BlockSpec((T, D), lambda i: (i, 0))],
        out_shape=[jax.ShapeDtypeStruct((8, LANES), F32), jax.ShapeDtypeStruct((L, D), F32),
                   jax.ShapeDtypeStruct((L, D), BF16)],
        compiler_params=_cparams(("arbitrary",)),
    )(h, target)


def _pair_add(name, grad, landed, parts, layer, core):
    _, _, r, C = grad.shape

    def body(s_ref, g_ref, l_ref, p_ref, o_ref):
        o_ref[...] = (g_ref[...].astype(F32) + l_ref[...].astype(F32)).astype(o_ref.dtype)

    return pl.pallas_call(
        body, name=name,
        grid_spec=pltpu.PrefetchScalarGridSpec(
            num_scalar_prefetch=1, grid=(4,),
            in_specs=[pl.BlockSpec((None, None, r, C), lambda k, s: (k, s[0], 0, 0)),
                      pl.BlockSpec((None, r, C), lambda k, s: (k, 0, 0)),
                      pl.BlockSpec(memory_space=pl.ANY)],
            out_specs=pl.BlockSpec((None, None, r, C), lambda k, s: (layer, k, 0, 0))),
        out_shape=jax.ShapeDtypeStruct(parts.shape, parts.dtype),
        input_output_aliases={3: 0},
        compiler_params=_cparams(("arbitrary",)),
    )(core, grad, landed, parts)


def _chip_sum(name, part, landed, chip, layer_axis):
    A, _, r, C = part.shape

    def body(s_ref, p_ref, l0_ref, l1_ref, l2_ref, o_ref):
        o_ref[...] = ((p_ref[...].astype(F32) + l0_ref[...].astype(F32))
                      + (l1_ref[...].astype(F32) + l2_ref[...].astype(F32)))

    def land(k):
        return pl.BlockSpec((None, None, r, C), lambda a, s, k=k: (k, a, 0, 0))

    if layer_axis == 0:
        out_spec, out_shape = pl.BlockSpec((None, r, C), lambda a, s: (a, 0, 0)), (A, r, C)
    else:
        out_spec, out_shape = pl.BlockSpec((r, C), lambda a, s: (0, a)), (r, A * C)
    out = pl.pallas_call(
        body, name=name,
        grid_spec=pltpu.PrefetchScalarGridSpec(
            num_scalar_prefetch=1, grid=(A,),
            in_specs=[pl.BlockSpec((None, None, r, C), lambda a, s: (a, s[0], 0, 0)), land(0), land(1), land(2)],
            out_specs=out_spec),
        out_shape=jax.ShapeDtypeStruct(out_shape, F32),
        compiler_params=_cparams(("parallel",)),
    )(chip, part, landed, landed, landed)
    return out if layer_axis == 0 else out.reshape(r, A, C)


def _gather_sum(name, v):
    R = v.shape[0]

    def body(x_ref, o_ref, buf_ref, send_sems, recv_sems):
        x, y, c = _place()
        me, sibling = (x, y, c), (x, y, 1 - c)
        chips = [(1 - x, y), (x, 1 - y), (1 - x, 1 - y)]

        def rows(dev):
            px, py, pc = dev
            return buf_ref.at[4 * px + 2 * py + pc]

        def copy(k, block, to, src=None):
            return pltpu.make_async_remote_copy(
                src_ref=rows(block) if src is None else src, dst_ref=rows(block),
                send_sem=send_sems.at[k], recv_sem=recv_sems.at[k], device_id=to, device_id_type=MESH)

        first = [copy(0, me, sibling, src=x_ref)]
        first += [copy(1 + j, me, (*chip, c), src=x_ref) for j, chip in enumerate(chips)]
        for cp in first:
            cp.start()
        rows(me)[...] = x_ref[...]
        passed = [copy(4 + j, (*chip, c), sibling) for j, chip in enumerate(chips)]
        for j, chip in enumerate(chips):
            copy(1 + j, (*chip, c), me).wait_recv()
            passed[j].start()
        copy(0, sibling, me).wait_recv()
        for j, chip in enumerate(chips):
            copy(4 + j, (*chip, 1 - c), me).wait_recv()
        for cp in first + passed:
            cp.wait_send()
        acc = buf_ref[0]
        for d in range(1, N_DEV):
            acc = acc + buf_ref[d]
        o_ref[...] = acc

    vm = pl.BlockSpec(memory_space=pltpu.VMEM)
    return pl.pallas_call(
        body, name=name, in_specs=[vm], out_specs=vm,
        out_shape=jax.ShapeDtypeStruct((R, LANES), F32),
        scratch_shapes=[pltpu.VMEM((N_DEV, R, LANES), F32), pltpu.SemaphoreType.DMA((7,)), pltpu.SemaphoreType.DMA((7,))],
    )(v)


def _adamw(name, g, w, m, v):
    A, R, C = g.shape
    budget = 1 << 18
    tr = _tile(R, max(8, budget // C // 8 * 8), 8)
    ta = _tile(A, max(1, budget // (tr * C)), 1)

    def body(g_ref, w_ref, m_ref, v_ref, d_ref, nm_ref, nv_ref):
        gv = g_ref[...]
        nm = ADAM_B1 * m_ref[...] + (1.0 - ADAM_B1) * gv
        nv = ADAM_B2 * v_ref[...] + (1.0 - ADAM_B2) * (gv * gv)
        m_hat = nm / (1.0 - ADAM_B1 ** ADAM_STEP)
        v_hat = nv / (1.0 - ADAM_B2 ** ADAM_STEP)
        d_ref[...] = -ADAM_LR * (m_hat / (jnp.sqrt(v_hat) + ADAM_EPS) + ADAM_WD * w_ref[...])
        nm_ref[...] = nm
        nv_ref[...] = nv

    blk = pl.BlockSpec((ta, tr, C), lambda i, j: (i, j, 0))
    shp = jax.ShapeDtypeStruct((A, R, C), F32)
    return pl.pallas_call(
        body, name=name, grid=(A // ta, R // tr),
        in_specs=[blk] * 4, out_specs=[blk] * 3, out_shape=[shp] * 3,
        compiler_params=_cparams(("parallel", "parallel")),
    )(g, w, m, v)


def _adamw_nd(name, g, w, m, v):
    shape = w.shape
    three = (1,) + shape if len(shape) == 2 else shape
    return tuple(o.reshape(shape) for o in _adamw(name, *[a.reshape(three) for a in (g, w, m, v)]))


def _scatter_heads(cfg, vals):
    v = jnp.pad(vals.reshape(cfg.hf // 2, 2), ((0, 0), (0, cfg.cs - 2)))
    return v.reshape(1, LANES)


def _gather_heads(cfg, row):
    return row.reshape(cfg.hf // 2, cfg.cs)[:, :2].reshape(cfg.hf)


def _permute_w_in(cfg, w_in_t):
    wf, hf = cfg.wf, cfg.hf
    o = 3 * wf
    cols = w_in_t.shape[1]
    fz = w_in_t[o:o + hf].reshape(hf // 2, 2, cols)
    fz_blk = jnp.pad(fz, ((0, 0), (0, cfg.cs - 2), (0, 0))).reshape(LANES, cols)
    return jnp.concatenate([w_in_t[:o], w_in_t[o + hf:], fz_blk], axis=0)


def _unpermute_dw_in(cfg, dwp):
    wf, hf = cfg.wf, cfg.hf
    o = 3 * wf
    cols = dwp.shape[1]
    fz = dwp[cfg.o_fz:].reshape(hf // 2, cfg.cs, cols)[:, :2].reshape(hf, cols)
    return jnp.concatenate([dwp[:o], fz, dwp[o:cfg.o_fz]], axis=0)


def _pair_gain(g):
    return jnp.tile(g, 2)[None]


def _fold_pair(dg):
    return dg[0, :HEAD_DIM] + dg[0, HEAD_DIM:]


def kernel(x, meta_tokens, ffn1_norm, ffn1_w_gate, ffn1_w_up, ffn1_w_down, mix_norm, w_in, b_forget, fox_q_norm, fox_k_norm, swa_q_norm, swa_k_norm, swa_sinks, fox_out_norm, swa_out_norm, w_out, ffn2_norm, ffn2_w_gate, ffn2_w_up, ffn2_w_down, loss_target, m_meta_tokens, m_ffn1_norm, m_ffn1_w_gate, m_ffn1_w_up, m_ffn1_w_down, m_mix_norm, m_w_in, m_b_forget, m_fox_q_norm, m_fox_k_norm, m_swa_q_norm, m_swa_k_norm, m_swa_sinks, m_fox_out_norm, m_swa_out_norm, m_w_out, m_ffn2_norm, m_ffn2_w_gate, m_ffn2_w_up, m_ffn2_w_down, v_meta_tokens, v_ffn1_norm, v_ffn1_w_gate, v_ffn1_w_up, v_ffn1_w_down, v_mix_norm, v_w_in, v_b_forget, v_fox_q_norm, v_fox_k_norm, v_swa_q_norm, v_swa_k_norm, v_swa_sinks, v_fox_out_norm, v_swa_out_norm, v_w_out, v_ffn2_norm, v_ffn2_w_gate, v_ffn2_w_up, v_ffn2_w_down):
    weights = dict(meta_tokens=meta_tokens, ffn1_norm=ffn1_norm, ffn1_w_gate=ffn1_w_gate, ffn1_w_up=ffn1_w_up,
                   ffn1_w_down=ffn1_w_down, mix_norm=mix_norm, w_in=w_in, b_forget=b_forget, fox_q_norm=fox_q_norm,
                   fox_k_norm=fox_k_norm, swa_q_norm=swa_q_norm, swa_k_norm=swa_k_norm, swa_sinks=swa_sinks,
                   fox_out_norm=fox_out_norm, swa_out_norm=swa_out_norm, w_out=w_out, ffn2_norm=ffn2_norm,
                   ffn2_w_gate=ffn2_w_gate, ffn2_w_up=ffn2_w_up, ffn2_w_down=ffn2_w_down)
    mom_m = dict(meta_tokens=m_meta_tokens, ffn1_norm=m_ffn1_norm, ffn1_w_gate=m_ffn1_w_gate, ffn1_w_up=m_ffn1_w_up,
                 ffn1_w_down=m_ffn1_w_down, mix_norm=m_mix_norm, w_in=m_w_in, b_forget=m_b_forget,
                 fox_q_norm=m_fox_q_norm, fox_k_norm=m_fox_k_norm, swa_q_norm=m_swa_q_norm, swa_k_norm=m_swa_k_norm,
                 swa_sinks=m_swa_sinks, fox_out_norm=m_fox_out_norm, swa_out_norm=m_swa_out_norm, w_out=m_w_out,
                 ffn2_norm=m_ffn2_norm, ffn2_w_gate=m_ffn2_w_gate, ffn2_w_up=m_ffn2_w_up, ffn2_w_down=m_ffn2_w_down)
    mom_v = dict(meta_tokens=v_meta_tokens, ffn1_norm=v_ffn1_norm, ffn1_w_gate=v_ffn1_w_gate, ffn1_w_up=v_ffn1_w_up,
                 ffn1_w_down=v_ffn1_w_down, mix_norm=v_mix_norm, w_in=v_w_in, b_forget=v_b_forget,
                 fox_q_norm=v_fox_q_norm, fox_k_norm=v_fox_k_norm, swa_q_norm=v_swa_q_norm, swa_k_norm=v_swa_k_norm,
                 swa_sinks=v_swa_sinks, fox_out_norm=v_fox_out_norm, swa_out_norm=v_swa_out_norm, w_out=v_w_out,
                 ffn2_norm=v_ffn2_norm, ffn2_w_gate=v_ffn2_w_gate, ffn2_w_up=v_ffn2_w_up, ffn2_w_down=v_ffn2_w_down)
    names = list(weights)

    _, S, D = x.shape
    depth = ffn1_norm.shape[0]
    hf, hs = b_forget.shape[1], swa_sinks.shape[1]
    U = w_in.shape[2] * N_DEV
    hkv = (U - 3 * HEAD_DIM * hf - hf - HEAD_DIM * hs) // (2 * HEAD_DIM)
    cfg = _Cfg(S, D, hf, hs, hkv)
    n_meta = meta_tokens.shape[0]
    assert n_meta == cfg.n_meta
    x_idx, y_idx, c_idx = _place()
    chip_idx = 2 * x_idx + y_idx
    dev_idx = 2 * chip_idx + c_idx
    core_s = jnp.reshape(c_idx, (1,)).astype(jnp.int32)
    chip_s = jnp.reshape(chip_idx, (1,)).astype(jnp.int32)

    col_sharded = ("ffn1_w_gate", "ffn1_w_up", "w_in", "ffn2_w_gate", "ffn2_w_up")
    use_order = ("ffn1_w_gate", "ffn1_w_up", "ffn1_w_down", "w_in", "w_out", "ffn2_w_gate", "ffn2_w_up", "ffn2_w_down")

    layer_axis = {k: (1 if k == "w_in" else 0) for k in use_order}

    def rows_view(k, a):
        if k not in col_sharded:
            return a
        return jnp.transpose(a, (2, 0, 1)) if k == "w_in" else jnp.swapaxes(a, 1, 2)

    def from_rows_view(k, a):
        if k not in col_sharded:
            return a
        return jnp.transpose(a, (1, 2, 0)) if k == "w_in" else jnp.swapaxes(a, 1, 2)

    w_rows = {k: rows_view(k, weights[k]) for k in use_order}

    def shard(key):
        k, l = key
        if k == "meta_tokens":
            return meta_tokens
        return _shard_bf16("weight_shard", w_rows[k], l, layer_axis[k] if k in col_sharded else 0)

    waiting = [("meta_tokens", 0)] + [(k, l) for l in range(depth) for k in use_order]
    halfway = []
    gathered = {}

    def fwd_carry(n_first):
        cy = _Carry()
        second = [(key, _gather_second(cy, buf)) for key, buf in halfway]
        first = [(key, _gather_first(cy, shard(key))) for key in waiting[:n_first]]
        del waiting[:n_first]
        halfway.clear()
        return cy, (first, second)

    def fwd_absorb(extra, plan):
        first, second = plan
        for key, idx in second:
            gathered[key] = extra[idx]
        for key, idx in first:
            halfway.append((key, extra[idx]))

    def weight(k, l):
        key = (k, l)
        while key not in gathered:
            n = 0 if any(key == hk for hk, _ in halfway) else waiting.index(key) + 1
            cy, plan = fwd_carry(n)
            fwd_absorb(_comm_only("weights_gather", cy), plan)
        g = gathered[key]
        return g.reshape(-1, g.shape[-1])

    weight("ffn1_w_down", 0)
    meta_full = jnp.swapaxes(weight("meta_tokens", 0).reshape(N_DEV, n_meta, -1), 0, 1).reshape(n_meta, D)
    slopes = jnp.asarray(2.0 ** (-8.0 * np.arange(1, hs + 1) / hs), dtype=F32)

    h = jnp.concatenate([jnp.zeros((cfg.pad, D), F32), meta_full, x[0]], axis=0)
    saved = []
    w_in_p = [None] * depth

    def mm_f(name, a, b, n_first=1, **kw):
        cy, plan = fwd_carry(n_first)
        out, extra = _matmul(name, a, b, carry=cy, **kw)
        fwd_absorb(extra, plan)
        return out

    def ffn_fwd(tag, l, h_in, norm, wg, wu, wd):
        xn = _rmsnorm_fwd(f"{tag}_norm", h_in, norm[l][None])
        wg_t, wu_t = weight(wg, l), weight(wu, l)
        cy, plan = fwd_carry(1)
        (gate, up, act), extra = _ffn_up(f"{tag}_up", xn, wg_t, wu_t, carry=cy)
        fwd_absorb(extra, plan)
        h_out = mm_f(f"{tag}_down", act, weight(wd, l), scale=0.5, residual=h_in, tm=544, tn=1024, tk=2816)
        return h_out, (xn, gate, up)

    for l in range(depth):
        st = {"h0": h}
        h, st["ffn1"] = ffn_fwd("ffn1", l, h, ffn1_norm, "ffn1_w_gate", "ffn1_w_up", "ffn1_w_down")
        st["h1"] = h
        xn = _rmsnorm_fwd("mix_norm", h, mix_norm[l][None])
        w_in_p[l] = _permute_w_in(cfg, weight("w_in", l))
        u = mm_f("mix_in", xn, w_in_p[l], trans_b=True, tm=544, tn=640, tk=2048)
        gq, gk = _pair_gain(fox_q_norm[l]), _pair_gain(fox_k_norm[l])
        gsq, gsk = _pair_gain(swa_q_norm[l]), _pair_gain(swa_k_norm[l])
        bias = _scatter_heads(cfg, b_forget[l])
        qn, kn, fv, sqn, skd, svd, c, ct = _mix_prep("mix_prep", cfg, u, gq, gk, gsq, gsk, bias)
        cy, plan = fwd_carry(2)
        (o_fox, lse), extra = _fox_fwd("fox_fwd", cfg, qn, kn, fv, c, ct, carry=cy)
        fwd_absorb(extra, plan)
        cy, plan = fwd_carry(1)
        (o_swa,), extra = _swa_fwd("swa_fwd", cfg, sqn, skd, svd, swa_sinks[l], slopes, carry=cy)
        fwd_absorb(extra, plan)
        o_cat = _out_norm("out_norm", cfg, o_fox, o_swa, fox_out_norm[l][None], swa_out_norm[l][None])
        st["mix"] = (xn, u, gq, gk, gsq, gsk, bias, qn, kn, fv, sqn, skd, svd, c, ct, o_fox, lse, o_swa, o_cat)
        h = mm_f("mix_out", o_cat, weight("w_out", l), n_first=0, residual=h, tm=544, tn=1024, tk=2048)
        st["h2"] = h
        h, st["ffn2"] = ffn_fwd("ffn2", l, h, ffn2_norm, "ffn2_w_gate", "ffn2_w_up", "ffn2_w_down")
        saved.append(st)

    loss_blk, dh, dh16 = _loss_head("loss_head", h, loss_target[0])

    small = {k: [None] * depth for k in names if k not in use_order and k != "meta_tokens"}
    parts, landing = {}, {}
    to_sibling, to_chips = [], []

    def bwd_carry(n_chips):
        cy = _Carry()
        t1, t3 = [], []
        for k, l in list(to_chips):
            if len(t3) < n_chips and all(k != k3 for k3, _ in t3):
                to_chips.remove((k, l))
                t3.append((k, _scatter_chips(cy, parts[k], landing[k], l)))
        while to_sibling:
            k, l, g = to_sibling.pop(0)
            t1.append((k, l, g, _scatter_sibling(cy, g)))
        return cy, (t1, t3)

    def bwd_absorb(extra, plan):
        t1, t3 = plan
        for k, idx in t3:
            landing[k] = extra[idx]
        for k, l, g, idx in t1:
            parts[k] = _pair_add("grads_pair_add", g, extra[idx], parts[k], l, core_s)
            to_chips.append((k, l))

    def emit_grad(k, l, dw):
        r, C = dw.shape[0] // N_DEV, dw.shape[1]
        if k not in parts:
            parts[k] = lax.empty((depth, 4, r, C), BF16)
            landing[k] = lax.empty((3, depth, r, C), BF16)
        to_sibling.append((k, l, dw.reshape(4, 2, r, C)))

    def mm_b(name, a, b, n_chips=0, **kw):
        cy, plan = bwd_carry(n_chips)
        out, extra = _matmul(name, a, b, carry=cy, **kw)
        bwd_absorb(extra, plan)
        return out

    def ffn_bwd(tag, l, dh_out, dh_out16, h_in, st_, norm, wg, wu, wd):
        xn, gate, up = st_
        short = 1 if l == 0 else 0
        cy, plan = bwd_carry(1)
        (dgate, dup, act), extra = _ffn_bwd_act(f"{tag}_dact", dh_out16, weight(wd, l), gate, up, carry=cy)
        bwd_absorb(extra, plan)
        emit_grad(wd, l, mm_b(f"{tag}_dwd", act, dh_out16, short, trans_a=True, scale=0.5, out_dtype=BF16,
                              tm=512, tn=1024, tk=2176))
        emit_grad(wg, l, mm_b(f"{tag}_dwg", dgate, xn, short, trans_a=True, out_dtype=BF16, tm=512, tn=1024, tk=2176))
        emit_grad(wu, l, mm_b(f"{tag}_dwu", dup, xn, short, trans_a=True, out_dtype=BF16, tm=512, tn=1024, tk=2176))
        dxn = mm_b(f"{tag}_dxn", dgate, weight(wg, l), 1, pair2=(dup, weight(wu, l)), tm=544, tn=1024, tk=1408)
        dh_in, dh_in16, dg = _rmsnorm_bwd(f"{tag}_dnorm", dxn, h_in, norm[l][None], dh_out)
        return dh_in, dh_in16, dg[0]

    for l in reversed(range(depth)):
        st = saved[l]
        dh, dh16, small["ffn2_norm"][l] = ffn_bwd("ffn2", l, dh, dh16, st["h2"], st["ffn2"], ffn2_norm,
                                                   "ffn2_w_gate", "ffn2_w_up", "ffn2_w_down")
        xn, u, gq, gk, gsq, gsk, bias, qn, kn, fv, sqn, skd, svd, c, ct, o_fox, lse, o_swa, o_cat = st["mix"]
        dcat = mm_b("mix_dcat", dh16, weight("w_out", l), int(l == 0), trans_b=True, tm=544, tn=1024, tk=2048)
        emit_grad("w_out", l, mm_b("mix_dwout", o_cat, dh16, int(l == 0), trans_a=True, out_dtype=BF16, tm=512, tn=1024, tk=2176))
        do_fox, do_swa, dgf, dgs = _out_norm_bwd("out_norm_bwd", cfg, dcat, o_fox, o_swa,
                                                 fox_out_norm[l][None], swa_out_norm[l][None])
        small["fox_out_norm"][l], small["swa_out_norm"][l] = dgf[0], dgs[0]
        cy, plan = bwd_carry(3)
        (dqn, dkn, dfv, dct, dcq), extra = _fox_bwd("fox_bwd", cfg, qn, kn, fv, c, ct, o_fox, lse, do_fox, carry=cy)
        bwd_absorb(extra, plan)
        cy, plan = bwd_carry(1)
        (dsqn, dskd, dsvd, dsink), extra = _swa_bwd("swa_bwd", cfg, sqn, skd, svd, swa_sinks[l], slopes, o_swa, do_swa,
                                                    carry=cy)
        bwd_absorb(extra, plan)
        small["swa_sinks"][l] = dsink.reshape(hs // 2, 8, LANES)[:, 0, ::HEAD_DIM].reshape(hs)
        du, dgq, dgk, dgsq, dgsk, db = _mix_prep_bwd("mix_prep_bwd", cfg, u, gq, gk, gsq, gsk, bias,
                                                     dqn, dkn, dfv, dsqn, dskd, dsvd, dct, dcq)
        small["fox_q_norm"][l], small["fox_k_norm"][l] = _fold_pair(dgq), _fold_pair(dgk)
        small["swa_q_norm"][l], small["swa_k_norm"][l] = _fold_pair(dgsq), _fold_pair(dgsk)
        small["b_forget"][l] = _gather_heads(cfg, db[0])
        dwp = mm_b("mix_dwin", du, xn, int(l == 0), trans_a=True, out_dtype=BF16, tm=640, tn=1024, tk=2176)
        emit_grad("w_in", l, _unpermute_dw_in(cfg, dwp))
        dxn = mm_b("mix_dxn", du, w_in_p[l], int(l == 0), tm=544, tn=1024, tk=4480)
        dh, dh16, dg = _rmsnorm_bwd("mix_dnorm", dxn, st["h1"], mix_norm[l][None], dh)
        small["mix_norm"][l] = dg[0]
        dh, dh16, small["ffn1_norm"][l] = ffn_bwd("ffn1", l, dh, dh16, st["h0"], st["ffn1"], ffn1_norm,
                                                   "ffn1_w_gate", "ffn1_w_up", "ffn1_w_down")

    grad_x = dh[BLOCK:][None]
    dmeta = dh[cfg.pad:BLOCK]

    while to_sibling or to_chips:
        cy, plan = bwd_carry(len(use_order))
        bwd_absorb(_comm_only("grads_scatter", cy), plan)

    grads = {}
    for k in use_order:
        grads[k] = _chip_sum("grads_chip_sum", parts[k], landing[k], chip_s, layer_axis[k] if k in col_sharded else 0)

    small_names = list(small)
    pieces = [loss_blk[0, :1], dmeta.reshape(-1)] + [jnp.stack(small[k]).reshape(-1) for k in small_names]
    sizes = [int(p.shape[0]) for p in pieces]
    total = sum(sizes)
    padded = -(-total // (8 * LANES)) * (8 * LANES)
    vec = jnp.concatenate(pieces + [jnp.zeros((padded - total,), F32)]).reshape(-1, LANES)
    summed = _gather_sum("small_gather_sum", vec).reshape(-1)
    offs = np.cumsum([0] + sizes)
    loss = summed[0]
    dmeta_full = summed[offs[1]:offs[2]].reshape(n_meta, D)
    mcols = meta_tokens.shape[1]
    grads["meta_tokens"] = lax.dynamic_slice_in_dim(dmeta_full, dev_idx * mcols, mcols, axis=1)
    for n_, k in enumerate(small_names):
        grads[k] = summed[offs[2 + n_]:offs[3 + n_]].reshape(weights[k].shape)

    delta, new_m, new_v = {}, {}, {}
    for k in names:
        if k in col_sharded:
            outs = _adamw_nd("adamw", grads[k], w_rows[k], rows_view(k, mom_m[k]), rows_view(k, mom_v[k]))
            delta[k], new_m[k], new_v[k] = (from_rows_view(k, o) for o in outs)
            grads[k] = from_rows_view(k, grads[k])
        else:
            delta[k], new_m[k], new_v[k] = _adamw_nd("adamw", grads[k], weights[k], mom_m[k], mom_v[k])

    return (loss, grad_x, *[grads[k] for k in names], *[delta[k] for k in names],
            *[new_m[k] for k in names], *[new_v[k] for k in names])
```

```python
import functools

import numpy as np
import jax
import jax.numpy as jnp
from jax import lax
from jax.experimental import pallas as pl
from jax.experimental.pallas import tpu as pltpu

F32 = jnp.float32
BF16 = jnp.bfloat16
MESH = pl.DeviceIdType.MESH

HEAD_DIM = 64
BLOCK = 128
LANES = 128
N_DEV = 8
EPS = 1e-6
NEG_INF = -1e30
SCALE = HEAD_DIM ** -0.5

ADAM_LR = 0.001
ADAM_B1 = 0.9
ADAM_B2 = 0.999
ADAM_EPS = 1e-08
ADAM_WD = 0.01
ADAM_STEP = 10

VMEM_BYTES_V7X = 64 * 1024 * 1024
VMEM_LIMIT = VMEM_BYTES_V7X * 3 // 4

NT = (((1,), (1,)), ((), ()))
TN = (((0,), (0,)), ((), ()))
HI = lax.Precision.HIGHEST


def _cparams(sem=None, vmem=VMEM_LIMIT):
    return pltpu.CompilerParams(dimension_semantics=sem, vmem_limit_bytes=vmem)


def _tile(n, pref, mult):
    best = None
    for t in range(mult, min(n, pref) + 1, mult):
        if n % t == 0:
            best = t
    return best if best is not None else n


def _dot(a, b):
    return jnp.dot(a, b, preferred_element_type=F32)


def _dot_nt(a, b):
    return lax.dot_general(a, b, NT, preferred_element_type=F32)


def _dot_tn(a, b):
    return lax.dot_general(a, b, TN, preferred_element_type=F32)


def _lane(shape):
    return lax.broadcasted_iota(jnp.int32, shape, len(shape) - 1)


def _half_sum(x, lo):
    s0 = jnp.sum(jnp.where(lo, x, 0.0), axis=1, keepdims=True)
    s1 = jnp.sum(jnp.where(lo, 0.0, x), axis=1, keepdims=True)
    return jnp.where(lo, s0, s1)


def _sigmoid(x):
    return 1.0 / (1.0 + jnp.exp(-x))


def _place():
    return lax.axis_index("x"), lax.axis_index("y"), lax.axis_index("c")


def _peers(x, y, c):
    return [(x, y, 1 - c), (1 - x, y, c), (x, 1 - y, c), (1 - x, 1 - y, c)]


def _dev_index(dev):
    px, py, pc = dev
    return 4 * px + 2 * py + pc


class _Carry:
    def __init__(self):
        self.ins, self.outs, self.alias, self.items = [], [], {}, []
        self.nsem = self.nloc = 0

    def add(self, ins, outs, alias, nsem, nloc, build):
        i0, o0 = len(self.ins), len(self.outs)
        for src, dst in alias.items():
            self.alias[i0 + src] = o0 + dst
        self.items.append((i0, len(ins), o0, len(outs), self.nsem, self.nloc, build))
        self.ins += ins
        self.outs += outs
        self.nsem += nsem
        self.nloc += nloc
        return list(range(o0, o0 + len(outs)))

    def build(self, in_refs, out_refs, ssem, rsem, lsem):
        ops = []
        for i0, ni, o0, no, s0, l0, fn in self.items:
            ops.append(fn(in_refs[i0:i0 + ni], out_refs[o0:o0 + no],
                          lambda k, s0=s0: (ssem.at[s0 + k], rsem.at[s0 + k]), lambda k, l0=l0: lsem.at[l0 + k]))
        return ops


def _remote(src, dst, sems, dev):
    return pltpu.make_async_remote_copy(src_ref=src, dst_ref=dst, send_sem=sems[0], recv_sem=sems[1],
                                        device_id=dev, device_id_type=MESH)


def _gather_first(carry, shard):
    def build(ins, outs, sems, locs):
        src, buf = ins[0], outs[0]
        x, y, c = _place()
        me = _dev_index((x, y, c))
        peers = _peers(x, y, c)[:3]
        local = pltpu.make_async_copy(src, buf.at[me], locs(0))
        sends = [_remote(src, buf.at[me], sems(k), dev) for k, dev in enumerate(peers)]
        recvs = [_remote(src, buf.at[_dev_index(dev)], sems(k), dev) for k, dev in enumerate(peers)]

        def start():
            local.start()
            for cp in sends:
                cp.start()

        def wait():
            for cp in sends:
                cp.wait_send()
            for cp in recvs:
                cp.wait_recv()
            local.wait()

        return start, wait

    return carry.add([shard], [jax.ShapeDtypeStruct((N_DEV,) + shard.shape, shard.dtype)], {}, 3, 1, build)[0]


def _gather_second(carry, buf):
    def build(ins, outs, sems, locs):
        b = outs[0]
        x, y, c = _place()
        sibling, xn, yn, diag = _peers(x, y, c)
        relay_src = _dev_index((1 - x, y, c)) * (1 - c) + _dev_index((x, 1 - y, c)) * c
        relay_to = (x * (1 - c) + (1 - x) * c, (1 - y) * (1 - c) + y * c, c)
        sends = [_remote(b.at[relay_src], b.at[relay_src], sems(0), relay_to)]
        recvs = [_remote(b.at[relay_src], b.at[_dev_index(diag)], sems(0), relay_to)]
        for k, dev in enumerate((xn, yn)):
            sends.append(_remote(b.at[_dev_index(dev)], b.at[_dev_index(dev)], sems(1 + k), sibling))
            recvs.append(_remote(b.at[_dev_index(dev)], b.at[_dev_index((dev[0], dev[1], 1 - c))], sems(1 + k), sibling))

        def start():
            for cp in sends:
                cp.start()

        def wait():
            for cp in sends:
                cp.wait_send()
            for cp in recvs:
                cp.wait_recv()

        return start, wait

    return carry.add([buf], [jax.ShapeDtypeStruct(buf.shape, buf.dtype)], {0: 0}, 3, 0, build)[0]


def _gather_third(carry, buf):
    def build(ins, outs, sems, locs):
        b = outs[0]
        x, y, c = _place()
        sibling, _xn, _yn, diag = _peers(x, y, c)
        send = _remote(b.at[_dev_index(diag)], b.at[_dev_index(diag)], sems(0), sibling)
        recv = _remote(b.at[_dev_index(diag)], b.at[_dev_index((diag[0], diag[1], 1 - c))], sems(0), sibling)

        def wait():
            send.wait_send()
            recv.wait_recv()

        return send.start, wait

    return carry.add([buf], [jax.ShapeDtypeStruct(buf.shape, buf.dtype)], {0: 0}, 1, 0, build)[0]


def _scatter_sibling(carry, grad):
    def build(ins, outs, sems, locs):
        x, y, c = _place()
        cp = _remote(ins[0].at[:, 1 - c], outs[0], sems(0), (x, y, 1 - c))
        return cp.start, cp.wait

    shape = (grad.shape[0],) + grad.shape[2:]
    return carry.add([grad], [jax.ShapeDtypeStruct(shape, grad.dtype)], {}, 1, 0, build)[0]


def _scatter_chips(carry, parts, landing, layer):
    def build(ins, outs, sems, locs):
        x, y, c = _place()
        cps = [_remote(ins[0].at[layer, 2 * dev[0] + dev[1]], outs[0].at[k, layer], sems(k), dev)
               for k, dev in enumerate(_peers(x, y, c)[1:])]

        def start():
            for cp in cps:
                cp.start()

        def wait():
            for cp in cps:
                cp.wait()

        return start, wait

    return carry.add([parts, landing], [jax.ShapeDtypeStruct(landing.shape, landing.dtype)], {1: 0}, 3, 0, build)[0]


def _call(name, body, grid, in_specs, out_specs, out_shape, args, scratch=(), carry=None):
    ni, no, ns = len(args), len(out_shape), len(scratch)
    if carry is None or not carry.items:
        res = pl.pallas_call(
            body, name=name, grid=grid, in_specs=list(in_specs), out_specs=list(out_specs), out_shape=list(out_shape),
            scratch_shapes=list(scratch), compiler_params=_cparams(("arbitrary",) * len(grid)))(*args)
        return list(res), []
    nci, nco = len(carry.ins), len(carry.outs)

    def full_body(*refs):
        c_in = refs[ni:ni + nci]
        c_out = refs[ni + nci + no:ni + nci + no + nco]
        sc = refs[ni + nci + no + nco:]
        ops = carry.build(c_in, c_out, sc[ns], sc[ns + 1], sc[ns + 2])
        first = last = None
        for d, n in enumerate(grid):
            pid = pl.program_id(d)
            first = (pid == 0) if first is None else first & (pid == 0)
            last = (pid == n - 1) if last is None else last & (pid == n - 1)

        @pl.when(first)
        def _():
            for start, _w in ops:
                start()

        body(*refs[:ni], *refs[ni + nci:ni + nci + no], *sc[:ns])

        @pl.when(last)
        def _():
            for _s, wait in ops:
                wait()

    hbm = pl.BlockSpec(memory_space=pl.ANY)
    res = pl.pallas_call(
        full_body, name=name, grid=grid,
        in_specs=list(in_specs) + [hbm] * nci, out_specs=list(out_specs) + [hbm] * nco,
        out_shape=list(out_shape) + list(carry.outs),
        input_output_aliases={ni + s: no + d for s, d in carry.alias.items()},
        scratch_shapes=list(scratch) + [pltpu.SemaphoreType.DMA((carry.nsem,)), pltpu.SemaphoreType.DMA((carry.nsem,)),
                                        pltpu.SemaphoreType.DMA((max(carry.nloc, 1),))],
        compiler_params=_cparams(("arbitrary",) * len(grid)))(*args, *carry.ins)
    return list(res[:no]), list(res[no:])


def _comm_only(name, carry):
    return _call(name, lambda *refs: None, (1,), [], [], [], [], carry=carry)[1]


def _matmul(name, a, b, *, pair2=None, trans_a=False, trans_b=False, out_dtype=F32, scale=None, residual=None,
            tm=512, tn=512, tk=512, carry=None):
    if trans_a:
        K, M = a.shape
    else:
        M, K = a.shape
    if trans_b:
        N, Kb = b.shape
    else:
        Kb, N = b.shape
    assert K == Kb, (name, a.shape, b.shape)
    tm = _tile(M, tm, LANES if trans_a else 16)
    tn = _tile(N, tn, LANES)
    tk = _tile(K, tk, 16 if (trans_a and not trans_b) else LANES)
    nk = K // tk
    dims = (((0 if trans_a else 1,), (1 if trans_b else 0,)), ((), ()))
    pairs = [(a, b)] + ([pair2] if pair2 is not None else [])
    npair = len(pairs)

    def body(*refs):
        ab = refs[:2 * npair]
        pos = 2 * npair
        r_ref = None
        if residual is not None:
            r_ref = refs[pos]
            pos += 1
        o_ref = refs[pos]

        def partial():
            t = None
            for q in range(npair):
                d = lax.dot_general(ab[2 * q][...].astype(BF16), ab[2 * q + 1][...].astype(BF16), dims,
                                    preferred_element_type=F32)
                t = d if t is None else t + d
            return t

        def finish(r):
            if scale is not None:
                r = r * scale
            if r_ref is not None:
                r = r + r_ref[...].astype(F32)
            o_ref[...] = r.astype(o_ref.dtype)

        if nk == 1:
            finish(partial())
            return
        acc_ref = refs[pos + 1]
        k = pl.program_id(2)

        @pl.when(k == 0)
        def _():
            acc_ref[...] = partial()

        @pl.when(k > 0)
        def _():
            acc_ref[...] += partial()

        @pl.when(k == nk - 1)
        def _():
            finish(acc_ref[...])

    a_spec = pl.BlockSpec((tk, tm), lambda i, j, k: (k, i)) if trans_a else pl.BlockSpec((tm, tk), lambda i, j, k: (i, k))
    b_spec = pl.BlockSpec((tn, tk), lambda i, j, k: (j, k)) if trans_b else pl.BlockSpec((tk, tn), lambda i, j, k: (k, j))
    in_specs, args = [], []
    for pa, pb in pairs:
        in_specs += [a_spec, b_spec]
        args += [pa, pb]
    if residual is not None:
        in_specs.append(pl.BlockSpec((tm, tn), lambda i, j, k: (i, j)))
        args.append(residual)
    res, extra = _call(
        name, body, (M // tm, N // tn, nk), in_specs, [pl.BlockSpec((tm, tn), lambda i, j, k: (i, j))],
        [jax.ShapeDtypeStruct((M, N), out_dtype)], args,
        scratch=[pltpu.VMEM((tm, tn), F32)] if nk > 1 else [], carry=carry)
    return res[0], extra


def _shard_bf16(name, w, layer, layer_axis):
    R, C = w.shape[1 - layer_axis], w.shape[2]
    tr = _tile(R, 512, 16)

    def body(x_ref, o_ref):
        o_ref[...] = x_ref[...].astype(BF16)

    if layer_axis == 0:
        in_spec = pl.BlockSpec((None, tr, C), lambda i: (layer, i, 0))
    else:
        in_spec = pl.BlockSpec((tr, C), lambda i: (i, layer))
        w = w.reshape(R, -1)
    return pl.pallas_call(
        body, name=name, grid=(R // tr,),
        in_specs=[in_spec], out_specs=pl.BlockSpec((tr, C), lambda i: (i, 0)),
        out_shape=jax.ShapeDtypeStruct((R, C), BF16),
        compiler_params=_cparams(("parallel",)),
    )(w)


def _rmsnorm_fwd(name, h, g):
    L, D = h.shape
    tm = _tile(L, 256, 16)

    def body(h_ref, g_ref, o_ref):
        x = h_ref[...]
        r = lax.rsqrt(jnp.mean(x * x, axis=1, keepdims=True) + EPS)
        o_ref[...] = (x * r * g_ref[...]).astype(o_ref.dtype)

    return pl.pallas_call(
        body, name=name, grid=(L // tm,),
        in_specs=[pl.BlockSpec((tm, D), lambda i: (i, 0)), pl.BlockSpec((1, D), lambda i: (0, 0))],
        out_specs=pl.BlockSpec((tm, D), lambda i: (i, 0)),
        out_shape=jax.ShapeDtypeStruct((L, D), BF16),
        compiler_params=_cparams(("parallel",)),
    )(h, g)


def _rmsnorm_bwd(name, dy, h, g, dres):
    L, D = h.shape
    tm = _tile(L, 256, 16)

    def body(dy_ref, h_ref, g_ref, dres_ref, dh_ref, dh16_ref, dg_ref):
        i = pl.program_id(0)
        x = h_ref[...]
        dyv = dy_ref[...].astype(F32)
        r = lax.rsqrt(jnp.mean(x * x, axis=1, keepdims=True) + EPS)
        w = dyv * g_ref[...]
        proj = jnp.sum(w * x, axis=1, keepdims=True) * (1.0 / D)
        dh = dres_ref[...] + r * w - x * (r * r * r * proj)
        dh_ref[...] = dh
        dh16_ref[...] = dh.astype(BF16)

        @pl.when(i == 0)
        def _():
            dg_ref[...] = jnp.zeros_like(dg_ref)

        dg_ref[...] += jnp.sum(dyv * x * r, axis=0, keepdims=True)

    return pl.pallas_call(
        body, name=name, grid=(L // tm,),
        in_specs=[pl.BlockSpec((tm, D), lambda i: (i, 0)), pl.BlockSpec((tm, D), lambda i: (i, 0)),
                  pl.BlockSpec((1, D), lambda i: (0, 0)), pl.BlockSpec((tm, D), lambda i: (i, 0))],
        out_specs=[pl.BlockSpec((tm, D), lambda i: (i, 0)), pl.BlockSpec((tm, D), lambda i: (i, 0)),
                   pl.BlockSpec((1, D), lambda i: (0, 0))],
        out_shape=[jax.ShapeDtypeStruct((L, D), F32), jax.ShapeDtypeStruct((L, D), BF16),
                   jax.ShapeDtypeStruct((1, D), F32)],
        compiler_params=_cparams(("arbitrary",)),
    )(dy, h, g, dres)


def _ffn_up(name, xn, wgT, wuT, carry=None):
    L, D = xn.shape
    F = wgT.shape[0]
    tm = _tile(L, 544, 16)
    tn = _tile(F, 512, LANES)

    def body(x_ref, wg_ref, wu_ref, g_ref, u_ref, a_ref):
        x = x_ref[...]
        g = _dot_nt(x, wg_ref[...])
        u = _dot_nt(x, wu_ref[...])
        g_ref[...] = g.astype(BF16)
        u_ref[...] = u.astype(BF16)
        a_ref[...] = (g * _sigmoid(g) * u).astype(BF16)

    o_spec = pl.BlockSpec((tm, tn), lambda i, j: (i, j))
    o_shape = jax.ShapeDtypeStruct((L, F), BF16)
    return _call(
        name, body, (L // tm, F // tn),
        [pl.BlockSpec((tm, D), lambda i, j: (i, 0)), pl.BlockSpec((tn, D), lambda i, j: (j, 0)),
         pl.BlockSpec((tn, D), lambda i, j: (j, 0))],
        [o_spec, o_spec, o_spec], [o_shape, o_shape, o_shape], [xn, wgT, wuT], carry=carry)


def _ffn_bwd_act(name, dh, wd, gate, up, carry=None):
    L, D = dh.shape
    F = wd.shape[0]
    tm = _tile(L, 544, 16)
    tn = _tile(F, 512, LANES)

    def body(dh_ref, wd_ref, g_ref, u_ref, dg_ref, du_ref, a_ref):
        da = 0.5 * _dot_nt(dh_ref[...].astype(BF16), wd_ref[...])
        g = g_ref[...].astype(F32)
        u = u_ref[...].astype(F32)
        sg = _sigmoid(g)
        silu = g * sg
        dg_ref[...] = (da * u * (sg * (1.0 + g * (1.0 - sg)))).astype(BF16)
        du_ref[...] = (da * silu).astype(BF16)
        a_ref[...] = (silu * u).astype(BF16)

    o_spec = pl.BlockSpec((tm, tn), lambda i, j: (i, j))
    o_shape = jax.ShapeDtypeStruct((L, F), BF16)
    return _call(
        name, body, (L // tm, F // tn),
        [pl.BlockSpec((tm, D), lambda i, j: (i, 0)), pl.BlockSpec((tn, D), lambda i, j: (j, 0)), o_spec, o_spec],
        [o_spec, o_spec, o_spec], [o_shape, o_shape, o_shape], [dh, wd, gate, up], carry=carry)


class _Cfg:
    def __init__(self, S, D, hf, hs, hkv):
        self.S, self.D, self.L = S, D, S + BLOCK
        self.hf, self.hs, self.hkv = hf, hs, hkv
        self.wf, self.ws = hf * HEAD_DIM, hs * HEAD_DIM
        self.group = hs // hkv
        self.cs = 2 * LANES // hf
        self.n_meta = 16
        self.pad = BLOCK - self.n_meta
        self.o_fk = self.wf
        self.o_fv = 2 * self.wf
        self.o_sq = 3 * self.wf
        self.o_sk = self.o_sq + self.ws
        self.o_sv = self.o_sk + LANES
        self.o_fz = self.o_sv + LANES
        self.up = self.o_fz + LANES
        assert hkv == 2 and hf % 2 == 0 and self.group % 2 == 0 and self.cs % 8 == 0
        assert self.o_sq % self.ws == 0 and (2 * self.wf) % LANES == 0


def _head_norm(x, gain, lo, mult):
    r = lax.rsqrt(_half_sum(x * x, lo) * (1.0 / HEAD_DIM) + EPS)
    return x * r * (gain * mult)


def _head_norm_bwd(dy, x, gain, lo, mult):
    r = lax.rsqrt(_half_sum(x * x, lo) * (1.0 / HEAD_DIM) + EPS)
    w = dy * (gain * mult)
    proj = _half_sum(w * x, lo) * (1.0 / HEAD_DIM)
    dx = r * w - x * (r * r * r * proj)
    dgain = jnp.sum(dy * mult * x * r, axis=0, keepdims=True)
    return dx, dgain


def _dup(x, lo):
    xr = pltpu.roll(x, 64, 1)
    return jnp.where(lo, x, xr), jnp.where(lo, xr, x)


def _mix_prep(name, cfg, u, gq, gk, gsq, gsk, bias):
    L, wf, ws = cfg.L, cfg.wf, cfg.ws
    T = BLOCK
    npf, nps = wf // LANES, ws // LANES

    def body(fqk_ref, fv_ref, sq_ref, sk_ref, sv_ref, fz_ref, gq_ref, gk_ref, gsq_ref, gsk_ref, b_ref,
             qn_ref, kn_ref, fvo_ref, sqn_ref, skd_ref, svd_ref, c_ref, ct_ref, carry_ref):
        i = pl.program_id(0)
        lo = _lane((T, LANES)) < HEAD_DIM
        for p in range(npf):
            sl = slice(p * LANES, (p + 1) * LANES)
            qn_ref[:, sl] = _head_norm(fqk_ref[:, sl], gq_ref[...], lo, SCALE).astype(BF16)
            kn_ref[:, sl] = _head_norm(fqk_ref[:, wf + p * LANES: wf + (p + 1) * LANES], gk_ref[...], lo, 1.0).astype(BF16)
        fvo_ref[...] = fv_ref[...].astype(BF16)
        for p in range(nps):
            sl = slice(p * LANES, (p + 1) * LANES)
            sqn_ref[:, sl] = _head_norm(sq_ref[:, sl], gsq_ref[...], lo, SCALE).astype(BF16)
        k0, k1 = _dup(_head_norm(sk_ref[...], gsk_ref[...], lo, 1.0), lo)
        skd_ref[:, :LANES] = k0.astype(BF16)
        skd_ref[:, LANES:] = k1.astype(BF16)
        v0, v1 = _dup(sv_ref[...], lo)
        svd_ref[:, :LANES] = v0.astype(BF16)
        svd_ref[:, LANES:] = v1.astype(BF16)

        @pl.when(i == 0)
        def _():
            carry_ref[...] = jnp.zeros_like(carry_ref)

        z = fz_ref[...] + b_ref[...]
        lf = jnp.minimum(z, 0.0) - jnp.log(1.0 + jnp.exp(-jnp.abs(z)))
        row = lax.broadcasted_iota(jnp.int32, (T, T), 0)
        col = lax.broadcasted_iota(jnp.int32, (T, T), 1)
        tri = jnp.where(col <= row, 1.0, 0.0).astype(F32)
        c = jnp.dot(tri, lf, precision=HI, preferred_element_type=F32) + carry_ref[0:1, :]
        c_ref[...] = c
        ct_ref[...] = c.T
        carry_ref[0:1, :] = c_ref[T - 1:T, :]

    def rows(w, cb):
        return pl.BlockSpec((T, w), lambda i, cb=cb: (i, cb))

    vec = pl.BlockSpec((1, LANES), lambda i: (0, 0))
    return pl.pallas_call(
        body, name=name, grid=(L // T,),
        in_specs=[rows(2 * wf, 0), rows(wf, 2), rows(ws, cfg.o_sq // ws), rows(LANES, cfg.o_sk // LANES),
                  rows(LANES, cfg.o_sv // LANES), rows(LANES, cfg.o_fz // LANES), vec, vec, vec, vec, vec],
        out_specs=[rows(wf, 0), rows(wf, 0), rows(wf, 0), rows(ws, 0), rows(2 * LANES, 0), rows(2 * LANES, 0),
                   rows(LANES, 0), pl.BlockSpec((LANES, T), lambda i: (0, i))],
        out_shape=[jax.ShapeDtypeStruct((L, wf), BF16)] * 3 + [jax.ShapeDtypeStruct((L, ws), BF16)]
        + [jax.ShapeDtypeStruct((L, 2 * LANES), BF16)] * 2
        + [jax.ShapeDtypeStruct((L, LANES), F32), jax.ShapeDtypeStruct((LANES, L), F32)],
        scratch_shapes=[pltpu.VMEM((8, LANES), F32)],
        compiler_params=_cparams(("arbitrary",)),
    )(u, u, u, u, u, u, gq, gk, gsq, gsk, bias)


def _mix_prep_bwd(name, cfg, u, gq, gk, gsq, gsk, bias, dqn, dkn, dfv, dsqn, dskd, dsvd, dct, dcq):
    L, wf, ws = cfg.L, cfg.wf, cfg.ws
    T = BLOCK
    nb = L // T
    npf, nps = wf // LANES, ws // LANES

    def body(fqk_ref, sq_ref, sk_ref, fz_ref, gq_ref, gk_ref, gsq_ref, gsk_ref, b_ref,
             dqn_ref, dkn_ref, dfv_ref, dsqn_ref, dskd_ref, dsvd_ref, dct_ref, dcq_ref,
             du_ref, dgq_ref, dgk_ref, dgsq_ref, dgsk_ref, db_ref, carry_ref):
        i = pl.program_id(0)
        lo = _lane((T, LANES)) < HEAD_DIM

        @pl.when(i == 0)
        def _():
            carry_ref[...] = jnp.zeros_like(carry_ref)
            for r in (dgq_ref, dgk_ref, dgsq_ref, dgsk_ref, db_ref):
                r[...] = jnp.zeros_like(r)

        accq = jnp.zeros((1, LANES), F32)
        acck = jnp.zeros((1, LANES), F32)
        for p in range(npf):
            sl = slice(p * LANES, (p + 1) * LANES)
            dx, dg = _head_norm_bwd(dqn_ref[:, sl], fqk_ref[:, sl], gq_ref[...], lo, SCALE)
            du_ref[:, sl] = dx.astype(BF16)
            accq = accq + dg
            slk = slice(wf + p * LANES, wf + (p + 1) * LANES)
            dx, dg = _head_norm_bwd(dkn_ref[:, sl], fqk_ref[:, slk], gk_ref[...], lo, 1.0)
            du_ref[:, slk] = dx.astype(BF16)
            acck = acck + dg
        dgq_ref[...] += accq
        dgk_ref[...] += acck
        du_ref[:, cfg.o_fv:cfg.o_fv + wf] = dfv_ref[...].astype(BF16)
        accs = jnp.zeros((1, LANES), F32)
        for p in range(nps):
            sl = slice(p * LANES, (p + 1) * LANES)
            dx, dg = _head_norm_bwd(dsqn_ref[:, sl], sq_ref[:, sl], gsq_ref[...], lo, SCALE)
            du_ref[:, cfg.o_sq + p * LANES: cfg.o_sq + (p + 1) * LANES] = dx.astype(BF16)
            accs = accs + dg
        dgsq_ref[...] += accs

        def fold(ref):
            a0, a1 = ref[:, :LANES], ref[:, LANES:]
            return jnp.where(lo, a0 + pltpu.roll(a0, 64, 1), a1 + pltpu.roll(a1, 64, 1))

        dx, dg = _head_norm_bwd(fold(dskd_ref), sk_ref[...], gsk_ref[...], lo, 1.0)
        du_ref[:, cfg.o_sk:cfg.o_sk + LANES] = dx.astype(BF16)
        dgsk_ref[...] += dg
        du_ref[:, cfg.o_sv:cfg.o_sv + LANES] = fold(dsvd_ref).astype(BF16)

        dc = dct_ref[...].T + dcq_ref[...]
        row = lax.broadcasted_iota(jnp.int32, (T, T), 0)
        col = lax.broadcasted_iota(jnp.int32, (T, T), 1)
        triu = jnp.where(col >= row, 1.0, 0.0).astype(F32)
        dlf = jnp.dot(triu, dc, precision=HI, preferred_element_type=F32) + carry_ref[0:1, :]
        carry_ref[0:1, :] = dlf[0:1, :]
        z = fz_ref[...] + b_ref[...]
        dz = dlf * _sigmoid(-z)
        du_ref[:, cfg.o_fz:cfg.o_fz + LANES] = dz.astype(BF16)
        db_ref[...] += jnp.sum(dz, axis=0, keepdims=True)

    def rows(w, cb):
        return pl.BlockSpec((T, w), lambda i, cb=cb: (nb - 1 - i, cb))

    vec = pl.BlockSpec((1, LANES), lambda i: (0, 0))
    vshape = jax.ShapeDtypeStruct((1, LANES), F32)
    return pl.pallas_call(
        body, name=name, grid=(nb,),
        in_specs=[rows(2 * wf, 0), rows(ws, cfg.o_sq // ws), rows(LANES, cfg.o_sk // LANES),
                  rows(LANES, cfg.o_fz // LANES), vec, vec, vec, vec, vec,
                  rows(wf, 0), rows(wf, 0), rows(wf, 0), rows(ws, 0), rows(2 * LANES, 0), rows(2 * LANES, 0),
                  pl.BlockSpec((LANES, T), lambda i: (0, nb - 1 - i)), rows(LANES, 0)],
        out_specs=[rows(cfg.up, 0), vec, vec, vec, vec, vec],
        out_shape=[jax.ShapeDtypeStruct((L, cfg.up), BF16)] + [vshape] * 5,
        scratch_shapes=[pltpu.VMEM((8, LANES), F32)],
        compiler_params=_cparams(("arbitrary",)),
    )(u, u, u, u, gq, gk, gsq, gsk, bias, dqn, dkn, dfv, dsqn, dskd, dsvd, dct, dcq)


def _fox_fwd(name, cfg, qn, kn, fv, c, ct, carry=None):
    L, wf, cs = cfg.L, cfg.wf, cfg.cs
    TQ = BLOCK
    TK = _tile(L, 544, 8)
    npairs = wf // LANES
    pad = cfg.pad

    def body(q_ref, k_ref, v_ref, c_ref, ct_ref, o_ref, lse_ref):
        p = pl.program_id(0)
        i = pl.program_id(1)
        lo_q = _lane((TQ, LANES)) < HEAD_DIM
        lane_k = _lane((TK, LANES))
        lo_k = lane_k < HEAD_DIM
        lo_d = lax.broadcasted_iota(jnp.int32, (LANES, TQ), 0) < HEAD_DIM
        first = _lane((TK, 2 * TQ)) < TQ
        q = q_ref[...]
        qs = jnp.concatenate([jnp.where(lo_q, q, jnp.zeros_like(q)), jnp.where(lo_q, jnp.zeros_like(q), q)], axis=0)
        cq = jnp.concatenate([ct_ref[0:1, :], ct_ref[1:2, :]], axis=1)
        qrow = lax.broadcasted_iota(jnp.int32, (TK, 2 * TQ), 1)
        qpos = i * TQ + jnp.where(first, qrow, qrow - TQ)

        def step(j, carry_):
            m, l, acc = carry_
            off = pl.multiple_of(j * TK, 8)
            k = k_ref[pl.ds(off, TK), :]
            v = v_ref[pl.ds(off, TK), :]
            cblk = c_ref[pl.ds(off, TK), :]
            kpos = j * TK + lax.broadcasted_iota(jnp.int32, (TK, 2 * TQ), 0)
            allowed = (kpos <= qpos) & (kpos >= pad)
            ck0 = jnp.sum(jnp.where(lane_k == p * cs, cblk, 0.0), axis=1, keepdims=True)
            ck1 = jnp.sum(jnp.where(lane_k == p * cs + 1, cblk, 0.0), axis=1, keepdims=True)
            s = _dot_nt(k, qs) + cq - jnp.where(first, ck0, ck1)
            s = jnp.where(allowed, s, NEG_INF)
            m_new = jnp.maximum(m, jnp.max(s, axis=0, keepdims=True))
            alpha = jnp.exp(m - m_new)
            pr = jnp.exp(s - m_new)
            l = alpha * l + jnp.sum(pr, axis=0, keepdims=True)
            prb = pr.astype(BF16)
            prs = jnp.concatenate([prb[:, :TQ], prb[:, TQ:]], axis=0)
            vs = jnp.concatenate([jnp.where(lo_k, v, jnp.zeros_like(v)), jnp.where(lo_k, jnp.zeros_like(v), v)], axis=0)
            acc = acc * jnp.where(lo_d, alpha[:, :TQ], alpha[:, TQ:]) + _dot_tn(vs, prs)
            return m_new, l, acc

        init = (jnp.full((1, 2 * TQ), NEG_INF, F32), jnp.zeros((1, 2 * TQ), F32), jnp.zeros((LANES, TQ), F32))
        m, l, acc = lax.fori_loop(0, ((i + 1) * TQ + TK - 1) // TK, step, init)
        o_ref[...] = (acc / jnp.where(lo_d, l[:, :TQ], l[:, TQ:])).T
        lse = m + jnp.log(l)
        lse_ref[...] = jnp.where(lo_d, lse[:, :TQ], lse[:, TQ:]).T

    blk = pl.BlockSpec((TQ, LANES), lambda p, i: (i, p))
    full = pl.BlockSpec((L, LANES), lambda p, i: (0, p))
    return _call(
        name, body, (npairs, L // TQ),
        [blk, full, full, pl.BlockSpec((L, LANES), lambda p, i: (0, 0)), pl.BlockSpec((cs, TQ), lambda p, i: (p, i))],
        [blk, blk], [jax.ShapeDtypeStruct((L, wf), F32)] * 2, [qn, kn, fv, c, ct], carry=carry)


def _fox_bwd(name, cfg, qn, kn, fv, c, ct, o, lse, do, carry=None):
    L, wf, cs = cfg.L, cfg.wf, cfg.cs
    T = BLOCK
    TQ = _tile(L, 544, 8)
    nb = L // T
    nq = L // TQ
    npairs = wf // LANES
    pad = cfg.pad

    def body(q_ref, k_ref, v_ref, c_ref, ct_ref, o_ref, lse_ref, do_ref, dq_ref, dk_ref, dv_ref, dct_ref, dcq_ref,
             cq_ref, lser_ref, dsum_ref, dsacc_ref):
        p = pl.program_id(0)
        j = pl.program_id(1)
        lane = _lane((TQ, LANES))
        lo = lane < HEAD_DIM
        sels = (lo, jnp.logical_not(lo))

        def stack(x):
            return jnp.concatenate([jnp.where(lo, x, jnp.zeros_like(x)), jnp.where(lo, jnp.zeros_like(x), x)], axis=0)

        @pl.when((j == 0) & (p == 0))
        def _():
            dcq_ref[...] = jnp.zeros_like(dcq_ref)

        @pl.when(j == 0)
        def _():
            dq_ref[...] = jnp.zeros_like(dq_ref)
            dsacc_ref[...] = jnp.zeros_like(dsacc_ref)
            for t in range(nq):
                rows = slice(t * TQ, (t + 1) * TQ)
                dd = do_ref[rows, :] * o_ref[rows, :]
                lse_b = lse_ref[rows, :]
                cblk = c_ref[rows, :]
                for hh in range(2):
                    half = slice(hh * TQ, (hh + 1) * TQ)
                    dsum_ref[t, half, :] = jnp.broadcast_to(
                        jnp.sum(jnp.where(sels[hh], dd, 0.0), axis=1, keepdims=True), (TQ, LANES))
                    lser_ref[t, half, :] = jnp.broadcast_to(
                        jnp.sum(jnp.where(lane == hh * HEAD_DIM, lse_b, 0.0), axis=1, keepdims=True), (TQ, LANES))
                    cq_ref[t, half, :] = jnp.broadcast_to(
                        jnp.sum(jnp.where(lane == p * cs + hh, cblk, 0.0), axis=1, keepdims=True), (TQ, LANES))

        k = k_ref[...]
        v = v_ref[...]
        ck = jnp.concatenate([jnp.broadcast_to(ct_ref[0:1, :], (TQ, T)), jnp.broadcast_to(ct_ref[1:2, :], (TQ, T))], axis=0)
        row = lax.broadcasted_iota(jnp.int32, (2 * TQ, T), 0)
        qrow = jnp.where(row < TQ, row, row - TQ)
        kpos = j * T + lax.broadcasted_iota(jnp.int32, (2 * TQ, T), 1)

        def step(i, carry_):
            dk, dv, dc0, dc1 = carry_
            off = pl.multiple_of(i * TQ, 8)
            rows = pl.ds(off, TQ)
            qs = stack(q_ref[rows, :])
            dos = stack(do_ref[rows, :]).astype(BF16)
            allowed = (kpos <= i * TQ + qrow) & (kpos >= pad)
            s = _dot_nt(qs, k) + cq_ref[i] - ck
            pr = jnp.where(allowed, jnp.exp(jnp.where(allowed, s, NEG_INF) - lser_ref[i]), 0.0)
            ds = pr * (_dot_nt(dos, v) - dsum_ref[i])
            dsb = ds.astype(BF16)
            dv = dv + _dot_tn(pr.astype(BF16), dos)
            dk = dk + _dot_tn(dsb, qs)
            dqs = _dot(dsb, k)
            dq_ref[rows, :] += jnp.where(lo, dqs[:TQ], dqs[TQ:])
            dc0 = dc0 - jnp.sum(ds[:TQ], axis=0, keepdims=True)
            dc1 = dc1 - jnp.sum(ds[TQ:], axis=0, keepdims=True)
            dsacc_ref[i] += ds
            return dk, dv, dc0, dc1

        init = (jnp.zeros((T, LANES), F32), jnp.zeros((T, LANES), F32),
                jnp.zeros((1, T), F32), jnp.zeros((1, T), F32))
        dk, dv, dc0, dc1 = lax.fori_loop((j * T) // TQ, nq, step, init)
        dk_ref[...] = dk
        dv_ref[...] = dv
        dct_ref[...] = jnp.zeros_like(dct_ref)
        dct_ref[0:1, :] = dc0
        dct_ref[1:2, :] = dc1

        @pl.when(j == nb - 1)
        def _():
            for t in range(nq):
                upd = jnp.zeros((TQ, LANES), F32)
                for hh in range(2):
                    upd = upd + jnp.where(lane == p * cs + hh,
                                          jnp.sum(dsacc_ref[t, hh * TQ:(hh + 1) * TQ, :], axis=1, keepdims=True), 0.0)
                dcq_ref[t * TQ:(t + 1) * TQ, :] += upd

    blk = pl.BlockSpec((T, LANES), lambda p, j: (j, p))
    full = pl.BlockSpec((L, LANES), lambda p, j: (0, p))
    return _call(
        name, body, (npairs, nb),
        [full, blk, blk, pl.BlockSpec((L, LANES), lambda p, j: (0, 0)), pl.BlockSpec((cs, T), lambda p, j: (p, j)),
         full, full, full],
        [full, blk, blk, pl.BlockSpec((cs, T), lambda p, j: (p, j)), pl.BlockSpec((L, LANES), lambda p, j: (0, 0))],
        [jax.ShapeDtypeStruct((L, wf), F32)] * 3
        + [jax.ShapeDtypeStruct((LANES, L), F32), jax.ShapeDtypeStruct((L, LANES), F32)],
        [qn, kn, fv, c, ct, o, lse, do], scratch=[pltpu.VMEM((nq, 2 * TQ, LANES), F32)] * 4, carry=carry)


def _swa_band(i, pad):
    T = BLOCK
    t = lax.broadcasted_iota(jnp.int32, (T, 2 * T), 0)
    u = lax.broadcasted_iota(jnp.int32, (T, 2 * T), 1)
    dist = t - u + T
    allowed = (dist >= 0) & (dist < T) & ((i - 1) * T + u >= pad)
    return dist.astype(F32), allowed


def _swa_probs(s, dist, allowed, slope, sink):
    s = jnp.where(allowed, s - slope * dist, NEG_INF)
    m = jnp.maximum(jnp.max(s, axis=1, keepdims=True), sink)
    e = jnp.exp(s - m)
    es = jnp.exp(sink - m)
    den = jnp.sum(e, axis=1, keepdims=True) + es
    return e / den, es / den


def _swa_stack(x, group, lo):
    tiles = []
    for h in range(group):
        pair = x[:, (h // 2) * LANES:(h // 2 + 1) * LANES]
        tiles.append(jnp.where(lo if h % 2 == 0 else jnp.logical_not(lo), pair, jnp.zeros_like(pair)))
    return jnp.concatenate(tiles, axis=0)


def _swa_fwd(name, cfg, sqn, skd, svd, sinks, slopes, carry=None):
    L, ws, group, hkv = cfg.L, cfg.ws, cfg.group, cfg.hkv
    T = BLOCK
    gw = group * HEAD_DIM
    pad = cfg.pad

    def body(sink_ref, slope_ref, q_ref, kp_ref, kc_ref, vp_ref, vc_ref, o_ref):
        kv = pl.program_id(0)
        i = pl.program_id(1)
        lo = _lane((T, LANES)) < HEAD_DIM
        dist, allowed = _swa_band(i, pad)
        kb = jnp.concatenate([kp_ref[...], kc_ref[...]], axis=0)
        vb = jnp.concatenate([vp_ref[...], vc_ref[...]], axis=0)
        s = _dot_nt(_swa_stack(q_ref[...], group, lo), kb)
        probs = []
        for h in range(group):
            p, _ = _swa_probs(s[h * T:(h + 1) * T], dist, allowed, slope_ref[kv * group + h], sink_ref[kv * group + h])
            probs.append(p.astype(BF16))
        o = _dot(jnp.concatenate(probs, axis=0), vb)
        for pp in range(group // 2):
            o_ref[:, pp * LANES:(pp + 1) * LANES] = jnp.where(lo, o[2 * pp * T:(2 * pp + 1) * T],
                                                              o[(2 * pp + 1) * T:(2 * pp + 2) * T])

    smem = pl.BlockSpec(memory_space=pltpu.SMEM)
    prev = pl.BlockSpec((T, LANES), lambda kv, i: (jnp.maximum(i - 1, 0), kv))
    cur = pl.BlockSpec((T, LANES), lambda kv, i: (i, kv))
    qblk = pl.BlockSpec((T, gw), lambda kv, i: (i, kv))
    return _call(name, body, (hkv, L // T), [smem, smem, qblk, prev, cur, prev, cur], [qblk],
                 [jax.ShapeDtypeStruct((L, ws), F32)], [sinks, slopes, sqn, skd, skd, svd, svd], carry=carry)


def _swa_bwd(name, cfg, sqn, skd, svd, sinks, slopes, o, do, carry=None):
    L, ws, group, hkv = cfg.L, cfg.ws, cfg.group, cfg.hkv
    T = BLOCK
    gw = group * HEAD_DIM
    ppk = group // 2
    pad = cfg.pad

    def body(sink_ref, slope_ref, q_ref, kp_ref, kc_ref, vp_ref, vc_ref, o_ref, do_ref,
             dq_ref, dk_ref, dv_ref, dsink_ref):
        kv = pl.program_id(0)
        i = pl.program_id(1)
        lane = _lane((T, LANES))
        lo = lane < HEAD_DIM

        @pl.when(i == 0)
        def _():
            dk_ref[...] = jnp.zeros_like(dk_ref)
            dv_ref[...] = jnp.zeros_like(dv_ref)
            dsink_ref[...] = jnp.zeros_like(dsink_ref)

        dist, allowed = _swa_band(i, pad)
        kb = jnp.concatenate([kp_ref[...], kc_ref[...]], axis=0)
        vb = jnp.concatenate([vp_ref[...], vc_ref[...]], axis=0)
        dov = do_ref[...]
        qs = _swa_stack(q_ref[...], group, lo)
        dos = _swa_stack(dov, group, lo).astype(BF16)
        dd = dov * o_ref[...]
        s = _dot_nt(qs, kb)
        dp = _dot_nt(dos, vb)
        probs, dss = [], []
        for h in range(group):
            pair = dd[:, (h // 2) * LANES:(h // 2 + 1) * LANES]
            dsum = jnp.sum(jnp.where(lo if h % 2 == 0 else jnp.logical_not(lo), pair, 0.0), axis=1, keepdims=True)
            p, ps = _swa_probs(s[h * T:(h + 1) * T], dist, allowed, slope_ref[kv * group + h], sink_ref[kv * group + h])
            probs.append(p.astype(BF16))
            dss.append((p * (dp[h * T:(h + 1) * T] - dsum)).astype(BF16))
            dsink_ref[8 * (h // 2):8 * (h // 2) + 1, :] += jnp.where(lane[0:1, :] == (h % 2) * HEAD_DIM,
                                                                    -jnp.sum(ps * dsum), 0.0)
        ds = jnp.concatenate(dss, axis=0)
        dq = _dot(ds, kb)
        for pp in range(ppk):
            dq_ref[:, pp * LANES:(pp + 1) * LANES] = jnp.where(lo, dq[2 * pp * T:(2 * pp + 1) * T],
                                                               dq[(2 * pp + 1) * T:(2 * pp + 2) * T])
        dk = _dot_tn(ds, qs)
        dv = _dot_tn(jnp.concatenate(probs, axis=0), dos)
        cur = pl.multiple_of(i * T, T)
        dk_ref[pl.ds(cur, T), :] += dk[T:]
        dv_ref[pl.ds(cur, T), :] += dv[T:]

        @pl.when(i > 0)
        def _():
            prv = pl.multiple_of((i - 1) * T, T)
            dk_ref[pl.ds(prv, T), :] += dk[:T]
            dv_ref[pl.ds(prv, T), :] += dv[:T]

    smem = pl.BlockSpec(memory_space=pltpu.SMEM)
    prev = pl.BlockSpec((T, LANES), lambda kv, i: (jnp.maximum(i - 1, 0), kv))
    cur = pl.BlockSpec((T, LANES), lambda kv, i: (i, kv))
    qblk = pl.BlockSpec((T, gw), lambda kv, i: (i, kv))
    full = pl.BlockSpec((L, LANES), lambda kv, i: (0, kv))
    return _call(
        name, body, (hkv, L // T), [smem, smem, qblk, prev, cur, prev, cur, qblk, qblk],
        [qblk, full, full, pl.BlockSpec((8 * ppk, LANES), lambda kv, i: (kv, 0))],
        [jax.ShapeDtypeStruct((L, ws), F32), jax.ShapeDtypeStruct((L, 2 * LANES), F32),
         jax.ShapeDtypeStruct((L, 2 * LANES), F32), jax.ShapeDtypeStruct((8 * ppk * hkv, LANES), F32)],
        [sinks, slopes, sqn, skd, skd, svd, svd, o, do], carry=carry)


def _out_norm(name, cfg, o_fox, o_swa, g_fox, g_swa):
    L, wf, ws = cfg.L, cfg.wf, cfg.ws
    tm = _tile(L, 256, 16)

    def body(of_ref, os_ref, gf_ref, gs_ref, o_ref):
        for src, g_ref, lo_, w in ((of_ref, gf_ref, 0, wf), (os_ref, gs_ref, wf, ws)):
            x = src[...]
            r = lax.rsqrt(jnp.mean(x * x, axis=1, keepdims=True) + EPS)
            o_ref[:, lo_:lo_ + w] = (x * r * g_ref[...]).astype(BF16)

    return pl.pallas_call(
        body, name=name, grid=(L // tm,),
        in_specs=[pl.BlockSpec((tm, wf), lambda i: (i, 0)), pl.BlockSpec((tm, ws), lambda i: (i, 0)),
                  pl.BlockSpec((1, wf), lambda i: (0, 0)), pl.BlockSpec((1, ws), lambda i: (0, 0))],
        out_specs=pl.BlockSpec((tm, wf + ws), lambda i: (i, 0)),
        out_shape=jax.ShapeDtypeStruct((L, wf + ws), BF16),
        compiler_params=_cparams(("parallel",)),
    )(o_fox, o_swa, g_fox, g_swa)


def _out_norm_bwd(name, cfg, dcat, o_fox, o_swa, g_fox, g_swa):
    L, wf, ws = cfg.L, cfg.wf, cfg.ws
    tm = _tile(L, 256, 16)

    def body(d_ref, of_ref, os_ref, gf_ref, gs_ref, dof_ref, dos_ref, dgf_ref, dgs_ref):
        i = pl.program_id(0)

        @pl.when(i == 0)
        def _():
            dgf_ref[...] = jnp.zeros_like(dgf_ref)
            dgs_ref[...] = jnp.zeros_like(dgs_ref)

        for src, g_ref, dst, dg_ref, lo_, w in ((of_ref, gf_ref, dof_ref, dgf_ref, 0, wf),
                                                (os_ref, gs_ref, dos_ref, dgs_ref, wf, ws)):
            x = src[...]
            dy = d_ref[:, lo_:lo_ + w]
            r = lax.rsqrt(jnp.mean(x * x, axis=1, keepdims=True) + EPS)
            wv = dy * g_ref[...]
            proj = jnp.sum(wv * x, axis=1, keepdims=True) * (1.0 / w)
            dst[...] = r * wv - x * (r * r * r * proj)
            dg_ref[...] += jnp.sum(dy * x * r, axis=0, keepdims=True)

    return pl.pallas_call(
        body, name=name, grid=(L // tm,),
        in_specs=[pl.BlockSpec((tm, wf + ws), lambda i: (i, 0)), pl.BlockSpec((tm, wf), lambda i: (i, 0)),
                  pl.BlockSpec((tm, ws), lambda i: (i, 0)),
                  pl.BlockSpec((1, wf), lambda i: (0, 0)), pl.BlockSpec((1, ws), lambda i: (0, 0))],
        out_specs=[pl.BlockSpec((tm, wf), lambda i: (i, 0)), pl.BlockSpec((tm, ws), lambda i: (i, 0)),
                   pl.BlockSpec((1, wf), lambda i: (0, 0)), pl.BlockSpec((1, ws), lambda i: (0, 0))],
        out_shape=[jax.ShapeDtypeStruct((L, wf), F32), jax.ShapeDtypeStruct((L, ws), F32),
                   jax.ShapeDtypeStruct((1, wf), F32), jax.ShapeDtypeStruct((1, ws), F32)],
        compiler_params=_cparams(("arbitrary",)),
    )(dcat, o_fox, o_swa, g_fox, g_swa)


def _loss_head(name, h, target):
    L, D = h.shape
    T = BLOCK

    def body(h_ref, t_ref, loss_ref, dh_ref, dh16_ref):
        i = pl.program_id(0)

        @pl.when(i == 0)
        def _():
            loss_ref[...] = jnp.zeros_like(loss_ref)
            dh_ref[...] = jnp.zeros_like(dh_ref)
            dh16_ref[...] = jnp.zeros_like(dh16_ref)

        @pl.when(i > 0)
        def _():
            err = h_ref[...] - t_ref[...]
            dh = err * (1.0 / D)
            dh_ref[...] = dh
            dh16_ref[...] = dh.astype(BF16)
            loss_ref[...] += jnp.sum(err * err) * (0.5 / D)

    return pl.pallas_call(
        body, name=name, grid=(L // T,),
        in_specs=[pl.BlockSpec((T, D), lambda i: (i, 0)), pl.BlockSpec((T, D), lambda i: (jnp.maximum(i - 1, 0), 0))],
        out_specs=[pl.BlockSpec((8, LANES), lambda i: (0, 0)), pl.BlockSpec((T, D), lambda i: (i, 0)),
                   pl.BlockSpec((T, D), lambda i: (i, 0))],
        out_shape=[jax.ShapeDtypeStruct((8, LANES), F32), jax.ShapeDtypeStruct((L, D), F32),
                   jax.ShapeDtypeStruct((L, D), BF16)],
        compiler_params=_cparams(("arbitrary",)),
    )(h, target)


def _pair_add(name, grad, landed, parts, layer, core):
    _, _, r, C = grad.shape

    def body(s_ref, g_ref, l_ref, p_ref, o_ref):
        o_ref[...] = (g_ref[...].astype(F32) + l_ref[...].astype(F32)).astype(o_ref.dtype)

    return pl.pallas_call(
        body, name=name,
        grid_spec=pltpu.PrefetchScalarGridSpec(
            num_scalar_prefetch=1, grid=(4,),
            in_specs=[pl.BlockSpec((None, None, r, C), lambda k, s: (k, s[0], 0, 0)),
                      pl.BlockSpec((None, r, C), lambda k, s: (k, 0, 0)),
                      pl.BlockSpec(memory_space=pl.ANY)],
            out_specs=pl.BlockSpec((None, None, r, C), lambda k, s: (layer, k, 0, 0))),
        out_shape=jax.ShapeDtypeStruct(parts.shape, parts.dtype),
        input_output_aliases={3: 0},
        compiler_params=_cparams(("arbitrary",)),
    )(core, grad, landed, parts)


def _chip_sum(name, part, landed, chip, layer_axis):
    A, _, r, C = part.shape

    def body(s_ref, p_ref, l0_ref, l1_ref, l2_ref, o_ref):
        o_ref[...] = ((p_ref[...].astype(F32) + l0_ref[...].astype(F32))
                      + (l1_ref[...].astype(F32) + l2_ref[...].astype(F32)))

    def land(k):
        return pl.BlockSpec((None, None, r, C), lambda a, s, k=k: (k, a, 0, 0))

    if layer_axis == 0:
        out_spec, out_shape = pl.BlockSpec((None, r, C), lambda a, s: (a, 0, 0)), (A, r, C)
    else:
        out_spec, out_shape = pl.BlockSpec((r, C), lambda a, s: (0, a)), (r, A * C)
    out = pl.pallas_call(
        body, name=name,
        grid_spec=pltpu.PrefetchScalarGridSpec(
            num_scalar_prefetch=1, grid=(A,),
            in_specs=[pl.BlockSpec((None, None, r, C), lambda a, s: (a, s[0], 0, 0)), land(0), land(1), land(2)],
            out_specs=out_spec),
        out_shape=jax.ShapeDtypeStruct(out_shape, F32),
        compiler_params=_cparams(("parallel",)),
    )(chip, part, landed, landed, landed)
    return out if layer_axis == 0 else out.reshape(r, A, C)


def _gather_sum(name, v):
    R = v.shape[0]

    def body(x_ref, o_ref, buf_ref, send_sems, recv_sems):
        x, y, c = _place()
        me, sibling = (x, y, c), (x, y, 1 - c)
        chips = [(1 - x, y), (x, 1 - y), (1 - x, 1 - y)]

        def rows(dev):
            px, py, pc = dev
            return buf_ref.at[4 * px + 2 * py + pc]

        def copy(k, block, to, src=None):
            return pltpu.make_async_remote_copy(
                src_ref=rows(block) if src is None else src, dst_ref=rows(block),
                send_sem=send_sems.at[k], recv_sem=recv_sems.at[k], device_id=to, device_id_type=MESH)

        first = [copy(0, me, sibling, src=x_ref)]
        first += [copy(1 + j, me, (*chip, c), src=x_ref) for j, chip in enumerate(chips)]
        for cp in first:
            cp.start()
        rows(me)[...] = x_ref[...]
        passed = [copy(4 + j, (*chip, c), sibling) for j, chip in enumerate(chips)]
        for j, chip in enumerate(chips):
            copy(1 + j, (*chip, c), me).wait_recv()
            passed[j].start()
        copy(0, sibling, me).wait_recv()
        for j, chip in enumerate(chips):
            copy(4 + j, (*chip, 1 - c), me).wait_recv()
        for cp in first + passed:
            cp.wait_send()
        acc = buf_ref[0]
        for d in range(1, N_DEV):
            acc = acc + buf_ref[d]
        o_ref[...] = acc

    vm = pl.BlockSpec(memory_space=pltpu.VMEM)
    return pl.pallas_call(
        body, name=name, in_specs=[vm], out_specs=vm,
        out_shape=jax.ShapeDtypeStruct((R, LANES), F32),
        scratch_shapes=[pltpu.VMEM((N_DEV, R, LANES), F32), pltpu.SemaphoreType.DMA((7,)), pltpu.SemaphoreType.DMA((7,))],
    )(v)


def _adamw(name, g, w, m, v):
    A, R, C = g.shape
    budget = 1 << 18
    tr = _tile(R, max(8, budget // C // 8 * 8), 8)
    ta = _tile(A, max(1, budget // (tr * C)), 1)

    def body(g_ref, w_ref, m_ref, v_ref, d_ref, nm_ref, nv_ref):
        gv = g_ref[...]
        nm = ADAM_B1 * m_ref[...] + (1.0 - ADAM_B1) * gv
        nv = ADAM_B2 * v_ref[...] + (1.0 - ADAM_B2) * (gv * gv)
        m_hat = nm / (1.0 - ADAM_B1 ** ADAM_STEP)
        v_hat = nv / (1.0 - ADAM_B2 ** ADAM_STEP)
        d_ref[...] = -ADAM_LR * (m_hat / (jnp.sqrt(v_hat) + ADAM_EPS) + ADAM_WD * w_ref[...])
        nm_ref[...] = nm
        nv_ref[...] = nv

    blk = pl.BlockSpec((ta, tr, C), lambda i, j: (i, j, 0))
    shp = jax.ShapeDtypeStruct((A, R, C), F32)
    return pl.pallas_call(
        body, name=name, grid=(A // ta, R // tr),
        in_specs=[blk] * 4, out_specs=[blk] * 3, out_shape=[shp] * 3,
        compiler_params=_cparams(("parallel", "parallel")),
    )(g, w, m, v)


def _adamw_nd(name, g, w, m, v):
    shape = w.shape
    three = (1,) + shape if len(shape) == 2 else shape
    return tuple(o.reshape(shape) for o in _adamw(name, *[a.reshape(three) for a in (g, w, m, v)]))


def _scatter_heads(cfg, vals):
    v = jnp.pad(vals.reshape(cfg.hf // 2, 2), ((0, 0), (0, cfg.cs - 2)))
    return v.reshape(1, LANES)


def _gather_heads(cfg, row):
    return row.reshape(cfg.hf // 2, cfg.cs)[:, :2].reshape(cfg.hf)


def _permute_w_in(cfg, w_in_t):
    wf, hf = cfg.wf, cfg.hf
    o = 3 * wf
    cols = w_in_t.shape[1]
    fz = w_in_t[o:o + hf].reshape(hf // 2, 2, cols)
    fz_blk = jnp.pad(fz, ((0, 0), (0, cfg.cs - 2), (0, 0))).reshape(LANES, cols)
    return jnp.concatenate([w_in_t[:o], w_in_t[o + hf:], fz_blk], axis=0)


def _unpermute_dw_in(cfg, dwp):
    wf, hf = cfg.wf, cfg.hf
    o = 3 * wf
    cols = dwp.shape[1]
    fz = dwp[cfg.o_fz:].reshape(hf // 2, cfg.cs, cols)[:, :2].reshape(hf, cols)
    return jnp.concatenate([dwp[:o], fz, dwp[o:cfg.o_fz]], axis=0)


def _pair_gain(g):
    return jnp.tile(g, 2)[None]


def _fold_pair(dg):
    return dg[0, :HEAD_DIM] + dg[0, HEAD_DIM:]


def kernel(x, meta_tokens, ffn1_norm, ffn1_w_gate, ffn1_w_up, ffn1_w_down, mix_norm, w_in, b_forget, fox_q_norm, fox_k_norm, swa_q_norm, swa_k_norm, swa_sinks, fox_out_norm, swa_out_norm, w_out, ffn2_norm, ffn2_w_gate, ffn2_w_up, ffn2_w_down, loss_target, m_meta_tokens, m_ffn1_norm, m_ffn1_w_gate, m_ffn1_w_up, m_ffn1_w_down, m_mix_norm, m_w_in, m_b_forget, m_fox_q_norm, m_fox_k_norm, m_swa_q_norm, m_swa_k_norm, m_swa_sinks, m_fox_out_norm, m_swa_out_norm, m_w_out, m_ffn2_norm, m_ffn2_w_gate, m_ffn2_w_up, m_ffn2_w_down, v_meta_tokens, v_ffn1_norm, v_ffn1_w_gate, v_ffn1_w_up, v_ffn1_w_down, v_mix_norm, v_w_in, v_b_forget, v_fox_q_norm, v_fox_k_norm, v_swa_q_norm, v_swa_k_norm, v_swa_sinks, v_fox_out_norm, v_swa_out_norm, v_w_out, v_ffn2_norm, v_ffn2_w_gate, v_ffn2_w_up, v_ffn2_w_down):
    weights = dict(meta_tokens=meta_tokens, ffn1_norm=ffn1_norm, ffn1_w_gate=ffn1_w_gate, ffn1_w_up=ffn1_w_up,
                   ffn1_w_down=ffn1_w_down, mix_norm=mix_norm, w_in=w_in, b_forget=b_forget, fox_q_norm=fox_q_norm,
                   fox_k_norm=fox_k_norm, swa_q_norm=swa_q_norm, swa_k_norm=swa_k_norm, swa_sinks=swa_sinks,
                   fox_out_norm=fox_out_norm, swa_out_norm=swa_out_norm, w_out=w_out, ffn2_norm=ffn2_norm,
                   ffn2_w_gate=ffn2_w_gate, ffn2_w_up=ffn2_w_up, ffn2_w_down=ffn2_w_down)
    mom_m = dict(meta_tokens=m_meta_tokens, ffn1_norm=m_ffn1_norm, ffn1_w_gate=m_ffn1_w_gate, ffn1_w_up=m_ffn1_w_up,
                 ffn1_w_down=m_ffn1_w_down, mix_norm=m_mix_norm, w_in=m_w_in, b_forget=m_b_forget,
                 fox_q_norm=m_fox_q_norm, fox_k_norm=m_fox_k_norm, swa_q_norm=m_swa_q_norm, swa_k_norm=m_swa_k_norm,
                 swa_sinks=m_swa_sinks, fox_out_norm=m_fox_out_norm, swa_out_norm=m_swa_out_norm, w_out=m_w_out,
                 ffn2_norm=m_ffn2_norm, ffn2_w_gate=m_ffn2_w_gate, ffn2_w_up=m_ffn2_w_up, ffn2_w_down=m_ffn2_w_down)
    mom_v = dict(meta_tokens=v_meta_tokens, ffn1_norm=v_ffn1_norm, ffn1_w_gate=v_ffn1_w_gate, ffn1_w_up=v_ffn1_w_up,
                 ffn1_w_down=v_ffn1_w_down, mix_norm=v_mix_norm, w_in=v_w_in, b_forget=v_b_forget,
                 fox_q_norm=v_fox_q_norm, fox_k_norm=v_fox_k_norm, swa_q_norm=v_swa_q_norm, swa_k_norm=v_swa_k_norm,
                 swa_sinks=v_swa_sinks, fox_out_norm=v_fox_out_norm, swa_out_norm=v_swa_out_norm, w_out=v_w_out,
                 ffn2_norm=v_ffn2_norm, ffn2_w_gate=v_ffn2_w_gate, ffn2_w_up=v_ffn2_w_up, ffn2_w_down=v_ffn2_w_down)
    names = list(weights)

    _, S, D = x.shape
    depth = ffn1_norm.shape[0]
    hf, hs = b_forget.shape[1], swa_sinks.shape[1]
    U = w_in.shape[2] * N_DEV
    hkv = (U - 3 * HEAD_DIM * hf - hf - HEAD_DIM * hs) // (2 * HEAD_DIM)
    cfg = _Cfg(S, D, hf, hs, hkv)
    n_meta = meta_tokens.shape[0]
    assert n_meta == cfg.n_meta
    x_idx, y_idx, c_idx = _place()
    chip_idx = 2 * x_idx + y_idx
    dev_idx = 2 * chip_idx + c_idx
    core_s = jnp.reshape(c_idx, (1,)).astype(jnp.int32)
    chip_s = jnp.reshape(chip_idx, (1,)).astype(jnp.int32)

    col_sharded = ("ffn1_w_gate", "ffn1_w_up", "w_in", "ffn2_w_gate", "ffn2_w_up")
    use_order = ("ffn1_w_gate", "ffn1_w_up", "ffn1_w_down", "w_in", "w_out", "ffn2_w_gate", "ffn2_w_up", "ffn2_w_down")

    layer_axis = {k: (1 if k == "w_in" else 0) for k in use_order}

    def rows_view(k, a):
        if k not in col_sharded:
            return a
        return jnp.transpose(a, (2, 0, 1)) if k == "w_in" else jnp.swapaxes(a, 1, 2)

    def from_rows_view(k, a):
        if k not in col_sharded:
            return a
        return jnp.transpose(a, (1, 2, 0)) if k == "w_in" else jnp.swapaxes(a, 1, 2)

    w_rows = {k: rows_view(k, weights[k]) for k in use_order}

    def shard(key):
        k, l = key
        if k == "meta_tokens":
            return meta_tokens
        return _shard_bf16("weight_shard", w_rows[k], l, layer_axis[k] if k in col_sharded else 0)

    waiting = [("meta_tokens", 0)] + [(k, l) for l in range(depth) for k in use_order]
    after_first, after_second = [], []
    gathered = {}

    def fwd_carry(n_first):
        cy = _Carry()
        third = [(key, _gather_third(cy, buf)) for key, buf in after_second]
        second = [(key, _gather_second(cy, buf)) for key, buf in after_first]
        first = [(key, _gather_first(cy, shard(key))) for key in waiting[:n_first]]
        del waiting[:n_first]
        after_first.clear()
        after_second.clear()
        return cy, (first, second, third)

    def fwd_absorb(extra, plan):
        first, second, third = plan
        for key, idx in third:
            gathered[key] = extra[idx]
        for key, idx in second:
            after_second.append((key, extra[idx]))
        for key, idx in first:
            after_first.append((key, extra[idx]))

    def weight(k, l):
        key = (k, l)
        while key not in gathered:
            n = waiting.index(key) + 1 if key in waiting else 0
            cy, plan = fwd_carry(n)
            fwd_absorb(_comm_only("weights_gather", cy), plan)
        g = gathered[key]
        return g.reshape(-1, g.shape[-1])

    weight("w_in", 0)
    meta_full = jnp.swapaxes(weight("meta_tokens", 0).reshape(N_DEV, n_meta, -1), 0, 1).reshape(n_meta, D)
    slopes = jnp.asarray(2.0 ** (-8.0 * np.arange(1, hs + 1) / hs), dtype=F32)

    h = jnp.concatenate([jnp.zeros((cfg.pad, D), F32), meta_full, x[0]], axis=0)
    saved = []
    w_in_p = [None] * depth

    def mm_f(name, a, b, n_first=1, **kw):
        cy, plan = fwd_carry(n_first)
        out, extra = _matmul(name, a, b, carry=cy, **kw)
        fwd_absorb(extra, plan)
        return out

    def ffn_fwd(tag, l, h_in, norm, wg, wu, wd):
        xn = _rmsnorm_fwd(f"{tag}_norm", h_in, norm[l][None])
        wg_t, wu_t = weight(wg, l), weight(wu, l)
        cy, plan = fwd_carry(1)
        (gate, up, act), extra = _ffn_up(f"{tag}_up", xn, wg_t, wu_t, carry=cy)
        fwd_absorb(extra, plan)
        h_out = mm_f(f"{tag}_down", act, weight(wd, l), scale=0.5, residual=h_in, tm=544, tn=1024, tk=2816)
        return h_out, (xn, gate, up)

    for l in range(depth):
        st = {"h0": h}
        h, st["ffn1"] = ffn_fwd("ffn1", l, h, ffn1_norm, "ffn1_w_gate", "ffn1_w_up", "ffn1_w_down")
        st["h1"] = h
        xn = _rmsnorm_fwd("mix_norm", h, mix_norm[l][None])
        w_in_p[l] = _permute_w_in(cfg, weight("w_in", l))
        u = mm_f("mix_in", xn, w_in_p[l], trans_b=True, tm=544, tn=640, tk=2048)
        gq, gk = _pair_gain(fox_q_norm[l]), _pair_gain(fox_k_norm[l])
        gsq, gsk = _pair_gain(swa_q_norm[l]), _pair_gain(swa_k_norm[l])
        bias = _scatter_heads(cfg, b_forget[l])
        qn, kn, fv, sqn, skd, svd, c, ct = _mix_prep("mix_prep", cfg, u, gq, gk, gsq, gsk, bias)
        cy, plan = fwd_carry(2)
        (o_fox, lse), extra = _fox_fwd("fox_fwd", cfg, qn, kn, fv, c, ct, carry=cy)
        fwd_absorb(extra, plan)
        cy, plan = fwd_carry(1)
        (o_swa,), extra = _swa_fwd("swa_fwd", cfg, sqn, skd, svd, swa_sinks[l], slopes, carry=cy)
        fwd_absorb(extra, plan)
        o_cat = _out_norm("out_norm", cfg, o_fox, o_swa, fox_out_norm[l][None], swa_out_norm[l][None])
        st["mix"] = (xn, u, gq, gk, gsq, gsk, bias, qn, kn, fv, sqn, skd, svd, c, ct, o_fox, lse, o_swa, o_cat)
        h = mm_f("mix_out", o_cat, weight("w_out", l), n_first=0, residual=h, tm=544, tn=1024, tk=2048)
        st["h2"] = h
        h, st["ffn2"] = ffn_fwd("ffn2", l, h, ffn2_norm, "ffn2_w_gate", "ffn2_w_up", "ffn2_w_down")
        saved.append(st)

    loss_blk, dh, dh16 = _loss_head("loss_head", h, loss_target[0])

    small = {k: [None] * depth for k in names if k not in use_order and k != "meta_tokens"}
    parts, landing = {}, {}
    to_sibling, to_chips = [], []

    def bwd_carry(n_chips):
        cy = _Carry()
        t1, t3 = [], []
        for k, l in list(to_chips):
            if len(t3) < n_chips and all(k != k3 for k3, _ in t3):
                to_chips.remove((k, l))
                t3.append((k, _scatter_chips(cy, parts[k], landing[k], l)))
        while to_sibling:
            k, l, g = to_sibling.pop(0)
            t1.append((k, l, g, _scatter_sibling(cy, g)))
        return cy, (t1, t3)

    def bwd_absorb(extra, plan):
        t1, t3 = plan
        for k, idx in t3:
            landing[k] = extra[idx]
        for k, l, g, idx in t1:
            parts[k] = _pair_add("grads_pair_add", g, extra[idx], parts[k], l, core_s)
            to_chips.append((k, l))

    def emit_grad(k, l, dw):
        r, C = dw.shape[0] // N_DEV, dw.shape[1]
        if k not in parts:
            parts[k] = lax.empty((depth, 4, r, C), BF16)
            landing[k] = lax.empty((3, depth, r, C), BF16)
        to_sibling.append((k, l, dw.reshape(4, 2, r, C)))

    def mm_b(name, a, b, n_chips=0, **kw):
        cy, plan = bwd_carry(n_chips)
        out, extra = _matmul(name, a, b, carry=cy, **kw)
        bwd_absorb(extra, plan)
        return out

    def ffn_bwd(tag, l, dh_out, dh_out16, h_in, st_, norm, wg, wu, wd):
        xn, gate, up = st_
        short = 1 if l == 0 else 0
        cy, plan = bwd_carry(1)
        (dgate, dup, act), extra = _ffn_bwd_act(f"{tag}_dact", dh_out16, weight(wd, l), gate, up, carry=cy)
        bwd_absorb(extra, plan)
        emit_grad(wd, l, mm_b(f"{tag}_dwd", act, dh_out16, short, trans_a=True, scale=0.5, out_dtype=BF16,
                              tm=512, tn=1024, tk=2176))
        emit_grad(wg, l, mm_b(f"{tag}_dwg", dgate, xn, short, trans_a=True, out_dtype=BF16, tm=512, tn=1024, tk=2176))
        emit_grad(wu, l, mm_b(f"{tag}_dwu", dup, xn, short, trans_a=True, out_dtype=BF16, tm=512, tn=1024, tk=2176))
        dxn = mm_b(f"{tag}_dxn", dgate, weight(wg, l), 1, pair2=(dup, weight(wu, l)), tm=544, tn=1024, tk=1408)
        dh_in, dh_in16, dg = _rmsnorm_bwd(f"{tag}_dnorm", dxn, h_in, norm[l][None], dh_out)
        return dh_in, dh_in16, dg[0]

    for l in reversed(range(depth)):
        st = saved[l]
        dh, dh16, small["ffn2_norm"][l] = ffn_bwd("ffn2", l, dh, dh16, st["h2"], st["ffn2"], ffn2_norm,
                                                   "ffn2_w_gate", "ffn2_w_up", "ffn2_w_down")
        xn, u, gq, gk, gsq, gsk, bias, qn, kn, fv, sqn, skd, svd, c, ct, o_fox, lse, o_swa, o_cat = st["mix"]
        dcat = mm_b("mix_dcat", dh16, weight("w_out", l), int(l == 0), trans_b=True, tm=544, tn=1024, tk=2048)
        emit_grad("w_out", l, mm_b("mix_dwout", o_cat, dh16, int(l == 0), trans_a=True, out_dtype=BF16, tm=512, tn=1024, tk=2176))
        do_fox, do_swa, dgf, dgs = _out_norm_bwd("out_norm_bwd", cfg, dcat, o_fox, o_swa,
                                                 fox_out_norm[l][None], swa_out_norm[l][None])
        small["fox_out_norm"][l], small["swa_out_norm"][l] = dgf[0], dgs[0]
        cy, plan = bwd_carry(3)
        (dqn, dkn, dfv, dct, dcq), extra = _fox_bwd("fox_bwd", cfg, qn, kn, fv, c, ct, o_fox, lse, do_fox, carry=cy)
        bwd_absorb(extra, plan)
        cy, plan = bwd_carry(1)
        (dsqn, dskd, dsvd, dsink), extra = _swa_bwd("swa_bwd", cfg, sqn, skd, svd, swa_sinks[l], slopes, o_swa, do_swa,
                                                    carry=cy)
        bwd_absorb(extra, plan)
        small["swa_sinks"][l] = dsink.reshape(hs // 2, 8, LANES)[:, 0, ::HEAD_DIM].reshape(hs)
        du, dgq, dgk, dgsq, dgsk, db = _mix_prep_bwd("mix_prep_bwd", cfg, u, gq, gk, gsq, gsk, bias,
                                                     dqn, dkn, dfv, dsqn, dskd, dsvd, dct, dcq)
        small["fox_q_norm"][l], small["fox_k_norm"][l] = _fold_pair(dgq), _fold_pair(dgk)
        small["swa_q_norm"][l], small["swa_k_norm"][l] = _fold_pair(dgsq), _fold_pair(dgsk)
        small["b_forget"][l] = _gather_heads(cfg, db[0])
        dwp = mm_b("mix_dwin", du, xn, int(l == 0), trans_a=True, out_dtype=BF16, tm=640, tn=1024, tk=2176)
        emit_grad("w_in", l, _unpermute_dw_in(cfg, dwp))
        dxn = mm_b("mix_dxn", du, w_in_p[l], int(l == 0), tm=544, tn=1024, tk=4480)
        dh, dh16, dg = _rmsnorm_bwd("mix_dnorm", dxn, st["h1"], mix_norm[l][None], dh)
        small["mix_norm"][l] = dg[0]
        dh, dh16, small["ffn1_norm"][l] = ffn_bwd("ffn1", l, dh, dh16, st["h0"], st["ffn1"], ffn1_norm,
                                                   "ffn1_w_gate", "ffn1_w_up", "ffn1_w_down")

    grad_x = dh[BLOCK:][None]
    dmeta = dh[cfg.pad:BLOCK]

    while to_sibling or to_chips:
        cy, plan = bwd_carry(len(use_order))
        bwd_absorb(_comm_only("grads_scatter", cy), plan)

    grads = {}
    for k in use_order:
        grads[k] = _chip_sum("grads_chip_sum", parts[k], landing[k], chip_s, layer_axis[k] if k in col_sharded else 0)

    small_names = list(small)
    pieces = [loss_blk[0, :1], dmeta.reshape(-1)] + [jnp.stack(small[k]).reshape(-1) for k in small_names]
    sizes = [int(p.shape[0]) for p in pieces]
    total = sum(sizes)
    padded = -(-total // (8 * LANES)) * (8 * LANES)
    vec = jnp.concatenate(pieces + [jnp.zeros((padded - total,), F32)]).reshape(-1, LANES)
    summed = _gather_sum("small_gather_sum", vec).reshape(-1)
    offs = np.cumsum([0] + sizes)
    loss = summed[0]
    dmeta_full = summed[offs[1]:offs[2]].reshape(n_meta, D)
    mcols = meta_tokens.shape[1]
    grads["meta_tokens"] = lax.dynamic_slice_in_dim(dmeta_full, dev_idx * mcols, mcols, axis=1)
    for n_, k in enumerate(small_names):
        grads[k] = summed[offs[2 + n_]:offs[3 + n_]].reshape(weights[k].shape)

    delta, new_m, new_v = {}, {}, {}
    for k in names:
        if k in col_sharded:
            outs = _adamw_nd("adamw", grads[k], w_rows[k], rows_view(k, mom_m[k]), rows_view(k, mom_v[k]))
            delta[k], new_m[k], new_v[k] = (from_rows_view(k, o) for o in outs)
            grads[k] = from_rows_view(k, grads[k])
        else:
            delta[k], new_m[k], new_v[k] = _adamw_nd("adamw", grads[k], weights[k], mom_m[k], mom_v[k])

    return (loss, grad_x, *[grads[k] for k in names], *[delta[k] for k in names],
            *[new_m[k] for k in names], *[new_v[k] for k in names])
```

```python
import functools

import numpy as np
import jax
import jax.numpy as jnp
from jax import lax
from jax.experimental import pallas as pl
from jax.experimental.pallas import tpu as pltpu

F32 = jnp.float32
BF16 = jnp.bfloat16
MESH = pl.DeviceIdType.MESH

HEAD_DIM = 64
BLOCK = 128
LANES = 128
N_DEV = 8
EPS = 1e-6
NEG_INF = -1e30
SCALE = HEAD_DIM ** -0.5

ADAM_LR = 0.001
ADAM_B1 = 0.9
ADAM_B2 = 0.999
ADAM_EPS = 1e-08
ADAM_WD = 0.01
ADAM_STEP = 10

VMEM_BYTES_V7X = 64 * 1024 * 1024
VMEM_LIMIT = VMEM_BYTES_V7X * 3 // 4

NT = (((1,), (1,)), ((), ()))
TN = (((0,), (0,)), ((), ()))
HI = lax.Precision.HIGHEST


def _cparams(sem=None, vmem=VMEM_LIMIT):
    return pltpu.CompilerParams(dimension_semantics=sem, vmem_limit_bytes=vmem)


def _tile(n, pref, mult):
    best = None
    for t in range(mult, min(n, pref) + 1, mult):
        if n % t == 0:
            best = t
    return best if best is not None else n


def _dot(a, b):
    return jnp.dot(a, b, preferred_element_type=F32)


def _dot_nt(a, b):
    return lax.dot_general(a, b, NT, preferred_element_type=F32)


def _dot_tn(a, b):
    return lax.dot_general(a, b, TN, preferred_element_type=F32)


def _lane(shape):
    return lax.broadcasted_iota(jnp.int32, shape, len(shape) - 1)


def _half_sum(x, lo):
    s0 = jnp.sum(jnp.where(lo, x, 0.0), axis=1, keepdims=True)
    s1 = jnp.sum(jnp.where(lo, 0.0, x), axis=1, keepdims=True)
    return jnp.where(lo, s0, s1)


def _sigmoid(x):
    return 1.0 / (1.0 + jnp.exp(-x))


def _place():
    return lax.axis_index("x"), lax.axis_index("y"), lax.axis_index("c")


def _peers(x, y, c):
    return [(x, y, 1 - c), (1 - x, y, c), (x, 1 - y, c), (1 - x, 1 - y, c)]


def _dev_index(dev):
    px, py, pc = dev
    return 4 * px + 2 * py + pc


class _Carry:
    def __init__(self):
        self.ins, self.outs, self.alias, self.items = [], [], {}, []
        self.nsem = self.nloc = 0

    def add(self, ins, outs, alias, nsem, nloc, build):
        i0, o0 = len(self.ins), len(self.outs)
        for src, dst in alias.items():
            self.alias[i0 + src] = o0 + dst
        self.items.append((i0, len(ins), o0, len(outs), self.nsem, self.nloc, build))
        self.ins += ins
        self.outs += outs
        self.nsem += nsem
        self.nloc += nloc
        return list(range(o0, o0 + len(outs)))

    def build(self, in_refs, out_refs, ssem, rsem, lsem):
        ops = []
        for i0, ni, o0, no, s0, l0, fn in self.items:
            ops.append(fn(in_refs[i0:i0 + ni], out_refs[o0:o0 + no],
                          lambda k, s0=s0: (ssem.at[s0 + k], rsem.at[s0 + k]), lambda k, l0=l0: lsem.at[l0 + k]))
        return ops


def _remote(src, dst, sems, dev):
    return pltpu.make_async_remote_copy(src_ref=src, dst_ref=dst, send_sem=sems[0], recv_sem=sems[1],
                                        device_id=dev, device_id_type=MESH)


def _gather_first(carry, shard):
    def build(ins, outs, sems, locs):
        src, buf = ins[0], outs[0]
        x, y, c = _place()
        me = _dev_index((x, y, c))
        peers = _peers(x, y, c)[:3]
        local = pltpu.make_async_copy(src, buf.at[me], locs(0))
        sends = [_remote(src, buf.at[me], sems(k), dev) for k, dev in enumerate(peers)]
        recvs = [_remote(src, buf.at[_dev_index(dev)], sems(k), dev) for k, dev in enumerate(peers)]

        def start():
            local.start()
            for cp in sends:
                cp.start()

        def wait():
            for cp in sends:
                cp.wait_send()
            for cp in recvs:
                cp.wait_recv()
            local.wait()

        return start, wait

    return carry.add([shard], [jax.ShapeDtypeStruct((N_DEV,) + shard.shape, shard.dtype)], {}, 3, 1, build)[0]


def _gather_second(carry, buf):
    def build(ins, outs, sems, locs):
        b = outs[0]
        x, y, c = _place()
        sibling, xn, yn, diag = _peers(x, y, c)
        relay_src = _dev_index((1 - x, y, c)) * (1 - c) + _dev_index((x, 1 - y, c)) * c
        relay_to = (x * (1 - c) + (1 - x) * c, (1 - y) * (1 - c) + y * c, c)
        sends = [_remote(b.at[relay_src], b.at[relay_src], sems(0), relay_to)]
        recvs = [_remote(b.at[relay_src], b.at[_dev_index(diag)], sems(0), relay_to)]
        for k, dev in enumerate((xn, yn)):
            sends.append(_remote(b.at[_dev_index(dev)], b.at[_dev_index(dev)], sems(1 + k), sibling))
            recvs.append(_remote(b.at[_dev_index(dev)], b.at[_dev_index((dev[0], dev[1], 1 - c))], sems(1 + k), sibling))

        def start():
            for cp in sends:
                cp.start()

        def wait():
            for cp in sends:
                cp.wait_send()
            for cp in recvs:
                cp.wait_recv()

        return start, wait

    return carry.add([buf], [jax.ShapeDtypeStruct(buf.shape, buf.dtype)], {0: 0}, 3, 0, build)[0]


def _gather_third(carry, buf):
    def build(ins, outs, sems, locs):
        b = outs[0]
        x, y, c = _place()
        sibling, _xn, _yn, diag = _peers(x, y, c)
        send = _remote(b.at[_dev_index(diag)], b.at[_dev_index(diag)], sems(0), sibling)
        recv = _remote(b.at[_dev_index(diag)], b.at[_dev_index((diag[0], diag[1], 1 - c))], sems(0), sibling)

        def wait():
            send.wait_send()
            recv.wait_recv()

        return send.start, wait

    return carry.add([buf], [jax.ShapeDtypeStruct(buf.shape, buf.dtype)], {0: 0}, 1, 0, build)[0]


def _scatter_sibling(carry, grad):
    def build(ins, outs, sems, locs):
        x, y, c = _place()
        cp = _remote(ins[0].at[:, 1 - c], outs[0], sems(0), (x, y, 1 - c))
        return cp.start, cp.wait

    shape = (grad.shape[0],) + grad.shape[2:]
    return carry.add([grad], [jax.ShapeDtypeStruct(shape, grad.dtype)], {}, 1, 0, build)[0]


def _scatter_chips(carry, parts, landing, layer):
    def build(ins, outs, sems, locs):
        x, y, c = _place()
        cps = [_remote(ins[0].at[layer, 2 * dev[0] + dev[1]], outs[0].at[k, layer], sems(k), dev)
               for k, dev in enumerate(_peers(x, y, c)[1:])]

        def start():
            for cp in cps:
                cp.start()

        def wait():
            for cp in cps:
                cp.wait()

        return start, wait

    return carry.add([parts, landing], [jax.ShapeDtypeStruct(landing.shape, landing.dtype)], {1: 0}, 3, 0, build)[0]


def _call(name, body, grid, in_specs, out_specs, out_shape, args, scratch=(), carry=None):
    ni, no, ns = len(args), len(out_shape), len(scratch)
    if carry is None or not carry.items:
        res = pl.pallas_call(
            body, name=name, grid=grid, in_specs=list(in_specs), out_specs=list(out_specs), out_shape=list(out_shape),
            scratch_shapes=list(scratch), compiler_params=_cparams(("arbitrary",) * len(grid)))(*args)
        return list(res), []
    nci, nco = len(carry.ins), len(carry.outs)

    def full_body(*refs):
        c_in = refs[ni:ni + nci]
        c_out = refs[ni + nci + no:ni + nci + no + nco]
        sc = refs[ni + nci + no + nco:]
        ops = carry.build(c_in, c_out, sc[ns], sc[ns + 1], sc[ns + 2])
        first = last = None
        for d, n in enumerate(grid):
            pid = pl.program_id(d)
            first = (pid == 0) if first is None else first & (pid == 0)
            last = (pid == n - 1) if last is None else last & (pid == n - 1)

        @pl.when(first)
        def _():
            for start, _w in ops:
                start()

        body(*refs[:ni], *refs[ni + nci:ni + nci + no], *sc[:ns])

        @pl.when(last)
        def _():
            for _s, wait in ops:
                wait()

    hbm = pl.BlockSpec(memory_space=pl.ANY)
    res = pl.pallas_call(
        full_body, name=name, grid=grid,
        in_specs=list(in_specs) + [hbm] * nci, out_specs=list(out_specs) + [hbm] * nco,
        out_shape=list(out_shape) + list(carry.outs),
        input_output_aliases={ni + s: no + d for s, d in carry.alias.items()},
        scratch_shapes=list(scratch) + [pltpu.SemaphoreType.DMA((carry.nsem,)), pltpu.SemaphoreType.DMA((carry.nsem,)),
                                        pltpu.SemaphoreType.DMA((max(carry.nloc, 1),))],
        compiler_params=_cparams(("arbitrary",) * len(grid)))(*args, *carry.ins)
    return list(res[:no]), list(res[no:])


def _comm_only(name, carry):
    return _call(name, lambda *refs: None, (1,), [], [], [], [], carry=carry)[1]


def _matmul(name, a, b, *, pair2=None, trans_a=False, trans_b=False, out_dtype=F32, scale=None, residual=None,
            tm=512, tn=512, tk=512, carry=None):
    if trans_a:
        K, M = a.shape
    else:
        M, K = a.shape
    if trans_b:
        N, Kb = b.shape
    else:
        Kb, N = b.shape
    assert K == Kb, (name, a.shape, b.shape)
    tm = _tile(M, tm, LANES if trans_a else 16)
    tn = _tile(N, tn, LANES)
    tk = _tile(K, tk, 16 if (trans_a and not trans_b) else LANES)
    nk = K // tk
    dims = (((0 if trans_a else 1,), (1 if trans_b else 0,)), ((), ()))
    pairs = [(a, b)] + ([pair2] if pair2 is not None else [])
    npair = len(pairs)

    def body(*refs):
        ab = refs[:2 * npair]
        pos = 2 * npair
        r_ref = None
        if residual is not None:
            r_ref = refs[pos]
            pos += 1
        o_ref = refs[pos]

        def partial():
            t = None
            for q in range(npair):
                d = lax.dot_general(ab[2 * q][...].astype(BF16), ab[2 * q + 1][...].astype(BF16), dims,
                                    preferred_element_type=F32)
                t = d if t is None else t + d
            return t

        def finish(r):
            if scale is not None:
                r = r * scale
            if r_ref is not None:
                r = r + r_ref[...].astype(F32)
            o_ref[...] = r.astype(o_ref.dtype)

        if nk == 1:
            finish(partial())
            return
        acc_ref = refs[pos + 1]
        k = pl.program_id(2)

        @pl.when(k == 0)
        def _():
            acc_ref[...] = partial()

        @pl.when(k > 0)
        def _():
            acc_ref[...] += partial()

        @pl.when(k == nk - 1)
        def _():
            finish(acc_ref[...])

    a_spec = pl.BlockSpec((tk, tm), lambda i, j, k: (k, i)) if trans_a else pl.BlockSpec((tm, tk), lambda i, j, k: (i, k))
    b_spec = pl.BlockSpec((tn, tk), lambda i, j, k: (j, k)) if trans_b else pl.BlockSpec((tk, tn), lambda i, j, k: (k, j))
    in_specs, args = [], []
    for pa, pb in pairs:
        in_specs += [a_spec, b_spec]
        args += [pa, pb]
    if residual is not None:
        in_specs.append(pl.BlockSpec((tm, tn), lambda i, j, k: (i, j)))
        args.append(residual)
    res, extra = _call(
        name, body, (M // tm, N // tn, nk), in_specs, [pl.BlockSpec((tm, tn), lambda i, j, k: (i, j))],
        [jax.ShapeDtypeStruct((M, N), out_dtype)], args,
        scratch=[pltpu.VMEM((tm, tn), F32)] if nk > 1 else [], carry=carry)
    return res[0], extra


def _shard_bf16(name, w, layer, layer_axis):
    R, C = w.shape[1 - layer_axis], w.shape[2]
    tr = _tile(R, 512, 16)

    def body(x_ref, o_ref):
        o_ref[...] = x_ref[...].astype(BF16)

    if layer_axis == 0:
        in_spec = pl.BlockSpec((None, tr, C), lambda i: (layer, i, 0))
    else:
        in_spec = pl.BlockSpec((tr, C), lambda i: (i, layer))
        w = w.reshape(R, -1)
    return pl.pallas_call(
        body, name=name, grid=(R // tr,),
        in_specs=[in_spec], out_specs=pl.BlockSpec((tr, C), lambda i: (i, 0)),
        out_shape=jax.ShapeDtypeStruct((R, C), BF16),
        compiler_params=_cparams(("parallel",)),
    )(w)


def _rmsnorm_fwd(name, h, g):
    L, D = h.shape
    tm = _tile(L, 256, 16)

    def body(h_ref, g_ref, o_ref):
        x = h_ref[...]
        r = lax.rsqrt(jnp.mean(x * x, axis=1, keepdims=True) + EPS)
        o_ref[...] = (x * r * g_ref[...]).astype(o_ref.dtype)

    return pl.pallas_call(
        body, name=name, grid=(L // tm,),
        in_specs=[pl.BlockSpec((tm, D), lambda i: (i, 0)), pl.BlockSpec((1, D), lambda i: (0, 0))],
        out_specs=pl.BlockSpec((tm, D), lambda i: (i, 0)),
        out_shape=jax.ShapeDtypeStruct((L, D), BF16),
        compiler_params=_cparams(("parallel",)),
    )(h, g)


def _rmsnorm_bwd(name, dy, h, g, dres):
    L, D = h.shape
    tm = _tile(L, 256, 16)

    def body(dy_ref, h_ref, g_ref, dres_ref, dh_ref, dh16_ref, dg_ref):
        i = pl.program_id(0)
        x = h_ref[...]
        dyv = dy_ref[...].astype(F32)
        r = lax.rsqrt(jnp.mean(x * x, axis=1, keepdims=True) + EPS)
        w = dyv * g_ref[...]
        proj = jnp.sum(w * x, axis=1, keepdims=True) * (1.0 / D)
        dh = dres_ref[...] + r * w - x * (r * r * r * proj)
        dh_ref[...] = dh
        dh16_ref[...] = dh.astype(BF16)

        @pl.when(i == 0)
        def _():
            dg_ref[...] = jnp.zeros_like(dg_ref)

        dg_ref[...] += jnp.sum(dyv * x * r, axis=0, keepdims=True)

    return pl.pallas_call(
        body, name=name, grid=(L // tm,),
        in_specs=[pl.BlockSpec((tm, D), lambda i: (i, 0)), pl.BlockSpec((tm, D), lambda i: (i, 0)),
                  pl.BlockSpec((1, D), lambda i: (0, 0)), pl.BlockSpec((tm, D), lambda i: (i, 0))],
        out_specs=[pl.BlockSpec((tm, D), lambda i: (i, 0)), pl.BlockSpec((tm, D), lambda i: (i, 0)),
                   pl.BlockSpec((1, D), lambda i: (0, 0))],
        out_shape=[jax.ShapeDtypeStruct((L, D), F32), jax.ShapeDtypeStruct((L, D), BF16),
                   jax.ShapeDtypeStruct((1, D), F32)],
        compiler_params=_cparams(("arbitrary",)),
    )(dy, h, g, dres)


def _ffn_up(name, xn, wgT, wuT, carry=None):
    L, D = xn.shape
    F = wgT.shape[0]
    tm = _tile(L, 1088, 16)
    tn = _tile(F, 512, LANES)

    def body(x_ref, wg_ref, wu_ref, g_ref, u_ref, a_ref):
        x = x_ref[...]
        g = _dot_nt(x, wg_ref[...])
        u = _dot_nt(x, wu_ref[...])
        g_ref[...] = g.astype(BF16)
        u_ref[...] = u.astype(BF16)
        a_ref[...] = (g * _sigmoid(g) * u).astype(BF16)

    o_spec = pl.BlockSpec((tm, tn), lambda i, j: (i, j))
    o_shape = jax.ShapeDtypeStruct((L, F), BF16)
    return _call(
        name, body, (L // tm, F // tn),
        [pl.BlockSpec((tm, D), lambda i, j: (i, 0)), pl.BlockSpec((tn, D), lambda i, j: (j, 0)),
         pl.BlockSpec((tn, D), lambda i, j: (j, 0))],
        [o_spec, o_spec, o_spec], [o_shape, o_shape, o_shape], [xn, wgT, wuT], carry=carry)


def _ffn_bwd_act(name, dh, wd, gate, up, carry=None):
    L, D = dh.shape
    F = wd.shape[0]
    tm = _tile(L, 1088, 16)
    tn = _tile(F, 512, LANES)

    def body(dh_ref, wd_ref, g_ref, u_ref, dg_ref, du_ref, a_ref):
        da = 0.5 * _dot_nt(dh_ref[...].astype(BF16), wd_ref[...])
        g = g_ref[...].astype(F32)
        u = u_ref[...].astype(F32)
        sg = _sigmoid(g)
        silu = g * sg
        dg_ref[...] = (da * u * (sg * (1.0 + g * (1.0 - sg)))).astype(BF16)
        du_ref[...] = (da * silu).astype(BF16)
        a_ref[...] = (silu * u).astype(BF16)

    o_spec = pl.BlockSpec((tm, tn), lambda i, j: (i, j))
    o_shape = jax.ShapeDtypeStruct((L, F), BF16)
    return _call(
        name, body, (L // tm, F // tn),
        [pl.BlockSpec((tm, D), lambda i, j: (i, 0)), pl.BlockSpec((tn, D), lambda i, j: (j, 0)), o_spec, o_spec],
        [o_spec, o_spec, o_spec], [o_shape, o_shape, o_shape], [dh, wd, gate, up], carry=carry)


class _Cfg:
    def __init__(self, S, D, hf, hs, hkv):
        self.S, self.D, self.L = S, D, S + BLOCK
        self.hf, self.hs, self.hkv = hf, hs, hkv
        self.wf, self.ws = hf * HEAD_DIM, hs * HEAD_DIM
        self.group = hs // hkv
        self.cs = 2 * LANES // hf
        self.n_meta = 16
        self.pad = BLOCK - self.n_meta
        self.o_fk = self.wf
        self.o_fv = 2 * self.wf
        self.o_sq = 3 * self.wf
        self.o_sk = self.o_sq + self.ws
        self.o_sv = self.o_sk + LANES
        self.o_fz = self.o_sv + LANES
        self.up = self.o_fz + LANES
        assert hkv == 2 and hf % 2 == 0 and self.group % 2 == 0 and self.cs % 8 == 0
        assert self.o_sq % self.ws == 0 and (2 * self.wf) % LANES == 0


def _head_norm(x, gain, lo, mult):
    r = lax.rsqrt(_half_sum(x * x, lo) * (1.0 / HEAD_DIM) + EPS)
    return x * r * (gain * mult)


def _head_norm_bwd(dy, x, gain, lo, mult):
    r = lax.rsqrt(_half_sum(x * x, lo) * (1.0 / HEAD_DIM) + EPS)
    w = dy * (gain * mult)
    proj = _half_sum(w * x, lo) * (1.0 / HEAD_DIM)
    dx = r * w - x * (r * r * r * proj)
    dgain = jnp.sum(dy * mult * x * r, axis=0, keepdims=True)
    return dx, dgain


def _dup(x, lo):
    xr = pltpu.roll(x, 64, 1)
    return jnp.where(lo, x, xr), jnp.where(lo, xr, x)


def _mix_prep(name, cfg, u, gq, gk, gsq, gsk, bias):
    L, wf, ws = cfg.L, cfg.wf, cfg.ws
    T = BLOCK
    npf, nps = wf // LANES, ws // LANES

    def body(fqk_ref, fv_ref, sq_ref, sk_ref, sv_ref, fz_ref, gq_ref, gk_ref, gsq_ref, gsk_ref, b_ref,
             qn_ref, kn_ref, fvo_ref, sqn_ref, skd_ref, svd_ref, c_ref, ct_ref, carry_ref):
        i = pl.program_id(0)
        lo = _lane((T, LANES)) < HEAD_DIM
        for p in range(npf):
            sl = slice(p * LANES, (p + 1) * LANES)
            qn_ref[:, sl] = _head_norm(fqk_ref[:, sl], gq_ref[...], lo, SCALE).astype(BF16)
            kn_ref[:, sl] = _head_norm(fqk_ref[:, wf + p * LANES: wf + (p + 1) * LANES], gk_ref[...], lo, 1.0).astype(BF16)
        fvo_ref[...] = fv_ref[...].astype(BF16)
        for p in range(nps):
            sl = slice(p * LANES, (p + 1) * LANES)
            sqn_ref[:, sl] = _head_norm(sq_ref[:, sl], gsq_ref[...], lo, SCALE).astype(BF16)
        k0, k1 = _dup(_head_norm(sk_ref[...], gsk_ref[...], lo, 1.0), lo)
        skd_ref[:, :LANES] = k0.astype(BF16)
        skd_ref[:, LANES:] = k1.astype(BF16)
        v0, v1 = _dup(sv_ref[...], lo)
        svd_ref[:, :LANES] = v0.astype(BF16)
        svd_ref[:, LANES:] = v1.astype(BF16)

        @pl.when(i == 0)
        def _():
            carry_ref[...] = jnp.zeros_like(carry_ref)

        z = fz_ref[...] + b_ref[...]
        lf = jnp.minimum(z, 0.0) - jnp.log(1.0 + jnp.exp(-jnp.abs(z)))
        row = lax.broadcasted_iota(jnp.int32, (T, T), 0)
        col = lax.broadcasted_iota(jnp.int32, (T, T), 1)
        tri = jnp.where(col <= row, 1.0, 0.0).astype(F32)
        c = jnp.dot(tri, lf, precision=HI, preferred_element_type=F32) + carry_ref[0:1, :]
        c_ref[...] = c
        ct_ref[...] = c.T
        carry_ref[0:1, :] = c_ref[T - 1:T, :]

    def rows(w, cb):
        return pl.BlockSpec((T, w), lambda i, cb=cb: (i, cb))

    vec = pl.BlockSpec((1, LANES), lambda i: (0, 0))
    return pl.pallas_call(
        body, name=name, grid=(L // T,),
        in_specs=[rows(2 * wf, 0), rows(wf, 2), rows(ws, cfg.o_sq // ws), rows(LANES, cfg.o_sk // LANES),
                  rows(LANES, cfg.o_sv // LANES), rows(LANES, cfg.o_fz // LANES), vec, vec, vec, vec, vec],
        out_specs=[rows(wf, 0), rows(wf, 0), rows(wf, 0), rows(ws, 0), rows(2 * LANES, 0), rows(2 * LANES, 0),
                   rows(LANES, 0), pl.BlockSpec((LANES, T), lambda i: (0, i))],
        out_shape=[jax.ShapeDtypeStruct((L, wf), BF16)] * 3 + [jax.ShapeDtypeStruct((L, ws), BF16)]
        + [jax.ShapeDtypeStruct((L, 2 * LANES), BF16)] * 2
        + [jax.ShapeDtypeStruct((L, LANES), F32), jax.ShapeDtypeStruct((LANES, L), F32)],
        scratch_shapes=[pltpu.VMEM((8, LANES), F32)],
        compiler_params=_cparams(("arbitrary",)),
    )(u, u, u, u, u, u, gq, gk, gsq, gsk, bias)


def _mix_prep_bwd(name, cfg, u, gq, gk, gsq, gsk, bias, dqn, dkn, dfv, dsqn, dskd, dsvd, dct, dcq):
    L, wf, ws = cfg.L, cfg.wf, cfg.ws
    T = BLOCK
    nb = L // T
    npf, nps = wf // LANES, ws // LANES

    def body(fqk_ref, sq_ref, sk_ref, fz_ref, gq_ref, gk_ref, gsq_ref, gsk_ref, b_ref,
             dqn_ref, dkn_ref, dfv_ref, dsqn_ref, dskd_ref, dsvd_ref, dct_ref, dcq_ref,
             du_ref, dgq_ref, dgk_ref, dgsq_ref, dgsk_ref, db_ref, carry_ref):
        i = pl.program_id(0)
        lo = _lane((T, LANES)) < HEAD_DIM

        @pl.when(i == 0)
        def _():
            carry_ref[...] = jnp.zeros_like(carry_ref)
            for r in (dgq_ref, dgk_ref, dgsq_ref, dgsk_ref, db_ref):
                r[...] = jnp.zeros_like(r)

        accq = jnp.zeros((1, LANES), F32)
        acck = jnp.zeros((1, LANES), F32)
        for p in range(npf):
            sl = slice(p * LANES, (p + 1) * LANES)
            dx, dg = _head_norm_bwd(dqn_ref[:, sl], fqk_ref[:, sl], gq_ref[...], lo, SCALE)
            du_ref[:, sl] = dx.astype(BF16)
            accq = accq + dg
            slk = slice(wf + p * LANES, wf + (p + 1) * LANES)
            dx, dg = _head_norm_bwd(dkn_ref[:, sl], fqk_ref[:, slk], gk_ref[...], lo, 1.0)
            du_ref[:, slk] = dx.astype(BF16)
            acck = acck + dg
        dgq_ref[...] += accq
        dgk_ref[...] += acck
        du_ref[:, cfg.o_fv:cfg.o_fv + wf] = dfv_ref[...].astype(BF16)
        accs = jnp.zeros((1, LANES), F32)
        for p in range(nps):
            sl = slice(p * LANES, (p + 1) * LANES)
            dx, dg = _head_norm_bwd(dsqn_ref[:, sl], sq_ref[:, sl], gsq_ref[...], lo, SCALE)
            du_ref[:, cfg.o_sq + p * LANES: cfg.o_sq + (p + 1) * LANES] = dx.astype(BF16)
            accs = accs + dg
        dgsq_ref[...] += accs

        def fold(ref):
            a0, a1 = ref[:, :LANES], ref[:, LANES:]
            return jnp.where(lo, a0 + pltpu.roll(a0, 64, 1), a1 + pltpu.roll(a1, 64, 1))

        dx, dg = _head_norm_bwd(fold(dskd_ref), sk_ref[...], gsk_ref[...], lo, 1.0)
        du_ref[:, cfg.o_sk:cfg.o_sk + LANES] = dx.astype(BF16)
        dgsk_ref[...] += dg
        du_ref[:, cfg.o_sv:cfg.o_sv + LANES] = fold(dsvd_ref).astype(BF16)

        dc = dct_ref[...].T + dcq_ref[...]
        row = lax.broadcasted_iota(jnp.int32, (T, T), 0)
        col = lax.broadcasted_iota(jnp.int32, (T, T), 1)
        triu = jnp.where(col >= row, 1.0, 0.0).astype(F32)
        dlf = jnp.dot(triu, dc, precision=HI, preferred_element_type=F32) + carry_ref[0:1, :]
        carry_ref[0:1, :] = dlf[0:1, :]
        z = fz_ref[...] + b_ref[...]
        dz = dlf * _sigmoid(-z)
        du_ref[:, cfg.o_fz:cfg.o_fz + LANES] = dz.astype(BF16)
        db_ref[...] += jnp.sum(dz, axis=0, keepdims=True)

    def rows(w, cb):
        return pl.BlockSpec((T, w), lambda i, cb=cb: (nb - 1 - i, cb))

    vec = pl.BlockSpec((1, LANES), lambda i: (0, 0))
    vshape = jax.ShapeDtypeStruct((1, LANES), F32)
    return pl.pallas_call(
        body, name=name, grid=(nb,),
        in_specs=[rows(2 * wf, 0), rows(ws, cfg.o_sq // ws), rows(LANES, cfg.o_sk // LANES),
                  rows(LANES, cfg.o_fz // LANES), vec, vec, vec, vec, vec,
                  rows(wf, 0), rows(wf, 0), rows(wf, 0), rows(ws, 0), rows(2 * LANES, 0), rows(2 * LANES, 0),
                  pl.BlockSpec((LANES, T), lambda i: (0, nb - 1 - i)), rows(LANES, 0)],
        out_specs=[rows(cfg.up, 0), vec, vec, vec, vec, vec],
        out_shape=[jax.ShapeDtypeStruct((L, cfg.up), BF16)] + [vshape] * 5,
        scratch_shapes=[pltpu.VMEM((8, LANES), F32)],
        compiler_params=_cparams(("arbitrary",)),
    )(u, u, u, u, gq, gk, gsq, gsk, bias, dqn, dkn, dfv, dsqn, dskd, dsvd, dct, dcq)


def _fox_fwd(name, cfg, qn, kn, fv, c, ct, carry=None):
    L, wf, cs = cfg.L, cfg.wf, cfg.cs
    TQ = BLOCK
    TK = _tile(L, 544, 8)
    npairs = wf // LANES
    pad = cfg.pad

    def body(q_ref, k_ref, v_ref, c_ref, ct_ref, o_ref, lse_ref):
        p = pl.program_id(0)
        i = pl.program_id(1)
        lo_q = _lane((TQ, LANES)) < HEAD_DIM
        lane_k = _lane((TK, LANES))
        lo_k = lane_k < HEAD_DIM
        lo_d = lax.broadcasted_iota(jnp.int32, (LANES, TQ), 0) < HEAD_DIM
        first = _lane((TK, 2 * TQ)) < TQ
        q = q_ref[...]
        qs = jnp.concatenate([jnp.where(lo_q, q, jnp.zeros_like(q)), jnp.where(lo_q, jnp.zeros_like(q), q)], axis=0)
        cq = jnp.concatenate([ct_ref[0:1, :], ct_ref[1:2, :]], axis=1)
        qrow = lax.broadcasted_iota(jnp.int32, (TK, 2 * TQ), 1)
        qpos = i * TQ + jnp.where(first, qrow, qrow - TQ)

        def step(j, carry_):
            m, l, acc = carry_
            off = pl.multiple_of(j * TK, 8)
            k = k_ref[pl.ds(off, TK), :]
            v = v_ref[pl.ds(off, TK), :]
            cblk = c_ref[pl.ds(off, TK), :]
            kpos = j * TK + lax.broadcasted_iota(jnp.int32, (TK, 2 * TQ), 0)
            allowed = (kpos <= qpos) & (kpos >= pad)
            ck0 = jnp.sum(jnp.where(lane_k == p * cs, cblk, 0.0), axis=1, keepdims=True)
            ck1 = jnp.sum(jnp.where(lane_k == p * cs + 1, cblk, 0.0), axis=1, keepdims=True)
            s = _dot_nt(k, qs) + cq - jnp.where(first, ck0, ck1)
            s = jnp.where(allowed, s, NEG_INF)
            m_new = jnp.maximum(m, jnp.max(s, axis=0, keepdims=True))
            alpha = jnp.exp(m - m_new)
            pr = jnp.exp(s - m_new)
            l = alpha * l + jnp.sum(pr, axis=0, keepdims=True)
            prb = pr.astype(BF16)
            prs = jnp.concatenate([prb[:, :TQ], prb[:, TQ:]], axis=0)
            vs = jnp.concatenate([jnp.where(lo_k, v, jnp.zeros_like(v)), jnp.where(lo_k, jnp.zeros_like(v), v)], axis=0)
            acc = acc * jnp.where(lo_d, alpha[:, :TQ], alpha[:, TQ:]) + _dot_tn(vs, prs)
            return m_new, l, acc

        init = (jnp.full((1, 2 * TQ), NEG_INF, F32), jnp.zeros((1, 2 * TQ), F32), jnp.zeros((LANES, TQ), F32))
        m, l, acc = lax.fori_loop(0, ((i + 1) * TQ + TK - 1) // TK, step, init)
        o_ref[...] = (acc / jnp.where(lo_d, l[:, :TQ], l[:, TQ:])).T
        lse = m + jnp.log(l)
        lse_ref[...] = jnp.where(lo_d, lse[:, :TQ], lse[:, TQ:]).T

    blk = pl.BlockSpec((TQ, LANES), lambda p, i: (i, p))
    full = pl.BlockSpec((L, LANES), lambda p, i: (0, p))
    return _call(
        name, body, (npairs, L // TQ),
        [blk, full, full, pl.BlockSpec((L, LANES), lambda p, i: (0, 0)), pl.BlockSpec((cs, TQ), lambda p, i: (p, i))],
        [blk, blk], [jax.ShapeDtypeStruct((L, wf), F32)] * 2, [qn, kn, fv, c, ct], carry=carry)


def _fox_bwd(name, cfg, qn, kn, fv, c, ct, o, lse, do, carry=None):
    L, wf, cs = cfg.L, cfg.wf, cfg.cs
    T = BLOCK
    TQ = _tile(L, 544, 8)
    nb = L // T
    nq = L // TQ
    npairs = wf // LANES
    pad = cfg.pad

    def body(q_ref, k_ref, v_ref, c_ref, ct_ref, o_ref, lse_ref, do_ref, dq_ref, dk_ref, dv_ref, dct_ref, dcq_ref,
             cq_ref, lser_ref, dsum_ref, dsacc_ref):
        p = pl.program_id(0)
        j = pl.program_id(1)
        lane = _lane((TQ, LANES))
        lo = lane < HEAD_DIM
        sels = (lo, jnp.logical_not(lo))

        def stack(x):
            return jnp.concatenate([jnp.where(lo, x, jnp.zeros_like(x)), jnp.where(lo, jnp.zeros_like(x), x)], axis=0)

        @pl.when((j == 0) & (p == 0))
        def _():
            dcq_ref[...] = jnp.zeros_like(dcq_ref)

        @pl.when(j == 0)
        def _():
            dq_ref[...] = jnp.zeros_like(dq_ref)
            dsacc_ref[...] = jnp.zeros_like(dsacc_ref)
            for t in range(nq):
                rows = slice(t * TQ, (t + 1) * TQ)
                dd = do_ref[rows, :] * o_ref[rows, :]
                lse_b = lse_ref[rows, :]
                cblk = c_ref[rows, :]
                for hh in range(2):
                    half = slice(hh * TQ, (hh + 1) * TQ)
                    dsum_ref[t, half, :] = jnp.broadcast_to(
                        jnp.sum(jnp.where(sels[hh], dd, 0.0), axis=1, keepdims=True), (TQ, LANES))
                    lser_ref[t, half, :] = jnp.broadcast_to(
                        jnp.sum(jnp.where(lane == hh * HEAD_DIM, lse_b, 0.0), axis=1, keepdims=True), (TQ, LANES))
                    cq_ref[t, half, :] = jnp.broadcast_to(
                        jnp.sum(jnp.where(lane == p * cs + hh, cblk, 0.0), axis=1, keepdims=True), (TQ, LANES))

        k = k_ref[...]
        v = v_ref[...]
        ck = jnp.concatenate([jnp.broadcast_to(ct_ref[0:1, :], (TQ, T)), jnp.broadcast_to(ct_ref[1:2, :], (TQ, T))], axis=0)
        row = lax.broadcasted_iota(jnp.int32, (2 * TQ, T), 0)
        qrow = jnp.where(row < TQ, row, row - TQ)
        kpos = j * T + lax.broadcasted_iota(jnp.int32, (2 * TQ, T), 1)

        def step(i, carry_):
            dk, dv, dc0, dc1 = carry_
            off = pl.multiple_of(i * TQ, 8)
            rows = pl.ds(off, TQ)
            qs = stack(q_ref[rows, :])
            dos = stack(do_ref[rows, :]).astype(BF16)
            allowed = (kpos <= i * TQ + qrow) & (kpos >= pad)
            s = _dot_nt(qs, k) + cq_ref[i] - ck
            pr = jnp.where(allowed, jnp.exp(jnp.where(allowed, s, NEG_INF) - lser_ref[i]), 0.0)
            ds = pr * (_dot_nt(dos, v) - dsum_ref[i])
            dsb = ds.astype(BF16)
            dv = dv + _dot_tn(pr.astype(BF16), dos)
            dk = dk + _dot_tn(dsb, qs)
            dqs = _dot(dsb, k)
            dq_ref[rows, :] += jnp.where(lo, dqs[:TQ], dqs[TQ:])
            dc0 = dc0 - jnp.sum(ds[:TQ], axis=0, keepdims=True)
            dc1 = dc1 - jnp.sum(ds[TQ:], axis=0, keepdims=True)
            dsacc_ref[i] += ds
            return dk, dv, dc0, dc1

        init = (jnp.zeros((T, LANES), F32), jnp.zeros((T, LANES), F32),
                jnp.zeros((1, T), F32), jnp.zeros((1, T), F32))
        dk, dv, dc0, dc1 = lax.fori_loop((j * T) // TQ, nq, step, init)
        dk_ref[...] = dk
        dv_ref[...] = dv
        dct_ref[...] = jnp.zeros_like(dct_ref)
        dct_ref[0:1, :] = dc0
        dct_ref[1:2, :] = dc1

        @pl.when(j == nb - 1)
        def _():
            for t in range(nq):
                upd = jnp.zeros((TQ, LANES), F32)
                for hh in range(2):
                    upd = upd + jnp.where(lane == p * cs + hh,
                                          jnp.sum(dsacc_ref[t, hh * TQ:(hh + 1) * TQ, :], axis=1, keepdims=True), 0.0)
                dcq_ref[t * TQ:(t + 1) * TQ, :] += upd

    blk = pl.BlockSpec((T, LANES), lambda p, j: (j, p))
    full = pl.BlockSpec((L, LANES), lambda p, j: (0, p))
    return _call(
        name, body, (npairs, nb),
        [full, blk, blk, pl.BlockSpec((L, LANES), lambda p, j: (0, 0)), pl.BlockSpec((cs, T), lambda p, j: (p, j)),
         full, full, full],
        [full, blk, blk, pl.BlockSpec((cs, T), lambda p, j: (p, j)), pl.BlockSpec((L, LANES), lambda p, j: (0, 0))],
        [jax.ShapeDtypeStruct((L, wf), F32)] * 3
        + [jax.ShapeDtypeStruct((LANES, L), F32), jax.ShapeDtypeStruct((L, LANES), F32)],
        [qn, kn, fv, c, ct, o, lse, do], scratch=[pltpu.VMEM((nq, 2 * TQ, LANES), F32)] * 4, carry=carry)


def _swa_band(i, pad):
    T = BLOCK
    t = lax.broadcasted_iota(jnp.int32, (T, 2 * T), 0)
    u = lax.broadcasted_iota(jnp.int32, (T, 2 * T), 1)
    dist = t - u + T
    allowed = (dist >= 0) & (dist < T) & ((i - 1) * T + u >= pad)
    return dist.astype(F32), allowed


def _swa_probs(s, dist, allowed, slope, sink):
    s = jnp.where(allowed, s - slope * dist, NEG_INF)
    m = jnp.maximum(jnp.max(s, axis=1, keepdims=True), sink)
    e = jnp.exp(s - m)
    es = jnp.exp(sink - m)
    den = jnp.sum(e, axis=1, keepdims=True) + es
    return e / den, es / den


def _swa_stack(x, group, lo):
    tiles = []
    for h in range(group):
        pair = x[:, (h // 2) * LANES:(h // 2 + 1) * LANES]
        tiles.append(jnp.where(lo if h % 2 == 0 else jnp.logical_not(lo), pair, jnp.zeros_like(pair)))
    return jnp.concatenate(tiles, axis=0)


def _swa_fwd(name, cfg, sqn, skd, svd, sinks, slopes, carry=None):
    L, ws, group, hkv = cfg.L, cfg.ws, cfg.group, cfg.hkv
    T = BLOCK
    gw = group * HEAD_DIM
    pad = cfg.pad

    def body(sink_ref, slope_ref, q_ref, kp_ref, kc_ref, vp_ref, vc_ref, o_ref):
        kv = pl.program_id(0)
        i = pl.program_id(1)
        lo = _lane((T, LANES)) < HEAD_DIM
        dist, allowed = _swa_band(i, pad)
        kb = jnp.concatenate([kp_ref[...], kc_ref[...]], axis=0)
        vb = jnp.concatenate([vp_ref[...], vc_ref[...]], axis=0)
        s = _dot_nt(_swa_stack(q_ref[...], group, lo), kb)
        probs = []
        for h in range(group):
            p, _ = _swa_probs(s[h * T:(h + 1) * T], dist, allowed, slope_ref[kv * group + h], sink_ref[kv * group + h])
            probs.append(p.astype(BF16))
        o = _dot(jnp.concatenate(probs, axis=0), vb)
        for pp in range(group // 2):
            o_ref[:, pp * LANES:(pp + 1) * LANES] = jnp.where(lo, o[2 * pp * T:(2 * pp + 1) * T],
                                                              o[(2 * pp + 1) * T:(2 * pp + 2) * T])

    smem = pl.BlockSpec(memory_space=pltpu.SMEM)
    prev = pl.BlockSpec((T, LANES), lambda kv, i: (jnp.maximum(i - 1, 0), kv))
    cur = pl.BlockSpec((T, LANES), lambda kv, i: (i, kv))
    qblk = pl.BlockSpec((T, gw), lambda kv, i: (i, kv))
    return _call(name, body, (hkv, L // T), [smem, smem, qblk, prev, cur, prev, cur], [qblk],
                 [jax.ShapeDtypeStruct((L, ws), F32)], [sinks, slopes, sqn, skd, skd, svd, svd], carry=carry)


def _swa_bwd(name, cfg, sqn, skd, svd, sinks, slopes, o, do, carry=None):
    L, ws, group, hkv = cfg.L, cfg.ws, cfg.group, cfg.hkv
    T = BLOCK
    gw = group * HEAD_DIM
    ppk = group // 2
    pad = cfg.pad

    def body(sink_ref, slope_ref, q_ref, kp_ref, kc_ref, vp_ref, vc_ref, o_ref, do_ref,
             dq_ref, dk_ref, dv_ref, dsink_ref):
        kv = pl.program_id(0)
        i = pl.program_id(1)
        lane = _lane((T, LANES))
        lo = lane < HEAD_DIM

        @pl.when(i == 0)
        def _():
            dk_ref[...] = jnp.zeros_like(dk_ref)
            dv_ref[...] = jnp.zeros_like(dv_ref)
            dsink_ref[...] = jnp.zeros_like(dsink_ref)

        dist, allowed = _swa_band(i, pad)
        kb = jnp.concatenate([kp_ref[...], kc_ref[...]], axis=0)
        vb = jnp.concatenate([vp_ref[...], vc_ref[...]], axis=0)
        dov = do_ref[...]
        qs = _swa_stack(q_ref[...], group, lo)
        dos = _swa_stack(dov, group, lo).astype(BF16)
        dd = dov * o_ref[...]
        s = _dot_nt(qs, kb)
        dp = _dot_nt(dos, vb)
        probs, dss = [], []
        for h in range(group):
            pair = dd[:, (h // 2) * LANES:(h // 2 + 1) * LANES]
            dsum = jnp.sum(jnp.where(lo if h % 2 == 0 else jnp.logical_not(lo), pair, 0.0), axis=1, keepdims=True)
            p, ps = _swa_probs(s[h * T:(h + 1) * T], dist, allowed, slope_ref[kv * group + h], sink_ref[kv * group + h])
            probs.append(p.astype(BF16))
            dss.append((p * (dp[h * T:(h + 1) * T] - dsum)).astype(BF16))
            dsink_ref[8 * (h // 2):8 * (h // 2) + 1, :] += jnp.where(lane[0:1, :] == (h % 2) * HEAD_DIM,
                                                                    -jnp.sum(ps * dsum), 0.0)
        ds = jnp.concatenate(dss, axis=0)
        dq = _dot(ds, kb)
        for pp in range(ppk):
            dq_ref[:, pp * LANES:(pp + 1) * LANES] = jnp.where(lo, dq[2 * pp * T:(2 * pp + 1) * T],
                                                               dq[(2 * pp + 1) * T:(2 * pp + 2) * T])
        dk = _dot_tn(ds, qs)
        dv = _dot_tn(jnp.concatenate(probs, axis=0), dos)
        cur = pl.multiple_of(i * T, T)
        dk_ref[pl.ds(cur, T), :] += dk[T:]
        dv_ref[pl.ds(cur, T), :] += dv[T:]

        @pl.when(i > 0)
        def _():
            prv = pl.multiple_of((i - 1) * T, T)
            dk_ref[pl.ds(prv, T), :] += dk[:T]
            dv_ref[pl.ds(prv, T), :] += dv[:T]

    smem = pl.BlockSpec(memory_space=pltpu.SMEM)
    prev = pl.BlockSpec((T, LANES), lambda kv, i: (jnp.maximum(i - 1, 0), kv))
    cur = pl.BlockSpec((T, LANES), lambda kv, i: (i, kv))
    qblk = pl.BlockSpec((T, gw), lambda kv, i: (i, kv))
    full = pl.BlockSpec((L, LANES), lambda kv, i: (0, kv))
    return _call(
        name, body, (hkv, L // T), [smem, smem, qblk, prev, cur, prev, cur, qblk, qblk],
        [qblk, full, full, pl.BlockSpec((8 * ppk, LANES), lambda kv, i: (kv, 0))],
        [jax.ShapeDtypeStruct((L, ws), F32), jax.ShapeDtypeStruct((L, 2 * LANES), F32),
         jax.ShapeDtypeStruct((L, 2 * LANES), F32), jax.ShapeDtypeStruct((8 * ppk * hkv, LANES), F32)],
        [sinks, slopes, sqn, skd, skd, svd, svd, o, do], carry=carry)


def _out_norm(name, cfg, o_fox, o_swa, g_fox, g_swa):
    L, wf, ws = cfg.L, cfg.wf, cfg.ws
    tm = _tile(L, 256, 16)

    def body(of_ref, os_ref, gf_ref, gs_ref, o_ref):
        for src, g_ref, lo_, w in ((of_ref, gf_ref, 0, wf), (os_ref, gs_ref, wf, ws)):
            x = src[...]
            r = lax.rsqrt(jnp.mean(x * x, axis=1, keepdims=True) + EPS)
            o_ref[:, lo_:lo_ + w] = (x * r * g_ref[...]).astype(BF16)

    return pl.pallas_call(
        body, name=name, grid=(L // tm,),
        in_specs=[pl.BlockSpec((tm, wf), lambda i: (i, 0)), pl.BlockSpec((tm, ws), lambda i: (i, 0)),
                  pl.BlockSpec((1, wf), lambda i: (0, 0)), pl.BlockSpec((1, ws), lambda i: (0, 0))],
        out_specs=pl.BlockSpec((tm, wf + ws), lambda i: (i, 0)),
        out_shape=jax.ShapeDtypeStruct((L, wf + ws), BF16),
        compiler_params=_cparams(("parallel",)),
    )(o_fox, o_swa, g_fox, g_swa)


def _out_norm_bwd(name, cfg, dcat, o_fox, o_swa, g_fox, g_swa):
    L, wf, ws = cfg.L, cfg.wf, cfg.ws
    tm = _tile(L, 256, 16)

    def body(d_ref, of_ref, os_ref, gf_ref, gs_ref, dof_ref, dos_ref, dgf_ref, dgs_ref):
        i = pl.program_id(0)

        @pl.when(i == 0)
        def _():
            dgf_ref[...] = jnp.zeros_like(dgf_ref)
            dgs_ref[...] = jnp.zeros_like(dgs_ref)

        for src, g_ref, dst, dg_ref, lo_, w in ((of_ref, gf_ref, dof_ref, dgf_ref, 0, wf),
                                                (os_ref, gs_ref, dos_ref, dgs_ref, wf, ws)):
            x = src[...]
            dy = d_ref[:, lo_:lo_ + w]
            r = lax.rsqrt(jnp.mean(x * x, axis=1, keepdims=True) + EPS)
            wv = dy * g_ref[...]
            proj = jnp.sum(wv * x, axis=1, keepdims=True) * (1.0 / w)
            dst[...] = r * wv - x * (r * r * r * proj)
            dg_ref[...] += jnp.sum(dy * x * r, axis=0, keepdims=True)

    return pl.pallas_call(
        body, name=name, grid=(L // tm,),
        in_specs=[pl.BlockSpec((tm, wf + ws), lambda i: (i, 0)), pl.BlockSpec((tm, wf), lambda i: (i, 0)),
                  pl.BlockSpec((tm, ws), lambda i: (i, 0)),
                  pl.BlockSpec((1, wf), lambda i: (0, 0)), pl.BlockSpec((1, ws), lambda i: (0, 0))],
        out_specs=[pl.BlockSpec((tm, wf), lambda i: (i, 0)), pl.BlockSpec((tm, ws), lambda i: (i, 0)),
                   pl.BlockSpec((1, wf), lambda i: (0, 0)), pl.BlockSpec((1, ws), lambda i: (0, 0))],
        out_shape=[jax.ShapeDtypeStruct((L, wf), F32), jax.ShapeDtypeStruct((L, ws), F32),
                   jax.ShapeDtypeStruct((1, wf), F32), jax.ShapeDtypeStruct((1, ws), F32)],
        compiler_params=_cparams(("arbitrary",)),
    )(dcat, o_fox, o_swa, g_fox, g_swa)


def _loss_head(name, h, target):
    L, D = h.shape
    T = BLOCK

    def body(h_ref, t_ref, loss_ref, dh_ref, dh16_ref):
        i = pl.program_id(0)

        @pl.when(i == 0)
        def _():
            loss_ref[...] = jnp.zeros_like(loss_ref)
            dh_ref[...] = jnp.zeros_like(dh_ref)
            dh16_ref[...] = jnp.zeros_like(dh16_ref)

        @pl.when(i > 0)
        def _():
            err = h_ref[...] - t_ref[...]
            dh = err * (1.0 / D)
            dh_ref[...] = dh
            dh16_ref[...] = dh.astype(BF16)
            loss_ref[...] += jnp.sum(err * err) * (0.5 / D)

    return pl.pallas_call(
        body, name=name, grid=(L // T,),
        in_specs=[pl.BlockSpec((T, D), lambda i: (i, 0)), pl.BlockSpec((T, D), lambda i: (jnp.maximum(i - 1, 0), 0))],
        out_specs=[pl.BlockSpec((8, LANES), lambda i: (0, 0)), pl.BlockSpec((T, D), lambda i: (i, 0)),
                   pl.BlockSpec((T, D), lambda i: (i, 0))],
        out_shape=[jax.ShapeDtypeStruct((8, LANES), F32), jax.ShapeDtypeStruct((L, D), F32),
                   jax.ShapeDtypeStruct((L, D), BF16)],
        compiler_params=_cparams(("arbitrary",)),
    )(h, target)


def _pair_add(name, grad, landed, parts, layer, core):
    _, _, r, C = grad.shape

    def body(s_ref, g_ref, l_ref, p_ref, o_ref):
        o_ref[...] = (g_ref[...].astype(F32) + l_ref[...].astype(F32)).astype(o_ref.dtype)

    return pl.pallas_call(
        body, name=name,
        grid_spec=pltpu.PrefetchScalarGridSpec(
            num_scalar_prefetch=1, grid=(4,),
            in_specs=[pl.BlockSpec((None, None, r, C), lambda k, s: (k, s[0], 0, 0)),
                      pl.BlockSpec((None, r, C), lambda k, s: (k, 0, 0)),
                      pl.BlockSpec(memory_space=pl.ANY)],
            out_specs=pl.BlockSpec((None, None, r, C), lambda k, s: (layer, k, 0, 0))),
        out_shape=jax.ShapeDtypeStruct(parts.shape, parts.dtype),
        input_output_aliases={3: 0},
        compiler_params=_cparams(("arbitrary",)),
    )(core, grad, landed, parts)


def _chip_sum(name, part, landed, chip, layer_axis):
    A, _, r, C = part.shape

    def body(s_ref, p_ref, l0_ref, l1_ref, l2_ref, o_ref):
        o_ref[...] = ((p_ref[...].astype(F32) + l0_ref[...].astype(F32))
                      + (l1_ref[...].astype(F32) + l2_ref[...].astype(F32)))

    def land(k):
        return pl.BlockSpec((None, None, r, C), lambda a, s, k=k: (k, a, 0, 0))

    if layer_axis == 0:
        out_spec, out_shape = pl.BlockSpec((None, r, C), lambda a, s: (a, 0, 0)), (A, r, C)
    else:
        out_spec, out_shape = pl.BlockSpec((r, C), lambda a, s: (0, a)), (r, A * C)
    out = pl.pallas_call(
        body, name=name,
        grid_spec=pltpu.PrefetchScalarGridSpec(
            num_scalar_prefetch=1, grid=(A,),
            in_specs=[pl.BlockSpec((None, None, r, C), lambda a, s: (a, s[0], 0, 0)), land(0), land(1), land(2)],
            out_specs=out_spec),
        out_shape=jax.ShapeDtypeStruct(out_shape, F32),
        compiler_params=_cparams(("parallel",)),
    )(chip, part, landed, landed, landed)
    return out if layer_axis == 0 else out.reshape(r, A, C)


def _gather_sum(name, v):
    R = v.shape[0]

    def body(x_ref, o_ref, buf_ref, send_sems, recv_sems):
        x, y, c = _place()
        me, sibling = (x, y, c), (x, y, 1 - c)
        chips = [(1 - x, y), (x, 1 - y), (1 - x, 1 - y)]

        def rows(dev):
            px, py, pc = dev
            return buf_ref.at[4 * px + 2 * py + pc]

        def copy(k, block, to, src=None):
            return pltpu.make_async_remote_copy(
                src_ref=rows(block) if src is None else src, dst_ref=rows(block),
                send_sem=send_sems.at[k], recv_sem=recv_sems.at[k], device_id=to, device_id_type=MESH)

        first = [copy(0, me, sibling, src=x_ref)]
        first += [copy(1 + j, me, (*chip, c), src=x_ref) for j, chip in enumerate(chips)]
        for cp in first:
            cp.start()
        rows(me)[...] = x_ref[...]
        passed = [copy(4 + j, (*chip, c), sibling) for j, chip in enumerate(chips)]
        for j, chip in enumerate(chips):
            copy(1 + j, (*chip, c), me).wait_recv()
            passed[j].start()
        copy(0, sibling, me).wait_recv()
        for j, chip in enumerate(chips):
            copy(4 + j, (*chip, 1 - c), me).wait_recv()
        for cp in first + passed:
            cp.wait_send()
        acc = buf_ref[0]
        for d in range(1, N_DEV):
            acc = acc + buf_ref[d]
        o_ref[...] = acc

    vm = pl.BlockSpec(memory_space=pltpu.VMEM)
    return pl.pallas_call(
        body, name=name, in_specs=[vm], out_specs=vm,
        out_shape=jax.ShapeDtypeStruct((R, LANES), F32),
        scratch_shapes=[pltpu.VMEM((N_DEV, R, LANES), F32), pltpu.SemaphoreType.DMA((7,)), pltpu.SemaphoreType.DMA((7,))],
    )(v)


def _adamw(name, g, w, m, v):
    A, R, C = g.shape
    budget = 1 << 18
    tr = _tile(R, max(8, budget // C // 8 * 8), 8)
    ta = _tile(A, max(1, budget // (tr * C)), 1)

    def body(g_ref, w_ref, m_ref, v_ref, d_ref, nm_ref, nv_ref):
        gv = g_ref[...]
        nm = ADAM_B1 * m_ref[...] + (1.0 - ADAM_B1) * gv
        nv = ADAM_B2 * v_ref[...] + (1.0 - ADAM_B2) * (gv * gv)
        m_hat = nm / (1.0 - ADAM_B1 ** ADAM_STEP)
        v_hat = nv / (1.0 - ADAM_B2 ** ADAM_STEP)
        d_ref[...] = -ADAM_LR * (m_hat / (jnp.sqrt(v_hat) + ADAM_EPS) + ADAM_WD * w_ref[...])
        nm_ref[...] = nm
        nv_ref[...] = nv

    blk = pl.BlockSpec((ta, tr, C), lambda i, j: (i, j, 0))
    shp = jax.ShapeDtypeStruct((A, R, C), F32)
    return pl.pallas_call(
        body, name=name, grid=(A // ta, R // tr),
        in_specs=[blk] * 4, out_specs=[blk] * 3, out_shape=[shp] * 3,
        compiler_params=_cparams(("parallel", "parallel")),
    )(g, w, m, v)


def _adamw_nd(name, g, w, m, v):
    shape = w.shape
    three = (1,) + shape if len(shape) == 2 else shape
    return tuple(o.reshape(shape) for o in _adamw(name, *[a.reshape(three) for a in (g, w, m, v)]))


def _scatter_heads(cfg, vals):
    v = jnp.pad(vals.reshape(cfg.hf // 2, 2), ((0, 0), (0, cfg.cs - 2)))
    return v.reshape(1, LANES)


def _gather_heads(cfg, row):
    return row.reshape(cfg.hf // 2, cfg.cs)[:, :2].reshape(cfg.hf)


def _permute_w_in(cfg, w_in_t):
    wf, hf = cfg.wf, cfg.hf
    o = 3 * wf
    cols = w_in_t.shape[1]
    fz = w_in_t[o:o + hf].reshape(hf // 2, 2, cols)
    fz_blk = jnp.pad(fz, ((0, 0), (0, cfg.cs - 2), (0, 0))).reshape(LANES, cols)
    return jnp.concatenate([w_in_t[:o], w_in_t[o + hf:], fz_blk], axis=0)


def _unpermute_dw_in(cfg, dwp):
    wf, hf = cfg.wf, cfg.hf
    o = 3 * wf
    cols = dwp.shape[1]
    fz = dwp[cfg.o_fz:].reshape(hf // 2, cfg.cs, cols)[:, :2].reshape(hf, cols)
    return jnp.concatenate([dwp[:o], fz, dwp[o:cfg.o_fz]], axis=0)


def _pair_gain(g):
    return jnp.tile(g, 2)[None]


def _fold_pair(dg):
    return dg[0, :HEAD_DIM] + dg[0, HEAD_DIM:]


def kernel(x, meta_tokens, ffn1_norm, ffn1_w_gate, ffn1_w_up, ffn1_w_down, mix_norm, w_in, b_forget, fox_q_norm, fox_k_norm, swa_q_norm, swa_k_norm, swa_sinks, fox_out_norm, swa_out_norm, w_out, ffn2_norm, ffn2_w_gate, ffn2_w_up, ffn2_w_down, loss_target, m_meta_tokens, m_ffn1_norm, m_ffn1_w_gate, m_ffn1_w_up, m_ffn1_w_down, m_mix_norm, m_w_in, m_b_forget, m_fox_q_norm, m_fox_k_norm, m_swa_q_norm, m_swa_k_norm, m_swa_sinks, m_fox_out_norm, m_swa_out_norm, m_w_out, m_ffn2_norm, m_ffn2_w_gate, m_ffn2_w_up, m_ffn2_w_down, v_meta_tokens, v_ffn1_norm, v_ffn1_w_gate, v_ffn1_w_up, v_ffn1_w_down, v_mix_norm, v_w_in, v_b_forget, v_fox_q_norm, v_fox_k_norm, v_swa_q_norm, v_swa_k_norm, v_swa_sinks, v_fox_out_norm, v_swa_out_norm, v_w_out, v_ffn2_norm, v_ffn2_w_gate, v_ffn2_w_up, v_ffn2_w_down):
    weights = dict(meta_tokens=meta_tokens, ffn1_norm=ffn1_norm, ffn1_w_gate=ffn1_w_gate, ffn1_w_up=ffn1_w_up,
                   ffn1_w_down=ffn1_w_down, mix_norm=mix_norm, w_in=w_in, b_forget=b_forget, fox_q_norm=fox_q_norm,
                   fox_k_norm=fox_k_norm, swa_q_norm=swa_q_norm, swa_k_norm=swa_k_norm, swa_sinks=swa_sinks,
                   fox_out_norm=fox_out_norm, swa_out_norm=swa_out_norm, w_out=w_out, ffn2_norm=ffn2_norm,
                   ffn2_w_gate=ffn2_w_gate, ffn2_w_up=ffn2_w_up, ffn2_w_down=ffn2_w_down)
    mom_m = dict(meta_tokens=m_meta_tokens, ffn1_norm=m_ffn1_norm, ffn1_w_gate=m_ffn1_w_gate, ffn1_w_up=m_ffn1_w_up,
                 ffn1_w_down=m_ffn1_w_down, mix_norm=m_mix_norm, w_in=m_w_in, b_forget=m_b_forget,
                 fox_q_norm=m_fox_q_norm, fox_k_norm=m_fox_k_norm, swa_q_norm=m_swa_q_norm, swa_k_norm=m_swa_k_norm,
                 swa_sinks=m_swa_sinks, fox_out_norm=m_fox_out_norm, swa_out_norm=m_swa_out_norm, w_out=m_w_out,
                 ffn2_norm=m_ffn2_norm, ffn2_w_gate=m_ffn2_w_gate, ffn2_w_up=m_ffn2_w_up, ffn2_w_down=m_ffn2_w_down)
    mom_v = dict(meta_tokens=v_meta_tokens, ffn1_norm=v_ffn1_norm, ffn1_w_gate=v_ffn1_w_gate, ffn1_w_up=v_ffn1_w_up,
                 ffn1_w_down=v_ffn1_w_down, mix_norm=v_mix_norm, w_in=v_w_in, b_forget=v_b_forget,
                 fox_q_norm=v_fox_q_norm, fox_k_norm=v_fox_k_norm, swa_q_norm=v_swa_q_norm, swa_k_norm=v_swa_k_norm,
                 swa_sinks=v_swa_sinks, fox_out_norm=v_fox_out_norm, swa_out_norm=v_swa_out_norm, w_out=v_w_out,
                 ffn2_norm=v_ffn2_norm, ffn2_w_gate=v_ffn2_w_gate, ffn2_w_up=v_ffn2_w_up, ffn2_w_down=v_ffn2_w_down)
    names = list(weights)

    _, S, D = x.shape
    depth = ffn1_norm.shape[0]
    hf, hs = b_forget.shape[1], swa_sinks.shape[1]
    U = w_in.shape[2] * N_DEV
    hkv = (U - 3 * HEAD_DIM * hf - hf - HEAD_DIM * hs) // (2 * HEAD_DIM)
    cfg = _Cfg(S, D, hf, hs, hkv)
    n_meta = meta_tokens.shape[0]
    assert n_meta == cfg.n_meta
    x_idx, y_idx, c_idx = _place()
    chip_idx = 2 * x_idx + y_idx
    dev_idx = 2 * chip_idx + c_idx
    core_s = jnp.reshape(c_idx, (1,)).astype(jnp.int32)
    chip_s = jnp.reshape(chip_idx, (1,)).astype(jnp.int32)

    col_sharded = ("ffn1_w_gate", "ffn1_w_up", "w_in", "ffn2_w_gate", "ffn2_w_up")
    use_order = ("ffn1_w_gate", "ffn1_w_up", "ffn1_w_down", "w_in", "w_out", "ffn2_w_gate", "ffn2_w_up", "ffn2_w_down")

    layer_axis = {k: (1 if k == "w_in" else 0) for k in use_order}

    def rows_view(k, a):
        if k not in col_sharded:
            return a
        return jnp.transpose(a, (2, 0, 1)) if k == "w_in" else jnp.swapaxes(a, 1, 2)

    def from_rows_view(k, a):
        if k not in col_sharded:
            return a
        return jnp.transpose(a, (1, 2, 0)) if k == "w_in" else jnp.swapaxes(a, 1, 2)

    w_rows = {k: rows_view(k, weights[k]) for k in use_order}

    def shard(key):
        k, l = key
        if k == "meta_tokens":
            return meta_tokens
        return _shard_bf16("weight_shard", w_rows[k], l, layer_axis[k] if k in col_sharded else 0)

    waiting = [("meta_tokens", 0)] + [(k, l) for l in range(depth) for k in use_order]
    after_first, after_second = [], []
    gathered = {}

    def fwd_carry(n_first):
        cy = _Carry()
        third = [(key, _gather_third(cy, buf)) for key, buf in after_second]
        second = [(key, _gather_second(cy, buf)) for key, buf in after_first]
        first = [(key, _gather_first(cy, shard(key))) for key in waiting[:n_first]]
        del waiting[:n_first]
        after_first.clear()
        after_second.clear()
        return cy, (first, second, third)

    def fwd_absorb(extra, plan):
        first, second, third = plan
        for key, idx in third:
            gathered[key] = extra[idx]
        for key, idx in second:
            after_second.append((key, extra[idx]))
        for key, idx in first:
            after_first.append((key, extra[idx]))

    def weight(k, l):
        key = (k, l)
        while key not in gathered:
            n = waiting.index(key) + 1 if key in waiting else 0
            cy, plan = fwd_carry(n)
            fwd_absorb(_comm_only("weights_gather", cy), plan)
        g = gathered[key]
        return g.reshape(-1, g.shape[-1])

    weight("w_in", 0)
    meta_full = jnp.swapaxes(weight("meta_tokens", 0).reshape(N_DEV, n_meta, -1), 0, 1).reshape(n_meta, D)
    slopes = jnp.asarray(2.0 ** (-8.0 * np.arange(1, hs + 1) / hs), dtype=F32)

    h = jnp.concatenate([jnp.zeros((cfg.pad, D), F32), meta_full, x[0]], axis=0)
    saved = []
    w_in_p = [None] * depth

    def mm_f(name, a, b, n_first=1, **kw):
        cy, plan = fwd_carry(n_first)
        out, extra = _matmul(name, a, b, carry=cy, **kw)
        fwd_absorb(extra, plan)
        return out

    def ffn_fwd(tag, l, h_in, norm, wg, wu, wd):
        xn = _rmsnorm_fwd(f"{tag}_norm", h_in, norm[l][None])
        wg_t, wu_t = weight(wg, l), weight(wu, l)
        cy, plan = fwd_carry(1)
        (gate, up, act), extra = _ffn_up(f"{tag}_up", xn, wg_t, wu_t, carry=cy)
        fwd_absorb(extra, plan)
        h_out = mm_f(f"{tag}_down", act, weight(wd, l), scale=0.5, residual=h_in, tm=1088, tn=512, tk=2816)
        return h_out, (xn, gate, up)

    for l in range(depth):
        st = {"h0": h}
        h, st["ffn1"] = ffn_fwd("ffn1", l, h, ffn1_norm, "ffn1_w_gate", "ffn1_w_up", "ffn1_w_down")
        st["h1"] = h
        xn = _rmsnorm_fwd("mix_norm", h, mix_norm[l][None])
        w_in_p[l] = _permute_w_in(cfg, weight("w_in", l))
        u = mm_f("mix_in", xn, w_in_p[l], trans_b=True, tm=1088, tn=640, tk=2048)
        gq, gk = _pair_gain(fox_q_norm[l]), _pair_gain(fox_k_norm[l])
        gsq, gsk = _pair_gain(swa_q_norm[l]), _pair_gain(swa_k_norm[l])
        bias = _scatter_heads(cfg, b_forget[l])
        qn, kn, fv, sqn, skd, svd, c, ct = _mix_prep("mix_prep", cfg, u, gq, gk, gsq, gsk, bias)
        cy, plan = fwd_carry(2)
        (o_fox, lse), extra = _fox_fwd("fox_fwd", cfg, qn, kn, fv, c, ct, carry=cy)
        fwd_absorb(extra, plan)
        cy, plan = fwd_carry(1)
        (o_swa,), extra = _swa_fwd("swa_fwd", cfg, sqn, skd, svd, swa_sinks[l], slopes, carry=cy)
        fwd_absorb(extra, plan)
        o_cat = _out_norm("out_norm", cfg, o_fox, o_swa, fox_out_norm[l][None], swa_out_norm[l][None])
        st["mix"] = (xn, u, gq, gk, gsq, gsk, bias, qn, kn, fv, sqn, skd, svd, c, ct, o_fox, lse, o_swa, o_cat)
        h = mm_f("mix_out", o_cat, weight("w_out", l), n_first=0, residual=h, tm=544, tn=1024, tk=2048)
        st["h2"] = h
        h, st["ffn2"] = ffn_fwd("ffn2", l, h, ffn2_norm, "ffn2_w_gate", "ffn2_w_up", "ffn2_w_down")
        saved.append(st)

    loss_blk, dh, dh16 = _loss_head("loss_head", h, loss_target[0])

    small = {k: [None] * depth for k in names if k not in use_order and k != "meta_tokens"}
    parts, landing = {}, {}
    to_sibling, to_chips = [], []

    def bwd_carry(n_chips):
        cy = _Carry()
        t1, t3 = [], []
        for k, l in list(to_chips):
            if len(t3) < n_chips and all(k != k3 for k3, _ in t3):
                to_chips.remove((k, l))
                t3.append((k, _scatter_chips(cy, parts[k], landing[k], l)))
        while to_sibling:
            k, l, g = to_sibling.pop(0)
            t1.append((k, l, g, _scatter_sibling(cy, g)))
        return cy, (t1, t3)

    def bwd_absorb(extra, plan):
        t1, t3 = plan
        for k, idx in t3:
            landing[k] = extra[idx]
        for k, l, g, idx in t1:
            parts[k] = _pair_add("grads_pair_add", g, extra[idx], parts[k], l, core_s)
            to_chips.append((k, l))

    def emit_grad(k, l, dw):
        r, C = dw.shape[0] // N_DEV, dw.shape[1]
        if k not in parts:
            parts[k] = lax.empty((depth, 4, r, C), BF16)
            landing[k] = lax.empty((3, depth, r, C), BF16)
        to_sibling.append((k, l, dw.reshape(4, 2, r, C)))

    def mm_b(name, a, b, n_chips=0, **kw):
        cy, plan = bwd_carry(n_chips)
        out, extra = _matmul(name, a, b, carry=cy, **kw)
        bwd_absorb(extra, plan)
        return out

    def ffn_bwd(tag, l, dh_out, dh_out16, h_in, st_, norm, wg, wu, wd):
        xn, gate, up = st_
        short = 1 if l == 0 else 0
        cy, plan = bwd_carry(1)
        (dgate, dup, act), extra = _ffn_bwd_act(f"{tag}_dact", dh_out16, weight(wd, l), gate, up, carry=cy)
        bwd_absorb(extra, plan)
        emit_grad(wd, l, mm_b(f"{tag}_dwd", act, dh_out16, short, trans_a=True, scale=0.5, out_dtype=BF16,
                              tm=512, tn=1024, tk=2176))
        emit_grad(wg, l, mm_b(f"{tag}_dwg", dgate, xn, short, trans_a=True, out_dtype=BF16, tm=512, tn=1024, tk=2176))
        emit_grad(wu, l, mm_b(f"{tag}_dwu", dup, xn, short, trans_a=True, out_dtype=BF16, tm=512, tn=1024, tk=2176))
        dxn = mm_b(f"{tag}_dxn", dgate, weight(wg, l), 1, pair2=(dup, weight(wu, l)), tm=1088, tn=512, tk=1408)
        dh_in, dh_in16, dg = _rmsnorm_bwd(f"{tag}_dnorm", dxn, h_in, norm[l][None], dh_out)
        return dh_in, dh_in16, dg[0]

    for l in reversed(range(depth)):
        st = saved[l]
        dh, dh16, small["ffn2_norm"][l] = ffn_bwd("ffn2", l, dh, dh16, st["h2"], st["ffn2"], ffn2_norm,
                                                   "ffn2_w_gate", "ffn2_w_up", "ffn2_w_down")
        xn, u, gq, gk, gsq, gsk, bias, qn, kn, fv, sqn, skd, svd, c, ct, o_fox, lse, o_swa, o_cat = st["mix"]
        dcat = mm_b("mix_dcat", dh16, weight("w_out", l), int(l == 0), trans_b=True, tm=544, tn=1024, tk=2048)
        emit_grad("w_out", l, mm_b("mix_dwout", o_cat, dh16, int(l == 0), trans_a=True, out_dtype=BF16, tm=512, tn=1024, tk=2176))
        do_fox, do_swa, dgf, dgs = _out_norm_bwd("out_norm_bwd", cfg, dcat, o_fox, o_swa,
                                                 fox_out_norm[l][None], swa_out_norm[l][None])
        small["fox_out_norm"][l], small["swa_out_norm"][l] = dgf[0], dgs[0]
        cy, plan = bwd_carry(3)
        (dqn, dkn, dfv, dct, dcq), extra = _fox_bwd("fox_bwd", cfg, qn, kn, fv, c, ct, o_fox, lse, do_fox, carry=cy)
        bwd_absorb(extra, plan)
        cy, plan = bwd_carry(1)
        (dsqn, dskd, dsvd, dsink), extra = _swa_bwd("swa_bwd", cfg, sqn, skd, svd, swa_sinks[l], slopes, o_swa, do_swa,
                                                    carry=cy)
        bwd_absorb(extra, plan)
        small["swa_sinks"][l] = dsink.reshape(hs // 2, 8, LANES)[:, 0, ::HEAD_DIM].reshape(hs)
        du, dgq, dgk, dgsq, dgsk, db = _mix_prep_bwd("mix_prep_bwd", cfg, u, gq, gk, gsq, gsk, bias,
                                                     dqn, dkn, dfv, dsqn, dskd, dsvd, dct, dcq)
        small["fox_q_norm"][l], small["fox_k_norm"][l] = _fold_pair(dgq), _fold_pair(dgk)
        small["swa_q_norm"][l], small["swa_k_norm"][l] = _fold_pair(dgsq), _fold_pair(dgsk)
        small["b_forget"][l] = _gather_heads(cfg, db[0])
        dwp = mm_b("mix_dwin", du, xn, int(l == 0), trans_a=True, out_dtype=BF16, tm=640, tn=1024, tk=2176)
        emit_grad("w_in", l, _unpermute_dw_in(cfg, dwp))
        dxn = mm_b("mix_dxn", du, w_in_p[l], int(l == 0), tm=544, tn=1024, tk=4480)
        dh, dh16, dg = _rmsnorm_bwd("mix_dnorm", dxn, st["h1"], mix_norm[l][None], dh)
        small["mix_norm"][l] = dg[0]
        dh, dh16, small["ffn1_norm"][l] = ffn_bwd("ffn1", l, dh, dh16, st["h0"], st["ffn1"], ffn1_norm,
                                                   "ffn1_w_gate", "ffn1_w_up", "ffn1_w_down")

    grad_x = dh[BLOCK:][None]
    dmeta = dh[cfg.pad:BLOCK]

    while to_sibling or to_chips:
        cy, plan = bwd_carry(len(use_order))
        bwd_absorb(_comm_only("grads_scatter", cy), plan)

    grads = {}
    for k in use_order:
        grads[k] = _chip_sum("grads_chip_sum", parts[k], landing[k], chip_s, layer_axis[k] if k in col_sharded else 0)

    small_names = list(small)
    pieces = [loss_blk[0, :1], dmeta.reshape(-1)] + [jnp.stack(small[k]).reshape(-1) for k in small_names]
    sizes = [int(p.shape[0]) for p in pieces]
    total = sum(sizes)
    padded = -(-total // (8 * LANES)) * (8 * LANES)
    vec = jnp.concatenate(pieces + [jnp.zeros((padded - total,), F32)]).reshape(-1, LANES)
    summed = _gather_sum("small_gather_sum", vec).reshape(-1)
    offs = np.cumsum([0] + sizes)
    loss = summed[0]
    dmeta_full = summed[offs[1]:offs[2]].reshape(n_meta, D)
    mcols = meta_tokens.shape[1]
    grads["meta_tokens"] = lax.dynamic_slice_in_dim(dmeta_full, dev_idx * mcols, mcols, axis=1)
    for n_, k in enumerate(small_names):
        grads[k] = summed[offs[2 + n_]:offs[3 + n_]].reshape(weights[k].shape)

    delta, new_m, new_v = {}, {}, {}
    for k in names:
        if k in col_sharded:
            outs = _adamw_nd("adamw", grads[k], w_rows[k], rows_view(k, mom_m[k]), rows_view(k, mom_v[k]))
            delta[k], new_m[k], new_v[k] = (from_rows_view(k, o) for o in outs)
            grads[k] = from_rows_view(k, grads[k])
        else:
            delta[k], new_m[k], new_v[k] = _adamw_nd("adamw", grads[k], weights[k], mom_m[k], mom_v[k])

    return (loss, grad_x, *[grads[k] for k in names], *[delta[k] for k in names],
            *[new_m[k] for k in names], *[new_v[k] for k in names])
```

```python
import functools

import numpy as np
import jax
import jax.numpy as jnp
from jax import lax
from jax.experimental import pallas as pl
from jax.experimental.pallas import tpu as pltpu

F32 = jnp.float32
BF16 = jnp.bfloat16
MESH = pl.DeviceIdType.MESH

HEAD_DIM = 64
BLOCK = 128
LANES = 128
N_DEV = 8
EPS = 1e-6
NEG_INF = -1e30
SCALE = HEAD_DIM ** -0.5

ADAM_LR = 0.001
ADAM_B1 = 0.9
ADAM_B2 = 0.999
ADAM_EPS = 1e-08
ADAM_WD = 0.01
ADAM_STEP = 10

VMEM_BYTES_V7X = 64 * 1024 * 1024
VMEM_LIMIT = VMEM_BYTES_V7X * 3 // 4

NT = (((1,), (1,)), ((), ()))
TN = (((0,), (0,)), ((), ()))
HI = lax.Precision.HIGHEST


def _cparams(sem=None, vmem=VMEM_LIMIT):
    return pltpu.CompilerParams(dimension_semantics=sem, vmem_limit_bytes=vmem)


def _tile(n, pref, mult):
    best = None
    for t in range(mult, min(n, pref) + 1, mult):
        if n % t == 0:
            best = t
    return best if best is not None else n


def _dot(a, b):
    return jnp.dot(a, b, preferred_element_type=F32)


def _dot_nt(a, b):
    return lax.dot_general(a, b, NT, preferred_element_type=F32)


def _dot_tn(a, b):
    return lax.dot_general(a, b, TN, preferred_element_type=F32)


def _lane(shape):
    return lax.broadcasted_iota(jnp.int32, shape, len(shape) - 1)


def _half_sum(x, lo):
    s0 = jnp.sum(jnp.where(lo, x, 0.0), axis=1, keepdims=True)
    s1 = jnp.sum(jnp.where(lo, 0.0, x), axis=1, keepdims=True)
    return jnp.where(lo, s0, s1)


def _sigmoid(x):
    return 1.0 / (1.0 + jnp.exp(-x))


def _place():
    return lax.axis_index("x"), lax.axis_index("y"), lax.axis_index("c")


def _peers(x, y, c):
    return [(x, y, 1 - c), (1 - x, y, c), (x, 1 - y, c), (1 - x, 1 - y, c)]


def _dev_index(dev):
    px, py, pc = dev
    return 4 * px + 2 * py + pc


class _Carry:
    def __init__(self):
        self.ins, self.outs, self.alias, self.items = [], [], {}, []
        self.nsem = self.nloc = 0

    def add(self, ins, outs, alias, nsem, nloc, build):
        i0, o0 = len(self.ins), len(self.outs)
        for src, dst in alias.items():
            self.alias[i0 + src] = o0 + dst
        self.items.append((i0, len(ins), o0, len(outs), self.nsem, self.nloc, build))
        self.ins += ins
        self.outs += outs
        self.nsem += nsem
        self.nloc += nloc
        return list(range(o0, o0 + len(outs)))

    def build(self, in_refs, out_refs, ssem, rsem, lsem):
        ops = []
        for i0, ni, o0, no, s0, l0, fn in self.items:
            ops.append(fn(in_refs[i0:i0 + ni], out_refs[o0:o0 + no],
                          lambda k, s0=s0: (ssem.at[s0 + k], rsem.at[s0 + k]), lambda k, l0=l0: lsem.at[l0 + k]))
        return ops


def _remote(src, dst, sems, dev):
    return pltpu.make_async_remote_copy(src_ref=src, dst_ref=dst, send_sem=sems[0], recv_sem=sems[1],
                                        device_id=dev, device_id_type=MESH)


def _gather_first(carry, shard):
    def build(ins, outs, sems, locs):
        src, buf = ins[0], outs[0]
        x, y, c = _place()
        me = _dev_index((x, y, c))
        peers = _peers(x, y, c)[:3]
        local = pltpu.make_async_copy(src, buf.at[me], locs(0))
        sends = [_remote(src, buf.at[me], sems(k), dev) for k, dev in enumerate(peers)]
        recvs = [_remote(src, buf.at[_dev_index(dev)], sems(k), dev) for k, dev in enumerate(peers)]

        def start():
            local.start()
            for cp in sends:
                cp.start()

        def wait():
            for cp in sends:
                cp.wait_send()
            for cp in recvs:
                cp.wait_recv()
            local.wait()

        return start, wait

    return carry.add([shard], [jax.ShapeDtypeStruct((N_DEV,) + shard.shape, shard.dtype)], {}, 3, 1, build)[0]


def _gather_second(carry, buf):
    def build(ins, outs, sems, locs):
        b = outs[0]
        x, y, c = _place()
        sibling, xn, yn, diag = _peers(x, y, c)
        relay_src = _dev_index((1 - x, y, c)) * (1 - c) + _dev_index((x, 1 - y, c)) * c
        relay_to = (x * (1 - c) + (1 - x) * c, (1 - y) * (1 - c) + y * c, c)
        sends = [_remote(b.at[relay_src], b.at[relay_src], sems(0), relay_to)]
        recvs = [_remote(b.at[relay_src], b.at[_dev_index(diag)], sems(0), relay_to)]
        for k, dev in enumerate((xn, yn)):
            sends.append(_remote(b.at[_dev_index(dev)], b.at[_dev_index(dev)], sems(1 + k), sibling))
            recvs.append(_remote(b.at[_dev_index(dev)], b.at[_dev_index((dev[0], dev[1], 1 - c))], sems(1 + k), sibling))

        def start():
            for cp in sends:
                cp.start()

        def wait():
            for cp in sends:
                cp.wait_send()
            for cp in recvs:
                cp.wait_recv()

        return start, wait

    return carry.add([buf], [jax.ShapeDtypeStruct(buf.shape, buf.dtype)], {0: 0}, 3, 0, build)[0]


def _gather_third(carry, buf):
    def build(ins, outs, sems, locs):
        b = outs[0]
        x, y, c = _place()
        sibling, _xn, _yn, diag = _peers(x, y, c)
        send = _remote(b.at[_dev_index(diag)], b.at[_dev_index(diag)], sems(0), sibling)
        recv = _remote(b.at[_dev_index(diag)], b.at[_dev_index((diag[0], diag[1], 1 - c))], sems(0), sibling)

        def wait():
            send.wait_send()
            recv.wait_recv()

        return send.start, wait

    return carry.add([buf], [jax.ShapeDtypeStruct(buf.shape, buf.dtype)], {0: 0}, 1, 0, build)[0]


def _scatter_sibling(carry, grad):
    def build(ins, outs, sems, locs):
        x, y, c = _place()
        cp = _remote(ins[0].at[:, 1 - c], outs[0], sems(0), (x, y, 1 - c))
        return cp.start, cp.wait

    shape = (grad.shape[0],) + grad.shape[2:]
    return carry.add([grad], [jax.ShapeDtypeStruct(shape, grad.dtype)], {}, 1, 0, build)[0]


def _scatter_chips(carry, parts, landing, layer):
    def build(ins, outs, sems, locs):
        x, y, c = _place()
        cps = [_remote(ins[0].at[layer, 2 * dev[0] + dev[1]], outs[0].at[k, layer], sems(k), dev)
               for k, dev in enumerate(_peers(x, y, c)[1:])]

        def start():
            for cp in cps:
                cp.start()

        def wait():
            for cp in cps:
                cp.wait()

        return start, wait

    return carry.add([parts, landing], [jax.ShapeDtypeStruct(landing.shape, landing.dtype)], {1: 0}, 3, 0, build)[0]


def _call(name, body, grid, in_specs, out_specs, out_shape, args, scratch=(), carry=None):
    ni, no, ns = len(args), len(out_shape), len(scratch)
    if carry is None or not carry.items:
        res = pl.pallas_call(
            body, name=name, grid=grid, in_specs=list(in_specs), out_specs=list(out_specs), out_shape=list(out_shape),
            scratch_shapes=list(scratch), compiler_params=_cparams(("arbitrary",) * len(grid)))(*args)
        return list(res), []
    nci, nco = len(carry.ins), len(carry.outs)

    def full_body(*refs):
        c_in = refs[ni:ni + nci]
        c_out = refs[ni + nci + no:ni + nci + no + nco]
        sc = refs[ni + nci + no + nco:]
        ops = carry.build(c_in, c_out, sc[ns], sc[ns + 1], sc[ns + 2])
        first = last = None
        for d, n in enumerate(grid):
            pid = pl.program_id(d)
            first = (pid == 0) if first is None else first & (pid == 0)
            last = (pid == n - 1) if last is None else last & (pid == n - 1)

        @pl.when(first)
        def _():
            for start, _w in ops:
                start()

        body(*refs[:ni], *refs[ni + nci:ni + nci + no], *sc[:ns])

        @pl.when(last)
        def _():
            for _s, wait in ops:
                wait()

    hbm = pl.BlockSpec(memory_space=pl.ANY)
    res = pl.pallas_call(
        full_body, name=name, grid=grid,
        in_specs=list(in_specs) + [hbm] * nci, out_specs=list(out_specs) + [hbm] * nco,
        out_shape=list(out_shape) + list(carry.outs),
        input_output_aliases={ni + s: no + d for s, d in carry.alias.items()},
        scratch_shapes=list(scratch) + [pltpu.SemaphoreType.DMA((carry.nsem,)), pltpu.SemaphoreType.DMA((carry.nsem,)),
                                        pltpu.SemaphoreType.DMA((max(carry.nloc, 1),))],
        compiler_params=_cparams(("arbitrary",) * len(grid)))(*args, *carry.ins)
    return list(res[:no]), list(res[no:])


def _comm_only(name, carry):
    return _call(name, lambda *refs: None, (1,), [], [], [], [], carry=carry)[1]


def _matmul(name, a, b, *, pair2=None, trans_a=False, trans_b=False, out_dtype=F32, scale=None, residual=None,
            tm=512, tn=512, tk=512, carry=None):
    if trans_a:
        K, M = a.shape
    else:
        M, K = a.shape
    if trans_b:
        N, Kb = b.shape
    else:
        Kb, N = b.shape
    assert K == Kb, (name, a.shape, b.shape)
    tm = _tile(M, tm, LANES if trans_a else 16)
    tn = _tile(N, tn, LANES)
    tk = _tile(K, tk, 16 if (trans_a and not trans_b) else LANES)
    nk = K // tk
    dims = (((0 if trans_a else 1,), (1 if trans_b else 0,)), ((), ()))
    pairs = [(a, b)] + ([pair2] if pair2 is not None else [])
    npair = len(pairs)

    def body(*refs):
        ab = refs[:2 * npair]
        pos = 2 * npair
        r_ref = None
        if residual is not None:
            r_ref = refs[pos]
            pos += 1
        o_ref = refs[pos]

        def partial():
            t = None
            for q in range(npair):
                d = lax.dot_general(ab[2 * q][...].astype(BF16), ab[2 * q + 1][...].astype(BF16), dims,
                                    preferred_element_type=F32)
                t = d if t is None else t + d
            return t

        def finish(r):
            if scale is not None:
                r = r * scale
            if r_ref is not None:
                r = r + r_ref[...].astype(F32)
            o_ref[...] = r.astype(o_ref.dtype)

        if nk == 1:
            finish(partial())
            return
        acc_ref = refs[pos + 1]
        k = pl.program_id(2)

        @pl.when(k == 0)
        def _():
            acc_ref[...] = partial()

        @pl.when(k > 0)
        def _():
            acc_ref[...] += partial()

        @pl.when(k == nk - 1)
        def _():
            finish(acc_ref[...])

    a_spec = pl.BlockSpec((tk, tm), lambda i, j, k: (k, i)) if trans_a else pl.BlockSpec((tm, tk), lambda i, j, k: (i, k))
    b_spec = pl.BlockSpec((tn, tk), lambda i, j, k: (j, k)) if trans_b else pl.BlockSpec((tk, tn), lambda i, j, k: (k, j))
    in_specs, args = [], []
    for pa, pb in pairs:
        in_specs += [a_spec, b_spec]
        args += [pa, pb]
    if residual is not None:
        in_specs.append(pl.BlockSpec((tm, tn), lambda i, j, k: (i, j)))
        args.append(residual)
    res, extra = _call(
        name, body, (M // tm, N // tn, nk), in_specs, [pl.BlockSpec((tm, tn), lambda i, j, k: (i, j))],
        [jax.ShapeDtypeStruct((M, N), out_dtype)], args,
        scratch=[pltpu.VMEM((tm, tn), F32)] if nk > 1 else [], carry=carry)
    return res[0], extra


def _shard_bf16(name, w, layer, layer_axis):
    R, C = w.shape[1 - layer_axis], w.shape[2]
    tr = _tile(R, 512, 16)

    def body(x_ref, o_ref):
        o_ref[...] = x_ref[...].astype(BF16)

    if layer_axis == 0:
        in_spec = pl.BlockSpec((None, tr, C), lambda i: (layer, i, 0))
    else:
        in_spec = pl.BlockSpec((tr, C), lambda i: (i, layer))
        w = w.reshape(R, -1)
    return pl.pallas_call(
        body, name=name, grid=(R // tr,),
        in_specs=[in_spec], out_specs=pl.BlockSpec((tr, C), lambda i: (i, 0)),
        out_shape=jax.ShapeDtypeStruct((R, C), BF16),
        compiler_params=_cparams(("parallel",)),
    )(w)


def _rmsnorm_fwd(name, h, g):
    L, D = h.shape
    tm = _tile(L, 256, 16)

    def body(h_ref, g_ref, o_ref):
        x = h_ref[...]
        r = lax.rsqrt(jnp.mean(x * x, axis=1, keepdims=True) + EPS)
        o_ref[...] = (x * r * g_ref[...]).astype(o_ref.dtype)

    return pl.pallas_call(
        body, name=name, grid=(L // tm,),
        in_specs=[pl.BlockSpec((tm, D), lambda i: (i, 0)), pl.BlockSpec((1, D), lambda i: (0, 0))],
        out_specs=pl.BlockSpec((tm, D), lambda i: (i, 0)),
        out_shape=jax.ShapeDtypeStruct((L, D), BF16),
        compiler_params=_cparams(("parallel",)),
    )(h, g)


def _rmsnorm_bwd(name, dy, h, g, dres):
    L, D = h.shape
    tm = _tile(L, 256, 16)

    def body(dy_ref, h_ref, g_ref, dres_ref, dh_ref, dh16_ref, dg_ref):
        i = pl.program_id(0)
        x = h_ref[...]
        dyv = dy_ref[...].astype(F32)
        r = lax.rsqrt(jnp.mean(x * x, axis=1, keepdims=True) + EPS)
        w = dyv * g_ref[...]
        proj = jnp.sum(w * x, axis=1, keepdims=True) * (1.0 / D)
        dh = dres_ref[...] + r * w - x * (r * r * r * proj)
        dh_ref[...] = dh
        dh16_ref[...] = dh.astype(BF16)

        @pl.when(i == 0)
        def _():
            dg_ref[...] = jnp.zeros_like(dg_ref)

        dg_ref[...] += jnp.sum(dyv * x * r, axis=0, keepdims=True)

    return pl.pallas_call(
        body, name=name, grid=(L // tm,),
        in_specs=[pl.BlockSpec((tm, D), lambda i: (i, 0)), pl.BlockSpec((tm, D), lambda i: (i, 0)),
                  pl.BlockSpec((1, D), lambda i: (0, 0)), pl.BlockSpec((tm, D), lambda i: (i, 0))],
        out_specs=[pl.BlockSpec((tm, D), lambda i: (i, 0)), pl.BlockSpec((tm, D), lambda i: (i, 0)),
                   pl.BlockSpec((1, D), lambda i: (0, 0))],
        out_shape=[jax.ShapeDtypeStruct((L, D), F32), jax.ShapeDtypeStruct((L, D), BF16),
                   jax.ShapeDtypeStruct((1, D), F32)],
        compiler_params=_cparams(("arbitrary",)),
    )(dy, h, g, dres)


def _ffn_up(name, xn, wgT, wuT, carry=None):
    L, D = xn.shape
    F = wgT.shape[0]
    tm = _tile(L, 1088, 16)
    tn = _tile(F, 512, LANES)

    def body(x_ref, wg_ref, wu_ref, g_ref, u_ref, a_ref):
        x = x_ref[...]
        g = _dot_nt(x, wg_ref[...])
        u = _dot_nt(x, wu_ref[...])
        g_ref[...] = g.astype(BF16)
        u_ref[...] = u.astype(BF16)
        a_ref[...] = (g * _sigmoid(g) * u).astype(BF16)

    o_spec = pl.BlockSpec((tm, tn), lambda i, j: (i, j))
    o_shape = jax.ShapeDtypeStruct((L, F), BF16)
    return _call(
        name, body, (L // tm, F // tn),
        [pl.BlockSpec((tm, D), lambda i, j: (i, 0)), pl.BlockSpec((tn, D), lambda i, j: (j, 0)),
         pl.BlockSpec((tn, D), lambda i, j: (j, 0))],
        [o_spec, o_spec, o_spec], [o_shape, o_shape, o_shape], [xn, wgT, wuT], carry=carry)


def _ffn_bwd_act(name, dh, wd, gate, up, carry=None):
    L, D = dh.shape
    F = wd.shape[0]
    tm = _tile(L, 1088, 16)
    tn = _tile(F, 512, LANES)

    def body(dh_ref, wd_ref, g_ref, u_ref, dg_ref, du_ref, a_ref):
        da = 0.5 * _dot_nt(dh_ref[...].astype(BF16), wd_ref[...])
        g = g_ref[...].astype(F32)
        u = u_ref[...].astype(F32)
        sg = _sigmoid(g)
        silu = g * sg
        dg_ref[...] = (da * u * (sg * (1.0 + g * (1.0 - sg)))).astype(BF16)
        du_ref[...] = (da * silu).astype(BF16)
        a_ref[...] = (silu * u).astype(BF16)

    o_spec = pl.BlockSpec((tm, tn), lambda i, j: (i, j))
    o_shape = jax.ShapeDtypeStruct((L, F), BF16)
    return _call(
        name, body, (L // tm, F // tn),
        [pl.BlockSpec((tm, D), lambda i, j: (i, 0)), pl.BlockSpec((tn, D), lambda i, j: (j, 0)), o_spec, o_spec],
        [o_spec, o_spec, o_spec], [o_shape, o_shape, o_shape], [dh, wd, gate, up], carry=carry)


class _Cfg:
    def __init__(self, S, D, hf, hs, hkv):
        self.S, self.D, self.L = S, D, S + BLOCK
        self.hf, self.hs, self.hkv = hf, hs, hkv
        self.wf, self.ws = hf * HEAD_DIM, hs * HEAD_DIM
        self.group = hs // hkv
        self.cs = 2 * LANES // hf
        self.n_meta = 16
        self.pad = BLOCK - self.n_meta
        self.o_fk = self.wf
        self.o_fv = 2 * self.wf
        self.o_sq = 3 * self.wf
        self.o_sk = self.o_sq + self.ws
        self.o_sv = self.o_sk + LANES
        self.o_fz = self.o_sv + LANES
        self.up = self.o_fz + LANES
        assert hkv == 2 and hf % 2 == 0 and self.group % 2 == 0 and self.cs % 8 == 0
        assert self.o_sq % self.ws == 0 and (2 * self.wf) % LANES == 0


def _head_norm(x, gain, lo, mult):
    r = lax.rsqrt(_half_sum(x * x, lo) * (1.0 / HEAD_DIM) + EPS)
    return x * r * (gain * mult)


def _head_norm_bwd(dy, x, gain, lo, mult):
    r = lax.rsqrt(_half_sum(x * x, lo) * (1.0 / HEAD_DIM) + EPS)
    w = dy * (gain * mult)
    proj = _half_sum(w * x, lo) * (1.0 / HEAD_DIM)
    dx = r * w - x * (r * r * r * proj)
    dgain = jnp.sum(dy * mult * x * r, axis=0, keepdims=True)
    return dx, dgain


def _dup(x, lo):
    xr = pltpu.roll(x, 64, 1)
    return jnp.where(lo, x, xr), jnp.where(lo, xr, x)


def _mix_prep(name, cfg, u, gq, gk, gsq, gsk, bias):
    L, wf, ws = cfg.L, cfg.wf, cfg.ws
    T = BLOCK
    npf, nps = wf // LANES, ws // LANES

    def body(fqk_ref, fv_ref, sq_ref, sk_ref, sv_ref, fz_ref, gq_ref, gk_ref, gsq_ref, gsk_ref, b_ref,
             qn_ref, kn_ref, fvo_ref, sqn_ref, skd_ref, svd_ref, c_ref, ct_ref, carry_ref):
        i = pl.program_id(0)
        lo = _lane((T, LANES)) < HEAD_DIM
        for p in range(npf):
            sl = slice(p * LANES, (p + 1) * LANES)
            qn_ref[:, sl] = _head_norm(fqk_ref[:, sl], gq_ref[...], lo, SCALE).astype(BF16)
            kn_ref[:, sl] = _head_norm(fqk_ref[:, wf + p * LANES: wf + (p + 1) * LANES], gk_ref[...], lo, 1.0).astype(BF16)
        fvo_ref[...] = fv_ref[...].astype(BF16)
        for p in range(nps):
            sl = slice(p * LANES, (p + 1) * LANES)
            sqn_ref[:, sl] = _head_norm(sq_ref[:, sl], gsq_ref[...], lo, SCALE).astype(BF16)
        k0, k1 = _dup(_head_norm(sk_ref[...], gsk_ref[...], lo, 1.0), lo)
        skd_ref[:, :LANES] = k0.astype(BF16)
        skd_ref[:, LANES:] = k1.astype(BF16)
        v0, v1 = _dup(sv_ref[...], lo)
        svd_ref[:, :LANES] = v0.astype(BF16)
        svd_ref[:, LANES:] = v1.astype(BF16)

        @pl.when(i == 0)
        def _():
            carry_ref[...] = jnp.zeros_like(carry_ref)

        z = fz_ref[...] + b_ref[...]
        lf = jnp.minimum(z, 0.0) - jnp.log(1.0 + jnp.exp(-jnp.abs(z)))
        row = lax.broadcasted_iota(jnp.int32, (T, T), 0)
        col = lax.broadcasted_iota(jnp.int32, (T, T), 1)
        tri = jnp.where(col <= row, 1.0, 0.0).astype(F32)
        c = jnp.dot(tri, lf, precision=HI, preferred_element_type=F32) + carry_ref[0:1, :]
        c_ref[...] = c
        ct_ref[...] = c.T
        carry_ref[0:1, :] = c_ref[T - 1:T, :]

    def rows(w, cb):
        return pl.BlockSpec((T, w), lambda i, cb=cb: (i, cb))

    vec = pl.BlockSpec((1, LANES), lambda i: (0, 0))
    return pl.pallas_call(
        body, name=name, grid=(L // T,),
        in_specs=[rows(2 * wf, 0), rows(wf, 2), rows(ws, cfg.o_sq // ws), rows(LANES, cfg.o_sk // LANES),
                  rows(LANES, cfg.o_sv // LANES), rows(LANES, cfg.o_fz // LANES), vec, vec, vec, vec, vec],
        out_specs=[rows(wf, 0), rows(wf, 0), rows(wf, 0), rows(ws, 0), rows(2 * LANES, 0), rows(2 * LANES, 0),
                   rows(LANES, 0), pl.BlockSpec((LANES, T), lambda i: (0, i))],
        out_shape=[jax.ShapeDtypeStruct((L, wf), BF16)] * 3 + [jax.ShapeDtypeStruct((L, ws), BF16)]
        + [jax.ShapeDtypeStruct((L, 2 * LANES), BF16)] * 2
        + [jax.ShapeDtypeStruct((L, LANES), F32), jax.ShapeDtypeStruct((LANES, L), F32)],
        scratch_shapes=[pltpu.VMEM((8, LANES), F32)],
        compiler_params=_cparams(("arbitrary",)),
    )(u, u, u, u, u, u, gq, gk, gsq, gsk, bias)


def _mix_prep_bwd(name, cfg, u, gq, gk, gsq, gsk, bias, dqn, dkn, dfv, dsqn, dskd, dsvd, dct, dcq):
    L, wf, ws = cfg.L, cfg.wf, cfg.ws
    T = BLOCK
    nb = L // T
    npf, nps = wf // LANES, ws // LANES

    def body(fqk_ref, sq_ref, sk_ref, fz_ref, gq_ref, gk_ref, gsq_ref, gsk_ref, b_ref,
             dqn_ref, dkn_ref, dfv_ref, dsqn_ref, dskd_ref, dsvd_ref, dct_ref, dcq_ref,
             du_ref, dgq_ref, dgk_ref, dgsq_ref, dgsk_ref, db_ref, carry_ref):
        i = pl.program_id(0)
        lo = _lane((T, LANES)) < HEAD_DIM

        @pl.when(i == 0)
        def _():
            carry_ref[...] = jnp.zeros_like(carry_ref)
            for r in (dgq_ref, dgk_ref, dgsq_ref, dgsk_ref, db_ref):
                r[...] = jnp.zeros_like(r)

        accq = jnp.zeros((1, LANES), F32)
        acck = jnp.zeros((1, LANES), F32)
        for p in range(npf):
            sl = slice(p * LANES, (p + 1) * LANES)
            dx, dg = _head_norm_bwd(dqn_ref[:, sl], fqk_ref[:, sl], gq_ref[...], lo, SCALE)
            du_ref[:, sl] = dx.astype(BF16)
            accq = accq + dg
            slk = slice(wf + p * LANES, wf + (p + 1) * LANES)
            dx, dg = _head_norm_bwd(dkn_ref[:, sl], fqk_ref[:, slk], gk_ref[...], lo, 1.0)
            du_ref[:, slk] = dx.astype(BF16)
            acck = acck + dg
        dgq_ref[...] += accq
        dgk_ref[...] += acck
        du_ref[:, cfg.o_fv:cfg.o_fv + wf] = dfv_ref[...].astype(BF16)
        accs = jnp.zeros((1, LANES), F32)
        for p in range(nps):
            sl = slice(p * LANES, (p + 1) * LANES)
            dx, dg = _head_norm_bwd(dsqn_ref[:, sl], sq_ref[:, sl], gsq_ref[...], lo, SCALE)
            du_ref[:, cfg.o_sq + p * LANES: cfg.o_sq + (p + 1) * LANES] = dx.astype(BF16)
            accs = accs + dg
        dgsq_ref[...] += accs

        def fold(ref):
            a0, a1 = ref[:, :LANES], ref[:, LANES:]
            return jnp.where(lo, a0 + pltpu.roll(a0, 64, 1), a1 + pltpu.roll(a1, 64, 1))

        dx, dg = _head_norm_bwd(fold(dskd_ref), sk_ref[...], gsk_ref[...], lo, 1.0)
        du_ref[:, cfg.o_sk:cfg.o_sk + LANES] = dx.astype(BF16)
        dgsk_ref[...] += dg
        du_ref[:, cfg.o_sv:cfg.o_sv + LANES] = fold(dsvd_ref).astype(BF16)

        dc = dct_ref[...].T + dcq_ref[...]
        row = lax.broadcasted_iota(jnp.int32, (T, T), 0)
        col = lax.broadcasted_iota(jnp.int32, (T, T), 1)
        triu = jnp.where(col >= row, 1.0, 0.0).astype(F32)
        dlf = jnp.dot(triu, dc, precision=HI, preferred_element_type=F32) + carry_ref[0:1, :]
        carry_ref[0:1, :] = dlf[0:1, :]
        z = fz_ref[...] + b_ref[...]
        dz = dlf * _sigmoid(-z)
        du_ref[:, cfg.o_fz:cfg.o_fz + LANES] = dz.astype(BF16)
        db_ref[...] += jnp.sum(dz, axis=0, keepdims=True)

    def rows(w, cb):
        return pl.BlockSpec((T, w), lambda i, cb=cb: (nb - 1 - i, cb))

    vec = pl.BlockSpec((1, LANES), lambda i: (0, 0))
    vshape = jax.ShapeDtypeStruct((1, LANES), F32)
    return pl.pallas_call(
        body, name=name, grid=(nb,),
        in_specs=[rows(2 * wf, 0), rows(ws, cfg.o_sq // ws), rows(LANES, cfg.o_sk // LANES),
                  rows(LANES, cfg.o_fz // LANES), vec, vec, vec, vec, vec,
                  rows(wf, 0), rows(wf, 0), rows(wf, 0), rows(ws, 0), rows(2 * LANES, 0), rows(2 * LANES, 0),
                  pl.BlockSpec((LANES, T), lambda i: (0, nb - 1 - i)), rows(LANES, 0)],
        out_specs=[rows(cfg.up, 0), vec, vec, vec, vec, vec],
        out_shape=[jax.ShapeDtypeStruct((L, cfg.up), BF16)] + [vshape] * 5,
        scratch_shapes=[pltpu.VMEM((8, LANES), F32)],
        compiler_params=_cparams(("arbitrary",)),
    )(u, u, u, u, gq, gk, gsq, gsk, bias, dqn, dkn, dfv, dsqn, dskd, dsvd, dct, dcq)


def _fox_fwd(name, cfg, qn, kn, fv, c, ct, carry=None):
    L, wf, cs = cfg.L, cfg.wf, cfg.cs
    TQ = BLOCK
    TK = _tile(L, 544, 8)
    npairs = wf // LANES
    pad = cfg.pad

    def body(q_ref, k_ref, v_ref, c_ref, ct_ref, o_ref, lse_ref):
        p = pl.program_id(0)
        i = pl.program_id(1)
        lo_q = _lane((TQ, LANES)) < HEAD_DIM
        lane_k = _lane((TK, LANES))
        lo_k = lane_k < HEAD_DIM
        lo_d = lax.broadcasted_iota(jnp.int32, (LANES, TQ), 0) < HEAD_DIM
        first = _lane((TK, 2 * TQ)) < TQ
        q = q_ref[...]
        qs = jnp.concatenate([jnp.where(lo_q, q, jnp.zeros_like(q)), jnp.where(lo_q, jnp.zeros_like(q), q)], axis=0)
        cq = jnp.concatenate([ct_ref[0:1, :], ct_ref[1:2, :]], axis=1)
        qrow = lax.broadcasted_iota(jnp.int32, (TK, 2 * TQ), 1)
        qpos = i * TQ + jnp.where(first, qrow, qrow - TQ)

        def step(j, carry_):
            m, l, acc = carry_
            off = pl.multiple_of(j * TK, 8)
            k = k_ref[pl.ds(off, TK), :]
            v = v_ref[pl.ds(off, TK), :]
            cblk = c_ref[pl.ds(off, TK), :]
            kpos = j * TK + lax.broadcasted_iota(jnp.int32, (TK, 2 * TQ), 0)
            allowed = (kpos <= qpos) & (kpos >= pad)
            ck0 = jnp.sum(jnp.where(lane_k == p * cs, cblk, 0.0), axis=1, keepdims=True)
            ck1 = jnp.sum(jnp.where(lane_k == p * cs + 1, cblk, 0.0), axis=1, keepdims=True)
            s = _dot_nt(k, qs) + cq - jnp.where(first, ck0, ck1)
            s = jnp.where(allowed, s, NEG_INF)
            m_new = jnp.maximum(m, jnp.max(s, axis=0, keepdims=True))
            alpha = jnp.exp(m - m_new)
            pr = jnp.exp(s - m_new)
            l = alpha * l + jnp.sum(pr, axis=0, keepdims=True)
            prb = pr.astype(BF16)
            prs = jnp.concatenate([prb[:, :TQ], prb[:, TQ:]], axis=0)
            vs = jnp.concatenate([jnp.where(lo_k, v, jnp.zeros_like(v)), jnp.where(lo_k, jnp.zeros_like(v), v)], axis=0)
            acc = acc * jnp.where(lo_d, alpha[:, :TQ], alpha[:, TQ:]) + _dot_tn(vs, prs)
            return m_new, l, acc

        init = (jnp.full((1, 2 * TQ), NEG_INF, F32), jnp.zeros((1, 2 * TQ), F32), jnp.zeros((LANES, TQ), F32))
        m, l, acc = lax.fori_loop(0, ((i + 1) * TQ + TK - 1) // TK, step, init)
        o_ref[...] = (acc / jnp.where(lo_d, l[:, :TQ], l[:, TQ:])).T
        lse = m + jnp.log(l)
        lse_ref[...] = jnp.where(lo_d, lse[:, :TQ], lse[:, TQ:]).T

    blk = pl.BlockSpec((TQ, LANES), lambda p, i: (i, p))
    full = pl.BlockSpec((L, LANES), lambda p, i: (0, p))
    return _call(
        name, body, (npairs, L // TQ),
        [blk, full, full, pl.BlockSpec((L, LANES), lambda p, i: (0, 0)), pl.BlockSpec((cs, TQ), lambda p, i: (p, i))],
        [blk, blk], [jax.ShapeDtypeStruct((L, wf), F32)] * 2, [qn, kn, fv, c, ct], carry=carry)


def _fox_bwd(name, cfg, qn, kn, fv, c, ct, o, lse, do, carry=None):
    L, wf, cs = cfg.L, cfg.wf, cfg.cs
    T = BLOCK
    TQ = _tile(L, 544, 8)
    nb = L // T
    nq = L // TQ
    npairs = wf // LANES
    pad = cfg.pad

    def body(q_ref, k_ref, v_ref, c_ref, ct_ref, o_ref, lse_ref, do_ref, dq_ref, dk_ref, dv_ref, dct_ref, dcq_ref,
             cq_ref, lser_ref, dsum_ref, dsacc_ref):
        p = pl.program_id(0)
        j = pl.program_id(1)
        lane = _lane((TQ, LANES))
        lo = lane < HEAD_DIM
        sels = (lo, jnp.logical_not(lo))

        def stack(x):
            return jnp.concatenate([jnp.where(lo, x, jnp.zeros_like(x)), jnp.where(lo, jnp.zeros_like(x), x)], axis=0)

        @pl.when((j == 0) & (p == 0))
        def _():
            dcq_ref[...] = jnp.zeros_like(dcq_ref)

        @pl.when(j == 0)
        def _():
            dq_ref[...] = jnp.zeros_like(dq_ref)
            dsacc_ref[...] = jnp.zeros_like(dsacc_ref)
            for t in range(nq):
                rows = slice(t * TQ, (t + 1) * TQ)
                dd = do_ref[rows, :] * o_ref[rows, :]
                lse_b = lse_ref[rows, :]
                cblk = c_ref[rows, :]
                for hh in range(2):
                    half = slice(hh * TQ, (hh + 1) * TQ)
                    dsum_ref[t, half, :] = jnp.broadcast_to(
                        jnp.sum(jnp.where(sels[hh], dd, 0.0), axis=1, keepdims=True), (TQ, LANES))
                    lser_ref[t, half, :] = jnp.broadcast_to(
                        jnp.sum(jnp.where(lane == hh * HEAD_DIM, lse_b, 0.0), axis=1, keepdims=True), (TQ, LANES))
                    cq_ref[t, half, :] = jnp.broadcast_to(
                        jnp.sum(jnp.where(lane == p * cs + hh, cblk, 0.0), axis=1, keepdims=True), (TQ, LANES))

        k = k_ref[...]
        v = v_ref[...]
        ck = jnp.concatenate([jnp.broadcast_to(ct_ref[0:1, :], (TQ, T)), jnp.broadcast_to(ct_ref[1:2, :], (TQ, T))], axis=0)
        row = lax.broadcasted_iota(jnp.int32, (2 * TQ, T), 0)
        qrow = jnp.where(row < TQ, row, row - TQ)
        kpos = j * T + lax.broadcasted_iota(jnp.int32, (2 * TQ, T), 1)

        def step(i, carry_):
            dk, dv, dc0, dc1 = carry_
            off = pl.multiple_of(i * TQ, 8)
            rows = pl.ds(off, TQ)
            qs = stack(q_ref[rows, :])
            dos = stack(do_ref[rows, :]).astype(BF16)
            allowed = (kpos <= i * TQ + qrow) & (kpos >= pad)
            s = _dot_nt(qs, k) + cq_ref[i] - ck
            pr = jnp.where(allowed, jnp.exp(jnp.where(allowed, s, NEG_INF) - lser_ref[i]), 0.0)
            ds = pr * (_dot_nt(dos, v) - dsum_ref[i])
            dsb = ds.astype(BF16)
            dv = dv + _dot_tn(pr.astype(BF16), dos)
            dk = dk + _dot_tn(dsb, qs)
            dqs = _dot(dsb, k)
            dq_ref[rows, :] += jnp.where(lo, dqs[:TQ], dqs[TQ:])
            dc0 = dc0 - jnp.sum(ds[:TQ], axis=0, keepdims=True)
            dc1 = dc1 - jnp.sum(ds[TQ:], axis=0, keepdims=True)
            dsacc_ref[i] += ds
            return dk, dv, dc0, dc1

        init = (jnp.zeros((T, LANES), F32), jnp.zeros((T, LANES), F32),
                jnp.zeros((1, T), F32), jnp.zeros((1, T), F32))
        dk, dv, dc0, dc1 = lax.fori_loop((j * T) // TQ, nq, step, init)
        dk_ref[...] = dk
        dv_ref[...] = dv
        dct_ref[...] = jnp.zeros_like(dct_ref)
        dct_ref[0:1, :] = dc0
        dct_ref[1:2, :] = dc1

        @pl.when(j == nb - 1)
        def _():
            for t in range(nq):
                upd = jnp.zeros((TQ, LANES), F32)
                for hh in range(2):
                    upd = upd + jnp.where(lane == p * cs + hh,
                                          jnp.sum(dsacc_ref[t, hh * TQ:(hh + 1) * TQ, :], axis=1, keepdims=True), 0.0)
                dcq_ref[t * TQ:(t + 1) * TQ, :] += upd

    blk = pl.BlockSpec((T, LANES), lambda p, j: (j, p))
    full = pl.BlockSpec((L, LANES), lambda p, j: (0, p))
    return _call(
        name, body, (npairs, nb),
        [full, blk, blk, pl.BlockSpec((L, LANES), lambda p, j: (0, 0)), pl.BlockSpec((cs, T), lambda p, j: (p, j)),
         full, full, full],
        [full, blk, blk, pl.BlockSpec((cs, T), lambda p, j: (p, j)), pl.BlockSpec((L, LANES), lambda p, j: (0, 0))],
        [jax.ShapeDtypeStruct((L, wf), F32)] * 3
        + [jax.ShapeDtypeStruct((LANES, L), F32), jax.ShapeDtypeStruct((L, LANES), F32)],
        [qn, kn, fv, c, ct, o, lse, do], scratch=[pltpu.VMEM((nq, 2 * TQ, LANES), F32)] * 4, carry=carry)


def _swa_band(i, pad):
    T = BLOCK
    t = lax.broadcasted_iota(jnp.int32, (T, 2 * T), 0)
    u = lax.broadcasted_iota(jnp.int32, (T, 2 * T), 1)
    dist = t - u + T
    allowed = (dist >= 0) & (dist < T) & ((i - 1) * T + u >= pad)
    return dist.astype(F32), allowed


def _swa_probs(s, dist, allowed, slope, sink):
    s = jnp.where(allowed, s - slope * dist, NEG_INF)
    m = jnp.maximum(jnp.max(s, axis=1, keepdims=True), sink)
    e = jnp.exp(s - m)
    es = jnp.exp(sink - m)
    den = jnp.sum(e, axis=1, keepdims=True) + es
    return e / den, es / den


def _swa_stack(x, group, lo):
    tiles = []
    for h in range(group):
        pair = x[:, (h // 2) * LANES:(h // 2 + 1) * LANES]
        tiles.append(jnp.where(lo if h % 2 == 0 else jnp.logical_not(lo), pair, jnp.zeros_like(pair)))
    return jnp.concatenate(tiles, axis=0)


def _swa_fwd(name, cfg, sqn, skd, svd, sinks, slopes, carry=None):
    L, ws, group, hkv = cfg.L, cfg.ws, cfg.group, cfg.hkv
    T = BLOCK
    gw = group * HEAD_DIM
    pad = cfg.pad

    def body(sink_ref, slope_ref, q_ref, kp_ref, kc_ref, vp_ref, vc_ref, o_ref):
        kv = pl.program_id(0)
        i = pl.program_id(1)
        lo = _lane((T, LANES)) < HEAD_DIM
        dist, allowed = _swa_band(i, pad)
        kb = jnp.concatenate([kp_ref[...], kc_ref[...]], axis=0)
        vb = jnp.concatenate([vp_ref[...], vc_ref[...]], axis=0)
        s = _dot_nt(_swa_stack(q_ref[...], group, lo), kb)
        probs = []
        for h in range(group):
            p, _ = _swa_probs(s[h * T:(h + 1) * T], dist, allowed, slope_ref[kv * group + h], sink_ref[kv * group + h])
            probs.append(p.astype(BF16))
        o = _dot(jnp.concatenate(probs, axis=0), vb)
        for pp in range(group // 2):
            o_ref[:, pp * LANES:(pp + 1) * LANES] = jnp.where(lo, o[2 * pp * T:(2 * pp + 1) * T],
                                                              o[(2 * pp + 1) * T:(2 * pp + 2) * T])

    smem = pl.BlockSpec(memory_space=pltpu.SMEM)
    prev = pl.BlockSpec((T, LANES), lambda kv, i: (jnp.maximum(i - 1, 0), kv))
    cur = pl.BlockSpec((T, LANES), lambda kv, i: (i, kv))
    qblk = pl.BlockSpec((T, gw), lambda kv, i: (i, kv))
    return _call(name, body, (hkv, L // T), [smem, smem, qblk, prev, cur, prev, cur], [qblk],
                 [jax.ShapeDtypeStruct((L, ws), F32)], [sinks, slopes, sqn, skd, skd, svd, svd], carry=carry)


def _swa_bwd(name, cfg, sqn, skd, svd, sinks, slopes, o, do, carry=None):
    L, ws, group, hkv = cfg.L, cfg.ws, cfg.group, cfg.hkv
    T = BLOCK
    gw = group * HEAD_DIM
    ppk = group // 2
    pad = cfg.pad

    def body(sink_ref, slope_ref, q_ref, kp_ref, kc_ref, vp_ref, vc_ref, o_ref, do_ref,
             dq_ref, dk_ref, dv_ref, dsink_ref):
        kv = pl.program_id(0)
        i = pl.program_id(1)
        lane = _lane((T, LANES))
        lo = lane < HEAD_DIM

        @pl.when(i == 0)
        def _():
            dk_ref[...] = jnp.zeros_like(dk_ref)
            dv_ref[...] = jnp.zeros_like(dv_ref)
            dsink_ref[...] = jnp.zeros_like(dsink_ref)

        dist, allowed = _swa_band(i, pad)
        kb = jnp.concatenate([kp_ref[...], kc_ref[...]], axis=0)
        vb = jnp.concatenate([vp_ref[...], vc_ref[...]], axis=0)
        dov = do_ref[...]
        qs = _swa_stack(q_ref[...], group, lo)
        dos = _swa_stack(dov, group, lo).astype(BF16)
        dd = dov * o_ref[...]
        s = _dot_nt(qs, kb)
        dp = _dot_nt(dos, vb)
        probs, dss = [], []
        for h in range(group):
            pair = dd[:, (h // 2) * LANES:(h // 2 + 1) * LANES]
            dsum = jnp.sum(jnp.where(lo if h % 2 == 0 else jnp.logical_not(lo), pair, 0.0), axis=1, keepdims=True)
            p, ps = _swa_probs(s[h * T:(h + 1) * T], dist, allowed, slope_ref[kv * group + h], sink_ref[kv * group + h])
            probs.append(p.astype(BF16))
            dss.append((p * (dp[h * T:(h + 1) * T] - dsum)).astype(BF16))
            dsink_ref[8 * (h // 2):8 * (h // 2) + 1, :] += jnp.where(lane[0:1, :] == (h % 2) * HEAD_DIM,
                                                                    -jnp.sum(ps * dsum), 0.0)
        ds = jnp.concatenate(dss, axis=0)
        dq = _dot(ds, kb)
        for pp in range(ppk):
            dq_ref[:, pp * LANES:(pp + 1) * LANES] = jnp.where(lo, dq[2 * pp * T:(2 * pp + 1) * T],
                                                               dq[(2 * pp + 1) * T:(2 * pp + 2) * T])
        dk = _dot_tn(ds, qs)
        dv = _dot_tn(jnp.concatenate(probs, axis=0), dos)
        cur = pl.multiple_of(i * T, T)
        dk_ref[pl.ds(cur, T), :] += dk[T:]
        dv_ref[pl.ds(cur, T), :] += dv[T:]

        @pl.when(i > 0)
        def _():
            prv = pl.multiple_of((i - 1) * T, T)
            dk_ref[pl.ds(prv, T), :] += dk[:T]
            dv_ref[pl.ds(prv, T), :] += dv[:T]

    smem = pl.BlockSpec(memory_space=pltpu.SMEM)
    prev = pl.BlockSpec((T, LANES), lambda kv, i: (jnp.maximum(i - 1, 0), kv))
    cur = pl.BlockSpec((T, LANES), lambda kv, i: (i, kv))
    qblk = pl.BlockSpec((T, gw), lambda kv, i: (i, kv))
    full = pl.BlockSpec((L, LANES), lambda kv, i: (0, kv))
    return _call(
        name, body, (hkv, L // T), [smem, smem, qblk, prev, cur, prev, cur, qblk, qblk],
        [qblk, full, full, pl.BlockSpec((8 * ppk, LANES), lambda kv, i: (kv, 0))],
        [jax.ShapeDtypeStruct((L, ws), F32), jax.ShapeDtypeStruct((L, 2 * LANES), F32),
         jax.ShapeDtypeStruct((L, 2 * LANES), F32), jax.ShapeDtypeStruct((8 * ppk * hkv, LANES), F32)],
        [sinks, slopes, sqn, skd, skd, svd, svd, o, do], carry=carry)


def _out_norm(name, cfg, o_fox, o_swa, g_fox, g_swa):
    L, wf, ws = cfg.L, cfg.wf, cfg.ws
    tm = _tile(L, 256, 16)

    def body(of_ref, os_ref, gf_ref, gs_ref, o_ref):
        for src, g_ref, lo_, w in ((of_ref, gf_ref, 0, wf), (os_ref, gs_ref, wf, ws)):
            x = src[...]
            r = lax.rsqrt(jnp.mean(x * x, axis=1, keepdims=True) + EPS)
            o_ref[:, lo_:lo_ + w] = (x * r * g_ref[...]).astype(BF16)

    return pl.pallas_call(
        body, name=name, grid=(L // tm,),
        in_specs=[pl.BlockSpec((tm, wf), lambda i: (i, 0)), pl.BlockSpec((tm, ws), lambda i: (i, 0)),
                  pl.BlockSpec((1, wf), lambda i: (0, 0)), pl.BlockSpec((1, ws), lambda i: (0, 0))],
        out_specs=pl.BlockSpec((tm, wf + ws), lambda i: (i, 0)),
        out_shape=jax.ShapeDtypeStruct((L, wf + ws), BF16),
        compiler_params=_cparams(("parallel",)),
    )(o_fox, o_swa, g_fox, g_swa)


def _out_norm_bwd(name, cfg, dcat, o_fox, o_swa, g_fox, g_swa):
    L, wf, ws = cfg.L, cfg.wf, cfg.ws
    tm = _tile(L, 256, 16)

    def body(d_ref, of_ref, os_ref, gf_ref, gs_ref, dof_ref, dos_ref, dgf_ref, dgs_ref):
        i = pl.program_id(0)

        @pl.when(i == 0)
        def _():
            dgf_ref[...] = jnp.zeros_like(dgf_ref)
            dgs_ref[...] = jnp.zeros_like(dgs_ref)

        for src, g_ref, dst, dg_ref, lo_, w in ((of_ref, gf_ref, dof_ref, dgf_ref, 0, wf),
                                                (os_ref, gs_ref, dos_ref, dgs_ref, wf, ws)):
            x = src[...]
            dy = d_ref[:, lo_:lo_ + w]
            r = lax.rsqrt(jnp.mean(x * x, axis=1, keepdims=True) + EPS)
            wv = dy * g_ref[...]
            proj = jnp.sum(wv * x, axis=1, keepdims=True) * (1.0 / w)
            dst[...] = r * wv - x * (r * r * r * proj)
            dg_ref[...] += jnp.sum(dy * x * r, axis=0, keepdims=True)

    return pl.pallas_call(
        body, name=name, grid=(L // tm,),
        in_specs=[pl.BlockSpec((tm, wf + ws), lambda i: (i, 0)), pl.BlockSpec((tm, wf), lambda i: (i, 0)),
                  pl.BlockSpec((tm, ws), lambda i: (i, 0)),
                  pl.BlockSpec((1, wf), lambda i: (0, 0)), pl.BlockSpec((1, ws), lambda i: (0, 0))],
        out_specs=[pl.BlockSpec((tm, wf), lambda i: (i, 0)), pl.BlockSpec((tm, ws), lambda i: (i, 0)),
                   pl.BlockSpec((1, wf), lambda i: (0, 0)), pl.BlockSpec((1, ws), lambda i: (0, 0))],
        out_shape=[jax.ShapeDtypeStruct((L, wf), F32), jax.ShapeDtypeStruct((L, ws), F32),
                   jax.ShapeDtypeStruct((1, wf), F32), jax.ShapeDtypeStruct((1, ws), F32)],
        compiler_params=_cparams(("arbitrary",)),
    )(dcat, o_fox, o_swa, g_fox, g_swa)


def _loss_head(name, h, target):
    L, D = h.shape
    T = BLOCK

    def body(h_ref, t_ref, loss_ref, dh_ref, dh16_ref):
        i = pl.program_id(0)

        @pl.when(i == 0)
        def _():
            loss_ref[...] = jnp.zeros_like(loss_ref)
            dh_ref[...] = jnp.zeros_like(dh_ref)
            dh16_ref[...] = jnp.zeros_like(dh16_ref)

        @pl.when(i > 0)
        def _():
            err = h_ref[...] - t_ref[...]
            dh = err * (1.0 / D)
            dh_ref[...] = dh
            dh16_ref[...] = dh.astype(BF16)
            loss_ref[...] += jnp.sum(err * err) * (0.5 / D)

    return pl.pallas_call(
        body, name=name, grid=(L // T,),
        in_specs=[pl.BlockSpec((T, D), lambda i: (i, 0)), pl.BlockSpec((T, D), lambda i: (jnp.maximum(i - 1, 0), 0))],
        out_specs=[pl.BlockSpec((8, LANES), lambda i: (0, 0)), pl.BlockSpec((T, D), lambda i: (i, 0)),
                   pl.BlockSpec((T, D), lambda i: (i, 0))],
        out_shape=[jax.ShapeDtypeStruct((8, LANES), F32), jax.ShapeDtypeStruct((L, D), F32),
                   jax.ShapeDtypeStruct((L, D), BF16)],
        compiler_params=_cparams(("arbitrary",)),
    )(h, target)


def _pair_add(name, grad, landed, parts, layer, core):
    _, _, r, C = grad.shape

    def body(s_ref, g_ref, l_ref, p_ref, o_ref):
        o_ref[...] = (g_ref[...].astype(F32) + l_ref[...].astype(F32)).astype(o_ref.dtype)

    return pl.pallas_call(
        body, name=name,
        grid_spec=pltpu.PrefetchScalarGridSpec(
            num_scalar_prefetch=1, grid=(4,),
            in_specs=[pl.BlockSpec((None, None, r, C), lambda k, s: (k, s[0], 0, 0)),
                      pl.BlockSpec((None, r, C), lambda k, s: (k, 0, 0)),
                      pl.BlockSpec(memory_space=pl.ANY)],
            out_specs=pl.BlockSpec((None, None, r, C), lambda k, s: (layer, k, 0, 0))),
        out_shape=jax.ShapeDtypeStruct(parts.shape, parts.dtype),
        input_output_aliases={3: 0},
        compiler_params=_cparams(("arbitrary",)),
    )(core, grad, landed, parts)


def _chip_sum(name, part, landed, chip, layer_axis):
    A, _, r, C = part.shape

    def body(s_ref, p_ref, l0_ref, l1_ref, l2_ref, o_ref):
        o_ref[...] = ((p_ref[...].astype(F32) + l0_ref[...].astype(F32))
                      + (l1_ref[...].astype(F32) + l2_ref[...].astype(F32)))

    def land(k):
        return pl.BlockSpec((None, None, r, C), lambda a, s, k=k: (k, a, 0, 0))

    if layer_axis == 0:
        out_spec, out_shape = pl.BlockSpec((None, r, C), lambda a, s: (a, 0, 0)), (A, r, C)
    else:
        out_spec, out_shape = pl.BlockSpec((r, C), lambda a, s: (0, a)), (r, A * C)
    out = pl.pallas_call(
        body, name=name,
        grid_spec=pltpu.PrefetchScalarGridSpec(
            num_scalar_prefetch=1, grid=(A,),
            in_specs=[pl.BlockSpec((None, None, r, C), lambda a, s: (a, s[0], 0, 0)), land(0), land(1), land(2)],
            out_specs=out_spec),
        out_shape=jax.ShapeDtypeStruct(out_shape, F32),
        compiler_params=_cparams(("parallel",)),
    )(chip, part, landed, landed, landed)
    return out if layer_axis == 0 else out.reshape(r, A, C)


def _gather_sum(name, v):
    R = v.shape[0]

    def body(x_ref, o_ref, buf_ref, send_sems, recv_sems):
        x, y, c = _place()
        me, sibling = (x, y, c), (x, y, 1 - c)
        chips = [(1 - x, y), (x, 1 - y), (1 - x, 1 - y)]

        def rows(dev):
            px, py, pc = dev
            return buf_ref.at[4 * px + 2 * py + pc]

        def copy(k, block, to, src=None):
            return pltpu.make_async_remote_copy(
                src_ref=rows(block) if src is None else src, dst_ref=rows(block),
                send_sem=send_sems.at[k], recv_sem=recv_sems.at[k], device_id=to, device_id_type=MESH)

        first = [copy(0, me, sibling, src=x_ref)]
        first += [copy(1 + j, me, (*chip, c), src=x_ref) for j, chip in enumerate(chips)]
        for cp in first:
            cp.start()
        rows(me)[...] = x_ref[...]
        passed = [copy(4 + j, (*chip, c), sibling) for j, chip in enumerate(chips)]
        for j, chip in enumerate(chips):
            copy(1 + j, (*chip, c), me).wait_recv()
            passed[j].start()
        copy(0, sibling, me).wait_recv()
        for j, chip in enumerate(chips):
            copy(4 + j, (*chip, 1 - c), me).wait_recv()
        for cp in first + passed:
            cp.wait_send()
        acc = buf_ref[0]
        for d in range(1, N_DEV):
            acc = acc + buf_ref[d]
        o_ref[...] = acc

    vm = pl.BlockSpec(memory_space=pltpu.VMEM)
    return pl.pallas_call(
        body, name=name, in_specs=[vm], out_specs=vm,
        out_shape=jax.ShapeDtypeStruct((R, LANES), F32),
        scratch_shapes=[pltpu.VMEM((N_DEV, R, LANES), F32), pltpu.SemaphoreType.DMA((7,)), pltpu.SemaphoreType.DMA((7,))],
    )(v)


def _adamw(name, g, w, m, v):
    A, R, C = g.shape
    budget = 1 << 18
    tr = _tile(R, max(8, budget // C // 8 * 8), 8)
    ta = _tile(A, max(1, budget // (tr * C)), 1)

    def body(g_ref, w_ref, m_ref, v_ref, d_ref, nm_ref, nv_ref):
        gv = g_ref[...]
        nm = ADAM_B1 * m_ref[...] + (1.0 - ADAM_B1) * gv
        nv = ADAM_B2 * v_ref[...] + (1.0 - ADAM_B2) * (gv * gv)
        m_hat = nm / (1.0 - ADAM_B1 ** ADAM_STEP)
        v_hat = nv / (1.0 - ADAM_B2 ** ADAM_STEP)
        d_ref[...] = -ADAM_LR * (m_hat / (jnp.sqrt(v_hat) + ADAM_EPS) + ADAM_WD * w_ref[...])
        nm_ref[...] = nm
        nv_ref[...] = nv

    blk = pl.BlockSpec((ta, tr, C), lambda i, j: (i, j, 0))
    shp = jax.ShapeDtypeStruct((A, R, C), F32)
    return pl.pallas_call(
        body, name=name, grid=(A // ta, R // tr),
        in_specs=[blk] * 4, out_specs=[blk] * 3, out_shape=[shp] * 3,
        compiler_params=_cparams(("parallel", "parallel")),
    )(g, w, m, v)


def _adamw_nd(name, g, w, m, v):
    shape = w.shape
    three = (1,) + shape if len(shape) == 2 else shape
    return tuple(o.reshape(shape) for o in _adamw(name, *[a.reshape(three) for a in (g, w, m, v)]))


def _scatter_heads(cfg, vals):
    v = jnp.pad(vals.reshape(cfg.hf // 2, 2), ((0, 0), (0, cfg.cs - 2)))
    return v.reshape(1, LANES)


def _gather_heads(cfg, row):
    return row.reshape(cfg.hf // 2, cfg.cs)[:, :2].reshape(cfg.hf)


def _permute_w_in(cfg, w_in_t):
    wf, hf = cfg.wf, cfg.hf
    o = 3 * wf
    cols = w_in_t.shape[1]
    fz = w_in_t[o:o + hf].reshape(hf // 2, 2, cols)
    fz_blk = jnp.pad(fz, ((0, 0), (0, cfg.cs - 2), (0, 0))).reshape(LANES, cols)
    return jnp.concatenate([w_in_t[:o], w_in_t[o + hf:], fz_blk], axis=0)


def _unpermute_dw_in(cfg, dwp):
    wf, hf = cfg.wf, cfg.hf
    o = 3 * wf
    cols = dwp.shape[1]
    fz = dwp[cfg.o_fz:].reshape(hf // 2, cfg.cs, cols)[:, :2].reshape(hf, cols)
    return jnp.concatenate([dwp[:o], fz, dwp[o:cfg.o_fz]], axis=0)


def _pair_gain(g):
    return jnp.tile(g, 2)[None]


def _fold_pair(dg):
    return dg[0, :HEAD_DIM] + dg[0, HEAD_DIM:]


def kernel(x, meta_tokens, ffn1_norm, ffn1_w_gate, ffn1_w_up, ffn1_w_down, mix_norm, w_in, b_forget, fox_q_norm, fox_k_norm, swa_q_norm, swa_k_norm, swa_sinks, fox_out_norm, swa_out_norm, w_out, ffn2_norm, ffn2_w_gate, ffn2_w_up, ffn2_w_down, loss_target, m_meta_tokens, m_ffn1_norm, m_ffn1_w_gate, m_ffn1_w_up, m_ffn1_w_down, m_mix_norm, m_w_in, m_b_forget, m_fox_q_norm, m_fox_k_norm, m_swa_q_norm, m_swa_k_norm, m_swa_sinks, m_fox_out_norm, m_swa_out_norm, m_w_out, m_ffn2_norm, m_ffn2_w_gate, m_ffn2_w_up, m_ffn2_w_down, v_meta_tokens, v_ffn1_norm, v_ffn1_w_gate, v_ffn1_w_up, v_ffn1_w_down, v_mix_norm, v_w_in, v_b_forget, v_fox_q_norm, v_fox_k_norm, v_swa_q_norm, v_swa_k_norm, v_swa_sinks, v_fox_out_norm, v_swa_out_norm, v_w_out, v_ffn2_norm, v_ffn2_w_gate, v_ffn2_w_up, v_ffn2_w_down):
    weights = dict(meta_tokens=meta_tokens, ffn1_norm=ffn1_norm, ffn1_w_gate=ffn1_w_gate, ffn1_w_up=ffn1_w_up,
                   ffn1_w_down=ffn1_w_down, mix_norm=mix_norm, w_in=w_in, b_forget=b_forget, fox_q_norm=fox_q_norm,
                   fox_k_norm=fox_k_norm, swa_q_norm=swa_q_norm, swa_k_norm=swa_k_norm, swa_sinks=swa_sinks,
                   fox_out_norm=fox_out_norm, swa_out_norm=swa_out_norm, w_out=w_out, ffn2_norm=ffn2_norm,
                   ffn2_w_gate=ffn2_w_gate, ffn2_w_up=ffn2_w_up, ffn2_w_down=ffn2_w_down)
    mom_m = dict(meta_tokens=m_meta_tokens, ffn1_norm=m_ffn1_norm, ffn1_w_gate=m_ffn1_w_gate, ffn1_w_up=m_ffn1_w_up,
                 ffn1_w_down=m_ffn1_w_down, mix_norm=m_mix_norm, w_in=m_w_in, b_forget=m_b_forget,
                 fox_q_norm=m_fox_q_norm, fox_k_norm=m_fox_k_norm, swa_q_norm=m_swa_q_norm, swa_k_norm=m_swa_k_norm,
                 swa_sinks=m_swa_sinks, fox_out_norm=m_fox_out_norm, swa_out_norm=m_swa_out_norm, w_out=m_w_out,
                 ffn2_norm=m_ffn2_norm, ffn2_w_gate=m_ffn2_w_gate, ffn2_w_up=m_ffn2_w_up, ffn2_w_down=m_ffn2_w_down)
    mom_v = dict(meta_tokens=v_meta_tokens, ffn1_norm=v_ffn1_norm, ffn1_w_gate=v_ffn1_w_gate, ffn1_w_up=v_ffn1_w_up,
                 ffn1_w_down=v_ffn1_w_down, mix_norm=v_mix_norm, w_in=v_w_in, b_forget=v_b_forget,
                 fox_q_norm=v_fox_q_norm, fox_k_norm=v_fox_k_norm, swa_q_norm=v_swa_q_norm, swa_k_norm=v_swa_k_norm,
                 swa_sinks=v_swa_sinks, fox_out_norm=v_fox_out_norm, swa_out_norm=v_swa_out_norm, w_out=v_w_out,
                 ffn2_norm=v_ffn2_norm, ffn2_w_gate=v_ffn2_w_gate, ffn2_w_up=v_ffn2_w_up, ffn2_w_down=v_ffn2_w_down)
    names = list(weights)

    _, S, D = x.shape
    depth = ffn1_norm.shape[0]
    hf, hs = b_forget.shape[1], swa_sinks.shape[1]
    U = w_in.shape[2] * N_DEV
    hkv = (U - 3 * HEAD_DIM * hf - hf - HEAD_DIM * hs) // (2 * HEAD_DIM)
    cfg = _Cfg(S, D, hf, hs, hkv)
    n_meta = meta_tokens.shape[0]
    assert n_meta == cfg.n_meta
    x_idx, y_idx, c_idx = _place()
    chip_idx = 2 * x_idx + y_idx
    dev_idx = 2 * chip_idx + c_idx
    core_s = jnp.reshape(c_idx, (1,)).astype(jnp.int32)
    chip_s = jnp.reshape(chip_idx, (1,)).astype(jnp.int32)

    col_sharded = ("ffn1_w_gate", "ffn1_w_up", "w_in", "ffn2_w_gate", "ffn2_w_up")
    use_order = ("ffn1_w_gate", "ffn1_w_up", "ffn1_w_down", "w_in", "w_out", "ffn2_w_gate", "ffn2_w_up", "ffn2_w_down")

    layer_axis = {k: (1 if k == "w_in" else 0) for k in use_order}

    def rows_view(k, a):
        if k not in col_sharded:
            return a
        return jnp.transpose(a, (2, 0, 1)) if k == "w_in" else jnp.swapaxes(a, 1, 2)

    def from_rows_view(k, a):
        if k not in col_sharded:
            return a
        return jnp.transpose(a, (1, 2, 0)) if k == "w_in" else jnp.swapaxes(a, 1, 2)

    w_rows = {k: rows_view(k, weights[k]) for k in use_order}

    def shard(key):
        k, l = key
        if k == "meta_tokens":
            return meta_tokens
        return _shard_bf16("weight_shard", w_rows[k], l, layer_axis[k] if k in col_sharded else 0)

    waiting = [("meta_tokens", 0)] + [(k, l) for l in range(depth) for k in use_order]
    after_first, after_second = [], []
    gathered = {}

    def fwd_carry(n_first):
        cy = _Carry()
        third = [(key, _gather_third(cy, buf)) for key, buf in after_second]
        second = [(key, _gather_second(cy, buf)) for key, buf in after_first]
        first = [(key, _gather_first(cy, shard(key))) for key in waiting[:n_first]]
        del waiting[:n_first]
        after_first.clear()
        after_second.clear()
        return cy, (first, second, third)

    def fwd_absorb(extra, plan):
        first, second, third = plan
        for key, idx in third:
            gathered[key] = extra[idx]
        for key, idx in second:
            after_second.append((key, extra[idx]))
        for key, idx in first:
            after_first.append((key, extra[idx]))

    def weight(k, l):
        key = (k, l)
        while key not in gathered:
            n = waiting.index(key) + 1 if key in waiting else 0
            cy, plan = fwd_carry(n)
            fwd_absorb(_comm_only("weights_gather", cy), plan)
        g = gathered[key]
        return g.reshape(-1, g.shape[-1])

    weight("w_in", 0)
    meta_full = jnp.swapaxes(weight("meta_tokens", 0).reshape(N_DEV, n_meta, -1), 0, 1).reshape(n_meta, D)
    slopes = jnp.asarray(2.0 ** (-8.0 * np.arange(1, hs + 1) / hs), dtype=F32)

    h = jnp.concatenate([jnp.zeros((cfg.pad, D), F32), meta_full, x[0]], axis=0)
    saved = []
    w_in_p = [None] * depth

    def mm_f(name, a, b, n_first=1, **kw):
        cy, plan = fwd_carry(n_first)
        out, extra = _matmul(name, a, b, carry=cy, **kw)
        fwd_absorb(extra, plan)
        return out

    def ffn_fwd(tag, l, h_in, norm, wg, wu, wd):
        xn = _rmsnorm_fwd(f"{tag}_norm", h_in, norm[l][None])
        wg_t, wu_t = weight(wg, l), weight(wu, l)
        cy, plan = fwd_carry(1)
        (gate, up, act), extra = _ffn_up(f"{tag}_up", xn, wg_t, wu_t, carry=cy)
        fwd_absorb(extra, plan)
        h_out = mm_f(f"{tag}_down", act, weight(wd, l), scale=0.5, residual=h_in, tm=1088, tn=512, tk=2816)
        return h_out, (xn, gate, up)

    for l in range(depth):
        st = {"h0": h}
        h, st["ffn1"] = ffn_fwd("ffn1", l, h, ffn1_norm, "ffn1_w_gate", "ffn1_w_up", "ffn1_w_down")
        st["h1"] = h
        xn = _rmsnorm_fwd("mix_norm", h, mix_norm[l][None])
        w_in_p[l] = _permute_w_in(cfg, weight("w_in", l))
        u = mm_f("mix_in", xn, w_in_p[l], trans_b=True, tm=1088, tn=640, tk=2048)
        gq, gk = _pair_gain(fox_q_norm[l]), _pair_gain(fox_k_norm[l])
        gsq, gsk = _pair_gain(swa_q_norm[l]), _pair_gain(swa_k_norm[l])
        bias = _scatter_heads(cfg, b_forget[l])
        qn, kn, fv, sqn, skd, svd, c, ct = _mix_prep("mix_prep", cfg, u, gq, gk, gsq, gsk, bias)
        cy, plan = fwd_carry(2)
        (o_fox, lse), extra = _fox_fwd("fox_fwd", cfg, qn, kn, fv, c, ct, carry=cy)
        fwd_absorb(extra, plan)
        cy, plan = fwd_carry(1)
        (o_swa,), extra = _swa_fwd("swa_fwd", cfg, sqn, skd, svd, swa_sinks[l], slopes, carry=cy)
        fwd_absorb(extra, plan)
        o_cat = _out_norm("out_norm", cfg, o_fox, o_swa, fox_out_norm[l][None], swa_out_norm[l][None])
        st["mix"] = (xn, u, gq, gk, gsq, gsk, bias, qn, kn, fv, sqn, skd, svd, c, ct, o_fox, lse, o_swa, o_cat)
        h = mm_f("mix_out", o_cat, weight("w_out", l), n_first=0, residual=h, tm=544, tn=1024, tk=2048)
        st["h2"] = h
        h, st["ffn2"] = ffn_fwd("ffn2", l, h, ffn2_norm, "ffn2_w_gate", "ffn2_w_up", "ffn2_w_down")
        saved.append(st)

    loss_blk, dh, dh16 = _loss_head("loss_head", h, loss_target[0])

    small = {k: [None] * depth for k in names if k not in use_order and k != "meta_tokens"}
    parts, landing = {}, {}
    to_sibling, to_chips = [], []

    def bwd_carry(n_chips):
        cy = _Carry()
        t1, t3 = [], []
        for k, l in list(to_chips):
            if len(t3) < n_chips and all(k != k3 for k3, _ in t3):
                to_chips.remove((k, l))
                t3.append((k, _scatter_chips(cy, parts[k], landing[k], l)))
        while to_sibling:
            k, l, g = to_sibling.pop(0)
            t1.append((k, l, g, _scatter_sibling(cy, g)))
        return cy, (t1, t3)

    def bwd_absorb(extra, plan):
        t1, t3 = plan
        for k, idx in t3:
            landing[k] = extra[idx]
        for k, l, g, idx in t1:
            parts[k] = _pair_add("grads_pair_add", g, extra[idx], parts[k], l, core_s)
            to_chips.append((k, l))

    def emit_grad(k, l, dw):
        r, C = dw.shape[0] // N_DEV, dw.shape[1]
        if k not in parts:
            parts[k] = lax.empty((depth, 4, r, C), BF16)
            landing[k] = lax.empty((3, depth, r, C), BF16)
        to_sibling.append((k, l, dw.reshape(4, 2, r, C)))

    def mm_b(name, a, b, n_chips=0, **kw):
        cy, plan = bwd_carry(n_chips)
        out, extra = _matmul(name, a, b, carry=cy, **kw)
        bwd_absorb(extra, plan)
        return out

    def ffn_bwd(tag, l, dh_out, dh_out16, h_in, st_, norm, wg, wu, wd):
        xn, gate, up = st_
        short = 1 if l == 0 else 0
        cy, plan = bwd_carry(1)
        (dgate, dup, act), extra = _ffn_bwd_act(f"{tag}_dact", dh_out16, weight(wd, l), gate, up, carry=cy)
        bwd_absorb(extra, plan)
        emit_grad(wd, l, mm_b(f"{tag}_dwd", act, dh_out16, short, trans_a=True, scale=0.5, out_dtype=BF16,
                              tm=1408, tn=1024, tk=2176))
        emit_grad(wg, l, mm_b(f"{tag}_dwg", dgate, xn, short, trans_a=True, out_dtype=BF16, tm=1408, tn=1024, tk=2176))
        emit_grad(wu, l, mm_b(f"{tag}_dwu", dup, xn, short, trans_a=True, out_dtype=BF16, tm=1408, tn=1024, tk=2176))
        dxn = mm_b(f"{tag}_dxn", dgate, weight(wg, l), 1, pair2=(dup, weight(wu, l)), tm=1088, tn=1024, tk=1408)
        dh_in, dh_in16, dg = _rmsnorm_bwd(f"{tag}_dnorm", dxn, h_in, norm[l][None], dh_out)
        return dh_in, dh_in16, dg[0]

    for l in reversed(range(depth)):
        st = saved[l]
        dh, dh16, small["ffn2_norm"][l] = ffn_bwd("ffn2", l, dh, dh16, st["h2"], st["ffn2"], ffn2_norm,
                                                   "ffn2_w_gate", "ffn2_w_up", "ffn2_w_down")
        xn, u, gq, gk, gsq, gsk, bias, qn, kn, fv, sqn, skd, svd, c, ct, o_fox, lse, o_swa, o_cat = st["mix"]
        dcat = mm_b("mix_dcat", dh16, weight("w_out", l), int(l == 0), trans_b=True, tm=544, tn=1024, tk=2048)
        emit_grad("w_out", l, mm_b("mix_dwout", o_cat, dh16, int(l == 0), trans_a=True, out_dtype=BF16, tm=512, tn=1024, tk=2176))
        do_fox, do_swa, dgf, dgs = _out_norm_bwd("out_norm_bwd", cfg, dcat, o_fox, o_swa,
                                                 fox_out_norm[l][None], swa_out_norm[l][None])
        small["fox_out_norm"][l], small["swa_out_norm"][l] = dgf[0], dgs[0]
        cy, plan = bwd_carry(3)
        (dqn, dkn, dfv, dct, dcq), extra = _fox_bwd("fox_bwd", cfg, qn, kn, fv, c, ct, o_fox, lse, do_fox, carry=cy)
        bwd_absorb(extra, plan)
        cy, plan = bwd_carry(1)
        (dsqn, dskd, dsvd, dsink), extra = _swa_bwd("swa_bwd", cfg, sqn, skd, svd, swa_sinks[l], slopes, o_swa, do_swa,
                                                    carry=cy)
        bwd_absorb(extra, plan)
        small["swa_sinks"][l] = dsink.reshape(hs // 2, 8, LANES)[:, 0, ::HEAD_DIM].reshape(hs)
        du, dgq, dgk, dgsq, dgsk, db = _mix_prep_bwd("mix_prep_bwd", cfg, u, gq, gk, gsq, gsk, bias,
                                                     dqn, dkn, dfv, dsqn, dskd, dsvd, dct, dcq)
        small["fox_q_norm"][l], small["fox_k_norm"][l] = _fold_pair(dgq), _fold_pair(dgk)
        small["swa_q_norm"][l], small["swa_k_norm"][l] = _fold_pair(dgsq), _fold_pair(dgsk)
        small["b_forget"][l] = _gather_heads(cfg, db[0])
        dwp = mm_b("mix_dwin", du, xn, int(l == 0), trans_a=True, out_dtype=BF16, tm=640, tn=1024, tk=2176)
        emit_grad("w_in", l, _unpermute_dw_in(cfg, dwp))
        dxn = mm_b("mix_dxn", du, w_in_p[l], int(l == 0), tm=544, tn=1024, tk=4480)
        dh, dh16, dg = _rmsnorm_bwd("mix_dnorm", dxn, st["h1"], mix_norm[l][None], dh)
        small["mix_norm"][l] = dg[0]
        dh, dh16, small["ffn1_norm"][l] = ffn_bwd("ffn1", l, dh, dh16, st["h0"], st["ffn1"], ffn1_norm,
                                                   "ffn1_w_gate", "ffn1_w_up", "ffn1_w_down")

    grad_x = dh[BLOCK:][None]
    dmeta = dh[cfg.pad:BLOCK]

    while to_sibling or to_chips:
        cy, plan = bwd_carry(len(use_order))
        bwd_absorb(_comm_only("grads_scatter", cy), plan)

    grads = {}
    for k in use_order:
        grads[k] = _chip_sum("grads_chip_sum", parts[k], landing[k], chip_s, layer_axis[k] if k in col_sharded else 0)

    small_names = list(small)
    pieces = [loss_blk[0, :1], dmeta.reshape(-1)] + [jnp.stack(small[k]).reshape(-1) for k in small_names]
    sizes = [int(p.shape[0]) for p in pieces]
    total = sum(sizes)
    padded = -(-total // (8 * LANES)) * (8 * LANES)
    vec = jnp.concatenate(pieces + [jnp.zeros((padded - total,), F32)]).reshape(-1, LANES)
    summed = _gather_sum("small_gather_sum", vec).reshape(-1)
    offs = np.cumsum([0] + sizes)
    loss = summed[0]
    dmeta_full = summed[offs[1]:offs[2]].reshape(n_meta, D)
    mcols = meta_tokens.shape[1]
    grads["meta_tokens"] = lax.dynamic_slice_in_dim(dmeta_full, dev_idx * mcols, mcols, axis=1)
    for n_, k in enumerate(small_names):
        grads[k] = summed[offs[2 + n_]:offs[3 + n_]].reshape(weights[k].shape)

    delta, new_m, new_v = {}, {}, {}
    for k in names:
        if k in col_sharded:
            outs = _adamw_nd("adamw", grads[k], w_rows[k], rows_view(k, mom_m[k]), rows_view(k, mom_v[k]))
            delta[k], new_m[k], new_v[k] = (from_rows_view(k, o) for o in outs)
            grads[k] = from_rows_view(k, grads[k])
        else:
            delta[k], new_m[k], new_v[k] = _adamw_nd("adamw", grads[k], weights[k], mom_m[k], mom_v[k])

    return (loss, grad_x, *[grads[k] for k in names], *[delta[k] for k in names],
            *[new_m[k] for k in names], *[new_v[k] for k in names])
```

```python
import functools

import numpy as np
import jax
import jax.numpy as jnp
from jax import lax
from jax.experimental import pallas as pl
from jax.experimental.pallas import tpu as pltpu

F32 = jnp.float32
BF16 = jnp.bfloat16
MESH = pl.DeviceIdType.MESH

HEAD_DIM = 64
BLOCK = 128
LANES = 128
N_DEV = 8
EPS = 1e-6
NEG_INF = -1e30
SCALE = HEAD_DIM ** -0.5

ADAM_LR = 0.001
ADAM_B1 = 0.9
ADAM_B2 = 0.999
ADAM_EPS = 1e-08
ADAM_WD = 0.01
ADAM_STEP = 10

VMEM_BYTES_V7X = 64 * 1024 * 1024
VMEM_LIMIT = VMEM_BYTES_V7X * 3 // 4

NT = (((1,), (1,)), ((), ()))
TN = (((0,), (0,)), ((), ()))
HI = lax.Precision.HIGHEST


def _cparams(sem=None, vmem=VMEM_LIMIT):
    return pltpu.CompilerParams(dimension_semantics=sem, vmem_limit_bytes=vmem)


def _tile(n, pref, mult):
    best = None
    for t in range(mult, min(n, pref) + 1, mult):
        if n % t == 0:
            best = t
    return best if best is not None else n


def _dot(a, b):
    return jnp.dot(a, b, preferred_element_type=F32)


def _dot_nt(a, b):
    return lax.dot_general(a, b, NT, preferred_element_type=F32)


def _dot_tn(a, b):
    return lax.dot_general(a, b, TN, preferred_element_type=F32)


def _lane(shape):
    return lax.broadcasted_iota(jnp.int32, shape, len(shape) - 1)


def _half_sum(x, lo):
    s0 = jnp.sum(jnp.where(lo, x, 0.0), axis=1, keepdims=True)
    s1 = jnp.sum(jnp.where(lo, 0.0, x), axis=1, keepdims=True)
    return jnp.where(lo, s0, s1)


def _sigmoid(x):
    return 1.0 / (1.0 + jnp.exp(-x))


def _place():
    return lax.axis_index("x"), lax.axis_index("y"), lax.axis_index("c")


def _peers(x, y, c):
    return [(x, y, 1 - c), (1 - x, y, c), (x, 1 - y, c), (1 - x, 1 - y, c)]


def _dev_index(dev):
    px, py, pc = dev
    return 4 * px + 2 * py + pc


class _Carry:
    def __init__(self):
        self.ins, self.outs, self.alias, self.items = [], [], {}, []
        self.nsem = self.nloc = 0

    def add(self, ins, outs, alias, nsem, nloc, build):
        i0, o0 = len(self.ins), len(self.outs)
        for src, dst in alias.items():
            self.alias[i0 + src] = o0 + dst
        self.items.append((i0, len(ins), o0, len(outs), self.nsem, self.nloc, build))
        self.ins += ins
        self.outs += outs
        self.nsem += nsem
        self.nloc += nloc
        return list(range(o0, o0 + len(outs)))

    def build(self, in_refs, out_refs, ssem, rsem, lsem):
        ops = []
        for i0, ni, o0, no, s0, l0, fn in self.items:
            ops.append(fn(in_refs[i0:i0 + ni], out_refs[o0:o0 + no],
                          lambda k, s0=s0: (ssem.at[s0 + k], rsem.at[s0 + k]), lambda k, l0=l0: lsem.at[l0 + k]))
        return ops


def _remote(src, dst, sems, dev):
    return pltpu.make_async_remote_copy(src_ref=src, dst_ref=dst, send_sem=sems[0], recv_sem=sems[1],
                                        device_id=dev, device_id_type=MESH)


def _gather_first(carry, shard):
    def build(ins, outs, sems, locs):
        src, buf = ins[0], outs[0]
        x, y, c = _place()
        me = _dev_index((x, y, c))
        peers = _peers(x, y, c)[:3]
        local = pltpu.make_async_copy(src, buf.at[me], locs(0))
        sends = [_remote(src, buf.at[me], sems(k), dev) for k, dev in enumerate(peers)]
        recvs = [_remote(src, buf.at[_dev_index(dev)], sems(k), dev) for k, dev in enumerate(peers)]

        def start():
            local.start()
            for cp in sends:
                cp.start()

        def wait():
            for cp in sends:
                cp.wait_send()
            for cp in recvs:
                cp.wait_recv()
            local.wait()

        return start, wait

    return carry.add([shard], [jax.ShapeDtypeStruct((N_DEV,) + shard.shape, shard.dtype)], {}, 3, 1, build)[0]


def _gather_second(carry, buf):
    def build(ins, outs, sems, locs):
        b = outs[0]
        x, y, c = _place()
        sibling, xn, yn, diag = _peers(x, y, c)
        relay_src = _dev_index((1 - x, y, c)) * (1 - c) + _dev_index((x, 1 - y, c)) * c
        relay_to = (x * (1 - c) + (1 - x) * c, (1 - y) * (1 - c) + y * c, c)
        sends = [_remote(b.at[relay_src], b.at[relay_src], sems(0), relay_to)]
        recvs = [_remote(b.at[relay_src], b.at[_dev_index(diag)], sems(0), relay_to)]
        for k, dev in enumerate((xn, yn)):
            sends.append(_remote(b.at[_dev_index(dev)], b.at[_dev_index(dev)], sems(1 + k), sibling))
            recvs.append(_remote(b.at[_dev_index(dev)], b.at[_dev_index((dev[0], dev[1], 1 - c))], sems(1 + k), sibling))

        def start():
            for cp in sends:
                cp.start()

        def wait():
            for cp in sends:
                cp.wait_send()
            for cp in recvs:
                cp.wait_recv()

        return start, wait

    return carry.add([buf], [jax.ShapeDtypeStruct(buf.shape, buf.dtype)], {0: 0}, 3, 0, build)[0]


def _gather_third(carry, buf):
    def build(ins, outs, sems, locs):
        b = outs[0]
        x, y, c = _place()
        sibling, _xn, _yn, diag = _peers(x, y, c)
        send = _remote(b.at[_dev_index(diag)], b.at[_dev_index(diag)], sems(0), sibling)
        recv = _remote(b.at[_dev_index(diag)], b.at[_dev_index((diag[0], diag[1], 1 - c))], sems(0), sibling)

        def wait():
            send.wait_send()
            recv.wait_recv()

        return send.start, wait

    return carry.add([buf], [jax.ShapeDtypeStruct(buf.shape, buf.dtype)], {0: 0}, 1, 0, build)[0]


def _scatter_sibling(carry, grad):
    def build(ins, outs, sems, locs):
        x, y, c = _place()
        cp = _remote(ins[0].at[:, 1 - c], outs[0], sems(0), (x, y, 1 - c))
        return cp.start, cp.wait

    shape = (grad.shape[0],) + grad.shape[2:]
    return carry.add([grad], [jax.ShapeDtypeStruct(shape, grad.dtype)], {}, 1, 0, build)[0]


def _scatter_chips(carry, parts, landing, layer):
    def build(ins, outs, sems, locs):
        x, y, c = _place()
        cps = [_remote(ins[0].at[layer, 2 * dev[0] + dev[1]], outs[0].at[k, layer], sems(k), dev)
               for k, dev in enumerate(_peers(x, y, c)[1:])]

        def start():
            for cp in cps:
                cp.start()

        def wait():
            for cp in cps:
                cp.wait()

        return start, wait

    return carry.add([parts, landing], [jax.ShapeDtypeStruct(landing.shape, landing.dtype)], {1: 0}, 3, 0, build)[0]


def _call(name, body, grid, in_specs, out_specs, out_shape, args, scratch=(), carry=None):
    ni, no, ns = len(args), len(out_shape), len(scratch)
    if carry is None or not carry.items:
        res = pl.pallas_call(
            body, name=name, grid=grid, in_specs=list(in_specs), out_specs=list(out_specs), out_shape=list(out_shape),
            scratch_shapes=list(scratch), compiler_params=_cparams(("arbitrary",) * len(grid)))(*args)
        return list(res), []
    nci, nco = len(carry.ins), len(carry.outs)

    def full_body(*refs):
        c_in = refs[ni:ni + nci]
        c_out = refs[ni + nci + no:ni + nci + no + nco]
        sc = refs[ni + nci + no + nco:]
        ops = carry.build(c_in, c_out, sc[ns], sc[ns + 1], sc[ns + 2])
        first = last = None
        for d, n in enumerate(grid):
            pid = pl.program_id(d)
            first = (pid == 0) if first is None else first & (pid == 0)
            last = (pid == n - 1) if last is None else last & (pid == n - 1)

        @pl.when(first)
        def _():
            for start, _w in ops:
                start()

        body(*refs[:ni], *refs[ni + nci:ni + nci + no], *sc[:ns])

        @pl.when(last)
        def _():
            for _s, wait in ops:
                wait()

    hbm = pl.BlockSpec(memory_space=pl.ANY)
    res = pl.pallas_call(
        full_body, name=name, grid=grid,
        in_specs=list(in_specs) + [hbm] * nci, out_specs=list(out_specs) + [hbm] * nco,
        out_shape=list(out_shape) + list(carry.outs),
        input_output_aliases={ni + s: no + d for s, d in carry.alias.items()},
        scratch_shapes=list(scratch) + [pltpu.SemaphoreType.DMA((carry.nsem,)), pltpu.SemaphoreType.DMA((carry.nsem,)),
                                        pltpu.SemaphoreType.DMA((max(carry.nloc, 1),))],
        compiler_params=_cparams(("arbitrary",) * len(grid)))(*args, *carry.ins)
    return list(res[:no]), list(res[no:])


def _comm_only(name, carry):
    return _call(name, lambda *refs: None, (1,), [], [], [], [], carry=carry)[1]


def _matmul(name, a, b, *, pair2=None, trans_a=False, trans_b=False, out_dtype=F32, scale=None, residual=None,
            tm=512, tn=512, tk=512, carry=None):
    if trans_a:
        K, M = a.shape
    else:
        M, K = a.shape
    if trans_b:
        N, Kb = b.shape
    else:
        Kb, N = b.shape
    assert K == Kb, (name, a.shape, b.shape)
    tm = _tile(M, tm, LANES if trans_a else 16)
    tn = _tile(N, tn, LANES)
    tk = _tile(K, tk, 16 if (trans_a and not trans_b) else LANES)
    nk = K // tk
    dims = (((0 if trans_a else 1,), (1 if trans_b else 0,)), ((), ()))
    pairs = [(a, b)] + ([pair2] if pair2 is not None else [])
    npair = len(pairs)

    def body(*refs):
        ab = refs[:2 * npair]
        pos = 2 * npair
        r_ref = None
        if residual is not None:
            r_ref = refs[pos]
            pos += 1
        o_ref = refs[pos]

        def partial():
            t = None
            for q in range(npair):
                d = lax.dot_general(ab[2 * q][...].astype(BF16), ab[2 * q + 1][...].astype(BF16), dims,
                                    preferred_element_type=F32)
                t = d if t is None else t + d
            return t

        def finish(r):
            if scale is not None:
                r = r * scale
            if r_ref is not None:
                r = r + r_ref[...].astype(F32)
            o_ref[...] = r.astype(o_ref.dtype)

        if nk == 1:
            finish(partial())
            return
        acc_ref = refs[pos + 1]
        k = pl.program_id(2)

        @pl.when(k == 0)
        def _():
            acc_ref[...] = partial()

        @pl.when(k > 0)
        def _():
            acc_ref[...] += partial()

        @pl.when(k == nk - 1)
        def _():
            finish(acc_ref[...])

    a_spec = pl.BlockSpec((tk, tm), lambda i, j, k: (k, i)) if trans_a else pl.BlockSpec((tm, tk), lambda i, j, k: (i, k))
    b_spec = pl.BlockSpec((tn, tk), lambda i, j, k: (j, k)) if trans_b else pl.BlockSpec((tk, tn), lambda i, j, k: (k, j))
    in_specs, args = [], []
    for pa, pb in pairs:
        in_specs += [a_spec, b_spec]
        args += [pa, pb]
    if residual is not None:
        in_specs.append(pl.BlockSpec((tm, tn), lambda i, j, k: (i, j)))
        args.append(residual)
    res, extra = _call(
        name, body, (M // tm, N // tn, nk), in_specs, [pl.BlockSpec((tm, tn), lambda i, j, k: (i, j))],
        [jax.ShapeDtypeStruct((M, N), out_dtype)], args,
        scratch=[pltpu.VMEM((tm, tn), F32)] if nk > 1 else [], carry=carry)
    return res[0], extra


def _shard_bf16(name, w, layer, layer_axis):
    R, C = w.shape[1 - layer_axis], w.shape[2]
    tr = _tile(R, 512, 16)

    def body(x_ref, o_ref):
        o_ref[...] = x_ref[...].astype(BF16)

    if layer_axis == 0:
        in_spec = pl.BlockSpec((None, tr, C), lambda i: (layer, i, 0))
    else:
        in_spec = pl.BlockSpec((tr, C), lambda i: (i, layer))
        w = w.reshape(R, -1)
    return pl.pallas_call(
        body, name=name, grid=(R // tr,),
        in_specs=[in_spec], out_specs=pl.BlockSpec((tr, C), lambda i: (i, 0)),
        out_shape=jax.ShapeDtypeStruct((R, C), BF16),
        compiler_params=_cparams(("parallel",)),
    )(w)


def _rmsnorm_fwd(name, h, g):
    L, D = h.shape
    tm = _tile(L, 256, 16)

    def body(h_ref, g_ref, o_ref):
        x = h_ref[...]
        r = lax.rsqrt(jnp.mean(x * x, axis=1, keepdims=True) + EPS)
        o_ref[...] = (x * r * g_ref[...]).astype(o_ref.dtype)

    return pl.pallas_call(
        body, name=name, grid=(L // tm,),
        in_specs=[pl.BlockSpec((tm, D), lambda i: (i, 0)), pl.BlockSpec((1, D), lambda i: (0, 0))],
        out_specs=pl.BlockSpec((tm, D), lambda i: (i, 0)),
        out_shape=jax.ShapeDtypeStruct((L, D), BF16),
        compiler_params=_cparams(("parallel",)),
    )(h, g)


def _rmsnorm_bwd(name, dy, h, g, dres):
    L, D = h.shape
    tm = _tile(L, 256, 16)

    def body(dy_ref, h_ref, g_ref, dres_ref, dh_ref, dh16_ref, dg_ref):
        i = pl.program_id(0)
        x = h_ref[...]
        dyv = dy_ref[...].astype(F32)
        r = lax.rsqrt(jnp.mean(x * x, axis=1, keepdims=True) + EPS)
        w = dyv * g_ref[...]
        proj = jnp.sum(w * x, axis=1, keepdims=True) * (1.0 / D)
        dh = dres_ref[...] + r * w - x * (r * r * r * proj)
        dh_ref[...] = dh
        dh16_ref[...] = dh.astype(BF16)

        @pl.when(i == 0)
        def _():
            dg_ref[...] = jnp.zeros_like(dg_ref)

        dg_ref[...] += jnp.sum(dyv * x * r, axis=0, keepdims=True)

    return pl.pallas_call(
        body, name=name, grid=(L // tm,),
        in_specs=[pl.BlockSpec((tm, D), lambda i: (i, 0)), pl.BlockSpec((tm, D), lambda i: (i, 0)),
                  pl.BlockSpec((1, D), lambda i: (0, 0)), pl.BlockSpec((tm, D), lambda i: (i, 0))],
        out_specs=[pl.BlockSpec((tm, D), lambda i: (i, 0)), pl.BlockSpec((tm, D), lambda i: (i, 0)),
                   pl.BlockSpec((1, D), lambda i: (0, 0))],
        out_shape=[jax.ShapeDtypeStruct((L, D), F32), jax.ShapeDtypeStruct((L, D), BF16),
                   jax.ShapeDtypeStruct((1, D), F32)],
        compiler_params=_cparams(("arbitrary",)),
    )(dy, h, g, dres)


def _ffn_up(name, xn, wgT, wuT, carry=None):
    L, D = xn.shape
    F = wgT.shape[0]
    tm = _tile(L, 1088, 16)
    tn = _tile(F, 512, LANES)

    def body(x_ref, wg_ref, wu_ref, g_ref, u_ref, a_ref):
        x = x_ref[...]
        g = _dot_nt(x, wg_ref[...])
        u = _dot_nt(x, wu_ref[...])
        g_ref[...] = g.astype(BF16)
        u_ref[...] = u.astype(BF16)
        a_ref[...] = (g * _sigmoid(g) * u).astype(BF16)

    o_spec = pl.BlockSpec((tm, tn), lambda i, j: (i, j))
    o_shape = jax.ShapeDtypeStruct((L, F), BF16)
    return _call(
        name, body, (L // tm, F // tn),
        [pl.BlockSpec((tm, D), lambda i, j: (i, 0)), pl.BlockSpec((tn, D), lambda i, j: (j, 0)),
         pl.BlockSpec((tn, D), lambda i, j: (j, 0))],
        [o_spec, o_spec, o_spec], [o_shape, o_shape, o_shape], [xn, wgT, wuT], carry=carry)


def _ffn_bwd_act(name, dh, wd, gate, up, carry=None):
    L, D = dh.shape
    F = wd.shape[0]
    tm = _tile(L, 1088, 16)
    tn = _tile(F, 512, LANES)

    def body(dh_ref, wd_ref, g_ref, u_ref, dg_ref, du_ref, a_ref):
        da = 0.5 * _dot_nt(dh_ref[...].astype(BF16), wd_ref[...])
        g = g_ref[...].astype(F32)
        u = u_ref[...].astype(F32)
        sg = _sigmoid(g)
        silu = g * sg
        dg_ref[...] = (da * u * (sg * (1.0 + g * (1.0 - sg)))).astype(BF16)
        du_ref[...] = (da * silu).astype(BF16)
        a_ref[...] = (silu * u).astype(BF16)

    o_spec = pl.BlockSpec((tm, tn), lambda i, j: (i, j))
    o_shape = jax.ShapeDtypeStruct((L, F), BF16)
    return _call(
        name, body, (L // tm, F // tn),
        [pl.BlockSpec((tm, D), lambda i, j: (i, 0)), pl.BlockSpec((tn, D), lambda i, j: (j, 0)), o_spec, o_spec],
        [o_spec, o_spec, o_spec], [o_shape, o_shape, o_shape], [dh, wd, gate, up], carry=carry)


class _Cfg:
    def __init__(self, S, D, hf, hs, hkv):
        self.S, self.D, self.L = S, D, S + BLOCK
        self.hf, self.hs, self.hkv = hf, hs, hkv
        self.wf, self.ws = hf * HEAD_DIM, hs * HEAD_DIM
        self.group = hs // hkv
        self.cs = 2 * LANES // hf
        self.n_meta = 16
        self.pad = BLOCK - self.n_meta
        self.o_fk = self.wf
        self.o_fv = 2 * self.wf
        self.o_sq = 3 * self.wf
        self.o_sk = self.o_sq + self.ws
        self.o_sv = self.o_sk + LANES
        self.o_fz = self.o_sv + LANES
        self.up = self.o_fz + LANES
        assert hkv == 2 and hf % 2 == 0 and self.group % 2 == 0 and self.cs % 8 == 0
        assert self.o_sq % self.ws == 0 and (2 * self.wf) % LANES == 0


def _head_norm(x, gain, lo, mult):
    r = lax.rsqrt(_half_sum(x * x, lo) * (1.0 / HEAD_DIM) + EPS)
    return x * r * (gain * mult)


def _head_norm_bwd(dy, x, gain, lo, mult):
    r = lax.rsqrt(_half_sum(x * x, lo) * (1.0 / HEAD_DIM) + EPS)
    w = dy * (gain * mult)
    proj = _half_sum(w * x, lo) * (1.0 / HEAD_DIM)
    dx = r * w - x * (r * r * r * proj)
    dgain = jnp.sum(dy * mult * x * r, axis=0, keepdims=True)
    return dx, dgain


def _dup(x, lo):
    xr = pltpu.roll(x, 64, 1)
    return jnp.where(lo, x, xr), jnp.where(lo, xr, x)


def _mix_prep(name, cfg, u, gq, gk, gsq, gsk, bias):
    L, wf, ws = cfg.L, cfg.wf, cfg.ws
    T = BLOCK
    npf, nps = wf // LANES, ws // LANES

    def body(fqk_ref, fv_ref, sq_ref, sk_ref, sv_ref, fz_ref, gq_ref, gk_ref, gsq_ref, gsk_ref, b_ref,
             qn_ref, kn_ref, fvo_ref, sqn_ref, skd_ref, svd_ref, c_ref, ct_ref, carry_ref):
        i = pl.program_id(0)
        lo = _lane((T, LANES)) < HEAD_DIM
        for p in range(npf):
            sl = slice(p * LANES, (p + 1) * LANES)
            qn_ref[:, sl] = _head_norm(fqk_ref[:, sl], gq_ref[...], lo, SCALE).astype(BF16)
            kn_ref[:, sl] = _head_norm(fqk_ref[:, wf + p * LANES: wf + (p + 1) * LANES], gk_ref[...], lo, 1.0).astype(BF16)
        fvo_ref[...] = fv_ref[...].astype(BF16)
        for p in range(nps):
            sl = slice(p * LANES, (p + 1) * LANES)
            sqn_ref[:, sl] = _head_norm(sq_ref[:, sl], gsq_ref[...], lo, SCALE).astype(BF16)
        k0, k1 = _dup(_head_norm(sk_ref[...], gsk_ref[...], lo, 1.0), lo)
        skd_ref[:, :LANES] = k0.astype(BF16)
        skd_ref[:, LANES:] = k1.astype(BF16)
        v0, v1 = _dup(sv_ref[...], lo)
        svd_ref[:, :LANES] = v0.astype(BF16)
        svd_ref[:, LANES:] = v1.astype(BF16)

        @pl.when(i == 0)
        def _():
            carry_ref[...] = jnp.zeros_like(carry_ref)

        z = fz_ref[...] + b_ref[...]
        lf = jnp.minimum(z, 0.0) - jnp.log(1.0 + jnp.exp(-jnp.abs(z)))
        row = lax.broadcasted_iota(jnp.int32, (T, T), 0)
        col = lax.broadcasted_iota(jnp.int32, (T, T), 1)
        tri = jnp.where(col <= row, 1.0, 0.0).astype(F32)
        c = jnp.dot(tri, lf, precision=HI, preferred_element_type=F32) + carry_ref[0:1, :]
        c_ref[...] = c
        ct_ref[...] = c.T
        carry_ref[0:1, :] = c_ref[T - 1:T, :]

    def rows(w, cb):
        return pl.BlockSpec((T, w), lambda i, cb=cb: (i, cb))

    vec = pl.BlockSpec((1, LANES), lambda i: (0, 0))
    return pl.pallas_call(
        body, name=name, grid=(L // T,),
        in_specs=[rows(2 * wf, 0), rows(wf, 2), rows(ws, cfg.o_sq // ws), rows(LANES, cfg.o_sk // LANES),
                  rows(LANES, cfg.o_sv // LANES), rows(LANES, cfg.o_fz // LANES), vec, vec, vec, vec, vec],
        out_specs=[rows(wf, 0), rows(wf, 0), rows(wf, 0), rows(ws, 0), rows(2 * LANES, 0), rows(2 * LANES, 0),
                   rows(LANES, 0), pl.BlockSpec((LANES, T), lambda i: (0, i))],
        out_shape=[jax.ShapeDtypeStruct((L, wf), BF16)] * 3 + [jax.ShapeDtypeStruct((L, ws), BF16)]
        + [jax.ShapeDtypeStruct((L, 2 * LANES), BF16)] * 2
        + [jax.ShapeDtypeStruct((L, LANES), F32), jax.ShapeDtypeStruct((LANES, L), F32)],
        scratch_shapes=[pltpu.VMEM((8, LANES), F32)],
        compiler_params=_cparams(("arbitrary",)),
    )(u, u, u, u, u, u, gq, gk, gsq, gsk, bias)


def _mix_prep_bwd(name, cfg, u, gq, gk, gsq, gsk, bias, dqn, dkn, dfv, dsqn, dskd, dsvd, dct, dcq):
    L, wf, ws = cfg.L, cfg.wf, cfg.ws
    T = BLOCK
    nb = L // T
    npf, nps = wf // LANES, ws // LANES

    def body(fqk_ref, sq_ref, sk_ref, fz_ref, gq_ref, gk_ref, gsq_ref, gsk_ref, b_ref,
             dqn_ref, dkn_ref, dfv_ref, dsqn_ref, dskd_ref, dsvd_ref, dct_ref, dcq_ref,
             du_ref, dgq_ref, dgk_ref, dgsq_ref, dgsk_ref, db_ref, carry_ref):
        i = pl.program_id(0)
        lo = _lane((T, LANES)) < HEAD_DIM

        @pl.when(i == 0)
        def _():
            carry_ref[...] = jnp.zeros_like(carry_ref)
            for r in (dgq_ref, dgk_ref, dgsq_ref, dgsk_ref, db_ref):
                r[...] = jnp.zeros_like(r)

        accq = jnp.zeros((1, LANES), F32)
        acck = jnp.zeros((1, LANES), F32)
        for p in range(npf):
            sl = slice(p * LANES, (p + 1) * LANES)
            dx, dg = _head_norm_bwd(dqn_ref[:, sl], fqk_ref[:, sl], gq_ref[...], lo, SCALE)
            du_ref[:, sl] = dx.astype(BF16)
            accq = accq + dg
            slk = slice(wf + p * LANES, wf + (p + 1) * LANES)
            dx, dg = _head_norm_bwd(dkn_ref[:, sl], fqk_ref[:, slk], gk_ref[...], lo, 1.0)
            du_ref[:, slk] = dx.astype(BF16)
            acck = acck + dg
        dgq_ref[...] += accq
        dgk_ref[...] += acck
        du_ref[:, cfg.o_fv:cfg.o_fv + wf] = dfv_ref[...].astype(BF16)
        accs = jnp.zeros((1, LANES), F32)
        for p in range(nps):
            sl = slice(p * LANES, (p + 1) * LANES)
            dx, dg = _head_norm_bwd(dsqn_ref[:, sl], sq_ref[:, sl], gsq_ref[...], lo, SCALE)
            du_ref[:, cfg.o_sq + p * LANES: cfg.o_sq + (p + 1) * LANES] = dx.astype(BF16)
            accs = accs + dg
        dgsq_ref[...] += accs

        def fold(ref):
            a0, a1 = ref[:, :LANES], ref[:, LANES:]
            return jnp.where(lo, a0 + pltpu.roll(a0, 64, 1), a1 + pltpu.roll(a1, 64, 1))

        dx, dg = _head_norm_bwd(fold(dskd_ref), sk_ref[...], gsk_ref[...], lo, 1.0)
        du_ref[:, cfg.o_sk:cfg.o_sk + LANES] = dx.astype(BF16)
        dgsk_ref[...] += dg
        du_ref[:, cfg.o_sv:cfg.o_sv + LANES] = fold(dsvd_ref).astype(BF16)

        dc = dct_ref[...].T + dcq_ref[...]
        row = lax.broadcasted_iota(jnp.int32, (T, T), 0)
        col = lax.broadcasted_iota(jnp.int32, (T, T), 1)
        triu = jnp.where(col >= row, 1.0, 0.0).astype(F32)
        dlf = jnp.dot(triu, dc, precision=HI, preferred_element_type=F32) + carry_ref[0:1, :]
        carry_ref[0:1, :] = dlf[0:1, :]
        z = fz_ref[...] + b_ref[...]
        dz = dlf * _sigmoid(-z)
        du_ref[:, cfg.o_fz:cfg.o_fz + LANES] = dz.astype(BF16)
        db_ref[...] += jnp.sum(dz, axis=0, keepdims=True)

    def rows(w, cb):
        return pl.BlockSpec((T, w), lambda i, cb=cb: (nb - 1 - i, cb))

    vec = pl.BlockSpec((1, LANES), lambda i: (0, 0))
    vshape = jax.ShapeDtypeStruct((1, LANES), F32)
    return pl.pallas_call(
        body, name=name, grid=(nb,),
        in_specs=[rows(2 * wf, 0), rows(ws, cfg.o_sq // ws), rows(LANES, cfg.o_sk // LANES),
                  rows(LANES, cfg.o_fz // LANES), vec, vec, vec, vec, vec,
                  rows(wf, 0), rows(wf, 0), rows(wf, 0), rows(ws, 0), rows(2 * LANES, 0), rows(2 * LANES, 0),
                  pl.BlockSpec((LANES, T), lambda i: (0, nb - 1 - i)), rows(LANES, 0)],
        out_specs=[rows(cfg.up, 0), vec, vec, vec, vec, vec],
        out_shape=[jax.ShapeDtypeStruct((L, cfg.up), BF16)] + [vshape] * 5,
        scratch_shapes=[pltpu.VMEM((8, LANES), F32)],
        compiler_params=_cparams(("arbitrary",)),
    )(u, u, u, u, gq, gk, gsq, gsk, bias, dqn, dkn, dfv, dsqn, dskd, dsvd, dct, dcq)


def _fox_fwd(name, cfg, qn, kn, fv, c, ct, carry=None):
    L, wf, cs = cfg.L, cfg.wf, cfg.cs
    TQ = BLOCK
    TK = _tile(L, 544, 8)
    npairs = wf // LANES
    pad = cfg.pad

    def body(q_ref, k_ref, v_ref, c_ref, ct_ref, o_ref, lse_ref):
        p = pl.program_id(0)
        i = pl.program_id(1)
        lo_q = _lane((TQ, LANES)) < HEAD_DIM
        lane_k = _lane((TK, LANES))
        lo_k = lane_k < HEAD_DIM
        lo_d = lax.broadcasted_iota(jnp.int32, (LANES, TQ), 0) < HEAD_DIM
        first = _lane((TK, 2 * TQ)) < TQ
        q = q_ref[...]
        qs = jnp.concatenate([jnp.where(lo_q, q, jnp.zeros_like(q)), jnp.where(lo_q, jnp.zeros_like(q), q)], axis=0)
        cq = jnp.concatenate([ct_ref[0:1, :], ct_ref[1:2, :]], axis=1)
        qrow = lax.broadcasted_iota(jnp.int32, (TK, 2 * TQ), 1)
        qpos = i * TQ + jnp.where(first, qrow, qrow - TQ)

        def step(j, carry_):
            m, l, acc = carry_
            off = pl.multiple_of(j * TK, 8)
            k = k_ref[pl.ds(off, TK), :]
            v = v_ref[pl.ds(off, TK), :]
            cblk = c_ref[pl.ds(off, TK), :]
            kpos = j * TK + lax.broadcasted_iota(jnp.int32, (TK, 2 * TQ), 0)
            allowed = (kpos <= qpos) & (kpos >= pad)
            ck0 = jnp.sum(jnp.where(lane_k == p * cs, cblk, 0.0), axis=1, keepdims=True)
            ck1 = jnp.sum(jnp.where(lane_k == p * cs + 1, cblk, 0.0), axis=1, keepdims=True)
            s = _dot_nt(k, qs) + cq - jnp.where(first, ck0, ck1)
            s = jnp.where(allowed, s, NEG_INF)
            m_new = jnp.maximum(m, jnp.max(s, axis=0, keepdims=True))
            alpha = jnp.exp(m - m_new)
            pr = jnp.exp(s - m_new)
            l = alpha * l + jnp.sum(pr, axis=0, keepdims=True)
            prb = pr.astype(BF16)
            prs = jnp.concatenate([prb[:, :TQ], prb[:, TQ:]], axis=0)
            vs = jnp.concatenate([jnp.where(lo_k, v, jnp.zeros_like(v)), jnp.where(lo_k, jnp.zeros_like(v), v)], axis=0)
            acc = acc * jnp.where(lo_d, alpha[:, :TQ], alpha[:, TQ:]) + _dot_tn(vs, prs)
            return m_new, l, acc

        init = (jnp.full((1, 2 * TQ), NEG_INF, F32), jnp.zeros((1, 2 * TQ), F32), jnp.zeros((LANES, TQ), F32))
        m, l, acc = lax.fori_loop(0, ((i + 1) * TQ + TK - 1) // TK, step, init)
        o_ref[...] = (acc / jnp.where(lo_d, l[:, :TQ], l[:, TQ:])).T
        lse = m + jnp.log(l)
        lse_ref[...] = jnp.where(lo_d, lse[:, :TQ], lse[:, TQ:]).T

    blk = pl.BlockSpec((TQ, LANES), lambda p, i: (i, p))
    full = pl.BlockSpec((L, LANES), lambda p, i: (0, p))
    return _call(
        name, body, (npairs, L // TQ),
        [blk, full, full, pl.BlockSpec((L, LANES), lambda p, i: (0, 0)), pl.BlockSpec((cs, TQ), lambda p, i: (p, i))],
        [blk, blk], [jax.ShapeDtypeStruct((L, wf), F32)] * 2, [qn, kn, fv, c, ct], carry=carry)


def _fox_bwd(name, cfg, qn, kn, fv, c, ct, o, lse, do, carry=None):
    L, wf, cs = cfg.L, cfg.wf, cfg.cs
    T = BLOCK
    TQ = _tile(L, 544, 8)
    nb = L // T
    nq = L // TQ
    npairs = wf // LANES
    pad = cfg.pad

    def body(q_ref, k_ref, v_ref, c_ref, ct_ref, o_ref, lse_ref, do_ref, dq_ref, dk_ref, dv_ref, dct_ref, dcq_ref,
             cq_ref, lser_ref, dsum_ref, dsacc_ref):
        p = pl.program_id(0)
        j = pl.program_id(1)
        lane = _lane((TQ, LANES))
        lo = lane < HEAD_DIM
        sels = (lo, jnp.logical_not(lo))

        def stack(x):
            return jnp.concatenate([jnp.where(lo, x, jnp.zeros_like(x)), jnp.where(lo, jnp.zeros_like(x), x)], axis=0)

        @pl.when((j == 0) & (p == 0))
        def _():
            dcq_ref[...] = jnp.zeros_like(dcq_ref)

        @pl.when(j == 0)
        def _():
            dq_ref[...] = jnp.zeros_like(dq_ref)
            dsacc_ref[...] = jnp.zeros_like(dsacc_ref)
            for t in range(nq):
                rows = slice(t * TQ, (t + 1) * TQ)
                dd = do_ref[rows, :] * o_ref[rows, :]
                lse_b = lse_ref[rows, :]
                cblk = c_ref[rows, :]
                for hh in range(2):
                    half = slice(hh * TQ, (hh + 1) * TQ)
                    dsum_ref[t, half, :] = jnp.broadcast_to(
                        jnp.sum(jnp.where(sels[hh], dd, 0.0), axis=1, keepdims=True), (TQ, LANES))
                    lser_ref[t, half, :] = jnp.broadcast_to(
                        jnp.sum(jnp.where(lane == hh * HEAD_DIM, lse_b, 0.0), axis=1, keepdims=True), (TQ, LANES))
                    cq_ref[t, half, :] = jnp.broadcast_to(
                        jnp.sum(jnp.where(lane == p * cs + hh, cblk, 0.0), axis=1, keepdims=True), (TQ, LANES))

        k = k_ref[...]
        v = v_ref[...]
        ck = jnp.concatenate([jnp.broadcast_to(ct_ref[0:1, :], (TQ, T)), jnp.broadcast_to(ct_ref[1:2, :], (TQ, T))], axis=0)
        row = lax.broadcasted_iota(jnp.int32, (2 * TQ, T), 0)
        qrow = jnp.where(row < TQ, row, row - TQ)
        kpos = j * T + lax.broadcasted_iota(jnp.int32, (2 * TQ, T), 1)

        def step(i, carry_):
            dk, dv, dc0, dc1 = carry_
            off = pl.multiple_of(i * TQ, 8)
            rows = pl.ds(off, TQ)
            qs = stack(q_ref[rows, :])
            dos = stack(do_ref[rows, :]).astype(BF16)
            allowed = (kpos <= i * TQ + qrow) & (kpos >= pad)
            s = _dot_nt(qs, k) + cq_ref[i] - ck
            pr = jnp.where(allowed, jnp.exp(jnp.where(allowed, s, NEG_INF) - lser_ref[i]), 0.0)
            ds = pr * (_dot_nt(dos, v) - dsum_ref[i])
            dsb = ds.astype(BF16)
            dv = dv + _dot_tn(pr.astype(BF16), dos)
            dk = dk + _dot_tn(dsb, qs)
            dqs = _dot(dsb, k)
            dq_ref[rows, :] += jnp.where(lo, dqs[:TQ], dqs[TQ:])
            dc0 = dc0 - jnp.sum(ds[:TQ], axis=0, keepdims=True)
            dc1 = dc1 - jnp.sum(ds[TQ:], axis=0, keepdims=True)
            dsacc_ref[i] += ds
            return dk, dv, dc0, dc1

        init = (jnp.zeros((T, LANES), F32), jnp.zeros((T, LANES), F32),
                jnp.zeros((1, T), F32), jnp.zeros((1, T), F32))
        dk, dv, dc0, dc1 = lax.fori_loop((j * T) // TQ, nq, step, init)
        dk_ref[...] = dk
        dv_ref[...] = dv
        dct_ref[...] = jnp.zeros_like(dct_ref)
        dct_ref[0:1, :] = dc0
        dct_ref[1:2, :] = dc1

        @pl.when(j == nb - 1)
        def _():
            for t in range(nq):
                upd = jnp.zeros((TQ, LANES), F32)
                for hh in range(2):
                    upd = upd + jnp.where(lane == p * cs + hh,
                                          jnp.sum(dsacc_ref[t, hh * TQ:(hh + 1) * TQ, :], axis=1, keepdims=True), 0.0)
                dcq_ref[t * TQ:(t + 1) * TQ, :] += upd

    blk = pl.BlockSpec((T, LANES), lambda p, j: (j, p))
    full = pl.BlockSpec((L, LANES), lambda p, j: (0, p))
    return _call(
        name, body, (npairs, nb),
        [full, blk, blk, pl.BlockSpec((L, LANES), lambda p, j: (0, 0)), pl.BlockSpec((cs, T), lambda p, j: (p, j)),
         full, full, full],
        [full, blk, blk, pl.BlockSpec((cs, T), lambda p, j: (p, j)), pl.BlockSpec((L, LANES), lambda p, j: (0, 0))],
        [jax.ShapeDtypeStruct((L, wf), F32)] * 3
        + [jax.ShapeDtypeStruct((LANES, L), F32), jax.ShapeDtypeStruct((L, LANES), F32)],
        [qn, kn, fv, c, ct, o, lse, do], scratch=[pltpu.VMEM((nq, 2 * TQ, LANES), F32)] * 4, carry=carry)


def _swa_band(i, pad):
    T = BLOCK
    t = lax.broadcasted_iota(jnp.int32, (T, 2 * T), 0)
    u = lax.broadcasted_iota(jnp.int32, (T, 2 * T), 1)
    dist = t - u + T
    allowed = (dist >= 0) & (dist < T) & ((i - 1) * T + u >= pad)
    return dist.astype(F32), allowed


def _swa_probs(s, dist, allowed, slope, sink):
    s = jnp.where(allowed, s - slope * dist, NEG_INF)
    m = jnp.maximum(jnp.max(s, axis=1, keepdims=True), sink)
    e = jnp.exp(s - m)
    es = jnp.exp(sink - m)
    den = jnp.sum(e, axis=1, keepdims=True) + es
    return e / den, es / den


def _swa_stack(x, group, lo):
    tiles = []
    for h in range(group):
        pair = x[:, (h // 2) * LANES:(h // 2 + 1) * LANES]
        tiles.append(jnp.where(lo if h % 2 == 0 else jnp.logical_not(lo), pair, jnp.zeros_like(pair)))
    return jnp.concatenate(tiles, axis=0)


def _swa_fwd(name, cfg, sqn, skd, svd, sinks, slopes, carry=None):
    L, ws, group, hkv = cfg.L, cfg.ws, cfg.group, cfg.hkv
    T = BLOCK
    gw = group * HEAD_DIM
    pad = cfg.pad

    def body(sink_ref, slope_ref, q_ref, kp_ref, kc_ref, vp_ref, vc_ref, o_ref):
        kv = pl.program_id(0)
        i = pl.program_id(1)
        lo = _lane((T, LANES)) < HEAD_DIM
        dist, allowed = _swa_band(i, pad)
        kb = jnp.concatenate([kp_ref[...], kc_ref[...]], axis=0)
        vb = jnp.concatenate([vp_ref[...], vc_ref[...]], axis=0)
        s = _dot_nt(_swa_stack(q_ref[...], group, lo), kb)
        probs = []
        for h in range(group):
            p, _ = _swa_probs(s[h * T:(h + 1) * T], dist, allowed, slope_ref[kv * group + h], sink_ref[kv * group + h])
            probs.append(p.astype(BF16))
        o = _dot(jnp.concatenate(probs, axis=0), vb)
        for pp in range(group // 2):
            o_ref[:, pp * LANES:(pp + 1) * LANES] = jnp.where(lo, o[2 * pp * T:(2 * pp + 1) * T],
                                                              o[(2 * pp + 1) * T:(2 * pp + 2) * T])

    smem = pl.BlockSpec(memory_space=pltpu.SMEM)
    prev = pl.BlockSpec((T, LANES), lambda kv, i: (jnp.maximum(i - 1, 0), kv))
    cur = pl.BlockSpec((T, LANES), lambda kv, i: (i, kv))
    qblk = pl.BlockSpec((T, gw), lambda kv, i: (i, kv))
    return _call(name, body, (hkv, L // T), [smem, smem, qblk, prev, cur, prev, cur], [qblk],
                 [jax.ShapeDtypeStruct((L, ws), F32)], [sinks, slopes, sqn, skd, skd, svd, svd], carry=carry)


def _swa_bwd(name, cfg, sqn, skd, svd, sinks, slopes, o, do, carry=None):
    L, ws, group, hkv = cfg.L, cfg.ws, cfg.group, cfg.hkv
    T = BLOCK
    gw = group * HEAD_DIM
    ppk = group // 2
    pad = cfg.pad

    def body(sink_ref, slope_ref, q_ref, kp_ref, kc_ref, vp_ref, vc_ref, o_ref, do_ref,
             dq_ref, dk_ref, dv_ref, dsink_ref):
        kv = pl.program_id(0)
        i = pl.program_id(1)
        lane = _lane((T, LANES))
        lo = lane < HEAD_DIM

        @pl.when(i == 0)
        def _():
            dk_ref[...] = jnp.zeros_like(dk_ref)
            dv_ref[...] = jnp.zeros_like(dv_ref)
            dsink_ref[...] = jnp.zeros_like(dsink_ref)

        dist, allowed = _swa_band(i, pad)
        kb = jnp.concatenate([kp_ref[...], kc_ref[...]], axis=0)
        vb = jnp.concatenate([vp_ref[...], vc_ref[...]], axis=0)
        dov = do_ref[...]
        qs = _swa_stack(q_ref[...], group, lo)
        dos = _swa_stack(dov, group, lo).astype(BF16)
        dd = dov * o_ref[...]
        s = _dot_nt(qs, kb)
        dp = _dot_nt(dos, vb)
        probs, dss = [], []
        for h in range(group):
            pair = dd[:, (h // 2) * LANES:(h // 2 + 1) * LANES]
            dsum = jnp.sum(jnp.where(lo if h % 2 == 0 else jnp.logical_not(lo), pair, 0.0), axis=1, keepdims=True)
            p, ps = _swa_probs(s[h * T:(h + 1) * T], dist, allowed, slope_ref[kv * group + h], sink_ref[kv * group + h])
            probs.append(p.astype(BF16))
            dss.append((p * (dp[h * T:(h + 1) * T] - dsum)).astype(BF16))
            dsink_ref[8 * (h // 2):8 * (h // 2) + 1, :] += jnp.where(lane[0:1, :] == (h % 2) * HEAD_DIM,
                                                                    -jnp.sum(ps * dsum), 0.0)
        ds = jnp.concatenate(dss, axis=0)
        dq = _dot(ds, kb)
        for pp in range(ppk):
            dq_ref[:, pp * LANES:(pp + 1) * LANES] = jnp.where(lo, dq[2 * pp * T:(2 * pp + 1) * T],
                                                               dq[(2 * pp + 1) * T:(2 * pp + 2) * T])
        dk = _dot_tn(ds, qs)
        dv = _dot_tn(jnp.concatenate(probs, axis=0), dos)
        cur = pl.multiple_of(i * T, T)
        dk_ref[pl.ds(cur, T), :] += dk[T:]
        dv_ref[pl.ds(cur, T), :] += dv[T:]

        @pl.when(i > 0)
        def _():
            prv = pl.multiple_of((i - 1) * T, T)
            dk_ref[pl.ds(prv, T), :] += dk[:T]
            dv_ref[pl.ds(prv, T), :] += dv[:T]

    smem = pl.BlockSpec(memory_space=pltpu.SMEM)
    prev = pl.BlockSpec((T, LANES), lambda kv, i: (jnp.maximum(i - 1, 0), kv))
    cur = pl.BlockSpec((T, LANES), lambda kv, i: (i, kv))
    qblk = pl.BlockSpec((T, gw), lambda kv, i: (i, kv))
    full = pl.BlockSpec((L, LANES), lambda kv, i: (0, kv))
    return _call(
        name, body, (hkv, L // T), [smem, smem, qblk, prev, cur, prev, cur, qblk, qblk],
        [qblk, full, full, pl.BlockSpec((8 * ppk, LANES), lambda kv, i: (kv, 0))],
        [jax.ShapeDtypeStruct((L, ws), F32), jax.ShapeDtypeStruct((L, 2 * LANES), F32),
         jax.ShapeDtypeStruct((L, 2 * LANES), F32), jax.ShapeDtypeStruct((8 * ppk * hkv, LANES), F32)],
        [sinks, slopes, sqn, skd, skd, svd, svd, o, do], carry=carry)


def _out_norm(name, cfg, o_fox, o_swa, g_fox, g_swa):
    L, wf, ws = cfg.L, cfg.wf, cfg.ws
    tm = _tile(L, 256, 16)

    def body(of_ref, os_ref, gf_ref, gs_ref, o_ref):
        for src, g_ref, lo_, w in ((of_ref, gf_ref, 0, wf), (os_ref, gs_ref, wf, ws)):
            x = src[...]
            r = lax.rsqrt(jnp.mean(x * x, axis=1, keepdims=True) + EPS)
            o_ref[:, lo_:lo_ + w] = (x * r * g_ref[...]).astype(BF16)

    return pl.pallas_call(
        body, name=name, grid=(L // tm,),
        in_specs=[pl.BlockSpec((tm, wf), lambda i: (i, 0)), pl.BlockSpec((tm, ws), lambda i: (i, 0)),
                  pl.BlockSpec((1, wf), lambda i: (0, 0)), pl.BlockSpec((1, ws), lambda i: (0, 0))],
        out_specs=pl.BlockSpec((tm, wf + ws), lambda i: (i, 0)),
        out_shape=jax.ShapeDtypeStruct((L, wf + ws), BF16),
        compiler_params=_cparams(("parallel",)),
    )(o_fox, o_swa, g_fox, g_swa)


def _out_norm_bwd(name, cfg, dcat, o_fox, o_swa, g_fox, g_swa):
    L, wf, ws = cfg.L, cfg.wf, cfg.ws
    tm = _tile(L, 256, 16)

    def body(d_ref, of_ref, os_ref, gf_ref, gs_ref, dof_ref, dos_ref, dgf_ref, dgs_ref):
        i = pl.program_id(0)

        @pl.when(i == 0)
        def _():
            dgf_ref[...] = jnp.zeros_like(dgf_ref)
            dgs_ref[...] = jnp.zeros_like(dgs_ref)

        for src, g_ref, dst, dg_ref, lo_, w in ((of_ref, gf_ref, dof_ref, dgf_ref, 0, wf),
                                                (os_ref, gs_ref, dos_ref, dgs_ref, wf, ws)):
            x = src[...]
            dy = d_ref[:, lo_:lo_ + w]
            r = lax.rsqrt(jnp.mean(x * x, axis=1, keepdims=True) + EPS)
            wv = dy * g_ref[...]
            proj = jnp.sum(wv * x, axis=1, keepdims=True) * (1.0 / w)
            dst[...] = r * wv - x * (r * r * r * proj)
            dg_ref[...] += jnp.sum(dy * x * r, axis=0, keepdims=True)

    return pl.pallas_call(
        body, name=name, grid=(L // tm,),
        in_specs=[pl.BlockSpec((tm, wf + ws), lambda i: (i, 0)), pl.BlockSpec((tm, wf), lambda i: (i, 0)),
                  pl.BlockSpec((tm, ws), lambda i: (i, 0)),
                  pl.BlockSpec((1, wf), lambda i: (0, 0)), pl.BlockSpec((1, ws), lambda i: (0, 0))],
        out_specs=[pl.BlockSpec((tm, wf), lambda i: (i, 0)), pl.BlockSpec((tm, ws), lambda i: (i, 0)),
                   pl.BlockSpec((1, wf), lambda i: (0, 0)), pl.BlockSpec((1, ws), lambda i: (0, 0))],
        out_shape=[jax.ShapeDtypeStruct((L, wf), F32), jax.ShapeDtypeStruct((L, ws), F32),
                   jax.ShapeDtypeStruct((1, wf), F32), jax.ShapeDtypeStruct((1, ws), F32)],
        compiler_params=_cparams(("arbitrary",)),
    )(dcat, o_fox, o_swa, g_fox, g_swa)


def _loss_head(name, h, target):
    L, D = h.shape
    T = BLOCK

    def body(h_ref, t_ref, loss_ref, dh_ref, dh16_ref):
        i = pl.program_id(0)

        @pl.when(i == 0)
        def _():
            loss_ref[...] = jnp.zeros_like(loss_ref)
            dh_ref[...] = jnp.zeros_like(dh_ref)
            dh16_ref[...] = jnp.zeros_like(dh16_ref)

        @pl.when(i > 0)
        def _():
            err = h_ref[...] - t_ref[...]
            dh = err * (1.0 / D)
            dh_ref[...] = dh
            dh16_ref[...] = dh.astype(BF16)
            loss_ref[...] += jnp.sum(err * err) * (0.5 / D)

    return pl.pallas_call(
        body, name=name, grid=(L // T,),
        in_specs=[pl.BlockSpec((T, D), lambda i: (i, 0)), pl.BlockSpec((T, D), lambda i: (jnp.maximum(i - 1, 0), 0))],
        out_specs=[pl.BlockSpec((8, LANES), lambda i: (0, 0)), pl.BlockSpec((T, D), lambda i: (i, 0)),
                   pl.BlockSpec((T, D), lambda i: (i, 0))],
        out_shape=[jax.ShapeDtypeStruct((8, LANES), F32), jax.ShapeDtypeStruct((L, D), F32),
                   jax.ShapeDtypeStruct((L, D), BF16)],
        compiler_params=_cparams(("arbitrary",)),
    )(h, target)


def _pair_add(name, grad, landed, parts, layer, core):
    _, _, r, C = grad.shape

    def body(s_ref, g_ref, l_ref, p_ref, o_ref):
        o_ref[...] = (g_ref[...].astype(F32) + l_ref[...].astype(F32)).astype(o_ref.dtype)

    return pl.pallas_call(
        body, name=name,
        grid_spec=pltpu.PrefetchScalarGridSpec(
            num_scalar_prefetch=1, grid=(4,),
            in_specs=[pl.BlockSpec((None, None, r, C), lambda k, s: (k, s[0], 0, 0)),
                      pl.BlockSpec((None, r, C), lambda k, s: (k, 0, 0)),
                      pl.BlockSpec(memory_space=pl.ANY)],
            out_specs=pl.BlockSpec((None, None, r, C), lambda k, s: (layer, k, 0, 0))),
        out_shape=jax.ShapeDtypeStruct(parts.shape, parts.dtype),
        input_output_aliases={3: 0},
        compiler_params=_cparams(("arbitrary",)),
    )(core, grad, landed, parts)


def _chip_sum(name, part, landed, chip, layer_axis):
    A, _, r, C = part.shape

    def body(s_ref, p_ref, l0_ref, l1_ref, l2_ref, o_ref):
        o_ref[...] = ((p_ref[...].astype(F32) + l0_ref[...].astype(F32))
                      + (l1_ref[...].astype(F32) + l2_ref[...].astype(F32)))

    def land(k):
        return pl.BlockSpec((None, None, r, C), lambda a, s, k=k: (k, a, 0, 0))

    if layer_axis == 0:
        out_spec, out_shape = pl.BlockSpec((None, r, C), lambda a, s: (a, 0, 0)), (A, r, C)
    else:
        out_spec, out_shape = pl.BlockSpec((r, C), lambda a, s: (0, a)), (r, A * C)
    out = pl.pallas_call(
        body, name=name,
        grid_spec=pltpu.PrefetchScalarGridSpec(
            num_scalar_prefetch=1, grid=(A,),
            in_specs=[pl.BlockSpec((None, None, r, C), lambda a, s: (a, s[0], 0, 0)), land(0), land(1), land(2)],
            out_specs=out_spec),
        out_shape=jax.ShapeDtypeStruct(out_shape, F32),
        compiler_params=_cparams(("parallel",)),
    )(chip, part, landed, landed, landed)
    return out if layer_axis == 0 else out.reshape(r, A, C)


def _gather_sum(name, v):
    R = v.shape[0]

    def body(x_ref, o_ref, buf_ref, send_sems, recv_sems):
        x, y, c = _place()
        me, sibling = (x, y, c), (x, y, 1 - c)
        chips = [(1 - x, y), (x, 1 - y), (1 - x, 1 - y)]

        def rows(dev):
            px, py, pc = dev
            return buf_ref.at[4 * px + 2 * py + pc]

        def copy(k, block, to, src=None):
            return pltpu.make_async_remote_copy(
                src_ref=rows(block) if src is None else src, dst_ref=rows(block),
                send_sem=send_sems.at[k], recv_sem=recv_sems.at[k], device_id=to, device_id_type=MESH)

        first = [copy(0, me, sibling, src=x_ref)]
        first += [copy(1 + j, me, (*chip, c), src=x_ref) for j, chip in enumerate(chips)]
        for cp in first:
            cp.start()
        rows(me)[...] = x_ref[...]
        passed = [copy(4 + j, (*chip, c), sibling) for j, chip in enumerate(chips)]
        for j, chip in enumerate(chips):
            copy(1 + j, (*chip, c), me).wait_recv()
            passed[j].start()
        copy(0, sibling, me).wait_recv()
        for j, chip in enumerate(chips):
            copy(4 + j, (*chip, 1 - c), me).wait_recv()
        for cp in first + passed:
            cp.wait_send()
        acc = buf_ref[0]
        for d in range(1, N_DEV):
            acc = acc + buf_ref[d]
        o_ref[...] = acc

    vm = pl.BlockSpec(memory_space=pltpu.VMEM)
    return pl.pallas_call(
        body, name=name, in_specs=[vm], out_specs=vm,
        out_shape=jax.ShapeDtypeStruct((R, LANES), F32),
        scratch_shapes=[pltpu.VMEM((N_DEV, R, LANES), F32), pltpu.SemaphoreType.DMA((7,)), pltpu.SemaphoreType.DMA((7,))],
    )(v)


def _adamw(name, g, w, m, v):
    A, R, C = g.shape
    budget = 1 << 18
    tr = _tile(R, max(8, budget // C // 8 * 8), 8)
    ta = _tile(A, max(1, budget // (tr * C)), 1)

    def body(g_ref, w_ref, m_ref, v_ref, d_ref, nm_ref, nv_ref):
        gv = g_ref[...]
        nm = ADAM_B1 * m_ref[...] + (1.0 - ADAM_B1) * gv
        nv = ADAM_B2 * v_ref[...] + (1.0 - ADAM_B2) * (gv * gv)
        m_hat = nm / (1.0 - ADAM_B1 ** ADAM_STEP)
        v_hat = nv / (1.0 - ADAM_B2 ** ADAM_STEP)
        d_ref[...] = -ADAM_LR * (m_hat / (jnp.sqrt(v_hat) + ADAM_EPS) + ADAM_WD * w_ref[...])
        nm_ref[...] = nm
        nv_ref[...] = nv

    blk = pl.BlockSpec((ta, tr, C), lambda i, j: (i, j, 0))
    shp = jax.ShapeDtypeStruct((A, R, C), F32)
    return pl.pallas_call(
        body, name=name, grid=(A // ta, R // tr),
        in_specs=[blk] * 4, out_specs=[blk] * 3, out_shape=[shp] * 3,
        compiler_params=_cparams(("parallel", "parallel")),
    )(g, w, m, v)


def _sum_adamw(name, part, landed, chip, w, m, v):
    A, _, r, C = part.shape
    tr = _tile(r, 176, 8)

    def body(s_ref, p_ref, l0_ref, l1_ref, l2_ref, w_ref, m_ref, v_ref, g_ref, d_ref, nm_ref, nv_ref):
        gv = ((p_ref[...].astype(F32) + l0_ref[...].astype(F32)) + (l1_ref[...].astype(F32) + l2_ref[...].astype(F32)))
        nm = ADAM_B1 * m_ref[...] + (1.0 - ADAM_B1) * gv
        nv = ADAM_B2 * v_ref[...] + (1.0 - ADAM_B2) * (gv * gv)
        m_hat = nm / (1.0 - ADAM_B1 ** ADAM_STEP)
        v_hat = nv / (1.0 - ADAM_B2 ** ADAM_STEP)
        g_ref[...] = gv
        d_ref[...] = -ADAM_LR * (m_hat / (jnp.sqrt(v_hat) + ADAM_EPS) + ADAM_WD * w_ref[...])
        nm_ref[...] = nm
        nv_ref[...] = nv

    def land(k):
        return pl.BlockSpec((None, None, tr, C), lambda a, t, s, k=k: (k, a, t, 0))

    blk = pl.BlockSpec((None, tr, C), lambda a, t, s: (a, t, 0))
    shp = jax.ShapeDtypeStruct((A, r, C), F32)
    return pl.pallas_call(
        body, name=name,
        grid_spec=pltpu.PrefetchScalarGridSpec(
            num_scalar_prefetch=1, grid=(A, r // tr),
            in_specs=[pl.BlockSpec((None, None, tr, C), lambda a, t, s: (a, s[0], t, 0)), land(0), land(1), land(2),
                      blk, blk, blk],
            out_specs=[blk] * 4),
        out_shape=[shp] * 4,
        compiler_params=_cparams(("parallel", "parallel")),
    )(chip, part, landed, landed, landed, w, m, v)


def _adamw_nd(name, g, w, m, v):
    shape = w.shape
    three = (1,) + shape if len(shape) == 2 else shape
    return tuple(o.reshape(shape) for o in _adamw(name, *[a.reshape(three) for a in (g, w, m, v)]))


def _scatter_heads(cfg, vals):
    v = jnp.pad(vals.reshape(cfg.hf // 2, 2), ((0, 0), (0, cfg.cs - 2)))
    return v.reshape(1, LANES)


def _gather_heads(cfg, row):
    return row.reshape(cfg.hf // 2, cfg.cs)[:, :2].reshape(cfg.hf)


def _permute_w_in(cfg, w_in_t):
    wf, hf = cfg.wf, cfg.hf
    o = 3 * wf
    cols = w_in_t.shape[1]
    fz = w_in_t[o:o + hf].reshape(hf // 2, 2, cols)
    fz_blk = jnp.pad(fz, ((0, 0), (0, cfg.cs - 2), (0, 0))).reshape(LANES, cols)
    return jnp.concatenate([w_in_t[:o], w_in_t[o + hf:], fz_blk], axis=0)


def _unpermute_dw_in(cfg, dwp):
    wf, hf = cfg.wf, cfg.hf
    o = 3 * wf
    cols = dwp.shape[1]
    fz = dwp[cfg.o_fz:].reshape(hf // 2, cfg.cs, cols)[:, :2].reshape(hf, cols)
    return jnp.concatenate([dwp[:o], fz, dwp[o:cfg.o_fz]], axis=0)


def _pair_gain(g):
    return jnp.tile(g, 2)[None]


def _fold_pair(dg):
    return dg[0, :HEAD_DIM] + dg[0, HEAD_DIM:]


def kernel(x, meta_tokens, ffn1_norm, ffn1_w_gate, ffn1_w_up, ffn1_w_down, mix_norm, w_in, b_forget, fox_q_norm, fox_k_norm, swa_q_norm, swa_k_norm, swa_sinks, fox_out_norm, swa_out_norm, w_out, ffn2_norm, ffn2_w_gate, ffn2_w_up, ffn2_w_down, loss_target, m_meta_tokens, m_ffn1_norm, m_ffn1_w_gate, m_ffn1_w_up, m_ffn1_w_down, m_mix_norm, m_w_in, m_b_forget, m_fox_q_norm, m_fox_k_norm, m_swa_q_norm, m_swa_k_norm, m_swa_sinks, m_fox_out_norm, m_swa_out_norm, m_w_out, m_ffn2_norm, m_ffn2_w_gate, m_ffn2_w_up, m_ffn2_w_down, v_meta_tokens, v_ffn1_norm, v_ffn1_w_gate, v_ffn1_w_up, v_ffn1_w_down, v_mix_norm, v_w_in, v_b_forget, v_fox_q_norm, v_fox_k_norm, v_swa_q_norm, v_swa_k_norm, v_swa_sinks, v_fox_out_norm, v_swa_out_norm, v_w_out, v_ffn2_norm, v_ffn2_w_gate, v_ffn2_w_up, v_ffn2_w_down):
    weights = dict(meta_tokens=meta_tokens, ffn1_norm=ffn1_norm, ffn1_w_gate=ffn1_w_gate, ffn1_w_up=ffn1_w_up,
                   ffn1_w_down=ffn1_w_down, mix_norm=mix_norm, w_in=w_in, b_forget=b_forget, fox_q_norm=fox_q_norm,
                   fox_k_norm=fox_k_norm, swa_q_norm=swa_q_norm, swa_k_norm=swa_k_norm, swa_sinks=swa_sinks,
                   fox_out_norm=fox_out_norm, swa_out_norm=swa_out_norm, w_out=w_out, ffn2_norm=ffn2_norm,
                   ffn2_w_gate=ffn2_w_gate, ffn2_w_up=ffn2_w_up, ffn2_w_down=ffn2_w_down)
    mom_m = dict(meta_tokens=m_meta_tokens, ffn1_norm=m_ffn1_norm, ffn1_w_gate=m_ffn1_w_gate, ffn1_w_up=m_ffn1_w_up,
                 ffn1_w_down=m_ffn1_w_down, mix_norm=m_mix_norm, w_in=m_w_in, b_forget=m_b_forget,
                 fox_q_norm=m_fox_q_norm, fox_k_norm=m_fox_k_norm, swa_q_norm=m_swa_q_norm, swa_k_norm=m_swa_k_norm,
                 swa_sinks=m_swa_sinks, fox_out_norm=m_fox_out_norm, swa_out_norm=m_swa_out_norm, w_out=m_w_out,
                 ffn2_norm=m_ffn2_norm, ffn2_w_gate=m_ffn2_w_gate, ffn2_w_up=m_ffn2_w_up, ffn2_w_down=m_ffn2_w_down)
    mom_v = dict(meta_tokens=v_meta_tokens, ffn1_norm=v_ffn1_norm, ffn1_w_gate=v_ffn1_w_gate, ffn1_w_up=v_ffn1_w_up,
                 ffn1_w_down=v_ffn1_w_down, mix_norm=v_mix_norm, w_in=v_w_in, b_forget=v_b_forget,
                 fox_q_norm=v_fox_q_norm, fox_k_norm=v_fox_k_norm, swa_q_norm=v_swa_q_norm, swa_k_norm=v_swa_k_norm,
                 swa_sinks=v_swa_sinks, fox_out_norm=v_fox_out_norm, swa_out_norm=v_swa_out_norm, w_out=v_w_out,
                 ffn2_norm=v_ffn2_norm, ffn2_w_gate=v_ffn2_w_gate, ffn2_w_up=v_ffn2_w_up, ffn2_w_down=v_ffn2_w_down)
    names = list(weights)

    _, S, D = x.shape
    depth = ffn1_norm.shape[0]
    hf, hs = b_forget.shape[1], swa_sinks.shape[1]
    U = w_in.shape[2] * N_DEV
    hkv = (U - 3 * HEAD_DIM * hf - hf - HEAD_DIM * hs) // (2 * HEAD_DIM)
    cfg = _Cfg(S, D, hf, hs, hkv)
    n_meta = meta_tokens.shape[0]
    assert n_meta == cfg.n_meta
    x_idx, y_idx, c_idx = _place()
    chip_idx = 2 * x_idx + y_idx
    dev_idx = 2 * chip_idx + c_idx
    core_s = jnp.reshape(c_idx, (1,)).astype(jnp.int32)
    chip_s = jnp.reshape(chip_idx, (1,)).astype(jnp.int32)

    col_sharded = ("ffn1_w_gate", "ffn1_w_up", "w_in", "ffn2_w_gate", "ffn2_w_up")
    use_order = ("ffn1_w_gate", "ffn1_w_up", "ffn1_w_down", "w_in", "w_out", "ffn2_w_gate", "ffn2_w_up", "ffn2_w_down")

    layer_axis = {k: (1 if k == "w_in" else 0) for k in use_order}

    def rows_view(k, a):
        if k not in col_sharded:
            return a
        return jnp.transpose(a, (2, 0, 1)) if k == "w_in" else jnp.swapaxes(a, 1, 2)

    def from_rows_view(k, a):
        if k not in col_sharded:
            return a
        return jnp.transpose(a, (1, 2, 0)) if k == "w_in" else jnp.swapaxes(a, 1, 2)

    w_rows = {k: rows_view(k, weights[k]) for k in use_order}

    def shard(key):
        k, l = key
        if k == "meta_tokens":
            return meta_tokens
        return _shard_bf16("weight_shard", w_rows[k], l, layer_axis[k] if k in col_sharded else 0)

    waiting = [("meta_tokens", 0)] + [(k, l) for l in range(depth) for k in use_order]
    after_first, after_second = [], []
    gathered = {}

    def fwd_carry(n_first):
        cy = _Carry()
        third = [(key, _gather_third(cy, buf)) for key, buf in after_second]
        second = [(key, _gather_second(cy, buf)) for key, buf in after_first]
        first = [(key, _gather_first(cy, shard(key))) for key in waiting[:n_first]]
        del waiting[:n_first]
        after_first.clear()
        after_second.clear()
        return cy, (first, second, third)

    def fwd_absorb(extra, plan):
        first, second, third = plan
        for key, idx in third:
            gathered[key] = extra[idx]
        for key, idx in second:
            after_second.append((key, extra[idx]))
        for key, idx in first:
            after_first.append((key, extra[idx]))

    def weight(k, l):
        key = (k, l)
        while key not in gathered:
            n = waiting.index(key) + 1 if key in waiting else 0
            cy, plan = fwd_carry(n)
            fwd_absorb(_comm_only("weights_gather", cy), plan)
        g = gathered[key]
        return g.reshape(-1, g.shape[-1])

    weight("w_in", 0)
    meta_full = jnp.swapaxes(weight("meta_tokens", 0).reshape(N_DEV, n_meta, -1), 0, 1).reshape(n_meta, D)
    slopes = jnp.asarray(2.0 ** (-8.0 * np.arange(1, hs + 1) / hs), dtype=F32)

    h = jnp.concatenate([jnp.zeros((cfg.pad, D), F32), meta_full, x[0]], axis=0)
    saved = []
    w_in_p = [None] * depth

    def mm_f(name, a, b, n_first=1, **kw):
        cy, plan = fwd_carry(n_first)
        out, extra = _matmul(name, a, b, carry=cy, **kw)
        fwd_absorb(extra, plan)
        return out

    def ffn_fwd(tag, l, h_in, norm, wg, wu, wd):
        xn = _rmsnorm_fwd(f"{tag}_norm", h_in, norm[l][None])
        wg_t, wu_t = weight(wg, l), weight(wu, l)
        cy, plan = fwd_carry(1)
        (gate, up, act), extra = _ffn_up(f"{tag}_up", xn, wg_t, wu_t, carry=cy)
        fwd_absorb(extra, plan)
        h_out = mm_f(f"{tag}_down", act, weight(wd, l), scale=0.5, residual=h_in, tm=1088, tn=512, tk=2816)
        return h_out, (xn, gate, up)

    for l in range(depth):
        st = {"h0": h}
        h, st["ffn1"] = ffn_fwd("ffn1", l, h, ffn1_norm, "ffn1_w_gate", "ffn1_w_up", "ffn1_w_down")
        st["h1"] = h
        xn = _rmsnorm_fwd("mix_norm", h, mix_norm[l][None])
        w_in_p[l] = _permute_w_in(cfg, weight("w_in", l))
        u = mm_f("mix_in", xn, w_in_p[l], trans_b=True, tm=1088, tn=640, tk=2048)
        gq, gk = _pair_gain(fox_q_norm[l]), _pair_gain(fox_k_norm[l])
        gsq, gsk = _pair_gain(swa_q_norm[l]), _pair_gain(swa_k_norm[l])
        bias = _scatter_heads(cfg, b_forget[l])
        qn, kn, fv, sqn, skd, svd, c, ct = _mix_prep("mix_prep", cfg, u, gq, gk, gsq, gsk, bias)
        cy, plan = fwd_carry(2)
        (o_fox, lse), extra = _fox_fwd("fox_fwd", cfg, qn, kn, fv, c, ct, carry=cy)
        fwd_absorb(extra, plan)
        cy, plan = fwd_carry(1)
        (o_swa,), extra = _swa_fwd("swa_fwd", cfg, sqn, skd, svd, swa_sinks[l], slopes, carry=cy)
        fwd_absorb(extra, plan)
        o_cat = _out_norm("out_norm", cfg, o_fox, o_swa, fox_out_norm[l][None], swa_out_norm[l][None])
        st["mix"] = (xn, u, gq, gk, gsq, gsk, bias, qn, kn, fv, sqn, skd, svd, c, ct, o_fox, lse, o_swa, o_cat)
        h = mm_f("mix_out", o_cat, weight("w_out", l), n_first=0, residual=h, tm=544, tn=1024, tk=2048)
        st["h2"] = h
        h, st["ffn2"] = ffn_fwd("ffn2", l, h, ffn2_norm, "ffn2_w_gate", "ffn2_w_up", "ffn2_w_down")
        saved.append(st)

    loss_blk, dh, dh16 = _loss_head("loss_head", h, loss_target[0])

    small = {k: [None] * depth for k in names if k not in use_order and k != "meta_tokens"}
    parts, landing = {}, {}
    to_sibling, to_chips = [], []

    def bwd_carry(n_chips):
        cy = _Carry()
        t1, t3 = [], []
        for k, l in list(to_chips):
            if len(t3) < n_chips and all(k != k3 for k3, _ in t3):
                to_chips.remove((k, l))
                t3.append((k, _scatter_chips(cy, parts[k], landing[k], l)))
        while to_sibling:
            k, l, g = to_sibling.pop(0)
            t1.append((k, l, g, _scatter_sibling(cy, g)))
        return cy, (t1, t3)

    def bwd_absorb(extra, plan):
        t1, t3 = plan
        for k, idx in t3:
            landing[k] = extra[idx]
        for k, l, g, idx in t1:
            parts[k] = _pair_add("grads_pair_add", g, extra[idx], parts[k], l, core_s)
            to_chips.append((k, l))

    def emit_grad(k, l, dw):
        r, C = dw.shape[0] // N_DEV, dw.shape[1]
        if k not in parts:
            parts[k] = lax.empty((depth, 4, r, C), BF16)
            landing[k] = lax.empty((3, depth, r, C), BF16)
        to_sibling.append((k, l, dw.reshape(4, 2, r, C)))

    def mm_b(name, a, b, n_chips=0, **kw):
        cy, plan = bwd_carry(n_chips)
        out, extra = _matmul(name, a, b, carry=cy, **kw)
        bwd_absorb(extra, plan)
        return out

    def ffn_bwd(tag, l, dh_out, dh_out16, h_in, st_, norm, wg, wu, wd):
        xn, gate, up = st_
        short = 1 if l == 0 else 0
        cy, plan = bwd_carry(1)
        (dgate, dup, act), extra = _ffn_bwd_act(f"{tag}_dact", dh_out16, weight(wd, l), gate, up, carry=cy)
        bwd_absorb(extra, plan)
        emit_grad(wd, l, mm_b(f"{tag}_dwd", act, dh_out16, short, trans_a=True, scale=0.5, out_dtype=BF16,
                              tm=1408, tn=1024, tk=2176))
        emit_grad(wg, l, mm_b(f"{tag}_dwg", dgate, xn, short, trans_a=True, out_dtype=BF16, tm=1408, tn=1024, tk=2176))
        emit_grad(wu, l, mm_b(f"{tag}_dwu", dup, xn, short, trans_a=True, out_dtype=BF16, tm=1408, tn=1024, tk=2176))
        dxn = mm_b(f"{tag}_dxn", dgate, weight(wg, l), 1, pair2=(dup, weight(wu, l)), tm=1088, tn=1024, tk=1408)
        dh_in, dh_in16, dg = _rmsnorm_bwd(f"{tag}_dnorm", dxn, h_in, norm[l][None], dh_out)
        return dh_in, dh_in16, dg[0]

    for l in reversed(range(depth)):
        st = saved[l]
        dh, dh16, small["ffn2_norm"][l] = ffn_bwd("ffn2", l, dh, dh16, st["h2"], st["ffn2"], ffn2_norm,
                                                   "ffn2_w_gate", "ffn2_w_up", "ffn2_w_down")
        xn, u, gq, gk, gsq, gsk, bias, qn, kn, fv, sqn, skd, svd, c, ct, o_fox, lse, o_swa, o_cat = st["mix"]
        dcat = mm_b("mix_dcat", dh16, weight("w_out", l), int(l == 0), trans_b=True, tm=544, tn=1024, tk=2048)
        emit_grad("w_out", l, mm_b("mix_dwout", o_cat, dh16, int(l == 0), trans_a=True, out_dtype=BF16, tm=512, tn=1024, tk=2176))
        do_fox, do_swa, dgf, dgs = _out_norm_bwd("out_norm_bwd", cfg, dcat, o_fox, o_swa,
                                                 fox_out_norm[l][None], swa_out_norm[l][None])
        small["fox_out_norm"][l], small["swa_out_norm"][l] = dgf[0], dgs[0]
        cy, plan = bwd_carry(3)
        (dqn, dkn, dfv, dct, dcq), extra = _fox_bwd("fox_bwd", cfg, qn, kn, fv, c, ct, o_fox, lse, do_fox, carry=cy)
        bwd_absorb(extra, plan)
        cy, plan = bwd_carry(1)
        (dsqn, dskd, dsvd, dsink), extra = _swa_bwd("swa_bwd", cfg, sqn, skd, svd, swa_sinks[l], slopes, o_swa, do_swa,
                                                    carry=cy)
        bwd_absorb(extra, plan)
        small["swa_sinks"][l] = dsink.reshape(hs // 2, 8, LANES)[:, 0, ::HEAD_DIM].reshape(hs)
        du, dgq, dgk, dgsq, dgsk, db = _mix_prep_bwd("mix_prep_bwd", cfg, u, gq, gk, gsq, gsk, bias,
                                                     dqn, dkn, dfv, dsqn, dskd, dsvd, dct, dcq)
        small["fox_q_norm"][l], small["fox_k_norm"][l] = _fold_pair(dgq), _fold_pair(dgk)
        small["swa_q_norm"][l], small["swa_k_norm"][l] = _fold_pair(dgsq), _fold_pair(dgsk)
        small["b_forget"][l] = _gather_heads(cfg, db[0])
        dwp = mm_b("mix_dwin", du, xn, int(l == 0), trans_a=True, out_dtype=BF16, tm=640, tn=1024, tk=2176)
        emit_grad("w_in", l, _unpermute_dw_in(cfg, dwp))
        dxn = mm_b("mix_dxn", du, w_in_p[l], int(l == 0), tm=544, tn=1024, tk=4480)
        dh, dh16, dg = _rmsnorm_bwd("mix_dnorm", dxn, st["h1"], mix_norm[l][None], dh)
        small["mix_norm"][l] = dg[0]
        dh, dh16, small["ffn1_norm"][l] = ffn_bwd("ffn1", l, dh, dh16, st["h0"], st["ffn1"], ffn1_norm,
                                                   "ffn1_w_gate", "ffn1_w_up", "ffn1_w_down")

    grad_x = dh[BLOCK:][None]
    dmeta = dh[cfg.pad:BLOCK]

    while to_sibling or to_chips:
        cy, plan = bwd_carry(len(use_order))
        bwd_absorb(_comm_only("grads_scatter", cy), plan)

    grads = {}
    fused = {}
    for k in use_order:
        if k == "w_in":
            grads[k] = _chip_sum("grads_chip_sum", parts[k], landing[k], chip_s, layer_axis[k])
        else:
            outs = _sum_adamw("grads_sum_adamw", parts[k], landing[k], chip_s, w_rows[k],
                              rows_view(k, mom_m[k]), rows_view(k, mom_v[k]))
            fused[k] = tuple(from_rows_view(k, o) for o in outs)

    small_names = list(small)
    pieces = [loss_blk[0, :1], dmeta.reshape(-1)] + [jnp.stack(small[k]).reshape(-1) for k in small_names]
    sizes = [int(p.shape[0]) for p in pieces]
    total = sum(sizes)
    padded = -(-total // (8 * LANES)) * (8 * LANES)
    vec = jnp.concatenate(pieces + [jnp.zeros((padded - total,), F32)]).reshape(-1, LANES)
    summed = _gather_sum("small_gather_sum", vec).reshape(-1)
    offs = np.cumsum([0] + sizes)
    loss = summed[0]
    dmeta_full = summed[offs[1]:offs[2]].reshape(n_meta, D)
    mcols = meta_tokens.shape[1]
    grads["meta_tokens"] = lax.dynamic_slice_in_dim(dmeta_full, dev_idx * mcols, mcols, axis=1)
    for n_, k in enumerate(small_names):
        grads[k] = summed[offs[2 + n_]:offs[3 + n_]].reshape(weights[k].shape)

    delta, new_m, new_v = {}, {}, {}
    for k in names:
        if k in fused:
            grads[k], delta[k], new_m[k], new_v[k] = fused[k]
        elif k in col_sharded:
            outs = _adamw_nd("adamw", grads[k], w_rows[k], rows_view(k, mom_m[k]), rows_view(k, mom_v[k]))
            delta[k], new_m[k], new_v[k] = (from_rows_view(k, o) for o in outs)
            grads[k] = from_rows_view(k, grads[k])
        else:
            delta[k], new_m[k], new_v[k] = _adamw_nd("adamw", grads[k], weights[k], mom_m[k], mom_v[k])

    return (loss, grad_x, *[grads[k] for k in names], *[delta[k] for k in names],
            *[new_m[k] for k in names], *[new_v[k] for k in names])
```
